```python
import math
import jax, jax.numpy as jnp
from jax import lax
import numpy as np

D_MODEL = 1024
BATCH = 8
SEQ = 16384
DEPTH = 4

D_MIX = D_MODEL
D_POOL = D_MIX // 2
D_SGU = D_MIX // 2
POOL_WINDOWS = (2, 4, 8, 16)
N_POOL_GROUPS = len(POOL_WINDOWS)
POOL_GROUP_DIM = D_POOL // N_POOL_GROUPS
CHUNK = 128
SGU_HEADS = 4
SGU_HEAD_DIM = D_SGU // SGU_HEADS
D_IN = D_POOL + 2 * D_SGU
D_FF = 2816
CONV_WIDTH = 3
N_MOD = 6
DEEPNORM_ALPHA = (2.0 * DEPTH) ** 0.25
DEEPNORM_BETA = (8.0 * DEPTH) ** -0.25
LN_EPS = 1e-5

kernel_name = "hybrid_pool_sgu_convffn_deepnorm_adaln"


def _layernorm(x, g, b):
    xf = x.astype(jnp.float32)
    mu = jnp.mean(xf, axis=-1, keepdims=True)
    var = jnp.mean(jnp.square(xf - mu), axis=-1, keepdims=True)
    y = (xf - mu) * lax.rsqrt(var + LN_EPS)
    return (y * g.astype(jnp.float32) + b.astype(jnp.float32)).astype(x.dtype)


def _modulate(x, shift, scale):
    return x * (1.0 + scale[:, None, :]) + shift[:, None, :]


def _pool_mixer(a, pool_w, pool_scale):
    B, S, _ = a.shape
    ag = a.reshape(B, S, N_POOL_GROUPS, POOL_GROUP_DIM).astype(jnp.float32)
    cs = jnp.cumsum(ag, axis=1)
    t = jnp.arange(S)
    pooled = []
    for g, w in enumerate(POOL_WINDOWS):
        csg = cs[:, :, g]
        prev = jnp.pad(csg, ((0, 0), (w, 0), (0, 0)))[:, :S]
        cnt = jnp.minimum(t + 1, w).astype(jnp.float32)[None, :, None]
        pooled.append((csg - prev) / cnt)
    pooled = (jnp.stack(pooled, axis=2) - ag).astype(a.dtype)
    mixed = jnp.einsum('bsgc,gcd->bsgd', pooled, pool_w)
    return mixed.reshape(B, S, D_POOL) * pool_scale


def _sgu_mixer(u, v, ln_g, ln_b, sgu_w, sgu_b):
    B, S, _ = v.shape
    u = jax.nn.gelu(u)
    v = _layernorm(jax.nn.gelu(v), ln_g, ln_b)
    vc = v.reshape(B, S // CHUNK, CHUNK, SGU_HEADS, SGU_HEAD_DIM)
    mask = jnp.tril(jnp.ones((CHUNK, CHUNK), dtype=bool))
    w = jnp.where(mask[None], sgu_w, jnp.zeros((), sgu_w.dtype))
    z = jnp.einsum('hts,bcshd->bcthd', w, vc) + jnp.transpose(sgu_b)[None, None, :, :, None]
    return u * z.reshape(B, S, D_SGU)


def _causal_dwconv(h, w, b):
    S = h.shape[1]
    hp = jnp.pad(h, ((0, 0), (CONV_WIDTH - 1, 0), (0, 0)))
    y = b
    for k in range(CONV_WIDTH):
        y = y + hp[:, k:k + S] * w[k]
    return y


def _fwd_setup_inputs(seed: int = 0) -> dict:
    key = jax.random.key(seed)
    ks = jax.random.split(key, 24)
    nrm = lambda k, shp: jax.random.normal(k, shp, dtype=jnp.float32)
    L, D = DEPTH, D_MODEL
    ada_b = jnp.concatenate([
        0.02 * nrm(ks[3], (L, 2 * D)),
        1.0 + 0.02 * nrm(ks[4], (L, D)),
        0.02 * nrm(ks[5], (L, 2 * D)),
        1.0 + 0.02 * nrm(ks[6], (L, D)),
    ], axis=-1)
    return {
        "x": nrm(ks[0], (BATCH, SEQ, D)),
        "c": nrm(ks[1], (BATCH, D)),
        "ada_w": 0.1 * D ** -0.5 * nrm(ks[2], (L, D, N_MOD * D)),
        "ada_b": ada_b,
        "w_in": D ** -0.5 * nrm(ks[7], (L, D, D_IN)),
        "pool_w": POOL_GROUP_DIM ** -0.5 * nrm(ks[8], (L, N_POOL_GROUPS, POOL_GROUP_DIM, POOL_GROUP_DIM)),
        "pool_scale": 1.0 + 0.02 * nrm(ks[9], (L, D_POOL)),
        "sgu_ln_g": 1.0 + 0.02 * nrm(ks[10], (L, D_SGU)),
        "sgu_ln_b": 0.02 * nrm(ks[11], (L, D_SGU)),
        "sgu_w": CHUNK ** -0.5 * nrm(ks[12], (L, SGU_HEADS, CHUNK, CHUNK)),
        "sgu_b": 1.0 + 0.02 * nrm(ks[13], (L, SGU_HEADS, CHUNK)),
        "w_out": DEEPNORM_BETA * D_MIX ** -0.5 * nrm(ks[14], (L, D_MIX, D)),
        "ln1_g": 1.0 + 0.02 * nrm(ks[15], (L, D)),
        "ln1_b": 0.02 * nrm(ks[16], (L, D)),
        "w_up": D ** -0.5 * nrm(ks[17], (L, D, 2 * D_FF)),
        "conv_w": 0.5 * nrm(ks[18], (L, CONV_WIDTH, D_FF)),
        "conv_b": 0.02 * nrm(ks[19], (L, D_FF)),
        "w_down": DEEPNORM_BETA * D_FF ** -0.5 * nrm(ks[20], (L, D_FF, D)),
        "ln2_g": 1.0 + 0.02 * nrm(ks[21], (L, D)),
        "ln2_b": 0.02 * nrm(ks[22], (L, D)),
    }


def _fwd_reference(x, c, ada_w, ada_b, w_in, pool_w, pool_scale, sgu_ln_g, sgu_ln_b, sgu_w, sgu_b,
              w_out, ln1_g, ln1_b, w_up, conv_w, conv_b, w_down, ln2_g, ln2_b):
    c_act = jax.nn.silu(c)
    for l in range(DEPTH):
        mod = c_act @ ada_w[l] + ada_b[l]
        shift1, scale1, gate1, shift2, scale2, gate2 = jnp.split(mod, N_MOD, axis=-1)

        h = _modulate(x, shift1, scale1)
        proj = jnp.einsum('bsd,de->bse', h, w_in[l])
        a = proj[..., :D_POOL]
        u = proj[..., D_POOL:D_POOL + D_SGU]
        v = proj[..., D_POOL + D_SGU:]
        y_a = _pool_mixer(a, pool_w[l], pool_scale[l])
        y_b = _sgu_mixer(u, v, sgu_ln_g[l], sgu_ln_b[l], sgu_w[l], sgu_b[l])
        mix = jnp.concatenate([y_a, y_b], axis=-1)
        f = jnp.einsum('bse,ed->bsd', mix, w_out[l])
        x = _layernorm(DEEPNORM_ALPHA * x + gate1[:, None, :] * f, ln1_g[l], ln1_b[l])

        h = _modulate(x, shift2, scale2)
        up = jnp.einsum('bsd,df->bsf', h, w_up[l])
        g, val = up[..., :D_FF], up[..., D_FF:]
        g = _causal_dwconv(g, conv_w[l], conv_b[l])
        f = jnp.einsum('bsf,fd->bsd', jax.nn.gelu(g) * val, w_down[l])
        x = _layernorm(DEEPNORM_ALPHA * x + gate2[:, None, :] * f, ln2_g[l], ln2_b[l])
    return x


import jax as _jax
import jax.numpy as _jnp

TWIN_FORMAT = 'train_step'
FWD_PARAMS = ['x', 'c', 'ada_w', 'ada_b', 'w_in', 'pool_w', 'pool_scale', 'sgu_ln_g', 'sgu_ln_b', 'sgu_w', 'sgu_b', 'w_out', 'ln1_g', 'ln1_b', 'w_up', 'conv_w', 'conv_b', 'w_down', 'ln2_g', 'ln2_b']
TWIN_WEIGHTS = ['ada_w', 'ada_b', 'w_in', 'pool_w', 'pool_scale', 'sgu_ln_g', 'sgu_ln_b', 'sgu_w', 'sgu_b', 'w_out', 'ln1_g', 'ln1_b', 'w_up', 'conv_w', 'conv_b', 'w_down', 'ln2_g', 'ln2_b']
TWIN_DIFF_INPUT = 'x'
TWIN_INPUTS = ['x', 'c', 'ada_w', 'ada_b', 'w_in', 'pool_w', 'pool_scale', 'sgu_ln_g', 'sgu_ln_b', 'sgu_w', 'sgu_b', 'w_out', 'ln1_g', 'ln1_b', 'w_up', 'conv_w', 'conv_b', 'w_down', 'ln2_g', 'ln2_b', 'loss_target', 'm_ada_w', 'm_ada_b', 'm_w_in', 'm_pool_w', 'm_pool_scale', 'm_sgu_ln_g', 'm_sgu_ln_b', 'm_sgu_w', 'm_sgu_b', 'm_w_out', 'm_ln1_g', 'm_ln1_b', 'm_w_up', 'm_conv_w', 'm_conv_b', 'm_w_down', 'm_ln2_g', 'm_ln2_b', 'v_ada_w', 'v_ada_b', 'v_w_in', 'v_pool_w', 'v_pool_scale', 'v_sgu_ln_g', 'v_sgu_ln_b', 'v_sgu_w', 'v_sgu_b', 'v_w_out', 'v_ln1_g', 'v_ln1_b', 'v_w_up', 'v_conv_w', 'v_conv_b', 'v_w_down', 'v_ln2_g', 'v_ln2_b']
TWIN_OUTPUTS = ['loss', 'grad_x', 'grad_ada_w', 'grad_ada_b', 'grad_w_in', 'grad_pool_w', 'grad_pool_scale', 'grad_sgu_ln_g', 'grad_sgu_ln_b', 'grad_sgu_w', 'grad_sgu_b', 'grad_w_out', 'grad_ln1_g', 'grad_ln1_b', 'grad_w_up', 'grad_conv_w', 'grad_conv_b', 'grad_w_down', 'grad_ln2_g', 'grad_ln2_b', 'delta_ada_w', 'delta_ada_b', 'delta_w_in', 'delta_pool_w', 'delta_pool_scale', 'delta_sgu_ln_g', 'delta_sgu_ln_b', 'delta_sgu_w', 'delta_sgu_b', 'delta_w_out', 'delta_ln1_g', 'delta_ln1_b', 'delta_w_up', 'delta_conv_w', 'delta_conv_b', 'delta_w_down', 'delta_ln2_g', 'delta_ln2_b', 'new_m_ada_w', 'new_m_ada_b', 'new_m_w_in', 'new_m_pool_w', 'new_m_pool_scale', 'new_m_sgu_ln_g', 'new_m_sgu_ln_b', 'new_m_sgu_w', 'new_m_sgu_b', 'new_m_w_out', 'new_m_ln1_g', 'new_m_ln1_b', 'new_m_w_up', 'new_m_conv_w', 'new_m_conv_b', 'new_m_w_down', 'new_m_ln2_g', 'new_m_ln2_b', 'new_v_ada_w', 'new_v_ada_b', 'new_v_w_in', 'new_v_pool_w', 'new_v_pool_scale', 'new_v_sgu_ln_g', 'new_v_sgu_ln_b', 'new_v_sgu_w', 'new_v_sgu_b', 'new_v_w_out', 'new_v_ln1_g', 'new_v_ln1_b', 'new_v_w_up', 'new_v_conv_w', 'new_v_conv_b', 'new_v_w_down', 'new_v_ln2_g', 'new_v_ln2_b']
TWIN_LEAF_KINDS = {'loss': 'loss', 'grad_x': 'grad_x', 'grad_ada_w': 'grad_w', 'grad_ada_b': 'grad_w', 'grad_w_in': 'grad_w', 'grad_pool_w': 'grad_w', 'grad_pool_scale': 'grad_w', 'grad_sgu_ln_g': 'grad_w', 'grad_sgu_ln_b': 'grad_w', 'grad_sgu_w': 'grad_w', 'grad_sgu_b': 'grad_w', 'grad_w_out': 'grad_w', 'grad_ln1_g': 'grad_w', 'grad_ln1_b': 'grad_w', 'grad_w_up': 'grad_w', 'grad_conv_w': 'grad_w', 'grad_conv_b': 'grad_w', 'grad_w_down': 'grad_w', 'grad_ln2_g': 'grad_w', 'grad_ln2_b': 'grad_w', 'delta_ada_w': 'delta_w', 'delta_ada_b': 'delta_w', 'delta_w_in': 'delta_w', 'delta_pool_w': 'delta_w', 'delta_pool_scale': 'delta_w', 'delta_sgu_ln_g': 'delta_w', 'delta_sgu_ln_b': 'delta_w', 'delta_sgu_w': 'delta_w', 'delta_sgu_b': 'delta_w', 'delta_w_out': 'delta_w', 'delta_ln1_g': 'delta_w', 'delta_ln1_b': 'delta_w', 'delta_w_up': 'delta_w', 'delta_conv_w': 'delta_w', 'delta_conv_b': 'delta_w', 'delta_w_down': 'delta_w', 'delta_ln2_g': 'delta_w', 'delta_ln2_b': 'delta_w', 'new_m_ada_w': 'new_m', 'new_m_ada_b': 'new_m', 'new_m_w_in': 'new_m', 'new_m_pool_w': 'new_m', 'new_m_pool_scale': 'new_m', 'new_m_sgu_ln_g': 'new_m', 'new_m_sgu_ln_b': 'new_m', 'new_m_sgu_w': 'new_m', 'new_m_sgu_b': 'new_m', 'new_m_w_out': 'new_m', 'new_m_ln1_g': 'new_m', 'new_m_ln1_b': 'new_m', 'new_m_w_up': 'new_m', 'new_m_conv_w': 'new_m', 'new_m_conv_b': 'new_m', 'new_m_w_down': 'new_m', 'new_m_ln2_g': 'new_m', 'new_m_ln2_b': 'new_m', 'new_v_ada_w': 'new_v', 'new_v_ada_b': 'new_v', 'new_v_w_in': 'new_v', 'new_v_pool_w': 'new_v', 'new_v_pool_scale': 'new_v', 'new_v_sgu_ln_g': 'new_v', 'new_v_sgu_ln_b': 'new_v', 'new_v_sgu_w': 'new_v', 'new_v_sgu_b': 'new_v', 'new_v_w_out': 'new_v', 'new_v_ln1_g': 'new_v', 'new_v_ln1_b': 'new_v', 'new_v_w_up': 'new_v', 'new_v_conv_w': 'new_v', 'new_v_conv_b': 'new_v', 'new_v_w_down': 'new_v', 'new_v_ln2_g': 'new_v', 'new_v_ln2_b': 'new_v'}


def _forward(args):
    return _fwd_reference(*[args[k] for k in FWD_PARAMS])


def _output_shape():
    def fwd():
        inp = _fwd_setup_inputs(0)
        return _fwd_reference(*[inp[k] for k in FWD_PARAMS])
    out = _jax.eval_shape(fwd)
    return out.shape, out.dtype

N_MICROBATCH = 1
ADAM_LR = 0.001
ADAM_B1 = 0.9
ADAM_B2 = 0.999
ADAM_EPS = 1e-08
ADAM_WD = 0.01
ADAM_STEP = 10
PER_EXAMPLE_BATCH_AXIS = {'x': 0, 'c': 0, 'loss_target': 0}
SHARED_INPUTS = []
_WEIGHT_DTYPES = {'ada_w': _jnp.float32, 'ada_b': _jnp.float32, 'w_in': _jnp.float32, 'pool_w': _jnp.float32, 'pool_scale': _jnp.float32, 'sgu_ln_g': _jnp.float32, 'sgu_ln_b': _jnp.float32, 'sgu_w': _jnp.float32, 'sgu_b': _jnp.float32, 'w_out': _jnp.float32, 'ln1_g': _jnp.float32, 'ln1_b': _jnp.float32, 'w_up': _jnp.float32, 'conv_w': _jnp.float32, 'conv_b': _jnp.float32, 'w_down': _jnp.float32, 'ln2_g': _jnp.float32, 'ln2_b': _jnp.float32}
MOMENT_SCALE = {'ada_w': 4.588933e-02, 'ada_b': 9.525845e-02, 'w_in': 6.948864e-02, 'pool_w': 7.960709e-02, 'pool_scale': 8.769357e-02, 'sgu_ln_g': 4.125197e-02, 'sgu_ln_b': 4.220570e-02, 'sgu_w': 4.222407e-02, 'sgu_b': 6.032977e-02, 'w_out': 2.190289e-01, 'ln1_g': 3.774713e+00, 'ln1_b': 1.651807e+00, 'w_up': 3.130574e-02, 'conv_w': 3.692376e-02, 'conv_b': 3.577298e-02, 'w_down': 1.218686e-01, 'ln2_g': 6.438749e+01, 'ln2_b': 6.618291e+00}


def _to_microbatches(a, axis):
    t = _jnp.moveaxis(a, axis, 0)
    t = t.reshape((N_MICROBATCH, t.shape[0] // N_MICROBATCH) + t.shape[1:])
    return _jnp.moveaxis(t, 1, axis + 1)


def setup_inputs(seed: int = 0) -> dict:
    inp = _fwd_setup_inputs(seed)
    key = _jax.random.fold_in(_jax.random.key(seed), 7919)
    shape, _ = _output_shape()
    out = dict(inp)
    out["loss_target"] = _jax.random.normal(_jax.random.fold_in(key, 0), shape, _jnp.float32)
    for i, name in enumerate(TWIN_WEIGHTS):
        w = inp[name].astype(_jnp.float32)
        if MOMENT_SCALE is None:
            s = _jnp.sqrt(_jnp.mean(_jnp.square(w)) + 1e-30)
        else:
            s = MOMENT_SCALE[name]
        km, kv = _jax.random.split(_jax.random.fold_in(key, i + 1))
        out[name] = w
        out["m_" + name] = s * _jax.random.normal(km, w.shape, _jnp.float32)
        out["v_" + name] = (s * s) * _jax.random.uniform(kv, w.shape, _jnp.float32, 0.5, 1.5)
    if N_MICROBATCH > 1:
        for name, axis in PER_EXAMPLE_BATCH_AXIS.items():
            out[name] = _to_microbatches(out[name], axis)
    return {'x': out['x'], 'c': out['c'], 'ada_w': out['ada_w'], 'ada_b': out['ada_b'], 'w_in': out['w_in'], 'pool_w': out['pool_w'], 'pool_scale': out['pool_scale'], 'sgu_ln_g': out['sgu_ln_g'], 'sgu_ln_b': out['sgu_ln_b'], 'sgu_w': out['sgu_w'], 'sgu_b': out['sgu_b'], 'w_out': out['w_out'], 'ln1_g': out['ln1_g'], 'ln1_b': out['ln1_b'], 'w_up': out['w_up'], 'conv_w': out['conv_w'], 'conv_b': out['conv_b'], 'w_down': out['w_down'], 'ln2_g': out['ln2_g'], 'ln2_b': out['ln2_b'], 'loss_target': out['loss_target'], 'm_ada_w': out['m_ada_w'], 'm_ada_b': out['m_ada_b'], 'm_w_in': out['m_w_in'], 'm_pool_w': out['m_pool_w'], 'm_pool_scale': out['m_pool_scale'], 'm_sgu_ln_g': out['m_sgu_ln_g'], 'm_sgu_ln_b': out['m_sgu_ln_b'], 'm_sgu_w': out['m_sgu_w'], 'm_sgu_b': out['m_sgu_b'], 'm_w_out': out['m_w_out'], 'm_ln1_g': out['m_ln1_g'], 'm_ln1_b': out['m_ln1_b'], 'm_w_up': out['m_w_up'], 'm_conv_w': out['m_conv_w'], 'm_conv_b': out['m_conv_b'], 'm_w_down': out['m_w_down'], 'm_ln2_g': out['m_ln2_g'], 'm_ln2_b': out['m_ln2_b'], 'v_ada_w': out['v_ada_w'], 'v_ada_b': out['v_ada_b'], 'v_w_in': out['v_w_in'], 'v_pool_w': out['v_pool_w'], 'v_pool_scale': out['v_pool_scale'], 'v_sgu_ln_g': out['v_sgu_ln_g'], 'v_sgu_ln_b': out['v_sgu_ln_b'], 'v_sgu_w': out['v_sgu_w'], 'v_sgu_b': out['v_sgu_b'], 'v_w_out': out['v_w_out'], 'v_ln1_g': out['v_ln1_g'], 'v_ln1_b': out['v_ln1_b'], 'v_w_up': out['v_w_up'], 'v_conv_w': out['v_conv_w'], 'v_conv_b': out['v_conv_b'], 'v_w_down': out['v_w_down'], 'v_ln2_g': out['v_ln2_g'], 'v_ln2_b': out['v_ln2_b']}


def _loss(weights, diff, rest, loss_target):
    with _jax.named_scope("forward"):
        args = {**rest, TWIN_DIFF_INPUT: diff, **{k: w.astype(_WEIGHT_DTYPES[k]) for k, w in weights.items()}}
        y = _forward(args)
    with _jax.named_scope("loss_head"):
        err = _jnp.square(y.astype(_jnp.float32) - loss_target)
        return 0.5 * _jnp.sum(_jnp.mean(err, axis=-1)) if err.ndim else 0.5 * err


def _adamw(w, g, m, v):
    m = ADAM_B1 * m + (1.0 - ADAM_B1) * g
    v = ADAM_B2 * v + (1.0 - ADAM_B2) * _jnp.square(g)
    m_hat = m / (1.0 - ADAM_B1 ** ADAM_STEP)
    v_hat = v / (1.0 - ADAM_B2 ** ADAM_STEP)
    delta = -ADAM_LR * (m_hat / (_jnp.sqrt(v_hat) + ADAM_EPS) + ADAM_WD * w)
    return delta, m, v


def reference(x, c, ada_w, ada_b, w_in, pool_w, pool_scale, sgu_ln_g, sgu_ln_b, sgu_w, sgu_b, w_out, ln1_g, ln1_b, w_up, conv_w, conv_b, w_down, ln2_g, ln2_b, loss_target, m_ada_w, m_ada_b, m_w_in, m_pool_w, m_pool_scale, m_sgu_ln_g, m_sgu_ln_b, m_sgu_w, m_sgu_b, m_w_out, m_ln1_g, m_ln1_b, m_w_up, m_conv_w, m_conv_b, m_w_down, m_ln2_g, m_ln2_b, v_ada_w, v_ada_b, v_w_in, v_pool_w, v_pool_scale, v_sgu_ln_g, v_sgu_ln_b, v_sgu_w, v_sgu_b, v_w_out, v_ln1_g, v_ln1_b, v_w_up, v_conv_w, v_conv_b, v_w_down, v_ln2_g, v_ln2_b):
    given = dict(x=x, c=c, ada_w=ada_w, ada_b=ada_b, w_in=w_in, pool_w=pool_w, pool_scale=pool_scale, sgu_ln_g=sgu_ln_g, sgu_ln_b=sgu_ln_b, sgu_w=sgu_w, sgu_b=sgu_b, w_out=w_out, ln1_g=ln1_g, ln1_b=ln1_b, w_up=w_up, conv_w=conv_w, conv_b=conv_b, w_down=w_down, ln2_g=ln2_g, ln2_b=ln2_b, loss_target=loss_target, m_ada_w=m_ada_w, m_ada_b=m_ada_b, m_w_in=m_w_in, m_pool_w=m_pool_w, m_pool_scale=m_pool_scale, m_sgu_ln_g=m_sgu_ln_g, m_sgu_ln_b=m_sgu_ln_b, m_sgu_w=m_sgu_w, m_sgu_b=m_sgu_b, m_w_out=m_w_out, m_ln1_g=m_ln1_g, m_ln1_b=m_ln1_b, m_w_up=m_w_up, m_conv_w=m_conv_w, m_conv_b=m_conv_b, m_w_down=m_w_down, m_ln2_g=m_ln2_g, m_ln2_b=m_ln2_b, v_ada_w=v_ada_w, v_ada_b=v_ada_b, v_w_in=v_w_in, v_pool_w=v_pool_w, v_pool_scale=v_pool_scale, v_sgu_ln_g=v_sgu_ln_g, v_sgu_ln_b=v_sgu_ln_b, v_sgu_w=v_sgu_w, v_sgu_b=v_sgu_b, v_w_out=v_w_out, v_ln1_g=v_ln1_g, v_ln1_b=v_ln1_b, v_w_up=v_w_up, v_conv_w=v_conv_w, v_conv_b=v_conv_b, v_w_down=v_w_down, v_ln2_g=v_ln2_g, v_ln2_b=v_ln2_b)
    weights = {n: given[n] for n in TWIN_WEIGHTS}
    shared = {n: given[n] for n in SHARED_INPUTS}
    per_example = {n: given[n] for n in ['x', 'c']}
    grad_fn = _jax.value_and_grad(_loss, argnums=(0, 1))

    def one_microbatch(ex, loss_target):
        ex = dict(ex)
        diff = ex.pop(TWIN_DIFF_INPUT)
        return grad_fn(weights, diff, {**shared, **ex}, loss_target)

    if N_MICROBATCH == 1:
        loss, (grad_w, grad_x) = one_microbatch(per_example, given["loss_target"])
    else:
        def body(carry, xs):
            loss_sum, grad_sum = carry
            l_k, (gw_k, gx_k) = one_microbatch(xs[0], xs[1])
            with _jax.named_scope("update"):
                return (loss_sum + l_k, _jax.tree.map(_jnp.add, grad_sum, gw_k)), gx_k

        init = (_jnp.zeros((), _jnp.float32), _jax.tree.map(_jnp.zeros_like, weights))
        (loss, grad_w), grad_x = _jax.lax.scan(body, init, (per_example, given["loss_target"]))
    with _jax.named_scope("update"):
        delta_w, new_m, new_v = {}, {}, {}
        for n in TWIN_WEIGHTS:
            delta_w[n], new_m[n], new_v[n] = _adamw(weights[n], grad_w[n], given["m_" + n], given["v_" + n])
    return (loss, grad_x, *[grad_w[n] for n in TWIN_WEIGHTS], *[delta_w[n] for n in TWIN_WEIGHTS],
            *[new_m[n] for n in TWIN_WEIGHTS], *[new_v[n] for n in TWIN_WEIGHTS])
```

```python
import functools
import math

import jax
import jax.numpy as jnp
from jax import lax
from jax.experimental import pallas as pl
from jax.experimental.pallas import tpu as pltpu

F32 = jnp.float32
MXU_DTYPE = jnp.bfloat16
WIRE_DTYPE = jnp.bfloat16

DEPTH = 4
D = 1024
D_POOL = 512
D_SGU = 512
N_GROUPS = 4
GROUP = 128
POOL_WINDOWS = (2, 4, 8, 16)
POOL_HALO = 16
CHUNK = 128
HEADS = 4
D_IN = D_POOL + 2 * D_SGU
D_FF = 2816
CONV_HALO = 8
N_CHIPS = 4
N_DEV = 8
ALPHA = (2.0 * DEPTH) ** 0.25
LN_EPS = 1e-5
ADAM_LR, ADAM_B1, ADAM_B2, ADAM_EPS, ADAM_WD, ADAM_STEP = 0.001, 0.9, 0.999, 1e-08, 0.01, 10

TS_PROJ = 512
TS_MIX = 256
TS_FF = 256
TS_TN = 1024
VMEM_LIMIT = 52 * 2 ** 20

MESH = pl.DeviceIdType.MESH
ANY = pl.BlockSpec(memory_space=pl.ANY)

_GELU_K0 = math.sqrt(2.0 / math.pi)
_GELU_K1 = 0.044715


def _params(n_axes):
    return pltpu.CompilerParams(dimension_semantics=("arbitrary",) * n_axes, vmem_limit_bytes=VMEM_LIMIT)


def _dot(a, b):
    return jnp.dot(a, b, preferred_element_type=F32)


def _dot_nt(a, b):
    return lax.dot_general(a, b, (((1,), (1,)), ((), ())), preferred_element_type=F32)


def _dot_tn(a, b):
    return lax.dot_general(a, b, (((0,), (0,)), ((), ())), preferred_element_type=F32)


def _gelu(x):
    x2 = x * x
    t = jnp.tanh(_GELU_K0 * (x + _GELU_K1 * (x2 * x)))
    cdf = 0.5 * (1.0 + t)
    dg = cdf + x * (0.5 * (1.0 - t * t)) * (_GELU_K0 * (1.0 + 3.0 * _GELU_K1 * x2))
    return x * cdf, dg


def _colsum(x):
    return jnp.sum(x, axis=0, keepdims=True)


def _row(d):
    return pl.BlockSpec((1, d), lambda *_: (0, 0))


def _full(shape):
    n = len(shape)
    return pl.BlockSpec(shape, lambda *_: (0,) * n)


def _layer_block(shape, layer):
    n = len(shape)
    return pl.BlockSpec((None,) + tuple(shape), lambda *_: (layer,) + (0,) * n)


def _pool_counts(first_row, rows, window):
    t = first_row + lax.broadcasted_iota(jnp.int32, (rows, 1), 0)
    return 1.0 / jnp.minimum(t + 1, window).astype(F32)


def _pool_forward(a_ext, a, first_row, rows):
    out = []
    for g, window in enumerate(POOL_WINDOWS):
        s = a_ext[:, g * GROUP:(g + 1) * GROUP]
        k = 1
        while k < window:
            s = s + pltpu.roll(s, k, 0)
            k *= 2
        inv = _pool_counts(first_row, rows, window)
        out.append(s[POOL_HALO:] * inv - a[:, g * GROUP:(g + 1) * GROUP])
    return out


def _sgu_norm(v):
    gv, dgv = _gelu(v)
    mu = jnp.mean(gv, axis=-1, keepdims=True)
    xc = gv - mu
    var = jnp.mean(xc * xc, axis=-1, keepdims=True)
    rs = lax.rsqrt(var + LN_EPS)
    return xc * rs, rs, dgv


def _f1_in_proj(xn, lg, lb, shift, scale, w_all, layer):
    S = xn.shape[0]
    ts = min(TS_PROJ, S)

    def body(xn_ref, lg_ref, lb_ref, sh_ref, sc_ref, w_ref, h_ref, p_ref):
        x = xn_ref[...] * lg_ref[...] + lb_ref[...]
        h = (x * (1.0 + sc_ref[...]) + sh_ref[...]).astype(MXU_DTYPE)
        h_ref[...] = h
        p_ref[...] = _dot(h, w_ref[...])

    return pl.pallas_call(
        body, name=f"f1_in_proj_l{layer}", grid=(S // ts,),
        in_specs=[pl.BlockSpec((ts, D), lambda i: (i, 0)), _row(D), _row(D), _row(D), _row(D),
                  _layer_block((D, D_IN), layer)],
        out_specs=[pl.BlockSpec((ts, D), lambda i: (i, 0)), pl.BlockSpec((ts, D_IN), lambda i: (i, 0))],
        out_shape=[jax.ShapeDtypeStruct((S, D), MXU_DTYPE), jax.ShapeDtypeStruct((S, D_IN), F32)],
        compiler_params=_params(1),
    )(xn, lg, lb, shift, scale, w_all)


def _f2_mixers(proj, xn, lg, lb, gate, pool_w, pool_scale, sg, sb, wm, bias_full, w_out_all, layer):
    S = xn.shape[0]
    ts = min(TS_MIX, S)
    n_chunks = ts // CHUNK
    halo_blocks = ts // POOL_HALO

    def body(p_ref, halo_ref, xn_ref, lg_ref, lb_ref, gate_ref, pw_ref, ps_ref, sg_ref, sb_ref, wm_ref,
             bias_ref, wo_ref, mix_ref, f_ref, xh_ref, rs_ref, z_scr):
        i = pl.program_id(0)
        a = p_ref[:, 0:D_POOL]
        u = p_ref[:, D_POOL:D_POOL + D_SGU]
        v = p_ref[:, D_POOL + D_SGU:D_IN]
        halo = halo_ref[...] * (i > 0).astype(F32)
        a_ext = jnp.concatenate([halo, a], axis=0)
        pooled = _pool_forward(a_ext, a, i * ts, ts)
        for g in range(N_GROUPS):
            mixed = _dot(pooled[g].astype(MXU_DTYPE), pw_ref[g])
            mix_ref[:, g * GROUP:(g + 1) * GROUP] = (mixed * ps_ref[:, g * GROUP:(g + 1) * GROUP]).astype(MXU_DTYPE)
        gu, _ = _gelu(u)
        vhat, _, _ = _sgu_norm(v)
        vn = (vhat * sg_ref[...] + sb_ref[...]).astype(MXU_DTYPE)
        for c in range(n_chunks):
            for h in range(HEADS):
                blk = vn[c * CHUNK:(c + 1) * CHUNK, h * GROUP:(h + 1) * GROUP]
                z_scr[c * CHUNK:(c + 1) * CHUNK, h * GROUP:(h + 1) * GROUP] = (
                    _dot(wm_ref[h], blk) + bias_ref[:, h * GROUP:(h + 1) * GROUP])
        mix_ref[:, D_POOL:D] = (gu * z_scr[...]).astype(MXU_DTYPE)
        f = _dot(mix_ref[...], wo_ref[...])
        f_ref[...] = f
        x = xn_ref[...] * lg_ref[...] + lb_ref[...]
        z1 = ALPHA * x + gate_ref[...] * f
        mu = jnp.mean(z1, axis=-1, keepdims=True)
        zc = z1 - mu
        var = jnp.mean(zc * zc, axis=-1, keepdims=True)
        rs = lax.rsqrt(var + LN_EPS)
        xh_ref[...] = zc * rs
        rs_ref[...] = rs

    tile = lambda w: pl.BlockSpec((ts, w), lambda i: (i, 0))
    return pl.pallas_call(
        body, name=f"f2_mixers_l{layer}", grid=(S // ts,),
        in_specs=[tile(D_IN),
                  pl.BlockSpec((POOL_HALO, D_POOL), lambda i: (jnp.maximum(i * halo_blocks - 1, 0), 0)),
                  tile(D), _row(D), _row(D), _row(D),
                  _full((N_GROUPS, GROUP, GROUP)), _row(D_POOL), _row(D_SGU), _row(D_SGU),
                  _full((HEADS, CHUNK, CHUNK)), _full((CHUNK, D_SGU)), _layer_block((D, D), layer)],
        out_specs=[tile(D), tile(D), tile(D), tile(1)],
        out_shape=[jax.ShapeDtypeStruct((S, D), MXU_DTYPE), jax.ShapeDtypeStruct((S, D), F32),
                   jax.ShapeDtypeStruct((S, D), F32), jax.ShapeDtypeStruct((S, 1), F32)],
        scratch_shapes=[pltpu.VMEM((ts, D_SGU), F32)],
        compiler_params=_params(1),
    )(proj, proj, xn, lg, lb, gate, pool_w, pool_scale, sg, sb, wm, bias_full, w_out_all)


def _f3_up_proj(xhat, lg, lb, shift, scale, w_up_all, layer):
    S = xhat.shape[0]
    ts = min(TS_PROJ, S)

    def body(xh_ref, lg_ref, lb_ref, sh_ref, sc_ref, w_ref, h_ref, up_ref):
        x = xh_ref[...] * lg_ref[...] + lb_ref[...]
        h = (x * (1.0 + sc_ref[...]) + sh_ref[...]).astype(MXU_DTYPE)
        h_ref[...] = h
        up_ref[...] = _dot(h, w_ref[...])

    return pl.pallas_call(
        body, name=f"f3_up_proj_l{layer}", grid=(S // ts, 2),
        in_specs=[pl.BlockSpec((ts, D), lambda i, j: (i, 0)), _row(D), _row(D), _row(D), _row(D),
                  pl.BlockSpec((None, D, D_FF), lambda i, j: (layer, 0, j))],
        out_specs=[pl.BlockSpec((ts, D), lambda i, j: (i, 0)), pl.BlockSpec((ts, D_FF), lambda i, j: (i, j))],
        out_shape=[jax.ShapeDtypeStruct((S, D), MXU_DTYPE), jax.ShapeDtypeStruct((S, 2 * D_FF), F32)],
        compiler_params=_params(2),
    )(xhat, lg, lb, shift, scale, w_up_all)


def _conv_forward(g_ext, cw_ref, cb_ref):
    gm2 = pltpu.roll(g_ext, 2, 0)[CONV_HALO:]
    gm1 = pltpu.roll(g_ext, 1, 0)[CONV_HALO:]
    g0 = g_ext[CONV_HALO:]
    gc = ((cb_ref[...] + gm2 * cw_ref[0:1, :]) + gm1 * cw_ref[1:2, :]) + g0 * cw_ref[2:3, :]
    return gc, gm2, gm1


def _f4_ffn(up, xhat1, lg, lb, gate, conv_w, conv_b, w_down_all, layer):
    S = xhat1.shape[0]
    ts = min(TS_FF, S)
    halo_blocks = ts // CONV_HALO

    def body(g_ref, halo_ref, val_ref, xh_ref, lg_ref, lb_ref, gate_ref, cw_ref, cb_ref, wd_ref,
             act_ref, f_ref, xo_ref, rs_ref):
        i = pl.program_id(0)
        halo = halo_ref[...] * (i > 0).astype(F32)
        g_ext = jnp.concatenate([halo, g_ref[...]], axis=0)
        gc, _, _ = _conv_forward(g_ext, cw_ref, cb_ref)
        ge, _ = _gelu(gc)
        act = (ge * val_ref[...]).astype(MXU_DTYPE)
        act_ref[...] = act
        f = _dot(act, wd_ref[...])
        f_ref[...] = f
        x = xh_ref[...] * lg_ref[...] + lb_ref[...]
        z = ALPHA * x + gate_ref[...] * f
        mu = jnp.mean(z, axis=-1, keepdims=True)
        zc = z - mu
        var = jnp.mean(zc * zc, axis=-1, keepdims=True)
        rs = lax.rsqrt(var + LN_EPS)
        xo_ref[...] = zc * rs
        rs_ref[...] = rs

    tile = lambda w: pl.BlockSpec((ts, w), lambda i: (i, 0))
    return pl.pallas_call(
        body, name=f"f4_ffn_l{layer}", grid=(S // ts,),
        in_specs=[pl.BlockSpec((ts, D_FF), lambda i: (i, 0)),
                  pl.BlockSpec((CONV_HALO, D_FF), lambda i: (jnp.maximum(i * halo_blocks - 1, 0), 0)),
                  pl.BlockSpec((ts, D_FF), lambda i: (i, 1)),
                  tile(D), _row(D), _row(D), _row(D), _full((3, D_FF)), _row(D_FF),
                  _layer_block((D_FF, D), layer)],
        out_specs=[tile(D_FF), tile(D), tile(D), tile(1)],
        out_shape=[jax.ShapeDtypeStruct((S, D_FF), MXU_DTYPE), jax.ShapeDtypeStruct((S, D), F32),
                   jax.ShapeDtypeStruct((S, D), F32), jax.ShapeDtypeStruct((S, 1), F32)],
        compiler_params=_params(1),
    )(up, up, up, xhat1, lg, lb, gate, conv_w, conv_b, w_down_all)


def _loss_head(xhat, lg, lb, target):
    S = xhat.shape[0]
    ts = min(TS_PROJ, S)

    def body(xh_ref, lg_ref, lb_ref, t_ref, loss_ref, dy_ref, acc):
        i = pl.program_id(0)

        @pl.when(i == 0)
        def _():
            acc[...] = jnp.zeros_like(acc)

        err = (xh_ref[...] * lg_ref[...] + lb_ref[...]) - t_ref[...]
        dy_ref[...] = err * (1.0 / D)
        acc[...] += _colsum(err * err)

        @pl.when(i == pl.num_programs(0) - 1)
        def _():
            loss_ref[...] = jnp.sum(acc[...], axis=1, keepdims=True) * (0.5 / D)

    tile = pl.BlockSpec((ts, D), lambda i: (i, 0))
    return pl.pallas_call(
        body, name="loss_head", grid=(S // ts,),
        in_specs=[tile, _row(D), _row(D), tile],
        out_specs=[_full((1, 1)), tile],
        out_shape=[jax.ShapeDtypeStruct((1, 1), F32), jax.ShapeDtypeStruct((S, D), F32)],
        scratch_shapes=[pltpu.VMEM((1, D), F32)],
        compiler_params=_params(1),
    )(xhat, lg, lb, target)


def _accumulate(i, ref, value):
    @pl.when(i == 0)
    def _():
        ref[...] = value

    @pl.when(i > 0)
    def _():
        ref[...] += value


def _b_ln(dxo, xhat, rstd, f, lg, gate, name):
    S = dxo.shape[0]
    ts = min(TS_PROJ, S)

    def body(d_ref, xh_ref, rs_ref, f_ref, lg_ref, gate_ref, df_ref, dres_ref, dlg_ref, dlb_ref, dgate_ref):
        i = pl.program_id(0)
        dxo_t = d_ref[...]
        xh = xh_ref[...]
        dxh = dxo_t * lg_ref[...]
        m1 = jnp.mean(dxh, axis=-1, keepdims=True)
        m2 = jnp.mean(dxh * xh, axis=-1, keepdims=True)
        dz = rs_ref[...] * (dxh - m1 - xh * m2)
        df_ref[...] = (dz * gate_ref[...]).astype(MXU_DTYPE)
        dres_ref[...] = ALPHA * dz
        _accumulate(i, dlg_ref, _colsum(dxo_t * xh))
        _accumulate(i, dlb_ref, _colsum(dxo_t))
        _accumulate(i, dgate_ref, _colsum(dz * f_ref[...]))

    tile = lambda w: pl.BlockSpec((ts, w), lambda i: (i, 0))
    return pl.pallas_call(
        body, name=name, grid=(S // ts,),
        in_specs=[tile(D), tile(D), tile(1), tile(D), _row(D), _row(D)],
        out_specs=[tile(D), tile(D), _row(D), _row(D), _row(D)],
        out_shape=[jax.ShapeDtypeStruct((S, D), MXU_DTYPE), jax.ShapeDtypeStruct((S, D), F32)]
        + [jax.ShapeDtypeStruct((1, D), F32)] * 3,
        compiler_params=_params(1),
    )(dxo, xhat, rstd, f, lg, gate)


def _b2_ffn(df2, w_down_all, up, conv_w, conv_b, layer):
    S = df2.shape[0]
    ts = min(TS_FF, S)
    halo_blocks = ts // CONV_HALO

    def body(df_ref, wd_ref, g_ref, halo_ref, val_ref, cw_ref, cb_ref, dval_ref, dgc_ref, dcw_ref, dcb_ref):
        i = pl.program_id(0)
        dact = _dot_nt(df_ref[...], wd_ref[...])
        halo = halo_ref[...] * (i > 0).astype(F32)
        g_ext = jnp.concatenate([halo, g_ref[...]], axis=0)
        gc, gm2, gm1 = _conv_forward(g_ext, cw_ref, cb_ref)
        ge, dge = _gelu(gc)
        dval_ref[...] = (dact * ge).astype(MXU_DTYPE)
        dgc = dact * val_ref[...] * dge
        dgc_ref[...] = dgc
        dcw = jnp.concatenate([_colsum(dgc * gm2), _colsum(dgc * gm1), _colsum(dgc * g_ref[...])], axis=0)
        _accumulate(i, dcw_ref, dcw)
        _accumulate(i, dcb_ref, _colsum(dgc))

    tile = lambda w: pl.BlockSpec((ts, w), lambda i: (i, 0))
    return pl.pallas_call(
        body, name=f"b2_ffn_l{layer}", grid=(S // ts,),
        in_specs=[tile(D), _layer_block((D_FF, D), layer), tile(D_FF),
                  pl.BlockSpec((CONV_HALO, D_FF), lambda i: (jnp.maximum(i * halo_blocks - 1, 0), 0)),
                  pl.BlockSpec((ts, D_FF), lambda i: (i, 1)), _full((3, D_FF)), _row(D_FF)],
        out_specs=[tile(D_FF), tile(D_FF), _full((3, D_FF)), _row(D_FF)],
        out_shape=[jax.ShapeDtypeStruct((S, D_FF), MXU_DTYPE), jax.ShapeDtypeStruct((S, D_FF), F32),
                   jax.ShapeDtypeStruct((3, D_FF), F32), jax.ShapeDtypeStruct((1, D_FF), F32)],
        compiler_params=_params(1),
    )(df2, w_down_all, up, up, up, conv_w, conv_b)


def _b3_up(dgc, dval, w_up_all, dres, xhat1, lg, lb, scale, conv_w, layer):
    S = dres.shape[0]
    ts = min(TS_FF, S)
    halo_blocks = ts // CONV_HALO
    last_halo = S // CONV_HALO - 1

    def body(dgc_ref, halo_ref, dval_ref, w_ref, dres_ref, xh_ref, lg_ref, lb_ref, sc_ref, cw_ref,
             dg_ref, dxo_ref, dsc_ref, dsh_ref, acc):
        i = pl.program_id(0)
        k = pl.program_id(1)

        @pl.when(k == 0)
        def _():
            halo = halo_ref[...] * (i < pl.num_programs(0) - 1).astype(F32)
            ext = jnp.concatenate([dgc_ref[...], halo], axis=0)
            n = ts + CONV_HALO
            dp1 = pltpu.roll(ext, n - 1, 0)[:ts]
            dp2 = pltpu.roll(ext, n - 2, 0)[:ts]
            dg = (dgc_ref[...] * cw_ref[2:3, :] + dp1 * cw_ref[1:2, :] + dp2 * cw_ref[0:1, :]).astype(MXU_DTYPE)
            dg_ref[...] = dg
            acc[...] = _dot_nt(dg, w_ref[...])

        @pl.when(k == 1)
        def _():
            dh = acc[...] + _dot_nt(dval_ref[...], w_ref[...])
            x1 = xh_ref[...] * lg_ref[...] + lb_ref[...]
            dxo_ref[...] = dh * (1.0 + sc_ref[...]) + dres_ref[...]
            _accumulate(i, dsc_ref, _colsum(dh * x1))
            _accumulate(i, dsh_ref, _colsum(dh))

    tile = lambda w: pl.BlockSpec((ts, w), lambda i, k: (i, 0))
    row = lambda w: pl.BlockSpec((1, w), lambda i, k: (0, 0))
    return pl.pallas_call(
        body, name=f"b3_up_l{layer}", grid=(S // ts, 2),
        in_specs=[tile(D_FF),
                  pl.BlockSpec((CONV_HALO, D_FF), lambda i, k: (jnp.minimum((i + 1) * halo_blocks, last_halo), 0)),
                  tile(D_FF), pl.BlockSpec((None, D, D_FF), lambda i, k: (layer, 0, k)),
                  tile(D), tile(D), row(D), row(D), row(D), pl.BlockSpec((3, D_FF), lambda i, k: (0, 0))],
        out_specs=[tile(D_FF), tile(D), row(D), row(D)],
        out_shape=[jax.ShapeDtypeStruct((S, D_FF), MXU_DTYPE), jax.ShapeDtypeStruct((S, D), F32),
                   jax.ShapeDtypeStruct((1, D), F32), jax.ShapeDtypeStruct((1, D), F32)],
        scratch_shapes=[pltpu.VMEM((ts, D), F32)],
        compiler_params=_params(2),
    )(dgc, dgc, dval, w_up_all, dres, xhat1, lg, lb, scale, conv_w)


def _b5_mixers(df1, w_out_all, proj, pool_w, pool_scale, sg, sb, wm, bias_full, layer):
    S = df1.shape[0]
    ts = min(TS_MIX, S)
    n_chunks = ts // CHUNK
    halo_blocks = ts // POOL_HALO
    last_halo = S // POOL_HALO - 1

    def body(df_ref, dfh_ref, wo_ref, p_ref, ah_ref, pw_ref, ps_ref, sg_ref, sb_ref, wm_ref, bias_ref,
             dp_ref, dpw_ref, dps_ref, dsg_ref, dsb_ref, dwm_ref, dbias_ref, z_scr, dvn_scr):
        i = pl.program_id(0)
        last = pl.num_programs(0) - 1
        dmix = _dot_nt(df_ref[...], wo_ref[...])
        dmix_halo = _dot_nt(dfh_ref[...], wo_ref[0:D_POOL, :]) * (i < last).astype(F32)

        a = p_ref[:, 0:D_POOL]
        halo = ah_ref[...] * (i > 0).astype(F32)
        pooled = _pool_forward(jnp.concatenate([halo, a], axis=0), a, i * ts, ts)
        n = ts + POOL_HALO
        dps_parts = []
        for g, window in enumerate(POOL_WINDOWS):
            cols = slice(g * GROUP, (g + 1) * GROUP)
            pooled_b = pooled[g].astype(MXU_DTYPE)
            mixed = _dot(pooled_b, pw_ref[g])
            dya = dmix[:, cols]
            dps_parts.append(_colsum(dya * mixed))
            dmixed = (dya * ps_ref[:, cols]).astype(MXU_DTYPE)
            dpw_g = _dot_tn(pooled_b, dmixed)

            @pl.when(i == 0)
            def _():
                dpw_ref[g] = dpw_g

            @pl.when(i > 0)
            def _():
                dpw_ref[g] += dpw_g

            dpooled = _dot_nt(dmixed, pw_ref[g])
            dmixed_h = (dmix_halo[:, cols] * ps_ref[:, cols]).astype(MXU_DTYPE)
            dpooled_h = _dot_nt(dmixed_h, pw_ref[g])
            q = dpooled * _pool_counts(i * ts, ts, window)
            s = jnp.concatenate([q, dpooled_h * (1.0 / window)], axis=0)
            k = 1
            while k < window:
                s = s + pltpu.roll(s, n - k, 0)
                k *= 2
            dp_ref[:, cols] = (s[:ts] - dpooled).astype(MXU_DTYPE)
        _accumulate(i, dps_ref, jnp.concatenate(dps_parts, axis=1))

        u = p_ref[:, D_POOL:D_POOL + D_SGU]
        v = p_ref[:, D_POOL + D_SGU:D_IN]
        gu, dgu = _gelu(u)
        vhat, rs, dgv = _sgu_norm(v)
        vn = (vhat * sg_ref[...] + sb_ref[...]).astype(MXU_DTYPE)
        dyb = dmix[:, D_POOL:D]
        dz = dyb * gu
        dzb = dz.astype(MXU_DTYPE)
        dbias = dz[0:CHUNK]
        for c in range(1, n_chunks):
            dbias = dbias + dz[c * CHUNK:(c + 1) * CHUNK]
        _accumulate(i, dbias_ref, dbias)
        for h in range(HEADS):
            cols = slice(h * GROUP, (h + 1) * GROUP)
            dwm_h = None
            for c in range(n_chunks):
                rows = slice(c * CHUNK, (c + 1) * CHUNK)
                vn_blk = vn[rows, cols]
                dz_blk = dzb[rows, cols]
                z_scr[rows, cols] = _dot(wm_ref[h], vn_blk) + bias_ref[:, cols]
                dvn_scr[rows, cols] = _dot_tn(wm_ref[h], dz_blk)
                part = _dot_nt(dz_blk, vn_blk)
                dwm_h = part if dwm_h is None else dwm_h + part

            @pl.when(i == 0)
            def _():
                dwm_ref[h] = dwm_h

            @pl.when(i > 0)
            def _():
                dwm_ref[h] += dwm_h

        dp_ref[:, D_POOL:D_POOL + D_SGU] = (dyb * z_scr[...] * dgu).astype(MXU_DTYPE)
        dvn = dvn_scr[...]
        _accumulate(i, dsg_ref, _colsum(dvn * vhat))
        _accumulate(i, dsb_ref, _colsum(dvn))
        dvh = dvn * sg_ref[...]
        m1 = jnp.mean(dvh, axis=-1, keepdims=True)
        m2 = jnp.mean(dvh * vhat, axis=-1, keepdims=True)
        dp_ref[:, D_POOL + D_SGU:D_IN] = (rs * (dvh - m1 - vhat * m2) * dgv).astype(MXU_DTYPE)

        @pl.when(i == last)
        def _():
            tri = (lax.broadcasted_iota(jnp.int32, (CHUNK, CHUNK), 0)
                   >= lax.broadcasted_iota(jnp.int32, (CHUNK, CHUNK), 1))
            for h in range(HEADS):
                dwm_ref[h] = jnp.where(tri, dwm_ref[h], 0.0)

    tile = lambda w: pl.BlockSpec((ts, w), lambda i: (i, 0))
    return pl.pallas_call(
        body, name=f"b5_mixers_l{layer}", grid=(S // ts,),
        in_specs=[tile(D),
                  pl.BlockSpec((POOL_HALO, D), lambda i: (jnp.minimum((i + 1) * halo_blocks, last_halo), 0)),
                  _layer_block((D, D), layer), tile(D_IN),
                  pl.BlockSpec((POOL_HALO, D_POOL), lambda i: (jnp.maximum(i * halo_blocks - 1, 0), 0)),
                  _full((N_GROUPS, GROUP, GROUP)), _row(D_POOL), _row(D_SGU), _row(D_SGU),
                  _full((HEADS, CHUNK, CHUNK)), _full((CHUNK, D_SGU))],
        out_specs=[tile(D_IN), _full((N_GROUPS, GROUP, GROUP)), _row(D_POOL), _row(D_SGU), _row(D_SGU),
                   _full((HEADS, CHUNK, CHUNK)), _full((CHUNK, D_SGU))],
        out_shape=[jax.ShapeDtypeStruct((S, D_IN), MXU_DTYPE), jax.ShapeDtypeStruct((N_GROUPS, GROUP, GROUP), F32),
                   jax.ShapeDtypeStruct((1, D_POOL), F32), jax.ShapeDtypeStruct((1, D_SGU), F32),
                   jax.ShapeDtypeStruct((1, D_SGU), F32), jax.ShapeDtypeStruct((HEADS, CHUNK, CHUNK), F32),
                   jax.ShapeDtypeStruct((CHUNK, D_SGU), F32)],
        scratch_shapes=[pltpu.VMEM((ts, D_SGU), F32), pltpu.VMEM((ts, D_SGU), F32)],
        compiler_params=_params(1),
    )(df1, df1, w_out_all, proj, proj, pool_w, pool_scale, sg, sb, wm, bias_full)


def _b6_in(dproj, w_in_all, dres, xn, lg, lb, scale, layer):
    S = dres.shape[0]
    ts = min(TS_PROJ, S)

    def body(dp_ref, w_ref, dres_ref, xn_ref, lg_ref, lb_ref, sc_ref, dx_ref, dsc_ref, dsh_ref):
        i = pl.program_id(0)
        dh = _dot_nt(dp_ref[...], w_ref[...])
        x = xn_ref[...] * lg_ref[...] + lb_ref[...]
        dx_ref[...] = dh * (1.0 + sc_ref[...]) + dres_ref[...]
        _accumulate(i, dsc_ref, _colsum(dh * x))
        _accumulate(i, dsh_ref, _colsum(dh))

    tile = lambda w: pl.BlockSpec((ts, w), lambda i: (i, 0))
    return pl.pallas_call(
        body, name=f"b6_in_l{layer}", grid=(S // ts,),
        in_specs=[tile(D_IN), _layer_block((D, D_IN), layer), tile(D), tile(D), _row(D), _row(D), _row(D)],
        out_specs=[tile(D), _row(D), _row(D)],
        out_shape=[jax.ShapeDtypeStruct((S, D), F32), jax.ShapeDtypeStruct((1, D), F32),
                   jax.ShapeDtypeStruct((1, D), F32)],
        compiler_params=_params(1),
    )(dproj, w_in_all, dres, xn, lg, lb, scale)


def _weight_grad(a, b, stacked, layer, col0, tn, n_total, name):
    S, M = a.shape
    N = b.shape[1]
    ts = min(TS_TN, S)
    col_block0 = col0 // tn

    def body(*refs):
        a_ref, b_ref, o_ref = refs[0], refs[1], refs[-1]
        i = pl.program_id(1)
        _accumulate(i, o_ref, _dot_tn(a_ref[...], b_ref[...]))

    in_specs = [pl.BlockSpec((ts, M), lambda j, i: (i, 0)), pl.BlockSpec((ts, tn), lambda j, i: (i, j))]
    args = [a, b]
    aliases = {}
    if stacked is not None:
        in_specs.append(ANY)
        args.append(stacked)
        aliases = {2: 0}
    return pl.pallas_call(
        body, name=name, grid=(N // tn, S // ts),
        in_specs=in_specs,
        out_specs=pl.BlockSpec((None, M, tn), lambda j, i: (layer, 0, col_block0 + j)),
        out_shape=jax.ShapeDtypeStruct((DEPTH, M, n_total), F32),
        input_output_aliases=aliases,
        compiler_params=_params(2),
    )(*args)


def _silu(x):
    return x * (1.0 / (1.0 + jnp.exp(-x)))


def _ada_forward(c_all, ada_w, ada_b_cols):
    n_cols = ada_w.shape[2]
    tc = 512

    def body(c_ref, w_ref, b_ref, o_ref):
        ca = _silu(c_ref[...]).astype(MXU_DTYPE)
        o_ref[...] = _dot(ca, w_ref[...].astype(MXU_DTYPE)) + b_ref[...]

    return pl.pallas_call(
        body, name="ada_forward", grid=(DEPTH, n_cols // tc),
        in_specs=[pl.BlockSpec((16, D), lambda l, j: (0, 0)), pl.BlockSpec((None, D, tc), lambda l, j: (l, 0, j)),
                  pl.BlockSpec((None, 1, tc), lambda l, j: (l, 0, j))],
        out_specs=pl.BlockSpec((None, 16, tc), lambda l, j: (l, 0, j)),
        out_shape=jax.ShapeDtypeStruct((DEPTH, 16, n_cols), F32),
        compiler_params=_params(2),
    )(c_all, ada_w, ada_b_cols)


def _ada_backward(c_all, dmod_cols):
    n_cols = dmod_cols.shape[2]
    tc = 512

    def body(c_ref, d_ref, o_ref):
        ca = _silu(c_ref[...]).astype(MXU_DTYPE)
        o_ref[...] = _dot_tn(ca, d_ref[...].astype(MXU_DTYPE))

    return pl.pallas_call(
        body, name="ada_backward", grid=(DEPTH, n_cols // tc),
        in_specs=[pl.BlockSpec((16, D), lambda l, j: (0, 0)), pl.BlockSpec((None, 16, tc), lambda l, j: (l, 0, j))],
        out_specs=pl.BlockSpec((None, D, tc), lambda l, j: (l, 0, j)),
        out_shape=jax.ShapeDtypeStruct((DEPTH, D, n_cols), F32),
        compiler_params=_params(2),
    )(c_all, dmod_cols)


def _adamw(w, g, m, v, name):
    R, C = w.shape
    tr = R
    for cand in (512, 256, 128, 64, 32, 16, 8):
        if R % cand == 0 and cand * C * 4 <= 2 ** 21:
            tr = cand
            break
    c1 = 1.0 - ADAM_B1 ** ADAM_STEP
    c2 = 1.0 - ADAM_B2 ** ADAM_STEP

    def body(w_ref, g_ref, m_ref, v_ref, d_ref, mo_ref, vo_ref):
        gg = g_ref[...]
        mn = ADAM_B1 * m_ref[...] + (1.0 - ADAM_B1) * gg
        vn = ADAM_B2 * v_ref[...] + (1.0 - ADAM_B2) * (gg * gg)
        mo_ref[...] = mn
        vo_ref[...] = vn
        d_ref[...] = -ADAM_LR * ((mn / c1) / (jnp.sqrt(vn / c2) + ADAM_EPS) + ADAM_WD * w_ref[...])

    tile = pl.BlockSpec((tr, C), lambda i: (i, 0))
    return pl.pallas_call(
        body, name=name, grid=(R // tr,), in_specs=[tile] * 4, out_specs=[tile] * 3,
        out_shape=[jax.ShapeDtypeStruct((R, C), F32)] * 3, compiler_params=_params(1),
    )(w, g, m, v)


def _position():
    x, y, c = lax.axis_index("x"), lax.axis_index("y"), lax.axis_index("c")
    other_chips = [(1 - x, y), (x, 1 - y), (1 - x, 1 - y)]
    return x, y, c, other_chips


def _all_gather8(block, name):
    R, C = block.shape

    def body(x_ref, out_ref, send_sems, recv_sems, local_sem):
        x, y, c, chips = _position()
        me, sibling = (x, y, c), (x, y, 1 - c)

        def rows(px, py, pc):
            return out_ref.at[pl.ds((4 * px + 2 * py + pc) * R, R), :]

        def copy(k, blk, to, src=None):
            return pltpu.make_async_remote_copy(
                src_ref=rows(*blk) if src is None else src, dst_ref=rows(*blk),
                send_sem=send_sems.at[k], recv_sem=recv_sems.at[k], device_id=to, device_id_type=MESH)

        mine = pltpu.make_async_copy(x_ref, rows(*me), local_sem)
        mine.start()
        first = [copy(0, me, sibling, src=x_ref)]
        first += [copy(1 + j, me, (*chip, c), src=x_ref) for j, chip in enumerate(chips)]
        for cp in first:
            cp.start()
        passed = [copy(4 + j, (*chip, c), sibling) for j, chip in enumerate(chips)]
        for j, chip in enumerate(chips):
            copy(1 + j, (*chip, c), me).wait_recv()
            passed[j].start()
        copy(0, sibling, me).wait_recv()
        for j, chip in enumerate(chips):
            copy(4 + j, (*chip, 1 - c), me).wait_recv()
        for cp in first + passed:
            cp.wait_send()
        mine.wait()

    return pl.pallas_call(
        body, name=name, out_shape=jax.ShapeDtypeStruct((N_DEV * R, C), block.dtype),
        in_specs=[ANY], out_specs=ANY,
        scratch_shapes=[pltpu.SemaphoreType.DMA((7,)), pltpu.SemaphoreType.DMA((7,)), pltpu.SemaphoreType.DMA(())],
    )(block)


def _gather_flat(vec, name):
    n = vec.shape[0]
    padded = -(-n // 1024) * 1024
    block = jnp.pad(vec, (0, padded - n)).reshape(8, padded // 8)
    out = _all_gather8(block, name)
    return out.reshape(N_DEV, padded)[:, :n]


def _shard_window(ref, kind, chip, layers):
    n = ref.shape[2] // N_CHIPS if kind == "cols" else ref.shape[1] // N_CHIPS
    if kind == "cols":
        return ref.at[pl.ds(layers, 2), :, pl.ds(chip * n, n)]
    return ref.at[pl.ds(layers, 2), pl.ds(chip * n, n), :]


_SHARD_KINDS = ("cols", "rows", "cols", "rows")


def _gather_weights(shards):
    n_w = len(shards)
    full_shapes = []
    for w, kind in zip(shards, _SHARD_KINDS):
        L, r, cdim = w.shape
        full_shapes.append((L, r, cdim * N_CHIPS) if kind == "cols" else (L, r * N_CHIPS, cdim))

    def body(*refs):
        srcs, outs = refs[:n_w], refs[n_w:2 * n_w]
        send_sems, recv_sems, local_sems = refs[2 * n_w:]
        x, y, c, chips = _position()
        my_chip = 2 * x + y
        sibling = (x, y, 1 - c)
        mine, theirs = 2 * c, 2 * (1 - c)

        def copy(kind_idx, a, src, dst, to):
            k = kind_idx * n_w + a
            return pltpu.make_async_remote_copy(src_ref=src, dst_ref=dst, send_sem=send_sems.at[k],
                                                recv_sem=recv_sems.at[k], device_id=to, device_id_type=MESH)

        def win(a, chip, layers):
            return _shard_window(outs[a], _SHARD_KINDS[a], chip, layers)

        started, local = [], []
        for a in range(n_w):
            own = srcs[a].at[pl.ds(mine, 2)]
            lc = pltpu.make_async_copy(own, win(a, my_chip, mine), local_sems.at[a])
            lc.start()
            local.append(lc)
            started.append(copy(0, a, own, win(a, my_chip, mine), sibling))
            for j, chip in enumerate(chips):
                started.append(copy(1 + j, a, own, win(a, my_chip, mine), (*chip, c)))
        for cp in started:
            cp.start()
        passed = []
        for j, (cx, cy) in enumerate(chips):
            for a in range(n_w):
                w = win(a, 2 * cx + cy, mine)
                copy(1 + j, a, w, w, (x, y, c)).wait_recv()
                fwd = copy(4 + j, a, w, w, sibling)
                fwd.start()
                passed.append(fwd)
        for a in range(n_w):
            w = win(a, my_chip, theirs)
            copy(0, a, w, w, (x, y, c)).wait_recv()
            for j, (cx, cy) in enumerate(chips):
                w = win(a, 2 * cx + cy, theirs)
                copy(4 + j, a, w, w, (x, y, c)).wait_recv()
        for cp in started + passed:
            cp.wait_send()
        for lc in local:
            lc.wait()

    return pl.pallas_call(
        body, name="gather_weights",
        out_shape=[jax.ShapeDtypeStruct(s, w.dtype) for s, w in zip(full_shapes, shards)],
        in_specs=[ANY] * n_w, out_specs=[ANY] * n_w,
        scratch_shapes=[pltpu.SemaphoreType.DMA((7 * n_w,)), pltpu.SemaphoreType.DMA((7 * n_w,)),
                        pltpu.SemaphoreType.DMA((n_w,))],
    )(*shards)


def _swap_halves(grads):
    n_w = len(grads)

    def body(*refs):
        srcs, outs = refs[:n_w], refs[n_w:2 * n_w]
        send_sems, recv_sems = refs[2 * n_w:]
        x, y, c, _ = _position()
        copies = [pltpu.make_async_remote_copy(
            src_ref=srcs[a].at[pl.ds(2 * (1 - c), 2)], dst_ref=outs[a], send_sem=send_sems.at[a],
            recv_sem=recv_sems.at[a], device_id=(x, y, 1 - c), device_id_type=MESH) for a in range(n_w)]
        for cp in copies:
            cp.start()
        for cp in copies:
            cp.wait()

    return pl.pallas_call(
        body, name="swap_halves",
        out_shape=[jax.ShapeDtypeStruct((2,) + g.shape[1:], g.dtype) for g in grads],
        in_specs=[ANY] * n_w, out_specs=[ANY] * n_w,
        scratch_shapes=[pltpu.SemaphoreType.DMA((n_w,)), pltpu.SemaphoreType.DMA((n_w,))],
    )(*grads)


def _scatter_partials(partials):
    n_w = len(partials)
    out_shapes = []
    for p, kind in zip(partials, _SHARD_KINDS):
        _, r, cdim = p.shape
        out_shapes.append((3, 2, r, cdim // N_CHIPS) if kind == "cols" else (3, 2, r // N_CHIPS, cdim))

    def body(*refs):
        srcs, outs = refs[:n_w], refs[n_w:2 * n_w]
        send_sems, recv_sems = refs[2 * n_w:]
        x, y, c, chips = _position()
        copies = []
        for j, (cx, cy) in enumerate(chips):
            for a in range(n_w):
                k = j * n_w + a
                copies.append(pltpu.make_async_remote_copy(
                    src_ref=_shard_window(srcs[a], _SHARD_KINDS[a], 2 * cx + cy, 0), dst_ref=outs[a].at[j],
                    send_sem=send_sems.at[k], recv_sem=recv_sems.at[k], device_id=(cx, cy, c), device_id_type=MESH))
        for cp in copies:
            cp.start()
        for cp in copies:
            cp.wait()

    return pl.pallas_call(
        body, name="scatter_partials",
        out_shape=[jax.ShapeDtypeStruct(s, p.dtype) for s, p in zip(out_shapes, partials)],
        in_specs=[ANY] * n_w, out_specs=[ANY] * n_w,
        scratch_shapes=[pltpu.SemaphoreType.DMA((3 * n_w,)), pltpu.SemaphoreType.DMA((3 * n_w,))],
    )(*partials)


def _share_reduced(halves):
    n_w = len(halves)

    def body(*refs):
        srcs, outs = refs[:n_w], refs[n_w:2 * n_w]
        send_sems, recv_sems, local_sems = refs[2 * n_w:]
        x, y, c, _ = _position()
        local, copies = [], []
        for a in range(n_w):
            dst = outs[a].at[pl.ds(2 * c, 2)]
            lc = pltpu.make_async_copy(srcs[a], dst, local_sems.at[a])
            lc.start()
            local.append(lc)
            copies.append(pltpu.make_async_remote_copy(
                src_ref=srcs[a], dst_ref=dst, send_sem=send_sems.at[a], recv_sem=recv_sems.at[a],
                device_id=(x, y, 1 - c), device_id_type=MESH))
        for cp in copies:
            cp.start()
        for a in range(n_w):
            theirs = outs[a].at[pl.ds(2 * (1 - c), 2)]
            pltpu.make_async_remote_copy(src_ref=srcs[a], dst_ref=theirs, send_sem=send_sems.at[a],
                                         recv_sem=recv_sems.at[a], device_id=(x, y, c),
                                         device_id_type=MESH).wait_recv()
        for cp in copies:
            cp.wait_send()
        for lc in local:
            lc.wait()

    return pl.pallas_call(
        body, name="share_reduced",
        out_shape=[jax.ShapeDtypeStruct((DEPTH,) + h.shape[1:], h.dtype) for h in halves],
        in_specs=[ANY] * n_w, out_specs=[ANY] * n_w,
        scratch_shapes=[pltpu.SemaphoreType.DMA((n_w,)), pltpu.SemaphoreType.DMA((n_w,)),
                        pltpu.SemaphoreType.DMA((n_w,))],
    )(*halves)


def _chip_partial(pos, grad, theirs, name):
    _, M, N = grad.shape
    tm = 256

    def body(pos_ref, g_ref, t_ref, o_ref):
        o_ref[...] = (g_ref[...] + t_ref[...]).astype(WIRE_DTYPE)

    grid_spec = pltpu.PrefetchScalarGridSpec(
        num_scalar_prefetch=1, grid=(2, M // tm),
        in_specs=[pl.BlockSpec((None, tm, N), lambda k, i, pos: (2 * pos[0] + k, i, 0)),
                  pl.BlockSpec((None, tm, N), lambda k, i, pos: (k, i, 0))],
        out_specs=pl.BlockSpec((None, tm, N), lambda k, i, pos: (k, i, 0)))
    return pl.pallas_call(
        body, name=name, grid_spec=grid_spec, out_shape=jax.ShapeDtypeStruct((2, M, N), WIRE_DTYPE),
        compiler_params=_params(2),
    )(pos, grad, theirs)


def _reduce_shard(pos, grad, theirs, received, kind, name):
    _, M, N = grad.shape
    if kind == "cols":
        n_shard = N // N_CHIPS
        tm = min(M, 512)
        block = (None, tm, n_shard)
        grid = (2, M // tm)
        g_map = lambda k, i, pos: (2 * pos[0] + k, i, pos[1])
        t_map = lambda k, i, pos: (k, i, pos[1])
        out_shape = (2, M, n_shard)
    else:
        m_shard = M // N_CHIPS
        block = (None, m_shard, N)
        grid = (2, 1)
        g_map = lambda k, i, pos: (2 * pos[0] + k, pos[1], 0)
        t_map = lambda k, i, pos: (k, pos[1], 0)
        out_shape = (2, m_shard, N)
    r_block = (None, None) + block[1:]

    def body(pos_ref, g_ref, t_ref, r0_ref, r1_ref, r2_ref, o_ref):
        chip = pos_ref[1]
        own = g_ref[...] + t_ref[...]
        r = [r0_ref[...].astype(F32), r1_ref[...].astype(F32), r2_ref[...].astype(F32)]
        total = None
        for s in range(N_CHIPS):
            rel = jnp.bitwise_xor(chip, s)
            term = jnp.where(rel == 0, own, jnp.where(rel == 2, r[0], jnp.where(rel == 1, r[1], r[2])))
            total = term if total is None else total + term
        o_ref[...] = total

    r_spec = lambda j: pl.BlockSpec(r_block, lambda k, i, pos: (j, k, i, 0))
    grid_spec = pltpu.PrefetchScalarGridSpec(
        num_scalar_prefetch=1, grid=grid,
        in_specs=[pl.BlockSpec(block, g_map), pl.BlockSpec(block, t_map), r_spec(0), r_spec(1), r_spec(2)],
        out_specs=pl.BlockSpec(block, lambda k, i, pos: (k, i, 0)))
    return pl.pallas_call(
        body, name=name, grid_spec=grid_spec, out_shape=jax.ShapeDtypeStruct(out_shape, F32),
        compiler_params=_params(2),
    )(pos, grad, theirs, received, received, received)


def _sum_devices(gathered):
    R8, C = gathered.shape
    R = R8 // N_DEV
    tc = C
    for cand in (4096, 2048, 1024, 512, 256, 128):
        if C % cand == 0:
            tc = cand
            break

    def body(g_ref, o_ref):
        total = g_ref[0:R, :]
        for d in range(1, N_DEV):
            total = total + g_ref[d * R:(d + 1) * R, :]
        o_ref[...] = total

    return pl.pallas_call(
        body, name="sum_devices", grid=(C // tc,),
        in_specs=[pl.BlockSpec((R8, tc), lambda j: (0, j))], out_specs=pl.BlockSpec((R, tc), lambda j: (0, j)),
        out_shape=jax.ShapeDtypeStruct((R, C), F32), compiler_params=_params(1),
    )(gathered)


_SMALL = ("pool_w", "pool_scale", "sgu_ln_g", "sgu_ln_b", "sgu_w", "sgu_b", "ln1_g", "ln1_b", "conv_w",
          "conv_b", "ln2_g", "ln2_b", "ada_b")
_WEIGHTS = ("ada_w", "ada_b", "w_in", "pool_w", "pool_scale", "sgu_ln_g", "sgu_ln_b", "sgu_w", "sgu_b", "w_out",
            "ln1_g", "ln1_b", "w_up", "conv_w", "conv_b", "w_down", "ln2_g", "ln2_b")


def kernel(x, c, ada_w, ada_b, w_in, pool_w, pool_scale, sgu_ln_g, sgu_ln_b, sgu_w, sgu_b, w_out, ln1_g, ln1_b, w_up, conv_w, conv_b, w_down, ln2_g, ln2_b, loss_target, m_ada_w, m_ada_b, m_w_in, m_pool_w, m_pool_scale, m_sgu_ln_g, m_sgu_ln_b, m_sgu_w, m_sgu_b, m_w_out, m_ln1_g, m_ln1_b, m_w_up, m_conv_w, m_conv_b, m_w_down, m_ln2_g, m_ln2_b, v_ada_w, v_ada_b, v_w_in, v_pool_w, v_pool_scale, v_sgu_ln_g, v_sgu_ln_b, v_sgu_w, v_sgu_b, v_w_out, v_ln1_g, v_ln1_b, v_w_up, v_conv_w, v_conv_b, v_w_down, v_ln2_g, v_ln2_b):
    weights = dict(ada_w=ada_w, ada_b=ada_b, w_in=w_in, pool_w=pool_w, pool_scale=pool_scale, sgu_ln_g=sgu_ln_g,
                   sgu_ln_b=sgu_ln_b, sgu_w=sgu_w, sgu_b=sgu_b, w_out=w_out, ln1_g=ln1_g, ln1_b=ln1_b, w_up=w_up,
                   conv_w=conv_w, conv_b=conv_b, w_down=w_down, ln2_g=ln2_g, ln2_b=ln2_b)
    mom_m = dict(ada_w=m_ada_w, ada_b=m_ada_b, w_in=m_w_in, pool_w=m_pool_w, pool_scale=m_pool_scale,
                 sgu_ln_g=m_sgu_ln_g, sgu_ln_b=m_sgu_ln_b, sgu_w=m_sgu_w, sgu_b=m_sgu_b, w_out=m_w_out,
                 ln1_g=m_ln1_g, ln1_b=m_ln1_b, w_up=m_w_up, conv_w=m_conv_w, conv_b=m_conv_b, w_down=m_w_down,
                 ln2_g=m_ln2_g, ln2_b=m_ln2_b)
    mom_v = dict(ada_w=v_ada_w, ada_b=v_ada_b, w_in=v_w_in, pool_w=v_pool_w, pool_scale=v_pool_scale,
                 sgu_ln_g=v_sgu_ln_g, sgu_ln_b=v_sgu_ln_b, sgu_w=v_sgu_w, sgu_b=v_sgu_b, w_out=v_w_out,
                 ln1_g=v_ln1_g, ln1_b=v_ln1_b, w_up=v_w_up, conv_w=v_conv_w, conv_b=v_conv_b, w_down=v_w_down,
                 ln2_g=v_ln2_g, ln2_b=v_ln2_b)

    ix, iy, ic = lax.axis_index("x"), lax.axis_index("y"), lax.axis_index("c")
    chip = 2 * ix + iy
    dev = 4 * ix + 2 * iy + ic
    pos = jnp.stack([ic, chip]).astype(jnp.int32)
    xs = x[0]
    target = loss_target[0]
    ff_shard = conv_w.shape[2]
    mod_shard = ada_w.shape[2]

    first = _gather_flat(jnp.concatenate([c.reshape(-1), conv_w.reshape(-1)]), "gather_c_conv")
    c_all = jnp.pad(first[:, :D], ((0, 8), (0, 0)))
    conv_parts = first[0::2, D:].reshape(N_CHIPS, DEPTH, 3, ff_shard)
    conv_full = jnp.transpose(conv_parts, (1, 2, 0, 3)).reshape(DEPTH, 3, D_FF)
    ada_b_cols = lax.dynamic_slice_in_dim(ada_b, chip * mod_shard, mod_shard, axis=1).reshape(DEPTH, 1, mod_shard)
    mod_part = _ada_forward(c_all, ada_w, ada_b_cols)[:, :8, :]
    mod_all = _gather_flat(mod_part.reshape(-1), "gather_mod").reshape(N_DEV, DEPTH, 8, mod_shard)
    mod_mine = lax.dynamic_index_in_dim(mod_all[0::2], dev, axis=2, keepdims=False)
    mod = jnp.transpose(mod_mine, (1, 0, 2)).reshape(DEPTH, 6, 1, D)

    wf_in, wf_out, wf_up, wf_down = _gather_weights(
        [w_in.astype(WIRE_DTYPE), w_out.astype(WIRE_DTYPE), w_up.astype(WIRE_DTYPE), w_down.astype(WIRE_DTYPE)])

    tri = jnp.tril(jnp.ones((CHUNK, CHUNK), dtype=bool))
    ones_row = jnp.ones((1, D), F32)
    zeros_row = jnp.zeros((1, D), F32)
    row = lambda a, l: a[l].reshape(1, -1)

    saved = []
    xn, lg, lb = xs, ones_row, zeros_row
    for l in range(DEPTH):
        shift1, scale1, gate1, shift2, scale2, gate2 = (mod[l, k] for k in range(6))
        pw = pool_w[l].astype(MXU_DTYPE)
        wm = jnp.where(tri[None], sgu_w[l], 0.0).astype(MXU_DTYPE)
        bias_full = jnp.repeat(jnp.transpose(sgu_b[l]), GROUP, axis=1)
        h, proj = _f1_in_proj(xn, lg, lb, shift1, scale1, wf_in, l)
        mix, f1, xhat1, rstd1 = _f2_mixers(proj, xn, lg, lb, gate1, pw, row(pool_scale, l), row(sgu_ln_g, l),
                                           row(sgu_ln_b, l), wm, bias_full, wf_out, l)
        lg1, lb1 = row(ln1_g, l), row(ln1_b, l)
        h2, up = _f3_up_proj(xhat1, lg1, lb1, shift2, scale2, wf_up, l)
        act, f2, xhat2, rstd2 = _f4_ffn(up, xhat1, lg1, lb1, gate2, conv_full[l], row(conv_b, l), wf_down, l)
        saved.append(dict(xn=xn, lg=lg, lb=lb, h=h, proj=proj, mix=mix, f1=f1, xhat1=xhat1, rstd1=rstd1, h2=h2,
                          up=up, act=act, f2=f2, xhat2=xhat2, rstd2=rstd2, pw=pw, wm=wm, bias_full=bias_full))
        xn, lg, lb = xhat2, row(ln2_g, l), row(ln2_b, l)

    loss_part, dxo = _loss_head(xn, lg, lb, target)
    loss = lax.psum(loss_part[0, 0], ("x", "y", "c"))

    g_in = g_out = g_up = g_down = None
    small = {n: [None] * DEPTH for n in _SMALL}
    dmod = [None] * DEPTH
    for l in reversed(range(DEPTH)):
        sv = saved[l]
        shift1, scale1, gate1, shift2, scale2, gate2 = (mod[l, k] for k in range(6))
        lg1, lb1 = row(ln1_g, l), row(ln1_b, l)
        df2, dres2, dlg2, dlb2, dgate2 = _b_ln(dxo, sv["xhat2"], sv["rstd2"], sv["f2"], row(ln2_g, l), gate2,
                                              f"b1_ln2_l{l}")
        dval, dgc, dcw, dcb = _b2_ffn(df2, wf_down, sv["up"], conv_full[l], row(conv_b, l), l)
        g_down = _weight_grad(sv["act"], df2, g_down, l, 0, 512, D, f"wg_down_l{l}")
        dg, dxo1, dscale2, dshift2 = _b3_up(dgc, dval, wf_up, dres2, sv["xhat1"], lg1, lb1, scale2, conv_full[l], l)
        g_up = _weight_grad(sv["h2"], dg, g_up, l, 0, 1408, 2 * D_FF, f"wg_up_g_l{l}")
        g_up = _weight_grad(sv["h2"], dval, g_up, l, D_FF, 1408, 2 * D_FF, f"wg_up_v_l{l}")
        df1, dres1, dlg1, dlb1, dgate1 = _b_ln(dxo1, sv["xhat1"], sv["rstd1"], sv["f1"], lg1, gate1, f"b4_ln1_l{l}")
        dproj, dpw, dps, dsg, dsb, dwm, dbias = _b5_mixers(
            df1, wf_out, sv["proj"], sv["pw"], row(pool_scale, l), row(sgu_ln_g, l), row(sgu_ln_b, l), sv["wm"],
            sv["bias_full"], l)
        g_out = _weight_grad(sv["mix"], df1, g_out, l, 0, 1024, D, f"wg_out_l{l}")
        dxo, dscale1, dshift1 = _b6_in(dproj, wf_in, dres1, sv["xn"], sv["lg"], sv["lb"], scale1, l)
        g_in = _weight_grad(sv["h"], dproj, g_in, l, 0, 768, D_IN, f"wg_in_l{l}")
        small["pool_w"][l], small["pool_scale"][l] = dpw, dps[0]
        small["sgu_ln_g"][l], small["sgu_ln_b"][l], small["sgu_w"][l] = dsg[0], dsb[0], dwm
        small["sgu_b"][l] = jnp.transpose(jnp.sum(dbias.reshape(CHUNK, HEADS, GROUP), axis=2))
        small["ln1_g"][l], small["ln1_b"][l] = dlg1[0], dlb1[0]
        small["conv_w"][l], small["conv_b"][l] = dcw, dcb[0]
        small["ln2_g"][l], small["ln2_b"][l] = dlg2[0], dlb2[0]
        dmod[l] = jnp.concatenate([dshift1[0], dscale1[0], dgate1[0], dshift2[0], dscale2[0], dgate2[0]])
    grad_x = dxo[None]
    small["ada_b"] = dmod

    names = _SMALL
    flat = jnp.concatenate([jnp.stack(small[n]).reshape(-1) for n in names])
    n_small = flat.shape[0]
    padded = -(-n_small // 1024) * 1024
    gathered = _all_gather8(jnp.pad(flat, (0, padded - n_small)).reshape(8, padded // 8), "gather_small_grads")
    summed = _sum_devices(gathered).reshape(-1)[:n_small]
    grads = {}
    offset = 0
    for n in names:
        size = math.prod(weights[n].shape[1:]) * DEPTH if n != "conv_w" else DEPTH * 3 * D_FF
        grads[n] = summed[offset:offset + size]
        offset += size
    grads["conv_w"] = lax.dynamic_slice_in_dim(grads["conv_w"].reshape(DEPTH, 3, D_FF), chip * ff_shard, ff_shard,
                                               axis=2)
    for n in names:
        grads[n] = grads[n].reshape(weights[n].shape)
    dmod_all = gathered.reshape(N_DEV, padded)[:, n_small - DEPTH * 6 * D:n_small].reshape(N_DEV, DEPTH, 6 * D)
    dmod_cols = lax.dynamic_slice_in_dim(jnp.transpose(dmod_all, (1, 0, 2)), chip * mod_shard, mod_shard, axis=2)
    grads["ada_w"] = _ada_backward(c_all, jnp.pad(dmod_cols, ((0, 0), (0, 8), (0, 0))))

    full = [g_in, g_out, g_up, g_down]
    theirs = _swap_halves(full)
    tags = ("in", "out", "up", "down")
    partials = [_chip_partial(pos, g, t, f"chip_partial_{tag}") for g, t, tag in zip(full, theirs, tags)]
    received = _scatter_partials(partials)
    halves = [_reduce_shard(pos, g, t, r, kind, f"reduce_shard_{tag}")
              for g, t, r, kind, tag in zip(full, theirs, received, _SHARD_KINDS, tags)]
    grads["w_in"], grads["w_out"], grads["w_up"], grads["w_down"] = _share_reduced(halves)

    delta, new_m, new_v = {}, {}, {}
    for n in ("ada_w", "w_in", "w_out", "w_up", "w_down"):
        shape = weights[n].shape
        two_d = (shape[0] * shape[1], shape[2])
        d_, m_, v_ = _adamw(weights[n].reshape(two_d), grads[n].reshape(two_d), mom_m[n].reshape(two_d),
                            mom_v[n].reshape(two_d), f"adamw_{n}")
        delta[n], new_m[n], new_v[n] = d_.reshape(shape), m_.reshape(shape), v_.reshape(shape)
    sizes = [math.prod(weights[n].shape) for n in names]
    total = sum(sizes)
    padded = -(-total // 1024) * 1024
    pack = lambda d: jnp.pad(jnp.concatenate([d[n].reshape(-1) for n in names]), (0, padded - total)).reshape(8, -1)
    v_pack = jnp.pad(jnp.concatenate([mom_v[n].reshape(-1) for n in names]), (0, padded - total),
                     constant_values=1.0).reshape(8, -1)
    d_, m_, v_ = _adamw(pack(weights), pack(grads), pack(mom_m), v_pack, "adamw_small")
    offset = 0
    for n, size in zip(names, sizes):
        for src, dst in ((d_, delta), (m_, new_m), (v_, new_v)):
            dst[n] = src.reshape(-1)[offset:offset + size].reshape(weights[n].shape)
        offset += size

    return (loss, grad_x, *[grads[n] for n in _WEIGHTS], *[delta[n] for n in _WEIGHTS],
            *[new_m[n] for n in _WEIGHTS], *[new_v[n] for n in _WEIGHTS])
```

```python
import functools
import math

import jax
import jax.numpy as jnp
from jax import lax
from jax.experimental import pallas as pl
from jax.experimental.pallas import tpu as pltpu

F32 = jnp.float32
MXU_DTYPE = jnp.bfloat16
WIRE_DTYPE = jnp.bfloat16

DEPTH = 4
D = 1024
D_POOL = 512
D_SGU = 512
N_GROUPS = 4
GROUP = 128
POOL_WINDOWS = (2, 4, 8, 16)
POOL_HALO = 16
CHUNK = 128
HEADS = 4
D_IN = D_POOL + 2 * D_SGU
D_FF = 2816
CONV_HALO = 16
STORE_DTYPE = jnp.bfloat16
N_CHIPS = 4
N_DEV = 8
ALPHA = (2.0 * DEPTH) ** 0.25
LN_EPS = 1e-5
ADAM_LR, ADAM_B1, ADAM_B2, ADAM_EPS, ADAM_WD, ADAM_STEP = 0.001, 0.9, 0.999, 1e-08, 0.01, 10

TS_PROJ = 512
TS_MM = 1024
TS_MIX = 256
TS_FF = 256
TS_TN = 1024
VMEM_LIMIT = 52 * 2 ** 20

MESH = pl.DeviceIdType.MESH
ANY = pl.BlockSpec(memory_space=pl.ANY)

_GELU_K0 = math.sqrt(2.0 / math.pi)
_GELU_K1 = 0.044715


def _params(n_axes):
    return pltpu.CompilerParams(dimension_semantics=("arbitrary",) * n_axes, vmem_limit_bytes=VMEM_LIMIT)


def _dot(a, b):
    return jnp.dot(a, b, preferred_element_type=F32)


def _dot_nt(a, b):
    return lax.dot_general(a, b, (((1,), (1,)), ((), ())), preferred_element_type=F32)


def _dot_tn(a, b):
    return lax.dot_general(a, b, (((0,), (0,)), ((), ())), preferred_element_type=F32)


def _gelu(x):
    x2 = x * x
    t = jnp.tanh(_GELU_K0 * (x + _GELU_K1 * (x2 * x)))
    cdf = 0.5 * (1.0 + t)
    dg = cdf + x * (0.5 * (1.0 - t * t)) * (_GELU_K0 * (1.0 + 3.0 * _GELU_K1 * x2))
    return x * cdf, dg


def _colsum(x):
    return jnp.sum(x, axis=0, keepdims=True)


def _row(d):
    return pl.BlockSpec((1, d), lambda *_: (0, 0))


def _full(shape):
    n = len(shape)
    return pl.BlockSpec(shape, lambda *_: (0,) * n)


def _layer_block(shape, layer):
    n = len(shape)
    return pl.BlockSpec((None,) + tuple(shape), lambda *_: (layer,) + (0,) * n)


def _pool_counts(first_row, rows, window):
    t = first_row + lax.broadcasted_iota(jnp.int32, (rows, 1), 0)
    return 1.0 / jnp.minimum(t + 1, window).astype(F32)


def _pool_forward(a_ext, a, first_row, rows):
    out = []
    for g, window in enumerate(POOL_WINDOWS):
        s = a_ext[:, g * GROUP:(g + 1) * GROUP]
        k = 1
        while k < window:
            s = s + pltpu.roll(s, k, 0)
            k *= 2
        inv = _pool_counts(first_row, rows, window)
        out.append(s[POOL_HALO:] * inv - a[:, g * GROUP:(g + 1) * GROUP])
    return out


def _sgu_norm(v):
    gv, dgv = _gelu(v)
    mu = jnp.mean(gv, axis=-1, keepdims=True)
    xc = gv - mu
    var = jnp.mean(xc * xc, axis=-1, keepdims=True)
    rs = lax.rsqrt(var + LN_EPS)
    return xc * rs, rs, dgv


def _modulate_input(x, shift, scale):
    S = x.shape[0]
    ts = min(TS_PROJ, S)

    def body(x_ref, sh_ref, sc_ref, h_ref):
        h_ref[...] = (x_ref[...] * (1.0 + sc_ref[...]) + sh_ref[...]).astype(MXU_DTYPE)

    tile = pl.BlockSpec((ts, D), lambda i: (i, 0))
    return pl.pallas_call(
        body, name="modulate_input", grid=(S // ts,), in_specs=[tile, _row(D), _row(D)], out_specs=tile,
        out_shape=jax.ShapeDtypeStruct((S, D), MXU_DTYPE), compiler_params=_params(1),
    )(x, shift, scale)


def _matmul(a, w_all, layer, tn, out_dtype, name):
    S, K = a.shape
    N = w_all.shape[2]
    ts = min(TS_MM, S)

    def body(a_ref, w_ref, o_ref):
        o_ref[...] = _dot(a_ref[...], w_ref[...]).astype(out_dtype)

    return pl.pallas_call(
        body, name=name, grid=(N // tn, S // ts),
        in_specs=[pl.BlockSpec((ts, K), lambda j, i: (i, 0)), pl.BlockSpec((None, K, tn), lambda j, i: (layer, 0, j))],
        out_specs=pl.BlockSpec((ts, tn), lambda j, i: (i, j)),
        out_shape=jax.ShapeDtypeStruct((S, N), out_dtype), compiler_params=_params(2),
    )(a, w_all)


def _matmul_nt(a, w_all, layer, tk, name):
    S, K = a.shape
    N = w_all.shape[1]
    ts = min(TS_MM, S)
    n_k = K // tk

    def body(a_ref, w_ref, o_ref, acc):
        k = pl.program_id(1)
        part = _dot_nt(a_ref[...], w_ref[...])
        if n_k == 1:
            o_ref[...] = part
        else:
            @pl.when(k == 0)
            def _():
                acc[...] = part

            @pl.when(jnp.logical_and(k > 0, k < n_k - 1))
            def _():
                acc[...] += part

            @pl.when(k == n_k - 1)
            def _():
                o_ref[...] = acc[...] + part

    return pl.pallas_call(
        body, name=name, grid=(S // ts, n_k),
        in_specs=[pl.BlockSpec((ts, tk), lambda i, k: (i, k)), pl.BlockSpec((None, N, tk), lambda i, k: (layer, 0, k))],
        out_specs=pl.BlockSpec((ts, N), lambda i, k: (i, 0)),
        out_shape=jax.ShapeDtypeStruct((S, N), F32),
        scratch_shapes=[pltpu.VMEM((ts, N), F32)], compiler_params=_params(2),
    )(a, w_all)


def _f2_mixers(proj, xn, lg, lb, gate, pool_w, pool_scale, sg, sb, wm, bias_full, w_out_all, lg1, lb1, shift2, scale2,
               layer):
    S = xn.shape[0]
    ts = min(TS_MIX, S)
    n_chunks = ts // CHUNK
    halo_blocks = ts // POOL_HALO

    def body(p_ref, halo_ref, xn_ref, lg_ref, lb_ref, gate_ref, pw_ref, ps_ref, sg_ref, sb_ref, wm_ref,
             bias_ref, wo_ref, lg1_ref, lb1_ref, sh2_ref, sc2_ref, mix_ref, f_ref, xh_ref, rs_ref, h2_ref, z_scr):
        i = pl.program_id(0)
        a = p_ref[:, 0:D_POOL].astype(F32)
        u = p_ref[:, D_POOL:D_POOL + D_SGU].astype(F32)
        v = p_ref[:, D_POOL + D_SGU:D_IN].astype(F32)
        halo = halo_ref[...].astype(F32) * (i > 0).astype(F32)
        a_ext = jnp.concatenate([halo, a], axis=0)
        pooled = _pool_forward(a_ext, a, i * ts, ts)
        for g in range(N_GROUPS):
            mixed = _dot(pooled[g].astype(MXU_DTYPE), pw_ref[g])
            mix_ref[:, g * GROUP:(g + 1) * GROUP] = (mixed * ps_ref[:, g * GROUP:(g + 1) * GROUP]).astype(MXU_DTYPE)
        gu, _ = _gelu(u)
        vhat, _, _ = _sgu_norm(v)
        vn = (vhat * sg_ref[...] + sb_ref[...]).astype(MXU_DTYPE)
        for c in range(n_chunks):
            for h in range(HEADS):
                blk = vn[c * CHUNK:(c + 1) * CHUNK, h * GROUP:(h + 1) * GROUP]
                z_scr[c * CHUNK:(c + 1) * CHUNK, h * GROUP:(h + 1) * GROUP] = (
                    _dot(wm_ref[h], blk) + bias_ref[:, h * GROUP:(h + 1) * GROUP])
        mix_ref[:, D_POOL:D] = (gu * z_scr[...]).astype(MXU_DTYPE)
        f = _dot(mix_ref[...], wo_ref[...])
        f_ref[...] = f.astype(STORE_DTYPE)
        x = xn_ref[...] * lg_ref[...] + lb_ref[...]
        z1 = ALPHA * x + gate_ref[...] * f
        mu = jnp.mean(z1, axis=-1, keepdims=True)
        zc = z1 - mu
        var = jnp.mean(zc * zc, axis=-1, keepdims=True)
        rs = lax.rsqrt(var + LN_EPS)
        xhat = zc * rs
        xh_ref[...] = xhat
        rs_ref[...] = rs
        x1 = xhat * lg1_ref[...] + lb1_ref[...]
        h2_ref[...] = (x1 * (1.0 + sc2_ref[...]) + sh2_ref[...]).astype(MXU_DTYPE)

    tile = lambda w: pl.BlockSpec((ts, w), lambda i: (i, 0))
    return pl.pallas_call(
        body, name=f"f2_mixers_l{layer}", grid=(S // ts,),
        in_specs=[tile(D_IN),
                  pl.BlockSpec((POOL_HALO, D_POOL), lambda i: (jnp.maximum(i * halo_blocks - 1, 0), 0)),
                  tile(D), _row(D), _row(D), _row(D),
                  _full((N_GROUPS, GROUP, GROUP)), _row(D_POOL), _row(D_SGU), _row(D_SGU),
                  _full((HEADS, CHUNK, CHUNK)), _full((CHUNK, D_SGU)), _layer_block((D, D), layer),
                  _row(D), _row(D), _row(D), _row(D)],
        out_specs=[tile(D), tile(D), tile(D), tile(1), tile(D)],
        out_shape=[jax.ShapeDtypeStruct((S, D), MXU_DTYPE), jax.ShapeDtypeStruct((S, D), STORE_DTYPE),
                   jax.ShapeDtypeStruct((S, D), F32), jax.ShapeDtypeStruct((S, 1), F32),
                   jax.ShapeDtypeStruct((S, D), MXU_DTYPE)],
        scratch_shapes=[pltpu.VMEM((ts, D_SGU), F32)],
        compiler_params=_params(1),
    )(proj, proj, xn, lg, lb, gate, pool_w, pool_scale, sg, sb, wm, bias_full, w_out_all, lg1, lb1, shift2, scale2)


def _conv_forward(g_ext, cw_ref, cb_ref):
    gm2 = pltpu.roll(g_ext, 2, 0)[CONV_HALO:]
    gm1 = pltpu.roll(g_ext, 1, 0)[CONV_HALO:]
    g0 = g_ext[CONV_HALO:]
    gc = ((cb_ref[...] + gm2 * cw_ref[0:1, :]) + gm1 * cw_ref[1:2, :]) + g0 * cw_ref[2:3, :]
    return gc, gm2, gm1


def _f4_ffn(up, xhat1, lg, lb, gate, conv_w, conv_b, w_down_all, lg2, lb2, shift_next, scale_next, layer):
    S = xhat1.shape[0]
    ts = min(TS_FF, S)
    halo_blocks = ts // CONV_HALO

    def body(g_ref, halo_ref, val_ref, xh_ref, lg_ref, lb_ref, gate_ref, cw_ref, cb_ref, wd_ref, lg2_ref, lb2_ref,
             shn_ref, scn_ref, act_ref, f_ref, xo_ref, rs_ref, hn_ref):
        i = pl.program_id(0)
        halo = halo_ref[...].astype(F32) * (i > 0).astype(F32)
        g_ext = jnp.concatenate([halo, g_ref[...].astype(F32)], axis=0)
        gc, _, _ = _conv_forward(g_ext, cw_ref, cb_ref)
        ge, _ = _gelu(gc)
        act = (ge * val_ref[...].astype(F32)).astype(MXU_DTYPE)
        act_ref[...] = act
        f = _dot(act, wd_ref[...])
        f_ref[...] = f.astype(STORE_DTYPE)
        x = xh_ref[...] * lg_ref[...] + lb_ref[...]
        z = ALPHA * x + gate_ref[...] * f
        mu = jnp.mean(z, axis=-1, keepdims=True)
        zc = z - mu
        var = jnp.mean(zc * zc, axis=-1, keepdims=True)
        rs = lax.rsqrt(var + LN_EPS)
        xhat = zc * rs
        xo_ref[...] = xhat
        rs_ref[...] = rs
        x2 = xhat * lg2_ref[...] + lb2_ref[...]
        hn_ref[...] = (x2 * (1.0 + scn_ref[...]) + shn_ref[...]).astype(MXU_DTYPE)

    tile = lambda w: pl.BlockSpec((ts, w), lambda i: (i, 0))
    return pl.pallas_call(
        body, name=f"f4_ffn_l{layer}", grid=(S // ts,),
        in_specs=[pl.BlockSpec((ts, D_FF), lambda i: (i, 0)),
                  pl.BlockSpec((CONV_HALO, D_FF), lambda i: (jnp.maximum(i * halo_blocks - 1, 0), 0)),
                  pl.BlockSpec((ts, D_FF), lambda i: (i, 1)),
                  tile(D), _row(D), _row(D), _row(D), _full((3, D_FF)), _row(D_FF),
                  _layer_block((D_FF, D), layer), _row(D), _row(D), _row(D), _row(D)],
        out_specs=[tile(D_FF), tile(D), tile(D), tile(1), tile(D)],
        out_shape=[jax.ShapeDtypeStruct((S, D_FF), MXU_DTYPE), jax.ShapeDtypeStruct((S, D), STORE_DTYPE),
                   jax.ShapeDtypeStruct((S, D), F32), jax.ShapeDtypeStruct((S, 1), F32),
                   jax.ShapeDtypeStruct((S, D), MXU_DTYPE)],
        compiler_params=_params(1),
    )(up, up, up, xhat1, lg, lb, gate, conv_w, conv_b, w_down_all, lg2, lb2, shift_next, scale_next)


def _loss_head(xhat, lg, lb, target):
    S = xhat.shape[0]
    ts = min(TS_PROJ, S)

    def body(xh_ref, lg_ref, lb_ref, t_ref, loss_ref, dy_ref, acc):
        i = pl.program_id(0)

        @pl.when(i == 0)
        def _():
            acc[...] = jnp.zeros_like(acc)

        err = (xh_ref[...] * lg_ref[...] + lb_ref[...]) - t_ref[...]
        dy_ref[...] = err * (1.0 / D)
        acc[...] += _colsum(err * err)

        @pl.when(i == pl.num_programs(0) - 1)
        def _():
            loss_ref[...] = jnp.sum(acc[...], axis=1, keepdims=True) * (0.5 / D)

    tile = pl.BlockSpec((ts, D), lambda i: (i, 0))
    return pl.pallas_call(
        body, name="loss_head", grid=(S // ts,),
        in_specs=[tile, _row(D), _row(D), tile],
        out_specs=[_full((1, 1)), tile],
        out_shape=[jax.ShapeDtypeStruct((1, 1), F32), jax.ShapeDtypeStruct((S, D), F32)],
        scratch_shapes=[pltpu.VMEM((1, D), F32)],
        compiler_params=_params(1),
    )(xhat, lg, lb, target)


def _accumulate(i, ref, value):
    @pl.when(i == 0)
    def _():
        ref[...] = value

    @pl.when(i > 0)
    def _():
        ref[...] += value


def _b_ln(d, xhat, rstd, f, lg, lb, gate, name, modulated=None):
    S = d.shape[0]
    ts = min(TS_PROJ, S)
    has_mod = modulated is not None

    def body(*refs):
        if has_mod:
            (d_ref, xh_ref, rs_ref, f_ref, lg_ref, lb_ref, gate_ref, dri_ref, sc_ref,
             df_ref, dres_ref, dlg_ref, dlb_ref, dgate_ref, dsc_ref, dsh_ref) = refs
        else:
            (d_ref, xh_ref, rs_ref, f_ref, lg_ref, lb_ref, gate_ref,
             df_ref, dres_ref, dlg_ref, dlb_ref, dgate_ref) = refs
        i = pl.program_id(0)
        xh = xh_ref[...]
        if has_mod:
            dh = d_ref[...]
            dxo_t = dh * (1.0 + sc_ref[...]) + dri_ref[...]
            _accumulate(i, dsc_ref, _colsum(dh * (xh * lg_ref[...] + lb_ref[...])))
            _accumulate(i, dsh_ref, _colsum(dh))
        else:
            dxo_t = d_ref[...]
        dxh = dxo_t * lg_ref[...]
        m1 = jnp.mean(dxh, axis=-1, keepdims=True)
        m2 = jnp.mean(dxh * xh, axis=-1, keepdims=True)
        dz = rs_ref[...] * (dxh - m1 - xh * m2)
        df_ref[...] = (dz * gate_ref[...]).astype(MXU_DTYPE)
        dres_ref[...] = ALPHA * dz
        _accumulate(i, dlg_ref, _colsum(dxo_t * xh))
        _accumulate(i, dlb_ref, _colsum(dxo_t))
        _accumulate(i, dgate_ref, _colsum(dz * f_ref[...].astype(F32)))

    tile = lambda w: pl.BlockSpec((ts, w), lambda i: (i, 0))
    n_sums = 5 if has_mod else 3
    in_specs = [tile(D), tile(D), tile(1), tile(D), _row(D), _row(D), _row(D)]
    args = [d, xhat, rstd, f, lg, lb, gate]
    if has_mod:
        in_specs += [tile(D), _row(D)]
        args += list(modulated)
    return pl.pallas_call(
        body, name=name, grid=(S // ts,), in_specs=in_specs,
        out_specs=[tile(D), tile(D)] + [_row(D)] * n_sums,
        out_shape=[jax.ShapeDtypeStruct((S, D), MXU_DTYPE), jax.ShapeDtypeStruct((S, D), F32)]
        + [jax.ShapeDtypeStruct((1, D), F32)] * n_sums,
        compiler_params=_params(1),
    )(*args)


def _input_grad(dh, dres, scale, x):
    S = dh.shape[0]
    ts = min(TS_PROJ, S)

    def body(dh_ref, dres_ref, sc_ref, x_ref, dx_ref, dsc_ref, dsh_ref):
        i = pl.program_id(0)
        dh_t = dh_ref[...]
        dx_ref[...] = dh_t * (1.0 + sc_ref[...]) + dres_ref[...]
        _accumulate(i, dsc_ref, _colsum(dh_t * x_ref[...]))
        _accumulate(i, dsh_ref, _colsum(dh_t))

    tile = pl.BlockSpec((ts, D), lambda i: (i, 0))
    return pl.pallas_call(
        body, name="input_grad", grid=(S // ts,), in_specs=[tile, tile, _row(D), tile],
        out_specs=[tile, _row(D), _row(D)],
        out_shape=[jax.ShapeDtypeStruct((S, D), F32)] + [jax.ShapeDtypeStruct((1, D), F32)] * 2,
        compiler_params=_params(1),
    )(dh, dres, scale, x)


def _b2_ffn(df2, w_down_all, up, conv_w, conv_b, layer):
    S = df2.shape[0]
    ts = min(TS_FF, S)
    n_tiles = S // ts
    halo_blocks = ts // CONV_HALO
    n_ext = ts + CONV_HALO

    def body(df_ref, wd_ref, g_ref, halo_ref, val_ref, cw_ref, cb_ref, dup_ref, dcw_ref, dcb_ref, next_dgc):
        i = pl.program_id(0)
        tile_idx = n_tiles - 1 - i

        @pl.when(i == 0)
        def _():
            next_dgc[...] = jnp.zeros_like(next_dgc)

        dact = _dot_nt(df_ref[...], wd_ref[...])
        g = g_ref[...].astype(F32)
        halo = halo_ref[...].astype(F32) * (tile_idx > 0).astype(F32)
        gc, gm2, gm1 = _conv_forward(jnp.concatenate([halo, g], axis=0), cw_ref, cb_ref)
        ge, dge = _gelu(gc)
        dup_ref[:, D_FF:2 * D_FF] = (dact * ge).astype(MXU_DTYPE)
        dgc = dact * val_ref[...].astype(F32) * dge
        ext = jnp.concatenate([dgc, next_dgc[...]], axis=0)
        dp1 = pltpu.roll(ext, n_ext - 1, 0)[:ts]
        dp2 = pltpu.roll(ext, n_ext - 2, 0)[:ts]
        dup_ref[:, 0:D_FF] = (dgc * cw_ref[2:3, :] + dp1 * cw_ref[1:2, :] + dp2 * cw_ref[0:1, :]).astype(MXU_DTYPE)
        next_dgc[...] = dgc[0:CONV_HALO]
        dcw = jnp.concatenate([_colsum(dgc * gm2), _colsum(dgc * gm1), _colsum(dgc * g)], axis=0)
        _accumulate(i, dcw_ref, dcw)
        _accumulate(i, dcb_ref, _colsum(dgc))

    tile = lambda w, col=0: pl.BlockSpec((ts, w), lambda i: (n_tiles - 1 - i, col))
    return pl.pallas_call(
        body, name=f"b2_ffn_l{layer}", grid=(n_tiles,),
        in_specs=[tile(D), _layer_block((D_FF, D), layer), tile(D_FF),
                  pl.BlockSpec((CONV_HALO, D_FF),
                               lambda i: (jnp.maximum((n_tiles - 1 - i) * halo_blocks - 1, 0), 0)),
                  tile(D_FF, 1), _full((3, D_FF)), _row(D_FF)],
        out_specs=[tile(2 * D_FF), _full((3, D_FF)), _row(D_FF)],
        out_shape=[jax.ShapeDtypeStruct((S, 2 * D_FF), MXU_DTYPE),
                   jax.ShapeDtypeStruct((3, D_FF), F32), jax.ShapeDtypeStruct((1, D_FF), F32)],
        scratch_shapes=[pltpu.VMEM((CONV_HALO, D_FF), F32)],
        compiler_params=_params(1),
    )(df2, w_down_all, up, up, up, conv_w, conv_b)


def _b5_mixers(df1, w_out_all, proj, pool_w, pool_scale, sg, sb, wm, bias_full, layer):
    S = df1.shape[0]
    ts = min(TS_MIX, S)
    n_chunks = ts // CHUNK
    halo_blocks = ts // POOL_HALO
    last_halo = S // POOL_HALO - 1

    def body(df_ref, dfh_ref, wo_ref, p_ref, ah_ref, pw_ref, ps_ref, sg_ref, sb_ref, wm_ref, bias_ref,
             dp_ref, dpw_ref, dps_ref, dsg_ref, dsb_ref, dwm_ref, dbias_ref, z_scr, dvn_scr):
        i = pl.program_id(0)
        last = pl.num_programs(0) - 1
        dmix = _dot_nt(df_ref[...], wo_ref[...])
        dmix_halo = _dot_nt(dfh_ref[...], wo_ref[0:D_POOL, :]) * (i < last).astype(F32)

        a = p_ref[:, 0:D_POOL].astype(F32)
        halo = ah_ref[...].astype(F32) * (i > 0).astype(F32)
        pooled = _pool_forward(jnp.concatenate([halo, a], axis=0), a, i * ts, ts)
        n = ts + POOL_HALO
        dps_parts = []
        for g, window in enumerate(POOL_WINDOWS):
            cols = slice(g * GROUP, (g + 1) * GROUP)
            pooled_b = pooled[g].astype(MXU_DTYPE)
            mixed = _dot(pooled_b, pw_ref[g])
            dya = dmix[:, cols]
            dps_parts.append(_colsum(dya * mixed))
            dmixed = (dya * ps_ref[:, cols]).astype(MXU_DTYPE)
            dpw_g = _dot_tn(pooled_b, dmixed)

            @pl.when(i == 0)
            def _():
                dpw_ref[g] = dpw_g

            @pl.when(i > 0)
            def _():
                dpw_ref[g] += dpw_g

            dpooled = _dot_nt(dmixed, pw_ref[g])
            dmixed_h = (dmix_halo[:, cols] * ps_ref[:, cols]).astype(MXU_DTYPE)
            dpooled_h = _dot_nt(dmixed_h, pw_ref[g])
            q = dpooled * _pool_counts(i * ts, ts, window)
            s = jnp.concatenate([q, dpooled_h * (1.0 / window)], axis=0)
            k = 1
            while k < window:
                s = s + pltpu.roll(s, n - k, 0)
                k *= 2
            dp_ref[:, cols] = (s[:ts] - dpooled).astype(MXU_DTYPE)
        _accumulate(i, dps_ref, jnp.concatenate(dps_parts, axis=1))

        u = p_ref[:, D_POOL:D_POOL + D_SGU].astype(F32)
        v = p_ref[:, D_POOL + D_SGU:D_IN].astype(F32)
        gu, dgu = _gelu(u)
        vhat, rs, dgv = _sgu_norm(v)
        vn = (vhat * sg_ref[...] + sb_ref[...]).astype(MXU_DTYPE)
        dyb = dmix[:, D_POOL:D]
        dz = dyb * gu
        dzb = dz.astype(MXU_DTYPE)
        dbias = dz[0:CHUNK]
        for c in range(1, n_chunks):
            dbias = dbias + dz[c * CHUNK:(c + 1) * CHUNK]
        _accumulate(i, dbias_ref, dbias)
        for h in range(HEADS):
            cols = slice(h * GROUP, (h + 1) * GROUP)
            dwm_h = None
            for c in range(n_chunks):
                rows = slice(c * CHUNK, (c + 1) * CHUNK)
                vn_blk = vn[rows, cols]
                dz_blk = dzb[rows, cols]
                z_scr[rows, cols] = _dot(wm_ref[h], vn_blk) + bias_ref[:, cols]
                dvn_scr[rows, cols] = _dot_tn(wm_ref[h], dz_blk)
                part = _dot_nt(dz_blk, vn_blk)
                dwm_h = part if dwm_h is None else dwm_h + part

            @pl.when(i == 0)
            def _():
                dwm_ref[h] = dwm_h

            @pl.when(i > 0)
            def _():
                dwm_ref[h] += dwm_h

        dp_ref[:, D_POOL:D_POOL + D_SGU] = (dyb * z_scr[...] * dgu).astype(MXU_DTYPE)
        dvn = dvn_scr[...]
        _accumulate(i, dsg_ref, _colsum(dvn * vhat))
        _accumulate(i, dsb_ref, _colsum(dvn))
        dvh = dvn * sg_ref[...]
        m1 = jnp.mean(dvh, axis=-1, keepdims=True)
        m2 = jnp.mean(dvh * vhat, axis=-1, keepdims=True)
        dp_ref[:, D_POOL + D_SGU:D_IN] = (rs * (dvh - m1 - vhat * m2) * dgv).astype(MXU_DTYPE)

        @pl.when(i == last)
        def _():
            tri = (lax.broadcasted_iota(jnp.int32, (CHUNK, CHUNK), 0)
                   >= lax.broadcasted_iota(jnp.int32, (CHUNK, CHUNK), 1))
            for h in range(HEADS):
                dwm_ref[h] = jnp.where(tri, dwm_ref[h], 0.0)

    tile = lambda w: pl.BlockSpec((ts, w), lambda i: (i, 0))
    return pl.pallas_call(
        body, name=f"b5_mixers_l{layer}", grid=(S // ts,),
        in_specs=[tile(D),
                  pl.BlockSpec((POOL_HALO, D), lambda i: (jnp.minimum((i + 1) * halo_blocks, last_halo), 0)),
                  _layer_block((D, D), layer), tile(D_IN),
                  pl.BlockSpec((POOL_HALO, D_POOL), lambda i: (jnp.maximum(i * halo_blocks - 1, 0), 0)),
                  _full((N_GROUPS, GROUP, GROUP)), _row(D_POOL), _row(D_SGU), _row(D_SGU),
                  _full((HEADS, CHUNK, CHUNK)), _full((CHUNK, D_SGU))],
        out_specs=[tile(D_IN), _full((N_GROUPS, GROUP, GROUP)), _row(D_POOL), _row(D_SGU), _row(D_SGU),
                   _full((HEADS, CHUNK, CHUNK)), _full((CHUNK, D_SGU))],
        out_shape=[jax.ShapeDtypeStruct((S, D_IN), MXU_DTYPE), jax.ShapeDtypeStruct((N_GROUPS, GROUP, GROUP), F32),
                   jax.ShapeDtypeStruct((1, D_POOL), F32), jax.ShapeDtypeStruct((1, D_SGU), F32),
                   jax.ShapeDtypeStruct((1, D_SGU), F32), jax.ShapeDtypeStruct((HEADS, CHUNK, CHUNK), F32),
                   jax.ShapeDtypeStruct((CHUNK, D_SGU), F32)],
        scratch_shapes=[pltpu.VMEM((ts, D_SGU), F32), pltpu.VMEM((ts, D_SGU), F32)],
        compiler_params=_params(1),
    )(df1, df1, w_out_all, proj, proj, pool_w, pool_scale, sg, sb, wm, bias_full)


def _weight_grad(a, b, stacked, layer, col0, tn, n_total, name):
    S, M = a.shape
    N = b.shape[1]
    ts = min(TS_TN, S)
    col_block0 = col0 // tn

    def body(*refs):
        a_ref, b_ref, o_ref = refs[0], refs[1], refs[-1]
        i = pl.program_id(1)
        _accumulate(i, o_ref, _dot_tn(a_ref[...], b_ref[...]))

    in_specs = [pl.BlockSpec((ts, M), lambda j, i: (i, 0)), pl.BlockSpec((ts, tn), lambda j, i: (i, j))]
    args = [a, b]
    aliases = {}
    if stacked is not None:
        in_specs.append(ANY)
        args.append(stacked)
        aliases = {2: 0}
    return pl.pallas_call(
        body, name=name, grid=(N // tn, S // ts),
        in_specs=in_specs,
        out_specs=pl.BlockSpec((None, M, tn), lambda j, i: (layer, 0, col_block0 + j)),
        out_shape=jax.ShapeDtypeStruct((DEPTH, M, n_total), F32),
        input_output_aliases=aliases,
        compiler_params=_params(2),
    )(*args)


def _silu(x):
    return x * (1.0 / (1.0 + jnp.exp(-x)))


def _ada_forward(c_all, ada_w, ada_b_cols):
    n_cols = ada_w.shape[2]
    tc = 512

    def body(c_ref, w_ref, b_ref, o_ref):
        ca = _silu(c_ref[...]).astype(MXU_DTYPE)
        o_ref[...] = _dot(ca, w_ref[...].astype(MXU_DTYPE)) + b_ref[...]

    return pl.pallas_call(
        body, name="ada_forward", grid=(DEPTH, n_cols // tc),
        in_specs=[pl.BlockSpec((16, D), lambda l, j: (0, 0)), pl.BlockSpec((None, D, tc), lambda l, j: (l, 0, j)),
                  pl.BlockSpec((None, 1, tc), lambda l, j: (l, 0, j))],
        out_specs=pl.BlockSpec((None, 16, tc), lambda l, j: (l, 0, j)),
        out_shape=jax.ShapeDtypeStruct((DEPTH, 16, n_cols), F32),
        compiler_params=_params(2),
    )(c_all, ada_w, ada_b_cols)


def _ada_backward(c_all, dmod_cols):
    n_cols = dmod_cols.shape[2]
    tc = 512

    def body(c_ref, d_ref, o_ref):
        ca = _silu(c_ref[...]).astype(MXU_DTYPE)
        o_ref[...] = _dot_tn(ca, d_ref[...].astype(MXU_DTYPE))

    return pl.pallas_call(
        body, name="ada_backward", grid=(DEPTH, n_cols // tc),
        in_specs=[pl.BlockSpec((16, D), lambda l, j: (0, 0)), pl.BlockSpec((None, 16, tc), lambda l, j: (l, 0, j))],
        out_specs=pl.BlockSpec((None, D, tc), lambda l, j: (l, 0, j)),
        out_shape=jax.ShapeDtypeStruct((DEPTH, D, n_cols), F32),
        compiler_params=_params(2),
    )(c_all, dmod_cols)


def _adamw(w, g, m, v, name):
    R, C = w.shape
    tr = R
    for cand in (512, 256, 128, 64, 32, 16, 8):
        if R % cand == 0 and cand * C * 4 <= 2 ** 21:
            tr = cand
            break
    c1 = 1.0 - ADAM_B1 ** ADAM_STEP
    c2 = 1.0 - ADAM_B2 ** ADAM_STEP

    def body(w_ref, g_ref, m_ref, v_ref, d_ref, mo_ref, vo_ref):
        gg = g_ref[...]
        mn = ADAM_B1 * m_ref[...] + (1.0 - ADAM_B1) * gg
        vn = ADAM_B2 * v_ref[...] + (1.0 - ADAM_B2) * (gg * gg)
        mo_ref[...] = mn
        vo_ref[...] = vn
        d_ref[...] = -ADAM_LR * ((mn / c1) / (jnp.sqrt(vn / c2) + ADAM_EPS) + ADAM_WD * w_ref[...])

    tile = pl.BlockSpec((tr, C), lambda i: (i, 0))
    return pl.pallas_call(
        body, name=name, grid=(R // tr,), in_specs=[tile] * 4, out_specs=[tile] * 3,
        out_shape=[jax.ShapeDtypeStruct((R, C), F32)] * 3, compiler_params=_params(1),
    )(w, g, m, v)


def _position():
    x, y, c = lax.axis_index("x"), lax.axis_index("y"), lax.axis_index("c")
    other_chips = [(1 - x, y), (x, 1 - y), (1 - x, 1 - y)]
    return x, y, c, other_chips


def _all_gather8(block, name):
    R, C = block.shape

    def body(x_ref, out_ref, send_sems, recv_sems, local_sem):
        x, y, c, chips = _position()
        me, sibling = (x, y, c), (x, y, 1 - c)

        def rows(px, py, pc):
            return out_ref.at[pl.ds((4 * px + 2 * py + pc) * R, R), :]

        def copy(k, blk, to, src=None):
            return pltpu.make_async_remote_copy(
                src_ref=rows(*blk) if src is None else src, dst_ref=rows(*blk),
                send_sem=send_sems.at[k], recv_sem=recv_sems.at[k], device_id=to, device_id_type=MESH)

        mine = pltpu.make_async_copy(x_ref, rows(*me), local_sem)
        mine.start()
        first = [copy(0, me, sibling, src=x_ref)]
        first += [copy(1 + j, me, (*chip, c), src=x_ref) for j, chip in enumerate(chips)]
        for cp in first:
            cp.start()
        passed = [copy(4 + j, (*chip, c), sibling) for j, chip in enumerate(chips)]
        for j, chip in enumerate(chips):
            copy(1 + j, (*chip, c), me).wait_recv()
            passed[j].start()
        copy(0, sibling, me).wait_recv()
        for j, chip in enumerate(chips):
            copy(4 + j, (*chip, 1 - c), me).wait_recv()
        for cp in first + passed:
            cp.wait_send()
        mine.wait()

    return pl.pallas_call(
        body, name=name, out_shape=jax.ShapeDtypeStruct((N_DEV * R, C), block.dtype),
        in_specs=[ANY], out_specs=ANY,
        scratch_shapes=[pltpu.SemaphoreType.DMA((7,)), pltpu.SemaphoreType.DMA((7,)), pltpu.SemaphoreType.DMA(())],
    )(block)


def _gather_flat(vec, name):
    n = vec.shape[0]
    padded = -(-n // 1024) * 1024
    block = jnp.pad(vec, (0, padded - n)).reshape(8, padded // 8)
    out = _all_gather8(block, name)
    return out.reshape(N_DEV, padded)[:, :n]


def _shard_window(ref, kind, chip, layers):
    n = ref.shape[2] // N_CHIPS if kind == "cols" else ref.shape[1] // N_CHIPS
    if kind == "cols":
        return ref.at[pl.ds(layers, 2), :, pl.ds(chip * n, n)]
    return ref.at[pl.ds(layers, 2), pl.ds(chip * n, n), :]


_SHARD_KINDS = ("cols", "rows", "cols", "rows")


def _gather_weights(shards):
    n_w = len(shards)
    full_shapes = []
    for w, kind in zip(shards, _SHARD_KINDS):
        L, r, cdim = w.shape
        full_shapes.append((L, r, cdim * N_CHIPS) if kind == "cols" else (L, r * N_CHIPS, cdim))

    def body(*refs):
        srcs, outs = refs[:n_w], refs[n_w:2 * n_w]
        send_sems, recv_sems, local_sems = refs[2 * n_w:]
        x, y, c, chips = _position()
        my_chip = 2 * x + y
        sibling = (x, y, 1 - c)
        mine, theirs = 2 * c, 2 * (1 - c)

        def copy(kind_idx, a, src, dst, to):
            k = kind_idx * n_w + a
            return pltpu.make_async_remote_copy(src_ref=src, dst_ref=dst, send_sem=send_sems.at[k],
                                                recv_sem=recv_sems.at[k], device_id=to, device_id_type=MESH)

        def win(a, chip, layers):
            return _shard_window(outs[a], _SHARD_KINDS[a], chip, layers)

        started, local = [], []
        for a in range(n_w):
            own = srcs[a].at[pl.ds(mine, 2)]
            lc = pltpu.make_async_copy(own, win(a, my_chip, mine), local_sems.at[a])
            lc.start()
            local.append(lc)
            started.append(copy(0, a, own, win(a, my_chip, mine), sibling))
            for j, chip in enumerate(chips):
                started.append(copy(1 + j, a, own, win(a, my_chip, mine), (*chip, c)))
        for cp in started:
            cp.start()
        passed = []
        for j, (cx, cy) in enumerate(chips):
            for a in range(n_w):
                w = win(a, 2 * cx + cy, mine)
                copy(1 + j, a, w, w, (x, y, c)).wait_recv()
                fwd = copy(4 + j, a, w, w, sibling)
                fwd.start()
                passed.append(fwd)
        for a in range(n_w):
            w = win(a, my_chip, theirs)
            copy(0, a, w, w, (x, y, c)).wait_recv()
            for j, (cx, cy) in enumerate(chips):
                w = win(a, 2 * cx + cy, theirs)
                copy(4 + j, a, w, w, (x, y, c)).wait_recv()
        for cp in started + passed:
            cp.wait_send()
        for lc in local:
            lc.wait()

    return pl.pallas_call(
        body, name="gather_weights",
        out_shape=[jax.ShapeDtypeStruct(s, w.dtype) for s, w in zip(full_shapes, shards)],
        in_specs=[ANY] * n_w, out_specs=[ANY] * n_w,
        scratch_shapes=[pltpu.SemaphoreType.DMA((7 * n_w,)), pltpu.SemaphoreType.DMA((7 * n_w,)),
                        pltpu.SemaphoreType.DMA((n_w,))],
    )(*shards)


def _swap_halves(grads):
    n_w = len(grads)

    def body(*refs):
        srcs, outs = refs[:n_w], refs[n_w:2 * n_w]
        send_sems, recv_sems = refs[2 * n_w:]
        x, y, c, _ = _position()
        copies = [pltpu.make_async_remote_copy(
            src_ref=srcs[a].at[pl.ds(2 * (1 - c), 2)], dst_ref=outs[a], send_sem=send_sems.at[a],
            recv_sem=recv_sems.at[a], device_id=(x, y, 1 - c), device_id_type=MESH) for a in range(n_w)]
        for cp in copies:
            cp.start()
        for cp in copies:
            cp.wait()

    return pl.pallas_call(
        body, name="swap_halves",
        out_shape=[jax.ShapeDtypeStruct((2,) + g.shape[1:], g.dtype) for g in grads],
        in_specs=[ANY] * n_w, out_specs=[ANY] * n_w,
        scratch_shapes=[pltpu.SemaphoreType.DMA((n_w,)), pltpu.SemaphoreType.DMA((n_w,))],
    )(*grads)


def _scatter_partials(partials):
    n_w = len(partials)
    out_shapes = []
    for p, kind in zip(partials, _SHARD_KINDS):
        _, r, cdim = p.shape
        out_shapes.append((3, 2, r, cdim // N_CHIPS) if kind == "cols" else (3, 2, r // N_CHIPS, cdim))

    def body(*refs):
        srcs, outs = refs[:n_w], refs[n_w:2 * n_w]
        send_sems, recv_sems = refs[2 * n_w:]
        x, y, c, chips = _position()
        copies = []
        for j, (cx, cy) in enumerate(chips):
            for a in range(n_w):
                k = j * n_w + a
                copies.append(pltpu.make_async_remote_copy(
                    src_ref=_shard_window(srcs[a], _SHARD_KINDS[a], 2 * cx + cy, 0), dst_ref=outs[a].at[j],
                    send_sem=send_sems.at[k], recv_sem=recv_sems.at[k], device_id=(cx, cy, c), device_id_type=MESH))
        for cp in copies:
            cp.start()
        for cp in copies:
            cp.wait()

    return pl.pallas_call(
        body, name="scatter_partials",
        out_shape=[jax.ShapeDtypeStruct(s, p.dtype) for s, p in zip(out_shapes, partials)],
        in_specs=[ANY] * n_w, out_specs=[ANY] * n_w,
        scratch_shapes=[pltpu.SemaphoreType.DMA((3 * n_w,)), pltpu.SemaphoreType.DMA((3 * n_w,))],
    )(*partials)


def _share_reduced(halves):
    n_w = len(halves)

    def body(*refs):
        srcs, outs = refs[:n_w], refs[n_w:2 * n_w]
        send_sems, recv_sems, local_sems = refs[2 * n_w:]
        x, y, c, _ = _position()
        local, copies = [], []
        for a in range(n_w):
            dst = outs[a].at[pl.ds(2 * c, 2)]
            lc = pltpu.make_async_copy(srcs[a], dst, local_sems.at[a])
            lc.start()
            local.append(lc)
            copies.append(pltpu.make_async_remote_copy(
                src_ref=srcs[a], dst_ref=dst, send_sem=send_sems.at[a], recv_sem=recv_sems.at[a],
                device_id=(x, y, 1 - c), device_id_type=MESH))
        for cp in copies:
            cp.start()
        for a in range(n_w):
            theirs = outs[a].at[pl.ds(2 * (1 - c), 2)]
            pltpu.make_async_remote_copy(src_ref=srcs[a], dst_ref=theirs, send_sem=send_sems.at[a],
                                         recv_sem=recv_sems.at[a], device_id=(x, y, c),
                                         device_id_type=MESH).wait_recv()
        for cp in copies:
            cp.wait_send()
        for lc in local:
            lc.wait()

    return pl.pallas_call(
        body, name="share_reduced",
        out_shape=[jax.ShapeDtypeStruct((DEPTH,) + h.shape[1:], h.dtype) for h in halves],
        in_specs=[ANY] * n_w, out_specs=[ANY] * n_w,
        scratch_shapes=[pltpu.SemaphoreType.DMA((n_w,)), pltpu.SemaphoreType.DMA((n_w,)),
                        pltpu.SemaphoreType.DMA((n_w,))],
    )(*halves)


def _chip_partial(pos, grad, theirs, name):
    _, M, N = grad.shape
    tm = 256

    def body(pos_ref, g_ref, t_ref, o_ref):
        o_ref[...] = (g_ref[...] + t_ref[...]).astype(WIRE_DTYPE)

    grid_spec = pltpu.PrefetchScalarGridSpec(
        num_scalar_prefetch=1, grid=(2, M // tm),
        in_specs=[pl.BlockSpec((None, tm, N), lambda k, i, pos: (2 * pos[0] + k, i, 0)),
                  pl.BlockSpec((None, tm, N), lambda k, i, pos: (k, i, 0))],
        out_specs=pl.BlockSpec((None, tm, N), lambda k, i, pos: (k, i, 0)))
    return pl.pallas_call(
        body, name=name, grid_spec=grid_spec, out_shape=jax.ShapeDtypeStruct((2, M, N), WIRE_DTYPE),
        compiler_params=_params(2),
    )(pos, grad, theirs)


def _reduce_shard(pos, grad, theirs, received, kind, name):
    _, M, N = grad.shape
    if kind == "cols":
        n_shard = N // N_CHIPS
        tm = min(M, 512)
        block = (None, tm, n_shard)
        grid = (2, M // tm)
        g_map = lambda k, i, pos: (2 * pos[0] + k, i, pos[1])
        t_map = lambda k, i, pos: (k, i, pos[1])
        out_shape = (2, M, n_shard)
    else:
        m_shard = M // N_CHIPS
        block = (None, m_shard, N)
        grid = (2, 1)
        g_map = lambda k, i, pos: (2 * pos[0] + k, pos[1], 0)
        t_map = lambda k, i, pos: (k, pos[1], 0)
        out_shape = (2, m_shard, N)
    r_block = (None, None) + block[1:]

    def body(pos_ref, g_ref, t_ref, r0_ref, r1_ref, r2_ref, o_ref):
        chip = pos_ref[1]
        own = g_ref[...] + t_ref[...]
        r = [r0_ref[...].astype(F32), r1_ref[...].astype(F32), r2_ref[...].astype(F32)]
        total = None
        for s in range(N_CHIPS):
            rel = jnp.bitwise_xor(chip, s)
            term = jnp.where(rel == 0, own, jnp.where(rel == 2, r[0], jnp.where(rel == 1, r[1], r[2])))
            total = term if total is None else total + term
        o_ref[...] = total

    r_spec = lambda j: pl.BlockSpec(r_block, lambda k, i, pos: (j, k, i, 0))
    grid_spec = pltpu.PrefetchScalarGridSpec(
        num_scalar_prefetch=1, grid=grid,
        in_specs=[pl.BlockSpec(block, g_map), pl.BlockSpec(block, t_map), r_spec(0), r_spec(1), r_spec(2)],
        out_specs=pl.BlockSpec(block, lambda k, i, pos: (k, i, 0)))
    return pl.pallas_call(
        body, name=name, grid_spec=grid_spec, out_shape=jax.ShapeDtypeStruct(out_shape, F32),
        compiler_params=_params(2),
    )(pos, grad, theirs, received, received, received)


def _sum_devices(gathered):
    R8, C = gathered.shape
    R = R8 // N_DEV
    tc = C
    for cand in (4096, 2048, 1024, 512, 256, 128):
        if C % cand == 0:
            tc = cand
            break

    def body(g_ref, o_ref):
        total = g_ref[0:R, :]
        for d in range(1, N_DEV):
            total = total + g_ref[d * R:(d + 1) * R, :]
        o_ref[...] = total

    return pl.pallas_call(
        body, name="sum_devices", grid=(C // tc,),
        in_specs=[pl.BlockSpec((R8, tc), lambda j: (0, j))], out_specs=pl.BlockSpec((R, tc), lambda j: (0, j)),
        out_shape=jax.ShapeDtypeStruct((R, C), F32), compiler_params=_params(1),
    )(gathered)


_SMALL = ("pool_w", "pool_scale", "sgu_ln_g", "sgu_ln_b", "sgu_w", "sgu_b", "ln1_g", "ln1_b", "conv_w",
          "conv_b", "ln2_g", "ln2_b", "ada_b")
_WEIGHTS = ("ada_w", "ada_b", "w_in", "pool_w", "pool_scale", "sgu_ln_g", "sgu_ln_b", "sgu_w", "sgu_b", "w_out",
            "ln1_g", "ln1_b", "w_up", "conv_w", "conv_b", "w_down", "ln2_g", "ln2_b")


def kernel(x, c, ada_w, ada_b, w_in, pool_w, pool_scale, sgu_ln_g, sgu_ln_b, sgu_w, sgu_b, w_out, ln1_g, ln1_b, w_up, conv_w, conv_b, w_down, ln2_g, ln2_b, loss_target, m_ada_w, m_ada_b, m_w_in, m_pool_w, m_pool_scale, m_sgu_ln_g, m_sgu_ln_b, m_sgu_w, m_sgu_b, m_w_out, m_ln1_g, m_ln1_b, m_w_up, m_conv_w, m_conv_b, m_w_down, m_ln2_g, m_ln2_b, v_ada_w, v_ada_b, v_w_in, v_pool_w, v_pool_scale, v_sgu_ln_g, v_sgu_ln_b, v_sgu_w, v_sgu_b, v_w_out, v_ln1_g, v_ln1_b, v_w_up, v_conv_w, v_conv_b, v_w_down, v_ln2_g, v_ln2_b):
    weights = dict(ada_w=ada_w, ada_b=ada_b, w_in=w_in, pool_w=pool_w, pool_scale=pool_scale, sgu_ln_g=sgu_ln_g,
                   sgu_ln_b=sgu_ln_b, sgu_w=sgu_w, sgu_b=sgu_b, w_out=w_out, ln1_g=ln1_g, ln1_b=ln1_b, w_up=w_up,
                   conv_w=conv_w, conv_b=conv_b, w_down=w_down, ln2_g=ln2_g, ln2_b=ln2_b)
    mom_m = dict(ada_w=m_ada_w, ada_b=m_ada_b, w_in=m_w_in, pool_w=m_pool_w, pool_scale=m_pool_scale,
                 sgu_ln_g=m_sgu_ln_g, sgu_ln_b=m_sgu_ln_b, sgu_w=m_sgu_w, sgu_b=m_sgu_b, w_out=m_w_out,
                 ln1_g=m_ln1_g, ln1_b=m_ln1_b, w_up=m_w_up, conv_w=m_conv_w, conv_b=m_conv_b, w_down=m_w_down,
                 ln2_g=m_ln2_g, ln2_b=m_ln2_b)
    mom_v = dict(ada_w=v_ada_w, ada_b=v_ada_b, w_in=v_w_in, pool_w=v_pool_w, pool_scale=v_pool_scale,
                 sgu_ln_g=v_sgu_ln_g, sgu_ln_b=v_sgu_ln_b, sgu_w=v_sgu_w, sgu_b=v_sgu_b, w_out=v_w_out,
                 ln1_g=v_ln1_g, ln1_b=v_ln1_b, w_up=v_w_up, conv_w=v_conv_w, conv_b=v_conv_b, w_down=v_w_down,
                 ln2_g=v_ln2_g, ln2_b=v_ln2_b)

    ix, iy, ic = lax.axis_index("x"), lax.axis_index("y"), lax.axis_index("c")
    chip = 2 * ix + iy
    dev = 4 * ix + 2 * iy + ic
    pos = jnp.stack([ic, chip]).astype(jnp.int32)
    xs = x[0]
    target = loss_target[0]
    ff_shard = conv_w.shape[2]
    mod_shard = ada_w.shape[2]

    first = _gather_flat(jnp.concatenate([c.reshape(-1), conv_w.reshape(-1)]), "gather_c_conv")
    c_all = jnp.pad(first[:, :D], ((0, 8), (0, 0)))
    conv_parts = first[0::2, D:].reshape(N_CHIPS, DEPTH, 3, ff_shard)
    conv_full = jnp.transpose(conv_parts, (1, 2, 0, 3)).reshape(DEPTH, 3, D_FF)
    ada_b_cols = lax.dynamic_slice_in_dim(ada_b, chip * mod_shard, mod_shard, axis=1).reshape(DEPTH, 1, mod_shard)
    mod_part = _ada_forward(c_all, ada_w, ada_b_cols)[:, :8, :]
    mod_all = _gather_flat(mod_part.reshape(-1), "gather_mod").reshape(N_DEV, DEPTH, 8, mod_shard)
    mod_mine = lax.dynamic_index_in_dim(mod_all[0::2], dev, axis=2, keepdims=False)
    mod = jnp.transpose(mod_mine, (1, 0, 2)).reshape(DEPTH, 6, 1, D)

    wf_in, wf_out, wf_up, wf_down = _gather_weights(
        [w_in.astype(WIRE_DTYPE), w_out.astype(WIRE_DTYPE), w_up.astype(WIRE_DTYPE), w_down.astype(WIRE_DTYPE)])

    tri = jnp.tril(jnp.ones((CHUNK, CHUNK), dtype=bool))
    ones_row = jnp.ones((1, D), F32)
    zeros_row = jnp.zeros((1, D), F32)
    row = lambda a, l: a[l].reshape(1, -1)

    saved = []
    xn, lg, lb = xs, ones_row, zeros_row
    h = _modulate_input(xs, mod[0, 0], mod[0, 1])
    for l in range(DEPTH):
        shift1, scale1, gate1, shift2, scale2, gate2 = (mod[l, k] for k in range(6))
        nxt = min(l + 1, DEPTH - 1)
        pw = pool_w[l].astype(MXU_DTYPE)
        wm = jnp.where(tri[None], sgu_w[l], 0.0).astype(MXU_DTYPE)
        bias_full = jnp.repeat(jnp.transpose(sgu_b[l]), GROUP, axis=1)
        lg1, lb1 = row(ln1_g, l), row(ln1_b, l)
        proj = _matmul(h, wf_in, l, D_IN, STORE_DTYPE, f"f1_in_proj_l{l}")
        mix, f1, xhat1, rstd1, h2 = _f2_mixers(proj, xn, lg, lb, gate1, pw, row(pool_scale, l), row(sgu_ln_g, l),
                                               row(sgu_ln_b, l), wm, bias_full, wf_out, lg1, lb1, shift2, scale2, l)
        up = _matmul(h2, wf_up, l, D_FF, STORE_DTYPE, f"f3_up_proj_l{l}")
        act, f2, xhat2, rstd2, h_next = _f4_ffn(up, xhat1, lg1, lb1, gate2, conv_full[l], row(conv_b, l), wf_down,
                                                row(ln2_g, l), row(ln2_b, l), mod[nxt, 0], mod[nxt, 1], l)
        saved.append(dict(xn=xn, lg=lg, lb=lb, h=h, proj=proj, mix=mix, f1=f1, xhat1=xhat1, rstd1=rstd1, h2=h2,
                          up=up, act=act, f2=f2, xhat2=xhat2, rstd2=rstd2, pw=pw, wm=wm, bias_full=bias_full))
        xn, lg, lb, h = xhat2, row(ln2_g, l), row(ln2_b, l), h_next

    loss_part, d_out = _loss_head(xn, lg, lb, target)
    loss = lax.psum(loss_part[0, 0], ("x", "y", "c"))

    g_in = g_out = g_up = g_down = None
    small = {n: [None] * DEPTH for n in _SMALL}
    dmod = [[None] * 6 for _ in range(DEPTH)]
    below = None
    for l in reversed(range(DEPTH)):
        sv = saved[l]
        scale1, gate1, scale2, gate2 = mod[l, 1], mod[l, 2], mod[l, 4], mod[l, 5]
        lg1, lb1 = row(ln1_g, l), row(ln1_b, l)
        outs = _b_ln(d_out, sv["xhat2"], sv["rstd2"], sv["f2"], row(ln2_g, l), row(ln2_b, l), gate2,
                     f"b1_ln2_l{l}", modulated=below)
        df2, dres2, dlg2, dlb2, dmod[l][5] = outs[:5]
        if below is not None:
            dmod[l + 1][1], dmod[l + 1][0] = outs[5], outs[6]
        dup, dcw, dcb = _b2_ffn(df2, wf_down, sv["up"], conv_full[l], row(conv_b, l), l)
        g_down = _weight_grad(sv["act"], df2, g_down, l, 0, D, D, f"wg_down_l{l}")
        dh2 = _matmul_nt(dup, wf_up, l, 1408, f"b3_up_l{l}")
        g_up = _weight_grad(sv["h2"], dup, g_up, l, 0, 1408, 2 * D_FF, f"wg_up_l{l}")
        df1, dres1, dlg1, dlb1, dmod[l][2], dmod[l][4], dmod[l][3] = _b_ln(
            dh2, sv["xhat1"], sv["rstd1"], sv["f1"], lg1, lb1, gate1, f"b4_ln1_l{l}", modulated=(dres2, scale2))
        dproj, dpw, dps, dsg, dsb, dwm, dbias = _b5_mixers(
            df1, wf_out, sv["proj"], sv["pw"], row(pool_scale, l), row(sgu_ln_g, l), row(sgu_ln_b, l), sv["wm"],
            sv["bias_full"], l)
        g_out = _weight_grad(sv["mix"], df1, g_out, l, 0, D, D, f"wg_out_l{l}")
        d_out = _matmul_nt(dproj, wf_in, l, D_IN, f"b6_in_l{l}")
        g_in = _weight_grad(sv["h"], dproj, g_in, l, 0, 768, D_IN, f"wg_in_l{l}")
        below = (dres1, scale1)
        small["pool_w"][l], small["pool_scale"][l] = dpw, dps[0]
        small["sgu_ln_g"][l], small["sgu_ln_b"][l], small["sgu_w"][l] = dsg[0], dsb[0], dwm
        small["sgu_b"][l] = jnp.transpose(jnp.sum(dbias.reshape(CHUNK, HEADS, GROUP), axis=2))
        small["ln1_g"][l], small["ln1_b"][l] = dlg1[0], dlb1[0]
        small["conv_w"][l], small["conv_b"][l] = dcw, dcb[0]
        small["ln2_g"][l], small["ln2_b"][l] = dlg2[0], dlb2[0]
    grad_x2d, dmod[0][1], dmod[0][0] = _input_grad(d_out, below[0], below[1], xs)
    grad_x = grad_x2d[None]
    small["ada_b"] = [jnp.concatenate([part[0] for part in dmod[l]]) for l in range(DEPTH)]

    names = _SMALL
    flat = jnp.concatenate([jnp.stack(small[n]).reshape(-1) for n in names])
    n_small = flat.shape[0]
    padded = -(-n_small // 1024) * 1024
    gathered = _all_gather8(jnp.pad(flat, (0, padded - n_small)).reshape(8, padded // 8), "gather_small_grads")
    summed = _sum_devices(gathered).reshape(-1)[:n_small]
    grads = {}
    offset = 0
    for n in names:
        size = math.prod(weights[n].shape[1:]) * DEPTH if n != "conv_w" else DEPTH * 3 * D_FF
        grads[n] = summed[offset:offset + size]
        offset += size
    grads["conv_w"] = lax.dynamic_slice_in_dim(grads["conv_w"].reshape(DEPTH, 3, D_FF), chip * ff_shard, ff_shard,
                                               axis=2)
    for n in names:
        grads[n] = grads[n].reshape(weights[n].shape)
    dmod_all = gathered.reshape(N_DEV, padded)[:, n_small - DEPTH * 6 * D:n_small].reshape(N_DEV, DEPTH, 6 * D)
    dmod_cols = lax.dynamic_slice_in_dim(jnp.transpose(dmod_all, (1, 0, 2)), chip * mod_shard, mod_shard, axis=2)
    grads["ada_w"] = _ada_backward(c_all, jnp.pad(dmod_cols, ((0, 0), (0, 8), (0, 0))))

    full = [g_in, g_out, g_up, g_down]
    theirs = _swap_halves(full)
    tags = ("in", "out", "up", "down")
    partials = [_chip_partial(pos, g, t, f"chip_partial_{tag}") for g, t, tag in zip(full, theirs, tags)]
    received = _scatter_partials(partials)
    halves = [_reduce_shard(pos, g, t, r, kind, f"reduce_shard_{tag}")
              for g, t, r, kind, tag in zip(full, theirs, received, _SHARD_KINDS, tags)]
    grads["w_in"], grads["w_out"], grads["w_up"], grads["w_down"] = _share_reduced(halves)

    delta, new_m, new_v = {}, {}, {}
    for n in ("ada_w", "w_in", "w_out", "w_up", "w_down"):
        shape = weights[n].shape
        two_d = (shape[0] * shape[1], shape[2])
        d_, m_, v_ = _adamw(weights[n].reshape(two_d), grads[n].reshape(two_d), mom_m[n].reshape(two_d),
                            mom_v[n].reshape(two_d), f"adamw_{n}")
        delta[n], new_m[n], new_v[n] = d_.reshape(shape), m_.reshape(shape), v_.reshape(shape)
    sizes = [math.prod(weights[n].shape) for n in names]
    total = sum(sizes)
    padded = -(-total // 1024) * 1024
    pack = lambda d: jnp.pad(jnp.concatenate([d[n].reshape(-1) for n in names]), (0, padded - total)).reshape(8, -1)
    v_pack = jnp.pad(jnp.concatenate([mom_v[n].reshape(-1) for n in names]), (0, padded - total),
                     constant_values=1.0).reshape(8, -1)
    d_, m_, v_ = _adamw(pack(weights), pack(grads), pack(mom_m), v_pack, "adamw_small")
    offset = 0
    for n, size in zip(names, sizes):
        for src, dst in ((d_, delta), (m_, new_m), (v_, new_v)):
            dst[n] = src.reshape(-1)[offset:offset + size].reshape(weights[n].shape)
        offset += size

    return (loss, grad_x, *[grads[n] for n in _WEIGHTS], *[delta[n] for n in _WEIGHTS],
            *[new_m[n] for n in _WEIGHTS], *[new_v[n] for n in _WEIGHTS])
```

```python
import functools
import math

import jax
import jax.numpy as jnp
from jax import lax
from jax.experimental import pallas as pl
from jax.experimental.pallas import tpu as pltpu

F32 = jnp.float32
MXU_DTYPE = jnp.bfloat16
WIRE_DTYPE = jnp.bfloat16

DEPTH = 4
D = 1024
D_POOL = 512
D_SGU = 512
N_GROUPS = 4
GROUP = 128
POOL_WINDOWS = (2, 4, 8, 16)
POOL_HALO = 16
CHUNK = 128
HEADS = 4
D_IN = D_POOL + 2 * D_SGU
D_FF = 2816
CONV_HALO = 16
STORE_DTYPE = jnp.bfloat16
N_CHIPS = 4
N_DEV = 8
ALPHA = (2.0 * DEPTH) ** 0.25
LN_EPS = 1e-5
ADAM_LR, ADAM_B1, ADAM_B2, ADAM_EPS, ADAM_WD, ADAM_STEP = 0.001, 0.9, 0.999, 1e-08, 0.01, 10

TS_PROJ = 512
TS_MM = 1024
TS_MIX = 256
TS_FF = 256
TS_TN = 1024
VMEM_LIMIT = 52 * 2 ** 20

MESH = pl.DeviceIdType.MESH
ANY = pl.BlockSpec(memory_space=pl.ANY)

_GELU_K0 = math.sqrt(2.0 / math.pi)
_GELU_K1 = 0.044715


def _params(n_axes):
    return pltpu.CompilerParams(dimension_semantics=("arbitrary",) * n_axes, vmem_limit_bytes=VMEM_LIMIT)


def _dot(a, b):
    return jnp.dot(a, b, preferred_element_type=F32)


def _dot_nt(a, b):
    return lax.dot_general(a, b, (((1,), (1,)), ((), ())), preferred_element_type=F32)


def _dot_tn(a, b):
    return lax.dot_general(a, b, (((0,), (0,)), ((), ())), preferred_element_type=F32)


def _gelu(x):
    x2 = x * x
    t = jnp.tanh(_GELU_K0 * (x + _GELU_K1 * (x2 * x)))
    cdf = 0.5 * (1.0 + t)
    dg = cdf + x * (0.5 * (1.0 - t * t)) * (_GELU_K0 * (1.0 + 3.0 * _GELU_K1 * x2))
    return x * cdf, dg


def _colsum(x):
    return jnp.sum(x, axis=0, keepdims=True)


def _row(d):
    return pl.BlockSpec((1, d), lambda *_: (0, 0))


def _full(shape):
    n = len(shape)
    return pl.BlockSpec(shape, lambda *_: (0,) * n)


def _layer_block(shape, layer):
    n = len(shape)
    return pl.BlockSpec((None,) + tuple(shape), lambda *_: (layer,) + (0,) * n)


def _pool_counts(first_row, rows, window):
    t = first_row + lax.broadcasted_iota(jnp.int32, (rows, 1), 0)
    return 1.0 / jnp.minimum(t + 1, window).astype(F32)


def _pool_forward(a_ext, a, first_row, rows):
    out = []
    for g, window in enumerate(POOL_WINDOWS):
        s = a_ext[:, g * GROUP:(g + 1) * GROUP]
        k = 1
        while k < window:
            s = s + pltpu.roll(s, k, 0)
            k *= 2
        inv = _pool_counts(first_row, rows, window)
        out.append(s[POOL_HALO:] * inv - a[:, g * GROUP:(g + 1) * GROUP])
    return out


def _sgu_norm(v):
    gv, dgv = _gelu(v)
    mu = jnp.mean(gv, axis=-1, keepdims=True)
    xc = gv - mu
    var = jnp.mean(xc * xc, axis=-1, keepdims=True)
    rs = lax.rsqrt(var + LN_EPS)
    return xc * rs, rs, dgv


def _modulate_input(x, shift, scale):
    S = x.shape[0]
    ts = min(TS_PROJ, S)

    def body(x_ref, sh_ref, sc_ref, h_ref):
        h_ref[...] = (x_ref[...] * (1.0 + sc_ref[...]) + sh_ref[...]).astype(MXU_DTYPE)

    tile = pl.BlockSpec((ts, D), lambda i: (i, 0))
    return pl.pallas_call(
        body, name="modulate_input", grid=(S // ts,), in_specs=[tile, _row(D), _row(D)], out_specs=tile,
        out_shape=jax.ShapeDtypeStruct((S, D), MXU_DTYPE), compiler_params=_params(1),
    )(x, shift, scale)


def _matmul(a, w_all, layer, tn, out_dtype, name):
    S, K = a.shape
    N = w_all.shape[2]
    ts = min(TS_MM, S)

    def body(a_ref, w_ref, o_ref):
        o_ref[...] = _dot(a_ref[...], w_ref[...]).astype(out_dtype)

    return pl.pallas_call(
        body, name=name, grid=(N // tn, S // ts),
        in_specs=[pl.BlockSpec((ts, K), lambda j, i: (i, 0)), pl.BlockSpec((None, K, tn), lambda j, i: (layer, 0, j))],
        out_specs=pl.BlockSpec((ts, tn), lambda j, i: (i, j)),
        out_shape=jax.ShapeDtypeStruct((S, N), out_dtype), compiler_params=_params(2),
    )(a, w_all)


def _matmul_nt(a, w_all, layer, tk, name):
    S, K = a.shape
    N = w_all.shape[1]
    ts = min(TS_MM, S)
    n_k = K // tk

    def body(a_ref, w_ref, o_ref, acc):
        k = pl.program_id(1)
        part = _dot_nt(a_ref[...], w_ref[...])
        if n_k == 1:
            o_ref[...] = part
        else:
            @pl.when(k == 0)
            def _():
                acc[...] = part

            @pl.when(jnp.logical_and(k > 0, k < n_k - 1))
            def _():
                acc[...] += part

            @pl.when(k == n_k - 1)
            def _():
                o_ref[...] = acc[...] + part

    return pl.pallas_call(
        body, name=name, grid=(S // ts, n_k),
        in_specs=[pl.BlockSpec((ts, tk), lambda i, k: (i, k)), pl.BlockSpec((None, N, tk), lambda i, k: (layer, 0, k))],
        out_specs=pl.BlockSpec((ts, N), lambda i, k: (i, 0)),
        out_shape=jax.ShapeDtypeStruct((S, N), F32),
        scratch_shapes=[pltpu.VMEM((ts, N), F32)], compiler_params=_params(2),
    )(a, w_all)


def _f2_mixers(proj, xn, lg, lb, gate, pool_w, pool_scale, sg, sb, wm, bias_full, w_out_all, lg1, lb1, shift2, scale2,
               layer):
    S = xn.shape[0]
    ts = min(TS_MIX, S)
    n_chunks = ts // CHUNK
    halo_blocks = ts // POOL_HALO

    def body(p_ref, halo_ref, xn_ref, lg_ref, lb_ref, gate_ref, pw_ref, ps_ref, sg_ref, sb_ref, wm_ref,
             bias_ref, wo_ref, lg1_ref, lb1_ref, sh2_ref, sc2_ref, mix_ref, f_ref, xh_ref, rs_ref, h2_ref, z_scr):
        i = pl.program_id(0)
        a = p_ref[:, 0:D_POOL].astype(F32)
        u = p_ref[:, D_POOL:D_POOL + D_SGU].astype(F32)
        v = p_ref[:, D_POOL + D_SGU:D_IN].astype(F32)
        halo = halo_ref[...].astype(F32) * (i > 0).astype(F32)
        a_ext = jnp.concatenate([halo, a], axis=0)
        pooled = _pool_forward(a_ext, a, i * ts, ts)
        for g in range(N_GROUPS):
            mixed = _dot(pooled[g].astype(MXU_DTYPE), pw_ref[g])
            mix_ref[:, g * GROUP:(g + 1) * GROUP] = (mixed * ps_ref[:, g * GROUP:(g + 1) * GROUP]).astype(MXU_DTYPE)
        gu, _ = _gelu(u)
        vhat, _, _ = _sgu_norm(v)
        vn = (vhat * sg_ref[...] + sb_ref[...]).astype(MXU_DTYPE)
        for c in range(n_chunks):
            for h in range(HEADS):
                blk = vn[c * CHUNK:(c + 1) * CHUNK, h * GROUP:(h + 1) * GROUP]
                z_scr[c * CHUNK:(c + 1) * CHUNK, h * GROUP:(h + 1) * GROUP] = (
                    _dot(wm_ref[h], blk) + bias_ref[:, h * GROUP:(h + 1) * GROUP])
        mix_ref[:, D_POOL:D] = (gu * z_scr[...]).astype(MXU_DTYPE)
        f = _dot(mix_ref[...], wo_ref[...])
        f_ref[...] = f.astype(STORE_DTYPE)
        x = xn_ref[...] * lg_ref[...] + lb_ref[...]
        z1 = ALPHA * x + gate_ref[...] * f
        mu = jnp.mean(z1, axis=-1, keepdims=True)
        zc = z1 - mu
        var = jnp.mean(zc * zc, axis=-1, keepdims=True)
        rs = lax.rsqrt(var + LN_EPS)
        xhat = zc * rs
        xh_ref[...] = xhat
        rs_ref[...] = rs
        x1 = xhat * lg1_ref[...] + lb1_ref[...]
        h2_ref[...] = (x1 * (1.0 + sc2_ref[...]) + sh2_ref[...]).astype(MXU_DTYPE)

    tile = lambda w: pl.BlockSpec((ts, w), lambda i: (i, 0))
    return pl.pallas_call(
        body, name=f"f2_mixers_l{layer}", grid=(S // ts,),
        in_specs=[tile(D_IN),
                  pl.BlockSpec((POOL_HALO, D_POOL), lambda i: (jnp.maximum(i * halo_blocks - 1, 0), 0)),
                  tile(D), _row(D), _row(D), _row(D),
                  _full((N_GROUPS, GROUP, GROUP)), _row(D_POOL), _row(D_SGU), _row(D_SGU),
                  _full((HEADS, CHUNK, CHUNK)), _full((CHUNK, D_SGU)), _layer_block((D, D), layer),
                  _row(D), _row(D), _row(D), _row(D)],
        out_specs=[tile(D), tile(D), tile(D), tile(1), tile(D)],
        out_shape=[jax.ShapeDtypeStruct((S, D), MXU_DTYPE), jax.ShapeDtypeStruct((S, D), STORE_DTYPE),
                   jax.ShapeDtypeStruct((S, D), F32), jax.ShapeDtypeStruct((S, 1), F32),
                   jax.ShapeDtypeStruct((S, D), MXU_DTYPE)],
        scratch_shapes=[pltpu.VMEM((ts, D_SGU), F32)],
        compiler_params=_params(1),
    )(proj, proj, xn, lg, lb, gate, pool_w, pool_scale, sg, sb, wm, bias_full, w_out_all, lg1, lb1, shift2, scale2)


def _conv_forward(g_ext, cw_ref, cb_ref):
    gm2 = pltpu.roll(g_ext, 2, 0)[CONV_HALO:]
    gm1 = pltpu.roll(g_ext, 1, 0)[CONV_HALO:]
    g0 = g_ext[CONV_HALO:]
    gc = ((cb_ref[...] + gm2 * cw_ref[0:1, :]) + gm1 * cw_ref[1:2, :]) + g0 * cw_ref[2:3, :]
    return gc, gm2, gm1


def _f4_ffn(up, xhat1, lg, lb, gate, conv_w, conv_b, w_down_all, lg2, lb2, shift_next, scale_next, layer):
    S = xhat1.shape[0]
    ts = min(TS_FF, S)
    halo_blocks = ts // CONV_HALO

    def body(g_ref, halo_ref, val_ref, xh_ref, lg_ref, lb_ref, gate_ref, cw_ref, cb_ref, wd_ref, lg2_ref, lb2_ref,
             shn_ref, scn_ref, act_ref, f_ref, xo_ref, rs_ref, hn_ref):
        i = pl.program_id(0)
        halo = halo_ref[...].astype(F32) * (i > 0).astype(F32)
        g_ext = jnp.concatenate([halo, g_ref[...].astype(F32)], axis=0)
        gc, _, _ = _conv_forward(g_ext, cw_ref, cb_ref)
        ge, _ = _gelu(gc)
        act = (ge * val_ref[...].astype(F32)).astype(MXU_DTYPE)
        act_ref[...] = act
        f = _dot(act, wd_ref[...])
        f_ref[...] = f.astype(STORE_DTYPE)
        x = xh_ref[...] * lg_ref[...] + lb_ref[...]
        z = ALPHA * x + gate_ref[...] * f
        mu = jnp.mean(z, axis=-1, keepdims=True)
        zc = z - mu
        var = jnp.mean(zc * zc, axis=-1, keepdims=True)
        rs = lax.rsqrt(var + LN_EPS)
        xhat = zc * rs
        xo_ref[...] = xhat
        rs_ref[...] = rs
        x2 = xhat * lg2_ref[...] + lb2_ref[...]
        hn_ref[...] = (x2 * (1.0 + scn_ref[...]) + shn_ref[...]).astype(MXU_DTYPE)

    tile = lambda w: pl.BlockSpec((ts, w), lambda i: (i, 0))
    return pl.pallas_call(
        body, name=f"f4_ffn_l{layer}", grid=(S // ts,),
        in_specs=[pl.BlockSpec((ts, D_FF), lambda i: (i, 0)),
                  pl.BlockSpec((CONV_HALO, D_FF), lambda i: (jnp.maximum(i * halo_blocks - 1, 0), 0)),
                  pl.BlockSpec((ts, D_FF), lambda i: (i, 1)),
                  tile(D), _row(D), _row(D), _row(D), _full((3, D_FF)), _row(D_FF),
                  _layer_block((D_FF, D), layer), _row(D), _row(D), _row(D), _row(D)],
        out_specs=[tile(D_FF), tile(D), tile(D), tile(1), tile(D)],
        out_shape=[jax.ShapeDtypeStruct((S, D_FF), MXU_DTYPE), jax.ShapeDtypeStruct((S, D), STORE_DTYPE),
                   jax.ShapeDtypeStruct((S, D), F32), jax.ShapeDtypeStruct((S, 1), F32),
                   jax.ShapeDtypeStruct((S, D), MXU_DTYPE)],
        compiler_params=_params(1),
    )(up, up, up, xhat1, lg, lb, gate, conv_w, conv_b, w_down_all, lg2, lb2, shift_next, scale_next)


def _loss_head(xhat, lg, lb, target):
    S = xhat.shape[0]
    ts = min(TS_PROJ, S)

    def body(xh_ref, lg_ref, lb_ref, t_ref, loss_ref, dy_ref, acc):
        i = pl.program_id(0)

        @pl.when(i == 0)
        def _():
            acc[...] = jnp.zeros_like(acc)

        err = (xh_ref[...] * lg_ref[...] + lb_ref[...]) - t_ref[...]
        dy_ref[...] = err * (1.0 / D)
        acc[...] += _colsum(err * err)

        @pl.when(i == pl.num_programs(0) - 1)
        def _():
            loss_ref[...] = jnp.sum(acc[...], axis=1, keepdims=True) * (0.5 / D)

    tile = pl.BlockSpec((ts, D), lambda i: (i, 0))
    return pl.pallas_call(
        body, name="loss_head", grid=(S // ts,),
        in_specs=[tile, _row(D), _row(D), tile],
        out_specs=[_full((1, 1)), tile],
        out_shape=[jax.ShapeDtypeStruct((1, 1), F32), jax.ShapeDtypeStruct((S, D), F32)],
        scratch_shapes=[pltpu.VMEM((1, D), F32)],
        compiler_params=_params(1),
    )(xhat, lg, lb, target)


def _accumulate(i, ref, value):
    @pl.when(i == 0)
    def _():
        ref[...] = value

    @pl.when(i > 0)
    def _():
        ref[...] += value


def _b_ln(d, xhat, rstd, f, lg, lb, gate, name, modulated=None):
    S = d.shape[0]
    ts = min(TS_PROJ, S)
    has_mod = modulated is not None

    def body(*refs):
        if has_mod:
            (d_ref, xh_ref, rs_ref, f_ref, lg_ref, lb_ref, gate_ref, dri_ref, sc_ref,
             df_ref, dres_ref, dlg_ref, dlb_ref, dgate_ref, dsc_ref, dsh_ref) = refs
        else:
            (d_ref, xh_ref, rs_ref, f_ref, lg_ref, lb_ref, gate_ref,
             df_ref, dres_ref, dlg_ref, dlb_ref, dgate_ref) = refs
        i = pl.program_id(0)
        xh = xh_ref[...]
        if has_mod:
            dh = d_ref[...]
            dxo_t = dh * (1.0 + sc_ref[...]) + dri_ref[...]
            _accumulate(i, dsc_ref, _colsum(dh * (xh * lg_ref[...] + lb_ref[...])))
            _accumulate(i, dsh_ref, _colsum(dh))
        else:
            dxo_t = d_ref[...]
        dxh = dxo_t * lg_ref[...]
        m1 = jnp.mean(dxh, axis=-1, keepdims=True)
        m2 = jnp.mean(dxh * xh, axis=-1, keepdims=True)
        dz = rs_ref[...] * (dxh - m1 - xh * m2)
        df_ref[...] = (dz * gate_ref[...]).astype(MXU_DTYPE)
        dres_ref[...] = ALPHA * dz
        _accumulate(i, dlg_ref, _colsum(dxo_t * xh))
        _accumulate(i, dlb_ref, _colsum(dxo_t))
        _accumulate(i, dgate_ref, _colsum(dz * f_ref[...].astype(F32)))

    tile = lambda w: pl.BlockSpec((ts, w), lambda i: (i, 0))
    n_sums = 5 if has_mod else 3
    in_specs = [tile(D), tile(D), tile(1), tile(D), _row(D), _row(D), _row(D)]
    args = [d, xhat, rstd, f, lg, lb, gate]
    if has_mod:
        in_specs += [tile(D), _row(D)]
        args += list(modulated)
    return pl.pallas_call(
        body, name=name, grid=(S // ts,), in_specs=in_specs,
        out_specs=[tile(D), tile(D)] + [_row(D)] * n_sums,
        out_shape=[jax.ShapeDtypeStruct((S, D), MXU_DTYPE), jax.ShapeDtypeStruct((S, D), F32)]
        + [jax.ShapeDtypeStruct((1, D), F32)] * n_sums,
        compiler_params=_params(1),
    )(*args)


def _input_grad(dh, dres, scale, x):
    S = dh.shape[0]
    ts = min(TS_PROJ, S)

    def body(dh_ref, dres_ref, sc_ref, x_ref, dx_ref, dsc_ref, dsh_ref):
        i = pl.program_id(0)
        dh_t = dh_ref[...]
        dx_ref[...] = dh_t * (1.0 + sc_ref[...]) + dres_ref[...]
        _accumulate(i, dsc_ref, _colsum(dh_t * x_ref[...]))
        _accumulate(i, dsh_ref, _colsum(dh_t))

    tile = pl.BlockSpec((ts, D), lambda i: (i, 0))
    return pl.pallas_call(
        body, name="input_grad", grid=(S // ts,), in_specs=[tile, tile, _row(D), tile],
        out_specs=[tile, _row(D), _row(D)],
        out_shape=[jax.ShapeDtypeStruct((S, D), F32)] + [jax.ShapeDtypeStruct((1, D), F32)] * 2,
        compiler_params=_params(1),
    )(dh, dres, scale, x)


def _b2_ffn(df2, w_down_all, up, conv_w, conv_b, layer):
    S = df2.shape[0]
    ts = min(TS_FF, S)
    n_tiles = S // ts
    halo_blocks = ts // CONV_HALO
    n_ext = ts + CONV_HALO

    def body(df_ref, wd_ref, g_ref, halo_ref, val_ref, cw_ref, cb_ref, dup_ref, dcw_ref, dcb_ref, next_dgc):
        i = pl.program_id(0)
        tile_idx = n_tiles - 1 - i

        @pl.when(i == 0)
        def _():
            next_dgc[...] = jnp.zeros_like(next_dgc)

        dact = _dot_nt(df_ref[...], wd_ref[...])
        g = g_ref[...].astype(F32)
        halo = halo_ref[...].astype(F32) * (tile_idx > 0).astype(F32)
        gc, gm2, gm1 = _conv_forward(jnp.concatenate([halo, g], axis=0), cw_ref, cb_ref)
        ge, dge = _gelu(gc)
        dup_ref[:, D_FF:2 * D_FF] = (dact * ge).astype(MXU_DTYPE)
        dgc = dact * val_ref[...].astype(F32) * dge
        ext = jnp.concatenate([dgc, next_dgc[...]], axis=0)
        dp1 = pltpu.roll(ext, n_ext - 1, 0)[:ts]
        dp2 = pltpu.roll(ext, n_ext - 2, 0)[:ts]
        dup_ref[:, 0:D_FF] = (dgc * cw_ref[2:3, :] + dp1 * cw_ref[1:2, :] + dp2 * cw_ref[0:1, :]).astype(MXU_DTYPE)
        next_dgc[...] = dgc[0:CONV_HALO]
        dcw = jnp.concatenate([_colsum(dgc * gm2), _colsum(dgc * gm1), _colsum(dgc * g)], axis=0)
        _accumulate(i, dcw_ref, dcw)
        _accumulate(i, dcb_ref, _colsum(dgc))

    tile = lambda w, col=0: pl.BlockSpec((ts, w), lambda i: (n_tiles - 1 - i, col))
    return pl.pallas_call(
        body, name=f"b2_ffn_l{layer}", grid=(n_tiles,),
        in_specs=[tile(D), _layer_block((D_FF, D), layer), tile(D_FF),
                  pl.BlockSpec((CONV_HALO, D_FF),
                               lambda i: (jnp.maximum((n_tiles - 1 - i) * halo_blocks - 1, 0), 0)),
                  tile(D_FF, 1), _full((3, D_FF)), _row(D_FF)],
        out_specs=[tile(2 * D_FF), _full((3, D_FF)), _row(D_FF)],
        out_shape=[jax.ShapeDtypeStruct((S, 2 * D_FF), MXU_DTYPE),
                   jax.ShapeDtypeStruct((3, D_FF), F32), jax.ShapeDtypeStruct((1, D_FF), F32)],
        scratch_shapes=[pltpu.VMEM((CONV_HALO, D_FF), F32)],
        compiler_params=_params(1),
    )(df2, w_down_all, up, up, up, conv_w, conv_b)


def _b5_mixers(df1, w_out_all, proj, pool_w, pool_scale, sg, sb, wm, bias_full, layer):
    S = df1.shape[0]
    ts = min(TS_MIX, S)
    n_chunks = ts // CHUNK
    halo_blocks = ts // POOL_HALO
    last_halo = S // POOL_HALO - 1

    def body(df_ref, dfh_ref, wo_ref, p_ref, ah_ref, pw_ref, ps_ref, sg_ref, sb_ref, wm_ref, bias_ref,
             dp_ref, dpw_ref, dps_ref, dsg_ref, dsb_ref, dwm_ref, dbias_ref, z_scr, dvn_scr):
        i = pl.program_id(0)
        last = pl.num_programs(0) - 1
        dmix = _dot_nt(df_ref[...], wo_ref[...])
        dmix_halo = _dot_nt(dfh_ref[...], wo_ref[0:D_POOL, :]) * (i < last).astype(F32)

        a = p_ref[:, 0:D_POOL].astype(F32)
        halo = ah_ref[...].astype(F32) * (i > 0).astype(F32)
        pooled = _pool_forward(jnp.concatenate([halo, a], axis=0), a, i * ts, ts)
        n = ts + POOL_HALO
        dps_parts = []
        for g, window in enumerate(POOL_WINDOWS):
            cols = slice(g * GROUP, (g + 1) * GROUP)
            pooled_b = pooled[g].astype(MXU_DTYPE)
            mixed = _dot(pooled_b, pw_ref[g])
            dya = dmix[:, cols]
            dps_parts.append(_colsum(dya * mixed))
            dmixed = (dya * ps_ref[:, cols]).astype(MXU_DTYPE)
            dpw_g = _dot_tn(pooled_b, dmixed)

            @pl.when(i == 0)
            def _():
                dpw_ref[g] = dpw_g

            @pl.when(i > 0)
            def _():
                dpw_ref[g] += dpw_g

            dpooled = _dot_nt(dmixed, pw_ref[g])
            dmixed_h = (dmix_halo[:, cols] * ps_ref[:, cols]).astype(MXU_DTYPE)
            dpooled_h = _dot_nt(dmixed_h, pw_ref[g])
            q = dpooled * _pool_counts(i * ts, ts, window)
            s = jnp.concatenate([q, dpooled_h * (1.0 / window)], axis=0)
            k = 1
            while k < window:
                s = s + pltpu.roll(s, n - k, 0)
                k *= 2
            dp_ref[:, cols] = (s[:ts] - dpooled).astype(MXU_DTYPE)
        _accumulate(i, dps_ref, jnp.concatenate(dps_parts, axis=1))

        u = p_ref[:, D_POOL:D_POOL + D_SGU].astype(F32)
        v = p_ref[:, D_POOL + D_SGU:D_IN].astype(F32)
        gu, dgu = _gelu(u)
        vhat, rs, dgv = _sgu_norm(v)
        vn = (vhat * sg_ref[...] + sb_ref[...]).astype(MXU_DTYPE)
        dyb = dmix[:, D_POOL:D]
        dz = dyb * gu
        dzb = dz.astype(MXU_DTYPE)
        dbias = dz[0:CHUNK]
        for c in range(1, n_chunks):
            dbias = dbias + dz[c * CHUNK:(c + 1) * CHUNK]
        _accumulate(i, dbias_ref, dbias)
        for h in range(HEADS):
            cols = slice(h * GROUP, (h + 1) * GROUP)
            dwm_h = None
            for c in range(n_chunks):
                rows = slice(c * CHUNK, (c + 1) * CHUNK)
                vn_blk = vn[rows, cols]
                dz_blk = dzb[rows, cols]
                z_scr[rows, cols] = _dot(wm_ref[h], vn_blk) + bias_ref[:, cols]
                dvn_scr[rows, cols] = _dot_tn(wm_ref[h], dz_blk)
                part = _dot_nt(dz_blk, vn_blk)
                dwm_h = part if dwm_h is None else dwm_h + part

            @pl.when(i == 0)
            def _():
                dwm_ref[h] = dwm_h

            @pl.when(i > 0)
            def _():
                dwm_ref[h] += dwm_h

        dp_ref[:, D_POOL:D_POOL + D_SGU] = (dyb * z_scr[...] * dgu).astype(MXU_DTYPE)
        dvn = dvn_scr[...]
        _accumulate(i, dsg_ref, _colsum(dvn * vhat))
        _accumulate(i, dsb_ref, _colsum(dvn))
        dvh = dvn * sg_ref[...]
        m1 = jnp.mean(dvh, axis=-1, keepdims=True)
        m2 = jnp.mean(dvh * vhat, axis=-1, keepdims=True)
        dp_ref[:, D_POOL + D_SGU:D_IN] = (rs * (dvh - m1 - vhat * m2) * dgv).astype(MXU_DTYPE)

        @pl.when(i == last)
        def _():
            tri = (lax.broadcasted_iota(jnp.int32, (CHUNK, CHUNK), 0)
                   >= lax.broadcasted_iota(jnp.int32, (CHUNK, CHUNK), 1))
            for h in range(HEADS):
                dwm_ref[h] = jnp.where(tri, dwm_ref[h], 0.0)

    tile = lambda w: pl.BlockSpec((ts, w), lambda i: (i, 0))
    return pl.pallas_call(
        body, name=f"b5_mixers_l{layer}", grid=(S // ts,),
        in_specs=[tile(D),
                  pl.BlockSpec((POOL_HALO, D), lambda i: (jnp.minimum((i + 1) * halo_blocks, last_halo), 0)),
                  _layer_block((D, D), layer), tile(D_IN),
                  pl.BlockSpec((POOL_HALO, D_POOL), lambda i: (jnp.maximum(i * halo_blocks - 1, 0), 0)),
                  _full((N_GROUPS, GROUP, GROUP)), _row(D_POOL), _row(D_SGU), _row(D_SGU),
                  _full((HEADS, CHUNK, CHUNK)), _full((CHUNK, D_SGU))],
        out_specs=[tile(D_IN), _full((N_GROUPS, GROUP, GROUP)), _row(D_POOL), _row(D_SGU), _row(D_SGU),
                   _full((HEADS, CHUNK, CHUNK)), _full((CHUNK, D_SGU))],
        out_shape=[jax.ShapeDtypeStruct((S, D_IN), MXU_DTYPE), jax.ShapeDtypeStruct((N_GROUPS, GROUP, GROUP), F32),
                   jax.ShapeDtypeStruct((1, D_POOL), F32), jax.ShapeDtypeStruct((1, D_SGU), F32),
                   jax.ShapeDtypeStruct((1, D_SGU), F32), jax.ShapeDtypeStruct((HEADS, CHUNK, CHUNK), F32),
                   jax.ShapeDtypeStruct((CHUNK, D_SGU), F32)],
        scratch_shapes=[pltpu.VMEM((ts, D_SGU), F32), pltpu.VMEM((ts, D_SGU), F32)],
        compiler_params=_params(1),
    )(df1, df1, w_out_all, proj, proj, pool_w, pool_scale, sg, sb, wm, bias_full)


def _weight_grad(a, b, stacked, layer, col0, tn, n_total, name):
    S, M = a.shape
    N = b.shape[1]
    ts = min(TS_TN, S)
    col_block0 = col0 // tn

    def body(*refs):
        a_ref, b_ref, o_ref = refs[0], refs[1], refs[-1]
        i = pl.program_id(1)
        _accumulate(i, o_ref, _dot_tn(a_ref[...], b_ref[...]))

    in_specs = [pl.BlockSpec((ts, M), lambda j, i: (i, 0)), pl.BlockSpec((ts, tn), lambda j, i: (i, j))]
    args = [a, b]
    aliases = {}
    if stacked is not None:
        in_specs.append(ANY)
        args.append(stacked)
        aliases = {2: 0}
    return pl.pallas_call(
        body, name=name, grid=(N // tn, S // ts),
        in_specs=in_specs,
        out_specs=pl.BlockSpec((None, M, tn), lambda j, i: (layer, 0, col_block0 + j)),
        out_shape=jax.ShapeDtypeStruct((DEPTH, M, n_total), F32),
        input_output_aliases=aliases,
        compiler_params=_params(2),
    )(*args)


def _silu(x):
    return x * (1.0 / (1.0 + jnp.exp(-x)))


def _ada_forward(c_all, ada_w, ada_b_cols):
    n_cols = ada_w.shape[2]
    tc = 512

    def body(c_ref, w_ref, b_ref, o_ref):
        ca = _silu(c_ref[...]).astype(MXU_DTYPE)
        o_ref[...] = _dot(ca, w_ref[...].astype(MXU_DTYPE)) + b_ref[...]

    return pl.pallas_call(
        body, name="ada_forward", grid=(DEPTH, n_cols // tc),
        in_specs=[pl.BlockSpec((16, D), lambda l, j: (0, 0)), pl.BlockSpec((None, D, tc), lambda l, j: (l, 0, j)),
                  pl.BlockSpec((None, 1, tc), lambda l, j: (l, 0, j))],
        out_specs=pl.BlockSpec((None, 16, tc), lambda l, j: (l, 0, j)),
        out_shape=jax.ShapeDtypeStruct((DEPTH, 16, n_cols), F32),
        compiler_params=_params(2),
    )(c_all, ada_w, ada_b_cols)


def _ada_backward(c_all, dmod_cols):
    n_cols = dmod_cols.shape[2]
    tc = 512

    def body(c_ref, d_ref, o_ref):
        ca = _silu(c_ref[...]).astype(MXU_DTYPE)
        o_ref[...] = _dot_tn(ca, d_ref[...].astype(MXU_DTYPE))

    return pl.pallas_call(
        body, name="ada_backward", grid=(DEPTH, n_cols // tc),
        in_specs=[pl.BlockSpec((16, D), lambda l, j: (0, 0)), pl.BlockSpec((None, 16, tc), lambda l, j: (l, 0, j))],
        out_specs=pl.BlockSpec((None, D, tc), lambda l, j: (l, 0, j)),
        out_shape=jax.ShapeDtypeStruct((DEPTH, D, n_cols), F32),
        compiler_params=_params(2),
    )(c_all, dmod_cols)


def _adamw(w, g, m, v, name):
    R, C = w.shape
    tr = R
    for cand in (512, 256, 128, 64, 32, 16, 8):
        if R % cand == 0 and cand * C * 4 <= 2 ** 21:
            tr = cand
            break
    c1 = 1.0 - ADAM_B1 ** ADAM_STEP
    c2 = 1.0 - ADAM_B2 ** ADAM_STEP

    def body(w_ref, g_ref, m_ref, v_ref, d_ref, mo_ref, vo_ref):
        gg = g_ref[...]
        mn = ADAM_B1 * m_ref[...] + (1.0 - ADAM_B1) * gg
        vn = ADAM_B2 * v_ref[...] + (1.0 - ADAM_B2) * (gg * gg)
        mo_ref[...] = mn
        vo_ref[...] = vn
        d_ref[...] = -ADAM_LR * ((mn / c1) / (jnp.sqrt(vn / c2) + ADAM_EPS) + ADAM_WD * w_ref[...])

    tile = pl.BlockSpec((tr, C), lambda i: (i, 0))
    return pl.pallas_call(
        body, name=name, grid=(R // tr,), in_specs=[tile] * 4, out_specs=[tile] * 3,
        out_shape=[jax.ShapeDtypeStruct((R, C), F32)] * 3, compiler_params=_params(1),
    )(w, g, m, v)


def _position():
    x, y, c = lax.axis_index("x"), lax.axis_index("y"), lax.axis_index("c")
    other_chips = [(1 - x, y), (x, 1 - y), (1 - x, 1 - y)]
    return x, y, c, other_chips


def _all_gather8(block, name):
    R, C = block.shape

    def body(x_ref, out_ref, send_sems, recv_sems, local_sem):
        x, y, c, chips = _position()
        me, sibling = (x, y, c), (x, y, 1 - c)

        def rows(px, py, pc):
            return out_ref.at[pl.ds((4 * px + 2 * py + pc) * R, R), :]

        def copy(k, blk, to, src=None):
            return pltpu.make_async_remote_copy(
                src_ref=rows(*blk) if src is None else src, dst_ref=rows(*blk),
                send_sem=send_sems.at[k], recv_sem=recv_sems.at[k], device_id=to, device_id_type=MESH)

        mine = pltpu.make_async_copy(x_ref, rows(*me), local_sem)
        mine.start()
        first = [copy(0, me, sibling, src=x_ref)]
        first += [copy(1 + j, me, (*chip, c), src=x_ref) for j, chip in enumerate(chips)]
        for cp in first:
            cp.start()
        passed = [copy(4 + j, (*chip, c), sibling) for j, chip in enumerate(chips)]
        for j, chip in enumerate(chips):
            copy(1 + j, (*chip, c), me).wait_recv()
            passed[j].start()
        copy(0, sibling, me).wait_recv()
        for j, chip in enumerate(chips):
            copy(4 + j, (*chip, 1 - c), me).wait_recv()
        for cp in first + passed:
            cp.wait_send()
        mine.wait()

    return pl.pallas_call(
        body, name=name, out_shape=jax.ShapeDtypeStruct((N_DEV * R, C), block.dtype),
        in_specs=[pl.BlockSpec(memory_space=pltpu.VMEM)], out_specs=ANY,
        scratch_shapes=[pltpu.SemaphoreType.DMA((7,)), pltpu.SemaphoreType.DMA((7,)), pltpu.SemaphoreType.DMA(())],
    )(block)


def _gather_flat(vec, name):
    n = vec.shape[0]
    padded = -(-n // 1024) * 1024
    block = jnp.pad(vec, (0, padded - n)).reshape(8, padded // 8)
    out = _all_gather8(block, name)
    return out.reshape(N_DEV, padded)[:, :n]


def _shard_window(ref, kind, chip, layers, n_layers=2):
    n = ref.shape[2] // N_CHIPS if kind == "cols" else ref.shape[1] // N_CHIPS
    if kind == "cols":
        return ref.at[pl.ds(layers, n_layers), :, pl.ds(chip * n, n)]
    return ref.at[pl.ds(layers, n_layers), pl.ds(chip * n, n), :]


_SHARD_KINDS = ("cols", "rows", "cols", "rows")


def _gather_weights(shards):
    n_w = len(shards)
    full_shapes = []
    for w, kind in zip(shards, _SHARD_KINDS):
        L, r, cdim = w.shape
        full_shapes.append((L, r, cdim * N_CHIPS) if kind == "cols" else (L, r * N_CHIPS, cdim))

    def body(*refs):
        srcs, outs = refs[:n_w], refs[n_w:2 * n_w]
        send_sems, recv_sems = refs[2 * n_w:]
        x, y, c, chips = _position()
        my_chip = 2 * x + y
        sibling = (x, y, 1 - c)
        mine, theirs = 2 * c, 2 * (1 - c)

        def copy(kind_idx, a, src, dst, to):
            k = kind_idx * n_w + a
            return pltpu.make_async_remote_copy(src_ref=src, dst_ref=dst, send_sem=send_sems.at[k],
                                                recv_sem=recv_sems.at[k], device_id=to, device_id_type=MESH)

        def win(a, chip, layers, n_layers=2):
            return _shard_window(outs[a], _SHARD_KINDS[a], chip, layers, n_layers)

        started = []
        for a in range(n_w):
            own = srcs[a].at[pl.ds(mine, 2)]
            started.append(copy(0, a, srcs[a], win(a, my_chip, 0, DEPTH), sibling))
            for j, chip in enumerate(chips):
                started.append(copy(1 + j, a, own, win(a, my_chip, mine), (*chip, c)))
        for cp in started:
            cp.start()
        passed = []
        for j, (cx, cy) in enumerate(chips):
            for a in range(n_w):
                w = win(a, 2 * cx + cy, mine)
                copy(1 + j, a, w, w, (x, y, c)).wait_recv()
                fwd = copy(4 + j, a, w, w, sibling)
                fwd.start()
                passed.append(fwd)
        for a in range(n_w):
            w = win(a, my_chip, 0, DEPTH)
            copy(0, a, w, w, (x, y, c)).wait_recv()
            for j, (cx, cy) in enumerate(chips):
                w = win(a, 2 * cx + cy, theirs)
                copy(4 + j, a, w, w, (x, y, c)).wait_recv()
        for cp in started + passed:
            cp.wait_send()

    return pl.pallas_call(
        body, name="gather_weights",
        out_shape=[jax.ShapeDtypeStruct(s, w.dtype) for s, w in zip(full_shapes, shards)],
        in_specs=[ANY] * n_w, out_specs=[ANY] * n_w,
        scratch_shapes=[pltpu.SemaphoreType.DMA((7 * n_w,)), pltpu.SemaphoreType.DMA((7 * n_w,))],
    )(*shards)


def _swap_halves(grads):
    n_w = len(grads)

    def body(*refs):
        srcs, outs = refs[:n_w], refs[n_w:2 * n_w]
        send_sems, recv_sems = refs[2 * n_w:]
        x, y, c, _ = _position()
        copies = [pltpu.make_async_remote_copy(
            src_ref=srcs[a].at[pl.ds(2 * (1 - c), 2)], dst_ref=outs[a], send_sem=send_sems.at[a],
            recv_sem=recv_sems.at[a], device_id=(x, y, 1 - c), device_id_type=MESH) for a in range(n_w)]
        for cp in copies:
            cp.start()
        for cp in copies:
            cp.wait()

    return pl.pallas_call(
        body, name="swap_halves",
        out_shape=[jax.ShapeDtypeStruct((2,) + g.shape[1:], g.dtype) for g in grads],
        in_specs=[ANY] * n_w, out_specs=[ANY] * n_w,
        scratch_shapes=[pltpu.SemaphoreType.DMA((n_w,)), pltpu.SemaphoreType.DMA((n_w,))],
    )(*grads)


def _scatter_partials(partials):
    n_w = len(partials)
    out_shapes = []
    for p, kind in zip(partials, _SHARD_KINDS):
        _, r, cdim = p.shape
        out_shapes.append((3, 2, r, cdim // N_CHIPS) if kind == "cols" else (3, 2, r // N_CHIPS, cdim))

    def body(*refs):
        srcs, outs = refs[:n_w], refs[n_w:2 * n_w]
        send_sems, recv_sems = refs[2 * n_w:]
        x, y, c, chips = _position()
        copies = []
        for j, (cx, cy) in enumerate(chips):
            for a in range(n_w):
                k = j * n_w + a
                copies.append(pltpu.make_async_remote_copy(
                    src_ref=_shard_window(srcs[a], _SHARD_KINDS[a], 2 * cx + cy, 0), dst_ref=outs[a].at[j],
                    send_sem=send_sems.at[k], recv_sem=recv_sems.at[k], device_id=(cx, cy, c), device_id_type=MESH))
        for cp in copies:
            cp.start()
        for cp in copies:
            cp.wait()

    return pl.pallas_call(
        body, name="scatter_partials",
        out_shape=[jax.ShapeDtypeStruct(s, p.dtype) for s, p in zip(out_shapes, partials)],
        in_specs=[ANY] * n_w, out_specs=[ANY] * n_w,
        scratch_shapes=[pltpu.SemaphoreType.DMA((3 * n_w,)), pltpu.SemaphoreType.DMA((3 * n_w,))],
    )(*partials)


def _share_reduced(stacked):
    n_w = len(stacked)

    def body(*refs):
        bufs = refs[n_w:2 * n_w]
        send_sems, recv_sems = refs[2 * n_w:]
        x, y, c, _ = _position()
        copies = []
        for a in range(n_w):
            mine = bufs[a].at[pl.ds(2 * c, 2)]
            copies.append(pltpu.make_async_remote_copy(
                src_ref=mine, dst_ref=mine, send_sem=send_sems.at[a], recv_sem=recv_sems.at[a],
                device_id=(x, y, 1 - c), device_id_type=MESH))
        for cp in copies:
            cp.start()
        for a in range(n_w):
            theirs = bufs[a].at[pl.ds(2 * (1 - c), 2)]
            pltpu.make_async_remote_copy(src_ref=theirs, dst_ref=theirs, send_sem=send_sems.at[a],
                                         recv_sem=recv_sems.at[a], device_id=(x, y, c),
                                         device_id_type=MESH).wait_recv()
        for cp in copies:
            cp.wait_send()

    return pl.pallas_call(
        body, name="share_reduced",
        out_shape=[jax.ShapeDtypeStruct(s.shape, s.dtype) for s in stacked],
        in_specs=[ANY] * n_w, out_specs=[ANY] * n_w, input_output_aliases={a: a for a in range(n_w)},
        scratch_shapes=[pltpu.SemaphoreType.DMA((n_w,)), pltpu.SemaphoreType.DMA((n_w,))],
    )(*stacked)


def _chip_partial(pos, grad, theirs, name):
    _, M, N = grad.shape
    tm = 256

    def body(pos_ref, g_ref, t_ref, o_ref):
        o_ref[...] = (g_ref[...] + t_ref[...]).astype(WIRE_DTYPE)

    grid_spec = pltpu.PrefetchScalarGridSpec(
        num_scalar_prefetch=1, grid=(2, M // tm),
        in_specs=[pl.BlockSpec((None, tm, N), lambda k, i, pos: (2 * pos[0] + k, i, 0)),
                  pl.BlockSpec((None, tm, N), lambda k, i, pos: (k, i, 0))],
        out_specs=pl.BlockSpec((None, tm, N), lambda k, i, pos: (k, i, 0)))
    return pl.pallas_call(
        body, name=name, grid_spec=grid_spec, out_shape=jax.ShapeDtypeStruct((2, M, N), WIRE_DTYPE),
        compiler_params=_params(2),
    )(pos, grad, theirs)


def _reduce_shard(pos, grad, theirs, received, kind, name):
    _, M, N = grad.shape
    if kind == "cols":
        n_shard = N // N_CHIPS
        tm = min(M, 512)
        block = (None, tm, n_shard)
        grid = (2, M // tm)
        g_map = lambda k, i, pos: (2 * pos[0] + k, i, pos[1])
        t_map = lambda k, i, pos: (k, i, pos[1])
        out_shape = (DEPTH, M, n_shard)
    else:
        m_shard = M // N_CHIPS
        block = (None, m_shard, N)
        grid = (2, 1)
        g_map = lambda k, i, pos: (2 * pos[0] + k, pos[1], 0)
        t_map = lambda k, i, pos: (k, pos[1], 0)
        out_shape = (DEPTH, m_shard, N)
    r_block = (None, None) + block[1:]

    def body(pos_ref, g_ref, t_ref, r0_ref, r1_ref, r2_ref, o_ref):
        chip = pos_ref[1]
        own = g_ref[...] + t_ref[...]
        r = [r0_ref[...].astype(F32), r1_ref[...].astype(F32), r2_ref[...].astype(F32)]
        total = None
        for s in range(N_CHIPS):
            rel = jnp.bitwise_xor(chip, s)
            term = jnp.where(rel == 0, own, jnp.where(rel == 2, r[0], jnp.where(rel == 1, r[1], r[2])))
            total = term if total is None else total + term
        o_ref[...] = total

    r_spec = lambda j: pl.BlockSpec(r_block, lambda k, i, pos: (j, k, i, 0))
    grid_spec = pltpu.PrefetchScalarGridSpec(
        num_scalar_prefetch=1, grid=grid,
        in_specs=[pl.BlockSpec(block, g_map), pl.BlockSpec(block, t_map), r_spec(0), r_spec(1), r_spec(2)],
        out_specs=pl.BlockSpec(block, lambda k, i, pos: (2 * pos[0] + k, i, 0)))
    return pl.pallas_call(
        body, name=name, grid_spec=grid_spec, out_shape=jax.ShapeDtypeStruct(out_shape, F32),
        compiler_params=_params(2),
    )(pos, grad, theirs, received, received, received)


def _sum_devices(gathered):
    R8, C = gathered.shape
    R = R8 // N_DEV
    tc = C
    for cand in (4096, 2048, 1024, 512, 256, 128):
        if C % cand == 0:
            tc = cand
            break

    def body(g_ref, o_ref):
        total = g_ref[0:R, :]
        for d in range(1, N_DEV):
            total = total + g_ref[d * R:(d + 1) * R, :]
        o_ref[...] = total

    return pl.pallas_call(
        body, name="sum_devices", grid=(C // tc,),
        in_specs=[pl.BlockSpec((R8, tc), lambda j: (0, j))], out_specs=pl.BlockSpec((R, tc), lambda j: (0, j)),
        out_shape=jax.ShapeDtypeStruct((R, C), F32), compiler_params=_params(1),
    )(gathered)


_SMALL = ("pool_w", "pool_scale", "sgu_ln_g", "sgu_ln_b", "sgu_w", "sgu_b", "ln1_g", "ln1_b", "conv_w",
          "conv_b", "ln2_g", "ln2_b", "ada_b")
_WEIGHTS = ("ada_w", "ada_b", "w_in", "pool_w", "pool_scale", "sgu_ln_g", "sgu_ln_b", "sgu_w", "sgu_b", "w_out",
            "ln1_g", "ln1_b", "w_up", "conv_w", "conv_b", "w_down", "ln2_g", "ln2_b")


def kernel(x, c, ada_w, ada_b, w_in, pool_w, pool_scale, sgu_ln_g, sgu_ln_b, sgu_w, sgu_b, w_out, ln1_g, ln1_b, w_up, conv_w, conv_b, w_down, ln2_g, ln2_b, loss_target, m_ada_w, m_ada_b, m_w_in, m_pool_w, m_pool_scale, m_sgu_ln_g, m_sgu_ln_b, m_sgu_w, m_sgu_b, m_w_out, m_ln1_g, m_ln1_b, m_w_up, m_conv_w, m_conv_b, m_w_down, m_ln2_g, m_ln2_b, v_ada_w, v_ada_b, v_w_in, v_pool_w, v_pool_scale, v_sgu_ln_g, v_sgu_ln_b, v_sgu_w, v_sgu_b, v_w_out, v_ln1_g, v_ln1_b, v_w_up, v_conv_w, v_conv_b, v_w_down, v_ln2_g, v_ln2_b):
    weights = dict(ada_w=ada_w, ada_b=ada_b, w_in=w_in, pool_w=pool_w, pool_scale=pool_scale, sgu_ln_g=sgu_ln_g,
                   sgu_ln_b=sgu_ln_b, sgu_w=sgu_w, sgu_b=sgu_b, w_out=w_out, ln1_g=ln1_g, ln1_b=ln1_b, w_up=w_up,
                   conv_w=conv_w, conv_b=conv_b, w_down=w_down, ln2_g=ln2_g, ln2_b=ln2_b)
    mom_m = dict(ada_w=m_ada_w, ada_b=m_ada_b, w_in=m_w_in, pool_w=m_pool_w, pool_scale=m_pool_scale,
                 sgu_ln_g=m_sgu_ln_g, sgu_ln_b=m_sgu_ln_b, sgu_w=m_sgu_w, sgu_b=m_sgu_b, w_out=m_w_out,
                 ln1_g=m_ln1_g, ln1_b=m_ln1_b, w_up=m_w_up, conv_w=m_conv_w, conv_b=m_conv_b, w_down=m_w_down,
                 ln2_g=m_ln2_g, ln2_b=m_ln2_b)
    mom_v = dict(ada_w=v_ada_w, ada_b=v_ada_b, w_in=v_w_in, pool_w=v_pool_w, pool_scale=v_pool_scale,
                 sgu_ln_g=v_sgu_ln_g, sgu_ln_b=v_sgu_ln_b, sgu_w=v_sgu_w, sgu_b=v_sgu_b, w_out=v_w_out,
                 ln1_g=v_ln1_g, ln1_b=v_ln1_b, w_up=v_w_up, conv_w=v_conv_w, conv_b=v_conv_b, w_down=v_w_down,
                 ln2_g=v_ln2_g, ln2_b=v_ln2_b)

    ix, iy, ic = lax.axis_index("x"), lax.axis_index("y"), lax.axis_index("c")
    chip = 2 * ix + iy
    dev = 4 * ix + 2 * iy + ic
    pos = jnp.stack([ic, chip]).astype(jnp.int32)
    xs = x[0]
    target = loss_target[0]
    ff_shard = conv_w.shape[2]
    mod_shard = ada_w.shape[2]

    first = _gather_flat(jnp.concatenate([c.reshape(-1), conv_w.reshape(-1)]), "gather_c_conv")
    c_all = jnp.pad(first[:, :D], ((0, 8), (0, 0)))
    conv_parts = first[0::2, D:].reshape(N_CHIPS, DEPTH, 3, ff_shard)
    conv_full = jnp.transpose(conv_parts, (1, 2, 0, 3)).reshape(DEPTH, 3, D_FF)
    ada_b_cols = lax.dynamic_slice_in_dim(ada_b, chip * mod_shard, mod_shard, axis=1).reshape(DEPTH, 1, mod_shard)
    mod_part = _ada_forward(c_all, ada_w, ada_b_cols)[:, :8, :]
    mod_all = _gather_flat(mod_part.reshape(-1), "gather_mod").reshape(N_DEV, DEPTH, 8, mod_shard)
    mod_mine = lax.dynamic_index_in_dim(mod_all[0::2], dev, axis=2, keepdims=False)
    mod = jnp.transpose(mod_mine, (1, 0, 2)).reshape(DEPTH, 6, 1, D)

    wf_in, wf_out, wf_up, wf_down = _gather_weights(
        [w_in.astype(WIRE_DTYPE), w_out.astype(WIRE_DTYPE), w_up.astype(WIRE_DTYPE), w_down.astype(WIRE_DTYPE)])

    tri = jnp.tril(jnp.ones((CHUNK, CHUNK), dtype=bool))
    ones_row = jnp.ones((1, D), F32)
    zeros_row = jnp.zeros((1, D), F32)
    row = lambda a, l: a[l].reshape(1, -1)

    saved = []
    xn, lg, lb = xs, ones_row, zeros_row
    h = _modulate_input(xs, mod[0, 0], mod[0, 1])
    for l in range(DEPTH):
        shift1, scale1, gate1, shift2, scale2, gate2 = (mod[l, k] for k in range(6))
        nxt = min(l + 1, DEPTH - 1)
        pw = pool_w[l].astype(MXU_DTYPE)
        wm = jnp.where(tri[None], sgu_w[l], 0.0).astype(MXU_DTYPE)
        bias_full = jnp.repeat(jnp.transpose(sgu_b[l]), GROUP, axis=1)
        lg1, lb1 = row(ln1_g, l), row(ln1_b, l)
        proj = _matmul(h, wf_in, l, D_IN, STORE_DTYPE, f"f1_in_proj_l{l}")
        mix, f1, xhat1, rstd1, h2 = _f2_mixers(proj, xn, lg, lb, gate1, pw, row(pool_scale, l), row(sgu_ln_g, l),
                                               row(sgu_ln_b, l), wm, bias_full, wf_out, lg1, lb1, shift2, scale2, l)
        up = _matmul(h2, wf_up, l, D_FF, STORE_DTYPE, f"f3_up_proj_l{l}")
        act, f2, xhat2, rstd2, h_next = _f4_ffn(up, xhat1, lg1, lb1, gate2, conv_full[l], row(conv_b, l), wf_down,
                                                row(ln2_g, l), row(ln2_b, l), mod[nxt, 0], mod[nxt, 1], l)
        saved.append(dict(xn=xn, lg=lg, lb=lb, h=h, proj=proj, mix=mix, f1=f1, xhat1=xhat1, rstd1=rstd1, h2=h2,
                          up=up, act=act, f2=f2, xhat2=xhat2, rstd2=rstd2, pw=pw, wm=wm, bias_full=bias_full))
        xn, lg, lb, h = xhat2, row(ln2_g, l), row(ln2_b, l), h_next

    loss_part, d_out = _loss_head(xn, lg, lb, target)
    loss = lax.psum(loss_part[0, 0], ("x", "y", "c"))

    g_in = g_out = g_up = g_down = None
    small = {n: [None] * DEPTH for n in _SMALL}
    dmod = [[None] * 6 for _ in range(DEPTH)]
    below = None
    for l in reversed(range(DEPTH)):
        sv = saved[l]
        scale1, gate1, scale2, gate2 = mod[l, 1], mod[l, 2], mod[l, 4], mod[l, 5]
        lg1, lb1 = row(ln1_g, l), row(ln1_b, l)
        outs = _b_ln(d_out, sv["xhat2"], sv["rstd2"], sv["f2"], row(ln2_g, l), row(ln2_b, l), gate2,
                     f"b1_ln2_l{l}", modulated=below)
        df2, dres2, dlg2, dlb2, dmod[l][5] = outs[:5]
        if below is not None:
            dmod[l + 1][1], dmod[l + 1][0] = outs[5], outs[6]
        dup, dcw, dcb = _b2_ffn(df2, wf_down, sv["up"], conv_full[l], row(conv_b, l), l)
        g_down = _weight_grad(sv["act"], df2, g_down, l, 0, D, D, f"wg_down_l{l}")
        dh2 = _matmul_nt(dup, wf_up, l, 1408, f"b3_up_l{l}")
        g_up = _weight_grad(sv["h2"], dup, g_up, l, 0, 1408, 2 * D_FF, f"wg_up_l{l}")
        df1, dres1, dlg1, dlb1, dmod[l][2], dmod[l][4], dmod[l][3] = _b_ln(
            dh2, sv["xhat1"], sv["rstd1"], sv["f1"], lg1, lb1, gate1, f"b4_ln1_l{l}", modulated=(dres2, scale2))
        dproj, dpw, dps, dsg, dsb, dwm, dbias = _b5_mixers(
            df1, wf_out, sv["proj"], sv["pw"], row(pool_scale, l), row(sgu_ln_g, l), row(sgu_ln_b, l), sv["wm"],
            sv["bias_full"], l)
        g_out = _weight_grad(sv["mix"], df1, g_out, l, 0, D, D, f"wg_out_l{l}")
        d_out = _matmul_nt(dproj, wf_in, l, D_IN, f"b6_in_l{l}")
        g_in = _weight_grad(sv["h"], dproj, g_in, l, 0, 768, D_IN, f"wg_in_l{l}")
        below = (dres1, scale1)
        small["pool_w"][l], small["pool_scale"][l] = dpw, dps[0]
        small["sgu_ln_g"][l], small["sgu_ln_b"][l], small["sgu_w"][l] = dsg[0], dsb[0], dwm
        small["sgu_b"][l] = jnp.transpose(jnp.sum(dbias.reshape(CHUNK, HEADS, GROUP), axis=2))
        small["ln1_g"][l], small["ln1_b"][l] = dlg1[0], dlb1[0]
        small["conv_w"][l], small["conv_b"][l] = dcw, dcb[0]
        small["ln2_g"][l], small["ln2_b"][l] = dlg2[0], dlb2[0]
    grad_x2d, dmod[0][1], dmod[0][0] = _input_grad(d_out, below[0], below[1], xs)
    grad_x = grad_x2d[None]
    small["ada_b"] = [jnp.concatenate([part[0] for part in dmod[l]]) for l in range(DEPTH)]

    names = _SMALL
    flat = jnp.concatenate([jnp.stack(small[n]).reshape(-1) for n in names])
    n_small = flat.shape[0]
    padded = -(-n_small // 1024) * 1024
    gathered = _all_gather8(jnp.pad(flat, (0, padded - n_small)).reshape(8, padded // 8), "gather_small_grads")
    summed = _sum_devices(gathered).reshape(-1)[:n_small]
    grads = {}
    offset = 0
    for n in names:
        size = math.prod(weights[n].shape[1:]) * DEPTH if n != "conv_w" else DEPTH * 3 * D_FF
        grads[n] = summed[offset:offset + size]
        offset += size
    grads["conv_w"] = lax.dynamic_slice_in_dim(grads["conv_w"].reshape(DEPTH, 3, D_FF), chip * ff_shard, ff_shard,
                                               axis=2)
    for n in names:
        grads[n] = grads[n].reshape(weights[n].shape)
    dmod_all = gathered.reshape(N_DEV, padded)[:, n_small - DEPTH * 6 * D:n_small].reshape(N_DEV, DEPTH, 6 * D)
    dmod_cols = lax.dynamic_slice_in_dim(jnp.transpose(dmod_all, (1, 0, 2)), chip * mod_shard, mod_shard, axis=2)
    grads["ada_w"] = _ada_backward(c_all, jnp.pad(dmod_cols, ((0, 0), (0, 8), (0, 0))))

    full = [g_in, g_out, g_up, g_down]
    theirs = _swap_halves(full)
    tags = ("in", "out", "up", "down")
    partials = [_chip_partial(pos, g, t, f"chip_partial_{tag}") for g, t, tag in zip(full, theirs, tags)]
    received = _scatter_partials(partials)
    halves = [_reduce_shard(pos, g, t, r, kind, f"reduce_shard_{tag}")
              for g, t, r, kind, tag in zip(full, theirs, received, _SHARD_KINDS, tags)]
    grads["w_in"], grads["w_out"], grads["w_up"], grads["w_down"] = _share_reduced(halves)

    delta, new_m, new_v = {}, {}, {}
    for n in ("ada_w", "w_in", "w_out", "w_up", "w_down"):
        shape = weights[n].shape
        two_d = (shape[0] * shape[1], shape[2])
        d_, m_, v_ = _adamw(weights[n].reshape(two_d), grads[n].reshape(two_d), mom_m[n].reshape(two_d),
                            mom_v[n].reshape(two_d), f"adamw_{n}")
        delta[n], new_m[n], new_v[n] = d_.reshape(shape), m_.reshape(shape), v_.reshape(shape)
    sizes = [math.prod(weights[n].shape) for n in names]
    total = sum(sizes)
    padded = -(-total // 1024) * 1024
    pack = lambda d: jnp.pad(jnp.concatenate([d[n].reshape(-1) for n in names]), (0, padded - total)).reshape(8, -1)
    v_pack = jnp.pad(jnp.concatenate([mom_v[n].reshape(-1) for n in names]), (0, padded - total),
                     constant_values=1.0).reshape(8, -1)
    d_, m_, v_ = _adamw(pack(weights), pack(grads), pack(mom_m), v_pack, "adamw_small")
    offset = 0
    for n, size in zip(names, sizes):
        for src, dst in ((d_, delta), (m_, new_m), (v_, new_v)):
            dst[n] = src.reshape(-1)[offset:offset + size].reshape(weights[n].shape)
        offset += size

    return (loss, grad_x, *[grads[n] for n in _WEIGHTS], *[delta[n] for n in _WEIGHTS],
            *[new_m[n] for n in _WEIGHTS], *[new_v[n] for n in _WEIGHTS])
```

```python
import functools
import math

import jax
import jax.numpy as jnp
from jax import lax
from jax.experimental import pallas as pl
from jax.experimental.pallas import tpu as pltpu

F32 = jnp.float32
MXU_DTYPE = jnp.bfloat16
WIRE_DTYPE = jnp.bfloat16

DEPTH = 4
D = 1024
D_POOL = 512
D_SGU = 512
N_GROUPS = 4
GROUP = 128
POOL_WINDOWS = (2, 4, 8, 16)
POOL_HALO = 16
CHUNK = 128
HEADS = 4
D_IN = D_POOL + 2 * D_SGU
D_FF = 2816
CONV_HALO = 16
STORE_DTYPE = jnp.bfloat16
N_CHIPS = 4
N_DEV = 8
ALPHA = (2.0 * DEPTH) ** 0.25
LN_EPS = 1e-5
ADAM_LR, ADAM_B1, ADAM_B2, ADAM_EPS, ADAM_WD, ADAM_STEP = 0.001, 0.9, 0.999, 1e-08, 0.01, 10

TS_PROJ = 512
TS_MM = 1024
TS_MIX = 256
TS_FF = 256
TS_TN = 1024
VMEM_LIMIT = 52 * 2 ** 20

MESH = pl.DeviceIdType.MESH
ANY = pl.BlockSpec(memory_space=pl.ANY)

_GELU_K0 = math.sqrt(2.0 / math.pi)
_GELU_K1 = 0.044715


def _params(n_axes):
    return pltpu.CompilerParams(dimension_semantics=("arbitrary",) * n_axes, vmem_limit_bytes=VMEM_LIMIT)


class _Exchange:
    def __init__(self, inputs, inplace, fresh, n_copies, copies):
        self.inputs, self.inplace, self.fresh, self.n_copies, self.copies = inputs, inplace, fresh, n_copies, copies

    def out_shapes(self):
        return [jax.ShapeDtypeStruct(self.inputs[i].shape, self.inputs[i].dtype) for i in self.inplace] + list(self.fresh)

    def semaphores(self):
        return [pltpu.SemaphoreType.DMA((self.n_copies,)), pltpu.SemaphoreType.DMA((self.n_copies,))]

    def aliases(self, first_input, first_output):
        return {first_input + i: first_output + k for k, i in enumerate(self.inplace)}


def _call(body, *, name, grid, in_specs, out_specs, out_shape, args, scratch_shapes=(), exchange=None):
    n_axes = len(grid)
    in_specs, out_specs, out_shape, scratch_shapes = list(in_specs), list(out_specs), list(out_shape), list(scratch_shapes)
    if exchange is None:
        outs = pl.pallas_call(body, name=name, grid=grid, in_specs=in_specs, out_specs=out_specs, out_shape=out_shape,
                              scratch_shapes=scratch_shapes, compiler_params=_params(n_axes))(*args)
        return list(outs), []
    n_in, n_out, n_scr = len(in_specs), len(out_shape), len(scratch_shapes)
    x_out = exchange.out_shapes()
    n_xin, n_xout = len(exchange.inputs), len(x_out)

    def hosted(*refs):
        ins, xin = refs[:n_in], refs[n_in:n_in + n_xin]
        o = n_in + n_xin
        outs, xout = refs[o:o + n_out], refs[o + n_out:o + n_out + n_xout]
        s = o + n_out + n_xout
        scr, (send_sems, recv_sems) = refs[s:s + n_scr], refs[s + n_scr:]
        first = functools.reduce(jnp.logical_and, [pl.program_id(d) == 0 for d in range(n_axes)])
        last = functools.reduce(jnp.logical_and, [pl.program_id(d) == pl.num_programs(d) - 1 for d in range(n_axes)])

        @pl.when(first)
        def _():
            for cp in exchange.copies(xin, xout, send_sems, recv_sems)[0]:
                cp.start()

        body(*ins, *outs, *scr)

        @pl.when(last)
        def _():
            sends, arrivals = exchange.copies(xin, xout, send_sems, recv_sems)
            for cp in arrivals:
                cp.wait_recv()
            for cp in sends:
                cp.wait_send()

    results = pl.pallas_call(
        hosted, name=name, grid=grid, in_specs=in_specs + [ANY] * n_xin, out_specs=out_specs + [ANY] * n_xout,
        out_shape=out_shape + x_out, scratch_shapes=scratch_shapes + exchange.semaphores(),
        input_output_aliases=exchange.aliases(n_in, n_out), compiler_params=_params(n_axes),
    )(*args, *exchange.inputs)
    return list(results[:n_out]), list(results[n_out:])


def _run_exchange(exchange, name):
    x_out = exchange.out_shapes()
    n_xin, n_xout = len(exchange.inputs), len(x_out)

    def body(*refs):
        xin, xout = refs[:n_xin], refs[n_xin:n_xin + n_xout]
        send_sems, recv_sems = refs[n_xin + n_xout:]
        sends, arrivals = exchange.copies(xin, xout, send_sems, recv_sems)
        for cp in sends:
            cp.start()
        for cp in arrivals:
            cp.wait_recv()
        for cp in sends:
            cp.wait_send()

    return list(pl.pallas_call(
        body, name=name, in_specs=[ANY] * n_xin, out_specs=[ANY] * n_xout, out_shape=x_out,
        scratch_shapes=exchange.semaphores(), input_output_aliases=exchange.aliases(0, 0),
    )(*exchange.inputs))


def _dot(a, b):
    return jnp.dot(a, b, preferred_element_type=F32)


def _dot_nt(a, b):
    return lax.dot_general(a, b, (((1,), (1,)), ((), ())), preferred_element_type=F32)


def _dot_tn(a, b):
    return lax.dot_general(a, b, (((0,), (0,)), ((), ())), preferred_element_type=F32)


def _gelu(x):
    x2 = x * x
    t = jnp.tanh(_GELU_K0 * (x + _GELU_K1 * (x2 * x)))
    cdf = 0.5 * (1.0 + t)
    dg = cdf + x * (0.5 * (1.0 - t * t)) * (_GELU_K0 * (1.0 + 3.0 * _GELU_K1 * x2))
    return x * cdf, dg


def _colsum(x):
    return jnp.sum(x, axis=0, keepdims=True)


def _row(d):
    return pl.BlockSpec((1, d), lambda *_: (0, 0))


def _full(shape):
    n = len(shape)
    return pl.BlockSpec(shape, lambda *_: (0,) * n)


def _layer_block(shape, layer):
    n = len(shape)
    return pl.BlockSpec((None,) + tuple(shape), lambda *_: (layer,) + (0,) * n)


def _pool_counts(first_row, rows, window):
    t = first_row + lax.broadcasted_iota(jnp.int32, (rows, 1), 0)
    return 1.0 / jnp.minimum(t + 1, window).astype(F32)


def _pool_forward(a_ext, a, first_row, rows):
    out = []
    for g, window in enumerate(POOL_WINDOWS):
        s = a_ext[:, g * GROUP:(g + 1) * GROUP]
        k = 1
        while k < window:
            s = s + pltpu.roll(s, k, 0)
            k *= 2
        inv = _pool_counts(first_row, rows, window)
        out.append(s[POOL_HALO:] * inv - a[:, g * GROUP:(g + 1) * GROUP])
    return out


def _sgu_norm(v):
    gv, dgv = _gelu(v)
    mu = jnp.mean(gv, axis=-1, keepdims=True)
    xc = gv - mu
    var = jnp.mean(xc * xc, axis=-1, keepdims=True)
    rs = lax.rsqrt(var + LN_EPS)
    return xc * rs, rs, dgv


def _modulate_input(x, shift, scale):
    S = x.shape[0]
    ts = min(TS_PROJ, S)

    def body(x_ref, sh_ref, sc_ref, h_ref):
        h_ref[...] = (x_ref[...] * (1.0 + sc_ref[...]) + sh_ref[...]).astype(MXU_DTYPE)

    tile = pl.BlockSpec((ts, D), lambda i: (i, 0))
    return pl.pallas_call(
        body, name="modulate_input", grid=(S // ts,), in_specs=[tile, _row(D), _row(D)], out_specs=tile,
        out_shape=jax.ShapeDtypeStruct((S, D), MXU_DTYPE), compiler_params=_params(1),
    )(x, shift, scale)


def _matmul(a, w_all, layer, tn, out_dtype, name, exchange=None):
    S, K = a.shape
    N = w_all.shape[2]
    ts = min(TS_MM, S)

    def body(a_ref, w_ref, o_ref):
        o_ref[...] = _dot(a_ref[...], w_ref[...]).astype(out_dtype)

    (out,), extra = _call(
        body, name=name, grid=(N // tn, S // ts),
        in_specs=[pl.BlockSpec((ts, K), lambda j, i: (i, 0)), pl.BlockSpec((None, K, tn), lambda j, i: (layer, 0, j))],
        out_specs=[pl.BlockSpec((ts, tn), lambda j, i: (i, j))],
        out_shape=[jax.ShapeDtypeStruct((S, N), out_dtype)], args=(a, w_all), exchange=exchange)
    return out, extra


def _matmul_nt(a, w_all, layer, tk, name, exchange=None):
    S, K = a.shape
    N = w_all.shape[1]
    ts = min(TS_MM, S)
    n_k = K // tk

    def body(a_ref, w_ref, o_ref, acc):
        k = pl.program_id(1)
        part = _dot_nt(a_ref[...], w_ref[...])
        if n_k == 1:
            o_ref[...] = part
        else:
            @pl.when(k == 0)
            def _():
                acc[...] = part

            @pl.when(jnp.logical_and(k > 0, k < n_k - 1))
            def _():
                acc[...] += part

            @pl.when(k == n_k - 1)
            def _():
                o_ref[...] = acc[...] + part

    (out,), extra = _call(
        body, name=name, grid=(S // ts, n_k),
        in_specs=[pl.BlockSpec((ts, tk), lambda i, k: (i, k)), pl.BlockSpec((None, N, tk), lambda i, k: (layer, 0, k))],
        out_specs=[pl.BlockSpec((ts, N), lambda i, k: (i, 0))],
        out_shape=[jax.ShapeDtypeStruct((S, N), F32)],
        scratch_shapes=[pltpu.VMEM((ts, N), F32)], args=(a, w_all), exchange=exchange)
    return out, extra


def _f2_mixers(proj, xn, lg, lb, gate, pool_w, pool_scale, sg, sb, wm, bias_full, w_out_all, lg1, lb1, shift2, scale2,
               layer, exchange=None):
    S = xn.shape[0]
    ts = min(TS_MIX, S)
    n_chunks = ts // CHUNK
    halo_blocks = ts // POOL_HALO

    def body(p_ref, halo_ref, xn_ref, lg_ref, lb_ref, gate_ref, pw_ref, ps_ref, sg_ref, sb_ref, wm_ref,
             bias_ref, wo_ref, lg1_ref, lb1_ref, sh2_ref, sc2_ref, mix_ref, f_ref, xh_ref, rs_ref, h2_ref, z_scr):
        i = pl.program_id(0)
        a = p_ref[:, 0:D_POOL].astype(F32)
        u = p_ref[:, D_POOL:D_POOL + D_SGU].astype(F32)
        v = p_ref[:, D_POOL + D_SGU:D_IN].astype(F32)
        halo = halo_ref[...].astype(F32) * (i > 0).astype(F32)
        a_ext = jnp.concatenate([halo, a], axis=0)
        pooled = _pool_forward(a_ext, a, i * ts, ts)
        for g in range(N_GROUPS):
            mixed = _dot(pooled[g].astype(MXU_DTYPE), pw_ref[g])
            mix_ref[:, g * GROUP:(g + 1) * GROUP] = (mixed * ps_ref[:, g * GROUP:(g + 1) * GROUP]).astype(MXU_DTYPE)
        gu, _ = _gelu(u)
        vhat, _, _ = _sgu_norm(v)
        vn = (vhat * sg_ref[...] + sb_ref[...]).astype(MXU_DTYPE)
        for c in range(n_chunks):
            for h in range(HEADS):
                blk = vn[c * CHUNK:(c + 1) * CHUNK, h * GROUP:(h + 1) * GROUP]
                z_scr[c * CHUNK:(c + 1) * CHUNK, h * GROUP:(h + 1) * GROUP] = (
                    _dot(wm_ref[h], blk) + bias_ref[:, h * GROUP:(h + 1) * GROUP])
        mix_ref[:, D_POOL:D] = (gu * z_scr[...]).astype(MXU_DTYPE)
        f = _dot(mix_ref[...], wo_ref[...])
        f_ref[...] = f.astype(STORE_DTYPE)
        x = xn_ref[...] * lg_ref[...] + lb_ref[...]
        z1 = ALPHA * x + gate_ref[...] * f
        mu = jnp.mean(z1, axis=-1, keepdims=True)
        zc = z1 - mu
        var = jnp.mean(zc * zc, axis=-1, keepdims=True)
        rs = lax.rsqrt(var + LN_EPS)
        xhat = zc * rs
        xh_ref[...] = xhat
        rs_ref[...] = rs
        x1 = xhat * lg1_ref[...] + lb1_ref[...]
        h2_ref[...] = (x1 * (1.0 + sc2_ref[...]) + sh2_ref[...]).astype(MXU_DTYPE)

    tile = lambda w: pl.BlockSpec((ts, w), lambda i: (i, 0))
    return _call(
        body, name=f"f2_mixers_l{layer}", grid=(S // ts,),
        in_specs=[tile(D_IN),
                  pl.BlockSpec((POOL_HALO, D_POOL), lambda i: (jnp.maximum(i * halo_blocks - 1, 0), 0)),
                  tile(D), _row(D), _row(D), _row(D),
                  _full((N_GROUPS, GROUP, GROUP)), _row(D_POOL), _row(D_SGU), _row(D_SGU),
                  _full((HEADS, CHUNK, CHUNK)), _full((CHUNK, D_SGU)), _layer_block((D, D), layer),
                  _row(D), _row(D), _row(D), _row(D)],
        out_specs=[tile(D), tile(D), tile(D), tile(1), tile(D)],
        out_shape=[jax.ShapeDtypeStruct((S, D), MXU_DTYPE), jax.ShapeDtypeStruct((S, D), STORE_DTYPE),
                   jax.ShapeDtypeStruct((S, D), F32), jax.ShapeDtypeStruct((S, 1), F32),
                   jax.ShapeDtypeStruct((S, D), MXU_DTYPE)],
        scratch_shapes=[pltpu.VMEM((ts, D_SGU), F32)],
        args=(proj, proj, xn, lg, lb, gate, pool_w, pool_scale, sg, sb, wm, bias_full, w_out_all, lg1, lb1, shift2,
              scale2), exchange=exchange)


def _conv_forward(g_ext, cw_ref, cb_ref):
    gm2 = pltpu.roll(g_ext, 2, 0)[CONV_HALO:]
    gm1 = pltpu.roll(g_ext, 1, 0)[CONV_HALO:]
    g0 = g_ext[CONV_HALO:]
    gc = ((cb_ref[...] + gm2 * cw_ref[0:1, :]) + gm1 * cw_ref[1:2, :]) + g0 * cw_ref[2:3, :]
    return gc, gm2, gm1


def _f4_ffn(up, xhat1, lg, lb, gate, conv_w, conv_b, w_down_all, lg2, lb2, shift_next, scale_next, layer,
            exchange=None):
    S = xhat1.shape[0]
    ts = min(TS_FF, S)
    halo_blocks = ts // CONV_HALO

    def body(g_ref, halo_ref, val_ref, xh_ref, lg_ref, lb_ref, gate_ref, cw_ref, cb_ref, wd_ref, lg2_ref, lb2_ref,
             shn_ref, scn_ref, act_ref, f_ref, xo_ref, rs_ref, hn_ref):
        i = pl.program_id(0)
        halo = halo_ref[...].astype(F32) * (i > 0).astype(F32)
        g_ext = jnp.concatenate([halo, g_ref[...].astype(F32)], axis=0)
        gc, _, _ = _conv_forward(g_ext, cw_ref, cb_ref)
        ge, _ = _gelu(gc)
        act = (ge * val_ref[...].astype(F32)).astype(MXU_DTYPE)
        act_ref[...] = act
        f = _dot(act, wd_ref[...])
        f_ref[...] = f.astype(STORE_DTYPE)
        x = xh_ref[...] * lg_ref[...] + lb_ref[...]
        z = ALPHA * x + gate_ref[...] * f
        mu = jnp.mean(z, axis=-1, keepdims=True)
        zc = z - mu
        var = jnp.mean(zc * zc, axis=-1, keepdims=True)
        rs = lax.rsqrt(var + LN_EPS)
        xhat = zc * rs
        xo_ref[...] = xhat
        rs_ref[...] = rs
        x2 = xhat * lg2_ref[...] + lb2_ref[...]
        hn_ref[...] = (x2 * (1.0 + scn_ref[...]) + shn_ref[...]).astype(MXU_DTYPE)

    tile = lambda w: pl.BlockSpec((ts, w), lambda i: (i, 0))
    return _call(
        body, name=f"f4_ffn_l{layer}", grid=(S // ts,),
        in_specs=[pl.BlockSpec((ts, D_FF), lambda i: (i, 0)),
                  pl.BlockSpec((CONV_HALO, D_FF), lambda i: (jnp.maximum(i * halo_blocks - 1, 0), 0)),
                  pl.BlockSpec((ts, D_FF), lambda i: (i, 1)),
                  tile(D), _row(D), _row(D), _row(D), _full((3, D_FF)), _row(D_FF),
                  _layer_block((D_FF, D), layer), _row(D), _row(D), _row(D), _row(D)],
        out_specs=[tile(D_FF), tile(D), tile(D), tile(1), tile(D)],
        out_shape=[jax.ShapeDtypeStruct((S, D_FF), MXU_DTYPE), jax.ShapeDtypeStruct((S, D), STORE_DTYPE),
                   jax.ShapeDtypeStruct((S, D), F32), jax.ShapeDtypeStruct((S, 1), F32),
                   jax.ShapeDtypeStruct((S, D), MXU_DTYPE)],
        args=(up, up, up, xhat1, lg, lb, gate, conv_w, conv_b, w_down_all, lg2, lb2, shift_next, scale_next),
        exchange=exchange)


def _loss_head(xhat, lg, lb, target):
    S = xhat.shape[0]
    ts = min(TS_PROJ, S)

    def body(xh_ref, lg_ref, lb_ref, t_ref, loss_ref, dy_ref, acc):
        i = pl.program_id(0)

        @pl.when(i == 0)
        def _():
            acc[...] = jnp.zeros_like(acc)

        err = (xh_ref[...] * lg_ref[...] + lb_ref[...]) - t_ref[...]
        dy_ref[...] = err * (1.0 / D)
        acc[...] += _colsum(err * err)

        @pl.when(i == pl.num_programs(0) - 1)
        def _():
            loss_ref[...] = jnp.sum(acc[...], axis=1, keepdims=True) * (0.5 / D)

    tile = pl.BlockSpec((ts, D), lambda i: (i, 0))
    return pl.pallas_call(
        body, name="loss_head", grid=(S // ts,),
        in_specs=[tile, _row(D), _row(D), tile],
        out_specs=[_full((1, 1)), tile],
        out_shape=[jax.ShapeDtypeStruct((1, 1), F32), jax.ShapeDtypeStruct((S, D), F32)],
        scratch_shapes=[pltpu.VMEM((1, D), F32)],
        compiler_params=_params(1),
    )(xhat, lg, lb, target)


def _accumulate(i, ref, value):
    @pl.when(i == 0)
    def _():
        ref[...] = value

    @pl.when(i > 0)
    def _():
        ref[...] += value


def _b_ln(d, xhat, rstd, f, lg, lb, gate, name, modulated=None, exchange=None):
    S = d.shape[0]
    ts = min(TS_PROJ, S)
    has_mod = modulated is not None

    def body(*refs):
        if has_mod:
            (d_ref, xh_ref, rs_ref, f_ref, lg_ref, lb_ref, gate_ref, dri_ref, sc_ref,
             df_ref, dres_ref, dlg_ref, dlb_ref, dgate_ref, dsc_ref, dsh_ref) = refs
        else:
            (d_ref, xh_ref, rs_ref, f_ref, lg_ref, lb_ref, gate_ref,
             df_ref, dres_ref, dlg_ref, dlb_ref, dgate_ref) = refs
        i = pl.program_id(0)
        xh = xh_ref[...]
        if has_mod:
            dh = d_ref[...]
            dxo_t = dh * (1.0 + sc_ref[...]) + dri_ref[...]
            _accumulate(i, dsc_ref, _colsum(dh * (xh * lg_ref[...] + lb_ref[...])))
            _accumulate(i, dsh_ref, _colsum(dh))
        else:
            dxo_t = d_ref[...]
        dxh = dxo_t * lg_ref[...]
        m1 = jnp.mean(dxh, axis=-1, keepdims=True)
        m2 = jnp.mean(dxh * xh, axis=-1, keepdims=True)
        dz = rs_ref[...] * (dxh - m1 - xh * m2)
        df_ref[...] = (dz * gate_ref[...]).astype(MXU_DTYPE)
        dres_ref[...] = ALPHA * dz
        _accumulate(i, dlg_ref, _colsum(dxo_t * xh))
        _accumulate(i, dlb_ref, _colsum(dxo_t))
        _accumulate(i, dgate_ref, _colsum(dz * f_ref[...].astype(F32)))

    tile = lambda w: pl.BlockSpec((ts, w), lambda i: (i, 0))
    n_sums = 5 if has_mod else 3
    in_specs = [tile(D), tile(D), tile(1), tile(D), _row(D), _row(D), _row(D)]
    args = [d, xhat, rstd, f, lg, lb, gate]
    if has_mod:
        in_specs += [tile(D), _row(D)]
        args += list(modulated)
    return _call(
        body, name=name, grid=(S // ts,), in_specs=in_specs,
        out_specs=[tile(D), tile(D)] + [_row(D)] * n_sums,
        out_shape=[jax.ShapeDtypeStruct((S, D), MXU_DTYPE), jax.ShapeDtypeStruct((S, D), F32)]
        + [jax.ShapeDtypeStruct((1, D), F32)] * n_sums,
        args=args, exchange=exchange)


def _input_grad(dh, dres, scale, x):
    S = dh.shape[0]
    ts = min(TS_PROJ, S)

    def body(dh_ref, dres_ref, sc_ref, x_ref, dx_ref, dsc_ref, dsh_ref):
        i = pl.program_id(0)
        dh_t = dh_ref[...]
        dx_ref[...] = dh_t * (1.0 + sc_ref[...]) + dres_ref[...]
        _accumulate(i, dsc_ref, _colsum(dh_t * x_ref[...]))
        _accumulate(i, dsh_ref, _colsum(dh_t))

    tile = pl.BlockSpec((ts, D), lambda i: (i, 0))
    return pl.pallas_call(
        body, name="input_grad", grid=(S // ts,), in_specs=[tile, tile, _row(D), tile],
        out_specs=[tile, _row(D), _row(D)],
        out_shape=[jax.ShapeDtypeStruct((S, D), F32)] + [jax.ShapeDtypeStruct((1, D), F32)] * 2,
        compiler_params=_params(1),
    )(dh, dres, scale, x)


def _b2_ffn(df2, w_down_all, up, conv_w, conv_b, layer, exchange=None):
    S = df2.shape[0]
    ts = min(TS_FF, S)
    n_tiles = S // ts
    halo_blocks = ts // CONV_HALO
    n_ext = ts + CONV_HALO

    def body(df_ref, wd_ref, g_ref, halo_ref, val_ref, cw_ref, cb_ref, dup_ref, dcw_ref, dcb_ref, next_dgc):
        i = pl.program_id(0)
        tile_idx = n_tiles - 1 - i

        @pl.when(i == 0)
        def _():
            next_dgc[...] = jnp.zeros_like(next_dgc)

        dact = _dot_nt(df_ref[...], wd_ref[...])
        g = g_ref[...].astype(F32)
        halo = halo_ref[...].astype(F32) * (tile_idx > 0).astype(F32)
        gc, gm2, gm1 = _conv_forward(jnp.concatenate([halo, g], axis=0), cw_ref, cb_ref)
        ge, dge = _gelu(gc)
        dup_ref[:, D_FF:2 * D_FF] = (dact * ge).astype(MXU_DTYPE)
        dgc = dact * val_ref[...].astype(F32) * dge
        ext = jnp.concatenate([dgc, next_dgc[...]], axis=0)
        dp1 = pltpu.roll(ext, n_ext - 1, 0)[:ts]
        dp2 = pltpu.roll(ext, n_ext - 2, 0)[:ts]
        dup_ref[:, 0:D_FF] = (dgc * cw_ref[2:3, :] + dp1 * cw_ref[1:2, :] + dp2 * cw_ref[0:1, :]).astype(MXU_DTYPE)
        next_dgc[...] = dgc[0:CONV_HALO]
        dcw = jnp.concatenate([_colsum(dgc * gm2), _colsum(dgc * gm1), _colsum(dgc * g)], axis=0)
        _accumulate(i, dcw_ref, dcw)
        _accumulate(i, dcb_ref, _colsum(dgc))

    tile = lambda w, col=0: pl.BlockSpec((ts, w), lambda i: (n_tiles - 1 - i, col))
    return _call(
        body, name=f"b2_ffn_l{layer}", grid=(n_tiles,),
        in_specs=[tile(D), _layer_block((D_FF, D), layer), tile(D_FF),
                  pl.BlockSpec((CONV_HALO, D_FF),
                               lambda i: (jnp.maximum((n_tiles - 1 - i) * halo_blocks - 1, 0), 0)),
                  tile(D_FF, 1), _full((3, D_FF)), _row(D_FF)],
        out_specs=[tile(2 * D_FF), _full((3, D_FF)), _row(D_FF)],
        out_shape=[jax.ShapeDtypeStruct((S, 2 * D_FF), MXU_DTYPE),
                   jax.ShapeDtypeStruct((3, D_FF), F32), jax.ShapeDtypeStruct((1, D_FF), F32)],
        scratch_shapes=[pltpu.VMEM((CONV_HALO, D_FF), F32)],
        args=(df2, w_down_all, up, up, up, conv_w, conv_b), exchange=exchange)


def _b5_mixers(df1, w_out_all, proj, pool_w, pool_scale, sg, sb, wm, bias_full, layer):
    S = df1.shape[0]
    ts = min(TS_MIX, S)
    n_chunks = ts // CHUNK
    halo_blocks = ts // POOL_HALO
    last_halo = S // POOL_HALO - 1

    def body(df_ref, dfh_ref, wo_ref, p_ref, ah_ref, pw_ref, ps_ref, sg_ref, sb_ref, wm_ref, bias_ref,
             dp_ref, dpw_ref, dps_ref, dsg_ref, dsb_ref, dwm_ref, dbias_ref, z_scr, dvn_scr):
        i = pl.program_id(0)
        last = pl.num_programs(0) - 1
        dmix = _dot_nt(df_ref[...], wo_ref[...])
        dmix_halo = _dot_nt(dfh_ref[...], wo_ref[0:D_POOL, :]) * (i < last).astype(F32)

        a = p_ref[:, 0:D_POOL].astype(F32)
        halo = ah_ref[...].astype(F32) * (i > 0).astype(F32)
        pooled = _pool_forward(jnp.concatenate([halo, a], axis=0), a, i * ts, ts)
        n = ts + POOL_HALO
        dps_parts = []
        for g, window in enumerate(POOL_WINDOWS):
            cols = slice(g * GROUP, (g + 1) * GROUP)
            pooled_b = pooled[g].astype(MXU_DTYPE)
            mixed = _dot(pooled_b, pw_ref[g])
            dya = dmix[:, cols]
            dps_parts.append(_colsum(dya * mixed))
            dmixed = (dya * ps_ref[:, cols]).astype(MXU_DTYPE)
            dpw_g = _dot_tn(pooled_b, dmixed)

            @pl.when(i == 0)
            def _():
                dpw_ref[g] = dpw_g

            @pl.when(i > 0)
            def _():
                dpw_ref[g] += dpw_g

            dpooled = _dot_nt(dmixed, pw_ref[g])
            dmixed_h = (dmix_halo[:, cols] * ps_ref[:, cols]).astype(MXU_DTYPE)
            dpooled_h = _dot_nt(dmixed_h, pw_ref[g])
            q = dpooled * _pool_counts(i * ts, ts, window)
            s = jnp.concatenate([q, dpooled_h * (1.0 / window)], axis=0)
            k = 1
            while k < window:
                s = s + pltpu.roll(s, n - k, 0)
                k *= 2
            dp_ref[:, cols] = (s[:ts] - dpooled).astype(MXU_DTYPE)
        _accumulate(i, dps_ref, jnp.concatenate(dps_parts, axis=1))

        u = p_ref[:, D_POOL:D_POOL + D_SGU].astype(F32)
        v = p_ref[:, D_POOL + D_SGU:D_IN].astype(F32)
        gu, dgu = _gelu(u)
        vhat, rs, dgv = _sgu_norm(v)
        vn = (vhat * sg_ref[...] + sb_ref[...]).astype(MXU_DTYPE)
        dyb = dmix[:, D_POOL:D]
        dz = dyb * gu
        dzb = dz.astype(MXU_DTYPE)
        dbias = dz[0:CHUNK]
        for c in range(1, n_chunks):
            dbias = dbias + dz[c * CHUNK:(c + 1) * CHUNK]
        _accumulate(i, dbias_ref, dbias)
        for h in range(HEADS):
            cols = slice(h * GROUP, (h + 1) * GROUP)
            dwm_h = None
            for c in range(n_chunks):
                rows = slice(c * CHUNK, (c + 1) * CHUNK)
                vn_blk = vn[rows, cols]
                dz_blk = dzb[rows, cols]
                z_scr[rows, cols] = _dot(wm_ref[h], vn_blk) + bias_ref[:, cols]
                dvn_scr[rows, cols] = _dot_tn(wm_ref[h], dz_blk)
                part = _dot_nt(dz_blk, vn_blk)
                dwm_h = part if dwm_h is None else dwm_h + part

            @pl.when(i == 0)
            def _():
                dwm_ref[h] = dwm_h

            @pl.when(i > 0)
            def _():
                dwm_ref[h] += dwm_h

        dp_ref[:, D_POOL:D_POOL + D_SGU] = (dyb * z_scr[...] * dgu).astype(MXU_DTYPE)
        dvn = dvn_scr[...]
        _accumulate(i, dsg_ref, _colsum(dvn * vhat))
        _accumulate(i, dsb_ref, _colsum(dvn))
        dvh = dvn * sg_ref[...]
        m1 = jnp.mean(dvh, axis=-1, keepdims=True)
        m2 = jnp.mean(dvh * vhat, axis=-1, keepdims=True)
        dp_ref[:, D_POOL + D_SGU:D_IN] = (rs * (dvh - m1 - vhat * m2) * dgv).astype(MXU_DTYPE)

        @pl.when(i == last)
        def _():
            tri = (lax.broadcasted_iota(jnp.int32, (CHUNK, CHUNK), 0)
                   >= lax.broadcasted_iota(jnp.int32, (CHUNK, CHUNK), 1))
            for h in range(HEADS):
                dwm_ref[h] = jnp.where(tri, dwm_ref[h], 0.0)

    tile = lambda w: pl.BlockSpec((ts, w), lambda i: (i, 0))
    return pl.pallas_call(
        body, name=f"b5_mixers_l{layer}", grid=(S // ts,),
        in_specs=[tile(D),
                  pl.BlockSpec((POOL_HALO, D), lambda i: (jnp.minimum((i + 1) * halo_blocks, last_halo), 0)),
                  _layer_block((D, D), layer), tile(D_IN),
                  pl.BlockSpec((POOL_HALO, D_POOL), lambda i: (jnp.maximum(i * halo_blocks - 1, 0), 0)),
                  _full((N_GROUPS, GROUP, GROUP)), _row(D_POOL), _row(D_SGU), _row(D_SGU),
                  _full((HEADS, CHUNK, CHUNK)), _full((CHUNK, D_SGU))],
        out_specs=[tile(D_IN), _full((N_GROUPS, GROUP, GROUP)), _row(D_POOL), _row(D_SGU), _row(D_SGU),
                   _full((HEADS, CHUNK, CHUNK)), _full((CHUNK, D_SGU))],
        out_shape=[jax.ShapeDtypeStruct((S, D_IN), MXU_DTYPE), jax.ShapeDtypeStruct((N_GROUPS, GROUP, GROUP), F32),
                   jax.ShapeDtypeStruct((1, D_POOL), F32), jax.ShapeDtypeStruct((1, D_SGU), F32),
                   jax.ShapeDtypeStruct((1, D_SGU), F32), jax.ShapeDtypeStruct((HEADS, CHUNK, CHUNK), F32),
                   jax.ShapeDtypeStruct((CHUNK, D_SGU), F32)],
        scratch_shapes=[pltpu.VMEM((ts, D_SGU), F32), pltpu.VMEM((ts, D_SGU), F32)],
        compiler_params=_params(1),
    )(df1, df1, w_out_all, proj, proj, pool_w, pool_scale, sg, sb, wm, bias_full)


def _weight_grad(a, b, tn, name):
    S, M = a.shape
    N = b.shape[1]
    ts = min(TS_TN, S)

    def body(a_ref, b_ref, o_ref):
        _accumulate(pl.program_id(1), o_ref, _dot_tn(a_ref[...], b_ref[...]))

    return pl.pallas_call(
        body, name=name, grid=(N // tn, S // ts),
        in_specs=[pl.BlockSpec((ts, M), lambda j, i: (i, 0)), pl.BlockSpec((ts, tn), lambda j, i: (i, j))],
        out_specs=pl.BlockSpec((M, tn), lambda j, i: (0, j)),
        out_shape=jax.ShapeDtypeStruct((M, N), F32), compiler_params=_params(2),
    )(a, b)


def _silu(x):
    return x * (1.0 / (1.0 + jnp.exp(-x)))


def _ada_forward(c_all, ada_w, ada_b_cols):
    n_cols = ada_w.shape[2]
    tc = 512

    def body(c_ref, w_ref, b_ref, o_ref):
        ca = _silu(c_ref[...]).astype(MXU_DTYPE)
        o_ref[...] = _dot(ca, w_ref[...].astype(MXU_DTYPE)) + b_ref[...]

    return pl.pallas_call(
        body, name="ada_forward", grid=(DEPTH, n_cols // tc),
        in_specs=[pl.BlockSpec((16, D), lambda l, j: (0, 0)), pl.BlockSpec((None, D, tc), lambda l, j: (l, 0, j)),
                  pl.BlockSpec((None, 1, tc), lambda l, j: (l, 0, j))],
        out_specs=pl.BlockSpec((None, 16, tc), lambda l, j: (l, 0, j)),
        out_shape=jax.ShapeDtypeStruct((DEPTH, 16, n_cols), F32),
        compiler_params=_params(2),
    )(c_all, ada_w, ada_b_cols)


def _ada_backward(c_all, dmod_cols):
    n_cols = dmod_cols.shape[2]
    tc = 512

    def body(c_ref, d_ref, o_ref):
        ca = _silu(c_ref[...]).astype(MXU_DTYPE)
        o_ref[...] = _dot_tn(ca, d_ref[...].astype(MXU_DTYPE))

    return pl.pallas_call(
        body, name="ada_backward", grid=(DEPTH, n_cols // tc),
        in_specs=[pl.BlockSpec((16, D), lambda l, j: (0, 0)), pl.BlockSpec((None, 16, tc), lambda l, j: (l, 0, j))],
        out_specs=pl.BlockSpec((None, D, tc), lambda l, j: (l, 0, j)),
        out_shape=jax.ShapeDtypeStruct((DEPTH, D, n_cols), F32),
        compiler_params=_params(2),
    )(c_all, dmod_cols)


def _adamw(w, g, m, v, name):
    R, C = w.shape
    tr = R
    for cand in (512, 256, 128, 64, 32, 16, 8):
        if R % cand == 0 and cand * C * 4 <= 2 ** 21:
            tr = cand
            break
    c1 = 1.0 - ADAM_B1 ** ADAM_STEP
    c2 = 1.0 - ADAM_B2 ** ADAM_STEP

    def body(w_ref, g_ref, m_ref, v_ref, d_ref, mo_ref, vo_ref):
        gg = g_ref[...]
        mn = ADAM_B1 * m_ref[...] + (1.0 - ADAM_B1) * gg
        vn = ADAM_B2 * v_ref[...] + (1.0 - ADAM_B2) * (gg * gg)
        mo_ref[...] = mn
        vo_ref[...] = vn
        d_ref[...] = -ADAM_LR * ((mn / c1) / (jnp.sqrt(vn / c2) + ADAM_EPS) + ADAM_WD * w_ref[...])

    tile = pl.BlockSpec((tr, C), lambda i: (i, 0))
    return pl.pallas_call(
        body, name=name, grid=(R // tr,), in_specs=[tile] * 4, out_specs=[tile] * 3,
        out_shape=[jax.ShapeDtypeStruct((R, C), F32)] * 3, compiler_params=_params(1),
    )(w, g, m, v)


def _position():
    x, y, c = lax.axis_index("x"), lax.axis_index("y"), lax.axis_index("c")
    other_chips = [(1 - x, y), (x, 1 - y), (1 - x, 1 - y)]
    return x, y, c, other_chips


def _all_gather8(block, name):
    R, C = block.shape

    def body(x_ref, out_ref, send_sems, recv_sems, local_sem):
        x, y, c, chips = _position()
        me, sibling = (x, y, c), (x, y, 1 - c)

        def rows(px, py, pc):
            return out_ref.at[pl.ds((4 * px + 2 * py + pc) * R, R), :]

        def copy(k, blk, to, src=None):
            return pltpu.make_async_remote_copy(
                src_ref=rows(*blk) if src is None else src, dst_ref=rows(*blk),
                send_sem=send_sems.at[k], recv_sem=recv_sems.at[k], device_id=to, device_id_type=MESH)

        mine = pltpu.make_async_copy(x_ref, rows(*me), local_sem)
        mine.start()
        first = [copy(0, me, sibling, src=x_ref)]
        first += [copy(1 + j, me, (*chip, c), src=x_ref) for j, chip in enumerate(chips)]
        for cp in first:
            cp.start()
        passed = [copy(4 + j, (*chip, c), sibling) for j, chip in enumerate(chips)]
        for j, chip in enumerate(chips):
            copy(1 + j, (*chip, c), me).wait_recv()
            passed[j].start()
        copy(0, sibling, me).wait_recv()
        for j, chip in enumerate(chips):
            copy(4 + j, (*chip, 1 - c), me).wait_recv()
        for cp in first + passed:
            cp.wait_send()
        mine.wait()

    return pl.pallas_call(
        body, name=name, out_shape=jax.ShapeDtypeStruct((N_DEV * R, C), block.dtype),
        in_specs=[pl.BlockSpec(memory_space=pltpu.VMEM)], out_specs=ANY,
        scratch_shapes=[pltpu.SemaphoreType.DMA((7,)), pltpu.SemaphoreType.DMA((7,)), pltpu.SemaphoreType.DMA(())],
    )(block)


def _gather_flat(vec, name):
    n = vec.shape[0]
    padded = -(-n // 1024) * 1024
    block = jnp.pad(vec, (0, padded - n)).reshape(8, padded // 8)
    out = _all_gather8(block, name)
    return out.reshape(N_DEV, padded)[:, :n]


_SHARD_KINDS = ("cols", "rows", "cols", "rows")


def _shard_of(ref, shape, kind, chip):
    m, n = shape
    return ref.at[:, pl.ds(chip * (n // 4), n // 4)] if kind == "cols" else ref.at[pl.ds(chip * (m // 4), m // 4), :]


def _half_of(ref, shape, kind, h):
    m, n = shape
    return ref.at[pl.ds(h * (m // 2), m // 2), :] if kind == "cols" else ref.at[:, pl.ds(h * (n // 2), n // 2)]


def _half_shape(shape, kind):
    m, n = shape
    return (m // 2, n) if kind == "cols" else (m, n // 2)


def _shard_shape(shape, kind):
    m, n = shape
    return (m, n // 4) if kind == "cols" else (m // 4, n)


def _remote(src, dst, k, to, send_sems, recv_sems):
    return pltpu.make_async_remote_copy(src_ref=src, dst_ref=dst, send_sem=send_sems.at[k], recv_sem=recv_sems.at[k],
                                        device_id=to, device_id_type=MESH)


def _gather_exchange(shards, full, kind, layer):
    shape = full.shape[1:]

    def copies(xin, xout, send_sems, recv_sems):
        x, y, c, chips = _position()
        my_chip = 2 * x + y
        own = xin[0].at[layer]
        place = lambda chip: _shard_of(xout[0].at[layer], shape, kind, chip)
        peers = [(x, y, 1 - c)] + [(*chip, c) for chip in chips]
        sources = [my_chip] + [2 * cx + cy for cx, cy in chips]
        sends = [_remote(own, place(my_chip), k, peer, send_sems, recv_sems) for k, peer in enumerate(peers)]
        arrivals = [_remote(place(s), place(s), k, (x, y, c), send_sems, recv_sems) for k, s in enumerate(sources)]
        return sends, arrivals

    return _Exchange([shards, full], [1], [], 4, copies)


def _swap_exchange(grads):
    shapes = [g.shape for g in grads]
    fresh = [jax.ShapeDtypeStruct(_half_shape(s, k), F32) for s, k in zip(shapes, _SHARD_KINDS)]

    def copies(xin, xout, send_sems, recv_sems):
        x, y, c, _ = _position()
        sends = [_remote(_half_of(xin[a], shapes[a], _SHARD_KINDS[a], 1 - c), xout[a], a, (x, y, 1 - c),
                         send_sems, recv_sems) for a in range(len(grads))]
        arrivals = [_remote(xout[a], xout[a], a, (x, y, c), send_sems, recv_sems) for a in range(len(grads))]
        return sends, arrivals

    return _Exchange(list(grads), [], fresh, len(grads), copies)


def _scatter_exchange(partials):
    n_w = len(partials)
    shapes = [p.shape for p in partials]
    fresh = [jax.ShapeDtypeStruct((3,) + _shard_shape(s, k), p.dtype) for s, k, p in zip(shapes, _SHARD_KINDS, partials)]

    def copies(xin, xout, send_sems, recv_sems):
        x, y, c, chips = _position()
        sends, arrivals = [], []
        for j, (cx, cy) in enumerate(chips):
            for a in range(n_w):
                src = _shard_of(xin[a], shapes[a], _SHARD_KINDS[a], 2 * cx + cy)
                sends.append(_remote(src, xout[a].at[j], j * n_w + a, (cx, cy, c), send_sems, recv_sems))
                arrivals.append(_remote(xout[a].at[j], xout[a].at[j], j * n_w + a, (x, y, c), send_sems, recv_sems))
        return sends, arrivals

    return _Exchange(list(partials), [], fresh, 3 * n_w, copies)


def _share_exchange(reduced, layer):
    n_w = len(reduced)
    shapes = [r.shape[1:] for r in reduced]

    def copies(xin, xout, send_sems, recv_sems):
        x, y, c, _ = _position()
        half = lambda a, h: _half_of(xout[a].at[layer], shapes[a], _SHARD_KINDS[a], h)
        sends = [_remote(half(a, c), half(a, c), a, (x, y, 1 - c), send_sems, recv_sems) for a in range(n_w)]
        arrivals = [_remote(half(a, 1 - c), half(a, 1 - c), a, (x, y, c), send_sems, recv_sems) for a in range(n_w)]
        return sends, arrivals

    return _Exchange(list(reduced), list(range(n_w)), [], n_w, copies)


def _chip_partial(pos, grad, theirs, kind, name):
    M, N = grad.shape
    tm = 256
    if kind == "cols":
        steps = M // 2 // tm
        block, g_map = (tm, N), (lambda i, pos: (pos[0] * steps + i, 0))
    else:
        steps = M // tm
        block, g_map = (tm, N // 2), (lambda i, pos: (i, pos[0]))

    def body(pos_ref, g_ref, t_ref, o_ref):
        o_ref[...] = (g_ref[...] + t_ref[...]).astype(WIRE_DTYPE)

    grid_spec = pltpu.PrefetchScalarGridSpec(
        num_scalar_prefetch=1, grid=(steps,),
        in_specs=[pl.BlockSpec(block, g_map), pl.BlockSpec(block, lambda i, pos: (i, 0))],
        out_specs=pl.BlockSpec(block, lambda i, pos: (i, 0)))
    return pl.pallas_call(
        body, name=name, grid_spec=grid_spec, out_shape=jax.ShapeDtypeStruct(_half_shape((M, N), kind), WIRE_DTYPE),
        compiler_params=_params(1),
    )(pos, grad, theirs)


def _reduce_shard(pos, grad, theirs, received, reduced, kind, layer, name):
    M, N = grad.shape
    if kind == "cols":
        tm = min(M // 2, 512)
        steps = M // 2 // tm
        block = (tm, N // 4)
        g_map = lambda i, pos: (pos[0] * steps + i, pos[1])
        t_map = lambda i, pos: (i, pos[1])
        o_map = lambda i, pos: (layer, pos[0] * steps + i, 0)
    else:
        steps = 1
        block = (M // 4, N // 2)
        g_map = lambda i, pos: (pos[1], pos[0])
        t_map = lambda i, pos: (pos[1], 0)
        o_map = lambda i, pos: (layer, 0, pos[0])
    out_shape = (DEPTH,) + _shard_shape((M, N), kind)

    def body(pos_ref, g_ref, t_ref, r0_ref, r1_ref, r2_ref, *rest):
        o_ref = rest[-1]
        chip = pos_ref[1]
        own = g_ref[...] + t_ref[...]
        r = [r0_ref[...].astype(F32), r1_ref[...].astype(F32), r2_ref[...].astype(F32)]
        total = None
        for s in range(N_CHIPS):
            rel = jnp.bitwise_xor(chip, s)
            term = jnp.where(rel == 0, own, jnp.where(rel == 2, r[0], jnp.where(rel == 1, r[1], r[2])))
            total = term if total is None else total + term
        o_ref[...] = total

    r_spec = lambda j: pl.BlockSpec((None,) + block, lambda i, pos: (j, i, 0))
    in_specs = [pl.BlockSpec(block, g_map), pl.BlockSpec(block, t_map), r_spec(0), r_spec(1), r_spec(2)]
    args = [pos, grad, theirs, received, received, received]
    aliases = {}
    if reduced is not None:
        in_specs.append(ANY)
        args.append(reduced)
        aliases = {6: 0}
    grid_spec = pltpu.PrefetchScalarGridSpec(
        num_scalar_prefetch=1, grid=(steps,), in_specs=in_specs, out_specs=pl.BlockSpec((None,) + block, o_map))
    return pl.pallas_call(
        body, name=name, grid_spec=grid_spec, out_shape=jax.ShapeDtypeStruct(out_shape, F32),
        input_output_aliases=aliases, compiler_params=_params(1),
    )(*args)


def _sum_devices(gathered):
    R8, C = gathered.shape
    R = R8 // N_DEV
    tc = C
    for cand in (4096, 2048, 1024, 512, 256, 128):
        if C % cand == 0:
            tc = cand
            break

    def body(g_ref, o_ref):
        total = g_ref[0:R, :]
        for d in range(1, N_DEV):
            total = total + g_ref[d * R:(d + 1) * R, :]
        o_ref[...] = total

    return pl.pallas_call(
        body, name="sum_devices", grid=(C // tc,),
        in_specs=[pl.BlockSpec((R8, tc), lambda j: (0, j))], out_specs=pl.BlockSpec((R, tc), lambda j: (0, j)),
        out_shape=jax.ShapeDtypeStruct((R, C), F32), compiler_params=_params(1),
    )(gathered)


_SMALL = ("pool_w", "pool_scale", "sgu_ln_g", "sgu_ln_b", "sgu_w", "sgu_b", "ln1_g", "ln1_b", "conv_w",
          "conv_b", "ln2_g", "ln2_b", "ada_b")
_WEIGHTS = ("ada_w", "ada_b", "w_in", "pool_w", "pool_scale", "sgu_ln_g", "sgu_ln_b", "sgu_w", "sgu_b", "w_out",
            "ln1_g", "ln1_b", "w_up", "conv_w", "conv_b", "w_down", "ln2_g", "ln2_b")


def kernel(x, c, ada_w, ada_b, w_in, pool_w, pool_scale, sgu_ln_g, sgu_ln_b, sgu_w, sgu_b, w_out, ln1_g, ln1_b, w_up, conv_w, conv_b, w_down, ln2_g, ln2_b, loss_target, m_ada_w, m_ada_b, m_w_in, m_pool_w, m_pool_scale, m_sgu_ln_g, m_sgu_ln_b, m_sgu_w, m_sgu_b, m_w_out, m_ln1_g, m_ln1_b, m_w_up, m_conv_w, m_conv_b, m_w_down, m_ln2_g, m_ln2_b, v_ada_w, v_ada_b, v_w_in, v_pool_w, v_pool_scale, v_sgu_ln_g, v_sgu_ln_b, v_sgu_w, v_sgu_b, v_w_out, v_ln1_g, v_ln1_b, v_w_up, v_conv_w, v_conv_b, v_w_down, v_ln2_g, v_ln2_b):
    weights = dict(ada_w=ada_w, ada_b=ada_b, w_in=w_in, pool_w=pool_w, pool_scale=pool_scale, sgu_ln_g=sgu_ln_g,
                   sgu_ln_b=sgu_ln_b, sgu_w=sgu_w, sgu_b=sgu_b, w_out=w_out, ln1_g=ln1_g, ln1_b=ln1_b, w_up=w_up,
                   conv_w=conv_w, conv_b=conv_b, w_down=w_down, ln2_g=ln2_g, ln2_b=ln2_b)
    mom_m = dict(ada_w=m_ada_w, ada_b=m_ada_b, w_in=m_w_in, pool_w=m_pool_w, pool_scale=m_pool_scale,
                 sgu_ln_g=m_sgu_ln_g, sgu_ln_b=m_sgu_ln_b, sgu_w=m_sgu_w, sgu_b=m_sgu_b, w_out=m_w_out,
                 ln1_g=m_ln1_g, ln1_b=m_ln1_b, w_up=m_w_up, conv_w=m_conv_w, conv_b=m_conv_b, w_down=m_w_down,
                 ln2_g=m_ln2_g, ln2_b=m_ln2_b)
    mom_v = dict(ada_w=v_ada_w, ada_b=v_ada_b, w_in=v_w_in, pool_w=v_pool_w, pool_scale=v_pool_scale,
                 sgu_ln_g=v_sgu_ln_g, sgu_ln_b=v_sgu_ln_b, sgu_w=v_sgu_w, sgu_b=v_sgu_b, w_out=v_w_out,
                 ln1_g=v_ln1_g, ln1_b=v_ln1_b, w_up=v_w_up, conv_w=v_conv_w, conv_b=v_conv_b, w_down=v_w_down,
                 ln2_g=v_ln2_g, ln2_b=v_ln2_b)

    ix, iy, ic = lax.axis_index("x"), lax.axis_index("y"), lax.axis_index("c")
    chip = 2 * ix + iy
    dev = 4 * ix + 2 * iy + ic
    pos = jnp.stack([ic, chip]).astype(jnp.int32)
    xs = x[0]
    target = loss_target[0]
    ff_shard = conv_w.shape[2]
    mod_shard = ada_w.shape[2]

    first = _gather_flat(jnp.concatenate([c.reshape(-1), conv_w.reshape(-1)]), "gather_c_conv")
    c_all = jnp.pad(first[:, :D], ((0, 8), (0, 0)))
    conv_parts = first[0::2, D:].reshape(N_CHIPS, DEPTH, 3, ff_shard)
    conv_full = jnp.transpose(conv_parts, (1, 2, 0, 3)).reshape(DEPTH, 3, D_FF)
    ada_b_cols = lax.dynamic_slice_in_dim(ada_b, chip * mod_shard, mod_shard, axis=1).reshape(DEPTH, 1, mod_shard)
    mod_part = _ada_forward(c_all, ada_w, ada_b_cols)[:, :8, :]
    mod_all = _gather_flat(mod_part.reshape(-1), "gather_mod").reshape(N_DEV, DEPTH, 8, mod_shard)
    mod_mine = lax.dynamic_index_in_dim(mod_all[0::2], dev, axis=2, keepdims=False)
    mod = jnp.transpose(mod_mine, (1, 0, 2)).reshape(DEPTH, 6, 1, D)

    sh_in, sh_out, sh_up, sh_down = (w.astype(WIRE_DTYPE) for w in (w_in, w_out, w_up, w_down))
    wf_in = lax.empty((DEPTH, D, D_IN), WIRE_DTYPE)
    wf_out = lax.empty((DEPTH, D, D), WIRE_DTYPE)
    wf_up = lax.empty((DEPTH, D, 2 * D_FF), WIRE_DTYPE)
    wf_down = lax.empty((DEPTH, D_FF, D), WIRE_DTYPE)
    (wf_in,) = _run_exchange(_gather_exchange(sh_in, wf_in, "cols", 0), "gather_w_in_l0")

    tri = jnp.tril(jnp.ones((CHUNK, CHUNK), dtype=bool))
    ones_row = jnp.ones((1, D), F32)
    zeros_row = jnp.zeros((1, D), F32)
    row = lambda a, l: a[l].reshape(1, -1)

    saved = []
    xn, lg, lb = xs, ones_row, zeros_row
    h = _modulate_input(xs, mod[0, 0], mod[0, 1])
    for l in range(DEPTH):
        shift1, scale1, gate1, shift2, scale2, gate2 = (mod[l, k] for k in range(6))
        nxt = min(l + 1, DEPTH - 1)
        pw = pool_w[l].astype(MXU_DTYPE)
        wm = jnp.where(tri[None], sgu_w[l], 0.0).astype(MXU_DTYPE)
        bias_full = jnp.repeat(jnp.transpose(sgu_b[l]), GROUP, axis=1)
        lg1, lb1 = row(ln1_g, l), row(ln1_b, l)
        proj, (wf_out,) = _matmul(h, wf_in, l, D_IN, STORE_DTYPE, f"f1_in_proj_l{l}",
                                  exchange=_gather_exchange(sh_out, wf_out, "rows", l))
        (mix, f1, xhat1, rstd1, h2), (wf_up,) = _f2_mixers(
            proj, xn, lg, lb, gate1, pw, row(pool_scale, l), row(sgu_ln_g, l), row(sgu_ln_b, l), wm, bias_full,
            wf_out, lg1, lb1, shift2, scale2, l, exchange=_gather_exchange(sh_up, wf_up, "cols", l))
        up, (wf_down,) = _matmul(h2, wf_up, l, D_FF, STORE_DTYPE, f"f3_up_proj_l{l}",
                                 exchange=_gather_exchange(sh_down, wf_down, "rows", l))
        (act, f2, xhat2, rstd2, h_next), filled = _f4_ffn(
            up, xhat1, lg1, lb1, gate2, conv_full[l], row(conv_b, l), wf_down, row(ln2_g, l), row(ln2_b, l),
            mod[nxt, 0], mod[nxt, 1], l,
            exchange=_gather_exchange(sh_in, wf_in, "cols", l + 1) if l + 1 < DEPTH else None)
        if filled:
            (wf_in,) = filled
        saved.append(dict(xn=xn, lg=lg, lb=lb, h=h, proj=proj, mix=mix, f1=f1, xhat1=xhat1, rstd1=rstd1, h2=h2,
                          up=up, act=act, f2=f2, xhat2=xhat2, rstd2=rstd2, pw=pw, wm=wm, bias_full=bias_full))
        xn, lg, lb, h = xhat2, row(ln2_g, l), row(ln2_b, l), h_next

    loss_part, d_out = _loss_head(xn, lg, lb, target)
    loss = lax.psum(loss_part[0, 0], ("x", "y", "c"))

    tags = ("in", "out", "up", "down")
    small = {n: [None] * DEPTH for n in _SMALL}
    dmod = [[None] * 6 for _ in range(DEPTH)]
    below = None
    ready = None
    reduced = [None] * 4

    def partial_sums(layer, full, theirs):
        return [_chip_partial(pos, g, t, kind, f"chip_partial_{tag}_l{layer}")
                for g, t, kind, tag in zip(full, theirs, _SHARD_KINDS, tags)]

    def chip_sums(layer, full, theirs, received):
        return [_reduce_shard(pos, g, t, r, out, kind, layer, f"reduce_shard_{tag}_l{layer}")
                for g, t, r, out, kind, tag in zip(full, theirs, received, reduced, _SHARD_KINDS, tags)]

    for l in reversed(range(DEPTH)):
        sv = saved[l]
        scale1, gate1, scale2, gate2 = mod[l, 1], mod[l, 2], mod[l, 4], mod[l, 5]
        lg1, lb1 = row(ln1_g, l), row(ln1_b, l)
        outs, theirs = _b_ln(d_out, sv["xhat2"], sv["rstd2"], sv["f2"], row(ln2_g, l), row(ln2_b, l), gate2,
                             f"b1_ln2_l{l}", modulated=below,
                             exchange=_swap_exchange(ready[1]) if ready else None)
        df2, dres2, dlg2, dlb2, dmod[l][5] = outs[:5]
        if below is not None:
            dmod[l + 1][1], dmod[l + 1][0] = outs[5], outs[6]
        partials = partial_sums(ready[0], ready[1], theirs) if ready else None
        (dup, dcw, dcb), received = _b2_ffn(df2, wf_down, sv["up"], conv_full[l], row(conv_b, l), l,
                                            exchange=_scatter_exchange(partials) if ready else None)
        if ready:
            reduced = chip_sums(ready[0], ready[1], theirs, received)
        g_down = _weight_grad(sv["act"], df2, D, f"wg_down_l{l}")
        dh2, shared = _matmul_nt(dup, wf_up, l, 1408, f"b3_up_l{l}",
                                 exchange=_share_exchange(reduced, ready[0]) if ready else None)
        if ready:
            reduced = shared
        g_up = _weight_grad(sv["h2"], dup, 1408, f"wg_up_l{l}")
        (df1, dres1, dlg1, dlb1, dmod[l][2], dmod[l][4], dmod[l][3]), _ = _b_ln(
            dh2, sv["xhat1"], sv["rstd1"], sv["f1"], lg1, lb1, gate1, f"b4_ln1_l{l}", modulated=(dres2, scale2))
        dproj, dpw, dps, dsg, dsb, dwm, dbias = _b5_mixers(
            df1, wf_out, sv["proj"], sv["pw"], row(pool_scale, l), row(sgu_ln_g, l), row(sgu_ln_b, l), sv["wm"],
            sv["bias_full"], l)
        g_out = _weight_grad(sv["mix"], df1, D, f"wg_out_l{l}")
        d_out, _ = _matmul_nt(dproj, wf_in, l, D_IN, f"b6_in_l{l}")
        g_in = _weight_grad(sv["h"], dproj, 768, f"wg_in_l{l}")
        ready = (l, [g_in, g_out, g_up, g_down])
        below = (dres1, scale1)
        small["pool_w"][l], small["pool_scale"][l] = dpw, dps[0]
        small["sgu_ln_g"][l], small["sgu_ln_b"][l], small["sgu_w"][l] = dsg[0], dsb[0], dwm
        small["sgu_b"][l] = jnp.transpose(jnp.sum(dbias.reshape(CHUNK, HEADS, GROUP), axis=2))
        small["ln1_g"][l], small["ln1_b"][l] = dlg1[0], dlb1[0]
        small["conv_w"][l], small["conv_b"][l] = dcw, dcb[0]
        small["ln2_g"][l], small["ln2_b"][l] = dlg2[0], dlb2[0]
    grad_x2d, dmod[0][1], dmod[0][0] = _input_grad(d_out, below[0], below[1], xs)
    grad_x = grad_x2d[None]
    small["ada_b"] = [jnp.concatenate([part[0] for part in dmod[l]]) for l in range(DEPTH)]

    names = _SMALL
    flat = jnp.concatenate([jnp.stack(small[n]).reshape(-1) for n in names])
    n_small = flat.shape[0]
    padded = -(-n_small // 1024) * 1024
    gathered = _all_gather8(jnp.pad(flat, (0, padded - n_small)).reshape(8, padded // 8), "gather_small_grads")
    summed = _sum_devices(gathered).reshape(-1)[:n_small]
    grads = {}
    offset = 0
    for n in names:
        size = math.prod(weights[n].shape[1:]) * DEPTH if n != "conv_w" else DEPTH * 3 * D_FF
        grads[n] = summed[offset:offset + size]
        offset += size
    grads["conv_w"] = lax.dynamic_slice_in_dim(grads["conv_w"].reshape(DEPTH, 3, D_FF), chip * ff_shard, ff_shard,
                                               axis=2)
    for n in names:
        grads[n] = grads[n].reshape(weights[n].shape)
    dmod_all = gathered.reshape(N_DEV, padded)[:, n_small - DEPTH * 6 * D:n_small].reshape(N_DEV, DEPTH, 6 * D)
    dmod_cols = lax.dynamic_slice_in_dim(jnp.transpose(dmod_all, (1, 0, 2)), chip * mod_shard, mod_shard, axis=2)
    grads["ada_w"] = _ada_backward(c_all, jnp.pad(dmod_cols, ((0, 0), (0, 8), (0, 0))))

    theirs = _run_exchange(_swap_exchange(ready[1]), "swap_halves_l0")
    partials = partial_sums(0, ready[1], theirs)
    received = _run_exchange(_scatter_exchange(partials), "scatter_partials_l0")
    reduced = chip_sums(0, ready[1], theirs, received)
    grads["w_in"], grads["w_out"], grads["w_up"], grads["w_down"] = _run_exchange(
        _share_exchange(reduced, 0), "share_reduced_l0")

    delta, new_m, new_v = {}, {}, {}
    for n in ("ada_w", "w_in", "w_out", "w_up", "w_down"):
        shape = weights[n].shape
        two_d = (shape[0] * shape[1], shape[2])
        d_, m_, v_ = _adamw(weights[n].reshape(two_d), grads[n].reshape(two_d), mom_m[n].reshape(two_d),
                            mom_v[n].reshape(two_d), f"adamw_{n}")
        delta[n], new_m[n], new_v[n] = d_.reshape(shape), m_.reshape(shape), v_.reshape(shape)
    sizes = [math.prod(weights[n].shape) for n in names]
    total = sum(sizes)
    padded = -(-total // 1024) * 1024
    pack = lambda d: jnp.pad(jnp.concatenate([d[n].reshape(-1) for n in names]), (0, padded - total)).reshape(8, -1)
    v_pack = jnp.pad(jnp.concatenate([mom_v[n].reshape(-1) for n in names]), (0, padded - total),
                     constant_values=1.0).reshape(8, -1)
    d_, m_, v_ = _adamw(pack(weights), pack(grads), pack(mom_m), v_pack, "adamw_small")
    offset = 0
    for n, size in zip(names, sizes):
        for src, dst in ((d_, delta), (m_, new_m), (v_, new_v)):
            dst[n] = src.reshape(-1)[offset:offset + size].reshape(weights[n].shape)
        offset += size

    return (loss, grad_x, *[grads[n] for n in _WEIGHTS], *[delta[n] for n in _WEIGHTS],
            *[new_m[n] for n in _WEIGHTS], *[new_v[n] for n in _WEIGHTS])
```

```python
import functools
import math

import jax
import jax.numpy as jnp
from jax import lax
from jax.experimental import pallas as pl
from jax.experimental.pallas import tpu as pltpu

F32 = jnp.float32
MXU_DTYPE = jnp.bfloat16
WIRE_DTYPE = jnp.bfloat16

DEPTH = 4
D = 1024
D_POOL = 512
D_SGU = 512
N_GROUPS = 4
GROUP = 128
POOL_WINDOWS = (2, 4, 8, 16)
POOL_HALO = 16
CHUNK = 128
HEADS = 4
D_IN = D_POOL + 2 * D_SGU
D_FF = 2816
CONV_HALO = 16
STORE_DTYPE = jnp.bfloat16
N_CHIPS = 4
N_DEV = 8
ALPHA = (2.0 * DEPTH) ** 0.25
LN_EPS = 1e-5
ADAM_LR, ADAM_B1, ADAM_B2, ADAM_EPS, ADAM_WD, ADAM_STEP = 0.001, 0.9, 0.999, 1e-08, 0.01, 10

TS_PROJ = 512
TS_MM = 1024
TS_MIX = 512
TS_FF = 256
TS_TN = 1024
VMEM_LIMIT = 52 * 2 ** 20

MESH = pl.DeviceIdType.MESH
ANY = pl.BlockSpec(memory_space=pl.ANY)

_GELU_K0 = math.sqrt(2.0 / math.pi)
_GELU_K1 = 0.044715


def _params(n_axes):
    return pltpu.CompilerParams(dimension_semantics=("arbitrary",) * n_axes, vmem_limit_bytes=VMEM_LIMIT)


class _Exchange:
    def __init__(self, inputs, inplace, fresh, n_copies, copies):
        self.inputs, self.inplace, self.fresh, self.n_copies, self.copies = inputs, inplace, fresh, n_copies, copies

    def out_shapes(self):
        return [jax.ShapeDtypeStruct(self.inputs[i].shape, self.inputs[i].dtype) for i in self.inplace] + list(self.fresh)

    def semaphores(self):
        return [pltpu.SemaphoreType.DMA((self.n_copies,)), pltpu.SemaphoreType.DMA((self.n_copies,))]

    def aliases(self, first_input, first_output):
        return {first_input + i: first_output + k for k, i in enumerate(self.inplace)}


def _call(body, *, name, grid, in_specs, out_specs, out_shape, args, scratch_shapes=(), exchange=None):
    n_axes = len(grid)
    in_specs, out_specs, out_shape, scratch_shapes = list(in_specs), list(out_specs), list(out_shape), list(scratch_shapes)
    if exchange is None:
        outs = pl.pallas_call(body, name=name, grid=grid, in_specs=in_specs, out_specs=out_specs, out_shape=out_shape,
                              scratch_shapes=scratch_shapes, compiler_params=_params(n_axes))(*args)
        return list(outs), []
    n_in, n_out, n_scr = len(in_specs), len(out_shape), len(scratch_shapes)
    x_out = exchange.out_shapes()
    n_xin, n_xout = len(exchange.inputs), len(x_out)

    def hosted(*refs):
        ins, xin = refs[:n_in], refs[n_in:n_in + n_xin]
        o = n_in + n_xin
        outs, xout = refs[o:o + n_out], refs[o + n_out:o + n_out + n_xout]
        s = o + n_out + n_xout
        scr, (send_sems, recv_sems) = refs[s:s + n_scr], refs[s + n_scr:]
        first = functools.reduce(jnp.logical_and, [pl.program_id(d) == 0 for d in range(n_axes)])
        last = functools.reduce(jnp.logical_and, [pl.program_id(d) == pl.num_programs(d) - 1 for d in range(n_axes)])

        @pl.when(first)
        def _():
            for cp in exchange.copies(xin, xout, send_sems, recv_sems)[0]:
                cp.start()

        body(*ins, *outs, *scr)

        @pl.when(last)
        def _():
            sends, arrivals = exchange.copies(xin, xout, send_sems, recv_sems)
            for cp in arrivals:
                cp.wait_recv()
            for cp in sends:
                cp.wait_send()

    results = pl.pallas_call(
        hosted, name=name, grid=grid, in_specs=in_specs + [ANY] * n_xin, out_specs=out_specs + [ANY] * n_xout,
        out_shape=out_shape + x_out, scratch_shapes=scratch_shapes + exchange.semaphores(),
        input_output_aliases=exchange.aliases(n_in, n_out), compiler_params=_params(n_axes),
    )(*args, *exchange.inputs)
    return list(results[:n_out]), list(results[n_out:])


def _run_exchange(exchange, name):
    x_out = exchange.out_shapes()
    n_xin, n_xout = len(exchange.inputs), len(x_out)

    def body(*refs):
        xin, xout = refs[:n_xin], refs[n_xin:n_xin + n_xout]
        send_sems, recv_sems = refs[n_xin + n_xout:]
        sends, arrivals = exchange.copies(xin, xout, send_sems, recv_sems)
        for cp in sends:
            cp.start()
        for cp in arrivals:
            cp.wait_recv()
        for cp in sends:
            cp.wait_send()

    return list(pl.pallas_call(
        body, name=name, in_specs=[ANY] * n_xin, out_specs=[ANY] * n_xout, out_shape=x_out,
        scratch_shapes=exchange.semaphores(), input_output_aliases=exchange.aliases(0, 0),
    )(*exchange.inputs))


def _dot(a, b):
    return jnp.dot(a, b, preferred_element_type=F32)


def _dot_nt(a, b):
    return lax.dot_general(a, b, (((1,), (1,)), ((), ())), preferred_element_type=F32)


def _dot_tn(a, b):
    return lax.dot_general(a, b, (((0,), (0,)), ((), ())), preferred_element_type=F32)


def _gelu(x):
    x2 = x * x
    t = jnp.tanh(x * (x2 * (_GELU_K0 * _GELU_K1) + _GELU_K0))
    cdf = 0.5 * t + 0.5
    dg = cdf + (x * (1.0 - t * t)) * (x2 * (1.5 * _GELU_K0 * _GELU_K1) + 0.5 * _GELU_K0)
    return x * cdf, dg


def _colsum(x):
    return jnp.sum(x, axis=0, keepdims=True)


def _row(d):
    return pl.BlockSpec((1, d), lambda *_: (0, 0))


def _full(shape):
    n = len(shape)
    return pl.BlockSpec(shape, lambda *_: (0,) * n)


def _layer_block(shape, layer):
    n = len(shape)
    return pl.BlockSpec((None,) + tuple(shape), lambda *_: (layer,) + (0,) * n)


def _pool_counts(first_row, rows, window):
    t = first_row + lax.broadcasted_iota(jnp.int32, (rows, 1), 0)
    return 1.0 / jnp.minimum(t + 1, window).astype(F32)


def _pool_forward(a_ext, a, first_row, rows):
    out = []
    for g, window in enumerate(POOL_WINDOWS):
        s = a_ext[:, g * GROUP:(g + 1) * GROUP]
        k = 1
        while k < window:
            s = s + pltpu.roll(s, k, 0)
            k *= 2
        inv = _pool_counts(first_row, rows, window)
        out.append(s[POOL_HALO:] * inv - a[:, g * GROUP:(g + 1) * GROUP])
    return out


def _sgu_norm(v):
    gv, dgv = _gelu(v)
    mu = jnp.mean(gv, axis=-1, keepdims=True)
    xc = gv - mu
    var = jnp.mean(xc * xc, axis=-1, keepdims=True)
    rs = lax.rsqrt(var + LN_EPS)
    return xc * rs, rs, dgv


def _modulate_input(x, shift, scale):
    S = x.shape[0]
    ts = min(TS_PROJ, S)

    def body(x_ref, sh_ref, sc_ref, h_ref):
        h_ref[...] = (x_ref[...] * (1.0 + sc_ref[...]) + sh_ref[...]).astype(MXU_DTYPE)

    tile = pl.BlockSpec((ts, D), lambda i: (i, 0))
    return pl.pallas_call(
        body, name="modulate_input", grid=(S // ts,), in_specs=[tile, _row(D), _row(D)], out_specs=tile,
        out_shape=jax.ShapeDtypeStruct((S, D), MXU_DTYPE), compiler_params=_params(1),
    )(x, shift, scale)


def _matmul(a, w_all, layer, tn, out_dtype, name, exchange=None):
    S, K = a.shape
    N = w_all.shape[2]
    ts = min(TS_MM, S)

    def body(a_ref, w_ref, o_ref):
        o_ref[...] = _dot(a_ref[...], w_ref[...]).astype(out_dtype)

    (out,), extra = _call(
        body, name=name, grid=(N // tn, S // ts),
        in_specs=[pl.BlockSpec((ts, K), lambda j, i: (i, 0)), pl.BlockSpec((None, K, tn), lambda j, i: (layer, 0, j))],
        out_specs=[pl.BlockSpec((ts, tn), lambda j, i: (i, j))],
        out_shape=[jax.ShapeDtypeStruct((S, N), out_dtype)], args=(a, w_all), exchange=exchange)
    return out, extra


def _matmul_nt(a, w_all, layer, tk, name, exchange=None):
    S, K = a.shape
    N = w_all.shape[1]
    ts = min(TS_MM, S)
    n_k = K // tk

    def body(a_ref, w_ref, o_ref, acc):
        k = pl.program_id(1)
        part = _dot_nt(a_ref[...], w_ref[...])
        if n_k == 1:
            o_ref[...] = part
        else:
            @pl.when(k == 0)
            def _():
                acc[...] = part

            @pl.when(jnp.logical_and(k > 0, k < n_k - 1))
            def _():
                acc[...] += part

            @pl.when(k == n_k - 1)
            def _():
                o_ref[...] = acc[...] + part

    (out,), extra = _call(
        body, name=name, grid=(S // ts, n_k),
        in_specs=[pl.BlockSpec((ts, tk), lambda i, k: (i, k)), pl.BlockSpec((None, N, tk), lambda i, k: (layer, 0, k))],
        out_specs=[pl.BlockSpec((ts, N), lambda i, k: (i, 0))],
        out_shape=[jax.ShapeDtypeStruct((S, N), F32)],
        scratch_shapes=[pltpu.VMEM((ts, N), F32)], args=(a, w_all), exchange=exchange)
    return out, extra


def _f2_mixers(proj, xn, lg, lb, gate, pool_w, pool_scale, sg, sb, wm, bias_full, w_out_all, lg1, lb1, shift2, scale2,
               layer, exchange=None):
    S = xn.shape[0]
    ts = min(TS_MIX, S)
    n_chunks = ts // CHUNK
    halo_blocks = ts // POOL_HALO

    def body(p_ref, halo_ref, xn_ref, lg_ref, lb_ref, gate_ref, pw_ref, ps_ref, sg_ref, sb_ref, wm_ref,
             bias_ref, wo_ref, lg1_ref, lb1_ref, sh2_ref, sc2_ref, mix_ref, f_ref, xh_ref, rs_ref, h2_ref, z_scr):
        i = pl.program_id(0)
        a = p_ref[:, 0:D_POOL].astype(F32)
        u = p_ref[:, D_POOL:D_POOL + D_SGU].astype(F32)
        v = p_ref[:, D_POOL + D_SGU:D_IN].astype(F32)
        halo = halo_ref[...].astype(F32) * (i > 0).astype(F32)
        a_ext = jnp.concatenate([halo, a], axis=0)
        pooled = _pool_forward(a_ext, a, i * ts, ts)
        for g in range(N_GROUPS):
            mixed = _dot(pooled[g].astype(MXU_DTYPE), pw_ref[g])
            mix_ref[:, g * GROUP:(g + 1) * GROUP] = (mixed * ps_ref[:, g * GROUP:(g + 1) * GROUP]).astype(MXU_DTYPE)
        gu, _ = _gelu(u)
        vhat, _, _ = _sgu_norm(v)
        vn = (vhat * sg_ref[...] + sb_ref[...]).astype(MXU_DTYPE)
        for c in range(n_chunks):
            for h in range(HEADS):
                blk = vn[c * CHUNK:(c + 1) * CHUNK, h * GROUP:(h + 1) * GROUP]
                z_scr[c * CHUNK:(c + 1) * CHUNK, h * GROUP:(h + 1) * GROUP] = (
                    _dot(wm_ref[h], blk) + bias_ref[:, h * GROUP:(h + 1) * GROUP])
        mix_ref[:, D_POOL:D] = (gu * z_scr[...]).astype(MXU_DTYPE)
        f = _dot(mix_ref[...], wo_ref[...])
        f_ref[...] = f.astype(STORE_DTYPE)
        x = xn_ref[...] * lg_ref[...] + lb_ref[...]
        z1 = ALPHA * x + gate_ref[...] * f
        mu = jnp.mean(z1, axis=-1, keepdims=True)
        zc = z1 - mu
        var = jnp.mean(zc * zc, axis=-1, keepdims=True)
        rs = lax.rsqrt(var + LN_EPS)
        xhat = zc * rs
        xh_ref[...] = xhat
        rs_ref[...] = rs
        x1 = xhat * lg1_ref[...] + lb1_ref[...]
        h2_ref[...] = (x1 * (1.0 + sc2_ref[...]) + sh2_ref[...]).astype(MXU_DTYPE)

    tile = lambda w: pl.BlockSpec((ts, w), lambda i: (i, 0))
    return _call(
        body, name=f"f2_mixers_l{layer}", grid=(S // ts,),
        in_specs=[tile(D_IN),
                  pl.BlockSpec((POOL_HALO, D_POOL), lambda i: (jnp.maximum(i * halo_blocks - 1, 0), 0)),
                  tile(D), _row(D), _row(D), _row(D),
                  _full((N_GROUPS, GROUP, GROUP)), _row(D_POOL), _row(D_SGU), _row(D_SGU),
                  _full((HEADS, CHUNK, CHUNK)), _full((CHUNK, D_SGU)), _layer_block((D, D), layer),
                  _row(D), _row(D), _row(D), _row(D)],
        out_specs=[tile(D), tile(D), tile(D), tile(1), tile(D)],
        out_shape=[jax.ShapeDtypeStruct((S, D), MXU_DTYPE), jax.ShapeDtypeStruct((S, D), STORE_DTYPE),
                   jax.ShapeDtypeStruct((S, D), F32), jax.ShapeDtypeStruct((S, 1), F32),
                   jax.ShapeDtypeStruct((S, D), MXU_DTYPE)],
        scratch_shapes=[pltpu.VMEM((ts, D_SGU), F32)],
        args=(proj, proj, xn, lg, lb, gate, pool_w, pool_scale, sg, sb, wm, bias_full, w_out_all, lg1, lb1, shift2,
              scale2), exchange=exchange)


def _conv_forward(g_ext, cw_ref, cb_ref):
    gm2 = pltpu.roll(g_ext, 2, 0)[CONV_HALO:]
    gm1 = pltpu.roll(g_ext, 1, 0)[CONV_HALO:]
    g0 = g_ext[CONV_HALO:]
    gc = ((cb_ref[...] + gm2 * cw_ref[0:1, :]) + gm1 * cw_ref[1:2, :]) + g0 * cw_ref[2:3, :]
    return gc, gm2, gm1


def _f4_ffn(up, xhat1, lg, lb, gate, conv_w, conv_b, w_down_all, lg2, lb2, shift_next, scale_next, layer,
            exchange=None):
    S = xhat1.shape[0]
    ts = min(TS_FF, S)
    halo_blocks = ts // CONV_HALO

    def body(g_ref, halo_ref, val_ref, xh_ref, lg_ref, lb_ref, gate_ref, cw_ref, cb_ref, wd_ref, lg2_ref, lb2_ref,
             shn_ref, scn_ref, act_ref, f_ref, xo_ref, rs_ref, hn_ref):
        i = pl.program_id(0)
        halo = halo_ref[...].astype(F32) * (i > 0).astype(F32)
        g_ext = jnp.concatenate([halo, g_ref[...].astype(F32)], axis=0)
        gc, _, _ = _conv_forward(g_ext, cw_ref, cb_ref)
        ge, _ = _gelu(gc)
        act = (ge * val_ref[...].astype(F32)).astype(MXU_DTYPE)
        act_ref[...] = act
        f = _dot(act, wd_ref[...])
        f_ref[...] = f.astype(STORE_DTYPE)
        x = xh_ref[...] * lg_ref[...] + lb_ref[...]
        z = ALPHA * x + gate_ref[...] * f
        mu = jnp.mean(z, axis=-1, keepdims=True)
        zc = z - mu
        var = jnp.mean(zc * zc, axis=-1, keepdims=True)
        rs = lax.rsqrt(var + LN_EPS)
        xhat = zc * rs
        xo_ref[...] = xhat
        rs_ref[...] = rs
        x2 = xhat * lg2_ref[...] + lb2_ref[...]
        hn_ref[...] = (x2 * (1.0 + scn_ref[...]) + shn_ref[...]).astype(MXU_DTYPE)

    tile = lambda w: pl.BlockSpec((ts, w), lambda i: (i, 0))
    return _call(
        body, name=f"f4_ffn_l{layer}", grid=(S // ts,),
        in_specs=[pl.BlockSpec((ts, D_FF), lambda i: (i, 0)),
                  pl.BlockSpec((CONV_HALO, D_FF), lambda i: (jnp.maximum(i * halo_blocks - 1, 0), 0)),
                  pl.BlockSpec((ts, D_FF), lambda i: (i, 1)),
                  tile(D), _row(D), _row(D), _row(D), _full((3, D_FF)), _row(D_FF),
                  _layer_block((D_FF, D), layer), _row(D), _row(D), _row(D), _row(D)],
        out_specs=[tile(D_FF), tile(D), tile(D), tile(1), tile(D)],
        out_shape=[jax.ShapeDtypeStruct((S, D_FF), MXU_DTYPE), jax.ShapeDtypeStruct((S, D), STORE_DTYPE),
                   jax.ShapeDtypeStruct((S, D), F32), jax.ShapeDtypeStruct((S, 1), F32),
                   jax.ShapeDtypeStruct((S, D), MXU_DTYPE)],
        args=(up, up, up, xhat1, lg, lb, gate, conv_w, conv_b, w_down_all, lg2, lb2, shift_next, scale_next),
        exchange=exchange)


def _loss_head(xhat, lg, lb, target):
    S = xhat.shape[0]
    ts = min(TS_PROJ, S)

    def body(xh_ref, lg_ref, lb_ref, t_ref, loss_ref, dy_ref, acc):
        i = pl.program_id(0)

        @pl.when(i == 0)
        def _():
            acc[...] = jnp.zeros_like(acc)

        err = (xh_ref[...] * lg_ref[...] + lb_ref[...]) - t_ref[...]
        dy_ref[...] = err * (1.0 / D)
        acc[...] += _colsum(err * err)

        @pl.when(i == pl.num_programs(0) - 1)
        def _():
            loss_ref[...] = jnp.sum(acc[...], axis=1, keepdims=True) * (0.5 / D)

    tile = pl.BlockSpec((ts, D), lambda i: (i, 0))
    return pl.pallas_call(
        body, name="loss_head", grid=(S // ts,),
        in_specs=[tile, _row(D), _row(D), tile],
        out_specs=[_full((1, 1)), tile],
        out_shape=[jax.ShapeDtypeStruct((1, 1), F32), jax.ShapeDtypeStruct((S, D), F32)],
        scratch_shapes=[pltpu.VMEM((1, D), F32)],
        compiler_params=_params(1),
    )(xhat, lg, lb, target)


def _accumulate(i, ref, value):
    @pl.when(i == 0)
    def _():
        ref[...] = value

    @pl.when(i > 0)
    def _():
        ref[...] += value


def _b_ln(d, xhat, rstd, f, lg, lb, gate, name, modulated=None, exchange=None):
    S = d.shape[0]
    ts = min(TS_PROJ, S)
    has_mod = modulated is not None

    def body(*refs):
        if has_mod:
            (d_ref, xh_ref, rs_ref, f_ref, lg_ref, lb_ref, gate_ref, dri_ref, sc_ref,
             df_ref, dres_ref, dlg_ref, dlb_ref, dgate_ref, dsc_ref, dsh_ref) = refs
        else:
            (d_ref, xh_ref, rs_ref, f_ref, lg_ref, lb_ref, gate_ref,
             df_ref, dres_ref, dlg_ref, dlb_ref, dgate_ref) = refs
        i = pl.program_id(0)
        xh = xh_ref[...]
        if has_mod:
            dh = d_ref[...]
            dxo_t = dh * (1.0 + sc_ref[...]) + dri_ref[...]
            _accumulate(i, dsc_ref, _colsum(dh * (xh * lg_ref[...] + lb_ref[...])))
            _accumulate(i, dsh_ref, _colsum(dh))
        else:
            dxo_t = d_ref[...]
        dxh = dxo_t * lg_ref[...]
        m1 = jnp.mean(dxh, axis=-1, keepdims=True)
        m2 = jnp.mean(dxh * xh, axis=-1, keepdims=True)
        dz = rs_ref[...] * (dxh - m1 - xh * m2)
        df_ref[...] = (dz * gate_ref[...]).astype(MXU_DTYPE)
        dres_ref[...] = ALPHA * dz
        _accumulate(i, dlg_ref, _colsum(dxo_t * xh))
        _accumulate(i, dlb_ref, _colsum(dxo_t))
        _accumulate(i, dgate_ref, _colsum(dz * f_ref[...].astype(F32)))

    tile = lambda w: pl.BlockSpec((ts, w), lambda i: (i, 0))
    n_sums = 5 if has_mod else 3
    in_specs = [tile(D), tile(D), tile(1), tile(D), _row(D), _row(D), _row(D)]
    args = [d, xhat, rstd, f, lg, lb, gate]
    if has_mod:
        in_specs += [tile(D), _row(D)]
        args += list(modulated)
    return _call(
        body, name=name, grid=(S // ts,), in_specs=in_specs,
        out_specs=[tile(D), tile(D)] + [_row(D)] * n_sums,
        out_shape=[jax.ShapeDtypeStruct((S, D), MXU_DTYPE), jax.ShapeDtypeStruct((S, D), F32)]
        + [jax.ShapeDtypeStruct((1, D), F32)] * n_sums,
        args=args, exchange=exchange)


def _input_grad(dh, dres, scale, x):
    S = dh.shape[0]
    ts = min(TS_PROJ, S)

    def body(dh_ref, dres_ref, sc_ref, x_ref, dx_ref, dsc_ref, dsh_ref):
        i = pl.program_id(0)
        dh_t = dh_ref[...]
        dx_ref[...] = dh_t * (1.0 + sc_ref[...]) + dres_ref[...]
        _accumulate(i, dsc_ref, _colsum(dh_t * x_ref[...]))
        _accumulate(i, dsh_ref, _colsum(dh_t))

    tile = pl.BlockSpec((ts, D), lambda i: (i, 0))
    return pl.pallas_call(
        body, name="input_grad", grid=(S // ts,), in_specs=[tile, tile, _row(D), tile],
        out_specs=[tile, _row(D), _row(D)],
        out_shape=[jax.ShapeDtypeStruct((S, D), F32)] + [jax.ShapeDtypeStruct((1, D), F32)] * 2,
        compiler_params=_params(1),
    )(dh, dres, scale, x)


def _b2_ffn(df2, w_down_all, up, conv_w, conv_b, layer, exchange=None):
    S = df2.shape[0]
    ts = min(TS_FF, S)
    n_tiles = S // ts
    halo_blocks = ts // CONV_HALO
    n_ext = ts + CONV_HALO

    def body(df_ref, wd_ref, g_ref, halo_ref, val_ref, cw_ref, cb_ref, dup_ref, dcw_ref, dcb_ref, next_dgc):
        i = pl.program_id(0)
        tile_idx = n_tiles - 1 - i

        @pl.when(i == 0)
        def _():
            next_dgc[...] = jnp.zeros_like(next_dgc)

        dact = _dot_nt(df_ref[...], wd_ref[...])
        g = g_ref[...].astype(F32)
        halo = halo_ref[...].astype(F32) * (tile_idx > 0).astype(F32)
        gc, gm2, gm1 = _conv_forward(jnp.concatenate([halo, g], axis=0), cw_ref, cb_ref)
        ge, dge = _gelu(gc)
        dup_ref[:, D_FF:2 * D_FF] = (dact * ge).astype(MXU_DTYPE)
        dgc = dact * val_ref[...].astype(F32) * dge
        ext = jnp.concatenate([dgc, next_dgc[...]], axis=0)
        dp1 = pltpu.roll(ext, n_ext - 1, 0)[:ts]
        dp2 = pltpu.roll(ext, n_ext - 2, 0)[:ts]
        dup_ref[:, 0:D_FF] = (dgc * cw_ref[2:3, :] + dp1 * cw_ref[1:2, :] + dp2 * cw_ref[0:1, :]).astype(MXU_DTYPE)
        next_dgc[...] = dgc[0:CONV_HALO]
        dcw = jnp.concatenate([_colsum(dgc * gm2), _colsum(dgc * gm1), _colsum(dgc * g)], axis=0)
        _accumulate(i, dcw_ref, dcw)
        _accumulate(i, dcb_ref, _colsum(dgc))

    tile = lambda w, col=0: pl.BlockSpec((ts, w), lambda i: (n_tiles - 1 - i, col))
    return _call(
        body, name=f"b2_ffn_l{layer}", grid=(n_tiles,),
        in_specs=[tile(D), _layer_block((D_FF, D), layer), tile(D_FF),
                  pl.BlockSpec((CONV_HALO, D_FF),
                               lambda i: (jnp.maximum((n_tiles - 1 - i) * halo_blocks - 1, 0), 0)),
                  tile(D_FF, 1), _full((3, D_FF)), _row(D_FF)],
        out_specs=[tile(2 * D_FF), _full((3, D_FF)), _row(D_FF)],
        out_shape=[jax.ShapeDtypeStruct((S, 2 * D_FF), MXU_DTYPE),
                   jax.ShapeDtypeStruct((3, D_FF), F32), jax.ShapeDtypeStruct((1, D_FF), F32)],
        scratch_shapes=[pltpu.VMEM((CONV_HALO, D_FF), F32)],
        args=(df2, w_down_all, up, up, up, conv_w, conv_b), exchange=exchange)


def _b5_mixers(df1, w_out_all, proj, pool_w, pool_scale, sg, sb, wm, bias_full, layer):
    S = df1.shape[0]
    ts = min(TS_MIX, S)
    n_chunks = ts // CHUNK
    halo_blocks = ts // POOL_HALO
    last_halo = S // POOL_HALO - 1

    def body(df_ref, dfh_ref, wo_ref, p_ref, ah_ref, pw_ref, ps_ref, sg_ref, sb_ref, wm_ref, bias_ref,
             dp_ref, dpw_ref, dps_ref, dsg_ref, dsb_ref, dwm_ref, dbias_ref, z_scr, dvn_scr):
        i = pl.program_id(0)
        last = pl.num_programs(0) - 1
        dmix = _dot_nt(df_ref[...], wo_ref[...])
        dmix_halo = _dot_nt(dfh_ref[...], wo_ref[0:D_POOL, :]) * (i < last).astype(F32)

        a = p_ref[:, 0:D_POOL].astype(F32)
        halo = ah_ref[...].astype(F32) * (i > 0).astype(F32)
        pooled = _pool_forward(jnp.concatenate([halo, a], axis=0), a, i * ts, ts)
        n = ts + POOL_HALO
        dps_parts = []
        for g, window in enumerate(POOL_WINDOWS):
            cols = slice(g * GROUP, (g + 1) * GROUP)
            pooled_b = pooled[g].astype(MXU_DTYPE)
            mixed = _dot(pooled_b, pw_ref[g])
            dya = dmix[:, cols]
            dps_parts.append(_colsum(dya * mixed))
            dmixed = (dya * ps_ref[:, cols]).astype(MXU_DTYPE)
            dpw_g = _dot_tn(pooled_b, dmixed)

            @pl.when(i == 0)
            def _():
                dpw_ref[g] = dpw_g

            @pl.when(i > 0)
            def _():
                dpw_ref[g] += dpw_g

            dpooled = _dot_nt(dmixed, pw_ref[g])
            dmixed_h = (dmix_halo[:, cols] * ps_ref[:, cols]).astype(MXU_DTYPE)
            dpooled_h = _dot_nt(dmixed_h, pw_ref[g])
            q = dpooled * _pool_counts(i * ts, ts, window)
            s = jnp.concatenate([q, dpooled_h * (1.0 / window)], axis=0)
            k = 1
            while k < window:
                s = s + pltpu.roll(s, n - k, 0)
                k *= 2
            dp_ref[:, cols] = (s[:ts] - dpooled).astype(MXU_DTYPE)
        _accumulate(i, dps_ref, jnp.concatenate(dps_parts, axis=1))

        u = p_ref[:, D_POOL:D_POOL + D_SGU].astype(F32)
        v = p_ref[:, D_POOL + D_SGU:D_IN].astype(F32)
        gu, dgu = _gelu(u)
        vhat, rs, dgv = _sgu_norm(v)
        vn = (vhat * sg_ref[...] + sb_ref[...]).astype(MXU_DTYPE)
        dyb = dmix[:, D_POOL:D]
        dz = dyb * gu
        dzb = dz.astype(MXU_DTYPE)
        dbias = dz[0:CHUNK]
        for c in range(1, n_chunks):
            dbias = dbias + dz[c * CHUNK:(c + 1) * CHUNK]
        _accumulate(i, dbias_ref, dbias)
        for h in range(HEADS):
            cols = slice(h * GROUP, (h + 1) * GROUP)
            dwm_h = None
            for c in range(n_chunks):
                rows = slice(c * CHUNK, (c + 1) * CHUNK)
                vn_blk = vn[rows, cols]
                dz_blk = dzb[rows, cols]
                z_scr[rows, cols] = _dot(wm_ref[h], vn_blk) + bias_ref[:, cols]
                dvn_scr[rows, cols] = _dot_tn(wm_ref[h], dz_blk)
                part = _dot_nt(dz_blk, vn_blk)
                dwm_h = part if dwm_h is None else dwm_h + part

            @pl.when(i == 0)
            def _():
                dwm_ref[h] = dwm_h

            @pl.when(i > 0)
            def _():
                dwm_ref[h] += dwm_h

        dp_ref[:, D_POOL:D_POOL + D_SGU] = (dyb * z_scr[...] * dgu).astype(MXU_DTYPE)
        dvn = dvn_scr[...]
        _accumulate(i, dsg_ref, _colsum(dvn * vhat))
        _accumulate(i, dsb_ref, _colsum(dvn))
        dvh = dvn * sg_ref[...]
        m1 = jnp.mean(dvh, axis=-1, keepdims=True)
        m2 = jnp.mean(dvh * vhat, axis=-1, keepdims=True)
        dp_ref[:, D_POOL + D_SGU:D_IN] = (rs * (dvh - m1 - vhat * m2) * dgv).astype(MXU_DTYPE)

        @pl.when(i == last)
        def _():
            tri = (lax.broadcasted_iota(jnp.int32, (CHUNK, CHUNK), 0)
                   >= lax.broadcasted_iota(jnp.int32, (CHUNK, CHUNK), 1))
            for h in range(HEADS):
                dwm_ref[h] = jnp.where(tri, dwm_ref[h], 0.0)

    tile = lambda w: pl.BlockSpec((ts, w), lambda i: (i, 0))
    return pl.pallas_call(
        body, name=f"b5_mixers_l{layer}", grid=(S // ts,),
        in_specs=[tile(D),
                  pl.BlockSpec((POOL_HALO, D), lambda i: (jnp.minimum((i + 1) * halo_blocks, last_halo), 0)),
                  _layer_block((D, D), layer), tile(D_IN),
                  pl.BlockSpec((POOL_HALO, D_POOL), lambda i: (jnp.maximum(i * halo_blocks - 1, 0), 0)),
                  _full((N_GROUPS, GROUP, GROUP)), _row(D_POOL), _row(D_SGU), _row(D_SGU),
                  _full((HEADS, CHUNK, CHUNK)), _full((CHUNK, D_SGU))],
        out_specs=[tile(D_IN), _full((N_GROUPS, GROUP, GROUP)), _row(D_POOL), _row(D_SGU), _row(D_SGU),
                   _full((HEADS, CHUNK, CHUNK)), _full((CHUNK, D_SGU))],
        out_shape=[jax.ShapeDtypeStruct((S, D_IN), MXU_DTYPE), jax.ShapeDtypeStruct((N_GROUPS, GROUP, GROUP), F32),
                   jax.ShapeDtypeStruct((1, D_POOL), F32), jax.ShapeDtypeStruct((1, D_SGU), F32),
                   jax.ShapeDtypeStruct((1, D_SGU), F32), jax.ShapeDtypeStruct((HEADS, CHUNK, CHUNK), F32),
                   jax.ShapeDtypeStruct((CHUNK, D_SGU), F32)],
        scratch_shapes=[pltpu.VMEM((ts, D_SGU), F32), pltpu.VMEM((ts, D_SGU), F32)],
        compiler_params=_params(1),
    )(df1, df1, w_out_all, proj, proj, pool_w, pool_scale, sg, sb, wm, bias_full)


def _weight_grad(a, bs, tn, name, rows=TS_TN):
    S, M = a.shape
    counts = [b.shape[1] // tn for b in bs]
    starts = [sum(counts[:k]) for k in range(len(bs))]
    ts = min(rows, S)

    def body(a_ref, *refs):
        b_refs, o_ref = refs[:-1], refs[-1]
        j = pl.program_id(0)
        for b_ref, lo, n in zip(b_refs, starts, counts):
            @pl.when(jnp.logical_and(j >= lo, j < lo + n))
            def _():
                _accumulate(pl.program_id(1), o_ref, _dot_tn(a_ref[...], b_ref[...]))

    b_spec = lambda lo, n: pl.BlockSpec((ts, tn), lambda j, i: (i, jnp.clip(j - lo, 0, n - 1)))
    return pl.pallas_call(
        body, name=name, grid=(sum(counts), S // ts),
        in_specs=[pl.BlockSpec((ts, M), lambda j, i: (i, 0))] + [b_spec(lo, n) for lo, n in zip(starts, counts)],
        out_specs=pl.BlockSpec((M, tn), lambda j, i: (0, j)),
        out_shape=jax.ShapeDtypeStruct((M, sum(counts) * tn), F32), compiler_params=_params(2),
    )(a, *bs)


def _silu(x):
    return x * (1.0 / (1.0 + jnp.exp(-x)))


def _ada_forward(c_all, ada_w, ada_b_cols):
    n_cols = ada_w.shape[2]
    tc = 512

    def body(c_ref, w_ref, b_ref, o_ref):
        ca = _silu(c_ref[...]).astype(MXU_DTYPE)
        o_ref[...] = _dot(ca, w_ref[...].astype(MXU_DTYPE)) + b_ref[...]

    return pl.pallas_call(
        body, name="ada_forward", grid=(DEPTH, n_cols // tc),
        in_specs=[pl.BlockSpec((16, D), lambda l, j: (0, 0)), pl.BlockSpec((None, D, tc), lambda l, j: (l, 0, j)),
                  pl.BlockSpec((None, 1, tc), lambda l, j: (l, 0, j))],
        out_specs=pl.BlockSpec((None, 16, tc), lambda l, j: (l, 0, j)),
        out_shape=jax.ShapeDtypeStruct((DEPTH, 16, n_cols), F32),
        compiler_params=_params(2),
    )(c_all, ada_w, ada_b_cols)


def _ada_backward(c_all, dmod_cols):
    n_cols = dmod_cols.shape[2]
    tc = 512

    def body(c_ref, d_ref, o_ref):
        ca = _silu(c_ref[...]).astype(MXU_DTYPE)
        o_ref[...] = _dot_tn(ca, d_ref[...].astype(MXU_DTYPE))

    return pl.pallas_call(
        body, name="ada_backward", grid=(DEPTH, n_cols // tc),
        in_specs=[pl.BlockSpec((16, D), lambda l, j: (0, 0)), pl.BlockSpec((None, 16, tc), lambda l, j: (l, 0, j))],
        out_specs=pl.BlockSpec((None, D, tc), lambda l, j: (l, 0, j)),
        out_shape=jax.ShapeDtypeStruct((DEPTH, D, n_cols), F32),
        compiler_params=_params(2),
    )(c_all, dmod_cols)


def _adamw(w, g, m, v, name):
    R, C = w.shape
    tr = R
    for cand in (512, 256, 128, 64, 32, 16, 8):
        if R % cand == 0 and cand * C * 4 <= 2 ** 21:
            tr = cand
            break
    c1 = 1.0 - ADAM_B1 ** ADAM_STEP
    c2 = 1.0 - ADAM_B2 ** ADAM_STEP

    def body(w_ref, g_ref, m_ref, v_ref, d_ref, mo_ref, vo_ref):
        gg = g_ref[...]
        mn = ADAM_B1 * m_ref[...] + (1.0 - ADAM_B1) * gg
        vn = ADAM_B2 * v_ref[...] + (1.0 - ADAM_B2) * (gg * gg)
        mo_ref[...] = mn
        vo_ref[...] = vn
        d_ref[...] = -ADAM_LR * ((mn / c1) / (jnp.sqrt(vn / c2) + ADAM_EPS) + ADAM_WD * w_ref[...])

    tile = pl.BlockSpec((tr, C), lambda i: (i, 0))
    return pl.pallas_call(
        body, name=name, grid=(R // tr,), in_specs=[tile] * 4, out_specs=[tile] * 3,
        out_shape=[jax.ShapeDtypeStruct((R, C), F32)] * 3, compiler_params=_params(1),
    )(w, g, m, v)


def _position():
    x, y, c = lax.axis_index("x"), lax.axis_index("y"), lax.axis_index("c")
    other_chips = [(1 - x, y), (x, 1 - y), (1 - x, 1 - y)]
    return x, y, c, other_chips


def _all_gather8(block, name):
    R, C = block.shape

    def body(x_ref, out_ref, send_sems, recv_sems, local_sem):
        x, y, c, chips = _position()
        me, sibling = (x, y, c), (x, y, 1 - c)

        def rows(px, py, pc):
            return out_ref.at[pl.ds((4 * px + 2 * py + pc) * R, R), :]

        def copy(k, blk, to, src=None):
            return pltpu.make_async_remote_copy(
                src_ref=rows(*blk) if src is None else src, dst_ref=rows(*blk),
                send_sem=send_sems.at[k], recv_sem=recv_sems.at[k], device_id=to, device_id_type=MESH)

        mine = pltpu.make_async_copy(x_ref, rows(*me), local_sem)
        mine.start()
        first = [copy(0, me, sibling, src=x_ref)]
        first += [copy(1 + j, me, (*chip, c), src=x_ref) for j, chip in enumerate(chips)]
        for cp in first:
            cp.start()
        passed = [copy(4 + j, (*chip, c), sibling) for j, chip in enumerate(chips)]
        for j, chip in enumerate(chips):
            copy(1 + j, (*chip, c), me).wait_recv()
            passed[j].start()
        copy(0, sibling, me).wait_recv()
        for j, chip in enumerate(chips):
            copy(4 + j, (*chip, 1 - c), me).wait_recv()
        for cp in first + passed:
            cp.wait_send()
        mine.wait()

    return pl.pallas_call(
        body, name=name, out_shape=jax.ShapeDtypeStruct((N_DEV * R, C), block.dtype),
        in_specs=[pl.BlockSpec(memory_space=pltpu.VMEM)], out_specs=ANY,
        scratch_shapes=[pltpu.SemaphoreType.DMA((7,)), pltpu.SemaphoreType.DMA((7,)), pltpu.SemaphoreType.DMA(())],
    )(block)


def _gather_flat(vec, name):
    n = vec.shape[0]
    padded = -(-n // 1024) * 1024
    block = jnp.pad(vec, (0, padded - n)).reshape(8, padded // 8)
    out = _all_gather8(block, name)
    return out.reshape(N_DEV, padded)[:, :n]


_SHARD_KINDS = ("cols", "rows", "cols", "rows")


def _shard_of(ref, shape, kind, chip):
    m, n = shape
    return ref.at[:, pl.ds(chip * (n // 4), n // 4)] if kind == "cols" else ref.at[pl.ds(chip * (m // 4), m // 4), :]


def _half_of(ref, shape, kind, h):
    m, n = shape
    return ref.at[pl.ds(h * (m // 2), m // 2), :] if kind == "cols" else ref.at[:, pl.ds(h * (n // 2), n // 2)]


def _half_shape(shape, kind):
    m, n = shape
    return (m // 2, n) if kind == "cols" else (m, n // 2)


def _shard_shape(shape, kind):
    m, n = shape
    return (m, n // 4) if kind == "cols" else (m // 4, n)


def _remote(src, dst, k, to, send_sems, recv_sems):
    return pltpu.make_async_remote_copy(src_ref=src, dst_ref=dst, send_sem=send_sems.at[k], recv_sem=recv_sems.at[k],
                                        device_id=to, device_id_type=MESH)


def _gather_exchange(shards, full, kind, layer):
    shape = full.shape[1:]

    def copies(xin, xout, send_sems, recv_sems):
        x, y, c, chips = _position()
        my_chip = 2 * x + y
        own = xin[0].at[layer]
        place = lambda chip: _shard_of(xout[0].at[layer], shape, kind, chip)
        peers = [(x, y, 1 - c)] + [(*chip, c) for chip in chips]
        sources = [my_chip] + [2 * cx + cy for cx, cy in chips]
        sends = [_remote(own, place(my_chip), k, peer, send_sems, recv_sems) for k, peer in enumerate(peers)]
        arrivals = [_remote(place(s), place(s), k, (x, y, c), send_sems, recv_sems) for k, s in enumerate(sources)]
        return sends, arrivals

    return _Exchange([shards, full], [1], [], 4, copies)


def _swap_exchange(grads):
    shapes = [g.shape for g in grads]
    fresh = [jax.ShapeDtypeStruct(_half_shape(s, k), F32) for s, k in zip(shapes, _SHARD_KINDS)]

    def copies(xin, xout, send_sems, recv_sems):
        x, y, c, _ = _position()
        sends = [_remote(_half_of(xin[a], shapes[a], _SHARD_KINDS[a], 1 - c), xout[a], a, (x, y, 1 - c),
                         send_sems, recv_sems) for a in range(len(grads))]
        arrivals = [_remote(xout[a], xout[a], a, (x, y, c), send_sems, recv_sems) for a in range(len(grads))]
        return sends, arrivals

    return _Exchange(list(grads), [], fresh, len(grads), copies)


def _scatter_exchange(partials):
    n_w = len(partials)
    shapes = [p.shape for p in partials]
    fresh = [jax.ShapeDtypeStruct((3,) + _shard_shape(s, k), p.dtype) for s, k, p in zip(shapes, _SHARD_KINDS, partials)]

    def copies(xin, xout, send_sems, recv_sems):
        x, y, c, chips = _position()
        sends, arrivals = [], []
        for j, (cx, cy) in enumerate(chips):
            for a in range(n_w):
                src = _shard_of(xin[a], shapes[a], _SHARD_KINDS[a], 2 * cx + cy)
                sends.append(_remote(src, xout[a].at[j], j * n_w + a, (cx, cy, c), send_sems, recv_sems))
                arrivals.append(_remote(xout[a].at[j], xout[a].at[j], j * n_w + a, (x, y, c), send_sems, recv_sems))
        return sends, arrivals

    return _Exchange(list(partials), [], fresh, 3 * n_w, copies)


def _share_exchange(reduced, layer):
    n_w = len(reduced)
    shapes = [r.shape[1:] for r in reduced]

    def copies(xin, xout, send_sems, recv_sems):
        x, y, c, _ = _position()
        half = lambda a, h: _half_of(xout[a].at[layer], shapes[a], _SHARD_KINDS[a], h)
        sends = [_remote(half(a, c), half(a, c), a, (x, y, 1 - c), send_sems, recv_sems) for a in range(n_w)]
        arrivals = [_remote(half(a, 1 - c), half(a, 1 - c), a, (x, y, c), send_sems, recv_sems) for a in range(n_w)]
        return sends, arrivals

    return _Exchange(list(reduced), list(range(n_w)), [], n_w, copies)


def _chip_partial(pos, grad, theirs, kind, name):
    M, N = grad.shape
    tm = 256
    if kind == "cols":
        steps = M // 2 // tm
        block, g_map = (tm, N), (lambda i, pos: (pos[0] * steps + i, 0))
    else:
        steps = M // tm
        block, g_map = (tm, N // 2), (lambda i, pos: (i, pos[0]))

    def body(pos_ref, g_ref, t_ref, o_ref):
        o_ref[...] = (g_ref[...] + t_ref[...]).astype(WIRE_DTYPE)

    grid_spec = pltpu.PrefetchScalarGridSpec(
        num_scalar_prefetch=1, grid=(steps,),
        in_specs=[pl.BlockSpec(block, g_map), pl.BlockSpec(block, lambda i, pos: (i, 0))],
        out_specs=pl.BlockSpec(block, lambda i, pos: (i, 0)))
    return pl.pallas_call(
        body, name=name, grid_spec=grid_spec, out_shape=jax.ShapeDtypeStruct(_half_shape((M, N), kind), WIRE_DTYPE),
        compiler_params=_params(1),
    )(pos, grad, theirs)


def _reduce_shard(pos, grad, theirs, received, reduced, kind, layer, name):
    M, N = grad.shape
    if kind == "cols":
        tm = min(M // 2, 512)
        steps = M // 2 // tm
        block = (tm, N // 4)
        g_map = lambda i, pos: (pos[0] * steps + i, pos[1])
        t_map = lambda i, pos: (i, pos[1])
        o_map = lambda i, pos: (layer, pos[0] * steps + i, 0)
    else:
        steps = 1
        block = (M // 4, N // 2)
        g_map = lambda i, pos: (pos[1], pos[0])
        t_map = lambda i, pos: (pos[1], 0)
        o_map = lambda i, pos: (layer, 0, pos[0])
    out_shape = (DEPTH,) + _shard_shape((M, N), kind)

    def body(pos_ref, g_ref, t_ref, r0_ref, r1_ref, r2_ref, *rest):
        o_ref = rest[-1]
        chip = pos_ref[1]
        own = g_ref[...] + t_ref[...]
        r = [r0_ref[...].astype(F32), r1_ref[...].astype(F32), r2_ref[...].astype(F32)]
        total = None
        for s in range(N_CHIPS):
            rel = jnp.bitwise_xor(chip, s)
            term = jnp.where(rel == 0, own, jnp.where(rel == 2, r[0], jnp.where(rel == 1, r[1], r[2])))
            total = term if total is None else total + term
        o_ref[...] = total

    r_spec = lambda j: pl.BlockSpec((None,) + block, lambda i, pos: (j, i, 0))
    in_specs = [pl.BlockSpec(block, g_map), pl.BlockSpec(block, t_map), r_spec(0), r_spec(1), r_spec(2)]
    args = [pos, grad, theirs, received, received, received]
    aliases = {}
    if reduced is not None:
        in_specs.append(ANY)
        args.append(reduced)
        aliases = {6: 0}
    grid_spec = pltpu.PrefetchScalarGridSpec(
        num_scalar_prefetch=1, grid=(steps,), in_specs=in_specs, out_specs=pl.BlockSpec((None,) + block, o_map))
    return pl.pallas_call(
        body, name=name, grid_spec=grid_spec, out_shape=jax.ShapeDtypeStruct(out_shape, F32),
        input_output_aliases=aliases, compiler_params=_params(1),
    )(*args)


def _sum_devices(gathered):
    R8, C = gathered.shape
    R = R8 // N_DEV
    lanes = C // 128
    tc = 128 * max(k for k in range(1, lanes + 1) if lanes % k == 0 and k * 128 * R8 * 4 <= 2 ** 22)

    def body(g_ref, o_ref):
        total = g_ref[0:R, :]
        for d in range(1, N_DEV):
            total = total + g_ref[d * R:(d + 1) * R, :]
        o_ref[...] = total

    return pl.pallas_call(
        body, name="sum_devices", grid=(C // tc,),
        in_specs=[pl.BlockSpec((R8, tc), lambda j: (0, j))], out_specs=pl.BlockSpec((R, tc), lambda j: (0, j)),
        out_shape=jax.ShapeDtypeStruct((R, C), F32), compiler_params=_params(1),
    )(gathered)


_SMALL = ("pool_w", "pool_scale", "sgu_ln_g", "sgu_ln_b", "sgu_w", "sgu_b", "ln1_g", "ln1_b", "conv_w",
          "conv_b", "ln2_g", "ln2_b", "ada_b")
_WEIGHTS = ("ada_w", "ada_b", "w_in", "pool_w", "pool_scale", "sgu_ln_g", "sgu_ln_b", "sgu_w", "sgu_b", "w_out",
            "ln1_g", "ln1_b", "w_up", "conv_w", "conv_b", "w_down", "ln2_g", "ln2_b")


def kernel(x, c, ada_w, ada_b, w_in, pool_w, pool_scale, sgu_ln_g, sgu_ln_b, sgu_w, sgu_b, w_out, ln1_g, ln1_b, w_up, conv_w, conv_b, w_down, ln2_g, ln2_b, loss_target, m_ada_w, m_ada_b, m_w_in, m_pool_w, m_pool_scale, m_sgu_ln_g, m_sgu_ln_b, m_sgu_w, m_sgu_b, m_w_out, m_ln1_g, m_ln1_b, m_w_up, m_conv_w, m_conv_b, m_w_down, m_ln2_g, m_ln2_b, v_ada_w, v_ada_b, v_w_in, v_pool_w, v_pool_scale, v_sgu_ln_g, v_sgu_ln_b, v_sgu_w, v_sgu_b, v_w_out, v_ln1_g, v_ln1_b, v_w_up, v_conv_w, v_conv_b, v_w_down, v_ln2_g, v_ln2_b):
    weights = dict(ada_w=ada_w, ada_b=ada_b, w_in=w_in, pool_w=pool_w, pool_scale=pool_scale, sgu_ln_g=sgu_ln_g,
                   sgu_ln_b=sgu_ln_b, sgu_w=sgu_w, sgu_b=sgu_b, w_out=w_out, ln1_g=ln1_g, ln1_b=ln1_b, w_up=w_up,
                   conv_w=conv_w, conv_b=conv_b, w_down=w_down, ln2_g=ln2_g, ln2_b=ln2_b)
    mom_m = dict(ada_w=m_ada_w, ada_b=m_ada_b, w_in=m_w_in, pool_w=m_pool_w, pool_scale=m_pool_scale,
                 sgu_ln_g=m_sgu_ln_g, sgu_ln_b=m_sgu_ln_b, sgu_w=m_sgu_w, sgu_b=m_sgu_b, w_out=m_w_out,
                 ln1_g=m_ln1_g, ln1_b=m_ln1_b, w_up=m_w_up, conv_w=m_conv_w, conv_b=m_conv_b, w_down=m_w_down,
                 ln2_g=m_ln2_g, ln2_b=m_ln2_b)
    mom_v = dict(ada_w=v_ada_w, ada_b=v_ada_b, w_in=v_w_in, pool_w=v_pool_w, pool_scale=v_pool_scale,
                 sgu_ln_g=v_sgu_ln_g, sgu_ln_b=v_sgu_ln_b, sgu_w=v_sgu_w, sgu_b=v_sgu_b, w_out=v_w_out,
                 ln1_g=v_ln1_g, ln1_b=v_ln1_b, w_up=v_w_up, conv_w=v_conv_w, conv_b=v_conv_b, w_down=v_w_down,
                 ln2_g=v_ln2_g, ln2_b=v_ln2_b)

    ix, iy, ic = lax.axis_index("x"), lax.axis_index("y"), lax.axis_index("c")
    chip = 2 * ix + iy
    dev = 4 * ix + 2 * iy + ic
    pos = jnp.stack([ic, chip]).astype(jnp.int32)
    xs = x[0]
    target = loss_target[0]
    ff_shard = conv_w.shape[2]
    mod_shard = ada_w.shape[2]

    first = _gather_flat(jnp.concatenate([c.reshape(-1), conv_w.reshape(-1)]), "gather_c_conv")
    c_all = jnp.pad(first[:, :D], ((0, 8), (0, 0)))
    conv_parts = first[0::2, D:].reshape(N_CHIPS, DEPTH, 3, ff_shard)
    conv_full = jnp.transpose(conv_parts, (1, 2, 0, 3)).reshape(DEPTH, 3, D_FF)
    ada_b_cols = lax.dynamic_slice_in_dim(ada_b, chip * mod_shard, mod_shard, axis=1).reshape(DEPTH, 1, mod_shard)
    mod_part = _ada_forward(c_all, ada_w, ada_b_cols)[:, :8, :]
    mod_all = _gather_flat(mod_part.reshape(-1), "gather_mod").reshape(N_DEV, DEPTH, 8, mod_shard)
    mod_mine = lax.dynamic_index_in_dim(mod_all[0::2], dev, axis=2, keepdims=False)
    mod = jnp.transpose(mod_mine, (1, 0, 2)).reshape(DEPTH, 6, 1, D)

    sh_in, sh_out, sh_up, sh_down = (w.astype(WIRE_DTYPE) for w in (w_in, w_out, w_up, w_down))
    wf_in = lax.empty((DEPTH, D, D_IN), WIRE_DTYPE)
    wf_out = lax.empty((DEPTH, D, D), WIRE_DTYPE)
    wf_up = lax.empty((DEPTH, D, 2 * D_FF), WIRE_DTYPE)
    wf_down = lax.empty((DEPTH, D_FF, D), WIRE_DTYPE)
    (wf_in,) = _run_exchange(_gather_exchange(sh_in, wf_in, "cols", 0), "gather_w_in_l0")

    tri = jnp.tril(jnp.ones((CHUNK, CHUNK), dtype=bool))
    ones_row = jnp.ones((1, D), F32)
    zeros_row = jnp.zeros((1, D), F32)
    row = lambda a, l: a[l].reshape(1, -1)

    saved = []
    xn, lg, lb = xs, ones_row, zeros_row
    h = _modulate_input(xs, mod[0, 0], mod[0, 1])
    for l in range(DEPTH):
        shift1, scale1, gate1, shift2, scale2, gate2 = (mod[l, k] for k in range(6))
        nxt = min(l + 1, DEPTH - 1)
        pw = pool_w[l].astype(MXU_DTYPE)
        wm = jnp.where(tri[None], sgu_w[l], 0.0).astype(MXU_DTYPE)
        bias_full = jnp.repeat(jnp.transpose(sgu_b[l]), GROUP, axis=1)
        lg1, lb1 = row(ln1_g, l), row(ln1_b, l)
        proj, (wf_out,) = _matmul(h, wf_in, l, D_IN, STORE_DTYPE, f"f1_in_proj_l{l}",
                                  exchange=_gather_exchange(sh_out, wf_out, "rows", l))
        (mix, f1, xhat1, rstd1, h2), (wf_up,) = _f2_mixers(
            proj, xn, lg, lb, gate1, pw, row(pool_scale, l), row(sgu_ln_g, l), row(sgu_ln_b, l), wm, bias_full,
            wf_out, lg1, lb1, shift2, scale2, l, exchange=_gather_exchange(sh_up, wf_up, "cols", l))
        up, (wf_down,) = _matmul(h2, wf_up, l, D_FF, STORE_DTYPE, f"f3_up_proj_l{l}",
                                 exchange=_gather_exchange(sh_down, wf_down, "rows", l))
        (act, f2, xhat2, rstd2, h_next), filled = _f4_ffn(
            up, xhat1, lg1, lb1, gate2, conv_full[l], row(conv_b, l), wf_down, row(ln2_g, l), row(ln2_b, l),
            mod[nxt, 0], mod[nxt, 1], l,
            exchange=_gather_exchange(sh_in, wf_in, "cols", l + 1) if l + 1 < DEPTH else None)
        if filled:
            (wf_in,) = filled
        saved.append(dict(xn=xn, lg=lg, lb=lb, h=h, proj=proj, mix=mix, f1=f1, xhat1=xhat1, rstd1=rstd1, h2=h2,
                          up=up, act=act, f2=f2, xhat2=xhat2, rstd2=rstd2, pw=pw, wm=wm, bias_full=bias_full))
        xn, lg, lb, h = xhat2, row(ln2_g, l), row(ln2_b, l), h_next

    loss_part, d_out = _loss_head(xn, lg, lb, target)
    loss = lax.psum(loss_part[0, 0], ("x", "y", "c"))

    tags = ("in", "out", "up", "down")
    small = {n: [None] * DEPTH for n in _SMALL}
    dmod = [[None] * 6 for _ in range(DEPTH)]
    below = None
    ready = None
    reduced = [None] * 4

    def partial_sums(layer, full, theirs):
        return [_chip_partial(pos, g, t, kind, f"chip_partial_{tag}_l{layer}")
                for g, t, kind, tag in zip(full, theirs, _SHARD_KINDS, tags)]

    def chip_sums(layer, full, theirs, received):
        return [_reduce_shard(pos, g, t, r, out, kind, layer, f"reduce_shard_{tag}_l{layer}")
                for g, t, r, out, kind, tag in zip(full, theirs, received, reduced, _SHARD_KINDS, tags)]

    for l in reversed(range(DEPTH)):
        sv = saved[l]
        scale1, gate1, scale2, gate2 = mod[l, 1], mod[l, 2], mod[l, 4], mod[l, 5]
        lg1, lb1 = row(ln1_g, l), row(ln1_b, l)
        outs, theirs = _b_ln(d_out, sv["xhat2"], sv["rstd2"], sv["f2"], row(ln2_g, l), row(ln2_b, l), gate2,
                             f"b1_ln2_l{l}", modulated=below,
                             exchange=_swap_exchange(ready[1]) if ready else None)
        df2, dres2, dlg2, dlb2, dmod[l][5] = outs[:5]
        if below is not None:
            dmod[l + 1][1], dmod[l + 1][0] = outs[5], outs[6]
        partials = partial_sums(ready[0], ready[1], theirs) if ready else None
        (dup, dcw, dcb), received = _b2_ffn(df2, wf_down, sv["up"], conv_full[l], row(conv_b, l), l,
                                            exchange=_scatter_exchange(partials) if ready else None)
        if ready:
            reduced = chip_sums(ready[0], ready[1], theirs, received)
        g_down = _weight_grad(sv["act"], [df2], D, f"wg_down_l{l}")
        dh2, shared = _matmul_nt(dup, wf_up, l, 1408, f"b3_up_l{l}",
                                 exchange=_share_exchange(reduced, ready[0]) if ready else None)
        if ready:
            reduced = shared
        g_up = _weight_grad(sv["h2"], [dup], 1408, f"wg_up_l{l}", rows=2 * TS_TN)
        (df1, dres1, dlg1, dlb1, dmod[l][2], dmod[l][4], dmod[l][3]), _ = _b_ln(
            dh2, sv["xhat1"], sv["rstd1"], sv["f1"], lg1, lb1, gate1, f"b4_ln1_l{l}", modulated=(dres2, scale2))
        dproj, dpw, dps, dsg, dsb, dwm, dbias = _b5_mixers(
            df1, wf_out, sv["proj"], sv["pw"], row(pool_scale, l), row(sgu_ln_g, l), row(sgu_ln_b, l), sv["wm"],
            sv["bias_full"], l)
        g_out = _weight_grad(sv["mix"], [df1], D, f"wg_out_l{l}", rows=2 * TS_TN)
        d_out, _ = _matmul_nt(dproj, wf_in, l, D_IN, f"b6_in_l{l}")
        g_in = _weight_grad(sv["h"], [dproj], 768, f"wg_in_l{l}", rows=2 * TS_TN)
        ready = (l, [g_in, g_out, g_up, g_down])
        below = (dres1, scale1)
        small["pool_w"][l], small["pool_scale"][l] = dpw, dps[0]
        small["sgu_ln_g"][l], small["sgu_ln_b"][l], small["sgu_w"][l] = dsg[0], dsb[0], dwm
        small["sgu_b"][l] = jnp.transpose(jnp.sum(dbias.reshape(CHUNK, HEADS, GROUP), axis=2))
        small["ln1_g"][l], small["ln1_b"][l] = dlg1[0], dlb1[0]
        small["conv_w"][l], small["conv_b"][l] = dcw, dcb[0]
        small["ln2_g"][l], small["ln2_b"][l] = dlg2[0], dlb2[0]
    grad_x2d, dmod[0][1], dmod[0][0] = _input_grad(d_out, below[0], below[1], xs)
    grad_x = grad_x2d[None]
    small["ada_b"] = [jnp.concatenate([part[0] for part in dmod[l]]) for l in range(DEPTH)]

    names = _SMALL
    flat = jnp.concatenate([jnp.stack(small[n]).reshape(-1) for n in names])
    n_small = flat.shape[0]
    padded = -(-n_small // 1024) * 1024
    gathered = _all_gather8(jnp.pad(flat, (0, padded - n_small)).reshape(8, padded // 8), "gather_small_grads")
    summed = _sum_devices(gathered).reshape(-1)[:n_small]
    grads = {}
    offset = 0
    for n in names:
        size = math.prod(weights[n].shape[1:]) * DEPTH if n != "conv_w" else DEPTH * 3 * D_FF
        grads[n] = summed[offset:offset + size]
        offset += size
    grads["conv_w"] = lax.dynamic_slice_in_dim(grads["conv_w"].reshape(DEPTH, 3, D_FF), chip * ff_shard, ff_shard,
                                               axis=2)
    for n in names:
        grads[n] = grads[n].reshape(weights[n].shape)
    dmod_all = gathered.reshape(N_DEV, padded)[:, n_small - DEPTH * 6 * D:n_small].reshape(N_DEV, DEPTH, 6 * D)
    dmod_cols = lax.dynamic_slice_in_dim(jnp.transpose(dmod_all, (1, 0, 2)), chip * mod_shard, mod_shard, axis=2)
    grads["ada_w"] = _ada_backward(c_all, jnp.pad(dmod_cols, ((0, 0), (0, 8), (0, 0))))

    theirs = _run_exchange(_swap_exchange(ready[1]), "swap_halves_l0")
    partials = partial_sums(0, ready[1], theirs)
    received = _run_exchange(_scatter_exchange(partials), "scatter_partials_l0")
    reduced = chip_sums(0, ready[1], theirs, received)
    grads["w_in"], grads["w_out"], grads["w_up"], grads["w_down"] = _run_exchange(
        _share_exchange(reduced, 0), "share_reduced_l0")

    delta, new_m, new_v = {}, {}, {}
    for n in ("ada_w", "w_in", "w_out", "w_up", "w_down"):
        shape = weights[n].shape
        two_d = (shape[0] * shape[1], shape[2])
        d_, m_, v_ = _adamw(weights[n].reshape(two_d), grads[n].reshape(two_d), mom_m[n].reshape(two_d),
                            mom_v[n].reshape(two_d), f"adamw_{n}")
        delta[n], new_m[n], new_v[n] = d_.reshape(shape), m_.reshape(shape), v_.reshape(shape)
    sizes = [math.prod(weights[n].shape) for n in names]
    total = sum(sizes)
    padded = -(-total // 1024) * 1024
    pack = lambda d: jnp.pad(jnp.concatenate([d[n].reshape(-1) for n in names]), (0, padded - total)).reshape(8, -1)
    v_pack = jnp.pad(jnp.concatenate([mom_v[n].reshape(-1) for n in names]), (0, padded - total),
                     constant_values=1.0).reshape(8, -1)
    d_, m_, v_ = _adamw(pack(weights), pack(grads), pack(mom_m), v_pack, "adamw_small")
    offset = 0
    for n, size in zip(names, sizes):
        for src, dst in ((d_, delta), (m_, new_m), (v_, new_v)):
            dst[n] = src.reshape(-1)[offset:offset + size].reshape(weights[n].shape)
        offset += size

    return (loss, grad_x, *[grads[n] for n in _WEIGHTS], *[delta[n] for n in _WEIGHTS],
            *[new_m[n] for n in _WEIGHTS], *[new_v[n] for n in _WEIGHTS])
```

```python
import functools
import math

import jax
import jax.numpy as jnp
from jax import lax
from jax.experimental import pallas as pl
from jax.experimental.pallas import tpu as pltpu

F32 = jnp.float32
MXU_DTYPE = jnp.bfloat16
WIRE_DTYPE = jnp.bfloat16

DEPTH = 4
D = 1024
D_POOL = 512
D_SGU = 512
N_GROUPS = 4
GROUP = 128
POOL_WINDOWS = (2, 4, 8, 16)
POOL_HALO = 16
CHUNK = 128
HEADS = 4
D_IN = D_POOL + 2 * D_SGU
D_FF = 2816
CONV_HALO = 16
STORE_DTYPE = jnp.bfloat16
N_CHIPS = 4
N_DEV = 8
ALPHA = (2.0 * DEPTH) ** 0.25
LN_EPS = 1e-5
ADAM_LR, ADAM_B1, ADAM_B2, ADAM_EPS, ADAM_WD, ADAM_STEP = 0.001, 0.9, 0.999, 1e-08, 0.01, 10

TS_PROJ = 512
TS_MM = 1024
TS_MIX = 512
TS_FF = 256
TS_TN = 1024
VMEM_LIMIT = 52 * 2 ** 20

MESH = pl.DeviceIdType.MESH
ANY = pl.BlockSpec(memory_space=pl.ANY)

_GELU_K0 = math.sqrt(2.0 / math.pi)
_GELU_K1 = 0.044715


def _params(n_axes):
    return pltpu.CompilerParams(dimension_semantics=("arbitrary",) * n_axes, vmem_limit_bytes=VMEM_LIMIT)


class _Exchange:
    def __init__(self, inputs, inplace, fresh, n_copies, copies):
        self.inputs, self.inplace, self.fresh, self.n_copies, self.copies = inputs, inplace, fresh, n_copies, copies

    def out_shapes(self):
        return [jax.ShapeDtypeStruct(self.inputs[i].shape, self.inputs[i].dtype) for i in self.inplace] + list(self.fresh)

    def semaphores(self):
        return [pltpu.SemaphoreType.DMA((self.n_copies,)), pltpu.SemaphoreType.DMA((self.n_copies,))]

    def aliases(self, first_input, first_output):
        return {first_input + i: first_output + k for k, i in enumerate(self.inplace)}


def _call(body, *, name, grid, in_specs, out_specs, out_shape, args, scratch_shapes=(), exchange=None):
    n_axes = len(grid)
    in_specs, out_specs, out_shape, scratch_shapes = list(in_specs), list(out_specs), list(out_shape), list(scratch_shapes)
    if exchange is None:
        outs = pl.pallas_call(body, name=name, grid=grid, in_specs=in_specs, out_specs=out_specs, out_shape=out_shape,
                              scratch_shapes=scratch_shapes, compiler_params=_params(n_axes))(*args)
        return list(outs), []
    n_in, n_out, n_scr = len(in_specs), len(out_shape), len(scratch_shapes)
    x_out = exchange.out_shapes()
    n_xin, n_xout = len(exchange.inputs), len(x_out)

    def hosted(*refs):
        ins, xin = refs[:n_in], refs[n_in:n_in + n_xin]
        o = n_in + n_xin
        outs, xout = refs[o:o + n_out], refs[o + n_out:o + n_out + n_xout]
        s = o + n_out + n_xout
        scr, (send_sems, recv_sems) = refs[s:s + n_scr], refs[s + n_scr:]
        first = functools.reduce(jnp.logical_and, [pl.program_id(d) == 0 for d in range(n_axes)])
        last = functools.reduce(jnp.logical_and, [pl.program_id(d) == pl.num_programs(d) - 1 for d in range(n_axes)])

        @pl.when(first)
        def _():
            for cp in exchange.copies(xin, xout, send_sems, recv_sems)[0]:
                cp.start()

        body(*ins, *outs, *scr)

        @pl.when(last)
        def _():
            sends, arrivals = exchange.copies(xin, xout, send_sems, recv_sems)
            for cp in arrivals:
                cp.wait_recv()
            for cp in sends:
                cp.wait_send()

    results = pl.pallas_call(
        hosted, name=name, grid=grid, in_specs=in_specs + [ANY] * n_xin, out_specs=out_specs + [ANY] * n_xout,
        out_shape=out_shape + x_out, scratch_shapes=scratch_shapes + exchange.semaphores(),
        input_output_aliases=exchange.aliases(n_in, n_out), compiler_params=_params(n_axes),
    )(*args, *exchange.inputs)
    return list(results[:n_out]), list(results[n_out:])


def _run_exchange(exchange, name):
    x_out = exchange.out_shapes()
    n_xin, n_xout = len(exchange.inputs), len(x_out)

    def body(*refs):
        xin, xout = refs[:n_xin], refs[n_xin:n_xin + n_xout]
        send_sems, recv_sems = refs[n_xin + n_xout:]
        sends, arrivals = exchange.copies(xin, xout, send_sems, recv_sems)
        for cp in sends:
            cp.start()
        for cp in arrivals:
            cp.wait_recv()
        for cp in sends:
            cp.wait_send()

    return list(pl.pallas_call(
        body, name=name, in_specs=[ANY] * n_xin, out_specs=[ANY] * n_xout, out_shape=x_out,
        scratch_shapes=exchange.semaphores(), input_output_aliases=exchange.aliases(0, 0),
    )(*exchange.inputs))


def _dot(a, b):
    return jnp.dot(a, b, preferred_element_type=F32)


def _dot_nt(a, b):
    return lax.dot_general(a, b, (((1,), (1,)), ((), ())), preferred_element_type=F32)


def _dot_tn(a, b):
    return lax.dot_general(a, b, (((0,), (0,)), ((), ())), preferred_element_type=F32)


def _gelu(x):
    x2 = x * x
    t = jnp.tanh(x * (x2 * (_GELU_K0 * _GELU_K1) + _GELU_K0))
    cdf = 0.5 * t + 0.5
    dg = cdf + (x * (1.0 - t * t)) * (x2 * (1.5 * _GELU_K0 * _GELU_K1) + 0.5 * _GELU_K0)
    return x * cdf, dg


def _colsum(x):
    return jnp.sum(x, axis=0, keepdims=True)


def _row(d):
    return pl.BlockSpec((1, d), lambda *_: (0, 0))


def _full(shape):
    n = len(shape)
    return pl.BlockSpec(shape, lambda *_: (0,) * n)


def _layer_block(shape, layer):
    n = len(shape)
    return pl.BlockSpec((None,) + tuple(shape), lambda *_: (layer,) + (0,) * n)


def _pool_counts(first_row, rows, window):
    t = first_row + lax.broadcasted_iota(jnp.int32, (rows, 1), 0)
    return 1.0 / jnp.minimum(t + 1, window).astype(F32)


def _pool_forward(a_ext, a, first_row, rows):
    out = []
    for g, window in enumerate(POOL_WINDOWS):
        s = a_ext[:, g * GROUP:(g + 1) * GROUP]
        k = 1
        while k < window:
            s = s + pltpu.roll(s, k, 0)
            k *= 2
        inv = _pool_counts(first_row, rows, window)
        out.append(s[POOL_HALO:] * inv - a[:, g * GROUP:(g + 1) * GROUP])
    return out


def _sgu_norm(v):
    gv, dgv = _gelu(v)
    mu = jnp.mean(gv, axis=-1, keepdims=True)
    xc = gv - mu
    var = jnp.mean(xc * xc, axis=-1, keepdims=True)
    rs = lax.rsqrt(var + LN_EPS)
    return xc * rs, rs, dgv


def _modulate_input(x, shift, scale):
    S = x.shape[0]
    ts = min(TS_PROJ, S)

    def body(x_ref, sh_ref, sc_ref, h_ref):
        h_ref[...] = (x_ref[...] * (1.0 + sc_ref[...]) + sh_ref[...]).astype(MXU_DTYPE)

    tile = pl.BlockSpec((ts, D), lambda i: (i, 0))
    return pl.pallas_call(
        body, name="modulate_input", grid=(S // ts,), in_specs=[tile, _row(D), _row(D)], out_specs=tile,
        out_shape=jax.ShapeDtypeStruct((S, D), MXU_DTYPE), compiler_params=_params(1),
    )(x, shift, scale)


def _matmul(a, w_all, layer, tn, out_dtype, name, exchange=None):
    S, K = a.shape
    N = w_all.shape[2]
    ts = min(TS_MM, S)

    def body(a_ref, w_ref, o_ref):
        o_ref[...] = _dot(a_ref[...], w_ref[...]).astype(out_dtype)

    (out,), extra = _call(
        body, name=name, grid=(N // tn, S // ts),
        in_specs=[pl.BlockSpec((ts, K), lambda j, i: (i, 0)), pl.BlockSpec((None, K, tn), lambda j, i: (layer, 0, j))],
        out_specs=[pl.BlockSpec((ts, tn), lambda j, i: (i, j))],
        out_shape=[jax.ShapeDtypeStruct((S, N), out_dtype)], args=(a, w_all), exchange=exchange)
    return out, extra


def _matmul_nt(a, w_all, layer, tk, name, exchange=None):
    S, K = a.shape
    N = w_all.shape[1]
    ts = min(TS_MM, S)
    n_k = K // tk

    def body(a_ref, w_ref, o_ref, acc):
        k = pl.program_id(1)
        part = _dot_nt(a_ref[...], w_ref[...])
        if n_k == 1:
            o_ref[...] = part
        else:
            @pl.when(k == 0)
            def _():
                acc[...] = part

            @pl.when(jnp.logical_and(k > 0, k < n_k - 1))
            def _():
                acc[...] += part

            @pl.when(k == n_k - 1)
            def _():
                o_ref[...] = acc[...] + part

    (out,), extra = _call(
        body, name=name, grid=(S // ts, n_k),
        in_specs=[pl.BlockSpec((ts, tk), lambda i, k: (i, k)), pl.BlockSpec((None, N, tk), lambda i, k: (layer, 0, k))],
        out_specs=[pl.BlockSpec((ts, N), lambda i, k: (i, 0))],
        out_shape=[jax.ShapeDtypeStruct((S, N), F32)],
        scratch_shapes=[pltpu.VMEM((ts, N), F32)], args=(a, w_all), exchange=exchange)
    return out, extra


def _f2_mixers(proj, xn, lg, lb, gate, pool_w, pool_scale, sg, sb, wm, bias_full, w_out_all, lg1, lb1, shift2, scale2,
               layer, exchange=None):
    S = xn.shape[0]
    ts = min(TS_MIX, S)
    n_chunks = ts // CHUNK
    halo_blocks = ts // POOL_HALO

    def body(p_ref, halo_ref, xn_ref, lg_ref, lb_ref, gate_ref, pw_ref, ps_ref, sg_ref, sb_ref, wm_ref,
             bias_ref, wo_ref, lg1_ref, lb1_ref, sh2_ref, sc2_ref, mix_ref, f_ref, xh_ref, rs_ref, h2_ref, z_scr):
        i = pl.program_id(0)
        a = p_ref[:, 0:D_POOL].astype(F32)
        u = p_ref[:, D_POOL:D_POOL + D_SGU].astype(F32)
        v = p_ref[:, D_POOL + D_SGU:D_IN].astype(F32)
        halo = halo_ref[...].astype(F32) * (i > 0).astype(F32)
        a_ext = jnp.concatenate([halo, a], axis=0)
        pooled = _pool_forward(a_ext, a, i * ts, ts)
        for g in range(N_GROUPS):
            mixed = _dot(pooled[g].astype(MXU_DTYPE), pw_ref[g])
            mix_ref[:, g * GROUP:(g + 1) * GROUP] = (mixed * ps_ref[:, g * GROUP:(g + 1) * GROUP]).astype(MXU_DTYPE)
        gu, _ = _gelu(u)
        vhat, _, _ = _sgu_norm(v)
        vn = (vhat * sg_ref[...] + sb_ref[...]).astype(MXU_DTYPE)
        for c in range(n_chunks):
            for h in range(HEADS):
                blk = vn[c * CHUNK:(c + 1) * CHUNK, h * GROUP:(h + 1) * GROUP]
                z_scr[c * CHUNK:(c + 1) * CHUNK, h * GROUP:(h + 1) * GROUP] = (
                    _dot(wm_ref[h], blk) + bias_ref[:, h * GROUP:(h + 1) * GROUP])
        mix_ref[:, D_POOL:D] = (gu * z_scr[...]).astype(MXU_DTYPE)
        f = _dot(mix_ref[...], wo_ref[...])
        f_ref[...] = f.astype(STORE_DTYPE)
        x = xn_ref[...] * lg_ref[...] + lb_ref[...]
        z1 = ALPHA * x + gate_ref[...] * f
        mu = jnp.mean(z1, axis=-1, keepdims=True)
        zc = z1 - mu
        var = jnp.mean(zc * zc, axis=-1, keepdims=True)
        rs = lax.rsqrt(var + LN_EPS)
        xhat = zc * rs
        xh_ref[...] = xhat
        rs_ref[...] = rs
        x1 = xhat * lg1_ref[...] + lb1_ref[...]
        h2_ref[...] = (x1 * (1.0 + sc2_ref[...]) + sh2_ref[...]).astype(MXU_DTYPE)

    tile = lambda w: pl.BlockSpec((ts, w), lambda i: (i, 0))
    return _call(
        body, name=f"f2_mixers_l{layer}", grid=(S // ts,),
        in_specs=[tile(D_IN),
                  pl.BlockSpec((POOL_HALO, D_POOL), lambda i: (jnp.maximum(i * halo_blocks - 1, 0), 0)),
                  tile(D), _row(D), _row(D), _row(D),
                  _full((N_GROUPS, GROUP, GROUP)), _row(D_POOL), _row(D_SGU), _row(D_SGU),
                  _full((HEADS, CHUNK, CHUNK)), _full((CHUNK, D_SGU)), _layer_block((D, D), layer),
                  _row(D), _row(D), _row(D), _row(D)],
        out_specs=[tile(D), tile(D), tile(D), tile(1), tile(D)],
        out_shape=[jax.ShapeDtypeStruct((S, D), MXU_DTYPE), jax.ShapeDtypeStruct((S, D), STORE_DTYPE),
                   jax.ShapeDtypeStruct((S, D), F32), jax.ShapeDtypeStruct((S, 1), F32),
                   jax.ShapeDtypeStruct((S, D), MXU_DTYPE)],
        scratch_shapes=[pltpu.VMEM((ts, D_SGU), F32)],
        args=(proj, proj, xn, lg, lb, gate, pool_w, pool_scale, sg, sb, wm, bias_full, w_out_all, lg1, lb1, shift2,
              scale2), exchange=exchange)


def _conv_forward(g_ext, cw_ref, cb_ref):
    gm2 = pltpu.roll(g_ext, 2, 0)[CONV_HALO:]
    gm1 = pltpu.roll(g_ext, 1, 0)[CONV_HALO:]
    g0 = g_ext[CONV_HALO:]
    gc = ((cb_ref[...] + gm2 * cw_ref[0:1, :]) + gm1 * cw_ref[1:2, :]) + g0 * cw_ref[2:3, :]
    return gc, gm2, gm1


def _f4_ffn(up, xhat1, lg, lb, gate, conv_w, conv_b, w_down_all, lg2, lb2, shift_next, scale_next, layer,
            exchange=None):
    S = xhat1.shape[0]
    ts = min(TS_FF, S)
    halo_blocks = ts // CONV_HALO

    def body(g_ref, halo_ref, val_ref, xh_ref, lg_ref, lb_ref, gate_ref, cw_ref, cb_ref, wd_ref, lg2_ref, lb2_ref,
             shn_ref, scn_ref, act_ref, f_ref, xo_ref, rs_ref, hn_ref, ge_ref, dge_ref):
        i = pl.program_id(0)
        halo = halo_ref[...].astype(F32) * (i > 0).astype(F32)
        g_ext = jnp.concatenate([halo, g_ref[...].astype(F32)], axis=0)
        gc, _, _ = _conv_forward(g_ext, cw_ref, cb_ref)
        ge, dge = _gelu(gc)
        ge_ref[...] = ge.astype(STORE_DTYPE)
        dge_ref[...] = dge.astype(STORE_DTYPE)
        act = (ge * val_ref[...].astype(F32)).astype(MXU_DTYPE)
        act_ref[...] = act
        f = _dot(act, wd_ref[...])
        f_ref[...] = f.astype(STORE_DTYPE)
        x = xh_ref[...] * lg_ref[...] + lb_ref[...]
        z = ALPHA * x + gate_ref[...] * f
        mu = jnp.mean(z, axis=-1, keepdims=True)
        zc = z - mu
        var = jnp.mean(zc * zc, axis=-1, keepdims=True)
        rs = lax.rsqrt(var + LN_EPS)
        xhat = zc * rs
        xo_ref[...] = xhat
        rs_ref[...] = rs
        x2 = xhat * lg2_ref[...] + lb2_ref[...]
        hn_ref[...] = (x2 * (1.0 + scn_ref[...]) + shn_ref[...]).astype(MXU_DTYPE)

    tile = lambda w: pl.BlockSpec((ts, w), lambda i: (i, 0))
    return _call(
        body, name=f"f4_ffn_l{layer}", grid=(S // ts,),
        in_specs=[pl.BlockSpec((ts, D_FF), lambda i: (i, 0)),
                  pl.BlockSpec((CONV_HALO, D_FF), lambda i: (jnp.maximum(i * halo_blocks - 1, 0), 0)),
                  pl.BlockSpec((ts, D_FF), lambda i: (i, 1)),
                  tile(D), _row(D), _row(D), _row(D), _full((3, D_FF)), _row(D_FF),
                  _layer_block((D_FF, D), layer), _row(D), _row(D), _row(D), _row(D)],
        out_specs=[tile(D_FF), tile(D), tile(D), tile(1), tile(D), tile(D_FF), tile(D_FF)],
        out_shape=[jax.ShapeDtypeStruct((S, D_FF), MXU_DTYPE), jax.ShapeDtypeStruct((S, D), STORE_DTYPE),
                   jax.ShapeDtypeStruct((S, D), F32), jax.ShapeDtypeStruct((S, 1), F32),
                   jax.ShapeDtypeStruct((S, D), MXU_DTYPE), jax.ShapeDtypeStruct((S, D_FF), STORE_DTYPE),
                   jax.ShapeDtypeStruct((S, D_FF), STORE_DTYPE)],
        args=(up, up, up, xhat1, lg, lb, gate, conv_w, conv_b, w_down_all, lg2, lb2, shift_next, scale_next),
        exchange=exchange)


def _loss_head(xhat, lg, lb, target):
    S = xhat.shape[0]
    ts = min(TS_PROJ, S)

    def body(xh_ref, lg_ref, lb_ref, t_ref, loss_ref, dy_ref, acc):
        i = pl.program_id(0)

        @pl.when(i == 0)
        def _():
            acc[...] = jnp.zeros_like(acc)

        err = (xh_ref[...] * lg_ref[...] + lb_ref[...]) - t_ref[...]
        dy_ref[...] = err * (1.0 / D)
        acc[...] += _colsum(err * err)

        @pl.when(i == pl.num_programs(0) - 1)
        def _():
            loss_ref[...] = jnp.sum(acc[...], axis=1, keepdims=True) * (0.5 / D)

    tile = pl.BlockSpec((ts, D), lambda i: (i, 0))
    return pl.pallas_call(
        body, name="loss_head", grid=(S // ts,),
        in_specs=[tile, _row(D), _row(D), tile],
        out_specs=[_full((1, 1)), tile],
        out_shape=[jax.ShapeDtypeStruct((1, 1), F32), jax.ShapeDtypeStruct((S, D), F32)],
        scratch_shapes=[pltpu.VMEM((1, D), F32)],
        compiler_params=_params(1),
    )(xhat, lg, lb, target)


def _accumulate(i, ref, value):
    @pl.when(i == 0)
    def _():
        ref[...] = value

    @pl.when(i > 0)
    def _():
        ref[...] += value


def _b_ln(d, xhat, rstd, f, lg, lb, gate, name, modulated=None, exchange=None):
    S = d.shape[0]
    ts = min(TS_PROJ, S)
    has_mod = modulated is not None

    def body(*refs):
        if has_mod:
            (d_ref, xh_ref, rs_ref, f_ref, lg_ref, lb_ref, gate_ref, dri_ref, sc_ref,
             df_ref, dres_ref, dlg_ref, dlb_ref, dgate_ref, dsc_ref, dsh_ref) = refs
        else:
            (d_ref, xh_ref, rs_ref, f_ref, lg_ref, lb_ref, gate_ref,
             df_ref, dres_ref, dlg_ref, dlb_ref, dgate_ref) = refs
        i = pl.program_id(0)
        xh = xh_ref[...]
        if has_mod:
            dh = d_ref[...]
            dxo_t = dh * (1.0 + sc_ref[...]) + dri_ref[...]
            _accumulate(i, dsc_ref, _colsum(dh * (xh * lg_ref[...] + lb_ref[...])))
            _accumulate(i, dsh_ref, _colsum(dh))
        else:
            dxo_t = d_ref[...]
        dxh = dxo_t * lg_ref[...]
        m1 = jnp.mean(dxh, axis=-1, keepdims=True)
        m2 = jnp.mean(dxh * xh, axis=-1, keepdims=True)
        dz = rs_ref[...] * (dxh - m1 - xh * m2)
        df_ref[...] = (dz * gate_ref[...]).astype(MXU_DTYPE)
        dres_ref[...] = ALPHA * dz
        _accumulate(i, dlg_ref, _colsum(dxo_t * xh))
        _accumulate(i, dlb_ref, _colsum(dxo_t))
        _accumulate(i, dgate_ref, _colsum(dz * f_ref[...].astype(F32)))

    tile = lambda w: pl.BlockSpec((ts, w), lambda i: (i, 0))
    n_sums = 5 if has_mod else 3
    in_specs = [tile(D), tile(D), tile(1), tile(D), _row(D), _row(D), _row(D)]
    args = [d, xhat, rstd, f, lg, lb, gate]
    if has_mod:
        in_specs += [tile(D), _row(D)]
        args += list(modulated)
    return _call(
        body, name=name, grid=(S // ts,), in_specs=in_specs,
        out_specs=[tile(D), tile(D)] + [_row(D)] * n_sums,
        out_shape=[jax.ShapeDtypeStruct((S, D), MXU_DTYPE), jax.ShapeDtypeStruct((S, D), F32)]
        + [jax.ShapeDtypeStruct((1, D), F32)] * n_sums,
        args=args, exchange=exchange)


def _input_grad(dh, dres, scale, x):
    S = dh.shape[0]
    ts = min(TS_PROJ, S)

    def body(dh_ref, dres_ref, sc_ref, x_ref, dx_ref, dsc_ref, dsh_ref):
        i = pl.program_id(0)
        dh_t = dh_ref[...]
        dx_ref[...] = dh_t * (1.0 + sc_ref[...]) + dres_ref[...]
        _accumulate(i, dsc_ref, _colsum(dh_t * x_ref[...]))
        _accumulate(i, dsh_ref, _colsum(dh_t))

    tile = pl.BlockSpec((ts, D), lambda i: (i, 0))
    return pl.pallas_call(
        body, name="input_grad", grid=(S // ts,), in_specs=[tile, tile, _row(D), tile],
        out_specs=[tile, _row(D), _row(D)],
        out_shape=[jax.ShapeDtypeStruct((S, D), F32)] + [jax.ShapeDtypeStruct((1, D), F32)] * 2,
        compiler_params=_params(1),
    )(dh, dres, scale, x)


def _b2_ffn(df2, w_down_all, up, ge, dge, conv_w, layer, exchange=None):
    S = df2.shape[0]
    ts = min(TS_FF, S)
    n_tiles = S // ts
    n_ext = ts + CONV_HALO

    def body(df_ref, wd_ref, g_ref, val_ref, ge_ref, dge_ref, cw_ref, dup_ref, dcw_ref, dcb_ref, next_dgc):
        i = pl.program_id(0)

        @pl.when(i == 0)
        def _():
            next_dgc[...] = jnp.zeros_like(next_dgc)

        dact = _dot_nt(df_ref[...], wd_ref[...])
        dup_ref[:, D_FF:2 * D_FF] = (dact * ge_ref[...].astype(F32)).astype(MXU_DTYPE)
        dgc = dact * val_ref[...].astype(F32) * dge_ref[...].astype(F32)
        ext = jnp.concatenate([dgc, next_dgc[...]], axis=0)
        dp1 = pltpu.roll(ext, n_ext - 1, 0)[:ts]
        dp2 = pltpu.roll(ext, n_ext - 2, 0)[:ts]
        dup_ref[:, 0:D_FF] = (dgc * cw_ref[2:3, :] + dp1 * cw_ref[1:2, :] + dp2 * cw_ref[0:1, :]).astype(MXU_DTYPE)
        next_dgc[...] = dgc[0:CONV_HALO]
        g = g_ref[...].astype(F32)
        dcw = jnp.concatenate([_colsum(dp2 * g), _colsum(dp1 * g), _colsum(dgc * g)], axis=0)
        _accumulate(i, dcw_ref, dcw)
        _accumulate(i, dcb_ref, _colsum(dgc))

    tile = lambda w, col=0: pl.BlockSpec((ts, w), lambda i: (n_tiles - 1 - i, col))
    return _call(
        body, name=f"b2_ffn_l{layer}", grid=(n_tiles,),
        in_specs=[tile(D), _layer_block((D_FF, D), layer), tile(D_FF), tile(D_FF, 1), tile(D_FF), tile(D_FF),
                  _full((3, D_FF))],
        out_specs=[tile(2 * D_FF), _full((3, D_FF)), _row(D_FF)],
        out_shape=[jax.ShapeDtypeStruct((S, 2 * D_FF), MXU_DTYPE),
                   jax.ShapeDtypeStruct((3, D_FF), F32), jax.ShapeDtypeStruct((1, D_FF), F32)],
        scratch_shapes=[pltpu.VMEM((CONV_HALO, D_FF), F32)],
        args=(df2, w_down_all, up, up, ge, dge, conv_w), exchange=exchange)


def _b5_mixers(df1, w_out_all, proj, pool_w, pool_scale, sg, sb, wm, bias_full, layer):
    S = df1.shape[0]
    ts = min(TS_MIX, S)
    n_chunks = ts // CHUNK
    halo_blocks = ts // POOL_HALO
    last_halo = S // POOL_HALO - 1

    def body(df_ref, dfh_ref, wo_ref, p_ref, ah_ref, pw_ref, ps_ref, sg_ref, sb_ref, wm_ref, bias_ref,
             dp_ref, dpw_ref, dps_ref, dsg_ref, dsb_ref, dwm_ref, dbias_ref, z_scr, dvn_scr):
        i = pl.program_id(0)
        last = pl.num_programs(0) - 1
        dmix = _dot_nt(df_ref[...], wo_ref[...])
        dmix_halo = _dot_nt(dfh_ref[...], wo_ref[0:D_POOL, :]) * (i < last).astype(F32)

        a = p_ref[:, 0:D_POOL].astype(F32)
        halo = ah_ref[...].astype(F32) * (i > 0).astype(F32)
        pooled = _pool_forward(jnp.concatenate([halo, a], axis=0), a, i * ts, ts)
        n = ts + POOL_HALO
        dps_parts = []
        for g, window in enumerate(POOL_WINDOWS):
            cols = slice(g * GROUP, (g + 1) * GROUP)
            pooled_b = pooled[g].astype(MXU_DTYPE)
            mixed = _dot(pooled_b, pw_ref[g])
            dya = dmix[:, cols]
            dps_parts.append(_colsum(dya * mixed))
            dmixed = (dya * ps_ref[:, cols]).astype(MXU_DTYPE)
            dpw_g = _dot_tn(pooled_b, dmixed)

            @pl.when(i == 0)
            def _():
                dpw_ref[g] = dpw_g

            @pl.when(i > 0)
            def _():
                dpw_ref[g] += dpw_g

            dpooled = _dot_nt(dmixed, pw_ref[g])
            dmixed_h = (dmix_halo[:, cols] * ps_ref[:, cols]).astype(MXU_DTYPE)
            dpooled_h = _dot_nt(dmixed_h, pw_ref[g])
            q = dpooled * _pool_counts(i * ts, ts, window)
            s = jnp.concatenate([q, dpooled_h * (1.0 / window)], axis=0)
            k = 1
            while k < window:
                s = s + pltpu.roll(s, n - k, 0)
                k *= 2
            dp_ref[:, cols] = (s[:ts] - dpooled).astype(MXU_DTYPE)
        _accumulate(i, dps_ref, jnp.concatenate(dps_parts, axis=1))

        u = p_ref[:, D_POOL:D_POOL + D_SGU].astype(F32)
        v = p_ref[:, D_POOL + D_SGU:D_IN].astype(F32)
        gu, dgu = _gelu(u)
        vhat, rs, dgv = _sgu_norm(v)
        vn = (vhat * sg_ref[...] + sb_ref[...]).astype(MXU_DTYPE)
        dyb = dmix[:, D_POOL:D]
        dz = dyb * gu
        dzb = dz.astype(MXU_DTYPE)
        dbias = dz[0:CHUNK]
        for c in range(1, n_chunks):
            dbias = dbias + dz[c * CHUNK:(c + 1) * CHUNK]
        _accumulate(i, dbias_ref, dbias)
        for h in range(HEADS):
            cols = slice(h * GROUP, (h + 1) * GROUP)
            dwm_h = None
            for c in range(n_chunks):
                rows = slice(c * CHUNK, (c + 1) * CHUNK)
                vn_blk = vn[rows, cols]
                dz_blk = dzb[rows, cols]
                z_scr[rows, cols] = _dot(wm_ref[h], vn_blk) + bias_ref[:, cols]
                dvn_scr[rows, cols] = _dot_tn(wm_ref[h], dz_blk)
                part = _dot_nt(dz_blk, vn_blk)
                dwm_h = part if dwm_h is None else dwm_h + part

            @pl.when(i == 0)
            def _():
                dwm_ref[h] = dwm_h

            @pl.when(i > 0)
            def _():
                dwm_ref[h] += dwm_h

        dp_ref[:, D_POOL:D_POOL + D_SGU] = (dyb * z_scr[...] * dgu).astype(MXU_DTYPE)
        dvn = dvn_scr[...]
        _accumulate(i, dsg_ref, _colsum(dvn * vhat))
        _accumulate(i, dsb_ref, _colsum(dvn))
        dvh = dvn * sg_ref[...]
        m1 = jnp.mean(dvh, axis=-1, keepdims=True)
        m2 = jnp.mean(dvh * vhat, axis=-1, keepdims=True)
        dp_ref[:, D_POOL + D_SGU:D_IN] = (rs * (dvh - m1 - vhat * m2) * dgv).astype(MXU_DTYPE)

        @pl.when(i == last)
        def _():
            tri = (lax.broadcasted_iota(jnp.int32, (CHUNK, CHUNK), 0)
                   >= lax.broadcasted_iota(jnp.int32, (CHUNK, CHUNK), 1))
            for h in range(HEADS):
                dwm_ref[h] = jnp.where(tri, dwm_ref[h], 0.0)

    tile = lambda w: pl.BlockSpec((ts, w), lambda i: (i, 0))
    return pl.pallas_call(
        body, name=f"b5_mixers_l{layer}", grid=(S // ts,),
        in_specs=[tile(D),
                  pl.BlockSpec((POOL_HALO, D), lambda i: (jnp.minimum((i + 1) * halo_blocks, last_halo), 0)),
                  _layer_block((D, D), layer), tile(D_IN),
                  pl.BlockSpec((POOL_HALO, D_POOL), lambda i: (jnp.maximum(i * halo_blocks - 1, 0), 0)),
                  _full((N_GROUPS, GROUP, GROUP)), _row(D_POOL), _row(D_SGU), _row(D_SGU),
                  _full((HEADS, CHUNK, CHUNK)), _full((CHUNK, D_SGU))],
        out_specs=[tile(D_IN), _full((N_GROUPS, GROUP, GROUP)), _row(D_POOL), _row(D_SGU), _row(D_SGU),
                   _full((HEADS, CHUNK, CHUNK)), _full((CHUNK, D_SGU))],
        out_shape=[jax.ShapeDtypeStruct((S, D_IN), MXU_DTYPE), jax.ShapeDtypeStruct((N_GROUPS, GROUP, GROUP), F32),
                   jax.ShapeDtypeStruct((1, D_POOL), F32), jax.ShapeDtypeStruct((1, D_SGU), F32),
                   jax.ShapeDtypeStruct((1, D_SGU), F32), jax.ShapeDtypeStruct((HEADS, CHUNK, CHUNK), F32),
                   jax.ShapeDtypeStruct((CHUNK, D_SGU), F32)],
        scratch_shapes=[pltpu.VMEM((ts, D_SGU), F32), pltpu.VMEM((ts, D_SGU), F32)],
        compiler_params=_params(1),
    )(df1, df1, w_out_all, proj, proj, pool_w, pool_scale, sg, sb, wm, bias_full)


def _weight_grad(a, bs, tn, name, rows=TS_TN):
    S, M = a.shape
    counts = [b.shape[1] // tn for b in bs]
    starts = [sum(counts[:k]) for k in range(len(bs))]
    ts = min(rows, S)

    def body(a_ref, *refs):
        b_refs, o_ref = refs[:-1], refs[-1]
        j = pl.program_id(0)
        for b_ref, lo, n in zip(b_refs, starts, counts):
            @pl.when(jnp.logical_and(j >= lo, j < lo + n))
            def _():
                _accumulate(pl.program_id(1), o_ref, _dot_tn(a_ref[...], b_ref[...]))

    b_spec = lambda lo, n: pl.BlockSpec((ts, tn), lambda j, i: (i, jnp.clip(j - lo, 0, n - 1)))
    return pl.pallas_call(
        body, name=name, grid=(sum(counts), S // ts),
        in_specs=[pl.BlockSpec((ts, M), lambda j, i: (i, 0))] + [b_spec(lo, n) for lo, n in zip(starts, counts)],
        out_specs=pl.BlockSpec((M, tn), lambda j, i: (0, j)),
        out_shape=jax.ShapeDtypeStruct((M, sum(counts) * tn), F32), compiler_params=_params(2),
    )(a, *bs)


def _silu(x):
    return x * (1.0 / (1.0 + jnp.exp(-x)))


def _ada_forward(c_all, ada_w, ada_b_cols):
    n_cols = ada_w.shape[2]
    tc = 512

    def body(c_ref, w_ref, b_ref, o_ref):
        ca = _silu(c_ref[...]).astype(MXU_DTYPE)
        o_ref[...] = _dot(ca, w_ref[...].astype(MXU_DTYPE)) + b_ref[...]

    return pl.pallas_call(
        body, name="ada_forward", grid=(DEPTH, n_cols // tc),
        in_specs=[pl.BlockSpec((16, D), lambda l, j: (0, 0)), pl.BlockSpec((None, D, tc), lambda l, j: (l, 0, j)),
                  pl.BlockSpec((None, 1, tc), lambda l, j: (l, 0, j))],
        out_specs=pl.BlockSpec((None, 16, tc), lambda l, j: (l, 0, j)),
        out_shape=jax.ShapeDtypeStruct((DEPTH, 16, n_cols), F32),
        compiler_params=_params(2),
    )(c_all, ada_w, ada_b_cols)


def _ada_backward(c_all, dmod_cols):
    n_cols = dmod_cols.shape[2]
    tc = 512

    def body(c_ref, d_ref, o_ref):
        ca = _silu(c_ref[...]).astype(MXU_DTYPE)
        o_ref[...] = _dot_tn(ca, d_ref[...].astype(MXU_DTYPE))

    return pl.pallas_call(
        body, name="ada_backward", grid=(DEPTH, n_cols // tc),
        in_specs=[pl.BlockSpec((16, D), lambda l, j: (0, 0)), pl.BlockSpec((None, 16, tc), lambda l, j: (l, 0, j))],
        out_specs=pl.BlockSpec((None, D, tc), lambda l, j: (l, 0, j)),
        out_shape=jax.ShapeDtypeStruct((DEPTH, D, n_cols), F32),
        compiler_params=_params(2),
    )(c_all, dmod_cols)


def _adamw(w, g, m, v, name):
    R, C = w.shape
    tr = R
    for cand in (512, 256, 128, 64, 32, 16, 8):
        if R % cand == 0 and cand * C * 4 <= 2 ** 21:
            tr = cand
            break
    c1 = 1.0 - ADAM_B1 ** ADAM_STEP
    c2 = 1.0 - ADAM_B2 ** ADAM_STEP

    def body(w_ref, g_ref, m_ref, v_ref, d_ref, mo_ref, vo_ref):
        gg = g_ref[...]
        mn = ADAM_B1 * m_ref[...] + (1.0 - ADAM_B1) * gg
        vn = ADAM_B2 * v_ref[...] + (1.0 - ADAM_B2) * (gg * gg)
        mo_ref[...] = mn
        vo_ref[...] = vn
        d_ref[...] = -ADAM_LR * ((mn / c1) / (jnp.sqrt(vn / c2) + ADAM_EPS) + ADAM_WD * w_ref[...])

    tile = pl.BlockSpec((tr, C), lambda i: (i, 0))
    return pl.pallas_call(
        body, name=name, grid=(R // tr,), in_specs=[tile] * 4, out_specs=[tile] * 3,
        out_shape=[jax.ShapeDtypeStruct((R, C), F32)] * 3, compiler_params=_params(1),
    )(w, g, m, v)


def _position():
    x, y, c = lax.axis_index("x"), lax.axis_index("y"), lax.axis_index("c")
    other_chips = [(1 - x, y), (x, 1 - y), (1 - x, 1 - y)]
    return x, y, c, other_chips


def _all_gather8(block, name):
    R, C = block.shape

    def body(x_ref, out_ref, send_sems, recv_sems, local_sem):
        x, y, c, chips = _position()
        me, sibling = (x, y, c), (x, y, 1 - c)

        def rows(px, py, pc):
            return out_ref.at[pl.ds((4 * px + 2 * py + pc) * R, R), :]

        def copy(k, blk, to, src=None):
            return pltpu.make_async_remote_copy(
                src_ref=rows(*blk) if src is None else src, dst_ref=rows(*blk),
                send_sem=send_sems.at[k], recv_sem=recv_sems.at[k], device_id=to, device_id_type=MESH)

        mine = pltpu.make_async_copy(x_ref, rows(*me), local_sem)
        mine.start()
        first = [copy(0, me, sibling, src=x_ref)]
        first += [copy(1 + j, me, (*chip, c), src=x_ref) for j, chip in enumerate(chips)]
        for cp in first:
            cp.start()
        passed = [copy(4 + j, (*chip, c), sibling) for j, chip in enumerate(chips)]
        for j, chip in enumerate(chips):
            copy(1 + j, (*chip, c), me).wait_recv()
            passed[j].start()
        copy(0, sibling, me).wait_recv()
        for j, chip in enumerate(chips):
            copy(4 + j, (*chip, 1 - c), me).wait_recv()
        for cp in first + passed:
            cp.wait_send()
        mine.wait()

    return pl.pallas_call(
        body, name=name, out_shape=jax.ShapeDtypeStruct((N_DEV * R, C), block.dtype),
        in_specs=[pl.BlockSpec(memory_space=pltpu.VMEM)], out_specs=ANY,
        scratch_shapes=[pltpu.SemaphoreType.DMA((7,)), pltpu.SemaphoreType.DMA((7,)), pltpu.SemaphoreType.DMA(())],
    )(block)


def _gather_flat(vec, name):
    n = vec.shape[0]
    padded = -(-n // 1024) * 1024
    block = jnp.pad(vec, (0, padded - n)).reshape(8, padded // 8)
    out = _all_gather8(block, name)
    return out.reshape(N_DEV, padded)[:, :n]


_SHARD_KINDS = ("cols", "rows", "cols", "rows")


def _shard_of(ref, shape, kind, chip):
    m, n = shape
    return ref.at[:, pl.ds(chip * (n // 4), n // 4)] if kind == "cols" else ref.at[pl.ds(chip * (m // 4), m // 4), :]


def _half_of(ref, shape, kind, h):
    m, n = shape
    return ref.at[pl.ds(h * (m // 2), m // 2), :] if kind == "cols" else ref.at[:, pl.ds(h * (n // 2), n // 2)]


def _half_shape(shape, kind):
    m, n = shape
    return (m // 2, n) if kind == "cols" else (m, n // 2)


def _shard_shape(shape, kind):
    m, n = shape
    return (m, n // 4) if kind == "cols" else (m // 4, n)


def _remote(src, dst, k, to, send_sems, recv_sems):
    return pltpu.make_async_remote_copy(src_ref=src, dst_ref=dst, send_sem=send_sems.at[k], recv_sem=recv_sems.at[k],
                                        device_id=to, device_id_type=MESH)


def _gather_exchange(shards, full, kind, layer):
    shape = full.shape[1:]

    def copies(xin, xout, send_sems, recv_sems):
        x, y, c, chips = _position()
        my_chip = 2 * x + y
        own = xin[0].at[layer]
        place = lambda chip: _shard_of(xout[0].at[layer], shape, kind, chip)
        peers = [(x, y, 1 - c)] + [(*chip, c) for chip in chips]
        sources = [my_chip] + [2 * cx + cy for cx, cy in chips]
        sends = [_remote(own, place(my_chip), k, peer, send_sems, recv_sems) for k, peer in enumerate(peers)]
        arrivals = [_remote(place(s), place(s), k, (x, y, c), send_sems, recv_sems) for k, s in enumerate(sources)]
        return sends, arrivals

    return _Exchange([shards, full], [1], [], 4, copies)


def _swap_exchange(grads):
    shapes = [g.shape for g in grads]
    fresh = [jax.ShapeDtypeStruct(_half_shape(s, k), F32) for s, k in zip(shapes, _SHARD_KINDS)]

    def copies(xin, xout, send_sems, recv_sems):
        x, y, c, _ = _position()
        sends = [_remote(_half_of(xin[a], shapes[a], _SHARD_KINDS[a], 1 - c), xout[a], a, (x, y, 1 - c),
                         send_sems, recv_sems) for a in range(len(grads))]
        arrivals = [_remote(xout[a], xout[a], a, (x, y, c), send_sems, recv_sems) for a in range(len(grads))]
        return sends, arrivals

    return _Exchange(list(grads), [], fresh, len(grads), copies)


def _scatter_exchange(partials):
    n_w = len(partials)
    shapes = [p.shape for p in partials]
    fresh = [jax.ShapeDtypeStruct((3,) + _shard_shape(s, k), p.dtype) for s, k, p in zip(shapes, _SHARD_KINDS, partials)]

    def copies(xin, xout, send_sems, recv_sems):
        x, y, c, chips = _position()
        sends, arrivals = [], []
        for j, (cx, cy) in enumerate(chips):
            for a in range(n_w):
                src = _shard_of(xin[a], shapes[a], _SHARD_KINDS[a], 2 * cx + cy)
                sends.append(_remote(src, xout[a].at[j], j * n_w + a, (cx, cy, c), send_sems, recv_sems))
                arrivals.append(_remote(xout[a].at[j], xout[a].at[j], j * n_w + a, (x, y, c), send_sems, recv_sems))
        return sends, arrivals

    return _Exchange(list(partials), [], fresh, 3 * n_w, copies)


def _share_exchange(reduced, layer):
    n_w = len(reduced)
    shapes = [r.shape[1:] for r in reduced]

    def copies(xin, xout, send_sems, recv_sems):
        x, y, c, _ = _position()
        half = lambda a, h: _half_of(xout[a].at[layer], shapes[a], _SHARD_KINDS[a], h)
        sends = [_remote(half(a, c), half(a, c), a, (x, y, 1 - c), send_sems, recv_sems) for a in range(n_w)]
        arrivals = [_remote(half(a, 1 - c), half(a, 1 - c), a, (x, y, c), send_sems, recv_sems) for a in range(n_w)]
        return sends, arrivals

    return _Exchange(list(reduced), list(range(n_w)), [], n_w, copies)


def _chip_partial(pos, grad, theirs, kind, name):
    M, N = grad.shape
    if kind == "cols":
        tm = 256
        steps = M // 2 // tm
        block, g_map = (tm, N), (lambda i, pos: (pos[0] * steps + i, 0))
    else:
        tm = M // 4
        steps = 4
        block, g_map = (tm, N // 2), (lambda i, pos: (i, pos[0]))

    def body(pos_ref, g_ref, t_ref, o_ref):
        o_ref[...] = (g_ref[...] + t_ref[...]).astype(WIRE_DTYPE)

    grid_spec = pltpu.PrefetchScalarGridSpec(
        num_scalar_prefetch=1, grid=(steps,),
        in_specs=[pl.BlockSpec(block, g_map), pl.BlockSpec(block, lambda i, pos: (i, 0))],
        out_specs=pl.BlockSpec(block, lambda i, pos: (i, 0)))
    return pl.pallas_call(
        body, name=name, grid_spec=grid_spec, out_shape=jax.ShapeDtypeStruct(_half_shape((M, N), kind), WIRE_DTYPE),
        compiler_params=_params(1),
    )(pos, grad, theirs)


def _reduce_shard(pos, grad, theirs, received, reduced, kind, layer, name):
    M, N = grad.shape
    if kind == "cols":
        tm = min(M // 2, 512)
        steps = M // 2 // tm
        block = (tm, N // 4)
        g_map = lambda i, pos: (pos[0] * steps + i, pos[1])
        t_map = lambda i, pos: (i, pos[1])
        o_map = lambda i, pos: (layer, pos[0] * steps + i, 0)
    else:
        steps = 1
        block = (M // 4, N // 2)
        g_map = lambda i, pos: (pos[1], pos[0])
        t_map = lambda i, pos: (pos[1], 0)
        o_map = lambda i, pos: (layer, 0, pos[0])
    out_shape = (DEPTH,) + _shard_shape((M, N), kind)

    def body(pos_ref, g_ref, t_ref, r0_ref, r1_ref, r2_ref, *rest):
        o_ref = rest[-1]
        chip = pos_ref[1]
        own = g_ref[...] + t_ref[...]
        r = [r0_ref[...].astype(F32), r1_ref[...].astype(F32), r2_ref[...].astype(F32)]
        total = None
        for s in range(N_CHIPS):
            rel = jnp.bitwise_xor(chip, s)
            term = jnp.where(rel == 0, own, jnp.where(rel == 2, r[0], jnp.where(rel == 1, r[1], r[2])))
            total = term if total is None else total + term
        o_ref[...] = total

    r_spec = lambda j: pl.BlockSpec((None,) + block, lambda i, pos: (j, i, 0))
    in_specs = [pl.BlockSpec(block, g_map), pl.BlockSpec(block, t_map), r_spec(0), r_spec(1), r_spec(2)]
    args = [pos, grad, theirs, received, received, received]
    aliases = {}
    if reduced is not None:
        in_specs.append(ANY)
        args.append(reduced)
        aliases = {6: 0}
    grid_spec = pltpu.PrefetchScalarGridSpec(
        num_scalar_prefetch=1, grid=(steps,), in_specs=in_specs, out_specs=pl.BlockSpec((None,) + block, o_map))
    return pl.pallas_call(
        body, name=name, grid_spec=grid_spec, out_shape=jax.ShapeDtypeStruct(out_shape, F32),
        input_output_aliases=aliases, compiler_params=_params(1),
    )(*args)


def _sum_devices(gathered):
    R8, C = gathered.shape
    R = R8 // N_DEV
    lanes = C // 128
    tc = 128 * max(k for k in range(1, lanes + 1) if lanes % k == 0 and k * 128 * R8 * 4 <= 2 ** 22)

    def body(g_ref, o_ref):
        total = g_ref[0:R, :]
        for d in range(1, N_DEV):
            total = total + g_ref[d * R:(d + 1) * R, :]
        o_ref[...] = total

    return pl.pallas_call(
        body, name="sum_devices", grid=(C // tc,),
        in_specs=[pl.BlockSpec((R8, tc), lambda j: (0, j))], out_specs=pl.BlockSpec((R, tc), lambda j: (0, j)),
        out_shape=jax.ShapeDtypeStruct((R, C), F32), compiler_params=_params(1),
    )(gathered)


_SMALL = ("pool_w", "pool_scale", "sgu_ln_g", "sgu_ln_b", "sgu_w", "sgu_b", "ln1_g", "ln1_b", "conv_w",
          "conv_b", "ln2_g", "ln2_b", "ada_b")
_WEIGHTS = ("ada_w", "ada_b", "w_in", "pool_w", "pool_scale", "sgu_ln_g", "sgu_ln_b", "sgu_w", "sgu_b", "w_out",
            "ln1_g", "ln1_b", "w_up", "conv_w", "conv_b", "w_down", "ln2_g", "ln2_b")


def kernel(x, c, ada_w, ada_b, w_in, pool_w, pool_scale, sgu_ln_g, sgu_ln_b, sgu_w, sgu_b, w_out, ln1_g, ln1_b, w_up, conv_w, conv_b, w_down, ln2_g, ln2_b, loss_target, m_ada_w, m_ada_b, m_w_in, m_pool_w, m_pool_scale, m_sgu_ln_g, m_sgu_ln_b, m_sgu_w, m_sgu_b, m_w_out, m_ln1_g, m_ln1_b, m_w_up, m_conv_w, m_conv_b, m_w_down, m_ln2_g, m_ln2_b, v_ada_w, v_ada_b, v_w_in, v_pool_w, v_pool_scale, v_sgu_ln_g, v_sgu_ln_b, v_sgu_w, v_sgu_b, v_w_out, v_ln1_g, v_ln1_b, v_w_up, v_conv_w, v_conv_b, v_w_down, v_ln2_g, v_ln2_b):
    weights = dict(ada_w=ada_w, ada_b=ada_b, w_in=w_in, pool_w=pool_w, pool_scale=pool_scale, sgu_ln_g=sgu_ln_g,
                   sgu_ln_b=sgu_ln_b, sgu_w=sgu_w, sgu_b=sgu_b, w_out=w_out, ln1_g=ln1_g, ln1_b=ln1_b, w_up=w_up,
                   conv_w=conv_w, conv_b=conv_b, w_down=w_down, ln2_g=ln2_g, ln2_b=ln2_b)
    mom_m = dict(ada_w=m_ada_w, ada_b=m_ada_b, w_in=m_w_in, pool_w=m_pool_w, pool_scale=m_pool_scale,
                 sgu_ln_g=m_sgu_ln_g, sgu_ln_b=m_sgu_ln_b, sgu_w=m_sgu_w, sgu_b=m_sgu_b, w_out=m_w_out,
                 ln1_g=m_ln1_g, ln1_b=m_ln1_b, w_up=m_w_up, conv_w=m_conv_w, conv_b=m_conv_b, w_down=m_w_down,
                 ln2_g=m_ln2_g, ln2_b=m_ln2_b)
    mom_v = dict(ada_w=v_ada_w, ada_b=v_ada_b, w_in=v_w_in, pool_w=v_pool_w, pool_scale=v_pool_scale,
                 sgu_ln_g=v_sgu_ln_g, sgu_ln_b=v_sgu_ln_b, sgu_w=v_sgu_w, sgu_b=v_sgu_b, w_out=v_w_out,
                 ln1_g=v_ln1_g, ln1_b=v_ln1_b, w_up=v_w_up, conv_w=v_conv_w, conv_b=v_conv_b, w_down=v_w_down,
                 ln2_g=v_ln2_g, ln2_b=v_ln2_b)

    ix, iy, ic = lax.axis_index("x"), lax.axis_index("y"), lax.axis_index("c")
    chip = 2 * ix + iy
    dev = 4 * ix + 2 * iy + ic
    pos = jnp.stack([ic, chip]).astype(jnp.int32)
    xs = x[0]
    target = loss_target[0]
    ff_shard = conv_w.shape[2]
    mod_shard = ada_w.shape[2]

    first = _gather_flat(jnp.concatenate([c.reshape(-1), conv_w.reshape(-1)]), "gather_c_conv")
    c_all = jnp.pad(first[:, :D], ((0, 8), (0, 0)))
    conv_parts = first[0::2, D:].reshape(N_CHIPS, DEPTH, 3, ff_shard)
    conv_full = jnp.transpose(conv_parts, (1, 2, 0, 3)).reshape(DEPTH, 3, D_FF)
    ada_b_cols = lax.dynamic_slice_in_dim(ada_b, chip * mod_shard, mod_shard, axis=1).reshape(DEPTH, 1, mod_shard)
    mod_part = _ada_forward(c_all, ada_w, ada_b_cols)[:, :8, :]
    mod_all = _gather_flat(mod_part.reshape(-1), "gather_mod").reshape(N_DEV, DEPTH, 8, mod_shard)
    mod_mine = lax.dynamic_index_in_dim(mod_all[0::2], dev, axis=2, keepdims=False)
    mod = jnp.transpose(mod_mine, (1, 0, 2)).reshape(DEPTH, 6, 1, D)

    sh_in, sh_out, sh_up, sh_down = (w.astype(WIRE_DTYPE) for w in (w_in, w_out, w_up, w_down))
    wf_in = lax.empty((DEPTH, D, D_IN), WIRE_DTYPE)
    wf_out = lax.empty((DEPTH, D, D), WIRE_DTYPE)
    wf_up = lax.empty((DEPTH, D, 2 * D_FF), WIRE_DTYPE)
    wf_down = lax.empty((DEPTH, D_FF, D), WIRE_DTYPE)
    (wf_in,) = _run_exchange(_gather_exchange(sh_in, wf_in, "cols", 0), "gather_w_in_l0")

    tri = jnp.tril(jnp.ones((CHUNK, CHUNK), dtype=bool))
    ones_row = jnp.ones((1, D), F32)
    zeros_row = jnp.zeros((1, D), F32)
    row = lambda a, l: a[l].reshape(1, -1)

    saved = []
    xn, lg, lb = xs, ones_row, zeros_row
    h = _modulate_input(xs, mod[0, 0], mod[0, 1])
    for l in range(DEPTH):
        shift1, scale1, gate1, shift2, scale2, gate2 = (mod[l, k] for k in range(6))
        nxt = min(l + 1, DEPTH - 1)
        pw = pool_w[l].astype(MXU_DTYPE)
        wm = jnp.where(tri[None], sgu_w[l], 0.0).astype(MXU_DTYPE)
        bias_full = jnp.repeat(jnp.transpose(sgu_b[l]), GROUP, axis=1)
        lg1, lb1 = row(ln1_g, l), row(ln1_b, l)
        proj, (wf_out,) = _matmul(h, wf_in, l, D_IN, STORE_DTYPE, f"f1_in_proj_l{l}",
                                  exchange=_gather_exchange(sh_out, wf_out, "rows", l))
        (mix, f1, xhat1, rstd1, h2), (wf_up,) = _f2_mixers(
            proj, xn, lg, lb, gate1, pw, row(pool_scale, l), row(sgu_ln_g, l), row(sgu_ln_b, l), wm, bias_full,
            wf_out, lg1, lb1, shift2, scale2, l, exchange=_gather_exchange(sh_up, wf_up, "cols", l))
        up, (wf_down,) = _matmul(h2, wf_up, l, D_FF, STORE_DTYPE, f"f3_up_proj_l{l}",
                                 exchange=_gather_exchange(sh_down, wf_down, "rows", l))
        (act, f2, xhat2, rstd2, h_next, ge, dge), filled = _f4_ffn(
            up, xhat1, lg1, lb1, gate2, conv_full[l], row(conv_b, l), wf_down, row(ln2_g, l), row(ln2_b, l),
            mod[nxt, 0], mod[nxt, 1], l,
            exchange=_gather_exchange(sh_in, wf_in, "cols", l + 1) if l + 1 < DEPTH else None)
        if filled:
            (wf_in,) = filled
        saved.append(dict(xn=xn, lg=lg, lb=lb, h=h, proj=proj, mix=mix, f1=f1, xhat1=xhat1, rstd1=rstd1, h2=h2,
                          up=up, act=act, f2=f2, xhat2=xhat2, rstd2=rstd2, pw=pw, wm=wm, bias_full=bias_full,
                          ge=ge, dge=dge))
        xn, lg, lb, h = xhat2, row(ln2_g, l), row(ln2_b, l), h_next

    loss_part, d_out = _loss_head(xn, lg, lb, target)
    loss = lax.psum(loss_part[0, 0], ("x", "y", "c"))

    tags = ("in", "out", "up", "down")
    small = {n: [None] * DEPTH for n in _SMALL}
    dmod = [[None] * 6 for _ in range(DEPTH)]
    below = None
    ready = None
    reduced = [None] * 4

    def partial_sums(layer, full, theirs):
        return [_chip_partial(pos, g, t, kind, f"chip_partial_{tag}_l{layer}")
                for g, t, kind, tag in zip(full, theirs, _SHARD_KINDS, tags)]

    def chip_sums(layer, full, theirs, received):
        return [_reduce_shard(pos, g, t, r, out, kind, layer, f"reduce_shard_{tag}_l{layer}")
                for g, t, r, out, kind, tag in zip(full, theirs, received, reduced, _SHARD_KINDS, tags)]

    for l in reversed(range(DEPTH)):
        sv = saved[l]
        scale1, gate1, scale2, gate2 = mod[l, 1], mod[l, 2], mod[l, 4], mod[l, 5]
        lg1, lb1 = row(ln1_g, l), row(ln1_b, l)
        outs, theirs = _b_ln(d_out, sv["xhat2"], sv["rstd2"], sv["f2"], row(ln2_g, l), row(ln2_b, l), gate2,
                             f"b1_ln2_l{l}", modulated=below,
                             exchange=_swap_exchange(ready[1]) if ready else None)
        df2, dres2, dlg2, dlb2, dmod[l][5] = outs[:5]
        if below is not None:
            dmod[l + 1][1], dmod[l + 1][0] = outs[5], outs[6]
        partials = partial_sums(ready[0], ready[1], theirs) if ready else None
        (dup, dcw, dcb), received = _b2_ffn(df2, wf_down, sv["up"], sv["ge"], sv["dge"], conv_full[l], l,
                                            exchange=_scatter_exchange(partials) if ready else None)
        if ready:
            reduced = chip_sums(ready[0], ready[1], theirs, received)
        g_down = _weight_grad(sv["act"], [df2], D, f"wg_down_l{l}")
        dh2, shared = _matmul_nt(dup, wf_up, l, 1408, f"b3_up_l{l}",
                                 exchange=_share_exchange(reduced, ready[0]) if ready else None)
        if ready:
            reduced = shared
        g_up = _weight_grad(sv["h2"], [dup], 1408, f"wg_up_l{l}", rows=2 * TS_TN)
        (df1, dres1, dlg1, dlb1, dmod[l][2], dmod[l][4], dmod[l][3]), _ = _b_ln(
            dh2, sv["xhat1"], sv["rstd1"], sv["f1"], lg1, lb1, gate1, f"b4_ln1_l{l}", modulated=(dres2, scale2))
        dproj, dpw, dps, dsg, dsb, dwm, dbias = _b5_mixers(
            df1, wf_out, sv["proj"], sv["pw"], row(pool_scale, l), row(sgu_ln_g, l), row(sgu_ln_b, l), sv["wm"],
            sv["bias_full"], l)
        g_out = _weight_grad(sv["mix"], [df1], D, f"wg_out_l{l}", rows=2 * TS_TN)
        d_out, _ = _matmul_nt(dproj, wf_in, l, D_IN, f"b6_in_l{l}")
        g_in = _weight_grad(sv["h"], [dproj], 768, f"wg_in_l{l}", rows=2 * TS_TN)
        ready = (l, [g_in, g_out, g_up, g_down])
        below = (dres1, scale1)
        small["pool_w"][l], small["pool_scale"][l] = dpw, dps[0]
        small["sgu_ln_g"][l], small["sgu_ln_b"][l], small["sgu_w"][l] = dsg[0], dsb[0], dwm
        small["sgu_b"][l] = jnp.transpose(jnp.sum(dbias.reshape(CHUNK, HEADS, GROUP), axis=2))
        small["ln1_g"][l], small["ln1_b"][l] = dlg1[0], dlb1[0]
        small["conv_w"][l], small["conv_b"][l] = dcw, dcb[0]
        small["ln2_g"][l], small["ln2_b"][l] = dlg2[0], dlb2[0]
    grad_x2d, dmod[0][1], dmod[0][0] = _input_grad(d_out, below[0], below[1], xs)
    grad_x = grad_x2d[None]
    small["ada_b"] = [jnp.concatenate([part[0] for part in dmod[l]]) for l in range(DEPTH)]

    names = _SMALL
    flat = jnp.concatenate([jnp.stack(small[n]).reshape(-1) for n in names])
    n_small = flat.shape[0]
    padded = -(-n_small // 1024) * 1024
    gathered = _all_gather8(jnp.pad(flat, (0, padded - n_small)).reshape(8, padded // 8), "gather_small_grads")
    summed = _sum_devices(gathered).reshape(-1)[:n_small]
    grads = {}
    offset = 0
    for n in names:
        size = math.prod(weights[n].shape[1:]) * DEPTH if n != "conv_w" else DEPTH * 3 * D_FF
        grads[n] = summed[offset:offset + size]
        offset += size
    grads["conv_w"] = lax.dynamic_slice_in_dim(grads["conv_w"].reshape(DEPTH, 3, D_FF), chip * ff_shard, ff_shard,
                                               axis=2)
    for n in names:
        grads[n] = grads[n].reshape(weights[n].shape)
    dmod_all = gathered.reshape(N_DEV, padded)[:, n_small - DEPTH * 6 * D:n_small].reshape(N_DEV, DEPTH, 6 * D)
    dmod_cols = lax.dynamic_slice_in_dim(jnp.transpose(dmod_all, (1, 0, 2)), chip * mod_shard, mod_shard, axis=2)
    grads["ada_w"] = _ada_backward(c_all, jnp.pad(dmod_cols, ((0, 0), (0, 8), (0, 0))))

    theirs = _run_exchange(_swap_exchange(ready[1]), "swap_halves_l0")
    partials = partial_sums(0, ready[1], theirs)
    received = _run_exchange(_scatter_exchange(partials), "scatter_partials_l0")
    reduced = chip_sums(0, ready[1], theirs, received)
    grads["w_in"], grads["w_out"], grads["w_up"], grads["w_down"] = _run_exchange(
        _share_exchange(reduced, 0), "share_reduced_l0")

    delta, new_m, new_v = {}, {}, {}
    for n in ("ada_w", "w_in", "w_out", "w_up", "w_down"):
        shape = weights[n].shape
        two_d = (shape[0] * shape[1], shape[2])
        d_, m_, v_ = _adamw(weights[n].reshape(two_d), grads[n].reshape(two_d), mom_m[n].reshape(two_d),
                            mom_v[n].reshape(two_d), f"adamw_{n}")
        delta[n], new_m[n], new_v[n] = d_.reshape(shape), m_.reshape(shape), v_.reshape(shape)
    sizes = [math.prod(weights[n].shape) for n in names]
    total = sum(sizes)
    padded = -(-total // 1024) * 1024
    pack = lambda d: jnp.pad(jnp.concatenate([d[n].reshape(-1) for n in names]), (0, padded - total)).reshape(8, -1)
    v_pack = jnp.pad(jnp.concatenate([mom_v[n].reshape(-1) for n in names]), (0, padded - total),
                     constant_values=1.0).reshape(8, -1)
    d_, m_, v_ = _adamw(pack(weights), pack(grads), pack(mom_m), v_pack, "adamw_small")
    offset = 0
    for n, size in zip(names, sizes):
        for src, dst in ((d_, delta), (m_, new_m), (v_, new_v)):
            dst[n] = src.reshape(-1)[offset:offset + size].reshape(weights[n].shape)
        offset += size

    return (loss, grad_x, *[grads[n] for n in _WEIGHTS], *[delta[n] for n in _WEIGHTS],
            *[new_m[n] for n in _WEIGHTS], *[new_v[n] for n in _WEIGHTS])
```

```python
import functools
import math

import jax
import jax.numpy as jnp
from jax import lax
from jax.experimental import pallas as pl
from jax.experimental.pallas import tpu as pltpu

F32 = jnp.float32
MXU_DTYPE = jnp.bfloat16
WIRE_DTYPE = jnp.bfloat16

DEPTH = 4
D = 1024
D_POOL = 512
D_SGU = 512
N_GROUPS = 4
GROUP = 128
POOL_WINDOWS = (2, 4, 8, 16)
POOL_HALO = 16
CHUNK = 128
HEADS = 4
D_IN = D_POOL + 2 * D_SGU
D_FF = 2816
CONV_HALO = 16
STORE_DTYPE = jnp.bfloat16
N_CHIPS = 4
N_DEV = 8
ALPHA = (2.0 * DEPTH) ** 0.25
LN_EPS = 1e-5
ADAM_LR, ADAM_B1, ADAM_B2, ADAM_EPS, ADAM_WD, ADAM_STEP = 0.001, 0.9, 0.999, 1e-08, 0.01, 10

TS_PROJ = 512
TS_MM = 1024
TS_MIX = 512
TS_FF = 256
TS_TN = 1024
VMEM_LIMIT = 52 * 2 ** 20

MESH = pl.DeviceIdType.MESH
ANY = pl.BlockSpec(memory_space=pl.ANY)

_GELU_K0 = math.sqrt(2.0 / math.pi)
_GELU_K1 = 0.044715


def _params(n_axes):
    return pltpu.CompilerParams(dimension_semantics=("arbitrary",) * n_axes, vmem_limit_bytes=VMEM_LIMIT)


class _Exchange:
    def __init__(self, inputs, inplace, fresh, n_copies, copies):
        self.inputs, self.inplace, self.fresh, self.n_copies, self.copies = inputs, inplace, fresh, n_copies, copies

    def out_shapes(self):
        return [jax.ShapeDtypeStruct(self.inputs[i].shape, self.inputs[i].dtype) for i in self.inplace] + list(self.fresh)

    def semaphores(self):
        return [pltpu.SemaphoreType.DMA((self.n_copies,)), pltpu.SemaphoreType.DMA((self.n_copies,))]

    def aliases(self, first_input, first_output):
        return {first_input + i: first_output + k for k, i in enumerate(self.inplace)}


def _call(body, *, name, grid, in_specs, out_specs, out_shape, args, scratch_shapes=(), exchange=None):
    n_axes = len(grid)
    in_specs, out_specs, out_shape, scratch_shapes = list(in_specs), list(out_specs), list(out_shape), list(scratch_shapes)
    if exchange is None:
        outs = pl.pallas_call(body, name=name, grid=grid, in_specs=in_specs, out_specs=out_specs, out_shape=out_shape,
                              scratch_shapes=scratch_shapes, compiler_params=_params(n_axes))(*args)
        return list(outs), []
    n_in, n_out, n_scr = len(in_specs), len(out_shape), len(scratch_shapes)
    x_out = exchange.out_shapes()
    n_xin, n_xout = len(exchange.inputs), len(x_out)

    def hosted(*refs):
        ins, xin = refs[:n_in], refs[n_in:n_in + n_xin]
        o = n_in + n_xin
        outs, xout = refs[o:o + n_out], refs[o + n_out:o + n_out + n_xout]
        s = o + n_out + n_xout
        scr, (send_sems, recv_sems) = refs[s:s + n_scr], refs[s + n_scr:]
        first = functools.reduce(jnp.logical_and, [pl.program_id(d) == 0 for d in range(n_axes)])
        last = functools.reduce(jnp.logical_and, [pl.program_id(d) == pl.num_programs(d) - 1 for d in range(n_axes)])

        @pl.when(first)
        def _():
            for cp in exchange.copies(xin, xout, send_sems, recv_sems)[0]:
                cp.start()

        body(*ins, *outs, *scr)

        @pl.when(last)
        def _():
            sends, arrivals = exchange.copies(xin, xout, send_sems, recv_sems)
            for cp in arrivals:
                cp.wait_recv()
            for cp in sends:
                cp.wait_send()

    results = pl.pallas_call(
        hosted, name=name, grid=grid, in_specs=in_specs + [ANY] * n_xin, out_specs=out_specs + [ANY] * n_xout,
        out_shape=out_shape + x_out, scratch_shapes=scratch_shapes + exchange.semaphores(),
        input_output_aliases=exchange.aliases(n_in, n_out), compiler_params=_params(n_axes),
    )(*args, *exchange.inputs)
    return list(results[:n_out]), list(results[n_out:])


def _run_exchange(exchange, name):
    x_out = exchange.out_shapes()
    n_xin, n_xout = len(exchange.inputs), len(x_out)

    def body(*refs):
        xin, xout = refs[:n_xin], refs[n_xin:n_xin + n_xout]
        send_sems, recv_sems = refs[n_xin + n_xout:]
        sends, arrivals = exchange.copies(xin, xout, send_sems, recv_sems)
        for cp in sends:
            cp.start()
        for cp in arrivals:
            cp.wait_recv()
        for cp in sends:
            cp.wait_send()

    return list(pl.pallas_call(
        body, name=name, in_specs=[ANY] * n_xin, out_specs=[ANY] * n_xout, out_shape=x_out,
        scratch_shapes=exchange.semaphores(), input_output_aliases=exchange.aliases(0, 0),
    )(*exchange.inputs))


def _dot(a, b):
    return jnp.dot(a, b, preferred_element_type=F32)


def _dot_nt(a, b):
    return lax.dot_general(a, b, (((1,), (1,)), ((), ())), preferred_element_type=F32)


def _dot_tn(a, b):
    return lax.dot_general(a, b, (((0,), (0,)), ((), ())), preferred_element_type=F32)


def _gelu(x):
    x2 = x * x
    t = jnp.tanh(x * (x2 * (_GELU_K0 * _GELU_K1) + _GELU_K0))
    cdf = 0.5 * t + 0.5
    dg = cdf + (x * (1.0 - t * t)) * (x2 * (1.5 * _GELU_K0 * _GELU_K1) + 0.5 * _GELU_K0)
    return x * cdf, dg


def _colsum(x):
    return jnp.sum(x, axis=0, keepdims=True)


def _row(d):
    return pl.BlockSpec((1, d), lambda *_: (0, 0))


def _full(shape):
    n = len(shape)
    return pl.BlockSpec(shape, lambda *_: (0,) * n)


def _layer_block(shape, layer):
    n = len(shape)
    return pl.BlockSpec((None,) + tuple(shape), lambda *_: (layer,) + (0,) * n)


def _pool_counts(first_row, rows, window):
    t = first_row + lax.broadcasted_iota(jnp.int32, (rows, 1), 0)
    return 1.0 / jnp.minimum(t + 1, window).astype(F32)


def _pool_forward(a_ext, a, first_row, rows):
    out = []
    for g, window in enumerate(POOL_WINDOWS):
        s = a_ext[:, g * GROUP:(g + 1) * GROUP]
        k = 1
        while k < window:
            s = s + pltpu.roll(s, k, 0)
            k *= 2
        inv = _pool_counts(first_row, rows, window)
        out.append(s[POOL_HALO:] * inv - a[:, g * GROUP:(g + 1) * GROUP])
    return out


def _sgu_norm(v):
    gv, dgv = _gelu(v)
    mu = jnp.mean(gv, axis=-1, keepdims=True)
    xc = gv - mu
    var = jnp.mean(xc * xc, axis=-1, keepdims=True)
    rs = lax.rsqrt(var + LN_EPS)
    return xc * rs, rs, dgv


def _modulate_input(x, shift, scale):
    S = x.shape[0]
    ts = min(TS_PROJ, S)

    def body(x_ref, sh_ref, sc_ref, h_ref):
        h_ref[...] = (x_ref[...] * (1.0 + sc_ref[...]) + sh_ref[...]).astype(MXU_DTYPE)

    tile = pl.BlockSpec((ts, D), lambda i: (i, 0))
    return pl.pallas_call(
        body, name="modulate_input", grid=(S // ts,), in_specs=[tile, _row(D), _row(D)], out_specs=tile,
        out_shape=jax.ShapeDtypeStruct((S, D), MXU_DTYPE), compiler_params=_params(1),
    )(x, shift, scale)


def _matmul(a, w_all, layer, tn, out_dtype, name, exchange=None):
    S, K = a.shape
    N = w_all.shape[2]
    ts = min(TS_MM, S)

    def body(a_ref, w_ref, o_ref):
        o_ref[...] = _dot(a_ref[...], w_ref[...]).astype(out_dtype)

    (out,), extra = _call(
        body, name=name, grid=(N // tn, S // ts),
        in_specs=[pl.BlockSpec((ts, K), lambda j, i: (i, 0)), pl.BlockSpec((None, K, tn), lambda j, i: (layer, 0, j))],
        out_specs=[pl.BlockSpec((ts, tn), lambda j, i: (i, j))],
        out_shape=[jax.ShapeDtypeStruct((S, N), out_dtype)], args=(a, w_all), exchange=exchange)
    return out, extra


def _matmul_nt(a, w_all, layer, tk, name, exchange=None):
    S, K = a.shape
    N = w_all.shape[1]
    ts = min(TS_MM, S)
    n_k = K // tk

    def body(a_ref, w_ref, o_ref, acc):
        k = pl.program_id(1)
        part = _dot_nt(a_ref[...], w_ref[...])
        if n_k == 1:
            o_ref[...] = part
        else:
            @pl.when(k == 0)
            def _():
                acc[...] = part

            @pl.when(jnp.logical_and(k > 0, k < n_k - 1))
            def _():
                acc[...] += part

            @pl.when(k == n_k - 1)
            def _():
                o_ref[...] = acc[...] + part

    (out,), extra = _call(
        body, name=name, grid=(S // ts, n_k),
        in_specs=[pl.BlockSpec((ts, tk), lambda i, k: (i, k)), pl.BlockSpec((None, N, tk), lambda i, k: (layer, 0, k))],
        out_specs=[pl.BlockSpec((ts, N), lambda i, k: (i, 0))],
        out_shape=[jax.ShapeDtypeStruct((S, N), F32)],
        scratch_shapes=[pltpu.VMEM((ts, N), F32)], args=(a, w_all), exchange=exchange)
    return out, extra


def _f2_mixers(proj, xn, lg, lb, gate, pool_w, pool_scale, sg, sb, wm, bias_full, w_out_all, lg1, lb1, shift2, scale2,
               layer, exchange=None):
    S = xn.shape[0]
    ts = min(TS_MIX, S)
    n_chunks = ts // CHUNK
    halo_blocks = ts // POOL_HALO

    def body(p_ref, halo_ref, xn_ref, lg_ref, lb_ref, gate_ref, pw_ref, ps_ref, sg_ref, sb_ref, wm_ref,
             bias_ref, wo_ref, lg1_ref, lb1_ref, sh2_ref, sc2_ref, mix_ref, f_ref, xh_ref, rs_ref, h2_ref, z_scr):
        i = pl.program_id(0)
        a = p_ref[:, 0:D_POOL].astype(F32)
        u = p_ref[:, D_POOL:D_POOL + D_SGU].astype(F32)
        v = p_ref[:, D_POOL + D_SGU:D_IN].astype(F32)
        halo = halo_ref[...].astype(F32) * (i > 0).astype(F32)
        a_ext = jnp.concatenate([halo, a], axis=0)
        pooled = _pool_forward(a_ext, a, i * ts, ts)
        for g in range(N_GROUPS):
            mixed = _dot(pooled[g].astype(MXU_DTYPE), pw_ref[g])
            mix_ref[:, g * GROUP:(g + 1) * GROUP] = (mixed * ps_ref[:, g * GROUP:(g + 1) * GROUP]).astype(MXU_DTYPE)
        gu, _ = _gelu(u)
        vhat, _, _ = _sgu_norm(v)
        vn = (vhat * sg_ref[...] + sb_ref[...]).astype(MXU_DTYPE)
        for c in range(n_chunks):
            for h in range(HEADS):
                blk = vn[c * CHUNK:(c + 1) * CHUNK, h * GROUP:(h + 1) * GROUP]
                z_scr[c * CHUNK:(c + 1) * CHUNK, h * GROUP:(h + 1) * GROUP] = (
                    _dot(wm_ref[h], blk) + bias_ref[:, h * GROUP:(h + 1) * GROUP])
        mix_ref[:, D_POOL:D] = (gu * z_scr[...]).astype(MXU_DTYPE)
        f = _dot(mix_ref[...], wo_ref[...])
        f_ref[...] = f.astype(STORE_DTYPE)
        x = xn_ref[...] * lg_ref[...] + lb_ref[...]
        z1 = ALPHA * x + gate_ref[...] * f
        mu = jnp.mean(z1, axis=-1, keepdims=True)
        zc = z1 - mu
        var = jnp.mean(zc * zc, axis=-1, keepdims=True)
        rs = lax.rsqrt(var + LN_EPS)
        xhat = zc * rs
        xh_ref[...] = xhat
        rs_ref[...] = rs
        x1 = xhat * lg1_ref[...] + lb1_ref[...]
        h2_ref[...] = (x1 * (1.0 + sc2_ref[...]) + sh2_ref[...]).astype(MXU_DTYPE)

    tile = lambda w: pl.BlockSpec((ts, w), lambda i: (i, 0))
    return _call(
        body, name=f"f2_mixers_l{layer}", grid=(S // ts,),
        in_specs=[tile(D_IN),
                  pl.BlockSpec((POOL_HALO, D_POOL), lambda i: (jnp.maximum(i * halo_blocks - 1, 0), 0)),
                  tile(D), _row(D), _row(D), _row(D),
                  _full((N_GROUPS, GROUP, GROUP)), _row(D_POOL), _row(D_SGU), _row(D_SGU),
                  _full((HEADS, CHUNK, CHUNK)), _full((CHUNK, D_SGU)), _layer_block((D, D), layer),
                  _row(D), _row(D), _row(D), _row(D)],
        out_specs=[tile(D), tile(D), tile(D), tile(1), tile(D)],
        out_shape=[jax.ShapeDtypeStruct((S, D), MXU_DTYPE), jax.ShapeDtypeStruct((S, D), STORE_DTYPE),
                   jax.ShapeDtypeStruct((S, D), F32), jax.ShapeDtypeStruct((S, 1), F32),
                   jax.ShapeDtypeStruct((S, D), MXU_DTYPE)],
        scratch_shapes=[pltpu.VMEM((ts, D_SGU), F32)],
        args=(proj, proj, xn, lg, lb, gate, pool_w, pool_scale, sg, sb, wm, bias_full, w_out_all, lg1, lb1, shift2,
              scale2), exchange=exchange)


def _conv_forward(g_ext, cw_ref, cb_ref):
    gm2 = pltpu.roll(g_ext, 2, 0)[CONV_HALO:]
    gm1 = pltpu.roll(g_ext, 1, 0)[CONV_HALO:]
    g0 = g_ext[CONV_HALO:]
    gc = ((cb_ref[...] + gm2 * cw_ref[0:1, :]) + gm1 * cw_ref[1:2, :]) + g0 * cw_ref[2:3, :]
    return gc, gm2, gm1


def _f4_ffn(up, xhat1, lg, lb, gate, conv_w, conv_b, w_down_all, lg2, lb2, shift_next, scale_next, layer,
            exchange=None):
    S = xhat1.shape[0]
    ts = min(TS_FF, S)
    halo_blocks = ts // CONV_HALO

    def body(g_ref, halo_ref, val_ref, xh_ref, lg_ref, lb_ref, gate_ref, cw_ref, cb_ref, wd_ref, lg2_ref, lb2_ref,
             shn_ref, scn_ref, act_ref, f_ref, xo_ref, rs_ref, hn_ref, ge_ref, dge_ref):
        i = pl.program_id(0)
        halo = halo_ref[...].astype(F32) * (i > 0).astype(F32)
        g_ext = jnp.concatenate([halo, g_ref[...].astype(F32)], axis=0)
        gc, _, _ = _conv_forward(g_ext, cw_ref, cb_ref)
        ge, dge = _gelu(gc)
        ge_ref[...] = ge.astype(STORE_DTYPE)
        dge_ref[...] = dge.astype(STORE_DTYPE)
        act = (ge * val_ref[...].astype(F32)).astype(MXU_DTYPE)
        act_ref[...] = act
        f = _dot(act, wd_ref[...])
        f_ref[...] = f.astype(STORE_DTYPE)
        x = xh_ref[...] * lg_ref[...] + lb_ref[...]
        z = ALPHA * x + gate_ref[...] * f
        mu = jnp.mean(z, axis=-1, keepdims=True)
        zc = z - mu
        var = jnp.mean(zc * zc, axis=-1, keepdims=True)
        rs = lax.rsqrt(var + LN_EPS)
        xhat = zc * rs
        xo_ref[...] = xhat
        rs_ref[...] = rs
        x2 = xhat * lg2_ref[...] + lb2_ref[...]
        hn_ref[...] = (x2 * (1.0 + scn_ref[...]) + shn_ref[...]).astype(MXU_DTYPE)

    tile = lambda w: pl.BlockSpec((ts, w), lambda i: (i, 0))
    return _call(
        body, name=f"f4_ffn_l{layer}", grid=(S // ts,),
        in_specs=[pl.BlockSpec((ts, D_FF), lambda i: (i, 0)),
                  pl.BlockSpec((CONV_HALO, D_FF), lambda i: (jnp.maximum(i * halo_blocks - 1, 0), 0)),
                  pl.BlockSpec((ts, D_FF), lambda i: (i, 1)),
                  tile(D), _row(D), _row(D), _row(D), _full((3, D_FF)), _row(D_FF),
                  _layer_block((D_FF, D), layer), _row(D), _row(D), _row(D), _row(D)],
        out_specs=[tile(D_FF), tile(D), tile(D), tile(1), tile(D), tile(D_FF), tile(D_FF)],
        out_shape=[jax.ShapeDtypeStruct((S, D_FF), MXU_DTYPE), jax.ShapeDtypeStruct((S, D), STORE_DTYPE),
                   jax.ShapeDtypeStruct((S, D), F32), jax.ShapeDtypeStruct((S, 1), F32),
                   jax.ShapeDtypeStruct((S, D), MXU_DTYPE), jax.ShapeDtypeStruct((S, D_FF), STORE_DTYPE),
                   jax.ShapeDtypeStruct((S, D_FF), STORE_DTYPE)],
        args=(up, up, up, xhat1, lg, lb, gate, conv_w, conv_b, w_down_all, lg2, lb2, shift_next, scale_next),
        exchange=exchange)


def _loss_head(xhat, lg, lb, target):
    S = xhat.shape[0]
    ts = min(TS_PROJ, S)

    def body(xh_ref, lg_ref, lb_ref, t_ref, loss_ref, dy_ref, acc):
        i = pl.program_id(0)

        @pl.when(i == 0)
        def _():
            acc[...] = jnp.zeros_like(acc)

        err = (xh_ref[...] * lg_ref[...] + lb_ref[...]) - t_ref[...]
        dy_ref[...] = err * (1.0 / D)
        acc[...] += _colsum(err * err)

        @pl.when(i == pl.num_programs(0) - 1)
        def _():
            loss_ref[...] = jnp.sum(acc[...], axis=1, keepdims=True) * (0.5 / D)

    tile = pl.BlockSpec((ts, D), lambda i: (i, 0))
    return pl.pallas_call(
        body, name="loss_head", grid=(S // ts,),
        in_specs=[tile, _row(D), _row(D), tile],
        out_specs=[_full((1, 1)), tile],
        out_shape=[jax.ShapeDtypeStruct((1, 1), F32), jax.ShapeDtypeStruct((S, D), F32)],
        scratch_shapes=[pltpu.VMEM((1, D), F32)],
        compiler_params=_params(1),
    )(xhat, lg, lb, target)


def _accumulate(i, ref, value):
    @pl.when(i == 0)
    def _():
        ref[...] = value

    @pl.when(i > 0)
    def _():
        ref[...] += value


def _b_ln(d, xhat, rstd, f, lg, lb, gate, name, modulated=None, exchange=None):
    S = d.shape[0]
    ts = min(TS_PROJ, S)
    has_mod = modulated is not None

    def body(*refs):
        if has_mod:
            (d_ref, xh_ref, rs_ref, f_ref, lg_ref, lb_ref, gate_ref, dri_ref, sc_ref,
             df_ref, dres_ref, dlg_ref, dlb_ref, dgate_ref, dsc_ref, dsh_ref) = refs
        else:
            (d_ref, xh_ref, rs_ref, f_ref, lg_ref, lb_ref, gate_ref,
             df_ref, dres_ref, dlg_ref, dlb_ref, dgate_ref) = refs
        i = pl.program_id(0)
        xh = xh_ref[...]
        if has_mod:
            dh = d_ref[...]
            dxo_t = dh * (1.0 + sc_ref[...]) + dri_ref[...]
            _accumulate(i, dsc_ref, _colsum(dh * (xh * lg_ref[...] + lb_ref[...])))
            _accumulate(i, dsh_ref, _colsum(dh))
        else:
            dxo_t = d_ref[...]
        dxh = dxo_t * lg_ref[...]
        m1 = jnp.mean(dxh, axis=-1, keepdims=True)
        m2 = jnp.mean(dxh * xh, axis=-1, keepdims=True)
        dz = rs_ref[...] * (dxh - m1 - xh * m2)
        df_ref[...] = (dz * gate_ref[...]).astype(MXU_DTYPE)
        dres_ref[...] = ALPHA * dz
        _accumulate(i, dlg_ref, _colsum(dxo_t * xh))
        _accumulate(i, dlb_ref, _colsum(dxo_t))
        _accumulate(i, dgate_ref, _colsum(dz * f_ref[...].astype(F32)))

    tile = lambda w: pl.BlockSpec((ts, w), lambda i: (i, 0))
    n_sums = 5 if has_mod else 3
    in_specs = [tile(D), tile(D), tile(1), tile(D), _row(D), _row(D), _row(D)]
    args = [d, xhat, rstd, f, lg, lb, gate]
    if has_mod:
        in_specs += [tile(D), _row(D)]
        args += list(modulated)
    return _call(
        body, name=name, grid=(S // ts,), in_specs=in_specs,
        out_specs=[tile(D), tile(D)] + [_row(D)] * n_sums,
        out_shape=[jax.ShapeDtypeStruct((S, D), MXU_DTYPE), jax.ShapeDtypeStruct((S, D), F32)]
        + [jax.ShapeDtypeStruct((1, D), F32)] * n_sums,
        args=args, exchange=exchange)


def _input_grad(dh, dres, scale, x):
    S = dh.shape[0]
    ts = min(TS_PROJ, S)

    def body(dh_ref, dres_ref, sc_ref, x_ref, dx_ref, dsc_ref, dsh_ref):
        i = pl.program_id(0)
        dh_t = dh_ref[...]
        dx_ref[...] = dh_t * (1.0 + sc_ref[...]) + dres_ref[...]
        _accumulate(i, dsc_ref, _colsum(dh_t * x_ref[...]))
        _accumulate(i, dsh_ref, _colsum(dh_t))

    tile = pl.BlockSpec((ts, D), lambda i: (i, 0))
    return pl.pallas_call(
        body, name="input_grad", grid=(S // ts,), in_specs=[tile, tile, _row(D), tile],
        out_specs=[tile, _row(D), _row(D)],
        out_shape=[jax.ShapeDtypeStruct((S, D), F32)] + [jax.ShapeDtypeStruct((1, D), F32)] * 2,
        compiler_params=_params(1),
    )(dh, dres, scale, x)


def _b2_ffn(df2, w_down_all, up, ge, dge, conv_w, layer, exchange=None):
    S = df2.shape[0]
    ts = min(TS_FF, S)
    n_tiles = S // ts
    n_ext = ts + CONV_HALO

    def body(df_ref, wd_ref, g_ref, val_ref, ge_ref, dge_ref, cw_ref, dup_ref, dcw_ref, dcb_ref, next_dgc):
        i = pl.program_id(0)

        @pl.when(i == 0)
        def _():
            next_dgc[...] = jnp.zeros_like(next_dgc)

        dact = _dot_nt(df_ref[...], wd_ref[...])
        dup_ref[:, D_FF:2 * D_FF] = (dact * ge_ref[...].astype(F32)).astype(MXU_DTYPE)
        dgc = dact * val_ref[...].astype(F32) * dge_ref[...].astype(F32)
        ext = jnp.concatenate([dgc, next_dgc[...]], axis=0)
        dp1 = pltpu.roll(ext, n_ext - 1, 0)[:ts]
        dp2 = pltpu.roll(ext, n_ext - 2, 0)[:ts]
        dup_ref[:, 0:D_FF] = (dgc * cw_ref[2:3, :] + dp1 * cw_ref[1:2, :] + dp2 * cw_ref[0:1, :]).astype(MXU_DTYPE)
        next_dgc[...] = dgc[0:CONV_HALO]
        g = g_ref[...].astype(F32)
        dcw = jnp.concatenate([_colsum(dp2 * g), _colsum(dp1 * g), _colsum(dgc * g)], axis=0)
        _accumulate(i, dcw_ref, dcw)
        _accumulate(i, dcb_ref, _colsum(dgc))

    tile = lambda w, col=0: pl.BlockSpec((ts, w), lambda i: (n_tiles - 1 - i, col))
    return _call(
        body, name=f"b2_ffn_l{layer}", grid=(n_tiles,),
        in_specs=[tile(D), _layer_block((D_FF, D), layer), tile(D_FF), tile(D_FF, 1), tile(D_FF), tile(D_FF),
                  _full((3, D_FF))],
        out_specs=[tile(2 * D_FF), _full((3, D_FF)), _row(D_FF)],
        out_shape=[jax.ShapeDtypeStruct((S, 2 * D_FF), MXU_DTYPE),
                   jax.ShapeDtypeStruct((3, D_FF), F32), jax.ShapeDtypeStruct((1, D_FF), F32)],
        scratch_shapes=[pltpu.VMEM((CONV_HALO, D_FF), F32)],
        args=(df2, w_down_all, up, up, ge, dge, conv_w), exchange=exchange)


def _b5_mixers(df1, w_out_all, proj, pool_w, pool_scale, sg, sb, wm, bias_full, layer):
    S = df1.shape[0]
    ts = min(TS_MIX, S)
    n_chunks = ts // CHUNK
    halo_blocks = ts // POOL_HALO
    last_halo = S // POOL_HALO - 1

    def body(df_ref, dfh_ref, wo_ref, p_ref, ah_ref, pw_ref, ps_ref, sg_ref, sb_ref, wm_ref, bias_ref,
             dp_ref, dpw_ref, dps_ref, dsg_ref, dsb_ref, dwm_ref, dbias_ref, z_scr, dvn_scr):
        i = pl.program_id(0)
        last = pl.num_programs(0) - 1
        dmix = _dot_nt(df_ref[...], wo_ref[...])
        dmix_halo = _dot_nt(dfh_ref[...], wo_ref[0:D_POOL, :]) * (i < last).astype(F32)

        a = p_ref[:, 0:D_POOL].astype(F32)
        halo = ah_ref[...].astype(F32) * (i > 0).astype(F32)
        pooled = _pool_forward(jnp.concatenate([halo, a], axis=0), a, i * ts, ts)
        n = ts + POOL_HALO
        dps_parts = []
        for g, window in enumerate(POOL_WINDOWS):
            cols = slice(g * GROUP, (g + 1) * GROUP)
            pooled_b = pooled[g].astype(MXU_DTYPE)
            mixed = _dot(pooled_b, pw_ref[g])
            dya = dmix[:, cols]
            dps_parts.append(_colsum(dya * mixed))
            dmixed = (dya * ps_ref[:, cols]).astype(MXU_DTYPE)
            dpw_g = _dot_tn(pooled_b, dmixed)

            @pl.when(i == 0)
            def _():
                dpw_ref[g] = dpw_g

            @pl.when(i > 0)
            def _():
                dpw_ref[g] += dpw_g

            dpooled = _dot_nt(dmixed, pw_ref[g])
            dmixed_h = (dmix_halo[:, cols] * ps_ref[:, cols]).astype(MXU_DTYPE)
            dpooled_h = _dot_nt(dmixed_h, pw_ref[g])
            q = dpooled * _pool_counts(i * ts, ts, window)
            s = jnp.concatenate([q, dpooled_h * (1.0 / window)], axis=0)
            k = 1
            while k < window:
                s = s + pltpu.roll(s, n - k, 0)
                k *= 2
            dp_ref[:, cols] = (s[:ts] - dpooled).astype(MXU_DTYPE)
        _accumulate(i, dps_ref, jnp.concatenate(dps_parts, axis=1))

        u = p_ref[:, D_POOL:D_POOL + D_SGU].astype(F32)
        v = p_ref[:, D_POOL + D_SGU:D_IN].astype(F32)
        gu, dgu = _gelu(u)
        vhat, rs, dgv = _sgu_norm(v)
        vn = (vhat * sg_ref[...] + sb_ref[...]).astype(MXU_DTYPE)
        dyb = dmix[:, D_POOL:D]
        dz = dyb * gu
        dzb = dz.astype(MXU_DTYPE)
        dbias = dz[0:CHUNK]
        for c in range(1, n_chunks):
            dbias = dbias + dz[c * CHUNK:(c + 1) * CHUNK]
        _accumulate(i, dbias_ref, dbias)
        for h in range(HEADS):
            cols = slice(h * GROUP, (h + 1) * GROUP)
            dwm_h = None
            for c in range(n_chunks):
                rows = slice(c * CHUNK, (c + 1) * CHUNK)
                vn_blk = vn[rows, cols]
                dz_blk = dzb[rows, cols]
                z_scr[rows, cols] = _dot(wm_ref[h], vn_blk) + bias_ref[:, cols]
                dvn_scr[rows, cols] = _dot_tn(wm_ref[h], dz_blk)
                part = _dot_nt(dz_blk, vn_blk)
                dwm_h = part if dwm_h is None else dwm_h + part

            @pl.when(i == 0)
            def _():
                dwm_ref[h] = dwm_h

            @pl.when(i > 0)
            def _():
                dwm_ref[h] += dwm_h

        dp_ref[:, D_POOL:D_POOL + D_SGU] = (dyb * z_scr[...] * dgu).astype(MXU_DTYPE)
        dvn = dvn_scr[...]
        _accumulate(i, dsg_ref, _colsum(dvn * vhat))
        _accumulate(i, dsb_ref, _colsum(dvn))
        dvh = dvn * sg_ref[...]
        m1 = jnp.mean(dvh, axis=-1, keepdims=True)
        m2 = jnp.mean(dvh * vhat, axis=-1, keepdims=True)
        dp_ref[:, D_POOL + D_SGU:D_IN] = (rs * (dvh - m1 - vhat * m2) * dgv).astype(MXU_DTYPE)

        @pl.when(i == last)
        def _():
            tri = (lax.broadcasted_iota(jnp.int32, (CHUNK, CHUNK), 0)
                   >= lax.broadcasted_iota(jnp.int32, (CHUNK, CHUNK), 1))
            for h in range(HEADS):
                dwm_ref[h] = jnp.where(tri, dwm_ref[h], 0.0)

    tile = lambda w: pl.BlockSpec((ts, w), lambda i: (i, 0))
    return pl.pallas_call(
        body, name=f"b5_mixers_l{layer}", grid=(S // ts,),
        in_specs=[tile(D),
                  pl.BlockSpec((POOL_HALO, D), lambda i: (jnp.minimum((i + 1) * halo_blocks, last_halo), 0)),
                  _layer_block((D, D), layer), tile(D_IN),
                  pl.BlockSpec((POOL_HALO, D_POOL), lambda i: (jnp.maximum(i * halo_blocks - 1, 0), 0)),
                  _full((N_GROUPS, GROUP, GROUP)), _row(D_POOL), _row(D_SGU), _row(D_SGU),
                  _full((HEADS, CHUNK, CHUNK)), _full((CHUNK, D_SGU))],
        out_specs=[tile(D_IN), _full((N_GROUPS, GROUP, GROUP)), _row(D_POOL), _row(D_SGU), _row(D_SGU),
                   _full((HEADS, CHUNK, CHUNK)), _full((CHUNK, D_SGU))],
        out_shape=[jax.ShapeDtypeStruct((S, D_IN), MXU_DTYPE), jax.ShapeDtypeStruct((N_GROUPS, GROUP, GROUP), F32),
                   jax.ShapeDtypeStruct((1, D_POOL), F32), jax.ShapeDtypeStruct((1, D_SGU), F32),
                   jax.ShapeDtypeStruct((1, D_SGU), F32), jax.ShapeDtypeStruct((HEADS, CHUNK, CHUNK), F32),
                   jax.ShapeDtypeStruct((CHUNK, D_SGU), F32)],
        scratch_shapes=[pltpu.VMEM((ts, D_SGU), F32), pltpu.VMEM((ts, D_SGU), F32)],
        compiler_params=_params(1),
    )(df1, df1, w_out_all, proj, proj, pool_w, pool_scale, sg, sb, wm, bias_full)


def _weight_grad(a, bs, tn, name, rows=TS_TN):
    S, M = a.shape
    counts = [b.shape[1] // tn for b in bs]
    starts = [sum(counts[:k]) for k in range(len(bs))]
    ts = min(rows, S)

    def body(a_ref, *refs):
        b_refs, o_ref = refs[:-1], refs[-1]
        j = pl.program_id(0)
        for b_ref, lo, n in zip(b_refs, starts, counts):
            @pl.when(jnp.logical_and(j >= lo, j < lo + n))
            def _():
                _accumulate(pl.program_id(1), o_ref, _dot_tn(a_ref[...], b_ref[...]))

    b_spec = lambda lo, n: pl.BlockSpec((ts, tn), lambda j, i: (i, jnp.clip(j - lo, 0, n - 1)))
    return pl.pallas_call(
        body, name=name, grid=(sum(counts), S // ts),
        in_specs=[pl.BlockSpec((ts, M), lambda j, i: (i, 0))] + [b_spec(lo, n) for lo, n in zip(starts, counts)],
        out_specs=pl.BlockSpec((M, tn), lambda j, i: (0, j)),
        out_shape=jax.ShapeDtypeStruct((M, sum(counts) * tn), F32), compiler_params=_params(2),
    )(a, *bs)


def _silu(x):
    return x * (1.0 / (1.0 + jnp.exp(-x)))


def _ada_forward(c_all, ada_w, ada_b_cols):
    n_cols = ada_w.shape[2]
    tc = 512

    def body(c_ref, w_ref, b_ref, o_ref):
        ca = _silu(c_ref[...]).astype(MXU_DTYPE)
        o_ref[...] = _dot(ca, w_ref[...].astype(MXU_DTYPE)) + b_ref[...]

    return pl.pallas_call(
        body, name="ada_forward", grid=(DEPTH, n_cols // tc),
        in_specs=[pl.BlockSpec((16, D), lambda l, j: (0, 0)), pl.BlockSpec((None, D, tc), lambda l, j: (l, 0, j)),
                  pl.BlockSpec((None, 1, tc), lambda l, j: (l, 0, j))],
        out_specs=pl.BlockSpec((None, 16, tc), lambda l, j: (l, 0, j)),
        out_shape=jax.ShapeDtypeStruct((DEPTH, 16, n_cols), F32),
        compiler_params=_params(2),
    )(c_all, ada_w, ada_b_cols)


def _ada_backward(c_all, dmod_cols):
    n_cols = dmod_cols.shape[2]
    tc = 512

    def body(c_ref, d_ref, o_ref):
        ca = _silu(c_ref[...]).astype(MXU_DTYPE)
        o_ref[...] = _dot_tn(ca, d_ref[...].astype(MXU_DTYPE))

    return pl.pallas_call(
        body, name="ada_backward", grid=(DEPTH, n_cols // tc),
        in_specs=[pl.BlockSpec((16, D), lambda l, j: (0, 0)), pl.BlockSpec((None, 16, tc), lambda l, j: (l, 0, j))],
        out_specs=pl.BlockSpec((None, D, tc), lambda l, j: (l, 0, j)),
        out_shape=jax.ShapeDtypeStruct((DEPTH, D, n_cols), F32),
        compiler_params=_params(2),
    )(c_all, dmod_cols)


def _adamw(w, g, m, v, name):
    R, C = w.shape
    tr = R
    for cand in (512, 256, 128, 64, 32, 16, 8):
        if R % cand == 0 and cand * C * 4 <= 2 ** 21:
            tr = cand
            break
    c1 = 1.0 - ADAM_B1 ** ADAM_STEP
    c2 = 1.0 - ADAM_B2 ** ADAM_STEP

    def body(w_ref, g_ref, m_ref, v_ref, d_ref, mo_ref, vo_ref):
        gg = g_ref[...]
        mn = ADAM_B1 * m_ref[...] + (1.0 - ADAM_B1) * gg
        vn = ADAM_B2 * v_ref[...] + (1.0 - ADAM_B2) * (gg * gg)
        mo_ref[...] = mn
        vo_ref[...] = vn
        d_ref[...] = -ADAM_LR * ((mn / c1) / (jnp.sqrt(vn / c2) + ADAM_EPS) + ADAM_WD * w_ref[...])

    tile = pl.BlockSpec((tr, C), lambda i: (i, 0))
    return pl.pallas_call(
        body, name=name, grid=(R // tr,), in_specs=[tile] * 4, out_specs=[tile] * 3,
        out_shape=[jax.ShapeDtypeStruct((R, C), F32)] * 3, compiler_params=_params(1),
    )(w, g, m, v)


def _position():
    x, y, c = lax.axis_index("x"), lax.axis_index("y"), lax.axis_index("c")
    other_chips = [(1 - x, y), (x, 1 - y), (1 - x, 1 - y)]
    return x, y, c, other_chips


def _all_gather8(block, name):
    R, C = block.shape

    def body(x_ref, out_ref, send_sems, recv_sems, local_sem):
        x, y, c, chips = _position()
        me, sibling = (x, y, c), (x, y, 1 - c)

        def rows(px, py, pc):
            return out_ref.at[pl.ds((4 * px + 2 * py + pc) * R, R), :]

        def copy(k, blk, to, src=None):
            return pltpu.make_async_remote_copy(
                src_ref=rows(*blk) if src is None else src, dst_ref=rows(*blk),
                send_sem=send_sems.at[k], recv_sem=recv_sems.at[k], device_id=to, device_id_type=MESH)

        mine = pltpu.make_async_copy(x_ref, rows(*me), local_sem)
        mine.start()
        first = [copy(0, me, sibling, src=x_ref)]
        first += [copy(1 + j, me, (*chip, c), src=x_ref) for j, chip in enumerate(chips)]
        for cp in first:
            cp.start()
        passed = [copy(4 + j, (*chip, c), sibling) for j, chip in enumerate(chips)]
        for j, chip in enumerate(chips):
            copy(1 + j, (*chip, c), me).wait_recv()
            passed[j].start()
        copy(0, sibling, me).wait_recv()
        for j, chip in enumerate(chips):
            copy(4 + j, (*chip, 1 - c), me).wait_recv()
        for cp in first + passed:
            cp.wait_send()
        mine.wait()

    return pl.pallas_call(
        body, name=name, out_shape=jax.ShapeDtypeStruct((N_DEV * R, C), block.dtype),
        in_specs=[pl.BlockSpec(memory_space=pltpu.VMEM)], out_specs=ANY,
        scratch_shapes=[pltpu.SemaphoreType.DMA((7,)), pltpu.SemaphoreType.DMA((7,)), pltpu.SemaphoreType.DMA(())],
    )(block)


def _gather_flat(vec, name):
    n = vec.shape[0]
    padded = -(-n // 1024) * 1024
    block = jnp.pad(vec, (0, padded - n)).reshape(8, padded // 8)
    out = _all_gather8(block, name)
    return out.reshape(N_DEV, padded)[:, :n]


_SHARD_KINDS = ("cols", "rows", "cols", "rows")


def _shard_of(ref, shape, kind, chip):
    m, n = shape
    return ref.at[:, pl.ds(chip * (n // 4), n // 4)] if kind == "cols" else ref.at[pl.ds(chip * (m // 4), m // 4), :]


def _half_of(ref, shape, kind, h):
    m, n = shape
    return ref.at[pl.ds(h * (m // 2), m // 2), :] if kind == "cols" else ref.at[:, pl.ds(h * (n // 2), n // 2)]


def _half_shape(shape, kind):
    m, n = shape
    return (m // 2, n) if kind == "cols" else (m, n // 2)


def _shard_shape(shape, kind):
    m, n = shape
    return (m, n // 4) if kind == "cols" else (m // 4, n)


def _remote(src, dst, k, to, send_sems, recv_sems):
    return pltpu.make_async_remote_copy(src_ref=src, dst_ref=dst, send_sem=send_sems.at[k], recv_sem=recv_sems.at[k],
                                        device_id=to, device_id_type=MESH)


def _gather_exchange(shards, full, kind, layer):
    shape = full.shape[1:]

    def copies(xin, xout, send_sems, recv_sems):
        x, y, c, chips = _position()
        my_chip = 2 * x + y
        own = xin[0].at[layer]
        place = lambda chip: _shard_of(xout[0].at[layer], shape, kind, chip)
        peers = [(x, y, 1 - c)] + [(*chip, c) for chip in chips]
        sources = [my_chip] + [2 * cx + cy for cx, cy in chips]
        sends = [_remote(own, place(my_chip), k, peer, send_sems, recv_sems) for k, peer in enumerate(peers)]
        arrivals = [_remote(place(s), place(s), k, (x, y, c), send_sems, recv_sems) for k, s in enumerate(sources)]
        return sends, arrivals

    return _Exchange([shards, full], [1], [], 4, copies)


def _swap_exchange(grads):
    shapes = [g.shape for g in grads]
    fresh = [jax.ShapeDtypeStruct(_half_shape(s, k), F32) for s, k in zip(shapes, _SHARD_KINDS)]

    def copies(xin, xout, send_sems, recv_sems):
        x, y, c, _ = _position()
        sends = [_remote(_half_of(xin[a], shapes[a], _SHARD_KINDS[a], 1 - c), xout[a], a, (x, y, 1 - c),
                         send_sems, recv_sems) for a in range(len(grads))]
        arrivals = [_remote(xout[a], xout[a], a, (x, y, c), send_sems, recv_sems) for a in range(len(grads))]
        return sends, arrivals

    return _Exchange(list(grads), [], fresh, len(grads), copies)


def _scatter_exchange(partials):
    n_w = len(partials)
    shapes = [p.shape for p in partials]
    fresh = [jax.ShapeDtypeStruct((3,) + _shard_shape(s, k), p.dtype) for s, k, p in zip(shapes, _SHARD_KINDS, partials)]

    def copies(xin, xout, send_sems, recv_sems):
        x, y, c, chips = _position()
        sends, arrivals = [], []
        for j, (cx, cy) in enumerate(chips):
            for a in range(n_w):
                src = _shard_of(xin[a], shapes[a], _SHARD_KINDS[a], 2 * cx + cy)
                sends.append(_remote(src, xout[a].at[j], j * n_w + a, (cx, cy, c), send_sems, recv_sems))
                arrivals.append(_remote(xout[a].at[j], xout[a].at[j], j * n_w + a, (x, y, c), send_sems, recv_sems))
        return sends, arrivals

    return _Exchange(list(partials), [], fresh, 3 * n_w, copies)


def _share_exchange(reduced, layer):
    n_w = len(reduced)
    shapes = [r.shape[1:] for r in reduced]

    def copies(xin, xout, send_sems, recv_sems):
        x, y, c, _ = _position()
        half = lambda a, h: _half_of(xout[a].at[layer], shapes[a], _SHARD_KINDS[a], h)
        sends = [_remote(half(a, c), half(a, c), a, (x, y, 1 - c), send_sems, recv_sems) for a in range(n_w)]
        arrivals = [_remote(half(a, 1 - c), half(a, 1 - c), a, (x, y, c), send_sems, recv_sems) for a in range(n_w)]
        return sends, arrivals

    return _Exchange(list(reduced), list(range(n_w)), [], n_w, copies)


def _chip_partial(pos, grad, theirs, kind, name):
    M, N = grad.shape
    if kind == "cols":
        tm = 256
        steps = M // 2 // tm
        block, g_map = (tm, N), (lambda i, pos: (pos[0] * steps + i, 0))
    else:
        tm = M // 4
        steps = 4
        block, g_map = (tm, N // 2), (lambda i, pos: (i, pos[0]))

    def body(pos_ref, g_ref, t_ref, o_ref):
        o_ref[...] = (g_ref[...] + t_ref[...]).astype(WIRE_DTYPE)

    grid_spec = pltpu.PrefetchScalarGridSpec(
        num_scalar_prefetch=1, grid=(steps,),
        in_specs=[pl.BlockSpec(block, g_map), pl.BlockSpec(block, lambda i, pos: (i, 0))],
        out_specs=pl.BlockSpec(block, lambda i, pos: (i, 0)))
    return pl.pallas_call(
        body, name=name, grid_spec=grid_spec, out_shape=jax.ShapeDtypeStruct(_half_shape((M, N), kind), WIRE_DTYPE),
        compiler_params=_params(1),
    )(pos, grad, theirs)


def _reduce_shard(pos, grad, theirs, received, reduced, kind, layer, name):
    M, N = grad.shape
    if kind == "cols":
        tm = min(M // 2, 512)
        steps = M // 2 // tm
        block = (tm, N // 4)
        g_map = lambda i, pos: (pos[0] * steps + i, pos[1])
        t_map = lambda i, pos: (i, pos[1])
        o_map = lambda i, pos: (layer, pos[0] * steps + i, 0)
    else:
        steps = 1
        block = (M // 4, N // 2)
        g_map = lambda i, pos: (pos[1], pos[0])
        t_map = lambda i, pos: (pos[1], 0)
        o_map = lambda i, pos: (layer, 0, pos[0])
    out_shape = (DEPTH,) + _shard_shape((M, N), kind)

    def body(pos_ref, g_ref, t_ref, r0_ref, r1_ref, r2_ref, *rest):
        o_ref = rest[-1]
        chip = pos_ref[1]
        own = g_ref[...] + t_ref[...]
        r = [r0_ref[...].astype(F32), r1_ref[...].astype(F32), r2_ref[...].astype(F32)]
        total = None
        for s in range(N_CHIPS):
            rel = jnp.bitwise_xor(chip, s)
            term = jnp.where(rel == 0, own, jnp.where(rel == 2, r[0], jnp.where(rel == 1, r[1], r[2])))
            total = term if total is None else total + term
        o_ref[...] = total

    r_spec = lambda j: pl.BlockSpec((None,) + block, lambda i, pos: (j, i, 0))
    in_specs = [pl.BlockSpec(block, g_map), pl.BlockSpec(block, t_map), r_spec(0), r_spec(1), r_spec(2)]
    args = [pos, grad, theirs, received, received, received]
    aliases = {}
    if reduced is not None:
        in_specs.append(ANY)
        args.append(reduced)
        aliases = {6: 0}
    grid_spec = pltpu.PrefetchScalarGridSpec(
        num_scalar_prefetch=1, grid=(steps,), in_specs=in_specs, out_specs=pl.BlockSpec((None,) + block, o_map))
    return pl.pallas_call(
        body, name=name, grid_spec=grid_spec, out_shape=jax.ShapeDtypeStruct(out_shape, F32),
        input_output_aliases=aliases, compiler_params=_params(1),
    )(*args)


def _sum_devices(gathered):
    R8, C = gathered.shape
    R = R8 // N_DEV
    lanes = C // 128
    tc = 128 * max(k for k in range(1, lanes + 1) if lanes % k == 0 and k * 128 * R8 * 4 <= 2 ** 22)

    def body(g_ref, o_ref):
        total = g_ref[0:R, :]
        for d in range(1, N_DEV):
            total = total + g_ref[d * R:(d + 1) * R, :]
        o_ref[...] = total

    return pl.pallas_call(
        body, name="sum_devices", grid=(C // tc,),
        in_specs=[pl.BlockSpec((R8, tc), lambda j: (0, j))], out_specs=pl.BlockSpec((R, tc), lambda j: (0, j)),
        out_shape=jax.ShapeDtypeStruct((R, C), F32), compiler_params=_params(1),
    )(gathered)


_SMALL = ("pool_w", "pool_scale", "sgu_ln_g", "sgu_ln_b", "sgu_w", "sgu_b", "ln1_g", "ln1_b", "conv_w",
          "conv_b", "ln2_g", "ln2_b", "ada_b")
_WEIGHTS = ("ada_w", "ada_b", "w_in", "pool_w", "pool_scale", "sgu_ln_g", "sgu_ln_b", "sgu_w", "sgu_b", "w_out",
            "ln1_g", "ln1_b", "w_up", "conv_w", "conv_b", "w_down", "ln2_g", "ln2_b")


def kernel(x, c, ada_w, ada_b, w_in, pool_w, pool_scale, sgu_ln_g, sgu_ln_b, sgu_w, sgu_b, w_out, ln1_g, ln1_b, w_up, conv_w, conv_b, w_down, ln2_g, ln2_b, loss_target, m_ada_w, m_ada_b, m_w_in, m_pool_w, m_pool_scale, m_sgu_ln_g, m_sgu_ln_b, m_sgu_w, m_sgu_b, m_w_out, m_ln1_g, m_ln1_b, m_w_up, m_conv_w, m_conv_b, m_w_down, m_ln2_g, m_ln2_b, v_ada_w, v_ada_b, v_w_in, v_pool_w, v_pool_scale, v_sgu_ln_g, v_sgu_ln_b, v_sgu_w, v_sgu_b, v_w_out, v_ln1_g, v_ln1_b, v_w_up, v_conv_w, v_conv_b, v_w_down, v_ln2_g, v_ln2_b):
    weights = dict(ada_w=ada_w, ada_b=ada_b, w_in=w_in, pool_w=pool_w, pool_scale=pool_scale, sgu_ln_g=sgu_ln_g,
                   sgu_ln_b=sgu_ln_b, sgu_w=sgu_w, sgu_b=sgu_b, w_out=w_out, ln1_g=ln1_g, ln1_b=ln1_b, w_up=w_up,
                   conv_w=conv_w, conv_b=conv_b, w_down=w_down, ln2_g=ln2_g, ln2_b=ln2_b)
    mom_m = dict(ada_w=m_ada_w, ada_b=m_ada_b, w_in=m_w_in, pool_w=m_pool_w, pool_scale=m_pool_scale,
                 sgu_ln_g=m_sgu_ln_g, sgu_ln_b=m_sgu_ln_b, sgu_w=m_sgu_w, sgu_b=m_sgu_b, w_out=m_w_out,
                 ln1_g=m_ln1_g, ln1_b=m_ln1_b, w_up=m_w_up, conv_w=m_conv_w, conv_b=m_conv_b, w_down=m_w_down,
                 ln2_g=m_ln2_g, ln2_b=m_ln2_b)
    mom_v = dict(ada_w=v_ada_w, ada_b=v_ada_b, w_in=v_w_in, pool_w=v_pool_w, pool_scale=v_pool_scale,
                 sgu_ln_g=v_sgu_ln_g, sgu_ln_b=v_sgu_ln_b, sgu_w=v_sgu_w, sgu_b=v_sgu_b, w_out=v_w_out,
                 ln1_g=v_ln1_g, ln1_b=v_ln1_b, w_up=v_w_up, conv_w=v_conv_w, conv_b=v_conv_b, w_down=v_w_down,
                 ln2_g=v_ln2_g, ln2_b=v_ln2_b)

    ix, iy, ic = lax.axis_index("x"), lax.axis_index("y"), lax.axis_index("c")
    chip = 2 * ix + iy
    dev = 4 * ix + 2 * iy + ic
    pos = jnp.stack([ic, chip]).astype(jnp.int32)
    xs = x[0]
    target = loss_target[0]
    ff_shard = conv_w.shape[2]
    mod_shard = ada_w.shape[2]

    first = _gather_flat(jnp.concatenate([c.reshape(-1), conv_w.reshape(-1)]), "gather_c_conv")
    c_all = jnp.pad(first[:, :D], ((0, 8), (0, 0)))
    conv_parts = first[0::2, D:].reshape(N_CHIPS, DEPTH, 3, ff_shard)
    conv_full = jnp.transpose(conv_parts, (1, 2, 0, 3)).reshape(DEPTH, 3, D_FF)
    ada_b_cols = lax.dynamic_slice_in_dim(ada_b, chip * mod_shard, mod_shard, axis=1).reshape(DEPTH, 1, mod_shard)
    mod_part = _ada_forward(c_all, ada_w, ada_b_cols)[:, :8, :]
    mod_all = _gather_flat(mod_part.reshape(-1), "gather_mod").reshape(N_DEV, DEPTH, 8, mod_shard)
    mod_mine = lax.dynamic_index_in_dim(mod_all[0::2], dev, axis=2, keepdims=False)
    mod = jnp.transpose(mod_mine, (1, 0, 2)).reshape(DEPTH, 6, 1, D)

    sh_in, sh_out, sh_up, sh_down = (w.astype(WIRE_DTYPE) for w in (w_in, w_out, w_up, w_down))
    wf_in = lax.empty((DEPTH, D, D_IN), WIRE_DTYPE)
    wf_out = lax.empty((DEPTH, D, D), WIRE_DTYPE)
    wf_up = lax.empty((DEPTH, D, 2 * D_FF), WIRE_DTYPE)
    wf_down = lax.empty((DEPTH, D_FF, D), WIRE_DTYPE)
    (wf_in,) = _run_exchange(_gather_exchange(sh_in, wf_in, "cols", 0), "gather_w_in_l0")

    tri = jnp.tril(jnp.ones((CHUNK, CHUNK), dtype=bool))
    ones_row = jnp.ones((1, D), F32)
    zeros_row = jnp.zeros((1, D), F32)
    row = lambda a, l: a[l].reshape(1, -1)

    saved = []
    xn, lg, lb = xs, ones_row, zeros_row
    h = _modulate_input(xs, mod[0, 0], mod[0, 1])
    for l in range(DEPTH):
        shift1, scale1, gate1, shift2, scale2, gate2 = (mod[l, k] for k in range(6))
        nxt = min(l + 1, DEPTH - 1)
        pw = pool_w[l].astype(MXU_DTYPE)
        wm = jnp.where(tri[None], sgu_w[l], 0.0).astype(MXU_DTYPE)
        bias_full = jnp.repeat(jnp.transpose(sgu_b[l]), GROUP, axis=1)
        lg1, lb1 = row(ln1_g, l), row(ln1_b, l)
        proj, (wf_out,) = _matmul(h, wf_in, l, D_IN, STORE_DTYPE, f"f1_in_proj_l{l}",
                                  exchange=_gather_exchange(sh_out, wf_out, "rows", l))
        (mix, f1, xhat1, rstd1, h2), (wf_up,) = _f2_mixers(
            proj, xn, lg, lb, gate1, pw, row(pool_scale, l), row(sgu_ln_g, l), row(sgu_ln_b, l), wm, bias_full,
            wf_out, lg1, lb1, shift2, scale2, l, exchange=_gather_exchange(sh_up, wf_up, "cols", l))
        up, (wf_down,) = _matmul(h2, wf_up, l, D_FF, STORE_DTYPE, f"f3_up_proj_l{l}",
                                 exchange=_gather_exchange(sh_down, wf_down, "rows", l))
        (act, f2, xhat2, rstd2, h_next, ge, dge), filled = _f4_ffn(
            up, xhat1, lg1, lb1, gate2, conv_full[l], row(conv_b, l), wf_down, row(ln2_g, l), row(ln2_b, l),
            mod[nxt, 0], mod[nxt, 1], l,
            exchange=_gather_exchange(sh_in, wf_in, "cols", l + 1) if l + 1 < DEPTH else None)
        if filled:
            (wf_in,) = filled
        saved.append(dict(xn=xn, lg=lg, lb=lb, h=h, proj=proj, mix=mix, f1=f1, xhat1=xhat1, rstd1=rstd1, h2=h2,
                          up=up, act=act, f2=f2, xhat2=xhat2, rstd2=rstd2, pw=pw, wm=wm, bias_full=bias_full,
                          ge=ge, dge=dge))
        xn, lg, lb, h = xhat2, row(ln2_g, l), row(ln2_b, l), h_next

    loss_part, d_out = _loss_head(xn, lg, lb, target)
    loss = lax.psum(loss_part[0, 0], ("x", "y", "c"))

    tags = ("in", "out", "up", "down")
    small = {n: [None] * DEPTH for n in _SMALL}
    dmod = [[None] * 6 for _ in range(DEPTH)]
    below = None
    ready = None
    reduced = [None] * 4

    def partial_sums(layer, full, theirs):
        return [_chip_partial(pos, g, t, kind, f"chip_partial_{tag}_l{layer}")
                for g, t, kind, tag in zip(full, theirs, _SHARD_KINDS, tags)]

    def chip_sums(layer, full, theirs, received):
        return [_reduce_shard(pos, g, t, r, out, kind, layer, f"reduce_shard_{tag}_l{layer}")
                for g, t, r, out, kind, tag in zip(full, theirs, received, reduced, _SHARD_KINDS, tags)]

    for l in reversed(range(DEPTH)):
        sv = saved[l]
        scale1, gate1, scale2, gate2 = mod[l, 1], mod[l, 2], mod[l, 4], mod[l, 5]
        lg1, lb1 = row(ln1_g, l), row(ln1_b, l)
        outs, theirs = _b_ln(d_out, sv["xhat2"], sv["rstd2"], sv["f2"], row(ln2_g, l), row(ln2_b, l), gate2,
                             f"b1_ln2_l{l}", modulated=below,
                             exchange=_swap_exchange(ready[1]) if ready else None)
        df2, dres2, dlg2, dlb2, dmod[l][5] = outs[:5]
        if below is not None:
            dmod[l + 1][1], dmod[l + 1][0] = outs[5], outs[6]
        partials = partial_sums(ready[0], ready[1], theirs) if ready else None
        (dup, dcw, dcb), received = _b2_ffn(df2, wf_down, sv["up"], sv["ge"], sv["dge"], conv_full[l], l,
                                            exchange=_scatter_exchange(partials) if ready else None)
        if ready:
            reduced = chip_sums(ready[0], ready[1], theirs, received)
        g_down = _weight_grad(sv["act"], [df2], 512, f"wg_down_l{l}", rows=2 * TS_TN)
        dh2, shared = _matmul_nt(dup, wf_up, l, D_FF, f"b3_up_l{l}",
                                 exchange=_share_exchange(reduced, ready[0]) if ready else None)
        if ready:
            reduced = shared
        g_up = _weight_grad(sv["h2"], [dup], 1408, f"wg_up_l{l}", rows=2 * TS_TN)
        (df1, dres1, dlg1, dlb1, dmod[l][2], dmod[l][4], dmod[l][3]), _ = _b_ln(
            dh2, sv["xhat1"], sv["rstd1"], sv["f1"], lg1, lb1, gate1, f"b4_ln1_l{l}", modulated=(dres2, scale2))
        dproj, dpw, dps, dsg, dsb, dwm, dbias = _b5_mixers(
            df1, wf_out, sv["proj"], sv["pw"], row(pool_scale, l), row(sgu_ln_g, l), row(sgu_ln_b, l), sv["wm"],
            sv["bias_full"], l)
        g_out = _weight_grad(sv["mix"], [df1], D, f"wg_out_l{l}", rows=2 * TS_TN)
        d_out, _ = _matmul_nt(dproj, wf_in, l, D_IN, f"b6_in_l{l}")
        g_in = _weight_grad(sv["h"], [dproj], 768, f"wg_in_l{l}", rows=2 * TS_TN)
        ready = (l, [g_in, g_out, g_up, g_down])
        below = (dres1, scale1)
        small["pool_w"][l], small["pool_scale"][l] = dpw, dps[0]
        small["sgu_ln_g"][l], small["sgu_ln_b"][l], small["sgu_w"][l] = dsg[0], dsb[0], dwm
        small["sgu_b"][l] = jnp.transpose(jnp.sum(dbias.reshape(CHUNK, HEADS, GROUP), axis=2))
        small["ln1_g"][l], small["ln1_b"][l] = dlg1[0], dlb1[0]
        small["conv_w"][l], small["conv_b"][l] = dcw, dcb[0]
        small["ln2_g"][l], small["ln2_b"][l] = dlg2[0], dlb2[0]
    grad_x2d, dmod[0][1], dmod[0][0] = _input_grad(d_out, below[0], below[1], xs)
    grad_x = grad_x2d[None]
    small["ada_b"] = [jnp.concatenate([part[0] for part in dmod[l]]) for l in range(DEPTH)]

    names = _SMALL
    flat = jnp.concatenate([jnp.stack(small[n]).reshape(-1) for n in names])
    n_small = flat.shape[0]
    padded = -(-n_small // 1024) * 1024
    gathered = _all_gather8(jnp.pad(flat, (0, padded - n_small)).reshape(8, padded // 8), "gather_small_grads")
    summed = _sum_devices(gathered).reshape(-1)[:n_small]
    grads = {}
    offset = 0
    for n in names:
        size = math.prod(weights[n].shape[1:]) * DEPTH if n != "conv_w" else DEPTH * 3 * D_FF
        grads[n] = summed[offset:offset + size]
        offset += size
    grads["conv_w"] = lax.dynamic_slice_in_dim(grads["conv_w"].reshape(DEPTH, 3, D_FF), chip * ff_shard, ff_shard,
                                               axis=2)
    for n in names:
        grads[n] = grads[n].reshape(weights[n].shape)
    dmod_all = gathered.reshape(N_DEV, padded)[:, n_small - DEPTH * 6 * D:n_small].reshape(N_DEV, DEPTH, 6 * D)
    dmod_cols = lax.dynamic_slice_in_dim(jnp.transpose(dmod_all, (1, 0, 2)), chip * mod_shard, mod_shard, axis=2)
    grads["ada_w"] = _ada_backward(c_all, jnp.pad(dmod_cols, ((0, 0), (0, 8), (0, 0))))

    theirs = _run_exchange(_swap_exchange(ready[1]), "swap_halves_l0")
    partials = partial_sums(0, ready[1], theirs)
    received = _run_exchange(_scatter_exchange(partials), "scatter_partials_l0")
    reduced = chip_sums(0, ready[1], theirs, received)
    grads["w_in"], grads["w_out"], grads["w_up"], grads["w_down"] = _run_exchange(
        _share_exchange(reduced, 0), "share_reduced_l0")

    delta, new_m, new_v = {}, {}, {}
    for n in ("ada_w", "w_in", "w_out", "w_up", "w_down"):
        shape = weights[n].shape
        two_d = (shape[0] * shape[1], shape[2])
        d_, m_, v_ = _adamw(weights[n].reshape(two_d), grads[n].reshape(two_d), mom_m[n].reshape(two_d),
                            mom_v[n].reshape(two_d), f"adamw_{n}")
        delta[n], new_m[n], new_v[n] = d_.reshape(shape), m_.reshape(shape), v_.reshape(shape)
    sizes = [math.prod(weights[n].shape) for n in names]
    total = sum(sizes)
    padded = -(-total // 1024) * 1024
    pack = lambda d: jnp.pad(jnp.concatenate([d[n].reshape(-1) for n in names]), (0, padded - total)).reshape(8, -1)
    v_pack = jnp.pad(jnp.concatenate([mom_v[n].reshape(-1) for n in names]), (0, padded - total),
                     constant_values=1.0).reshape(8, -1)
    d_, m_, v_ = _adamw(pack(weights), pack(grads), pack(mom_m), v_pack, "adamw_small")
    offset = 0
    for n, size in zip(names, sizes):
        for src, dst in ((d_, delta), (m_, new_m), (v_, new_v)):
            dst[n] = src.reshape(-1)[offset:offset + size].reshape(weights[n].shape)
        offset += size

    return (loss, grad_x, *[grads[n] for n in _WEIGHTS], *[delta[n] for n in _WEIGHTS],
            *[new_m[n] for n in _WEIGHTS], *[new_v[n] for n in _WEIGHTS])
```

```python
import functools
import math

import jax
import jax.numpy as jnp
from jax import lax
from jax.experimental import pallas as pl
from jax.experimental.pallas import tpu as pltpu

F32 = jnp.float32
MXU_DTYPE = jnp.bfloat16
WIRE_DTYPE = jnp.bfloat16

DEPTH = 4
D = 1024
D_POOL = 512
D_SGU = 512
N_GROUPS = 4
GROUP = 128
POOL_WINDOWS = (2, 4, 8, 16)
POOL_HALO = 16
CHUNK = 128
HEADS = 4
D_IN = D_POOL + 2 * D_SGU
D_FF = 2816
CONV_HALO = 16
STORE_DTYPE = jnp.bfloat16
N_CHIPS = 4
N_DEV = 8
ALPHA = (2.0 * DEPTH) ** 0.25
LN_EPS = 1e-5
ADAM_LR, ADAM_B1, ADAM_B2, ADAM_EPS, ADAM_WD, ADAM_STEP = 0.001, 0.9, 0.999, 1e-08, 0.01, 10

TS_PROJ = 512
TS_MM = 1024
TS_MIX = 512
TS_FF = 256
TS_TN = 1024
VMEM_LIMIT = 52 * 2 ** 20

MESH = pl.DeviceIdType.MESH
ANY = pl.BlockSpec(memory_space=pl.ANY)

_GELU_K0 = math.sqrt(2.0 / math.pi)
_GELU_K1 = 0.044715


def _params(n_axes):
    return pltpu.CompilerParams(dimension_semantics=("arbitrary",) * n_axes, vmem_limit_bytes=VMEM_LIMIT)


class _Exchange:
    def __init__(self, inputs, inplace, fresh, n_copies, copies):
        self.inputs, self.inplace, self.fresh, self.n_copies, self.copies = inputs, inplace, fresh, n_copies, copies

    def out_shapes(self):
        return [jax.ShapeDtypeStruct(self.inputs[i].shape, self.inputs[i].dtype) for i in self.inplace] + list(self.fresh)

    def semaphores(self):
        return [pltpu.SemaphoreType.DMA((self.n_copies,)), pltpu.SemaphoreType.DMA((self.n_copies,))]

    def aliases(self, first_input, first_output):
        return {first_input + i: first_output + k for k, i in enumerate(self.inplace)}


def _call(body, *, name, grid, in_specs, out_specs, out_shape, args, scratch_shapes=(), exchange=None):
    n_axes = len(grid)
    in_specs, out_specs, out_shape, scratch_shapes = list(in_specs), list(out_specs), list(out_shape), list(scratch_shapes)
    if exchange is None:
        outs = pl.pallas_call(body, name=name, grid=grid, in_specs=in_specs, out_specs=out_specs, out_shape=out_shape,
                              scratch_shapes=scratch_shapes, compiler_params=_params(n_axes))(*args)
        return list(outs), []
    n_in, n_out, n_scr = len(in_specs), len(out_shape), len(scratch_shapes)
    x_out = exchange.out_shapes()
    n_xin, n_xout = len(exchange.inputs), len(x_out)

    def hosted(*refs):
        ins, xin = refs[:n_in], refs[n_in:n_in + n_xin]
        o = n_in + n_xin
        outs, xout = refs[o:o + n_out], refs[o + n_out:o + n_out + n_xout]
        s = o + n_out + n_xout
        scr, (send_sems, recv_sems) = refs[s:s + n_scr], refs[s + n_scr:]
        first = functools.reduce(jnp.logical_and, [pl.program_id(d) == 0 for d in range(n_axes)])
        last = functools.reduce(jnp.logical_and, [pl.program_id(d) == pl.num_programs(d) - 1 for d in range(n_axes)])

        @pl.when(first)
        def _():
            for cp in exchange.copies(xin, xout, send_sems, recv_sems)[0]:
                cp.start()

        body(*ins, *outs, *scr)

        @pl.when(last)
        def _():
            sends, arrivals = exchange.copies(xin, xout, send_sems, recv_sems)
            for cp in arrivals:
                cp.wait_recv()
            for cp in sends:
                cp.wait_send()

    results = pl.pallas_call(
        hosted, name=name, grid=grid, in_specs=in_specs + [ANY] * n_xin, out_specs=out_specs + [ANY] * n_xout,
        out_shape=out_shape + x_out, scratch_shapes=scratch_shapes + exchange.semaphores(),
        input_output_aliases=exchange.aliases(n_in, n_out), compiler_params=_params(n_axes),
    )(*args, *exchange.inputs)
    return list(results[:n_out]), list(results[n_out:])


def _run_exchange(exchange, name):
    x_out = exchange.out_shapes()
    n_xin, n_xout = len(exchange.inputs), len(x_out)

    def body(*refs):
        xin, xout = refs[:n_xin], refs[n_xin:n_xin + n_xout]
        send_sems, recv_sems = refs[n_xin + n_xout:]
        sends, arrivals = exchange.copies(xin, xout, send_sems, recv_sems)
        for cp in sends:
            cp.start()
        for cp in arrivals:
            cp.wait_recv()
        for cp in sends:
            cp.wait_send()

    return list(pl.pallas_call(
        body, name=name, in_specs=[ANY] * n_xin, out_specs=[ANY] * n_xout, out_shape=x_out,
        scratch_shapes=exchange.semaphores(), input_output_aliases=exchange.aliases(0, 0),
    )(*exchange.inputs))


def _dot(a, b):
    return jnp.dot(a, b, preferred_element_type=F32)


def _dot_nt(a, b):
    return lax.dot_general(a, b, (((1,), (1,)), ((), ())), preferred_element_type=F32)


def _dot_tn(a, b):
    return lax.dot_general(a, b, (((0,), (0,)), ((), ())), preferred_element_type=F32)


def _gelu(x):
    x2 = x * x
    t = jnp.tanh(x * (x2 * (_GELU_K0 * _GELU_K1) + _GELU_K0))
    cdf = 0.5 * t + 0.5
    dg = cdf + (x * (1.0 - t * t)) * (x2 * (1.5 * _GELU_K0 * _GELU_K1) + 0.5 * _GELU_K0)
    return x * cdf, dg


def _colsum(x):
    return jnp.sum(x, axis=0, keepdims=True)


def _row(d):
    return pl.BlockSpec((1, d), lambda *_: (0, 0))


def _full(shape):
    n = len(shape)
    return pl.BlockSpec(shape, lambda *_: (0,) * n)


def _layer_block(shape, layer):
    n = len(shape)
    return pl.BlockSpec((None,) + tuple(shape), lambda *_: (layer,) + (0,) * n)


def _pool_counts(first_row, rows, window):
    t = first_row + lax.broadcasted_iota(jnp.int32, (rows, 1), 0)
    return 1.0 / jnp.minimum(t + 1, window).astype(F32)


def _pool_forward(a_ext, a, first_row, rows):
    out = []
    for g, window in enumerate(POOL_WINDOWS):
        s = a_ext[:, g * GROUP:(g + 1) * GROUP]
        k = 1
        while k < window:
            s = s + pltpu.roll(s, k, 0)
            k *= 2
        inv = _pool_counts(first_row, rows, window)
        out.append(s[POOL_HALO:] * inv - a[:, g * GROUP:(g + 1) * GROUP])
    return out


def _sgu_norm(gv):
    mu = jnp.mean(gv, axis=-1, keepdims=True)
    xc = gv - mu
    var = jnp.mean(xc * xc, axis=-1, keepdims=True)
    rs = lax.rsqrt(var + LN_EPS)
    return xc * rs, rs


def _modulate_input(x, shift, scale):
    S = x.shape[0]
    ts = min(TS_PROJ, S)

    def body(x_ref, sh_ref, sc_ref, h_ref):
        h_ref[...] = (x_ref[...] * (1.0 + sc_ref[...]) + sh_ref[...]).astype(MXU_DTYPE)

    tile = pl.BlockSpec((ts, D), lambda i: (i, 0))
    return pl.pallas_call(
        body, name="modulate_input", grid=(S // ts,), in_specs=[tile, _row(D), _row(D)], out_specs=tile,
        out_shape=jax.ShapeDtypeStruct((S, D), MXU_DTYPE), compiler_params=_params(1),
    )(x, shift, scale)


def _matmul(a, w_all, layer, tn, out_dtype, name, exchange=None):
    S, K = a.shape
    N = w_all.shape[2]
    ts = min(TS_MM, S)

    def body(a_ref, w_ref, o_ref):
        o_ref[...] = _dot(a_ref[...], w_ref[...]).astype(out_dtype)

    (out,), extra = _call(
        body, name=name, grid=(N // tn, S // ts),
        in_specs=[pl.BlockSpec((ts, K), lambda j, i: (i, 0)), pl.BlockSpec((None, K, tn), lambda j, i: (layer, 0, j))],
        out_specs=[pl.BlockSpec((ts, tn), lambda j, i: (i, j))],
        out_shape=[jax.ShapeDtypeStruct((S, N), out_dtype)], args=(a, w_all), exchange=exchange)
    return out, extra


def _matmul_nt(a, w_all, layer, tk, name, exchange=None):
    S, K = a.shape
    N = w_all.shape[1]
    ts = min(TS_MM, S)
    n_k = K // tk

    def body(a_ref, w_ref, o_ref, acc):
        k = pl.program_id(1)
        part = _dot_nt(a_ref[...], w_ref[...])
        if n_k == 1:
            o_ref[...] = part
        else:
            @pl.when(k == 0)
            def _():
                acc[...] = part

            @pl.when(jnp.logical_and(k > 0, k < n_k - 1))
            def _():
                acc[...] += part

            @pl.when(k == n_k - 1)
            def _():
                o_ref[...] = acc[...] + part

    (out,), extra = _call(
        body, name=name, grid=(S // ts, n_k),
        in_specs=[pl.BlockSpec((ts, tk), lambda i, k: (i, k)), pl.BlockSpec((None, N, tk), lambda i, k: (layer, 0, k))],
        out_specs=[pl.BlockSpec((ts, N), lambda i, k: (i, 0))],
        out_shape=[jax.ShapeDtypeStruct((S, N), F32)],
        scratch_shapes=[pltpu.VMEM((ts, N), F32)], args=(a, w_all), exchange=exchange)
    return out, extra


def _f2_mixers(proj, xn, lg, lb, gate, pool_w, pool_scale, sg, sb, wm, bias_full, w_out_all, lg1, lb1, shift2, scale2,
               layer, exchange=None):
    S = xn.shape[0]
    ts = min(TS_MIX, S)
    n_chunks = ts // CHUNK
    halo_blocks = ts // POOL_HALO

    def body(p_ref, halo_ref, xn_ref, lg_ref, lb_ref, gate_ref, pw_ref, ps_ref, sg_ref, sb_ref, wm_ref,
             bias_ref, wo_ref, lg1_ref, lb1_ref, sh2_ref, sc2_ref, mix_ref, f_ref, xh_ref, rs_ref, h2_ref, gel_ref,
             z_scr):
        i = pl.program_id(0)
        a = p_ref[:, 0:D_POOL].astype(F32)
        u = p_ref[:, D_POOL:D_POOL + D_SGU].astype(F32)
        v = p_ref[:, D_POOL + D_SGU:D_IN].astype(F32)
        halo = halo_ref[...].astype(F32) * (i > 0).astype(F32)
        a_ext = jnp.concatenate([halo, a], axis=0)
        pooled = _pool_forward(a_ext, a, i * ts, ts)
        for g in range(N_GROUPS):
            mixed = _dot(pooled[g].astype(MXU_DTYPE), pw_ref[g])
            mix_ref[:, g * GROUP:(g + 1) * GROUP] = (mixed * ps_ref[:, g * GROUP:(g + 1) * GROUP]).astype(MXU_DTYPE)
        gu, dgu = _gelu(u)
        gv, dgv = _gelu(v)
        for k, part in enumerate((gu, dgu, gv, dgv)):
            gel_ref[:, k * D_SGU:(k + 1) * D_SGU] = part.astype(STORE_DTYPE)
        vhat, _ = _sgu_norm(gv)
        vn = (vhat * sg_ref[...] + sb_ref[...]).astype(MXU_DTYPE)
        for h in range(HEADS):
            cols = slice(h * GROUP, (h + 1) * GROUP)
            vn_h = jnp.concatenate([vn[c * CHUNK:(c + 1) * CHUNK, cols] for c in range(n_chunks)], axis=1)
            z_h = _dot(wm_ref[h], vn_h)
            for c in range(n_chunks):
                z_scr[c * CHUNK:(c + 1) * CHUNK, cols] = z_h[:, c * GROUP:(c + 1) * GROUP] + bias_ref[:, cols]
        mix_ref[:, D_POOL:D] = (gu * z_scr[...]).astype(MXU_DTYPE)
        f = _dot(mix_ref[...], wo_ref[...])
        f_ref[...] = f.astype(STORE_DTYPE)
        x = xn_ref[...] * lg_ref[...] + lb_ref[...]
        z1 = ALPHA * x + gate_ref[...] * f
        mu = jnp.mean(z1, axis=-1, keepdims=True)
        zc = z1 - mu
        var = jnp.mean(zc * zc, axis=-1, keepdims=True)
        rs = lax.rsqrt(var + LN_EPS)
        xhat = zc * rs
        xh_ref[...] = xhat
        rs_ref[...] = rs
        x1 = xhat * lg1_ref[...] + lb1_ref[...]
        h2_ref[...] = (x1 * (1.0 + sc2_ref[...]) + sh2_ref[...]).astype(MXU_DTYPE)

    tile = lambda w: pl.BlockSpec((ts, w), lambda i: (i, 0))
    return _call(
        body, name=f"f2_mixers_l{layer}", grid=(S // ts,),
        in_specs=[tile(D_IN),
                  pl.BlockSpec((POOL_HALO, D_POOL), lambda i: (jnp.maximum(i * halo_blocks - 1, 0), 0)),
                  tile(D), _row(D), _row(D), _row(D),
                  _full((N_GROUPS, GROUP, GROUP)), _row(D_POOL), _row(D_SGU), _row(D_SGU),
                  _full((HEADS, CHUNK, CHUNK)), _full((CHUNK, D_SGU)), _layer_block((D, D), layer),
                  _row(D), _row(D), _row(D), _row(D)],
        out_specs=[tile(D), tile(D), tile(D), tile(1), tile(D), tile(4 * D_SGU)],
        out_shape=[jax.ShapeDtypeStruct((S, D), MXU_DTYPE), jax.ShapeDtypeStruct((S, D), STORE_DTYPE),
                   jax.ShapeDtypeStruct((S, D), F32), jax.ShapeDtypeStruct((S, 1), F32),
                   jax.ShapeDtypeStruct((S, D), MXU_DTYPE), jax.ShapeDtypeStruct((S, 4 * D_SGU), STORE_DTYPE)],
        scratch_shapes=[pltpu.VMEM((ts, D_SGU), F32)],
        args=(proj, proj, xn, lg, lb, gate, pool_w, pool_scale, sg, sb, wm, bias_full, w_out_all, lg1, lb1, shift2,
              scale2), exchange=exchange)


def _conv_forward(g_ext, cw_ref, cb_ref):
    gm2 = pltpu.roll(g_ext, 2, 0)[CONV_HALO:]
    gm1 = pltpu.roll(g_ext, 1, 0)[CONV_HALO:]
    g0 = g_ext[CONV_HALO:]
    gc = ((cb_ref[...] + gm2 * cw_ref[0:1, :]) + gm1 * cw_ref[1:2, :]) + g0 * cw_ref[2:3, :]
    return gc, gm2, gm1


def _f4_ffn(up, xhat1, lg, lb, gate, conv_w, conv_b, w_down_all, lg2, lb2, shift_next, scale_next, layer,
            exchange=None):
    S = xhat1.shape[0]
    ts = min(TS_FF, S)
    halo_blocks = ts // CONV_HALO

    def body(g_ref, halo_ref, val_ref, xh_ref, lg_ref, lb_ref, gate_ref, cw_ref, cb_ref, wd_ref, lg2_ref, lb2_ref,
             shn_ref, scn_ref, act_ref, f_ref, xo_ref, rs_ref, hn_ref, ge_ref, dge_ref):
        i = pl.program_id(0)
        halo = halo_ref[...].astype(F32) * (i > 0).astype(F32)
        g_ext = jnp.concatenate([halo, g_ref[...].astype(F32)], axis=0)
        gc, _, _ = _conv_forward(g_ext, cw_ref, cb_ref)
        ge, dge = _gelu(gc)
        ge_ref[...] = ge.astype(STORE_DTYPE)
        dge_ref[...] = dge.astype(STORE_DTYPE)
        act = (ge * val_ref[...].astype(F32)).astype(MXU_DTYPE)
        act_ref[...] = act
        f = _dot(act, wd_ref[...])
        f_ref[...] = f.astype(STORE_DTYPE)
        x = xh_ref[...] * lg_ref[...] + lb_ref[...]
        z = ALPHA * x + gate_ref[...] * f
        mu = jnp.mean(z, axis=-1, keepdims=True)
        zc = z - mu
        var = jnp.mean(zc * zc, axis=-1, keepdims=True)
        rs = lax.rsqrt(var + LN_EPS)
        xhat = zc * rs
        xo_ref[...] = xhat
        rs_ref[...] = rs
        x2 = xhat * lg2_ref[...] + lb2_ref[...]
        hn_ref[...] = (x2 * (1.0 + scn_ref[...]) + shn_ref[...]).astype(MXU_DTYPE)

    tile = lambda w: pl.BlockSpec((ts, w), lambda i: (i, 0))
    return _call(
        body, name=f"f4_ffn_l{layer}", grid=(S // ts,),
        in_specs=[pl.BlockSpec((ts, D_FF), lambda i: (i, 0)),
                  pl.BlockSpec((CONV_HALO, D_FF), lambda i: (jnp.maximum(i * halo_blocks - 1, 0), 0)),
                  pl.BlockSpec((ts, D_FF), lambda i: (i, 1)),
                  tile(D), _row(D), _row(D), _row(D), _full((3, D_FF)), _row(D_FF),
                  _layer_block((D_FF, D), layer), _row(D), _row(D), _row(D), _row(D)],
        out_specs=[tile(D_FF), tile(D), tile(D), tile(1), tile(D), tile(D_FF), tile(D_FF)],
        out_shape=[jax.ShapeDtypeStruct((S, D_FF), MXU_DTYPE), jax.ShapeDtypeStruct((S, D), STORE_DTYPE),
                   jax.ShapeDtypeStruct((S, D), F32), jax.ShapeDtypeStruct((S, 1), F32),
                   jax.ShapeDtypeStruct((S, D), MXU_DTYPE), jax.ShapeDtypeStruct((S, D_FF), STORE_DTYPE),
                   jax.ShapeDtypeStruct((S, D_FF), STORE_DTYPE)],
        args=(up, up, up, xhat1, lg, lb, gate, conv_w, conv_b, w_down_all, lg2, lb2, shift_next, scale_next),
        exchange=exchange)


def _loss_head(xhat, lg, lb, target):
    S = xhat.shape[0]
    ts = min(TS_PROJ, S)

    def body(xh_ref, lg_ref, lb_ref, t_ref, loss_ref, dy_ref, acc):
        i = pl.program_id(0)

        @pl.when(i == 0)
        def _():
            acc[...] = jnp.zeros_like(acc)

        err = (xh_ref[...] * lg_ref[...] + lb_ref[...]) - t_ref[...]
        dy_ref[...] = err * (1.0 / D)
        acc[...] += _colsum(err * err)

        @pl.when(i == pl.num_programs(0) - 1)
        def _():
            loss_ref[...] = jnp.sum(acc[...], axis=1, keepdims=True) * (0.5 / D)

    tile = pl.BlockSpec((ts, D), lambda i: (i, 0))
    return pl.pallas_call(
        body, name="loss_head", grid=(S // ts,),
        in_specs=[tile, _row(D), _row(D), tile],
        out_specs=[_full((1, 1)), tile],
        out_shape=[jax.ShapeDtypeStruct((1, 1), F32), jax.ShapeDtypeStruct((S, D), F32)],
        scratch_shapes=[pltpu.VMEM((1, D), F32)],
        compiler_params=_params(1),
    )(xhat, lg, lb, target)


def _accumulate(i, ref, value):
    @pl.when(i == 0)
    def _():
        ref[...] = value

    @pl.when(i > 0)
    def _():
        ref[...] += value


def _b_ln(d, xhat, rstd, f, lg, lb, gate, name, modulated=None, exchange=None):
    S = d.shape[0]
    ts = min(TS_PROJ, S)
    has_mod = modulated is not None

    def body(*refs):
        if has_mod:
            (d_ref, xh_ref, rs_ref, f_ref, lg_ref, lb_ref, gate_ref, dri_ref, sc_ref,
             df_ref, dres_ref, dlg_ref, dlb_ref, dgate_ref, dsc_ref, dsh_ref) = refs
        else:
            (d_ref, xh_ref, rs_ref, f_ref, lg_ref, lb_ref, gate_ref,
             df_ref, dres_ref, dlg_ref, dlb_ref, dgate_ref) = refs
        i = pl.program_id(0)
        xh = xh_ref[...]
        if has_mod:
            dh = d_ref[...]
            dxo_t = dh * (1.0 + sc_ref[...]) + dri_ref[...]
            _accumulate(i, dsc_ref, _colsum(dh * (xh * lg_ref[...] + lb_ref[...])))
            _accumulate(i, dsh_ref, _colsum(dh))
        else:
            dxo_t = d_ref[...]
        dxh = dxo_t * lg_ref[...]
        m1 = jnp.mean(dxh, axis=-1, keepdims=True)
        m2 = jnp.mean(dxh * xh, axis=-1, keepdims=True)
        dz = rs_ref[...] * (dxh - m1 - xh * m2)
        df_ref[...] = (dz * gate_ref[...]).astype(MXU_DTYPE)
        dres_ref[...] = ALPHA * dz
        _accumulate(i, dlg_ref, _colsum(dxo_t * xh))
        _accumulate(i, dlb_ref, _colsum(dxo_t))
        _accumulate(i, dgate_ref, _colsum(dz * f_ref[...].astype(F32)))

    tile = lambda w: pl.BlockSpec((ts, w), lambda i: (i, 0))
    n_sums = 5 if has_mod else 3
    in_specs = [tile(D), tile(D), tile(1), tile(D), _row(D), _row(D), _row(D)]
    args = [d, xhat, rstd, f, lg, lb, gate]
    if has_mod:
        in_specs += [tile(D), _row(D)]
        args += list(modulated)
    return _call(
        body, name=name, grid=(S // ts,), in_specs=in_specs,
        out_specs=[tile(D), tile(D)] + [_row(D)] * n_sums,
        out_shape=[jax.ShapeDtypeStruct((S, D), MXU_DTYPE), jax.ShapeDtypeStruct((S, D), F32)]
        + [jax.ShapeDtypeStruct((1, D), F32)] * n_sums,
        args=args, exchange=exchange)


def _input_grad(dh, dres, scale, x):
    S = dh.shape[0]
    ts = min(TS_PROJ, S)

    def body(dh_ref, dres_ref, sc_ref, x_ref, dx_ref, dsc_ref, dsh_ref):
        i = pl.program_id(0)
        dh_t = dh_ref[...]
        dx_ref[...] = dh_t * (1.0 + sc_ref[...]) + dres_ref[...]
        _accumulate(i, dsc_ref, _colsum(dh_t * x_ref[...]))
        _accumulate(i, dsh_ref, _colsum(dh_t))

    tile = pl.BlockSpec((ts, D), lambda i: (i, 0))
    return pl.pallas_call(
        body, name="input_grad", grid=(S // ts,), in_specs=[tile, tile, _row(D), tile],
        out_specs=[tile, _row(D), _row(D)],
        out_shape=[jax.ShapeDtypeStruct((S, D), F32)] + [jax.ShapeDtypeStruct((1, D), F32)] * 2,
        compiler_params=_params(1),
    )(dh, dres, scale, x)


def _b2_ffn(df2, w_down_all, up, ge, dge, conv_w, layer, exchange=None):
    S = df2.shape[0]
    ts = min(TS_FF, S)
    n_tiles = S // ts
    n_ext = ts + CONV_HALO

    def body(df_ref, wd_ref, g_ref, val_ref, ge_ref, dge_ref, cw_ref, dup_ref, dcw_ref, dcb_ref, next_dgc):
        i = pl.program_id(0)

        @pl.when(i == 0)
        def _():
            next_dgc[...] = jnp.zeros_like(next_dgc)

        dact = _dot_nt(df_ref[...], wd_ref[...])
        dup_ref[:, D_FF:2 * D_FF] = (dact * ge_ref[...].astype(F32)).astype(MXU_DTYPE)
        dgc = dact * val_ref[...].astype(F32) * dge_ref[...].astype(F32)
        ext = jnp.concatenate([dgc, next_dgc[...]], axis=0)
        dp1 = pltpu.roll(ext, n_ext - 1, 0)[:ts]
        dp2 = pltpu.roll(ext, n_ext - 2, 0)[:ts]
        dup_ref[:, 0:D_FF] = (dgc * cw_ref[2:3, :] + dp1 * cw_ref[1:2, :] + dp2 * cw_ref[0:1, :]).astype(MXU_DTYPE)
        next_dgc[...] = dgc[0:CONV_HALO]
        g = g_ref[...].astype(F32)
        dcw = jnp.concatenate([_colsum(dp2 * g), _colsum(dp1 * g), _colsum(dgc * g)], axis=0)
        _accumulate(i, dcw_ref, dcw)
        _accumulate(i, dcb_ref, _colsum(dgc))

    tile = lambda w, col=0: pl.BlockSpec((ts, w), lambda i: (n_tiles - 1 - i, col))
    return _call(
        body, name=f"b2_ffn_l{layer}", grid=(n_tiles,),
        in_specs=[tile(D), _layer_block((D_FF, D), layer), tile(D_FF), tile(D_FF, 1), tile(D_FF), tile(D_FF),
                  _full((3, D_FF))],
        out_specs=[tile(2 * D_FF), _full((3, D_FF)), _row(D_FF)],
        out_shape=[jax.ShapeDtypeStruct((S, 2 * D_FF), MXU_DTYPE),
                   jax.ShapeDtypeStruct((3, D_FF), F32), jax.ShapeDtypeStruct((1, D_FF), F32)],
        scratch_shapes=[pltpu.VMEM((CONV_HALO, D_FF), F32)],
        args=(df2, w_down_all, up, up, ge, dge, conv_w), exchange=exchange)


def _b5_mixers(df1, w_out_all, proj, gel, pool_w, pool_scale, sg, sb, wm, bias_full, layer):
    S = df1.shape[0]
    ts = min(TS_MIX, S)
    n_chunks = ts // CHUNK
    halo_blocks = ts // POOL_HALO
    last_halo = S // POOL_HALO - 1

    def body(df_ref, dfh_ref, wo_ref, p_ref, ah_ref, gel_ref, pw_ref, ps_ref, sg_ref, sb_ref, wm_ref, bias_ref,
             dp_ref, dpw_ref, dps_ref, dsg_ref, dsb_ref, dwm_ref, dbias_ref, z_scr, dvn_scr):
        i = pl.program_id(0)
        last = pl.num_programs(0) - 1
        dmix = _dot_nt(df_ref[...], wo_ref[...])
        dmix_halo = _dot_nt(dfh_ref[...], wo_ref[0:D_POOL, :]) * (i < last).astype(F32)

        a = p_ref[:, 0:D_POOL].astype(F32)
        halo = ah_ref[...].astype(F32) * (i > 0).astype(F32)
        pooled = _pool_forward(jnp.concatenate([halo, a], axis=0), a, i * ts, ts)
        n = ts + POOL_HALO
        dps_parts = []
        for g, window in enumerate(POOL_WINDOWS):
            cols = slice(g * GROUP, (g + 1) * GROUP)
            pooled_b = pooled[g].astype(MXU_DTYPE)
            mixed = _dot(pooled_b, pw_ref[g])
            dya = dmix[:, cols]
            dps_parts.append(_colsum(dya * mixed))
            dmixed = (dya * ps_ref[:, cols]).astype(MXU_DTYPE)
            dpw_g = _dot_tn(pooled_b, dmixed)

            @pl.when(i == 0)
            def _():
                dpw_ref[g] = dpw_g

            @pl.when(i > 0)
            def _():
                dpw_ref[g] += dpw_g

            dpooled = _dot_nt(dmixed, pw_ref[g])
            dmixed_h = (dmix_halo[:, cols] * ps_ref[:, cols]).astype(MXU_DTYPE)
            dpooled_h = _dot_nt(dmixed_h, pw_ref[g])
            q = dpooled * _pool_counts(i * ts, ts, window)
            s = jnp.concatenate([q, dpooled_h * (1.0 / window)], axis=0)
            k = 1
            while k < window:
                s = s + pltpu.roll(s, n - k, 0)
                k *= 2
            dp_ref[:, cols] = (s[:ts] - dpooled).astype(MXU_DTYPE)
        _accumulate(i, dps_ref, jnp.concatenate(dps_parts, axis=1))

        gu, dgu, gv, dgv = (gel_ref[:, k * D_SGU:(k + 1) * D_SGU].astype(F32) for k in range(4))
        vhat, rs = _sgu_norm(gv)
        vn = (vhat * sg_ref[...] + sb_ref[...]).astype(MXU_DTYPE)
        dyb = dmix[:, D_POOL:D]
        dz = dyb * gu
        dzb = dz.astype(MXU_DTYPE)
        dbias = dz[0:CHUNK]
        for c in range(1, n_chunks):
            dbias = dbias + dz[c * CHUNK:(c + 1) * CHUNK]
        _accumulate(i, dbias_ref, dbias)
        for h in range(HEADS):
            cols = slice(h * GROUP, (h + 1) * GROUP)
            vn_h = jnp.concatenate([vn[c * CHUNK:(c + 1) * CHUNK, cols] for c in range(n_chunks)], axis=1)
            dz_h = jnp.concatenate([dzb[c * CHUNK:(c + 1) * CHUNK, cols] for c in range(n_chunks)], axis=1)
            z_h = _dot(wm_ref[h], vn_h)
            dvn_h = _dot_tn(wm_ref[h], dz_h)
            dwm_h = _dot_nt(dz_h, vn_h)
            for c in range(n_chunks):
                rows = slice(c * CHUNK, (c + 1) * CHUNK)
                z_scr[rows, cols] = z_h[:, c * GROUP:(c + 1) * GROUP] + bias_ref[:, cols]
                dvn_scr[rows, cols] = dvn_h[:, c * GROUP:(c + 1) * GROUP]

            @pl.when(i == 0)
            def _():
                dwm_ref[h] = dwm_h

            @pl.when(i > 0)
            def _():
                dwm_ref[h] += dwm_h

        dp_ref[:, D_POOL:D_POOL + D_SGU] = (dyb * z_scr[...] * dgu).astype(MXU_DTYPE)
        dvn = dvn_scr[...]
        _accumulate(i, dsg_ref, _colsum(dvn * vhat))
        _accumulate(i, dsb_ref, _colsum(dvn))
        dvh = dvn * sg_ref[...]
        m1 = jnp.mean(dvh, axis=-1, keepdims=True)
        m2 = jnp.mean(dvh * vhat, axis=-1, keepdims=True)
        dp_ref[:, D_POOL + D_SGU:D_IN] = (rs * (dvh - m1 - vhat * m2) * dgv).astype(MXU_DTYPE)

        @pl.when(i == last)
        def _():
            tri = (lax.broadcasted_iota(jnp.int32, (CHUNK, CHUNK), 0)
                   >= lax.broadcasted_iota(jnp.int32, (CHUNK, CHUNK), 1))
            for h in range(HEADS):
                dwm_ref[h] = jnp.where(tri, dwm_ref[h], 0.0)

    tile = lambda w: pl.BlockSpec((ts, w), lambda i: (i, 0))
    return pl.pallas_call(
        body, name=f"b5_mixers_l{layer}", grid=(S // ts,),
        in_specs=[tile(D),
                  pl.BlockSpec((POOL_HALO, D), lambda i: (jnp.minimum((i + 1) * halo_blocks, last_halo), 0)),
                  _layer_block((D, D), layer), tile(D_POOL),
                  pl.BlockSpec((POOL_HALO, D_POOL), lambda i: (jnp.maximum(i * halo_blocks - 1, 0), 0)),
                  tile(4 * D_SGU),
                  _full((N_GROUPS, GROUP, GROUP)), _row(D_POOL), _row(D_SGU), _row(D_SGU),
                  _full((HEADS, CHUNK, CHUNK)), _full((CHUNK, D_SGU))],
        out_specs=[tile(D_IN), _full((N_GROUPS, GROUP, GROUP)), _row(D_POOL), _row(D_SGU), _row(D_SGU),
                   _full((HEADS, CHUNK, CHUNK)), _full((CHUNK, D_SGU))],
        out_shape=[jax.ShapeDtypeStruct((S, D_IN), MXU_DTYPE), jax.ShapeDtypeStruct((N_GROUPS, GROUP, GROUP), F32),
                   jax.ShapeDtypeStruct((1, D_POOL), F32), jax.ShapeDtypeStruct((1, D_SGU), F32),
                   jax.ShapeDtypeStruct((1, D_SGU), F32), jax.ShapeDtypeStruct((HEADS, CHUNK, CHUNK), F32),
                   jax.ShapeDtypeStruct((CHUNK, D_SGU), F32)],
        scratch_shapes=[pltpu.VMEM((ts, D_SGU), F32), pltpu.VMEM((ts, D_SGU), F32)],
        compiler_params=_params(1),
    )(df1, df1, w_out_all, proj, proj, gel, pool_w, pool_scale, sg, sb, wm, bias_full)


def _weight_grad(a, bs, tn, name, rows=TS_TN):
    S, M = a.shape
    counts = [b.shape[1] // tn for b in bs]
    starts = [sum(counts[:k]) for k in range(len(bs))]
    ts = min(rows, S)

    def body(a_ref, *refs):
        b_refs, o_ref = refs[:-1], refs[-1]
        j = pl.program_id(0)
        for b_ref, lo, n in zip(b_refs, starts, counts):
            @pl.when(jnp.logical_and(j >= lo, j < lo + n))
            def _():
                _accumulate(pl.program_id(1), o_ref, _dot_tn(a_ref[...], b_ref[...]))

    b_spec = lambda lo, n: pl.BlockSpec((ts, tn), lambda j, i: (i, jnp.clip(j - lo, 0, n - 1)))
    return pl.pallas_call(
        body, name=name, grid=(sum(counts), S // ts),
        in_specs=[pl.BlockSpec((ts, M), lambda j, i: (i, 0))] + [b_spec(lo, n) for lo, n in zip(starts, counts)],
        out_specs=pl.BlockSpec((M, tn), lambda j, i: (0, j)),
        out_shape=jax.ShapeDtypeStruct((M, sum(counts) * tn), F32), compiler_params=_params(2),
    )(a, *bs)


def _silu(x):
    return x * (1.0 / (1.0 + jnp.exp(-x)))


def _ada_forward(c_all, ada_w, ada_b_cols):
    n_cols = ada_w.shape[2]
    tc = 512

    def body(c_ref, w_ref, b_ref, o_ref):
        ca = _silu(c_ref[...]).astype(MXU_DTYPE)
        o_ref[...] = _dot(ca, w_ref[...].astype(MXU_DTYPE)) + b_ref[...]

    return pl.pallas_call(
        body, name="ada_forward", grid=(DEPTH, n_cols // tc),
        in_specs=[pl.BlockSpec((16, D), lambda l, j: (0, 0)), pl.BlockSpec((None, D, tc), lambda l, j: (l, 0, j)),
                  pl.BlockSpec((None, 1, tc), lambda l, j: (l, 0, j))],
        out_specs=pl.BlockSpec((None, 16, tc), lambda l, j: (l, 0, j)),
        out_shape=jax.ShapeDtypeStruct((DEPTH, 16, n_cols), F32),
        compiler_params=_params(2),
    )(c_all, ada_w, ada_b_cols)


def _ada_backward(c_all, dmod_cols):
    n_cols = dmod_cols.shape[2]
    tc = 512

    def body(c_ref, d_ref, o_ref):
        ca = _silu(c_ref[...]).astype(MXU_DTYPE)
        o_ref[...] = _dot_tn(ca, d_ref[...].astype(MXU_DTYPE))

    return pl.pallas_call(
        body, name="ada_backward", grid=(DEPTH, n_cols // tc),
        in_specs=[pl.BlockSpec((16, D), lambda l, j: (0, 0)), pl.BlockSpec((None, 16, tc), lambda l, j: (l, 0, j))],
        out_specs=pl.BlockSpec((None, D, tc), lambda l, j: (l, 0, j)),
        out_shape=jax.ShapeDtypeStruct((DEPTH, D, n_cols), F32),
        compiler_params=_params(2),
    )(c_all, dmod_cols)


def _adamw(w, g, m, v, name):
    R, C = w.shape
    tr = R
    for cand in (512, 256, 128, 64, 32, 16, 8):
        if R % cand == 0 and cand * C * 4 <= 2 ** 21:
            tr = cand
            break
    c1 = 1.0 - ADAM_B1 ** ADAM_STEP
    c2 = 1.0 - ADAM_B2 ** ADAM_STEP

    def body(w_ref, g_ref, m_ref, v_ref, d_ref, mo_ref, vo_ref):
        gg = g_ref[...]
        mn = ADAM_B1 * m_ref[...] + (1.0 - ADAM_B1) * gg
        vn = ADAM_B2 * v_ref[...] + (1.0 - ADAM_B2) * (gg * gg)
        mo_ref[...] = mn
        vo_ref[...] = vn
        d_ref[...] = -ADAM_LR * ((mn / c1) / (jnp.sqrt(vn / c2) + ADAM_EPS) + ADAM_WD * w_ref[...])

    tile = pl.BlockSpec((tr, C), lambda i: (i, 0))
    return pl.pallas_call(
        body, name=name, grid=(R // tr,), in_specs=[tile] * 4, out_specs=[tile] * 3,
        out_shape=[jax.ShapeDtypeStruct((R, C), F32)] * 3, compiler_params=_params(1),
    )(w, g, m, v)


def _position():
    x, y, c = lax.axis_index("x"), lax.axis_index("y"), lax.axis_index("c")
    other_chips = [(1 - x, y), (x, 1 - y), (1 - x, 1 - y)]
    return x, y, c, other_chips


def _all_gather8(block, name):
    R, C = block.shape

    def body(x_ref, out_ref, send_sems, recv_sems, local_sem):
        x, y, c, chips = _position()
        me, sibling = (x, y, c), (x, y, 1 - c)

        def rows(px, py, pc):
            return out_ref.at[pl.ds((4 * px + 2 * py + pc) * R, R), :]

        def copy(k, blk, to, src=None):
            return pltpu.make_async_remote_copy(
                src_ref=rows(*blk) if src is None else src, dst_ref=rows(*blk),
                send_sem=send_sems.at[k], recv_sem=recv_sems.at[k], device_id=to, device_id_type=MESH)

        mine = pltpu.make_async_copy(x_ref, rows(*me), local_sem)
        mine.start()
        first = [copy(0, me, sibling, src=x_ref)]
        first += [copy(1 + j, me, (*chip, c), src=x_ref) for j, chip in enumerate(chips)]
        for cp in first:
            cp.start()
        passed = [copy(4 + j, (*chip, c), sibling) for j, chip in enumerate(chips)]
        for j, chip in enumerate(chips):
            copy(1 + j, (*chip, c), me).wait_recv()
            passed[j].start()
        copy(0, sibling, me).wait_recv()
        for j, chip in enumerate(chips):
            copy(4 + j, (*chip, 1 - c), me).wait_recv()
        for cp in first + passed:
            cp.wait_send()
        mine.wait()

    return pl.pallas_call(
        body, name=name, out_shape=jax.ShapeDtypeStruct((N_DEV * R, C), block.dtype),
        in_specs=[pl.BlockSpec(memory_space=pltpu.VMEM)], out_specs=ANY,
        scratch_shapes=[pltpu.SemaphoreType.DMA((7,)), pltpu.SemaphoreType.DMA((7,)), pltpu.SemaphoreType.DMA(())],
    )(block)


def _gather_flat(vec, name):
    n = vec.shape[0]
    padded = -(-n // 1024) * 1024
    block = jnp.pad(vec, (0, padded - n)).reshape(8, padded // 8)
    out = _all_gather8(block, name)
    return out.reshape(N_DEV, padded)[:, :n]


_SHARD_KINDS = ("cols", "rows", "cols", "rows")


def _shard_of(ref, shape, kind, chip):
    m, n = shape
    return ref.at[:, pl.ds(chip * (n // 4), n // 4)] if kind == "cols" else ref.at[pl.ds(chip * (m // 4), m // 4), :]


def _half_of(ref, shape, kind, h):
    m, n = shape
    return ref.at[pl.ds(h * (m // 2), m // 2), :] if kind == "cols" else ref.at[:, pl.ds(h * (n // 2), n // 2)]


def _half_shape(shape, kind):
    m, n = shape
    return (m // 2, n) if kind == "cols" else (m, n // 2)


def _shard_shape(shape, kind):
    m, n = shape
    return (m, n // 4) if kind == "cols" else (m // 4, n)


def _remote(src, dst, k, to, send_sems, recv_sems):
    return pltpu.make_async_remote_copy(src_ref=src, dst_ref=dst, send_sem=send_sems.at[k], recv_sem=recv_sems.at[k],
                                        device_id=to, device_id_type=MESH)


def _gather_exchange(shards, full, kind, layer):
    shape = full.shape[1:]

    def copies(xin, xout, send_sems, recv_sems):
        x, y, c, chips = _position()
        my_chip = 2 * x + y
        own = xin[0].at[layer]
        place = lambda chip: _shard_of(xout[0].at[layer], shape, kind, chip)
        peers = [(x, y, 1 - c)] + [(*chip, c) for chip in chips]
        sources = [my_chip] + [2 * cx + cy for cx, cy in chips]
        sends = [_remote(own, place(my_chip), k, peer, send_sems, recv_sems) for k, peer in enumerate(peers)]
        arrivals = [_remote(place(s), place(s), k, (x, y, c), send_sems, recv_sems) for k, s in enumerate(sources)]
        return sends, arrivals

    return _Exchange([shards, full], [1], [], 4, copies)


def _swap_exchange(grads):
    shapes = [g.shape for g in grads]
    fresh = [jax.ShapeDtypeStruct(_half_shape(s, k), F32) for s, k in zip(shapes, _SHARD_KINDS)]

    def copies(xin, xout, send_sems, recv_sems):
        x, y, c, _ = _position()
        sends = [_remote(_half_of(xin[a], shapes[a], _SHARD_KINDS[a], 1 - c), xout[a], a, (x, y, 1 - c),
                         send_sems, recv_sems) for a in range(len(grads))]
        arrivals = [_remote(xout[a], xout[a], a, (x, y, c), send_sems, recv_sems) for a in range(len(grads))]
        return sends, arrivals

    return _Exchange(list(grads), [], fresh, len(grads), copies)


def _scatter_exchange(partials):
    n_w = len(partials)
    shapes = [p.shape for p in partials]
    fresh = [jax.ShapeDtypeStruct((3,) + _shard_shape(s, k), p.dtype) for s, k, p in zip(shapes, _SHARD_KINDS, partials)]

    def copies(xin, xout, send_sems, recv_sems):
        x, y, c, chips = _position()
        sends, arrivals = [], []
        for j, (cx, cy) in enumerate(chips):
            for a in range(n_w):
                src = _shard_of(xin[a], shapes[a], _SHARD_KINDS[a], 2 * cx + cy)
                sends.append(_remote(src, xout[a].at[j], j * n_w + a, (cx, cy, c), send_sems, recv_sems))
                arrivals.append(_remote(xout[a].at[j], xout[a].at[j], j * n_w + a, (x, y, c), send_sems, recv_sems))
        return sends, arrivals

    return _Exchange(list(partials), [], fresh, 3 * n_w, copies)


def _share_exchange(reduced, layer):
    n_w = len(reduced)
    shapes = [r.shape[1:] for r in reduced]

    def copies(xin, xout, send_sems, recv_sems):
        x, y, c, _ = _position()
        half = lambda a, h: _half_of(xout[a].at[layer], shapes[a], _SHARD_KINDS[a], h)
        sends = [_remote(half(a, c), half(a, c), a, (x, y, 1 - c), send_sems, recv_sems) for a in range(n_w)]
        arrivals = [_remote(half(a, 1 - c), half(a, 1 - c), a, (x, y, c), send_sems, recv_sems) for a in range(n_w)]
        return sends, arrivals

    return _Exchange(list(reduced), list(range(n_w)), [], n_w, copies)


def _chip_partial(pos, grad, theirs, kind, name):
    M, N = grad.shape
    if kind == "cols":
        tm = 256
        steps = M // 2 // tm
        block, g_map = (tm, N), (lambda i, pos: (pos[0] * steps + i, 0))
    else:
        tm = M // 4
        steps = 4
        block, g_map = (tm, N // 2), (lambda i, pos: (i, pos[0]))

    def body(pos_ref, g_ref, t_ref, o_ref):
        o_ref[...] = (g_ref[...] + t_ref[...]).astype(WIRE_DTYPE)

    grid_spec = pltpu.PrefetchScalarGridSpec(
        num_scalar_prefetch=1, grid=(steps,),
        in_specs=[pl.BlockSpec(block, g_map), pl.BlockSpec(block, lambda i, pos: (i, 0))],
        out_specs=pl.BlockSpec(block, lambda i, pos: (i, 0)))
    return pl.pallas_call(
        body, name=name, grid_spec=grid_spec, out_shape=jax.ShapeDtypeStruct(_half_shape((M, N), kind), WIRE_DTYPE),
        compiler_params=_params(1),
    )(pos, grad, theirs)


def _reduce_shard(pos, grad, theirs, received, reduced, kind, layer, name):
    M, N = grad.shape
    if kind == "cols":
        tm = min(M // 2, 512)
        steps = M // 2 // tm
        block = (tm, N // 4)
        g_map = lambda i, pos: (pos[0] * steps + i, pos[1])
        t_map = lambda i, pos: (i, pos[1])
        o_map = lambda i, pos: (layer, pos[0] * steps + i, 0)
    else:
        steps = 1
        block = (M // 4, N // 2)
        g_map = lambda i, pos: (pos[1], pos[0])
        t_map = lambda i, pos: (pos[1], 0)
        o_map = lambda i, pos: (layer, 0, pos[0])
    out_shape = (DEPTH,) + _shard_shape((M, N), kind)

    def body(pos_ref, g_ref, t_ref, r0_ref, r1_ref, r2_ref, *rest):
        o_ref = rest[-1]
        chip = pos_ref[1]
        own = g_ref[...] + t_ref[...]
        r = [r0_ref[...].astype(F32), r1_ref[...].astype(F32), r2_ref[...].astype(F32)]
        total = None
        for s in range(N_CHIPS):
            rel = jnp.bitwise_xor(chip, s)
            term = jnp.where(rel == 0, own, jnp.where(rel == 2, r[0], jnp.where(rel == 1, r[1], r[2])))
            total = term if total is None else total + term
        o_ref[...] = total

    r_spec = lambda j: pl.BlockSpec((None,) + block, lambda i, pos: (j, i, 0))
    in_specs = [pl.BlockSpec(block, g_map), pl.BlockSpec(block, t_map), r_spec(0), r_spec(1), r_spec(2)]
    args = [pos, grad, theirs, received, received, received]
    aliases = {}
    if reduced is not None:
        in_specs.append(ANY)
        args.append(reduced)
        aliases = {6: 0}
    grid_spec = pltpu.PrefetchScalarGridSpec(
        num_scalar_prefetch=1, grid=(steps,), in_specs=in_specs, out_specs=pl.BlockSpec((None,) + block, o_map))
    return pl.pallas_call(
        body, name=name, grid_spec=grid_spec, out_shape=jax.ShapeDtypeStruct(out_shape, F32),
        input_output_aliases=aliases, compiler_params=_params(1),
    )(*args)


def _sum_devices(gathered):
    R8, C = gathered.shape
    R = R8 // N_DEV
    lanes = C // 128
    tc = 128 * max(k for k in range(1, lanes + 1) if lanes % k == 0 and k * 128 * R8 * 4 <= 2 ** 22)

    def body(g_ref, o_ref):
        total = g_ref[0:R, :]
        for d in range(1, N_DEV):
            total = total + g_ref[d * R:(d + 1) * R, :]
        o_ref[...] = total

    return pl.pallas_call(
        body, name="sum_devices", grid=(C // tc,),
        in_specs=[pl.BlockSpec((R8, tc), lambda j: (0, j))], out_specs=pl.BlockSpec((R, tc), lambda j: (0, j)),
        out_shape=jax.ShapeDtypeStruct((R, C), F32), compiler_params=_params(1),
    )(gathered)


_SMALL = ("pool_w", "pool_scale", "sgu_ln_g", "sgu_ln_b", "sgu_w", "sgu_b", "ln1_g", "ln1_b", "conv_w",
          "conv_b", "ln2_g", "ln2_b", "ada_b")
_WEIGHTS = ("ada_w", "ada_b", "w_in", "pool_w", "pool_scale", "sgu_ln_g", "sgu_ln_b", "sgu_w", "sgu_b", "w_out",
            "ln1_g", "ln1_b", "w_up", "conv_w", "conv_b", "w_down", "ln2_g", "ln2_b")


def kernel(x, c, ada_w, ada_b, w_in, pool_w, pool_scale, sgu_ln_g, sgu_ln_b, sgu_w, sgu_b, w_out, ln1_g, ln1_b, w_up, conv_w, conv_b, w_down, ln2_g, ln2_b, loss_target, m_ada_w, m_ada_b, m_w_in, m_pool_w, m_pool_scale, m_sgu_ln_g, m_sgu_ln_b, m_sgu_w, m_sgu_b, m_w_out, m_ln1_g, m_ln1_b, m_w_up, m_conv_w, m_conv_b, m_w_down, m_ln2_g, m_ln2_b, v_ada_w, v_ada_b, v_w_in, v_pool_w, v_pool_scale, v_sgu_ln_g, v_sgu_ln_b, v_sgu_w, v_sgu_b, v_w_out, v_ln1_g, v_ln1_b, v_w_up, v_conv_w, v_conv_b, v_w_down, v_ln2_g, v_ln2_b):
    weights = dict(ada_w=ada_w, ada_b=ada_b, w_in=w_in, pool_w=pool_w, pool_scale=pool_scale, sgu_ln_g=sgu_ln_g,
                   sgu_ln_b=sgu_ln_b, sgu_w=sgu_w, sgu_b=sgu_b, w_out=w_out, ln1_g=ln1_g, ln1_b=ln1_b, w_up=w_up,
                   conv_w=conv_w, conv_b=conv_b, w_down=w_down, ln2_g=ln2_g, ln2_b=ln2_b)
    mom_m = dict(ada_w=m_ada_w, ada_b=m_ada_b, w_in=m_w_in, pool_w=m_pool_w, pool_scale=m_pool_scale,
                 sgu_ln_g=m_sgu_ln_g, sgu_ln_b=m_sgu_ln_b, sgu_w=m_sgu_w, sgu_b=m_sgu_b, w_out=m_w_out,
                 ln1_g=m_ln1_g, ln1_b=m_ln1_b, w_up=m_w_up, conv_w=m_conv_w, conv_b=m_conv_b, w_down=m_w_down,
                 ln2_g=m_ln2_g, ln2_b=m_ln2_b)
    mom_v = dict(ada_w=v_ada_w, ada_b=v_ada_b, w_in=v_w_in, pool_w=v_pool_w, pool_scale=v_pool_scale,
                 sgu_ln_g=v_sgu_ln_g, sgu_ln_b=v_sgu_ln_b, sgu_w=v_sgu_w, sgu_b=v_sgu_b, w_out=v_w_out,
                 ln1_g=v_ln1_g, ln1_b=v_ln1_b, w_up=v_w_up, conv_w=v_conv_w, conv_b=v_conv_b, w_down=v_w_down,
                 ln2_g=v_ln2_g, ln2_b=v_ln2_b)

    ix, iy, ic = lax.axis_index("x"), lax.axis_index("y"), lax.axis_index("c")
    chip = 2 * ix + iy
    dev = 4 * ix + 2 * iy + ic
    pos = jnp.stack([ic, chip]).astype(jnp.int32)
    xs = x[0]
    target = loss_target[0]
    ff_shard = conv_w.shape[2]
    mod_shard = ada_w.shape[2]

    first = _gather_flat(jnp.concatenate([c.reshape(-1), conv_w.reshape(-1)]), "gather_c_conv")
    c_all = jnp.pad(first[:, :D], ((0, 8), (0, 0)))
    conv_parts = first[0::2, D:].reshape(N_CHIPS, DEPTH, 3, ff_shard)
    conv_full = jnp.transpose(conv_parts, (1, 2, 0, 3)).reshape(DEPTH, 3, D_FF)
    ada_b_cols = lax.dynamic_slice_in_dim(ada_b, chip * mod_shard, mod_shard, axis=1).reshape(DEPTH, 1, mod_shard)
    mod_part = _ada_forward(c_all, ada_w, ada_b_cols)[:, :8, :]
    mod_all = _gather_flat(mod_part.reshape(-1), "gather_mod").reshape(N_DEV, DEPTH, 8, mod_shard)
    mod_mine = lax.dynamic_index_in_dim(mod_all[0::2], dev, axis=2, keepdims=False)
    mod = jnp.transpose(mod_mine, (1, 0, 2)).reshape(DEPTH, 6, 1, D)

    sh_in, sh_out, sh_up, sh_down = (w.astype(WIRE_DTYPE) for w in (w_in, w_out, w_up, w_down))
    wf_in = lax.empty((DEPTH, D, D_IN), WIRE_DTYPE)
    wf_out = lax.empty((DEPTH, D, D), WIRE_DTYPE)
    wf_up = lax.empty((DEPTH, D, 2 * D_FF), WIRE_DTYPE)
    wf_down = lax.empty((DEPTH, D_FF, D), WIRE_DTYPE)
    (wf_in,) = _run_exchange(_gather_exchange(sh_in, wf_in, "cols", 0), "gather_w_in_l0")

    tri = jnp.tril(jnp.ones((CHUNK, CHUNK), dtype=bool))
    ones_row = jnp.ones((1, D), F32)
    zeros_row = jnp.zeros((1, D), F32)
    row = lambda a, l: a[l].reshape(1, -1)

    saved = []
    xn, lg, lb = xs, ones_row, zeros_row
    h = _modulate_input(xs, mod[0, 0], mod[0, 1])
    for l in range(DEPTH):
        shift1, scale1, gate1, shift2, scale2, gate2 = (mod[l, k] for k in range(6))
        nxt = min(l + 1, DEPTH - 1)
        pw = pool_w[l].astype(MXU_DTYPE)
        wm = jnp.where(tri[None], sgu_w[l], 0.0).astype(MXU_DTYPE)
        bias_full = jnp.repeat(jnp.transpose(sgu_b[l]), GROUP, axis=1)
        lg1, lb1 = row(ln1_g, l), row(ln1_b, l)
        proj, (wf_out,) = _matmul(h, wf_in, l, D_IN, STORE_DTYPE, f"f1_in_proj_l{l}",
                                  exchange=_gather_exchange(sh_out, wf_out, "rows", l))
        (mix, f1, xhat1, rstd1, h2, gel), (wf_up,) = _f2_mixers(
            proj, xn, lg, lb, gate1, pw, row(pool_scale, l), row(sgu_ln_g, l), row(sgu_ln_b, l), wm, bias_full,
            wf_out, lg1, lb1, shift2, scale2, l, exchange=_gather_exchange(sh_up, wf_up, "cols", l))
        up, (wf_down,) = _matmul(h2, wf_up, l, D_FF, STORE_DTYPE, f"f3_up_proj_l{l}",
                                 exchange=_gather_exchange(sh_down, wf_down, "rows", l))
        (act, f2, xhat2, rstd2, h_next, ge, dge), filled = _f4_ffn(
            up, xhat1, lg1, lb1, gate2, conv_full[l], row(conv_b, l), wf_down, row(ln2_g, l), row(ln2_b, l),
            mod[nxt, 0], mod[nxt, 1], l,
            exchange=_gather_exchange(sh_in, wf_in, "cols", l + 1) if l + 1 < DEPTH else None)
        if filled:
            (wf_in,) = filled
        saved.append(dict(xn=xn, lg=lg, lb=lb, h=h, proj=proj, mix=mix, f1=f1, xhat1=xhat1, rstd1=rstd1, h2=h2,
                          up=up, act=act, f2=f2, xhat2=xhat2, rstd2=rstd2, pw=pw, wm=wm, bias_full=bias_full,
                          ge=ge, dge=dge, gel=gel))
        xn, lg, lb, h = xhat2, row(ln2_g, l), row(ln2_b, l), h_next

    loss_part, d_out = _loss_head(xn, lg, lb, target)
    loss = lax.psum(loss_part[0, 0], ("x", "y", "c"))

    tags = ("in", "out", "up", "down")
    small = {n: [None] * DEPTH for n in _SMALL}
    dmod = [[None] * 6 for _ in range(DEPTH)]
    below = None
    ready = None
    reduced = [None] * 4

    def partial_sums(layer, full, theirs):
        return [_chip_partial(pos, g, t, kind, f"chip_partial_{tag}_l{layer}")
                for g, t, kind, tag in zip(full, theirs, _SHARD_KINDS, tags)]

    def chip_sums(layer, full, theirs, received):
        return [_reduce_shard(pos, g, t, r, out, kind, layer, f"reduce_shard_{tag}_l{layer}")
                for g, t, r, out, kind, tag in zip(full, theirs, received, reduced, _SHARD_KINDS, tags)]

    for l in reversed(range(DEPTH)):
        sv = saved[l]
        scale1, gate1, scale2, gate2 = mod[l, 1], mod[l, 2], mod[l, 4], mod[l, 5]
        lg1, lb1 = row(ln1_g, l), row(ln1_b, l)
        outs, theirs = _b_ln(d_out, sv["xhat2"], sv["rstd2"], sv["f2"], row(ln2_g, l), row(ln2_b, l), gate2,
                             f"b1_ln2_l{l}", modulated=below,
                             exchange=_swap_exchange(ready[1]) if ready else None)
        df2, dres2, dlg2, dlb2, dmod[l][5] = outs[:5]
        if below is not None:
            dmod[l + 1][1], dmod[l + 1][0] = outs[5], outs[6]
        partials = partial_sums(ready[0], ready[1], theirs) if ready else None
        (dup, dcw, dcb), received = _b2_ffn(df2, wf_down, sv["up"], sv["ge"], sv["dge"], conv_full[l], l,
                                            exchange=_scatter_exchange(partials) if ready else None)
        if ready:
            reduced = chip_sums(ready[0], ready[1], theirs, received)
        g_down = _weight_grad(sv["act"], [df2], 512, f"wg_down_l{l}", rows=2 * TS_TN)
        dh2, shared = _matmul_nt(dup, wf_up, l, D_FF, f"b3_up_l{l}",
                                 exchange=_share_exchange(reduced, ready[0]) if ready else None)
        if ready:
            reduced = shared
        g_up = _weight_grad(sv["h2"], [dup], 1408, f"wg_up_l{l}", rows=2 * TS_TN)
        (df1, dres1, dlg1, dlb1, dmod[l][2], dmod[l][4], dmod[l][3]), _ = _b_ln(
            dh2, sv["xhat1"], sv["rstd1"], sv["f1"], lg1, lb1, gate1, f"b4_ln1_l{l}", modulated=(dres2, scale2))
        dproj, dpw, dps, dsg, dsb, dwm, dbias = _b5_mixers(
            df1, wf_out, sv["proj"], sv["gel"], sv["pw"], row(pool_scale, l), row(sgu_ln_g, l), row(sgu_ln_b, l), sv["wm"],
            sv["bias_full"], l)
        g_out = _weight_grad(sv["mix"], [df1], D, f"wg_out_l{l}", rows=2 * TS_TN)
        d_out, _ = _matmul_nt(dproj, wf_in, l, D_IN, f"b6_in_l{l}")
        g_in = _weight_grad(sv["h"], [dproj], 768, f"wg_in_l{l}", rows=2 * TS_TN)
        ready = (l, [g_in, g_out, g_up, g_down])
        below = (dres1, scale1)
        small["pool_w"][l], small["pool_scale"][l] = dpw, dps[0]
        small["sgu_ln_g"][l], small["sgu_ln_b"][l], small["sgu_w"][l] = dsg[0], dsb[0], dwm
        small["sgu_b"][l] = jnp.transpose(jnp.sum(dbias.reshape(CHUNK, HEADS, GROUP), axis=2))
        small["ln1_g"][l], small["ln1_b"][l] = dlg1[0], dlb1[0]
        small["conv_w"][l], small["conv_b"][l] = dcw, dcb[0]
        small["ln2_g"][l], small["ln2_b"][l] = dlg2[0], dlb2[0]
    grad_x2d, dmod[0][1], dmod[0][0] = _input_grad(d_out, below[0], below[1], xs)
    grad_x = grad_x2d[None]
    small["ada_b"] = [jnp.concatenate([part[0] for part in dmod[l]]) for l in range(DEPTH)]

    names = _SMALL
    flat = jnp.concatenate([jnp.stack(small[n]).reshape(-1) for n in names])
    n_small = flat.shape[0]
    padded = -(-n_small // 1024) * 1024
    gathered = _all_gather8(jnp.pad(flat, (0, padded - n_small)).reshape(8, padded // 8), "gather_small_grads")
    summed = _sum_devices(gathered).reshape(-1)[:n_small]
    grads = {}
    offset = 0
    for n in names:
        size = math.prod(weights[n].shape[1:]) * DEPTH if n != "conv_w" else DEPTH * 3 * D_FF
        grads[n] = summed[offset:offset + size]
        offset += size
    grads["conv_w"] = lax.dynamic_slice_in_dim(grads["conv_w"].reshape(DEPTH, 3, D_FF), chip * ff_shard, ff_shard,
                                               axis=2)
    for n in names:
        grads[n] = grads[n].reshape(weights[n].shape)
    dmod_all = gathered.reshape(N_DEV, padded)[:, n_small - DEPTH * 6 * D:n_small].reshape(N_DEV, DEPTH, 6 * D)
    dmod_cols = lax.dynamic_slice_in_dim(jnp.transpose(dmod_all, (1, 0, 2)), chip * mod_shard, mod_shard, axis=2)
    grads["ada_w"] = _ada_backward(c_all, jnp.pad(dmod_cols, ((0, 0), (0, 8), (0, 0))))

    theirs = _run_exchange(_swap_exchange(ready[1]), "swap_halves_l0")
    partials = partial_sums(0, ready[1], theirs)
    received = _run_exchange(_scatter_exchange(partials), "scatter_partials_l0")
    reduced = chip_sums(0, ready[1], theirs, received)
    grads["w_in"], grads["w_out"], grads["w_up"], grads["w_down"] = _run_exchange(
        _share_exchange(reduced, 0), "share_reduced_l0")

    delta, new_m, new_v = {}, {}, {}
    for n in ("ada_w", "w_in", "w_out", "w_up", "w_down"):
        shape = weights[n].shape
        two_d = (shape[0] * shape[1], shape[2])
        d_, m_, v_ = _adamw(weights[n].reshape(two_d), grads[n].reshape(two_d), mom_m[n].reshape(two_d),
                            mom_v[n].reshape(two_d), f"adamw_{n}")
        delta[n], new_m[n], new_v[n] = d_.reshape(shape), m_.reshape(shape), v_.reshape(shape)
    sizes = [math.prod(weights[n].shape) for n in names]
    total = sum(sizes)
    padded = -(-total // 1024) * 1024
    pack = lambda d: jnp.pad(jnp.concatenate([d[n].reshape(-1) for n in names]), (0, padded - total)).reshape(8, -1)
    v_pack = jnp.pad(jnp.concatenate([mom_v[n].reshape(-1) for n in names]), (0, padded - total),
                     constant_values=1.0).reshape(8, -1)
    d_, m_, v_ = _adamw(pack(weights), pack(grads), pack(mom_m), v_pack, "adamw_small")
    offset = 0
    for n, size in zip(names, sizes):
        for src, dst in ((d_, delta), (m_, new_m), (v_, new_v)):
            dst[n] = src.reshape(-1)[offset:offset + size].reshape(weights[n].shape)
        offset += size

    return (loss, grad_x, *[grads[n] for n in _WEIGHTS], *[delta[n] for n in _WEIGHTS],
            *[new_m[n] for n in _WEIGHTS], *[new_v[n] for n in _WEIGHTS])
```

```python
import functools
import math

import jax
import jax.numpy as jnp
from jax import lax
from jax.experimental import pallas as pl
from jax.experimental.pallas import tpu as pltpu

F32 = jnp.float32
MXU_DTYPE = jnp.bfloat16
WIRE_DTYPE = jnp.bfloat16

DEPTH = 4
D = 1024
D_POOL = 512
D_SGU = 512
N_GROUPS = 4
GROUP = 128
POOL_WINDOWS = (2, 4, 8, 16)
POOL_HALO = 16
CHUNK = 128
HEADS = 4
D_IN = D_POOL + 2 * D_SGU
D_FF = 2816
CONV_HALO = 16
STORE_DTYPE = jnp.bfloat16
GELU_DTYPE = jnp.bfloat16
N_CHIPS = 4
N_DEV = 8
ALPHA = (2.0 * DEPTH) ** 0.25
LN_EPS = 1e-5
ADAM_LR, ADAM_B1, ADAM_B2, ADAM_EPS, ADAM_WD, ADAM_STEP = 0.001, 0.9, 0.999, 1e-08, 0.01, 10

TS_PROJ = 512
TS_MM = 1024
TS_MIX = 512
TS_FF = 256
TS_TN = 1024
VMEM_LIMIT = 52 * 2 ** 20

MESH = pl.DeviceIdType.MESH
ANY = pl.BlockSpec(memory_space=pl.ANY)

_GELU_K0 = math.sqrt(2.0 / math.pi)
_GELU_K1 = 0.044715


def _params(n_axes):
    return pltpu.CompilerParams(dimension_semantics=("arbitrary",) * n_axes, vmem_limit_bytes=VMEM_LIMIT)


class _Exchange:
    def __init__(self, inputs, inplace, fresh, n_copies, copies):
        self.inputs, self.inplace, self.fresh, self.n_copies, self.copies = inputs, inplace, fresh, n_copies, copies

    def out_shapes(self):
        return [jax.ShapeDtypeStruct(self.inputs[i].shape, self.inputs[i].dtype) for i in self.inplace] + list(self.fresh)

    def semaphores(self):
        return [pltpu.SemaphoreType.DMA((self.n_copies,)), pltpu.SemaphoreType.DMA((self.n_copies,))]

    def aliases(self, first_input, first_output):
        return {first_input + i: first_output + k for k, i in enumerate(self.inplace)}


def _call(body, *, name, grid, in_specs, out_specs, out_shape, args, scratch_shapes=(), exchange=None):
    n_axes = len(grid)
    in_specs, out_specs, out_shape, scratch_shapes = list(in_specs), list(out_specs), list(out_shape), list(scratch_shapes)
    if exchange is None:
        outs = pl.pallas_call(body, name=name, grid=grid, in_specs=in_specs, out_specs=out_specs, out_shape=out_shape,
                              scratch_shapes=scratch_shapes, compiler_params=_params(n_axes))(*args)
        return list(outs), []
    n_in, n_out, n_scr = len(in_specs), len(out_shape), len(scratch_shapes)
    x_out = exchange.out_shapes()
    n_xin, n_xout = len(exchange.inputs), len(x_out)

    def hosted(*refs):
        ins, xin = refs[:n_in], refs[n_in:n_in + n_xin]
        o = n_in + n_xin
        outs, xout = refs[o:o + n_out], refs[o + n_out:o + n_out + n_xout]
        s = o + n_out + n_xout
        scr, (send_sems, recv_sems) = refs[s:s + n_scr], refs[s + n_scr:]
        first = functools.reduce(jnp.logical_and, [pl.program_id(d) == 0 for d in range(n_axes)])
        last = functools.reduce(jnp.logical_and, [pl.program_id(d) == pl.num_programs(d) - 1 for d in range(n_axes)])

        @pl.when(first)
        def _():
            for cp in exchange.copies(xin, xout, send_sems, recv_sems)[0]:
                cp.start()

        body(*ins, *outs, *scr)

        @pl.when(last)
        def _():
            sends, arrivals = exchange.copies(xin, xout, send_sems, recv_sems)
            for cp in arrivals:
                cp.wait_recv()
            for cp in sends:
                cp.wait_send()

    results = pl.pallas_call(
        hosted, name=name, grid=grid, in_specs=in_specs + [ANY] * n_xin, out_specs=out_specs + [ANY] * n_xout,
        out_shape=out_shape + x_out, scratch_shapes=scratch_shapes + exchange.semaphores(),
        input_output_aliases=exchange.aliases(n_in, n_out), compiler_params=_params(n_axes),
    )(*args, *exchange.inputs)
    return list(results[:n_out]), list(results[n_out:])


def _run_exchange(exchange, name):
    x_out = exchange.out_shapes()
    n_xin, n_xout = len(exchange.inputs), len(x_out)

    def body(*refs):
        xin, xout = refs[:n_xin], refs[n_xin:n_xin + n_xout]
        send_sems, recv_sems = refs[n_xin + n_xout:]
        sends, arrivals = exchange.copies(xin, xout, send_sems, recv_sems)
        for cp in sends:
            cp.start()
        for cp in arrivals:
            cp.wait_recv()
        for cp in sends:
            cp.wait_send()

    return list(pl.pallas_call(
        body, name=name, in_specs=[ANY] * n_xin, out_specs=[ANY] * n_xout, out_shape=x_out,
        scratch_shapes=exchange.semaphores(), input_output_aliases=exchange.aliases(0, 0),
    )(*exchange.inputs))


def _dot(a, b):
    return jnp.dot(a, b, preferred_element_type=F32)


def _dot_nt(a, b):
    return lax.dot_general(a, b, (((1,), (1,)), ((), ())), preferred_element_type=F32)


def _dot_tn(a, b):
    return lax.dot_general(a, b, (((0,), (0,)), ((), ())), preferred_element_type=F32)


def _gelu(x):
    x2 = x * x
    t = jnp.tanh(x * (x2 * (_GELU_K0 * _GELU_K1) + _GELU_K0))
    cdf = 0.5 * t + 0.5
    dg = cdf + (x * (1.0 - t * t)) * (x2 * (1.5 * _GELU_K0 * _GELU_K1) + 0.5 * _GELU_K0)
    return x * cdf, dg


def _colsum(x):
    return jnp.sum(x, axis=0, keepdims=True)


def _row(d):
    return pl.BlockSpec((1, d), lambda *_: (0, 0))


def _full(shape):
    n = len(shape)
    return pl.BlockSpec(shape, lambda *_: (0,) * n)


def _layer_block(shape, layer):
    n = len(shape)
    return pl.BlockSpec((None,) + tuple(shape), lambda *_: (layer,) + (0,) * n)


def _pool_counts(first_row, rows, window):
    t = first_row + lax.broadcasted_iota(jnp.int32, (rows, 1), 0)
    return 1.0 / jnp.minimum(t + 1, window).astype(F32)


def _pool_forward(a_ext, a, first_row, rows):
    out = []
    for g, window in enumerate(POOL_WINDOWS):
        s = a_ext[:, g * GROUP:(g + 1) * GROUP]
        k = 1
        while k < window:
            s = s + pltpu.roll(s, k, 0)
            k *= 2
        inv = _pool_counts(first_row, rows, window)
        out.append(s[POOL_HALO:] * inv - a[:, g * GROUP:(g + 1) * GROUP])
    return out


def _sgu_norm(gv):
    mu = jnp.mean(gv, axis=-1, keepdims=True)
    xc = gv - mu
    var = jnp.mean(xc * xc, axis=-1, keepdims=True)
    rs = lax.rsqrt(var + LN_EPS)
    return xc * rs, rs


def _modulate_input(x, shift, scale):
    S = x.shape[0]
    ts = min(TS_PROJ, S)

    def body(x_ref, sh_ref, sc_ref, h_ref):
        h_ref[...] = (x_ref[...] * (1.0 + sc_ref[...]) + sh_ref[...]).astype(MXU_DTYPE)

    tile = pl.BlockSpec((ts, D), lambda i: (i, 0))
    return pl.pallas_call(
        body, name="modulate_input", grid=(S // ts,), in_specs=[tile, _row(D), _row(D)], out_specs=tile,
        out_shape=jax.ShapeDtypeStruct((S, D), MXU_DTYPE), compiler_params=_params(1),
    )(x, shift, scale)


def _matmul(a, w_all, layer, tn, out_dtype, name, exchange=None):
    S, K = a.shape
    N = w_all.shape[2]
    ts = min(TS_MM, S)

    def body(a_ref, w_ref, o_ref):
        o_ref[...] = _dot(a_ref[...], w_ref[...]).astype(out_dtype)

    (out,), extra = _call(
        body, name=name, grid=(N // tn, S // ts),
        in_specs=[pl.BlockSpec((ts, K), lambda j, i: (i, 0)), pl.BlockSpec((None, K, tn), lambda j, i: (layer, 0, j))],
        out_specs=[pl.BlockSpec((ts, tn), lambda j, i: (i, j))],
        out_shape=[jax.ShapeDtypeStruct((S, N), out_dtype)], args=(a, w_all), exchange=exchange)
    return out, extra


def _matmul_nt(a, w_all, layer, tk, name, exchange=None):
    S, K = a.shape
    N = w_all.shape[1]
    ts = min(TS_MM, S)
    n_k = K // tk

    def body(a_ref, w_ref, o_ref, acc):
        k = pl.program_id(1)
        part = _dot_nt(a_ref[...], w_ref[...])
        if n_k == 1:
            o_ref[...] = part
        else:
            @pl.when(k == 0)
            def _():
                acc[...] = part

            @pl.when(jnp.logical_and(k > 0, k < n_k - 1))
            def _():
                acc[...] += part

            @pl.when(k == n_k - 1)
            def _():
                o_ref[...] = acc[...] + part

    (out,), extra = _call(
        body, name=name, grid=(S // ts, n_k),
        in_specs=[pl.BlockSpec((ts, tk), lambda i, k: (i, k)), pl.BlockSpec((None, N, tk), lambda i, k: (layer, 0, k))],
        out_specs=[pl.BlockSpec((ts, N), lambda i, k: (i, 0))],
        out_shape=[jax.ShapeDtypeStruct((S, N), F32)],
        scratch_shapes=[pltpu.VMEM((ts, N), F32)], args=(a, w_all), exchange=exchange)
    return out, extra


def _f2_mixers(proj, xn, lg, lb, gate, pool_w, pool_scale, sg, sb, wm, bias_full, w_out_all, lg1, lb1, shift2, scale2,
               layer, exchange=None):
    S = xn.shape[0]
    ts = min(TS_MIX, S)
    n_chunks = ts // CHUNK
    halo_blocks = ts // POOL_HALO

    def body(p_ref, halo_ref, xn_ref, lg_ref, lb_ref, gate_ref, pw_ref, ps_ref, sg_ref, sb_ref, wm_ref,
             bias_ref, wo_ref, lg1_ref, lb1_ref, sh2_ref, sc2_ref, mix_ref, f_ref, xh_ref, rs_ref, h2_ref, gel_ref,
             z_scr):
        i = pl.program_id(0)
        a = p_ref[:, 0:D_POOL].astype(F32)
        u = p_ref[:, D_POOL:D_POOL + D_SGU].astype(GELU_DTYPE)
        v = p_ref[:, D_POOL + D_SGU:D_IN].astype(GELU_DTYPE)
        halo = halo_ref[...].astype(F32) * (i > 0).astype(F32)
        a_ext = jnp.concatenate([halo, a], axis=0)
        pooled = _pool_forward(a_ext, a, i * ts, ts)
        for g in range(N_GROUPS):
            mixed = _dot(pooled[g].astype(MXU_DTYPE), pw_ref[g])
            mix_ref[:, g * GROUP:(g + 1) * GROUP] = (mixed * ps_ref[:, g * GROUP:(g + 1) * GROUP]).astype(MXU_DTYPE)
        gu, dgu = _gelu(u)
        gv, dgv = _gelu(v)
        for k, part in enumerate((gu, dgu, gv, dgv)):
            gel_ref[:, k * D_SGU:(k + 1) * D_SGU] = part.astype(STORE_DTYPE)
        gu = gu.astype(F32)
        vhat, _ = _sgu_norm(gv.astype(F32))
        vn = (vhat * sg_ref[...] + sb_ref[...]).astype(MXU_DTYPE)
        for h in range(HEADS):
            cols = slice(h * GROUP, (h + 1) * GROUP)
            vn_h = jnp.concatenate([vn[c * CHUNK:(c + 1) * CHUNK, cols] for c in range(n_chunks)], axis=1)
            z_h = _dot(wm_ref[h], vn_h)
            for c in range(n_chunks):
                z_scr[c * CHUNK:(c + 1) * CHUNK, cols] = z_h[:, c * GROUP:(c + 1) * GROUP] + bias_ref[:, cols]
        mix_ref[:, D_POOL:D] = (gu * z_scr[...]).astype(MXU_DTYPE)
        f = _dot(mix_ref[...], wo_ref[...])
        f_ref[...] = f.astype(STORE_DTYPE)
        x = xn_ref[...] * lg_ref[...] + lb_ref[...]
        z1 = ALPHA * x + gate_ref[...] * f
        mu = jnp.mean(z1, axis=-1, keepdims=True)
        zc = z1 - mu
        var = jnp.mean(zc * zc, axis=-1, keepdims=True)
        rs = lax.rsqrt(var + LN_EPS)
        xhat = zc * rs
        xh_ref[...] = xhat
        rs_ref[...] = rs
        x1 = xhat * lg1_ref[...] + lb1_ref[...]
        h2_ref[...] = (x1 * (1.0 + sc2_ref[...]) + sh2_ref[...]).astype(MXU_DTYPE)

    tile = lambda w: pl.BlockSpec((ts, w), lambda i: (i, 0))
    return _call(
        body, name=f"f2_mixers_l{layer}", grid=(S // ts,),
        in_specs=[tile(D_IN),
                  pl.BlockSpec((POOL_HALO, D_POOL), lambda i: (jnp.maximum(i * halo_blocks - 1, 0), 0)),
                  tile(D), _row(D), _row(D), _row(D),
                  _full((N_GROUPS, GROUP, GROUP)), _row(D_POOL), _row(D_SGU), _row(D_SGU),
                  _full((HEADS, CHUNK, CHUNK)), _full((CHUNK, D_SGU)), _layer_block((D, D), layer),
                  _row(D), _row(D), _row(D), _row(D)],
        out_specs=[tile(D), tile(D), tile(D), tile(1), tile(D), tile(4 * D_SGU)],
        out_shape=[jax.ShapeDtypeStruct((S, D), MXU_DTYPE), jax.ShapeDtypeStruct((S, D), STORE_DTYPE),
                   jax.ShapeDtypeStruct((S, D), F32), jax.ShapeDtypeStruct((S, 1), F32),
                   jax.ShapeDtypeStruct((S, D), MXU_DTYPE), jax.ShapeDtypeStruct((S, 4 * D_SGU), STORE_DTYPE)],
        scratch_shapes=[pltpu.VMEM((ts, D_SGU), F32)],
        args=(proj, proj, xn, lg, lb, gate, pool_w, pool_scale, sg, sb, wm, bias_full, w_out_all, lg1, lb1, shift2,
              scale2), exchange=exchange)


def _conv_forward(g_ext, cw_ref, cb_ref):
    gm2 = pltpu.roll(g_ext, 2, 0)[CONV_HALO:]
    gm1 = pltpu.roll(g_ext, 1, 0)[CONV_HALO:]
    g0 = g_ext[CONV_HALO:]
    gc = ((cb_ref[...] + gm2 * cw_ref[0:1, :]) + gm1 * cw_ref[1:2, :]) + g0 * cw_ref[2:3, :]
    return gc, gm2, gm1


def _f4_ffn(up, xhat1, lg, lb, gate, conv_w, conv_b, w_down_all, lg2, lb2, shift_next, scale_next, layer,
            exchange=None):
    S = xhat1.shape[0]
    ts = min(TS_FF, S)
    halo_blocks = ts // CONV_HALO

    def body(g_ref, halo_ref, val_ref, xh_ref, lg_ref, lb_ref, gate_ref, cw_ref, cb_ref, wd_ref, lg2_ref, lb2_ref,
             shn_ref, scn_ref, act_ref, f_ref, xo_ref, rs_ref, hn_ref, ge_ref, dge_ref):
        i = pl.program_id(0)
        halo = halo_ref[...].astype(F32) * (i > 0).astype(F32)
        g_ext = jnp.concatenate([halo, g_ref[...].astype(F32)], axis=0)
        gc, _, _ = _conv_forward(g_ext, cw_ref, cb_ref)
        ge, dge = _gelu(gc.astype(GELU_DTYPE))
        ge_ref[...] = ge.astype(STORE_DTYPE)
        dge_ref[...] = dge.astype(STORE_DTYPE)
        act = (ge * val_ref[...].astype(GELU_DTYPE)).astype(MXU_DTYPE)
        act_ref[...] = act
        f = _dot(act, wd_ref[...])
        f_ref[...] = f.astype(STORE_DTYPE)
        x = xh_ref[...] * lg_ref[...] + lb_ref[...]
        z = ALPHA * x + gate_ref[...] * f
        mu = jnp.mean(z, axis=-1, keepdims=True)
        zc = z - mu
        var = jnp.mean(zc * zc, axis=-1, keepdims=True)
        rs = lax.rsqrt(var + LN_EPS)
        xhat = zc * rs
        xo_ref[...] = xhat
        rs_ref[...] = rs
        x2 = xhat * lg2_ref[...] + lb2_ref[...]
        hn_ref[...] = (x2 * (1.0 + scn_ref[...]) + shn_ref[...]).astype(MXU_DTYPE)

    tile = lambda w: pl.BlockSpec((ts, w), lambda i: (i, 0))
    return _call(
        body, name=f"f4_ffn_l{layer}", grid=(S // ts,),
        in_specs=[pl.BlockSpec((ts, D_FF), lambda i: (i, 0)),
                  pl.BlockSpec((CONV_HALO, D_FF), lambda i: (jnp.maximum(i * halo_blocks - 1, 0), 0)),
                  pl.BlockSpec((ts, D_FF), lambda i: (i, 1)),
                  tile(D), _row(D), _row(D), _row(D), _full((3, D_FF)), _row(D_FF),
                  _layer_block((D_FF, D), layer), _row(D), _row(D), _row(D), _row(D)],
        out_specs=[tile(D_FF), tile(D), tile(D), tile(1), tile(D), tile(D_FF), tile(D_FF)],
        out_shape=[jax.ShapeDtypeStruct((S, D_FF), MXU_DTYPE), jax.ShapeDtypeStruct((S, D), STORE_DTYPE),
                   jax.ShapeDtypeStruct((S, D), F32), jax.ShapeDtypeStruct((S, 1), F32),
                   jax.ShapeDtypeStruct((S, D), MXU_DTYPE), jax.ShapeDtypeStruct((S, D_FF), STORE_DTYPE),
                   jax.ShapeDtypeStruct((S, D_FF), STORE_DTYPE)],
        args=(up, up, up, xhat1, lg, lb, gate, conv_w, conv_b, w_down_all, lg2, lb2, shift_next, scale_next),
        exchange=exchange)


def _loss_head(xhat, lg, lb, target):
    S = xhat.shape[0]
    ts = min(TS_PROJ, S)

    def body(xh_ref, lg_ref, lb_ref, t_ref, loss_ref, dy_ref, acc):
        i = pl.program_id(0)

        @pl.when(i == 0)
        def _():
            acc[...] = jnp.zeros_like(acc)

        err = (xh_ref[...] * lg_ref[...] + lb_ref[...]) - t_ref[...]
        dy_ref[...] = err * (1.0 / D)
        acc[...] += _colsum(err * err)

        @pl.when(i == pl.num_programs(0) - 1)
        def _():
            loss_ref[...] = jnp.sum(acc[...], axis=1, keepdims=True) * (0.5 / D)

    tile = pl.BlockSpec((ts, D), lambda i: (i, 0))
    return pl.pallas_call(
        body, name="loss_head", grid=(S // ts,),
        in_specs=[tile, _row(D), _row(D), tile],
        out_specs=[_full((1, 1)), tile],
        out_shape=[jax.ShapeDtypeStruct((1, 1), F32), jax.ShapeDtypeStruct((S, D), F32)],
        scratch_shapes=[pltpu.VMEM((1, D), F32)],
        compiler_params=_params(1),
    )(xhat, lg, lb, target)


def _accumulate(i, ref, value):
    @pl.when(i == 0)
    def _():
        ref[...] = value

    @pl.when(i > 0)
    def _():
        ref[...] += value


def _b_ln(d, xhat, rstd, f, lg, lb, gate, name, modulated=None, exchange=None):
    S = d.shape[0]
    ts = min(TS_PROJ, S)
    has_mod = modulated is not None

    def body(*refs):
        if has_mod:
            (d_ref, xh_ref, rs_ref, f_ref, lg_ref, lb_ref, gate_ref, dri_ref, sc_ref,
             df_ref, dres_ref, dlg_ref, dlb_ref, dgate_ref, dsc_ref, dsh_ref) = refs
        else:
            (d_ref, xh_ref, rs_ref, f_ref, lg_ref, lb_ref, gate_ref,
             df_ref, dres_ref, dlg_ref, dlb_ref, dgate_ref) = refs
        i = pl.program_id(0)
        xh = xh_ref[...]
        if has_mod:
            dh = d_ref[...]
            dxo_t = dh * (1.0 + sc_ref[...]) + dri_ref[...]
            _accumulate(i, dsc_ref, _colsum(dh * (xh * lg_ref[...] + lb_ref[...])))
            _accumulate(i, dsh_ref, _colsum(dh))
        else:
            dxo_t = d_ref[...]
        dxh = dxo_t * lg_ref[...]
        m1 = jnp.mean(dxh, axis=-1, keepdims=True)
        m2 = jnp.mean(dxh * xh, axis=-1, keepdims=True)
        dz = rs_ref[...] * (dxh - m1 - xh * m2)
        df_ref[...] = (dz * gate_ref[...]).astype(MXU_DTYPE)
        dres_ref[...] = ALPHA * dz
        _accumulate(i, dlg_ref, _colsum(dxo_t * xh))
        _accumulate(i, dlb_ref, _colsum(dxo_t))
        _accumulate(i, dgate_ref, _colsum(dz * f_ref[...].astype(F32)))

    tile = lambda w: pl.BlockSpec((ts, w), lambda i: (i, 0))
    n_sums = 5 if has_mod else 3
    in_specs = [tile(D), tile(D), tile(1), tile(D), _row(D), _row(D), _row(D)]
    args = [d, xhat, rstd, f, lg, lb, gate]
    if has_mod:
        in_specs += [tile(D), _row(D)]
        args += list(modulated)
    return _call(
        body, name=name, grid=(S // ts,), in_specs=in_specs,
        out_specs=[tile(D), tile(D)] + [_row(D)] * n_sums,
        out_shape=[jax.ShapeDtypeStruct((S, D), MXU_DTYPE), jax.ShapeDtypeStruct((S, D), F32)]
        + [jax.ShapeDtypeStruct((1, D), F32)] * n_sums,
        args=args, exchange=exchange)


def _input_grad(dh, dres, scale, x, exchange=None):
    S = dh.shape[0]
    ts = min(TS_PROJ, S)

    def body(dh_ref, dres_ref, sc_ref, x_ref, dx_ref, dsc_ref, dsh_ref):
        i = pl.program_id(0)
        dh_t = dh_ref[...]
        dx_ref[...] = dh_t * (1.0 + sc_ref[...]) + dres_ref[...]
        _accumulate(i, dsc_ref, _colsum(dh_t * x_ref[...]))
        _accumulate(i, dsh_ref, _colsum(dh_t))

    tile = pl.BlockSpec((ts, D), lambda i: (i, 0))
    return _call(
        body, name="input_grad", grid=(S // ts,), in_specs=[tile, tile, _row(D), tile],
        out_specs=[tile, _row(D), _row(D)],
        out_shape=[jax.ShapeDtypeStruct((S, D), F32)] + [jax.ShapeDtypeStruct((1, D), F32)] * 2,
        args=(dh, dres, scale, x), exchange=exchange)


def _b2_ffn(df2, w_down_all, up, ge, dge, conv_w, layer, exchange=None):
    S = df2.shape[0]
    ts = min(TS_FF, S)
    n_tiles = S // ts
    n_ext = ts + CONV_HALO

    def body(df_ref, wd_ref, g_ref, val_ref, ge_ref, dge_ref, cw_ref, dup_ref, dcw_ref, dcb_ref, next_dgc):
        i = pl.program_id(0)

        @pl.when(i == 0)
        def _():
            next_dgc[...] = jnp.zeros_like(next_dgc)

        dact = _dot_nt(df_ref[...], wd_ref[...])
        dup_ref[:, D_FF:2 * D_FF] = (dact * ge_ref[...].astype(F32)).astype(MXU_DTYPE)
        dgc = dact * val_ref[...].astype(F32) * dge_ref[...].astype(F32)
        ext = jnp.concatenate([dgc, next_dgc[...]], axis=0)
        dp1 = pltpu.roll(ext, n_ext - 1, 0)[:ts]
        dp2 = pltpu.roll(ext, n_ext - 2, 0)[:ts]
        dup_ref[:, 0:D_FF] = (dgc * cw_ref[2:3, :] + dp1 * cw_ref[1:2, :] + dp2 * cw_ref[0:1, :]).astype(MXU_DTYPE)
        next_dgc[...] = dgc[0:CONV_HALO]
        g = g_ref[...].astype(F32)
        dcw = jnp.concatenate([_colsum(dp2 * g), _colsum(dp1 * g), _colsum(dgc * g)], axis=0)
        _accumulate(i, dcw_ref, dcw)
        _accumulate(i, dcb_ref, _colsum(dgc))

    tile = lambda w, col=0: pl.BlockSpec((ts, w), lambda i: (n_tiles - 1 - i, col))
    return _call(
        body, name=f"b2_ffn_l{layer}", grid=(n_tiles,),
        in_specs=[tile(D), _layer_block((D_FF, D), layer), tile(D_FF), tile(D_FF, 1), tile(D_FF), tile(D_FF),
                  _full((3, D_FF))],
        out_specs=[tile(2 * D_FF), _full((3, D_FF)), _row(D_FF)],
        out_shape=[jax.ShapeDtypeStruct((S, 2 * D_FF), MXU_DTYPE),
                   jax.ShapeDtypeStruct((3, D_FF), F32), jax.ShapeDtypeStruct((1, D_FF), F32)],
        scratch_shapes=[pltpu.VMEM((CONV_HALO, D_FF), F32)],
        args=(df2, w_down_all, up, up, ge, dge, conv_w), exchange=exchange)


def _b5_mixers(df1, w_out_all, proj, gel, pool_w, pool_scale, sg, sb, wm, bias_full, layer):
    S = df1.shape[0]
    ts = min(TS_MIX, S)
    n_chunks = ts // CHUNK
    halo_blocks = ts // POOL_HALO
    last_halo = S // POOL_HALO - 1

    def body(df_ref, dfh_ref, wo_ref, p_ref, ah_ref, gel_ref, pw_ref, ps_ref, sg_ref, sb_ref, wm_ref, bias_ref,
             dp_ref, dpw_ref, dps_ref, dsg_ref, dsb_ref, dwm_ref, dbias_ref, z_scr, dvn_scr):
        i = pl.program_id(0)
        last = pl.num_programs(0) - 1
        dmix = _dot_nt(df_ref[...], wo_ref[...])
        dmix_halo = _dot_nt(dfh_ref[...], wo_ref[0:D_POOL, :]) * (i < last).astype(F32)

        a = p_ref[:, 0:D_POOL].astype(F32)
        halo = ah_ref[...].astype(F32) * (i > 0).astype(F32)
        pooled = _pool_forward(jnp.concatenate([halo, a], axis=0), a, i * ts, ts)
        n = ts + POOL_HALO
        dps_parts = []
        for g, window in enumerate(POOL_WINDOWS):
            cols = slice(g * GROUP, (g + 1) * GROUP)
            pooled_b = pooled[g].astype(MXU_DTYPE)
            mixed = _dot(pooled_b, pw_ref[g])
            dya = dmix[:, cols]
            dps_parts.append(_colsum(dya * mixed))
            dmixed = (dya * ps_ref[:, cols]).astype(MXU_DTYPE)
            dpw_g = _dot_tn(pooled_b, dmixed)

            @pl.when(i == 0)
            def _():
                dpw_ref[g] = dpw_g

            @pl.when(i > 0)
            def _():
                dpw_ref[g] += dpw_g

            dpooled = _dot_nt(dmixed, pw_ref[g])
            dmixed_h = (dmix_halo[:, cols] * ps_ref[:, cols]).astype(MXU_DTYPE)
            dpooled_h = _dot_nt(dmixed_h, pw_ref[g])
            q = dpooled * _pool_counts(i * ts, ts, window)
            s = jnp.concatenate([q, dpooled_h * (1.0 / window)], axis=0)
            k = 1
            while k < window:
                s = s + pltpu.roll(s, n - k, 0)
                k *= 2
            dp_ref[:, cols] = (s[:ts] - dpooled).astype(MXU_DTYPE)
        _accumulate(i, dps_ref, jnp.concatenate(dps_parts, axis=1))

        gu, dgu, gv, dgv = (gel_ref[:, k * D_SGU:(k + 1) * D_SGU].astype(F32) for k in range(4))
        vhat, rs = _sgu_norm(gv)
        vn = (vhat * sg_ref[...] + sb_ref[...]).astype(MXU_DTYPE)
        dyb = dmix[:, D_POOL:D]
        dz = dyb * gu
        dzb = dz.astype(MXU_DTYPE)
        dbias = dz[0:CHUNK]
        for c in range(1, n_chunks):
            dbias = dbias + dz[c * CHUNK:(c + 1) * CHUNK]
        _accumulate(i, dbias_ref, dbias)
        for h in range(HEADS):
            cols = slice(h * GROUP, (h + 1) * GROUP)
            vn_h = jnp.concatenate([vn[c * CHUNK:(c + 1) * CHUNK, cols] for c in range(n_chunks)], axis=1)
            dz_h = jnp.concatenate([dzb[c * CHUNK:(c + 1) * CHUNK, cols] for c in range(n_chunks)], axis=1)
            z_h = _dot(wm_ref[h], vn_h)
            dvn_h = _dot_tn(wm_ref[h], dz_h)
            dwm_h = _dot_nt(dz_h, vn_h)
            for c in range(n_chunks):
                rows = slice(c * CHUNK, (c + 1) * CHUNK)
                z_scr[rows, cols] = z_h[:, c * GROUP:(c + 1) * GROUP] + bias_ref[:, cols]
                dvn_scr[rows, cols] = dvn_h[:, c * GROUP:(c + 1) * GROUP]

            @pl.when(i == 0)
            def _():
                dwm_ref[h] = dwm_h

            @pl.when(i > 0)
            def _():
                dwm_ref[h] += dwm_h

        dp_ref[:, D_POOL:D_POOL + D_SGU] = (dyb * z_scr[...] * dgu).astype(MXU_DTYPE)
        dvn = dvn_scr[...]
        _accumulate(i, dsg_ref, _colsum(dvn * vhat))
        _accumulate(i, dsb_ref, _colsum(dvn))
        dvh = dvn * sg_ref[...]
        m1 = jnp.mean(dvh, axis=-1, keepdims=True)
        m2 = jnp.mean(dvh * vhat, axis=-1, keepdims=True)
        dp_ref[:, D_POOL + D_SGU:D_IN] = (rs * (dvh - m1 - vhat * m2) * dgv).astype(MXU_DTYPE)

        @pl.when(i == last)
        def _():
            tri = (lax.broadcasted_iota(jnp.int32, (CHUNK, CHUNK), 0)
                   >= lax.broadcasted_iota(jnp.int32, (CHUNK, CHUNK), 1))
            for h in range(HEADS):
                dwm_ref[h] = jnp.where(tri, dwm_ref[h], 0.0)

    tile = lambda w: pl.BlockSpec((ts, w), lambda i: (i, 0))
    return pl.pallas_call(
        body, name=f"b5_mixers_l{layer}", grid=(S // ts,),
        in_specs=[tile(D),
                  pl.BlockSpec((POOL_HALO, D), lambda i: (jnp.minimum((i + 1) * halo_blocks, last_halo), 0)),
                  _layer_block((D, D), layer), tile(D_POOL),
                  pl.BlockSpec((POOL_HALO, D_POOL), lambda i: (jnp.maximum(i * halo_blocks - 1, 0), 0)),
                  tile(4 * D_SGU),
                  _full((N_GROUPS, GROUP, GROUP)), _row(D_POOL), _row(D_SGU), _row(D_SGU),
                  _full((HEADS, CHUNK, CHUNK)), _full((CHUNK, D_SGU))],
        out_specs=[tile(D_IN), _full((N_GROUPS, GROUP, GROUP)), _row(D_POOL), _row(D_SGU), _row(D_SGU),
                   _full((HEADS, CHUNK, CHUNK)), _full((CHUNK, D_SGU))],
        out_shape=[jax.ShapeDtypeStruct((S, D_IN), MXU_DTYPE), jax.ShapeDtypeStruct((N_GROUPS, GROUP, GROUP), F32),
                   jax.ShapeDtypeStruct((1, D_POOL), F32), jax.ShapeDtypeStruct((1, D_SGU), F32),
                   jax.ShapeDtypeStruct((1, D_SGU), F32), jax.ShapeDtypeStruct((HEADS, CHUNK, CHUNK), F32),
                   jax.ShapeDtypeStruct((CHUNK, D_SGU), F32)],
        scratch_shapes=[pltpu.VMEM((ts, D_SGU), F32), pltpu.VMEM((ts, D_SGU), F32)],
        compiler_params=_params(1),
    )(df1, df1, w_out_all, proj, proj, gel, pool_w, pool_scale, sg, sb, wm, bias_full)


def _weight_grad(a, bs, tn, name, rows=TS_TN):
    S, M = a.shape
    counts = [b.shape[1] // tn for b in bs]
    starts = [sum(counts[:k]) for k in range(len(bs))]
    ts = min(rows, S)

    def body(a_ref, *refs):
        b_refs, o_ref = refs[:-1], refs[-1]
        j = pl.program_id(0)
        for b_ref, lo, n in zip(b_refs, starts, counts):
            @pl.when(jnp.logical_and(j >= lo, j < lo + n))
            def _():
                _accumulate(pl.program_id(1), o_ref, _dot_tn(a_ref[...], b_ref[...]))

    b_spec = lambda lo, n: pl.BlockSpec((ts, tn), lambda j, i: (i, jnp.clip(j - lo, 0, n - 1)))
    return pl.pallas_call(
        body, name=name, grid=(sum(counts), S // ts),
        in_specs=[pl.BlockSpec((ts, M), lambda j, i: (i, 0))] + [b_spec(lo, n) for lo, n in zip(starts, counts)],
        out_specs=pl.BlockSpec((M, tn), lambda j, i: (0, j)),
        out_shape=jax.ShapeDtypeStruct((M, sum(counts) * tn), F32), compiler_params=_params(2),
    )(a, *bs)


def _silu(x):
    return x * (1.0 / (1.0 + jnp.exp(-x)))


def _ada_forward(c_all, ada_w, ada_b_cols):
    n_cols = ada_w.shape[2]
    tc = 512

    def body(c_ref, w_ref, b_ref, o_ref):
        ca = _silu(c_ref[...]).astype(MXU_DTYPE)
        o_ref[...] = _dot(ca, w_ref[...].astype(MXU_DTYPE)) + b_ref[...]

    return pl.pallas_call(
        body, name="ada_forward", grid=(DEPTH, n_cols // tc),
        in_specs=[pl.BlockSpec((16, D), lambda l, j: (0, 0)), pl.BlockSpec((None, D, tc), lambda l, j: (l, 0, j)),
                  pl.BlockSpec((None, 1, tc), lambda l, j: (l, 0, j))],
        out_specs=pl.BlockSpec((None, 16, tc), lambda l, j: (l, 0, j)),
        out_shape=jax.ShapeDtypeStruct((DEPTH, 16, n_cols), F32),
        compiler_params=_params(2),
    )(c_all, ada_w, ada_b_cols)


def _ada_backward(c_all, dmod_cols):
    n_cols = dmod_cols.shape[2]
    tc = 512

    def body(c_ref, d_ref, o_ref):
        ca = _silu(c_ref[...]).astype(MXU_DTYPE)
        o_ref[...] = _dot_tn(ca, d_ref[...].astype(MXU_DTYPE))

    return pl.pallas_call(
        body, name="ada_backward", grid=(DEPTH, n_cols // tc),
        in_specs=[pl.BlockSpec((16, D), lambda l, j: (0, 0)), pl.BlockSpec((None, 16, tc), lambda l, j: (l, 0, j))],
        out_specs=pl.BlockSpec((None, D, tc), lambda l, j: (l, 0, j)),
        out_shape=jax.ShapeDtypeStruct((DEPTH, D, n_cols), F32),
        compiler_params=_params(2),
    )(c_all, dmod_cols)


def _adamw(w, g, m, v, name):
    R, C = w.shape
    tr = R
    for cand in (512, 256, 128, 64, 32, 16, 8):
        if R % cand == 0 and cand * C * 4 <= 2 ** 21:
            tr = cand
            break
    c1 = 1.0 - ADAM_B1 ** ADAM_STEP
    c2 = 1.0 - ADAM_B2 ** ADAM_STEP

    def body(w_ref, g_ref, m_ref, v_ref, d_ref, mo_ref, vo_ref):
        gg = g_ref[...]
        mn = ADAM_B1 * m_ref[...] + (1.0 - ADAM_B1) * gg
        vn = ADAM_B2 * v_ref[...] + (1.0 - ADAM_B2) * (gg * gg)
        mo_ref[...] = mn
        vo_ref[...] = vn
        d_ref[...] = -ADAM_LR * ((mn / c1) / (jnp.sqrt(vn / c2) + ADAM_EPS) + ADAM_WD * w_ref[...])

    tile = pl.BlockSpec((tr, C), lambda i: (i, 0))
    return pl.pallas_call(
        body, name=name, grid=(R // tr,), in_specs=[tile] * 4, out_specs=[tile] * 3,
        out_shape=[jax.ShapeDtypeStruct((R, C), F32)] * 3, compiler_params=_params(1),
    )(w, g, m, v)


def _position():
    x, y, c = lax.axis_index("x"), lax.axis_index("y"), lax.axis_index("c")
    other_chips = [(1 - x, y), (x, 1 - y), (1 - x, 1 - y)]
    return x, y, c, other_chips


def _all_gather8(block, name):
    R, C = block.shape

    def body(x_ref, out_ref, send_sems, recv_sems, local_sem):
        x, y, c, chips = _position()
        me, sibling = (x, y, c), (x, y, 1 - c)

        def rows(px, py, pc):
            return out_ref.at[pl.ds((4 * px + 2 * py + pc) * R, R), :]

        def copy(k, blk, to, src=None):
            return pltpu.make_async_remote_copy(
                src_ref=rows(*blk) if src is None else src, dst_ref=rows(*blk),
                send_sem=send_sems.at[k], recv_sem=recv_sems.at[k], device_id=to, device_id_type=MESH)

        mine = pltpu.make_async_copy(x_ref, rows(*me), local_sem)
        mine.start()
        first = [copy(0, me, sibling, src=x_ref)]
        first += [copy(1 + j, me, (*chip, c), src=x_ref) for j, chip in enumerate(chips)]
        for cp in first:
            cp.start()
        passed = [copy(4 + j, (*chip, c), sibling) for j, chip in enumerate(chips)]
        for j, chip in enumerate(chips):
            copy(1 + j, (*chip, c), me).wait_recv()
            passed[j].start()
        copy(0, sibling, me).wait_recv()
        for j, chip in enumerate(chips):
            copy(4 + j, (*chip, 1 - c), me).wait_recv()
        for cp in first + passed:
            cp.wait_send()
        mine.wait()

    return pl.pallas_call(
        body, name=name, out_shape=jax.ShapeDtypeStruct((N_DEV * R, C), block.dtype),
        in_specs=[pl.BlockSpec(memory_space=pltpu.VMEM)], out_specs=ANY,
        scratch_shapes=[pltpu.SemaphoreType.DMA((7,)), pltpu.SemaphoreType.DMA((7,)), pltpu.SemaphoreType.DMA(())],
    )(block)


def _gather_flat(vec, name):
    n = vec.shape[0]
    padded = -(-n // 1024) * 1024
    block = jnp.pad(vec, (0, padded - n)).reshape(8, padded // 8)
    out = _all_gather8(block, name)
    return out.reshape(N_DEV, padded)[:, :n]


_SHARD_KINDS = ("cols", "rows", "cols", "rows")


def _shard_of(ref, shape, kind, chip):
    m, n = shape
    return ref.at[:, pl.ds(chip * (n // 4), n // 4)] if kind == "cols" else ref.at[pl.ds(chip * (m // 4), m // 4), :]


def _half_of(ref, shape, kind, h):
    m, n = shape
    return ref.at[pl.ds(h * (m // 2), m // 2), :] if kind == "cols" else ref.at[:, pl.ds(h * (n // 2), n // 2)]


def _half_shape(shape, kind):
    m, n = shape
    return (m // 2, n) if kind == "cols" else (m, n // 2)


def _shard_shape(shape, kind):
    m, n = shape
    return (m, n // 4) if kind == "cols" else (m // 4, n)


def _remote(src, dst, k, to, send_sems, recv_sems):
    return pltpu.make_async_remote_copy(src_ref=src, dst_ref=dst, send_sem=send_sems.at[k], recv_sem=recv_sems.at[k],
                                        device_id=to, device_id_type=MESH)


def _gather_exchange(shards, full, kind, layer):
    shape = full.shape[1:]

    def copies(xin, xout, send_sems, recv_sems):
        x, y, c, chips = _position()
        my_chip = 2 * x + y
        own = xin[0].at[layer]
        place = lambda chip: _shard_of(xout[0].at[layer], shape, kind, chip)
        peers = [(x, y, 1 - c)] + [(*chip, c) for chip in chips]
        sources = [my_chip] + [2 * cx + cy for cx, cy in chips]
        sends = [_remote(own, place(my_chip), k, peer, send_sems, recv_sems) for k, peer in enumerate(peers)]
        arrivals = [_remote(place(s), place(s), k, (x, y, c), send_sems, recv_sems) for k, s in enumerate(sources)]
        return sends, arrivals

    return _Exchange([shards, full], [1], [], 4, copies)


def _swap_exchange(grads):
    shapes = [g.shape for g in grads]
    fresh = [jax.ShapeDtypeStruct(_half_shape(s, k), F32) for s, k in zip(shapes, _SHARD_KINDS)]

    def copies(xin, xout, send_sems, recv_sems):
        x, y, c, _ = _position()
        sends = [_remote(_half_of(xin[a], shapes[a], _SHARD_KINDS[a], 1 - c), xout[a], a, (x, y, 1 - c),
                         send_sems, recv_sems) for a in range(len(grads))]
        arrivals = [_remote(xout[a], xout[a], a, (x, y, c), send_sems, recv_sems) for a in range(len(grads))]
        return sends, arrivals

    return _Exchange(list(grads), [], fresh, len(grads), copies)


def _scatter_exchange(partials):
    n_w = len(partials)
    shapes = [p.shape for p in partials]
    fresh = [jax.ShapeDtypeStruct((3,) + _shard_shape(s, k), p.dtype) for s, k, p in zip(shapes, _SHARD_KINDS, partials)]

    def copies(xin, xout, send_sems, recv_sems):
        x, y, c, chips = _position()
        sends, arrivals = [], []
        for j, (cx, cy) in enumerate(chips):
            for a in range(n_w):
                src = _shard_of(xin[a], shapes[a], _SHARD_KINDS[a], 2 * cx + cy)
                sends.append(_remote(src, xout[a].at[j], j * n_w + a, (cx, cy, c), send_sems, recv_sems))
                arrivals.append(_remote(xout[a].at[j], xout[a].at[j], j * n_w + a, (x, y, c), send_sems, recv_sems))
        return sends, arrivals

    return _Exchange(list(partials), [], fresh, 3 * n_w, copies)


def _share_exchange(reduced, layer):
    n_w = len(reduced)
    shapes = [r.shape[1:] for r in reduced]

    def copies(xin, xout, send_sems, recv_sems):
        x, y, c, _ = _position()
        half = lambda a, h: _half_of(xout[a].at[layer], shapes[a], _SHARD_KINDS[a], h)
        sends = [_remote(half(a, c), half(a, c), a, (x, y, 1 - c), send_sems, recv_sems) for a in range(n_w)]
        arrivals = [_remote(half(a, 1 - c), half(a, 1 - c), a, (x, y, c), send_sems, recv_sems) for a in range(n_w)]
        return sends, arrivals

    return _Exchange(list(reduced), list(range(n_w)), [], n_w, copies)


def _chip_partial(pos, grad, theirs, kind, name):
    M, N = grad.shape
    if kind == "cols":
        tm = 256
        steps = M // 2 // tm
        block, g_map = (tm, N), (lambda i, pos: (pos[0] * steps + i, 0))
    else:
        tm = M // 4
        steps = 4
        block, g_map = (tm, N // 2), (lambda i, pos: (i, pos[0]))

    def body(pos_ref, g_ref, t_ref, o_ref):
        o_ref[...] = (g_ref[...] + t_ref[...]).astype(WIRE_DTYPE)

    grid_spec = pltpu.PrefetchScalarGridSpec(
        num_scalar_prefetch=1, grid=(steps,),
        in_specs=[pl.BlockSpec(block, g_map), pl.BlockSpec(block, lambda i, pos: (i, 0))],
        out_specs=pl.BlockSpec(block, lambda i, pos: (i, 0)))
    return pl.pallas_call(
        body, name=name, grid_spec=grid_spec, out_shape=jax.ShapeDtypeStruct(_half_shape((M, N), kind), WIRE_DTYPE),
        compiler_params=_params(1),
    )(pos, grad, theirs)


def _reduce_shard(pos, grad, theirs, received, reduced, kind, layer, name):
    M, N = grad.shape
    if kind == "cols":
        tm = min(M // 2, 512)
        steps = M // 2 // tm
        block = (tm, N // 4)
        g_map = lambda i, pos: (pos[0] * steps + i, pos[1])
        t_map = lambda i, pos: (i, pos[1])
        o_map = lambda i, pos: (layer, pos[0] * steps + i, 0)
    else:
        steps = 1
        block = (M // 4, N // 2)
        g_map = lambda i, pos: (pos[1], pos[0])
        t_map = lambda i, pos: (pos[1], 0)
        o_map = lambda i, pos: (layer, 0, pos[0])
    out_shape = (DEPTH,) + _shard_shape((M, N), kind)

    def body(pos_ref, g_ref, t_ref, r0_ref, r1_ref, r2_ref, *rest):
        o_ref = rest[-1]
        chip = pos_ref[1]
        own = g_ref[...] + t_ref[...]
        r = [r0_ref[...].astype(F32), r1_ref[...].astype(F32), r2_ref[...].astype(F32)]
        total = None
        for s in range(N_CHIPS):
            rel = jnp.bitwise_xor(chip, s)
            term = jnp.where(rel == 0, own, jnp.where(rel == 2, r[0], jnp.where(rel == 1, r[1], r[2])))
            total = term if total is None else total + term
        o_ref[...] = total

    r_spec = lambda j: pl.BlockSpec((None,) + block, lambda i, pos: (j, i, 0))
    in_specs = [pl.BlockSpec(block, g_map), pl.BlockSpec(block, t_map), r_spec(0), r_spec(1), r_spec(2)]
    args = [pos, grad, theirs, received, received, received]
    aliases = {}
    if reduced is not None:
        in_specs.append(ANY)
        args.append(reduced)
        aliases = {6: 0}
    grid_spec = pltpu.PrefetchScalarGridSpec(
        num_scalar_prefetch=1, grid=(steps,), in_specs=in_specs, out_specs=pl.BlockSpec((None,) + block, o_map))
    return pl.pallas_call(
        body, name=name, grid_spec=grid_spec, out_shape=jax.ShapeDtypeStruct(out_shape, F32),
        input_output_aliases=aliases, compiler_params=_params(1),
    )(*args)


def _sum_devices(gathered):
    R8, C = gathered.shape
    R = R8 // N_DEV
    lanes = C // 128
    tc = 128 * max(k for k in range(1, lanes + 1) if lanes % k == 0 and k * 128 * R8 * 4 <= 2 ** 22)

    def body(g_ref, o_ref):
        total = g_ref[0:R, :]
        for d in range(1, N_DEV):
            total = total + g_ref[d * R:(d + 1) * R, :]
        o_ref[...] = total

    return pl.pallas_call(
        body, name="sum_devices", grid=(C // tc,),
        in_specs=[pl.BlockSpec((R8, tc), lambda j: (0, j))], out_specs=pl.BlockSpec((R, tc), lambda j: (0, j)),
        out_shape=jax.ShapeDtypeStruct((R, C), F32), compiler_params=_params(1),
    )(gathered)


_SMALL = ("pool_w", "pool_scale", "sgu_ln_g", "sgu_ln_b", "sgu_w", "sgu_b", "ln1_g", "ln1_b", "conv_w",
          "conv_b", "ln2_g", "ln2_b", "ada_b")
_WEIGHTS = ("ada_w", "ada_b", "w_in", "pool_w", "pool_scale", "sgu_ln_g", "sgu_ln_b", "sgu_w", "sgu_b", "w_out",
            "ln1_g", "ln1_b", "w_up", "conv_w", "conv_b", "w_down", "ln2_g", "ln2_b")


def kernel(x, c, ada_w, ada_b, w_in, pool_w, pool_scale, sgu_ln_g, sgu_ln_b, sgu_w, sgu_b, w_out, ln1_g, ln1_b, w_up, conv_w, conv_b, w_down, ln2_g, ln2_b, loss_target, m_ada_w, m_ada_b, m_w_in, m_pool_w, m_pool_scale, m_sgu_ln_g, m_sgu_ln_b, m_sgu_w, m_sgu_b, m_w_out, m_ln1_g, m_ln1_b, m_w_up, m_conv_w, m_conv_b, m_w_down, m_ln2_g, m_ln2_b, v_ada_w, v_ada_b, v_w_in, v_pool_w, v_pool_scale, v_sgu_ln_g, v_sgu_ln_b, v_sgu_w, v_sgu_b, v_w_out, v_ln1_g, v_ln1_b, v_w_up, v_conv_w, v_conv_b, v_w_down, v_ln2_g, v_ln2_b):
    weights = dict(ada_w=ada_w, ada_b=ada_b, w_in=w_in, pool_w=pool_w, pool_scale=pool_scale, sgu_ln_g=sgu_ln_g,
                   sgu_ln_b=sgu_ln_b, sgu_w=sgu_w, sgu_b=sgu_b, w_out=w_out, ln1_g=ln1_g, ln1_b=ln1_b, w_up=w_up,
                   conv_w=conv_w, conv_b=conv_b, w_down=w_down, ln2_g=ln2_g, ln2_b=ln2_b)
    mom_m = dict(ada_w=m_ada_w, ada_b=m_ada_b, w_in=m_w_in, pool_w=m_pool_w, pool_scale=m_pool_scale,
                 sgu_ln_g=m_sgu_ln_g, sgu_ln_b=m_sgu_ln_b, sgu_w=m_sgu_w, sgu_b=m_sgu_b, w_out=m_w_out,
                 ln1_g=m_ln1_g, ln1_b=m_ln1_b, w_up=m_w_up, conv_w=m_conv_w, conv_b=m_conv_b, w_down=m_w_down,
                 ln2_g=m_ln2_g, ln2_b=m_ln2_b)
    mom_v = dict(ada_w=v_ada_w, ada_b=v_ada_b, w_in=v_w_in, pool_w=v_pool_w, pool_scale=v_pool_scale,
                 sgu_ln_g=v_sgu_ln_g, sgu_ln_b=v_sgu_ln_b, sgu_w=v_sgu_w, sgu_b=v_sgu_b, w_out=v_w_out,
                 ln1_g=v_ln1_g, ln1_b=v_ln1_b, w_up=v_w_up, conv_w=v_conv_w, conv_b=v_conv_b, w_down=v_w_down,
                 ln2_g=v_ln2_g, ln2_b=v_ln2_b)

    ix, iy, ic = lax.axis_index("x"), lax.axis_index("y"), lax.axis_index("c")
    chip = 2 * ix + iy
    dev = 4 * ix + 2 * iy + ic
    pos = jnp.stack([ic, chip]).astype(jnp.int32)
    xs = x[0]
    target = loss_target[0]
    ff_shard = conv_w.shape[2]
    mod_shard = ada_w.shape[2]

    first = _gather_flat(jnp.concatenate([c.reshape(-1), conv_w.reshape(-1)]), "gather_c_conv")
    c_all = jnp.pad(first[:, :D], ((0, 8), (0, 0)))
    conv_parts = first[0::2, D:].reshape(N_CHIPS, DEPTH, 3, ff_shard)
    conv_full = jnp.transpose(conv_parts, (1, 2, 0, 3)).reshape(DEPTH, 3, D_FF)
    ada_b_cols = lax.dynamic_slice_in_dim(ada_b, chip * mod_shard, mod_shard, axis=1).reshape(DEPTH, 1, mod_shard)
    mod_part = _ada_forward(c_all, ada_w, ada_b_cols)[:, :8, :]
    mod_all = _gather_flat(mod_part.reshape(-1), "gather_mod").reshape(N_DEV, DEPTH, 8, mod_shard)
    mod_mine = lax.dynamic_index_in_dim(mod_all[0::2], dev, axis=2, keepdims=False)
    mod = jnp.transpose(mod_mine, (1, 0, 2)).reshape(DEPTH, 6, 1, D)

    sh_in, sh_out, sh_up, sh_down = (w.astype(WIRE_DTYPE) for w in (w_in, w_out, w_up, w_down))
    wf_in = lax.empty((DEPTH, D, D_IN), WIRE_DTYPE)
    wf_out = lax.empty((DEPTH, D, D), WIRE_DTYPE)
    wf_up = lax.empty((DEPTH, D, 2 * D_FF), WIRE_DTYPE)
    wf_down = lax.empty((DEPTH, D_FF, D), WIRE_DTYPE)
    (wf_in,) = _run_exchange(_gather_exchange(sh_in, wf_in, "cols", 0), "gather_w_in_l0")

    tri = jnp.tril(jnp.ones((CHUNK, CHUNK), dtype=bool))
    ones_row = jnp.ones((1, D), F32)
    zeros_row = jnp.zeros((1, D), F32)
    row = lambda a, l: a[l].reshape(1, -1)

    saved = []
    xn, lg, lb = xs, ones_row, zeros_row
    h = _modulate_input(xs, mod[0, 0], mod[0, 1])
    for l in range(DEPTH):
        shift1, scale1, gate1, shift2, scale2, gate2 = (mod[l, k] for k in range(6))
        nxt = min(l + 1, DEPTH - 1)
        pw = pool_w[l].astype(MXU_DTYPE)
        wm = jnp.where(tri[None], sgu_w[l], 0.0).astype(MXU_DTYPE)
        bias_full = jnp.repeat(jnp.transpose(sgu_b[l]), GROUP, axis=1)
        lg1, lb1 = row(ln1_g, l), row(ln1_b, l)
        proj, (wf_out,) = _matmul(h, wf_in, l, D_IN, STORE_DTYPE, f"f1_in_proj_l{l}",
                                  exchange=_gather_exchange(sh_out, wf_out, "rows", l))
        (mix, f1, xhat1, rstd1, h2, gel), (wf_up,) = _f2_mixers(
            proj, xn, lg, lb, gate1, pw, row(pool_scale, l), row(sgu_ln_g, l), row(sgu_ln_b, l), wm, bias_full,
            wf_out, lg1, lb1, shift2, scale2, l, exchange=_gather_exchange(sh_up, wf_up, "cols", l))
        up, (wf_down,) = _matmul(h2, wf_up, l, D_FF, STORE_DTYPE, f"f3_up_proj_l{l}",
                                 exchange=_gather_exchange(sh_down, wf_down, "rows", l))
        (act, f2, xhat2, rstd2, h_next, ge, dge), filled = _f4_ffn(
            up, xhat1, lg1, lb1, gate2, conv_full[l], row(conv_b, l), wf_down, row(ln2_g, l), row(ln2_b, l),
            mod[nxt, 0], mod[nxt, 1], l,
            exchange=_gather_exchange(sh_in, wf_in, "cols", l + 1) if l + 1 < DEPTH else None)
        if filled:
            (wf_in,) = filled
        saved.append(dict(xn=xn, lg=lg, lb=lb, h=h, proj=proj, mix=mix, f1=f1, xhat1=xhat1, rstd1=rstd1, h2=h2,
                          up=up, act=act, f2=f2, xhat2=xhat2, rstd2=rstd2, pw=pw, wm=wm, bias_full=bias_full,
                          ge=ge, dge=dge, gel=gel))
        xn, lg, lb, h = xhat2, row(ln2_g, l), row(ln2_b, l), h_next

    loss_part, d_out = _loss_head(xn, lg, lb, target)
    loss = lax.psum(loss_part[0, 0], ("x", "y", "c"))

    tags = ("in", "out", "up", "down")
    small = {n: [None] * DEPTH for n in _SMALL}
    dmod = [[None] * 6 for _ in range(DEPTH)]
    below = None
    ready = None
    reduced = [None] * 4

    def partial_sums(layer, full, theirs):
        return [_chip_partial(pos, g, t, kind, f"chip_partial_{tag}_l{layer}")
                for g, t, kind, tag in zip(full, theirs, _SHARD_KINDS, tags)]

    def chip_sums(layer, full, theirs, received):
        return [_reduce_shard(pos, g, t, r, out, kind, layer, f"reduce_shard_{tag}_l{layer}")
                for g, t, r, out, kind, tag in zip(full, theirs, received, reduced, _SHARD_KINDS, tags)]

    for l in reversed(range(DEPTH)):
        sv = saved[l]
        scale1, gate1, scale2, gate2 = mod[l, 1], mod[l, 2], mod[l, 4], mod[l, 5]
        lg1, lb1 = row(ln1_g, l), row(ln1_b, l)
        outs, theirs = _b_ln(d_out, sv["xhat2"], sv["rstd2"], sv["f2"], row(ln2_g, l), row(ln2_b, l), gate2,
                             f"b1_ln2_l{l}", modulated=below,
                             exchange=_swap_exchange(ready[1]) if ready else None)
        df2, dres2, dlg2, dlb2, dmod[l][5] = outs[:5]
        if below is not None:
            dmod[l + 1][1], dmod[l + 1][0] = outs[5], outs[6]
        partials = partial_sums(ready[0], ready[1], theirs) if ready else None
        (dup, dcw, dcb), received = _b2_ffn(df2, wf_down, sv["up"], sv["ge"], sv["dge"], conv_full[l], l,
                                            exchange=_scatter_exchange(partials) if ready else None)
        if ready:
            reduced = chip_sums(ready[0], ready[1], theirs, received)
        g_down = _weight_grad(sv["act"], [df2], 512, f"wg_down_l{l}", rows=2 * TS_TN)
        dh2, shared = _matmul_nt(dup, wf_up, l, D_FF, f"b3_up_l{l}",
                                 exchange=_share_exchange(reduced, ready[0]) if ready else None)
        if ready:
            reduced = shared
        g_up = _weight_grad(sv["h2"], [dup], 1408, f"wg_up_l{l}", rows=2 * TS_TN)
        (df1, dres1, dlg1, dlb1, dmod[l][2], dmod[l][4], dmod[l][3]), _ = _b_ln(
            dh2, sv["xhat1"], sv["rstd1"], sv["f1"], lg1, lb1, gate1, f"b4_ln1_l{l}", modulated=(dres2, scale2))
        dproj, dpw, dps, dsg, dsb, dwm, dbias = _b5_mixers(
            df1, wf_out, sv["proj"], sv["gel"], sv["pw"], row(pool_scale, l), row(sgu_ln_g, l), row(sgu_ln_b, l), sv["wm"],
            sv["bias_full"], l)
        g_out = _weight_grad(sv["mix"], [df1], D, f"wg_out_l{l}", rows=2 * TS_TN)
        d_out, _ = _matmul_nt(dproj, wf_in, l, D_IN, f"b6_in_l{l}")
        g_in = _weight_grad(sv["h"], [dproj], 768, f"wg_in_l{l}", rows=2 * TS_TN)
        ready = (l, [g_in, g_out, g_up, g_down])
        below = (dres1, scale1)
        small["pool_w"][l], small["pool_scale"][l] = dpw, dps[0]
        small["sgu_ln_g"][l], small["sgu_ln_b"][l], small["sgu_w"][l] = dsg[0], dsb[0], dwm
        small["sgu_b"][l] = jnp.transpose(jnp.sum(dbias.reshape(CHUNK, HEADS, GROUP), axis=2))
        small["ln1_g"][l], small["ln1_b"][l] = dlg1[0], dlb1[0]
        small["conv_w"][l], small["conv_b"][l] = dcw, dcb[0]
        small["ln2_g"][l], small["ln2_b"][l] = dlg2[0], dlb2[0]
    (grad_x2d, dmod[0][1], dmod[0][0]), theirs = _input_grad(d_out, below[0], below[1], xs,
                                                              exchange=_swap_exchange(ready[1]))
    grad_x = grad_x2d[None]
    small["ada_b"] = [jnp.concatenate([part[0] for part in dmod[l]]) for l in range(DEPTH)]

    names = _SMALL
    flat = jnp.concatenate([jnp.stack(small[n]).reshape(-1) for n in names])
    n_small = flat.shape[0]
    padded = -(-n_small // 1024) * 1024
    gathered = _all_gather8(jnp.pad(flat, (0, padded - n_small)).reshape(8, padded // 8), "gather_small_grads")
    summed = _sum_devices(gathered).reshape(-1)[:n_small]
    grads = {}
    offset = 0
    for n in names:
        size = math.prod(weights[n].shape[1:]) * DEPTH if n != "conv_w" else DEPTH * 3 * D_FF
        grads[n] = summed[offset:offset + size]
        offset += size
    grads["conv_w"] = lax.dynamic_slice_in_dim(grads["conv_w"].reshape(DEPTH, 3, D_FF), chip * ff_shard, ff_shard,
                                               axis=2)
    for n in names:
        grads[n] = grads[n].reshape(weights[n].shape)
    dmod_all = gathered.reshape(N_DEV, padded)[:, n_small - DEPTH * 6 * D:n_small].reshape(N_DEV, DEPTH, 6 * D)
    dmod_cols = lax.dynamic_slice_in_dim(jnp.transpose(dmod_all, (1, 0, 2)), chip * mod_shard, mod_shard, axis=2)
    grads["ada_w"] = _ada_backward(c_all, jnp.pad(dmod_cols, ((0, 0), (0, 8), (0, 0))))

    partials = partial_sums(0, ready[1], theirs)
    received = _run_exchange(_scatter_exchange(partials), "scatter_partials_l0")
    reduced = chip_sums(0, ready[1], theirs, received)
    grads["w_in"], grads["w_out"], grads["w_up"], grads["w_down"] = _run_exchange(
        _share_exchange(reduced, 0), "share_reduced_l0")

    delta, new_m, new_v = {}, {}, {}
    for n in ("ada_w", "w_in", "w_out", "w_up", "w_down"):
        shape = weights[n].shape
        two_d = (shape[0] * shape[1], shape[2])
        d_, m_, v_ = _adamw(weights[n].reshape(two_d), grads[n].reshape(two_d), mom_m[n].reshape(two_d),
                            mom_v[n].reshape(two_d), f"adamw_{n}")
        delta[n], new_m[n], new_v[n] = d_.reshape(shape), m_.reshape(shape), v_.reshape(shape)
    sizes = [math.prod(weights[n].shape) for n in names]
    total = sum(sizes)
    padded = -(-total // 1024) * 1024
    pack = lambda d: jnp.pad(jnp.concatenate([d[n].reshape(-1) for n in names]), (0, padded - total)).reshape(8, -1)
    v_pack = jnp.pad(jnp.concatenate([mom_v[n].reshape(-1) for n in names]), (0, padded - total),
                     constant_values=1.0).reshape(8, -1)
    d_, m_, v_ = _adamw(pack(weights), pack(grads), pack(mom_m), v_pack, "adamw_small")
    offset = 0
    for n, size in zip(names, sizes):
        for src, dst in ((d_, delta), (m_, new_m), (v_, new_v)):
            dst[n] = src.reshape(-1)[offset:offset + size].reshape(weights[n].shape)
        offset += size

    return (loss, grad_x, *[grads[n] for n in _WEIGHTS], *[delta[n] for n in _WEIGHTS],
            *[new_m[n] for n in _WEIGHTS], *[new_v[n] for n in _WEIGHTS])
```

```python
import functools
import math

import jax
import jax.numpy as jnp
from jax import lax
from jax.experimental import pallas as pl
from jax.experimental.pallas import tpu as pltpu

F32 = jnp.float32
MXU_DTYPE = jnp.bfloat16
WIRE_DTYPE = jnp.bfloat16

DEPTH = 4
D = 1024
D_POOL = 512
D_SGU = 512
N_GROUPS = 4
GROUP = 128
POOL_WINDOWS = (2, 4, 8, 16)
POOL_HALO = 16
CHUNK = 128
HEADS = 4
D_IN = D_POOL + 2 * D_SGU
D_FF = 2816
CONV_HALO = 16
STORE_DTYPE = jnp.bfloat16
GELU_DTYPE = jnp.bfloat16
N_CHIPS = 4
N_DEV = 8
ALPHA = (2.0 * DEPTH) ** 0.25
LN_EPS = 1e-5
ADAM_LR, ADAM_B1, ADAM_B2, ADAM_EPS, ADAM_WD, ADAM_STEP = 0.001, 0.9, 0.999, 1e-08, 0.01, 10

TS_PROJ = 512
TS_MM = 1024
TS_MIX = 512
TS_FF = 256
TS_TN = 1024
VMEM_LIMIT = 52 * 2 ** 20

MESH = pl.DeviceIdType.MESH
ANY = pl.BlockSpec(memory_space=pl.ANY)

_GELU_K0 = math.sqrt(2.0 / math.pi)
_GELU_K1 = 0.044715


def _params(n_axes):
    return pltpu.CompilerParams(dimension_semantics=("arbitrary",) * n_axes, vmem_limit_bytes=VMEM_LIMIT)


class _Exchange:
    def __init__(self, inputs, inplace, fresh, n_copies, copies):
        self.inputs, self.inplace, self.fresh, self.n_copies, self.copies = inputs, inplace, fresh, n_copies, copies

    def out_shapes(self):
        return [jax.ShapeDtypeStruct(self.inputs[i].shape, self.inputs[i].dtype) for i in self.inplace] + list(self.fresh)

    def semaphores(self):
        return [pltpu.SemaphoreType.DMA((self.n_copies,)), pltpu.SemaphoreType.DMA((self.n_copies,))]

    def aliases(self, first_input, first_output):
        return {first_input + i: first_output + k for k, i in enumerate(self.inplace)}


def _call(body, *, name, grid, in_specs, out_specs, out_shape, args, scratch_shapes=(), exchange=None):
    n_axes = len(grid)
    in_specs, out_specs, out_shape, scratch_shapes = list(in_specs), list(out_specs), list(out_shape), list(scratch_shapes)
    if exchange is None:
        outs = pl.pallas_call(body, name=name, grid=grid, in_specs=in_specs, out_specs=out_specs, out_shape=out_shape,
                              scratch_shapes=scratch_shapes, compiler_params=_params(n_axes))(*args)
        return list(outs), []
    n_in, n_out, n_scr = len(in_specs), len(out_shape), len(scratch_shapes)
    x_out = exchange.out_shapes()
    n_xin, n_xout = len(exchange.inputs), len(x_out)

    def hosted(*refs):
        ins, xin = refs[:n_in], refs[n_in:n_in + n_xin]
        o = n_in + n_xin
        outs, xout = refs[o:o + n_out], refs[o + n_out:o + n_out + n_xout]
        s = o + n_out + n_xout
        scr, (send_sems, recv_sems) = refs[s:s + n_scr], refs[s + n_scr:]
        first = functools.reduce(jnp.logical_and, [pl.program_id(d) == 0 for d in range(n_axes)])
        last = functools.reduce(jnp.logical_and, [pl.program_id(d) == pl.num_programs(d) - 1 for d in range(n_axes)])

        @pl.when(first)
        def _():
            for cp in exchange.copies(xin, xout, send_sems, recv_sems)[0]:
                cp.start()

        body(*ins, *outs, *scr)

        @pl.when(last)
        def _():
            sends, arrivals = exchange.copies(xin, xout, send_sems, recv_sems)
            for cp in arrivals:
                cp.wait_recv()
            for cp in sends:
                cp.wait_send()

    results = pl.pallas_call(
        hosted, name=name, grid=grid, in_specs=in_specs + [ANY] * n_xin, out_specs=out_specs + [ANY] * n_xout,
        out_shape=out_shape + x_out, scratch_shapes=scratch_shapes + exchange.semaphores(),
        input_output_aliases=exchange.aliases(n_in, n_out), compiler_params=_params(n_axes),
    )(*args, *exchange.inputs)
    return list(results[:n_out]), list(results[n_out:])


def _run_exchange(exchange, name):
    x_out = exchange.out_shapes()
    n_xin, n_xout = len(exchange.inputs), len(x_out)

    def body(*refs):
        xin, xout = refs[:n_xin], refs[n_xin:n_xin + n_xout]
        send_sems, recv_sems = refs[n_xin + n_xout:]
        sends, arrivals = exchange.copies(xin, xout, send_sems, recv_sems)
        for cp in sends:
            cp.start()
        for cp in arrivals:
            cp.wait_recv()
        for cp in sends:
            cp.wait_send()

    return list(pl.pallas_call(
        body, name=name, in_specs=[ANY] * n_xin, out_specs=[ANY] * n_xout, out_shape=x_out,
        scratch_shapes=exchange.semaphores(), input_output_aliases=exchange.aliases(0, 0),
    )(*exchange.inputs))


def _dot(a, b):
    return jnp.dot(a, b, preferred_element_type=F32)


def _dot_nt(a, b):
    return lax.dot_general(a, b, (((1,), (1,)), ((), ())), preferred_element_type=F32)


def _dot_tn(a, b):
    return lax.dot_general(a, b, (((0,), (0,)), ((), ())), preferred_element_type=F32)


def _gelu(x):
    x2 = x * x
    t = jnp.tanh(x * (x2 * (_GELU_K0 * _GELU_K1) + _GELU_K0))
    cdf = 0.5 * t + 0.5
    dg = cdf + (x * (1.0 - t * t)) * (x2 * (1.5 * _GELU_K0 * _GELU_K1) + 0.5 * _GELU_K0)
    return x * cdf, dg


def _colsum(x):
    return jnp.sum(x, axis=0, keepdims=True)


def _row(d):
    return pl.BlockSpec((1, d), lambda *_: (0, 0))


def _full(shape):
    n = len(shape)
    return pl.BlockSpec(shape, lambda *_: (0,) * n)


def _layer_block(shape, layer):
    n = len(shape)
    return pl.BlockSpec((None,) + tuple(shape), lambda *_: (layer,) + (0,) * n)


def _pool_counts(first_row, rows, window):
    t = first_row + lax.broadcasted_iota(jnp.int32, (rows, 1), 0)
    return 1.0 / jnp.minimum(t + 1, window).astype(F32)


def _pool_forward(a_ext, a, first_row, rows):
    out = []
    for g, window in enumerate(POOL_WINDOWS):
        s = a_ext[:, g * GROUP:(g + 1) * GROUP]
        k = 1
        while k < window:
            s = s + pltpu.roll(s, k, 0)
            k *= 2
        inv = _pool_counts(first_row, rows, window)
        out.append(s[POOL_HALO:] * inv - a[:, g * GROUP:(g + 1) * GROUP])
    return out


def _sgu_norm(gv):
    mu = jnp.mean(gv, axis=-1, keepdims=True)
    xc = gv - mu
    var = jnp.mean(xc * xc, axis=-1, keepdims=True)
    rs = lax.rsqrt(var + LN_EPS)
    return xc * rs, rs


def _modulate_input(x, shift, scale):
    S = x.shape[0]
    ts = min(TS_PROJ, S)

    def body(x_ref, sh_ref, sc_ref, h_ref):
        h_ref[...] = (x_ref[...] * (1.0 + sc_ref[...]) + sh_ref[...]).astype(MXU_DTYPE)

    tile = pl.BlockSpec((ts, D), lambda i: (i, 0))
    return pl.pallas_call(
        body, name="modulate_input", grid=(S // ts,), in_specs=[tile, _row(D), _row(D)], out_specs=tile,
        out_shape=jax.ShapeDtypeStruct((S, D), MXU_DTYPE), compiler_params=_params(1),
    )(x, shift, scale)


def _matmul(a, w_all, layer, tn, out_dtype, name, exchange=None):
    S, K = a.shape
    N = w_all.shape[2]
    ts = min(TS_MM, S)

    def body(a_ref, w_ref, o_ref):
        o_ref[...] = _dot(a_ref[...], w_ref[...]).astype(out_dtype)

    (out,), extra = _call(
        body, name=name, grid=(N // tn, S // ts),
        in_specs=[pl.BlockSpec((ts, K), lambda j, i: (i, 0)), pl.BlockSpec((None, K, tn), lambda j, i: (layer, 0, j))],
        out_specs=[pl.BlockSpec((ts, tn), lambda j, i: (i, j))],
        out_shape=[jax.ShapeDtypeStruct((S, N), out_dtype)], args=(a, w_all), exchange=exchange)
    return out, extra


def _matmul_nt(a, w_all, layer, tk, name, exchange=None):
    S, K = a.shape
    N = w_all.shape[1]
    ts = min(TS_MM, S)
    n_k = K // tk

    def body(a_ref, w_ref, o_ref, acc):
        k = pl.program_id(1)
        part = _dot_nt(a_ref[...], w_ref[...])
        if n_k == 1:
            o_ref[...] = part
        else:
            @pl.when(k == 0)
            def _():
                acc[...] = part

            @pl.when(jnp.logical_and(k > 0, k < n_k - 1))
            def _():
                acc[...] += part

            @pl.when(k == n_k - 1)
            def _():
                o_ref[...] = acc[...] + part

    (out,), extra = _call(
        body, name=name, grid=(S // ts, n_k),
        in_specs=[pl.BlockSpec((ts, tk), lambda i, k: (i, k)), pl.BlockSpec((None, N, tk), lambda i, k: (layer, 0, k))],
        out_specs=[pl.BlockSpec((ts, N), lambda i, k: (i, 0))],
        out_shape=[jax.ShapeDtypeStruct((S, N), F32)],
        scratch_shapes=[pltpu.VMEM((ts, N), F32)], args=(a, w_all), exchange=exchange)
    return out, extra


def _f2_mixers(proj, xn, lg, lb, gate, pool_w, pool_scale, sg, sb, wm, bias_full, w_out_all, lg1, lb1, shift2, scale2,
               layer, exchange=None):
    S = xn.shape[0]
    ts = min(TS_MIX, S)
    n_chunks = ts // CHUNK
    halo_blocks = ts // POOL_HALO

    def body(p_ref, halo_ref, xn_ref, lg_ref, lb_ref, gate_ref, pw_ref, ps_ref, sg_ref, sb_ref, wm_ref,
             bias_ref, wo_ref, lg1_ref, lb1_ref, sh2_ref, sc2_ref, mix_ref, f_ref, xh_ref, rs_ref, h2_ref, gel_ref,
             z_scr):
        i = pl.program_id(0)
        a = p_ref[:, 0:D_POOL].astype(F32)
        u = p_ref[:, D_POOL:D_POOL + D_SGU].astype(GELU_DTYPE)
        v = p_ref[:, D_POOL + D_SGU:D_IN].astype(GELU_DTYPE)
        halo = halo_ref[...].astype(F32) * (i > 0).astype(F32)
        a_ext = jnp.concatenate([halo, a], axis=0)
        pooled = _pool_forward(a_ext, a, i * ts, ts)
        for g in range(N_GROUPS):
            mixed = _dot(pooled[g].astype(MXU_DTYPE), pw_ref[g])
            mix_ref[:, g * GROUP:(g + 1) * GROUP] = (mixed * ps_ref[:, g * GROUP:(g + 1) * GROUP]).astype(MXU_DTYPE)
        gu, dgu = _gelu(u)
        gv, dgv = _gelu(v)
        for k, part in enumerate((gu, dgu, gv, dgv)):
            gel_ref[:, k * D_SGU:(k + 1) * D_SGU] = part.astype(STORE_DTYPE)
        gu = gu.astype(F32)
        vhat, _ = _sgu_norm(gv.astype(F32))
        vn = (vhat * sg_ref[...] + sb_ref[...]).astype(MXU_DTYPE)
        for h in range(HEADS):
            cols = slice(h * GROUP, (h + 1) * GROUP)
            vn_h = jnp.concatenate([vn[c * CHUNK:(c + 1) * CHUNK, cols] for c in range(n_chunks)], axis=1)
            z_h = _dot(wm_ref[h], vn_h)
            for c in range(n_chunks):
                z_scr[c * CHUNK:(c + 1) * CHUNK, cols] = z_h[:, c * GROUP:(c + 1) * GROUP] + bias_ref[:, cols]
        mix_ref[:, D_POOL:D] = (gu * z_scr[...]).astype(MXU_DTYPE)
        f = _dot(mix_ref[...], wo_ref[...])
        f_ref[...] = f.astype(STORE_DTYPE)
        x = xn_ref[...] * lg_ref[...] + lb_ref[...]
        z1 = ALPHA * x + gate_ref[...] * f
        mu = jnp.mean(z1, axis=-1, keepdims=True)
        zc = z1 - mu
        var = jnp.mean(zc * zc, axis=-1, keepdims=True)
        rs = lax.rsqrt(var + LN_EPS)
        xhat = zc * rs
        xh_ref[...] = xhat
        rs_ref[...] = rs
        x1 = xhat * lg1_ref[...] + lb1_ref[...]
        h2_ref[...] = (x1 * (1.0 + sc2_ref[...]) + sh2_ref[...]).astype(MXU_DTYPE)

    tile = lambda w: pl.BlockSpec((ts, w), lambda i: (i, 0))
    return _call(
        body, name=f"f2_mixers_l{layer}", grid=(S // ts,),
        in_specs=[tile(D_IN),
                  pl.BlockSpec((POOL_HALO, D_POOL), lambda i: (jnp.maximum(i * halo_blocks - 1, 0), 0)),
                  tile(D), _row(D), _row(D), _row(D),
                  _full((N_GROUPS, GROUP, GROUP)), _row(D_POOL), _row(D_SGU), _row(D_SGU),
                  _full((HEADS, CHUNK, CHUNK)), _full((CHUNK, D_SGU)), _layer_block((D, D), layer),
                  _row(D), _row(D), _row(D), _row(D)],
        out_specs=[tile(D), tile(D), tile(D), tile(1), tile(D), tile(4 * D_SGU)],
        out_shape=[jax.ShapeDtypeStruct((S, D), MXU_DTYPE), jax.ShapeDtypeStruct((S, D), STORE_DTYPE),
                   jax.ShapeDtypeStruct((S, D), F32), jax.ShapeDtypeStruct((S, 1), F32),
                   jax.ShapeDtypeStruct((S, D), MXU_DTYPE), jax.ShapeDtypeStruct((S, 4 * D_SGU), STORE_DTYPE)],
        scratch_shapes=[pltpu.VMEM((ts, D_SGU), F32)],
        args=(proj, proj, xn, lg, lb, gate, pool_w, pool_scale, sg, sb, wm, bias_full, w_out_all, lg1, lb1, shift2,
              scale2), exchange=exchange)


def _conv_forward(g_ext, cw_ref, cb_ref):
    gm2 = pltpu.roll(g_ext, 2, 0)[CONV_HALO:]
    gm1 = pltpu.roll(g_ext, 1, 0)[CONV_HALO:]
    g0 = g_ext[CONV_HALO:]
    gc = ((cb_ref[...] + gm2 * cw_ref[0:1, :]) + gm1 * cw_ref[1:2, :]) + g0 * cw_ref[2:3, :]
    return gc, gm2, gm1


def _f4_ffn(up, xhat1, lg, lb, gate, conv_w, conv_b, w_down_all, lg2, lb2, shift_next, scale_next, layer,
            exchange=None):
    S = xhat1.shape[0]
    ts = min(TS_FF, S)
    halo_blocks = ts // CONV_HALO

    def body(g_ref, halo_ref, val_ref, xh_ref, lg_ref, lb_ref, gate_ref, cw_ref, cb_ref, wd_ref, lg2_ref, lb2_ref,
             shn_ref, scn_ref, act_ref, f_ref, xo_ref, rs_ref, hn_ref, ge_ref, dge_ref):
        i = pl.program_id(0)
        halo = halo_ref[...].astype(GELU_DTYPE) * (i > 0).astype(GELU_DTYPE)
        g_ext = jnp.concatenate([halo, g_ref[...].astype(GELU_DTYPE)], axis=0)
        cw = cw_ref[...].astype(GELU_DTYPE)
        gm2 = pltpu.roll(g_ext, 2, 0)[CONV_HALO:]
        gm1 = pltpu.roll(g_ext, 1, 0)[CONV_HALO:]
        gc = ((cb_ref[...].astype(GELU_DTYPE) + gm2 * cw[0:1, :]) + gm1 * cw[1:2, :]) + g_ext[CONV_HALO:] * cw[2:3, :]
        ge, dge = _gelu(gc)
        ge_ref[...] = ge.astype(STORE_DTYPE)
        dge_ref[...] = dge.astype(STORE_DTYPE)
        act = (ge * val_ref[...].astype(GELU_DTYPE)).astype(MXU_DTYPE)
        act_ref[...] = act
        f = _dot(act, wd_ref[...])
        f_ref[...] = f.astype(STORE_DTYPE)
        x = xh_ref[...] * lg_ref[...] + lb_ref[...]
        z = ALPHA * x + gate_ref[...] * f
        mu = jnp.mean(z, axis=-1, keepdims=True)
        zc = z - mu
        var = jnp.mean(zc * zc, axis=-1, keepdims=True)
        rs = lax.rsqrt(var + LN_EPS)
        xhat = zc * rs
        xo_ref[...] = xhat
        rs_ref[...] = rs
        x2 = xhat * lg2_ref[...] + lb2_ref[...]
        hn_ref[...] = (x2 * (1.0 + scn_ref[...]) + shn_ref[...]).astype(MXU_DTYPE)

    tile = lambda w: pl.BlockSpec((ts, w), lambda i: (i, 0))
    return _call(
        body, name=f"f4_ffn_l{layer}", grid=(S // ts,),
        in_specs=[pl.BlockSpec((ts, D_FF), lambda i: (i, 0)),
                  pl.BlockSpec((CONV_HALO, D_FF), lambda i: (jnp.maximum(i * halo_blocks - 1, 0), 0)),
                  pl.BlockSpec((ts, D_FF), lambda i: (i, 1)),
                  tile(D), _row(D), _row(D), _row(D), _full((3, D_FF)), _row(D_FF),
                  _layer_block((D_FF, D), layer), _row(D), _row(D), _row(D), _row(D)],
        out_specs=[tile(D_FF), tile(D), tile(D), tile(1), tile(D), tile(D_FF), tile(D_FF)],
        out_shape=[jax.ShapeDtypeStruct((S, D_FF), MXU_DTYPE), jax.ShapeDtypeStruct((S, D), STORE_DTYPE),
                   jax.ShapeDtypeStruct((S, D), F32), jax.ShapeDtypeStruct((S, 1), F32),
                   jax.ShapeDtypeStruct((S, D), MXU_DTYPE), jax.ShapeDtypeStruct((S, D_FF), STORE_DTYPE),
                   jax.ShapeDtypeStruct((S, D_FF), STORE_DTYPE)],
        args=(up, up, up, xhat1, lg, lb, gate, conv_w, conv_b, w_down_all, lg2, lb2, shift_next, scale_next),
        exchange=exchange)


def _loss_head(xhat, lg, lb, target):
    S = xhat.shape[0]
    ts = min(TS_PROJ, S)

    def body(xh_ref, lg_ref, lb_ref, t_ref, loss_ref, dy_ref, acc):
        i = pl.program_id(0)

        @pl.when(i == 0)
        def _():
            acc[...] = jnp.zeros_like(acc)

        err = (xh_ref[...] * lg_ref[...] + lb_ref[...]) - t_ref[...]
        dy_ref[...] = err * (1.0 / D)
        acc[...] += _colsum(err * err)

        @pl.when(i == pl.num_programs(0) - 1)
        def _():
            loss_ref[...] = jnp.sum(acc[...], axis=1, keepdims=True) * (0.5 / D)

    tile = pl.BlockSpec((ts, D), lambda i: (i, 0))
    return pl.pallas_call(
        body, name="loss_head", grid=(S // ts,),
        in_specs=[tile, _row(D), _row(D), tile],
        out_specs=[_full((1, 1)), tile],
        out_shape=[jax.ShapeDtypeStruct((1, 1), F32), jax.ShapeDtypeStruct((S, D), F32)],
        scratch_shapes=[pltpu.VMEM((1, D), F32)],
        compiler_params=_params(1),
    )(xhat, lg, lb, target)


def _accumulate(i, ref, value):
    @pl.when(i == 0)
    def _():
        ref[...] = value

    @pl.when(i > 0)
    def _():
        ref[...] += value


def _b_ln(d, xhat, rstd, f, lg, lb, gate, name, modulated=None, exchange=None):
    S = d.shape[0]
    ts = min(TS_PROJ, S)
    has_mod = modulated is not None

    def body(*refs):
        if has_mod:
            (d_ref, xh_ref, rs_ref, f_ref, lg_ref, lb_ref, gate_ref, dri_ref, sc_ref,
             df_ref, dres_ref, dlg_ref, dlb_ref, dgate_ref, dsc_ref, dsh_ref) = refs
        else:
            (d_ref, xh_ref, rs_ref, f_ref, lg_ref, lb_ref, gate_ref,
             df_ref, dres_ref, dlg_ref, dlb_ref, dgate_ref) = refs
        i = pl.program_id(0)
        xh = xh_ref[...]
        if has_mod:
            dh = d_ref[...]
            dxo_t = dh * (1.0 + sc_ref[...]) + dri_ref[...]
            _accumulate(i, dsc_ref, _colsum(dh * (xh * lg_ref[...] + lb_ref[...])))
            _accumulate(i, dsh_ref, _colsum(dh))
        else:
            dxo_t = d_ref[...]
        dxh = dxo_t * lg_ref[...]
        m1 = jnp.mean(dxh, axis=-1, keepdims=True)
        m2 = jnp.mean(dxh * xh, axis=-1, keepdims=True)
        dz = rs_ref[...] * (dxh - m1 - xh * m2)
        df_ref[...] = (dz * gate_ref[...]).astype(MXU_DTYPE)
        dres_ref[...] = ALPHA * dz
        _accumulate(i, dlg_ref, _colsum(dxo_t * xh))
        _accumulate(i, dlb_ref, _colsum(dxo_t))
        _accumulate(i, dgate_ref, _colsum(dz * f_ref[...].astype(F32)))

    tile = lambda w: pl.BlockSpec((ts, w), lambda i: (i, 0))
    n_sums = 5 if has_mod else 3
    in_specs = [tile(D), tile(D), tile(1), tile(D), _row(D), _row(D), _row(D)]
    args = [d, xhat, rstd, f, lg, lb, gate]
    if has_mod:
        in_specs += [tile(D), _row(D)]
        args += list(modulated)
    return _call(
        body, name=name, grid=(S // ts,), in_specs=in_specs,
        out_specs=[tile(D), tile(D)] + [_row(D)] * n_sums,
        out_shape=[jax.ShapeDtypeStruct((S, D), MXU_DTYPE), jax.ShapeDtypeStruct((S, D), F32)]
        + [jax.ShapeDtypeStruct((1, D), F32)] * n_sums,
        args=args, exchange=exchange)


def _input_grad(dh, dres, scale, x, exchange=None):
    S = dh.shape[0]
    ts = min(TS_PROJ, S)

    def body(dh_ref, dres_ref, sc_ref, x_ref, dx_ref, dsc_ref, dsh_ref):
        i = pl.program_id(0)
        dh_t = dh_ref[...]
        dx_ref[...] = dh_t * (1.0 + sc_ref[...]) + dres_ref[...]
        _accumulate(i, dsc_ref, _colsum(dh_t * x_ref[...]))
        _accumulate(i, dsh_ref, _colsum(dh_t))

    tile = pl.BlockSpec((ts, D), lambda i: (i, 0))
    return _call(
        body, name="input_grad", grid=(S // ts,), in_specs=[tile, tile, _row(D), tile],
        out_specs=[tile, _row(D), _row(D)],
        out_shape=[jax.ShapeDtypeStruct((S, D), F32)] + [jax.ShapeDtypeStruct((1, D), F32)] * 2,
        args=(dh, dres, scale, x), exchange=exchange)


def _b2_ffn(df2, w_down_all, up, ge, dge, conv_w, layer, exchange=None):
    S = df2.shape[0]
    ts = min(TS_FF, S)
    n_tiles = S // ts
    n_ext = ts + CONV_HALO

    def body(df_ref, wd_ref, g_ref, val_ref, ge_ref, dge_ref, cw_ref, dup_ref, dcw_ref, dcb_ref, next_dgc):
        i = pl.program_id(0)

        @pl.when(i == 0)
        def _():
            next_dgc[...] = jnp.zeros_like(next_dgc)

        dact = _dot_nt(df_ref[...], wd_ref[...]).astype(GELU_DTYPE)
        dup_ref[:, D_FF:2 * D_FF] = (dact * ge_ref[...].astype(GELU_DTYPE)).astype(MXU_DTYPE)
        dgc = (dact * val_ref[...].astype(GELU_DTYPE) * dge_ref[...].astype(GELU_DTYPE)).astype(F32)
        ext = jnp.concatenate([dgc, next_dgc[...]], axis=0)
        dp1 = pltpu.roll(ext, n_ext - 1, 0)[:ts]
        dp2 = pltpu.roll(ext, n_ext - 2, 0)[:ts]
        dup_ref[:, 0:D_FF] = (dgc * cw_ref[2:3, :] + dp1 * cw_ref[1:2, :] + dp2 * cw_ref[0:1, :]).astype(MXU_DTYPE)
        next_dgc[...] = dgc[0:CONV_HALO]
        g = g_ref[...].astype(F32)
        dcw = jnp.concatenate([_colsum(dp2 * g), _colsum(dp1 * g), _colsum(dgc * g)], axis=0)
        _accumulate(i, dcw_ref, dcw)
        _accumulate(i, dcb_ref, _colsum(dgc))

    tile = lambda w, col=0: pl.BlockSpec((ts, w), lambda i: (n_tiles - 1 - i, col))
    return _call(
        body, name=f"b2_ffn_l{layer}", grid=(n_tiles,),
        in_specs=[tile(D), _layer_block((D_FF, D), layer), tile(D_FF), tile(D_FF, 1), tile(D_FF), tile(D_FF),
                  _full((3, D_FF))],
        out_specs=[tile(2 * D_FF), _full((3, D_FF)), _row(D_FF)],
        out_shape=[jax.ShapeDtypeStruct((S, 2 * D_FF), MXU_DTYPE),
                   jax.ShapeDtypeStruct((3, D_FF), F32), jax.ShapeDtypeStruct((1, D_FF), F32)],
        scratch_shapes=[pltpu.VMEM((CONV_HALO, D_FF), F32)],
        args=(df2, w_down_all, up, up, ge, dge, conv_w), exchange=exchange)


def _b5_mixers(df1, w_out_all, proj, gel, pool_w, pool_scale, sg, sb, wm, bias_full, layer):
    S = df1.shape[0]
    ts = min(TS_MIX, S)
    n_chunks = ts // CHUNK
    halo_blocks = ts // POOL_HALO
    last_halo = S // POOL_HALO - 1

    def body(df_ref, dfh_ref, wo_ref, p_ref, ah_ref, gel_ref, pw_ref, ps_ref, sg_ref, sb_ref, wm_ref, bias_ref,
             dp_ref, dpw_ref, dps_ref, dsg_ref, dsb_ref, dwm_ref, dbias_ref, z_scr, dvn_scr):
        i = pl.program_id(0)
        last = pl.num_programs(0) - 1
        dmix = _dot_nt(df_ref[...], wo_ref[...])
        dmix_halo = _dot_nt(dfh_ref[...], wo_ref[0:D_POOL, :]) * (i < last).astype(F32)

        a = p_ref[:, 0:D_POOL].astype(F32)
        halo = ah_ref[...].astype(F32) * (i > 0).astype(F32)
        pooled = _pool_forward(jnp.concatenate([halo, a], axis=0), a, i * ts, ts)
        n = ts + POOL_HALO
        dps_parts = []
        for g, window in enumerate(POOL_WINDOWS):
            cols = slice(g * GROUP, (g + 1) * GROUP)
            pooled_b = pooled[g].astype(MXU_DTYPE)
            mixed = _dot(pooled_b, pw_ref[g])
            dya = dmix[:, cols]
            dps_parts.append(_colsum(dya * mixed))
            dmixed = (dya * ps_ref[:, cols]).astype(MXU_DTYPE)
            dpw_g = _dot_tn(pooled_b, dmixed)

            @pl.when(i == 0)
            def _():
                dpw_ref[g] = dpw_g

            @pl.when(i > 0)
            def _():
                dpw_ref[g] += dpw_g

            dpooled = _dot_nt(dmixed, pw_ref[g])
            dmixed_h = (dmix_halo[:, cols] * ps_ref[:, cols]).astype(MXU_DTYPE)
            dpooled_h = _dot_nt(dmixed_h, pw_ref[g])
            q = dpooled * _pool_counts(i * ts, ts, window)
            s = jnp.concatenate([q, dpooled_h * (1.0 / window)], axis=0)
            k = 1
            while k < window:
                s = s + pltpu.roll(s, n - k, 0)
                k *= 2
            dp_ref[:, cols] = (s[:ts] - dpooled).astype(MXU_DTYPE)
        _accumulate(i, dps_ref, jnp.concatenate(dps_parts, axis=1))

        gu, dgu, gv, dgv = (gel_ref[:, k * D_SGU:(k + 1) * D_SGU].astype(F32) for k in range(4))
        vhat, rs = _sgu_norm(gv)
        vn = (vhat * sg_ref[...] + sb_ref[...]).astype(MXU_DTYPE)
        dyb = dmix[:, D_POOL:D]
        dz = dyb * gu
        dzb = dz.astype(MXU_DTYPE)
        dbias = dz[0:CHUNK]
        for c in range(1, n_chunks):
            dbias = dbias + dz[c * CHUNK:(c + 1) * CHUNK]
        _accumulate(i, dbias_ref, dbias)
        for h in range(HEADS):
            cols = slice(h * GROUP, (h + 1) * GROUP)
            vn_h = jnp.concatenate([vn[c * CHUNK:(c + 1) * CHUNK, cols] for c in range(n_chunks)], axis=1)
            dz_h = jnp.concatenate([dzb[c * CHUNK:(c + 1) * CHUNK, cols] for c in range(n_chunks)], axis=1)
            z_h = _dot(wm_ref[h], vn_h)
            dvn_h = _dot_tn(wm_ref[h], dz_h)
            dwm_h = _dot_nt(dz_h, vn_h)
            for c in range(n_chunks):
                rows = slice(c * CHUNK, (c + 1) * CHUNK)
                z_scr[rows, cols] = z_h[:, c * GROUP:(c + 1) * GROUP] + bias_ref[:, cols]
                dvn_scr[rows, cols] = dvn_h[:, c * GROUP:(c + 1) * GROUP]

            @pl.when(i == 0)
            def _():
                dwm_ref[h] = dwm_h

            @pl.when(i > 0)
            def _():
                dwm_ref[h] += dwm_h

        dp_ref[:, D_POOL:D_POOL + D_SGU] = (dyb * z_scr[...] * dgu).astype(MXU_DTYPE)
        dvn = dvn_scr[...]
        _accumulate(i, dsg_ref, _colsum(dvn * vhat))
        _accumulate(i, dsb_ref, _colsum(dvn))
        dvh = dvn * sg_ref[...]
        m1 = jnp.mean(dvh, axis=-1, keepdims=True)
        m2 = jnp.mean(dvh * vhat, axis=-1, keepdims=True)
        dp_ref[:, D_POOL + D_SGU:D_IN] = (rs * (dvh - m1 - vhat * m2) * dgv).astype(MXU_DTYPE)

        @pl.when(i == last)
        def _():
            tri = (lax.broadcasted_iota(jnp.int32, (CHUNK, CHUNK), 0)
                   >= lax.broadcasted_iota(jnp.int32, (CHUNK, CHUNK), 1))
            for h in range(HEADS):
                dwm_ref[h] = jnp.where(tri, dwm_ref[h], 0.0)

    tile = lambda w: pl.BlockSpec((ts, w), lambda i: (i, 0))
    return pl.pallas_call(
        body, name=f"b5_mixers_l{layer}", grid=(S // ts,),
        in_specs=[tile(D),
                  pl.BlockSpec((POOL_HALO, D), lambda i: (jnp.minimum((i + 1) * halo_blocks, last_halo), 0)),
                  _layer_block((D, D), layer), tile(D_POOL),
                  pl.BlockSpec((POOL_HALO, D_POOL), lambda i: (jnp.maximum(i * halo_blocks - 1, 0), 0)),
                  tile(4 * D_SGU),
                  _full((N_GROUPS, GROUP, GROUP)), _row(D_POOL), _row(D_SGU), _row(D_SGU),
                  _full((HEADS, CHUNK, CHUNK)), _full((CHUNK, D_SGU))],
        out_specs=[tile(D_IN), _full((N_GROUPS, GROUP, GROUP)), _row(D_POOL), _row(D_SGU), _row(D_SGU),
                   _full((HEADS, CHUNK, CHUNK)), _full((CHUNK, D_SGU))],
        out_shape=[jax.ShapeDtypeStruct((S, D_IN), MXU_DTYPE), jax.ShapeDtypeStruct((N_GROUPS, GROUP, GROUP), F32),
                   jax.ShapeDtypeStruct((1, D_POOL), F32), jax.ShapeDtypeStruct((1, D_SGU), F32),
                   jax.ShapeDtypeStruct((1, D_SGU), F32), jax.ShapeDtypeStruct((HEADS, CHUNK, CHUNK), F32),
                   jax.ShapeDtypeStruct((CHUNK, D_SGU), F32)],
        scratch_shapes=[pltpu.VMEM((ts, D_SGU), F32), pltpu.VMEM((ts, D_SGU), F32)],
        compiler_params=_params(1),
    )(df1, df1, w_out_all, proj, proj, gel, pool_w, pool_scale, sg, sb, wm, bias_full)


def _weight_grad(a, bs, tn, name, rows=TS_TN):
    S, M = a.shape
    counts = [b.shape[1] // tn for b in bs]
    starts = [sum(counts[:k]) for k in range(len(bs))]
    ts = min(rows, S)

    def body(a_ref, *refs):
        b_refs, o_ref = refs[:-1], refs[-1]
        j = pl.program_id(0)
        for b_ref, lo, n in zip(b_refs, starts, counts):
            @pl.when(jnp.logical_and(j >= lo, j < lo + n))
            def _():
                _accumulate(pl.program_id(1), o_ref, _dot_tn(a_ref[...], b_ref[...]))

    b_spec = lambda lo, n: pl.BlockSpec((ts, tn), lambda j, i: (i, jnp.clip(j - lo, 0, n - 1)))
    return pl.pallas_call(
        body, name=name, grid=(sum(counts), S // ts),
        in_specs=[pl.BlockSpec((ts, M), lambda j, i: (i, 0))] + [b_spec(lo, n) for lo, n in zip(starts, counts)],
        out_specs=pl.BlockSpec((M, tn), lambda j, i: (0, j)),
        out_shape=jax.ShapeDtypeStruct((M, sum(counts) * tn), F32), compiler_params=_params(2),
    )(a, *bs)


def _silu(x):
    return x * (1.0 / (1.0 + jnp.exp(-x)))


def _ada_forward(c_all, ada_w, ada_b_cols):
    n_cols = ada_w.shape[2]
    tc = 512

    def body(c_ref, w_ref, b_ref, o_ref):
        ca = _silu(c_ref[...]).astype(MXU_DTYPE)
        o_ref[...] = _dot(ca, w_ref[...].astype(MXU_DTYPE)) + b_ref[...]

    return pl.pallas_call(
        body, name="ada_forward", grid=(DEPTH, n_cols // tc),
        in_specs=[pl.BlockSpec((16, D), lambda l, j: (0, 0)), pl.BlockSpec((None, D, tc), lambda l, j: (l, 0, j)),
                  pl.BlockSpec((None, 1, tc), lambda l, j: (l, 0, j))],
        out_specs=pl.BlockSpec((None, 16, tc), lambda l, j: (l, 0, j)),
        out_shape=jax.ShapeDtypeStruct((DEPTH, 16, n_cols), F32),
        compiler_params=_params(2),
    )(c_all, ada_w, ada_b_cols)


def _ada_backward(c_all, dmod_cols):
    n_cols = dmod_cols.shape[2]
    tc = 512

    def body(c_ref, d_ref, o_ref):
        ca = _silu(c_ref[...]).astype(MXU_DTYPE)
        o_ref[...] = _dot_tn(ca, d_ref[...].astype(MXU_DTYPE))

    return pl.pallas_call(
        body, name="ada_backward", grid=(DEPTH, n_cols // tc),
        in_specs=[pl.BlockSpec((16, D), lambda l, j: (0, 0)), pl.BlockSpec((None, 16, tc), lambda l, j: (l, 0, j))],
        out_specs=pl.BlockSpec((None, D, tc), lambda l, j: (l, 0, j)),
        out_shape=jax.ShapeDtypeStruct((DEPTH, D, n_cols), F32),
        compiler_params=_params(2),
    )(c_all, dmod_cols)


def _adamw(w, g, m, v, name):
    R, C = w.shape
    tr = R
    for cand in (512, 256, 128, 64, 32, 16, 8):
        if R % cand == 0 and cand * C * 4 <= 2 ** 21:
            tr = cand
            break
    c1 = 1.0 - ADAM_B1 ** ADAM_STEP
    c2 = 1.0 - ADAM_B2 ** ADAM_STEP

    def body(w_ref, g_ref, m_ref, v_ref, d_ref, mo_ref, vo_ref):
        gg = g_ref[...]
        mn = ADAM_B1 * m_ref[...] + (1.0 - ADAM_B1) * gg
        vn = ADAM_B2 * v_ref[...] + (1.0 - ADAM_B2) * (gg * gg)
        mo_ref[...] = mn
        vo_ref[...] = vn
        d_ref[...] = -ADAM_LR * ((mn / c1) / (jnp.sqrt(vn / c2) + ADAM_EPS) + ADAM_WD * w_ref[...])

    tile = pl.BlockSpec((tr, C), lambda i: (i, 0))
    return pl.pallas_call(
        body, name=name, grid=(R // tr,), in_specs=[tile] * 4, out_specs=[tile] * 3,
        out_shape=[jax.ShapeDtypeStruct((R, C), F32)] * 3, compiler_params=_params(1),
    )(w, g, m, v)


def _position():
    x, y, c = lax.axis_index("x"), lax.axis_index("y"), lax.axis_index("c")
    other_chips = [(1 - x, y), (x, 1 - y), (1 - x, 1 - y)]
    return x, y, c, other_chips


def _all_gather8(block, name):
    R, C = block.shape

    def body(x_ref, out_ref, send_sems, recv_sems, local_sem):
        x, y, c, chips = _position()
        me, sibling = (x, y, c), (x, y, 1 - c)

        def rows(px, py, pc):
            return out_ref.at[pl.ds((4 * px + 2 * py + pc) * R, R), :]

        def copy(k, blk, to, src=None):
            return pltpu.make_async_remote_copy(
                src_ref=rows(*blk) if src is None else src, dst_ref=rows(*blk),
                send_sem=send_sems.at[k], recv_sem=recv_sems.at[k], device_id=to, device_id_type=MESH)

        mine = pltpu.make_async_copy(x_ref, rows(*me), local_sem)
        mine.start()
        first = [copy(0, me, sibling, src=x_ref)]
        first += [copy(1 + j, me, (*chip, c), src=x_ref) for j, chip in enumerate(chips)]
        for cp in first:
            cp.start()
        passed = [copy(4 + j, (*chip, c), sibling) for j, chip in enumerate(chips)]
        for j, chip in enumerate(chips):
            copy(1 + j, (*chip, c), me).wait_recv()
            passed[j].start()
        copy(0, sibling, me).wait_recv()
        for j, chip in enumerate(chips):
            copy(4 + j, (*chip, 1 - c), me).wait_recv()
        for cp in first + passed:
            cp.wait_send()
        mine.wait()

    return pl.pallas_call(
        body, name=name, out_shape=jax.ShapeDtypeStruct((N_DEV * R, C), block.dtype),
        in_specs=[pl.BlockSpec(memory_space=pltpu.VMEM)], out_specs=ANY,
        scratch_shapes=[pltpu.SemaphoreType.DMA((7,)), pltpu.SemaphoreType.DMA((7,)), pltpu.SemaphoreType.DMA(())],
    )(block)


def _gather_flat(vec, name):
    n = vec.shape[0]
    padded = -(-n // 1024) * 1024
    block = jnp.pad(vec, (0, padded - n)).reshape(8, padded // 8)
    out = _all_gather8(block, name)
    return out.reshape(N_DEV, padded)[:, :n]


_SHARD_KINDS = ("cols", "rows", "cols", "rows")


def _shard_of(ref, shape, kind, chip):
    m, n = shape
    return ref.at[:, pl.ds(chip * (n // 4), n // 4)] if kind == "cols" else ref.at[pl.ds(chip * (m // 4), m // 4), :]


def _half_of(ref, shape, kind, h):
    m, n = shape
    return ref.at[pl.ds(h * (m // 2), m // 2), :] if kind == "cols" else ref.at[:, pl.ds(h * (n // 2), n // 2)]


def _half_shape(shape, kind):
    m, n = shape
    return (m // 2, n) if kind == "cols" else (m, n // 2)


def _shard_shape(shape, kind):
    m, n = shape
    return (m, n // 4) if kind == "cols" else (m // 4, n)


def _remote(src, dst, k, to, send_sems, recv_sems):
    return pltpu.make_async_remote_copy(src_ref=src, dst_ref=dst, send_sem=send_sems.at[k], recv_sem=recv_sems.at[k],
                                        device_id=to, device_id_type=MESH)


def _gather_exchange(shards, full, kind, layer):
    shape = full.shape[1:]

    def copies(xin, xout, send_sems, recv_sems):
        x, y, c, chips = _position()
        my_chip = 2 * x + y
        own = xin[0].at[layer]
        place = lambda chip: _shard_of(xout[0].at[layer], shape, kind, chip)
        peers = [(x, y, 1 - c)] + [(*chip, c) for chip in chips]
        sources = [my_chip] + [2 * cx + cy for cx, cy in chips]
        sends = [_remote(own, place(my_chip), k, peer, send_sems, recv_sems) for k, peer in enumerate(peers)]
        arrivals = [_remote(place(s), place(s), k, (x, y, c), send_sems, recv_sems) for k, s in enumerate(sources)]
        return sends, arrivals

    return _Exchange([shards, full], [1], [], 4, copies)


def _swap_exchange(grads):
    shapes = [g.shape for g in grads]
    fresh = [jax.ShapeDtypeStruct(_half_shape(s, k), F32) for s, k in zip(shapes, _SHARD_KINDS)]

    def copies(xin, xout, send_sems, recv_sems):
        x, y, c, _ = _position()
        sends = [_remote(_half_of(xin[a], shapes[a], _SHARD_KINDS[a], 1 - c), xout[a], a, (x, y, 1 - c),
                         send_sems, recv_sems) for a in range(len(grads))]
        arrivals = [_remote(xout[a], xout[a], a, (x, y, c), send_sems, recv_sems) for a in range(len(grads))]
        return sends, arrivals

    return _Exchange(list(grads), [], fresh, len(grads), copies)


def _scatter_exchange(partials):
    n_w = len(partials)
    shapes = [p.shape for p in partials]
    fresh = [jax.ShapeDtypeStruct((3,) + _shard_shape(s, k), p.dtype) for s, k, p in zip(shapes, _SHARD_KINDS, partials)]

    def copies(xin, xout, send_sems, recv_sems):
        x, y, c, chips = _position()
        sends, arrivals = [], []
        for j, (cx, cy) in enumerate(chips):
            for a in range(n_w):
                src = _shard_of(xin[a], shapes[a], _SHARD_KINDS[a], 2 * cx + cy)
                sends.append(_remote(src, xout[a].at[j], j * n_w + a, (cx, cy, c), send_sems, recv_sems))
                arrivals.append(_remote(xout[a].at[j], xout[a].at[j], j * n_w + a, (x, y, c), send_sems, recv_sems))
        return sends, arrivals

    return _Exchange(list(partials), [], fresh, 3 * n_w, copies)


def _share_exchange(reduced, layer):
    n_w = len(reduced)
    shapes = [r.shape[1:] for r in reduced]

    def copies(xin, xout, send_sems, recv_sems):
        x, y, c, _ = _position()
        half = lambda a, h: _half_of(xout[a].at[layer], shapes[a], _SHARD_KINDS[a], h)
        sends = [_remote(half(a, c), half(a, c), a, (x, y, 1 - c), send_sems, recv_sems) for a in range(n_w)]
        arrivals = [_remote(half(a, 1 - c), half(a, 1 - c), a, (x, y, c), send_sems, recv_sems) for a in range(n_w)]
        return sends, arrivals

    return _Exchange(list(reduced), list(range(n_w)), [], n_w, copies)


def _chip_partial(pos, grad, theirs, kind, name):
    M, N = grad.shape
    if kind == "cols":
        tm = 256
        steps = M // 2 // tm
        block, g_map = (tm, N), (lambda i, pos: (pos[0] * steps + i, 0))
    else:
        tm = M // 4
        steps = 4
        block, g_map = (tm, N // 2), (lambda i, pos: (i, pos[0]))

    def body(pos_ref, g_ref, t_ref, o_ref):
        o_ref[...] = (g_ref[...] + t_ref[...]).astype(WIRE_DTYPE)

    grid_spec = pltpu.PrefetchScalarGridSpec(
        num_scalar_prefetch=1, grid=(steps,),
        in_specs=[pl.BlockSpec(block, g_map), pl.BlockSpec(block, lambda i, pos: (i, 0))],
        out_specs=pl.BlockSpec(block, lambda i, pos: (i, 0)))
    return pl.pallas_call(
        body, name=name, grid_spec=grid_spec, out_shape=jax.ShapeDtypeStruct(_half_shape((M, N), kind), WIRE_DTYPE),
        compiler_params=_params(1),
    )(pos, grad, theirs)


def _reduce_shard(pos, grad, theirs, received, reduced, kind, layer, name):
    M, N = grad.shape
    if kind == "cols":
        tm = min(M // 2, 512)
        steps = M // 2 // tm
        block = (tm, N // 4)
        g_map = lambda i, pos: (pos[0] * steps + i, pos[1])
        t_map = lambda i, pos: (i, pos[1])
        o_map = lambda i, pos: (layer, pos[0] * steps + i, 0)
    else:
        steps = 1
        block = (M // 4, N // 2)
        g_map = lambda i, pos: (pos[1], pos[0])
        t_map = lambda i, pos: (pos[1], 0)
        o_map = lambda i, pos: (layer, 0, pos[0])
    out_shape = (DEPTH,) + _shard_shape((M, N), kind)

    def body(pos_ref, g_ref, t_ref, r0_ref, r1_ref, r2_ref, *rest):
        o_ref = rest[-1]
        chip = pos_ref[1]
        own = g_ref[...] + t_ref[...]
        r = [r0_ref[...].astype(F32), r1_ref[...].astype(F32), r2_ref[...].astype(F32)]
        total = None
        for s in range(N_CHIPS):
            rel = jnp.bitwise_xor(chip, s)
            term = jnp.where(rel == 0, own, jnp.where(rel == 2, r[0], jnp.where(rel == 1, r[1], r[2])))
            total = term if total is None else total + term
        o_ref[...] = total

    r_spec = lambda j: pl.BlockSpec((None,) + block, lambda i, pos: (j, i, 0))
    in_specs = [pl.BlockSpec(block, g_map), pl.BlockSpec(block, t_map), r_spec(0), r_spec(1), r_spec(2)]
    args = [pos, grad, theirs, received, received, received]
    aliases = {}
    if reduced is not None:
        in_specs.append(ANY)
        args.append(reduced)
        aliases = {6: 0}
    grid_spec = pltpu.PrefetchScalarGridSpec(
        num_scalar_prefetch=1, grid=(steps,), in_specs=in_specs, out_specs=pl.BlockSpec((None,) + block, o_map))
    return pl.pallas_call(
        body, name=name, grid_spec=grid_spec, out_shape=jax.ShapeDtypeStruct(out_shape, F32),
        input_output_aliases=aliases, compiler_params=_params(1),
    )(*args)


def _sum_devices(gathered):
    R8, C = gathered.shape
    R = R8 // N_DEV
    lanes = C // 128
    tc = 128 * max(k for k in range(1, lanes + 1) if lanes % k == 0 and k * 128 * R8 * 4 <= 2 ** 22)

    def body(g_ref, o_ref):
        total = g_ref[0:R, :]
        for d in range(1, N_DEV):
            total = total + g_ref[d * R:(d + 1) * R, :]
        o_ref[...] = total

    return pl.pallas_call(
        body, name="sum_devices", grid=(C // tc,),
        in_specs=[pl.BlockSpec((R8, tc), lambda j: (0, j))], out_specs=pl.BlockSpec((R, tc), lambda j: (0, j)),
        out_shape=jax.ShapeDtypeStruct((R, C), F32), compiler_params=_params(1),
    )(gathered)


_SMALL = ("pool_w", "pool_scale", "sgu_ln_g", "sgu_ln_b", "sgu_w", "sgu_b", "ln1_g", "ln1_b", "conv_w",
          "conv_b", "ln2_g", "ln2_b", "ada_b")
_WEIGHTS = ("ada_w", "ada_b", "w_in", "pool_w", "pool_scale", "sgu_ln_g", "sgu_ln_b", "sgu_w", "sgu_b", "w_out",
            "ln1_g", "ln1_b", "w_up", "conv_w", "conv_b", "w_down", "ln2_g", "ln2_b")


def kernel(x, c, ada_w, ada_b, w_in, pool_w, pool_scale, sgu_ln_g, sgu_ln_b, sgu_w, sgu_b, w_out, ln1_g, ln1_b, w_up, conv_w, conv_b, w_down, ln2_g, ln2_b, loss_target, m_ada_w, m_ada_b, m_w_in, m_pool_w, m_pool_scale, m_sgu_ln_g, m_sgu_ln_b, m_sgu_w, m_sgu_b, m_w_out, m_ln1_g, m_ln1_b, m_w_up, m_conv_w, m_conv_b, m_w_down, m_ln2_g, m_ln2_b, v_ada_w, v_ada_b, v_w_in, v_pool_w, v_pool_scale, v_sgu_ln_g, v_sgu_ln_b, v_sgu_w, v_sgu_b, v_w_out, v_ln1_g, v_ln1_b, v_w_up, v_conv_w, v_conv_b, v_w_down, v_ln2_g, v_ln2_b):
    weights = dict(ada_w=ada_w, ada_b=ada_b, w_in=w_in, pool_w=pool_w, pool_scale=pool_scale, sgu_ln_g=sgu_ln_g,
                   sgu_ln_b=sgu_ln_b, sgu_w=sgu_w, sgu_b=sgu_b, w_out=w_out, ln1_g=ln1_g, ln1_b=ln1_b, w_up=w_up,
                   conv_w=conv_w, conv_b=conv_b, w_down=w_down, ln2_g=ln2_g, ln2_b=ln2_b)
    mom_m = dict(ada_w=m_ada_w, ada_b=m_ada_b, w_in=m_w_in, pool_w=m_pool_w, pool_scale=m_pool_scale,
                 sgu_ln_g=m_sgu_ln_g, sgu_ln_b=m_sgu_ln_b, sgu_w=m_sgu_w, sgu_b=m_sgu_b, w_out=m_w_out,
                 ln1_g=m_ln1_g, ln1_b=m_ln1_b, w_up=m_w_up, conv_w=m_conv_w, conv_b=m_conv_b, w_down=m_w_down,
                 ln2_g=m_ln2_g, ln2_b=m_ln2_b)
    mom_v = dict(ada_w=v_ada_w, ada_b=v_ada_b, w_in=v_w_in, pool_w=v_pool_w, pool_scale=v_pool_scale,
                 sgu_ln_g=v_sgu_ln_g, sgu_ln_b=v_sgu_ln_b, sgu_w=v_sgu_w, sgu_b=v_sgu_b, w_out=v_w_out,
                 ln1_g=v_ln1_g, ln1_b=v_ln1_b, w_up=v_w_up, conv_w=v_conv_w, conv_b=v_conv_b, w_down=v_w_down,
                 ln2_g=v_ln2_g, ln2_b=v_ln2_b)

    ix, iy, ic = lax.axis_index("x"), lax.axis_index("y"), lax.axis_index("c")
    chip = 2 * ix + iy
    dev = 4 * ix + 2 * iy + ic
    pos = jnp.stack([ic, chip]).astype(jnp.int32)
    xs = x[0]
    target = loss_target[0]
    ff_shard = conv_w.shape[2]
    mod_shard = ada_w.shape[2]

    first = _gather_flat(jnp.concatenate([c.reshape(-1), conv_w.reshape(-1)]), "gather_c_conv")
    c_all = jnp.pad(first[:, :D], ((0, 8), (0, 0)))
    conv_parts = first[0::2, D:].reshape(N_CHIPS, DEPTH, 3, ff_shard)
    conv_full = jnp.transpose(conv_parts, (1, 2, 0, 3)).reshape(DEPTH, 3, D_FF)
    ada_b_cols = lax.dynamic_slice_in_dim(ada_b, chip * mod_shard, mod_shard, axis=1).reshape(DEPTH, 1, mod_shard)
    mod_part = _ada_forward(c_all, ada_w, ada_b_cols)[:, :8, :]
    mod_all = _gather_flat(mod_part.reshape(-1), "gather_mod").reshape(N_DEV, DEPTH, 8, mod_shard)
    mod_mine = lax.dynamic_index_in_dim(mod_all[0::2], dev, axis=2, keepdims=False)
    mod = jnp.transpose(mod_mine, (1, 0, 2)).reshape(DEPTH, 6, 1, D)

    sh_in, sh_out, sh_up, sh_down = (w.astype(WIRE_DTYPE) for w in (w_in, w_out, w_up, w_down))
    wf_in = lax.empty((DEPTH, D, D_IN), WIRE_DTYPE)
    wf_out = lax.empty((DEPTH, D, D), WIRE_DTYPE)
    wf_up = lax.empty((DEPTH, D, 2 * D_FF), WIRE_DTYPE)
    wf_down = lax.empty((DEPTH, D_FF, D), WIRE_DTYPE)
    (wf_in,) = _run_exchange(_gather_exchange(sh_in, wf_in, "cols", 0), "gather_w_in_l0")

    tri = jnp.tril(jnp.ones((CHUNK, CHUNK), dtype=bool))
    ones_row = jnp.ones((1, D), F32)
    zeros_row = jnp.zeros((1, D), F32)
    row = lambda a, l: a[l].reshape(1, -1)

    saved = []
    xn, lg, lb = xs, ones_row, zeros_row
    h = _modulate_input(xs, mod[0, 0], mod[0, 1])
    for l in range(DEPTH):
        shift1, scale1, gate1, shift2, scale2, gate2 = (mod[l, k] for k in range(6))
        nxt = min(l + 1, DEPTH - 1)
        pw = pool_w[l].astype(MXU_DTYPE)
        wm = jnp.where(tri[None], sgu_w[l], 0.0).astype(MXU_DTYPE)
        bias_full = jnp.repeat(jnp.transpose(sgu_b[l]), GROUP, axis=1)
        lg1, lb1 = row(ln1_g, l), row(ln1_b, l)
        proj, (wf_out,) = _matmul(h, wf_in, l, D_IN, STORE_DTYPE, f"f1_in_proj_l{l}",
                                  exchange=_gather_exchange(sh_out, wf_out, "rows", l))
        (mix, f1, xhat1, rstd1, h2, gel), (wf_up,) = _f2_mixers(
            proj, xn, lg, lb, gate1, pw, row(pool_scale, l), row(sgu_ln_g, l), row(sgu_ln_b, l), wm, bias_full,
            wf_out, lg1, lb1, shift2, scale2, l, exchange=_gather_exchange(sh_up, wf_up, "cols", l))
        up, (wf_down,) = _matmul(h2, wf_up, l, D_FF, STORE_DTYPE, f"f3_up_proj_l{l}",
                                 exchange=_gather_exchange(sh_down, wf_down, "rows", l))
        (act, f2, xhat2, rstd2, h_next, ge, dge), filled = _f4_ffn(
            up, xhat1, lg1, lb1, gate2, conv_full[l], row(conv_b, l), wf_down, row(ln2_g, l), row(ln2_b, l),
            mod[nxt, 0], mod[nxt, 1], l,
            exchange=_gather_exchange(sh_in, wf_in, "cols", l + 1) if l + 1 < DEPTH else None)
        if filled:
            (wf_in,) = filled
        saved.append(dict(xn=xn, lg=lg, lb=lb, h=h, proj=proj, mix=mix, f1=f1, xhat1=xhat1, rstd1=rstd1, h2=h2,
                          up=up, act=act, f2=f2, xhat2=xhat2, rstd2=rstd2, pw=pw, wm=wm, bias_full=bias_full,
                          ge=ge, dge=dge, gel=gel))
        xn, lg, lb, h = xhat2, row(ln2_g, l), row(ln2_b, l), h_next

    loss_part, d_out = _loss_head(xn, lg, lb, target)
    loss = lax.psum(loss_part[0, 0], ("x", "y", "c"))

    tags = ("in", "out", "up", "down")
    small = {n: [None] * DEPTH for n in _SMALL}
    dmod = [[None] * 6 for _ in range(DEPTH)]
    below = None
    ready = None
    reduced = [None] * 4

    def partial_sums(layer, full, theirs):
        return [_chip_partial(pos, g, t, kind, f"chip_partial_{tag}_l{layer}")
                for g, t, kind, tag in zip(full, theirs, _SHARD_KINDS, tags)]

    def chip_sums(layer, full, theirs, received):
        return [_reduce_shard(pos, g, t, r, out, kind, layer, f"reduce_shard_{tag}_l{layer}")
                for g, t, r, out, kind, tag in zip(full, theirs, received, reduced, _SHARD_KINDS, tags)]

    for l in reversed(range(DEPTH)):
        sv = saved[l]
        scale1, gate1, scale2, gate2 = mod[l, 1], mod[l, 2], mod[l, 4], mod[l, 5]
        lg1, lb1 = row(ln1_g, l), row(ln1_b, l)
        outs, theirs = _b_ln(d_out, sv["xhat2"], sv["rstd2"], sv["f2"], row(ln2_g, l), row(ln2_b, l), gate2,
                             f"b1_ln2_l{l}", modulated=below,
                             exchange=_swap_exchange(ready[1]) if ready else None)
        df2, dres2, dlg2, dlb2, dmod[l][5] = outs[:5]
        if below is not None:
            dmod[l + 1][1], dmod[l + 1][0] = outs[5], outs[6]
        partials = partial_sums(ready[0], ready[1], theirs) if ready else None
        (dup, dcw, dcb), received = _b2_ffn(df2, wf_down, sv["up"], sv["ge"], sv["dge"], conv_full[l], l,
                                            exchange=_scatter_exchange(partials) if ready else None)
        if ready:
            reduced = chip_sums(ready[0], ready[1], theirs, received)
        g_down = _weight_grad(sv["act"], [df2], 512, f"wg_down_l{l}", rows=2 * TS_TN)
        dh2, shared = _matmul_nt(dup, wf_up, l, D_FF, f"b3_up_l{l}",
                                 exchange=_share_exchange(reduced, ready[0]) if ready else None)
        if ready:
            reduced = shared
        g_up = _weight_grad(sv["h2"], [dup], 1408, f"wg_up_l{l}", rows=2 * TS_TN)
        (df1, dres1, dlg1, dlb1, dmod[l][2], dmod[l][4], dmod[l][3]), _ = _b_ln(
            dh2, sv["xhat1"], sv["rstd1"], sv["f1"], lg1, lb1, gate1, f"b4_ln1_l{l}", modulated=(dres2, scale2))
        dproj, dpw, dps, dsg, dsb, dwm, dbias = _b5_mixers(
            df1, wf_out, sv["proj"], sv["gel"], sv["pw"], row(pool_scale, l), row(sgu_ln_g, l), row(sgu_ln_b, l), sv["wm"],
            sv["bias_full"], l)
        g_out = _weight_grad(sv["mix"], [df1], D, f"wg_out_l{l}", rows=2 * TS_TN)
        d_out, _ = _matmul_nt(dproj, wf_in, l, D_IN, f"b6_in_l{l}")
        g_in = _weight_grad(sv["h"], [dproj], 768, f"wg_in_l{l}", rows=2 * TS_TN)
        ready = (l, [g_in, g_out, g_up, g_down])
        below = (dres1, scale1)
        small["pool_w"][l], small["pool_scale"][l] = dpw, dps[0]
        small["sgu_ln_g"][l], small["sgu_ln_b"][l], small["sgu_w"][l] = dsg[0], dsb[0], dwm
        small["sgu_b"][l] = jnp.transpose(jnp.sum(dbias.reshape(CHUNK, HEADS, GROUP), axis=2))
        small["ln1_g"][l], small["ln1_b"][l] = dlg1[0], dlb1[0]
        small["conv_w"][l], small["conv_b"][l] = dcw, dcb[0]
        small["ln2_g"][l], small["ln2_b"][l] = dlg2[0], dlb2[0]
    (grad_x2d, dmod[0][1], dmod[0][0]), theirs = _input_grad(d_out, below[0], below[1], xs,
                                                              exchange=_swap_exchange(ready[1]))
    grad_x = grad_x2d[None]
    small["ada_b"] = [jnp.concatenate([part[0] for part in dmod[l]]) for l in range(DEPTH)]

    names = _SMALL
    flat = jnp.concatenate([jnp.stack(small[n]).reshape(-1) for n in names])
    n_small = flat.shape[0]
    padded = -(-n_small // 1024) * 1024
    gathered = _all_gather8(jnp.pad(flat, (0, padded - n_small)).reshape(8, padded // 8), "gather_small_grads")
    summed = _sum_devices(gathered).reshape(-1)[:n_small]
    grads = {}
    offset = 0
    for n in names:
        size = math.prod(weights[n].shape[1:]) * DEPTH if n != "conv_w" else DEPTH * 3 * D_FF
        grads[n] = summed[offset:offset + size]
        offset += size
    grads["conv_w"] = lax.dynamic_slice_in_dim(grads["conv_w"].reshape(DEPTH, 3, D_FF), chip * ff_shard, ff_shard,
                                               axis=2)
    for n in names:
        grads[n] = grads[n].reshape(weights[n].shape)
    dmod_all = gathered.reshape(N_DEV, padded)[:, n_small - DEPTH * 6 * D:n_small].reshape(N_DEV, DEPTH, 6 * D)
    dmod_cols = lax.dynamic_slice_in_dim(jnp.transpose(dmod_all, (1, 0, 2)), chip * mod_shard, mod_shard, axis=2)
    grads["ada_w"] = _ada_backward(c_all, jnp.pad(dmod_cols, ((0, 0), (0, 8), (0, 0))))

    partials = partial_sums(0, ready[1], theirs)
    received = _run_exchange(_scatter_exchange(partials), "scatter_partials_l0")
    reduced = chip_sums(0, ready[1], theirs, received)
    grads["w_in"], grads["w_out"], grads["w_up"], grads["w_down"] = _run_exchange(
        _share_exchange(reduced, 0), "share_reduced_l0")

    delta, new_m, new_v = {}, {}, {}
    for n in ("ada_w", "w_in", "w_out", "w_up", "w_down"):
        shape = weights[n].shape
        two_d = (shape[0] * shape[1], shape[2])
        d_, m_, v_ = _adamw(weights[n].reshape(two_d), grads[n].reshape(two_d), mom_m[n].reshape(two_d),
                            mom_v[n].reshape(two_d), f"adamw_{n}")
        delta[n], new_m[n], new_v[n] = d_.reshape(shape), m_.reshape(shape), v_.reshape(shape)
    sizes = [math.prod(weights[n].shape) for n in names]
    total = sum(sizes)
    padded = -(-total // 1024) * 1024
    pack = lambda d: jnp.pad(jnp.concatenate([d[n].reshape(-1) for n in names]), (0, padded - total)).reshape(8, -1)
    v_pack = jnp.pad(jnp.concatenate([mom_v[n].reshape(-1) for n in names]), (0, padded - total),
                     constant_values=1.0).reshape(8, -1)
    d_, m_, v_ = _adamw(pack(weights), pack(grads), pack(mom_m), v_pack, "adamw_small")
    offset = 0
    for n, size in zip(names, sizes):
        for src, dst in ((d_, delta), (m_, new_m), (v_, new_v)):
            dst[n] = src.reshape(-1)[offset:offset + size].reshape(weights[n].shape)
        offset += size

    return (loss, grad_x, *[grads[n] for n in _WEIGHTS], *[delta[n] for n in _WEIGHTS],
            *[new_m[n] for n in _WEIGHTS], *[new_v[n] for n in _WEIGHTS])
```

```python
import functools
import math

import jax
import jax.numpy as jnp
from jax import lax
from jax.experimental import pallas as pl
from jax.experimental.pallas import tpu as pltpu

F32 = jnp.float32
MXU_DTYPE = jnp.bfloat16
WIRE_DTYPE = jnp.bfloat16

DEPTH = 4
D = 1024
D_POOL = 512
D_SGU = 512
N_GROUPS = 4
GROUP = 128
POOL_WINDOWS = (2, 4, 8, 16)
POOL_HALO = 16
CHUNK = 128
HEADS = 4
D_IN = D_POOL + 2 * D_SGU
D_FF = 2816
CONV_HALO = 16
STORE_DTYPE = jnp.bfloat16
GELU_DTYPE = jnp.bfloat16
N_CHIPS = 4
N_DEV = 8
ALPHA = (2.0 * DEPTH) ** 0.25
LN_EPS = 1e-5
ADAM_LR, ADAM_B1, ADAM_B2, ADAM_EPS, ADAM_WD, ADAM_STEP = 0.001, 0.9, 0.999, 1e-08, 0.01, 10

TS_PROJ = 512
TS_MM = 1024
TS_MIX = 512
TS_FF = 256
TS_TN = 1024
VMEM_LIMIT = 52 * 2 ** 20

MESH = pl.DeviceIdType.MESH
ANY = pl.BlockSpec(memory_space=pl.ANY)

_GELU_K0 = math.sqrt(2.0 / math.pi)
_GELU_K1 = 0.044715


def _params(n_axes):
    return pltpu.CompilerParams(dimension_semantics=("arbitrary",) * n_axes, vmem_limit_bytes=VMEM_LIMIT)


class _Exchange:
    def __init__(self, inputs, inplace, fresh, n_copies, copies):
        self.inputs, self.inplace, self.fresh, self.n_copies, self.copies = inputs, inplace, fresh, n_copies, copies

    def out_shapes(self):
        return [jax.ShapeDtypeStruct(self.inputs[i].shape, self.inputs[i].dtype) for i in self.inplace] + list(self.fresh)

    def semaphores(self):
        return [pltpu.SemaphoreType.DMA((self.n_copies,)), pltpu.SemaphoreType.DMA((self.n_copies,))]

    def aliases(self, first_input, first_output):
        return {first_input + i: first_output + k for k, i in enumerate(self.inplace)}


def _call(body, *, name, grid, in_specs, out_specs, out_shape, args, scratch_shapes=(), exchange=None):
    n_axes = len(grid)
    in_specs, out_specs, out_shape, scratch_shapes = list(in_specs), list(out_specs), list(out_shape), list(scratch_shapes)
    if exchange is None:
        outs = pl.pallas_call(body, name=name, grid=grid, in_specs=in_specs, out_specs=out_specs, out_shape=out_shape,
                              scratch_shapes=scratch_shapes, compiler_params=_params(n_axes))(*args)
        return list(outs), []
    n_in, n_out, n_scr = len(in_specs), len(out_shape), len(scratch_shapes)
    x_out = exchange.out_shapes()
    n_xin, n_xout = len(exchange.inputs), len(x_out)

    def hosted(*refs):
        ins, xin = refs[:n_in], refs[n_in:n_in + n_xin]
        o = n_in + n_xin
        outs, xout = refs[o:o + n_out], refs[o + n_out:o + n_out + n_xout]
        s = o + n_out + n_xout
        scr, (send_sems, recv_sems) = refs[s:s + n_scr], refs[s + n_scr:]
        first = functools.reduce(jnp.logical_and, [pl.program_id(d) == 0 for d in range(n_axes)])
        last = functools.reduce(jnp.logical_and, [pl.program_id(d) == pl.num_programs(d) - 1 for d in range(n_axes)])

        @pl.when(first)
        def _():
            for cp in exchange.copies(xin, xout, send_sems, recv_sems)[0]:
                cp.start()

        body(*ins, *outs, *scr)

        @pl.when(last)
        def _():
            sends, arrivals = exchange.copies(xin, xout, send_sems, recv_sems)
            for cp in arrivals:
                cp.wait_recv()
            for cp in sends:
                cp.wait_send()

    results = pl.pallas_call(
        hosted, name=name, grid=grid, in_specs=in_specs + [ANY] * n_xin, out_specs=out_specs + [ANY] * n_xout,
        out_shape=out_shape + x_out, scratch_shapes=scratch_shapes + exchange.semaphores(),
        input_output_aliases=exchange.aliases(n_in, n_out), compiler_params=_params(n_axes),
    )(*args, *exchange.inputs)
    return list(results[:n_out]), list(results[n_out:])


def _run_exchange(exchange, name):
    x_out = exchange.out_shapes()
    n_xin, n_xout = len(exchange.inputs), len(x_out)

    def body(*refs):
        xin, xout = refs[:n_xin], refs[n_xin:n_xin + n_xout]
        send_sems, recv_sems = refs[n_xin + n_xout:]
        sends, arrivals = exchange.copies(xin, xout, send_sems, recv_sems)
        for cp in sends:
            cp.start()
        for cp in arrivals:
            cp.wait_recv()
        for cp in sends:
            cp.wait_send()

    return list(pl.pallas_call(
        body, name=name, in_specs=[ANY] * n_xin, out_specs=[ANY] * n_xout, out_shape=x_out,
        scratch_shapes=exchange.semaphores(), input_output_aliases=exchange.aliases(0, 0),
    )(*exchange.inputs))


def _dot(a, b):
    return jnp.dot(a, b, preferred_element_type=F32)


def _dot_nt(a, b):
    return lax.dot_general(a, b, (((1,), (1,)), ((), ())), preferred_element_type=F32)


def _dot_tn(a, b):
    return lax.dot_general(a, b, (((0,), (0,)), ((), ())), preferred_element_type=F32)


def _gelu(x):
    x2 = x * x
    t = jnp.tanh(x * (x2 * (_GELU_K0 * _GELU_K1) + _GELU_K0))
    cdf = 0.5 * t + 0.5
    dg = cdf + (x * (1.0 - t * t)) * (x2 * (1.5 * _GELU_K0 * _GELU_K1) + 0.5 * _GELU_K0)
    return x * cdf, dg


def _colsum(x):
    return jnp.sum(x, axis=0, keepdims=True)


def _row(d):
    return pl.BlockSpec((1, d), lambda *_: (0, 0))


def _full(shape):
    n = len(shape)
    return pl.BlockSpec(shape, lambda *_: (0,) * n)


def _layer_block(shape, layer):
    n = len(shape)
    return pl.BlockSpec((None,) + tuple(shape), lambda *_: (layer,) + (0,) * n)


def _pool_counts(first_row, rows, window):
    t = first_row + lax.broadcasted_iota(jnp.int32, (rows, 1), 0)
    return 1.0 / jnp.minimum(t + 1, window).astype(F32)


def _pool_forward(a_ext, a, first_row, rows):
    out = []
    for g, window in enumerate(POOL_WINDOWS):
        s = a_ext[:, g * GROUP:(g + 1) * GROUP]
        k = 1
        while k < window:
            s = s + pltpu.roll(s, k, 0)
            k *= 2
        inv = _pool_counts(first_row, rows, window)
        out.append(s[POOL_HALO:] * inv - a[:, g * GROUP:(g + 1) * GROUP])
    return out


def _sgu_norm(gv):
    mu = jnp.mean(gv, axis=-1, keepdims=True)
    xc = gv - mu
    var = jnp.mean(xc * xc, axis=-1, keepdims=True)
    rs = lax.rsqrt(var + LN_EPS)
    return xc * rs, rs


def _modulate_input(x, shift, scale):
    S = x.shape[0]
    ts = min(TS_PROJ, S)

    def body(x_ref, sh_ref, sc_ref, h_ref):
        h_ref[...] = (x_ref[...] * (1.0 + sc_ref[...]) + sh_ref[...]).astype(MXU_DTYPE)

    tile = pl.BlockSpec((ts, D), lambda i: (i, 0))
    return pl.pallas_call(
        body, name="modulate_input", grid=(S // ts,), in_specs=[tile, _row(D), _row(D)], out_specs=tile,
        out_shape=jax.ShapeDtypeStruct((S, D), MXU_DTYPE), compiler_params=_params(1),
    )(x, shift, scale)


def _matmul(a, w_all, layer, tn, out_dtype, name, exchange=None):
    S, K = a.shape
    N = w_all.shape[2]
    ts = min(TS_MM, S)

    def body(a_ref, w_ref, o_ref):
        o_ref[...] = _dot(a_ref[...], w_ref[...]).astype(out_dtype)

    (out,), extra = _call(
        body, name=name, grid=(N // tn, S // ts),
        in_specs=[pl.BlockSpec((ts, K), lambda j, i: (i, 0)), pl.BlockSpec((None, K, tn), lambda j, i: (layer, 0, j))],
        out_specs=[pl.BlockSpec((ts, tn), lambda j, i: (i, j))],
        out_shape=[jax.ShapeDtypeStruct((S, N), out_dtype)], args=(a, w_all), exchange=exchange)
    return out, extra


def _matmul_nt(a, w_all, layer, tk, name, exchange=None):
    S, K = a.shape
    N = w_all.shape[1]
    ts = min(TS_MM, S)
    n_k = K // tk

    def body(a_ref, w_ref, o_ref, acc):
        k = pl.program_id(1)
        part = _dot_nt(a_ref[...], w_ref[...])
        if n_k == 1:
            o_ref[...] = part
        else:
            @pl.when(k == 0)
            def _():
                acc[...] = part

            @pl.when(jnp.logical_and(k > 0, k < n_k - 1))
            def _():
                acc[...] += part

            @pl.when(k == n_k - 1)
            def _():
                o_ref[...] = acc[...] + part

    (out,), extra = _call(
        body, name=name, grid=(S // ts, n_k),
        in_specs=[pl.BlockSpec((ts, tk), lambda i, k: (i, k)), pl.BlockSpec((None, N, tk), lambda i, k: (layer, 0, k))],
        out_specs=[pl.BlockSpec((ts, N), lambda i, k: (i, 0))],
        out_shape=[jax.ShapeDtypeStruct((S, N), F32)],
        scratch_shapes=[pltpu.VMEM((ts, N), F32)], args=(a, w_all), exchange=exchange)
    return out, extra


def _f2_mixers(proj, xn, lg, lb, gate, pool_w, pool_scale, sg, sb, wm, bias_full, w_out_all, lg1, lb1, shift2, scale2,
               layer, exchange=None):
    S = xn.shape[0]
    ts = min(TS_MIX, S)
    n_chunks = ts // CHUNK
    halo_blocks = ts // POOL_HALO

    def body(p_ref, halo_ref, xn_ref, lg_ref, lb_ref, gate_ref, pw_ref, ps_ref, sg_ref, sb_ref, wm_ref,
             bias_ref, wo_ref, lg1_ref, lb1_ref, sh2_ref, sc2_ref, mix_ref, f_ref, xh_ref, rs_ref, h2_ref, gel_ref,
             z_scr):
        i = pl.program_id(0)
        a = p_ref[:, 0:D_POOL].astype(F32)
        u = p_ref[:, D_POOL:D_POOL + D_SGU].astype(GELU_DTYPE)
        v = p_ref[:, D_POOL + D_SGU:D_IN].astype(GELU_DTYPE)
        halo = halo_ref[...].astype(F32) * (i > 0).astype(F32)
        a_ext = jnp.concatenate([halo, a], axis=0)
        pooled = _pool_forward(a_ext, a, i * ts, ts)
        for g in range(N_GROUPS):
            mixed = _dot(pooled[g].astype(MXU_DTYPE), pw_ref[g])
            mix_ref[:, g * GROUP:(g + 1) * GROUP] = (mixed * ps_ref[:, g * GROUP:(g + 1) * GROUP]).astype(MXU_DTYPE)
        gu, dgu = _gelu(u)
        gv, dgv = _gelu(v)
        for k, part in enumerate((gu, dgu, gv, dgv)):
            gel_ref[:, k * D_SGU:(k + 1) * D_SGU] = part.astype(STORE_DTYPE)
        gu = gu.astype(F32)
        vhat, _ = _sgu_norm(gv.astype(F32))
        vn = (vhat * sg_ref[...] + sb_ref[...]).astype(MXU_DTYPE)
        for h in range(HEADS):
            cols = slice(h * GROUP, (h + 1) * GROUP)
            vn_h = jnp.concatenate([vn[c * CHUNK:(c + 1) * CHUNK, cols] for c in range(n_chunks)], axis=1)
            z_h = _dot(wm_ref[h], vn_h)
            for c in range(n_chunks):
                z_scr[c * CHUNK:(c + 1) * CHUNK, cols] = z_h[:, c * GROUP:(c + 1) * GROUP] + bias_ref[:, cols]
        mix_ref[:, D_POOL:D] = (gu * z_scr[...]).astype(MXU_DTYPE)
        f = _dot(mix_ref[...], wo_ref[...])
        f_ref[...] = f.astype(STORE_DTYPE)
        x = xn_ref[...] * lg_ref[...] + lb_ref[...]
        z1 = ALPHA * x + gate_ref[...] * f
        mu = jnp.mean(z1, axis=-1, keepdims=True)
        zc = z1 - mu
        var = jnp.mean(zc * zc, axis=-1, keepdims=True)
        rs = lax.rsqrt(var + LN_EPS)
        xhat = zc * rs
        xh_ref[...] = xhat
        rs_ref[...] = rs
        x1 = xhat * lg1_ref[...] + lb1_ref[...]
        h2_ref[...] = (x1 * (1.0 + sc2_ref[...]) + sh2_ref[...]).astype(MXU_DTYPE)

    tile = lambda w: pl.BlockSpec((ts, w), lambda i: (i, 0))
    return _call(
        body, name=f"f2_mixers_l{layer}", grid=(S // ts,),
        in_specs=[tile(D_IN),
                  pl.BlockSpec((POOL_HALO, D_POOL), lambda i: (jnp.maximum(i * halo_blocks - 1, 0), 0)),
                  tile(D), _row(D), _row(D), _row(D),
                  _full((N_GROUPS, GROUP, GROUP)), _row(D_POOL), _row(D_SGU), _row(D_SGU),
                  _full((HEADS, CHUNK, CHUNK)), _full((CHUNK, D_SGU)), _layer_block((D, D), layer),
                  _row(D), _row(D), _row(D), _row(D)],
        out_specs=[tile(D), tile(D), tile(D), tile(1), tile(D), tile(4 * D_SGU)],
        out_shape=[jax.ShapeDtypeStruct((S, D), MXU_DTYPE), jax.ShapeDtypeStruct((S, D), STORE_DTYPE),
                   jax.ShapeDtypeStruct((S, D), F32), jax.ShapeDtypeStruct((S, 1), F32),
                   jax.ShapeDtypeStruct((S, D), MXU_DTYPE), jax.ShapeDtypeStruct((S, 4 * D_SGU), STORE_DTYPE)],
        scratch_shapes=[pltpu.VMEM((ts, D_SGU), F32)],
        args=(proj, proj, xn, lg, lb, gate, pool_w, pool_scale, sg, sb, wm, bias_full, w_out_all, lg1, lb1, shift2,
              scale2), exchange=exchange)


def _conv_forward(g_ext, cw_ref, cb_ref):
    gm2 = pltpu.roll(g_ext, 2, 0)[CONV_HALO:]
    gm1 = pltpu.roll(g_ext, 1, 0)[CONV_HALO:]
    g0 = g_ext[CONV_HALO:]
    gc = ((cb_ref[...] + gm2 * cw_ref[0:1, :]) + gm1 * cw_ref[1:2, :]) + g0 * cw_ref[2:3, :]
    return gc, gm2, gm1


def _f4_ffn(up, xhat1, lg, lb, gate, conv_w, conv_b, w_down_all, lg2, lb2, shift_next, scale_next, layer,
            exchange=None):
    S = xhat1.shape[0]
    ts = min(TS_FF, S)
    halo_blocks = ts // CONV_HALO

    def body(g_ref, halo_ref, val_ref, xh_ref, lg_ref, lb_ref, gate_ref, cw_ref, cb_ref, wd_ref, lg2_ref, lb2_ref,
             shn_ref, scn_ref, act_ref, f_ref, xo_ref, rs_ref, hn_ref, ge_ref, dge_ref):
        i = pl.program_id(0)
        halo = halo_ref[...].astype(GELU_DTYPE) * (i > 0).astype(GELU_DTYPE)
        g_ext = jnp.concatenate([halo, g_ref[...].astype(GELU_DTYPE)], axis=0)
        cw = cw_ref[...].astype(GELU_DTYPE)
        gm2 = pltpu.roll(g_ext, 2, 0)[CONV_HALO:]
        gm1 = pltpu.roll(g_ext, 1, 0)[CONV_HALO:]
        gc = ((cb_ref[...].astype(GELU_DTYPE) + gm2 * cw[0:1, :]) + gm1 * cw[1:2, :]) + g_ext[CONV_HALO:] * cw[2:3, :]
        ge, dge = _gelu(gc)
        ge_ref[...] = ge.astype(STORE_DTYPE)
        dge_ref[...] = dge.astype(STORE_DTYPE)
        act = (ge * val_ref[...].astype(GELU_DTYPE)).astype(MXU_DTYPE)
        act_ref[...] = act
        f = _dot(act, wd_ref[...])
        f_ref[...] = f.astype(STORE_DTYPE)
        x = xh_ref[...] * lg_ref[...] + lb_ref[...]
        z = ALPHA * x + gate_ref[...] * f
        mu = jnp.mean(z, axis=-1, keepdims=True)
        zc = z - mu
        var = jnp.mean(zc * zc, axis=-1, keepdims=True)
        rs = lax.rsqrt(var + LN_EPS)
        xhat = zc * rs
        xo_ref[...] = xhat
        rs_ref[...] = rs
        x2 = xhat * lg2_ref[...] + lb2_ref[...]
        hn_ref[...] = (x2 * (1.0 + scn_ref[...]) + shn_ref[...]).astype(MXU_DTYPE)

    tile = lambda w: pl.BlockSpec((ts, w), lambda i: (i, 0))
    return _call(
        body, name=f"f4_ffn_l{layer}", grid=(S // ts,),
        in_specs=[pl.BlockSpec((ts, D_FF), lambda i: (i, 0)),
                  pl.BlockSpec((CONV_HALO, D_FF), lambda i: (jnp.maximum(i * halo_blocks - 1, 0), 0)),
                  pl.BlockSpec((ts, D_FF), lambda i: (i, 1)),
                  tile(D), _row(D), _row(D), _row(D), _full((3, D_FF)), _row(D_FF),
                  _layer_block((D_FF, D), layer), _row(D), _row(D), _row(D), _row(D)],
        out_specs=[tile(D_FF), tile(D), tile(D), tile(1), tile(D), tile(D_FF), tile(D_FF)],
        out_shape=[jax.ShapeDtypeStruct((S, D_FF), MXU_DTYPE), jax.ShapeDtypeStruct((S, D), STORE_DTYPE),
                   jax.ShapeDtypeStruct((S, D), F32), jax.ShapeDtypeStruct((S, 1), F32),
                   jax.ShapeDtypeStruct((S, D), MXU_DTYPE), jax.ShapeDtypeStruct((S, D_FF), STORE_DTYPE),
                   jax.ShapeDtypeStruct((S, D_FF), STORE_DTYPE)],
        args=(up, up, up, xhat1, lg, lb, gate, conv_w, conv_b, w_down_all, lg2, lb2, shift_next, scale_next),
        exchange=exchange)


def _accumulate(i, ref, value):
    @pl.when(i == 0)
    def _():
        ref[...] = value

    @pl.when(i > 0)
    def _():
        ref[...] += value


def _b_ln(d, xhat, rstd, f, lg, lb, gate, name, modulated=None, loss_head=False, exchange=None):
    S = d.shape[0]
    ts = min(TS_PROJ, S)
    has_mod = modulated is not None
    assert not (has_mod and loss_head)

    def body(*refs):
        if has_mod:
            (d_ref, xh_ref, rs_ref, f_ref, lg_ref, lb_ref, gate_ref, dri_ref, sc_ref,
             df_ref, dres_ref, dlg_ref, dlb_ref, dgate_ref, dsc_ref, dsh_ref) = refs
        elif loss_head:
            (d_ref, xh_ref, rs_ref, f_ref, lg_ref, lb_ref, gate_ref,
             df_ref, dres_ref, dlg_ref, dlb_ref, dgate_ref, sq_ref) = refs
        else:
            (d_ref, xh_ref, rs_ref, f_ref, lg_ref, lb_ref, gate_ref,
             df_ref, dres_ref, dlg_ref, dlb_ref, dgate_ref) = refs
        i = pl.program_id(0)
        xh = xh_ref[...]
        if has_mod:
            dh = d_ref[...]
            dxo_t = dh * (1.0 + sc_ref[...]) + dri_ref[...]
            _accumulate(i, dsc_ref, _colsum(dh * (xh * lg_ref[...] + lb_ref[...])))
            _accumulate(i, dsh_ref, _colsum(dh))
        elif loss_head:
            err = (xh * lg_ref[...] + lb_ref[...]) - d_ref[...]
            dxo_t = err * (1.0 / D)
            _accumulate(i, sq_ref, _colsum(err * err))
        else:
            dxo_t = d_ref[...]
        dxh = dxo_t * lg_ref[...]
        m1 = jnp.mean(dxh, axis=-1, keepdims=True)
        m2 = jnp.mean(dxh * xh, axis=-1, keepdims=True)
        dz = rs_ref[...] * (dxh - m1 - xh * m2)
        df_ref[...] = (dz * gate_ref[...]).astype(MXU_DTYPE)
        dres_ref[...] = ALPHA * dz
        _accumulate(i, dlg_ref, _colsum(dxo_t * xh))
        _accumulate(i, dlb_ref, _colsum(dxo_t))
        _accumulate(i, dgate_ref, _colsum(dz * f_ref[...].astype(F32)))

    tile = lambda w: pl.BlockSpec((ts, w), lambda i: (i, 0))
    n_sums = 5 if has_mod else 4 if loss_head else 3
    in_specs = [tile(D), tile(D), tile(1), tile(D), _row(D), _row(D), _row(D)]
    args = [d, xhat, rstd, f, lg, lb, gate]
    if has_mod:
        in_specs += [tile(D), _row(D)]
        args += list(modulated)
    return _call(
        body, name=name, grid=(S // ts,), in_specs=in_specs,
        out_specs=[tile(D), tile(D)] + [_row(D)] * n_sums,
        out_shape=[jax.ShapeDtypeStruct((S, D), MXU_DTYPE), jax.ShapeDtypeStruct((S, D), F32)]
        + [jax.ShapeDtypeStruct((1, D), F32)] * n_sums,
        args=args, exchange=exchange)


def _input_grad(dh, dres, scale, x, exchange=None):
    S = dh.shape[0]
    ts = min(TS_PROJ, S)

    def body(dh_ref, dres_ref, sc_ref, x_ref, dx_ref, dsc_ref, dsh_ref):
        i = pl.program_id(0)
        dh_t = dh_ref[...]
        dx_ref[...] = dh_t * (1.0 + sc_ref[...]) + dres_ref[...]
        _accumulate(i, dsc_ref, _colsum(dh_t * x_ref[...]))
        _accumulate(i, dsh_ref, _colsum(dh_t))

    tile = pl.BlockSpec((ts, D), lambda i: (i, 0))
    return _call(
        body, name="input_grad", grid=(S // ts,), in_specs=[tile, tile, _row(D), tile],
        out_specs=[tile, _row(D), _row(D)],
        out_shape=[jax.ShapeDtypeStruct((S, D), F32)] + [jax.ShapeDtypeStruct((1, D), F32)] * 2,
        args=(dh, dres, scale, x), exchange=exchange)


def _b2_ffn(df2, w_down_all, up, ge, dge, conv_w, layer, exchange=None):
    S = df2.shape[0]
    ts = min(TS_FF, S)
    n_tiles = S // ts
    n_ext = ts + CONV_HALO

    def body(df_ref, wd_ref, g_ref, val_ref, ge_ref, dge_ref, cw_ref, dup_ref, dcw_ref, dcb_ref, next_dgc):
        i = pl.program_id(0)

        @pl.when(i == 0)
        def _():
            next_dgc[...] = jnp.zeros_like(next_dgc)

        dact = _dot_nt(df_ref[...], wd_ref[...]).astype(GELU_DTYPE)
        dup_ref[:, D_FF:2 * D_FF] = (dact * ge_ref[...].astype(GELU_DTYPE)).astype(MXU_DTYPE)
        dgc = (dact * val_ref[...].astype(GELU_DTYPE) * dge_ref[...].astype(GELU_DTYPE)).astype(F32)
        ext = jnp.concatenate([dgc, next_dgc[...]], axis=0)
        dp1 = pltpu.roll(ext, n_ext - 1, 0)[:ts]
        dp2 = pltpu.roll(ext, n_ext - 2, 0)[:ts]
        dup_ref[:, 0:D_FF] = (dgc * cw_ref[2:3, :] + dp1 * cw_ref[1:2, :] + dp2 * cw_ref[0:1, :]).astype(MXU_DTYPE)
        next_dgc[...] = dgc[0:CONV_HALO]
        g = g_ref[...].astype(F32)
        dcw = jnp.concatenate([_colsum(dp2 * g), _colsum(dp1 * g), _colsum(dgc * g)], axis=0)
        _accumulate(i, dcw_ref, dcw)
        _accumulate(i, dcb_ref, _colsum(dgc))

    tile = lambda w, col=0: pl.BlockSpec((ts, w), lambda i: (n_tiles - 1 - i, col))
    return _call(
        body, name=f"b2_ffn_l{layer}", grid=(n_tiles,),
        in_specs=[tile(D), _layer_block((D_FF, D), layer), tile(D_FF), tile(D_FF, 1), tile(D_FF), tile(D_FF),
                  _full((3, D_FF))],
        out_specs=[tile(2 * D_FF), _full((3, D_FF)), _row(D_FF)],
        out_shape=[jax.ShapeDtypeStruct((S, 2 * D_FF), MXU_DTYPE),
                   jax.ShapeDtypeStruct((3, D_FF), F32), jax.ShapeDtypeStruct((1, D_FF), F32)],
        scratch_shapes=[pltpu.VMEM((CONV_HALO, D_FF), F32)],
        args=(df2, w_down_all, up, up, ge, dge, conv_w), exchange=exchange)


def _b5_mixers(df1, w_out_all, proj, gel, pool_w, pool_scale, sg, sb, wm, bias_full, layer):
    S = df1.shape[0]
    ts = min(TS_MIX, S)
    n_chunks = ts // CHUNK
    halo_blocks = ts // POOL_HALO
    last_halo = S // POOL_HALO - 1

    def body(df_ref, dfh_ref, wo_ref, p_ref, ah_ref, gel_ref, pw_ref, ps_ref, sg_ref, sb_ref, wm_ref, bias_ref,
             dp_ref, dpw_ref, dps_ref, dsg_ref, dsb_ref, dwm_ref, dbias_ref, z_scr, dvn_scr):
        i = pl.program_id(0)
        last = pl.num_programs(0) - 1
        dmix = _dot_nt(df_ref[...], wo_ref[...])
        dmix_halo = _dot_nt(dfh_ref[...], wo_ref[0:D_POOL, :]) * (i < last).astype(F32)

        a = p_ref[:, 0:D_POOL].astype(F32)
        halo = ah_ref[...].astype(F32) * (i > 0).astype(F32)
        pooled = _pool_forward(jnp.concatenate([halo, a], axis=0), a, i * ts, ts)
        n = ts + POOL_HALO
        dps_parts = []
        for g, window in enumerate(POOL_WINDOWS):
            cols = slice(g * GROUP, (g + 1) * GROUP)
            pooled_b = pooled[g].astype(MXU_DTYPE)
            mixed = _dot(pooled_b, pw_ref[g])
            dya = dmix[:, cols]
            dps_parts.append(_colsum(dya * mixed))
            dmixed = (dya * ps_ref[:, cols]).astype(MXU_DTYPE)
            dpw_g = _dot_tn(pooled_b, dmixed)

            @pl.when(i == 0)
            def _():
                dpw_ref[g] = dpw_g

            @pl.when(i > 0)
            def _():
                dpw_ref[g] += dpw_g

            dpooled = _dot_nt(dmixed, pw_ref[g])
            dmixed_h = (dmix_halo[:, cols] * ps_ref[:, cols]).astype(MXU_DTYPE)
            dpooled_h = _dot_nt(dmixed_h, pw_ref[g])
            q = dpooled * _pool_counts(i * ts, ts, window)
            s = jnp.concatenate([q, dpooled_h * (1.0 / window)], axis=0)
            k = 1
            while k < window:
                s = s + pltpu.roll(s, n - k, 0)
                k *= 2
            dp_ref[:, cols] = (s[:ts] - dpooled).astype(MXU_DTYPE)
        _accumulate(i, dps_ref, jnp.concatenate(dps_parts, axis=1))

        gu, dgu, gv, dgv = (gel_ref[:, k * D_SGU:(k + 1) * D_SGU].astype(F32) for k in range(4))
        vhat, rs = _sgu_norm(gv)
        vn = (vhat * sg_ref[...] + sb_ref[...]).astype(MXU_DTYPE)
        dyb = dmix[:, D_POOL:D]
        dz = dyb * gu
        dzb = dz.astype(MXU_DTYPE)
        dbias = dz[0:CHUNK]
        for c in range(1, n_chunks):
            dbias = dbias + dz[c * CHUNK:(c + 1) * CHUNK]
        _accumulate(i, dbias_ref, dbias)
        for h in range(HEADS):
            cols = slice(h * GROUP, (h + 1) * GROUP)
            vn_h = jnp.concatenate([vn[c * CHUNK:(c + 1) * CHUNK, cols] for c in range(n_chunks)], axis=1)
            dz_h = jnp.concatenate([dzb[c * CHUNK:(c + 1) * CHUNK, cols] for c in range(n_chunks)], axis=1)
            z_h = _dot(wm_ref[h], vn_h)
            dvn_h = _dot_tn(wm_ref[h], dz_h)
            dwm_h = _dot_nt(dz_h, vn_h)
            for c in range(n_chunks):
                rows = slice(c * CHUNK, (c + 1) * CHUNK)
                z_scr[rows, cols] = z_h[:, c * GROUP:(c + 1) * GROUP] + bias_ref[:, cols]
                dvn_scr[rows, cols] = dvn_h[:, c * GROUP:(c + 1) * GROUP]

            @pl.when(i == 0)
            def _():
                dwm_ref[h] = dwm_h

            @pl.when(i > 0)
            def _():
                dwm_ref[h] += dwm_h

        dp_ref[:, D_POOL:D_POOL + D_SGU] = (dyb * z_scr[...] * dgu).astype(MXU_DTYPE)
        dvn = dvn_scr[...]
        _accumulate(i, dsg_ref, _colsum(dvn * vhat))
        _accumulate(i, dsb_ref, _colsum(dvn))
        dvh = dvn * sg_ref[...]
        m1 = jnp.mean(dvh, axis=-1, keepdims=True)
        m2 = jnp.mean(dvh * vhat, axis=-1, keepdims=True)
        dp_ref[:, D_POOL + D_SGU:D_IN] = (rs * (dvh - m1 - vhat * m2) * dgv).astype(MXU_DTYPE)

        @pl.when(i == last)
        def _():
            tri = (lax.broadcasted_iota(jnp.int32, (CHUNK, CHUNK), 0)
                   >= lax.broadcasted_iota(jnp.int32, (CHUNK, CHUNK), 1))
            for h in range(HEADS):
                dwm_ref[h] = jnp.where(tri, dwm_ref[h], 0.0)

    tile = lambda w: pl.BlockSpec((ts, w), lambda i: (i, 0))
    return pl.pallas_call(
        body, name=f"b5_mixers_l{layer}", grid=(S // ts,),
        in_specs=[tile(D),
                  pl.BlockSpec((POOL_HALO, D), lambda i: (jnp.minimum((i + 1) * halo_blocks, last_halo), 0)),
                  _layer_block((D, D), layer), tile(D_POOL),
                  pl.BlockSpec((POOL_HALO, D_POOL), lambda i: (jnp.maximum(i * halo_blocks - 1, 0), 0)),
                  tile(4 * D_SGU),
                  _full((N_GROUPS, GROUP, GROUP)), _row(D_POOL), _row(D_SGU), _row(D_SGU),
                  _full((HEADS, CHUNK, CHUNK)), _full((CHUNK, D_SGU))],
        out_specs=[tile(D_IN), _full((N_GROUPS, GROUP, GROUP)), _row(D_POOL), _row(D_SGU), _row(D_SGU),
                   _full((HEADS, CHUNK, CHUNK)), _full((CHUNK, D_SGU))],
        out_shape=[jax.ShapeDtypeStruct((S, D_IN), MXU_DTYPE), jax.ShapeDtypeStruct((N_GROUPS, GROUP, GROUP), F32),
                   jax.ShapeDtypeStruct((1, D_POOL), F32), jax.ShapeDtypeStruct((1, D_SGU), F32),
                   jax.ShapeDtypeStruct((1, D_SGU), F32), jax.ShapeDtypeStruct((HEADS, CHUNK, CHUNK), F32),
                   jax.ShapeDtypeStruct((CHUNK, D_SGU), F32)],
        scratch_shapes=[pltpu.VMEM((ts, D_SGU), F32), pltpu.VMEM((ts, D_SGU), F32)],
        compiler_params=_params(1),
    )(df1, df1, w_out_all, proj, proj, gel, pool_w, pool_scale, sg, sb, wm, bias_full)


def _weight_grad(a, bs, tn, name, rows=TS_TN):
    S, M = a.shape
    counts = [b.shape[1] // tn for b in bs]
    starts = [sum(counts[:k]) for k in range(len(bs))]
    ts = min(rows, S)

    def body(a_ref, *refs):
        b_refs, o_ref = refs[:-1], refs[-1]
        j = pl.program_id(0)
        for b_ref, lo, n in zip(b_refs, starts, counts):
            @pl.when(jnp.logical_and(j >= lo, j < lo + n))
            def _():
                _accumulate(pl.program_id(1), o_ref, _dot_tn(a_ref[...], b_ref[...]))

    b_spec = lambda lo, n: pl.BlockSpec((ts, tn), lambda j, i: (i, jnp.clip(j - lo, 0, n - 1)))
    return pl.pallas_call(
        body, name=name, grid=(sum(counts), S // ts),
        in_specs=[pl.BlockSpec((ts, M), lambda j, i: (i, 0))] + [b_spec(lo, n) for lo, n in zip(starts, counts)],
        out_specs=pl.BlockSpec((M, tn), lambda j, i: (0, j)),
        out_shape=jax.ShapeDtypeStruct((M, sum(counts) * tn), F32), compiler_params=_params(2),
    )(a, *bs)


def _silu(x):
    return x * (1.0 / (1.0 + jnp.exp(-x)))


def _ada_forward(c_all, ada_w, ada_b_cols):
    n_cols = ada_w.shape[2]
    tc = 512

    def body(c_ref, w_ref, b_ref, o_ref):
        ca = _silu(c_ref[...]).astype(MXU_DTYPE)
        o_ref[...] = _dot(ca, w_ref[...].astype(MXU_DTYPE)) + b_ref[...]

    return pl.pallas_call(
        body, name="ada_forward", grid=(DEPTH, n_cols // tc),
        in_specs=[pl.BlockSpec((16, D), lambda l, j: (0, 0)), pl.BlockSpec((None, D, tc), lambda l, j: (l, 0, j)),
                  pl.BlockSpec((None, 1, tc), lambda l, j: (l, 0, j))],
        out_specs=pl.BlockSpec((None, 16, tc), lambda l, j: (l, 0, j)),
        out_shape=jax.ShapeDtypeStruct((DEPTH, 16, n_cols), F32),
        compiler_params=_params(2),
    )(c_all, ada_w, ada_b_cols)


def _ada_backward(c_all, dmod_cols):
    n_cols = dmod_cols.shape[2]
    tc = 512

    def body(c_ref, d_ref, o_ref):
        ca = _silu(c_ref[...]).astype(MXU_DTYPE)
        o_ref[...] = _dot_tn(ca, d_ref[...].astype(MXU_DTYPE))

    return pl.pallas_call(
        body, name="ada_backward", grid=(DEPTH, n_cols // tc),
        in_specs=[pl.BlockSpec((16, D), lambda l, j: (0, 0)), pl.BlockSpec((None, 16, tc), lambda l, j: (l, 0, j))],
        out_specs=pl.BlockSpec((None, D, tc), lambda l, j: (l, 0, j)),
        out_shape=jax.ShapeDtypeStruct((DEPTH, D, n_cols), F32),
        compiler_params=_params(2),
    )(c_all, dmod_cols)


def _adamw(w, g, m, v, name):
    R, C = w.shape
    tr = R
    for cand in (512, 256, 128, 64, 32, 16, 8):
        if R % cand == 0 and cand * C * 4 <= 2 ** 21:
            tr = cand
            break
    c1 = 1.0 - ADAM_B1 ** ADAM_STEP
    c2 = 1.0 - ADAM_B2 ** ADAM_STEP

    def body(w_ref, g_ref, m_ref, v_ref, d_ref, mo_ref, vo_ref):
        gg = g_ref[...]
        mn = ADAM_B1 * m_ref[...] + (1.0 - ADAM_B1) * gg
        vn = ADAM_B2 * v_ref[...] + (1.0 - ADAM_B2) * (gg * gg)
        mo_ref[...] = mn
        vo_ref[...] = vn
        d_ref[...] = -ADAM_LR * ((mn / c1) / (jnp.sqrt(vn / c2) + ADAM_EPS) + ADAM_WD * w_ref[...])

    tile = pl.BlockSpec((tr, C), lambda i: (i, 0))
    return pl.pallas_call(
        body, name=name, grid=(R // tr,), in_specs=[tile] * 4, out_specs=[tile] * 3,
        out_shape=[jax.ShapeDtypeStruct((R, C), F32)] * 3, compiler_params=_params(1),
    )(w, g, m, v)


def _position():
    x, y, c = lax.axis_index("x"), lax.axis_index("y"), lax.axis_index("c")
    other_chips = [(1 - x, y), (x, 1 - y), (1 - x, 1 - y)]
    return x, y, c, other_chips


def _all_gather8(block, name):
    R, C = block.shape

    def body(x_ref, out_ref, send_sems, recv_sems, local_sem):
        x, y, c, chips = _position()
        me, sibling = (x, y, c), (x, y, 1 - c)

        def rows(px, py, pc):
            return out_ref.at[pl.ds((4 * px + 2 * py + pc) * R, R), :]

        def copy(k, blk, to, src=None):
            return pltpu.make_async_remote_copy(
                src_ref=rows(*blk) if src is None else src, dst_ref=rows(*blk),
                send_sem=send_sems.at[k], recv_sem=recv_sems.at[k], device_id=to, device_id_type=MESH)

        mine = pltpu.make_async_copy(x_ref, rows(*me), local_sem)
        mine.start()
        first = [copy(0, me, sibling, src=x_ref)]
        first += [copy(1 + j, me, (*chip, c), src=x_ref) for j, chip in enumerate(chips)]
        for cp in first:
            cp.start()
        passed = [copy(4 + j, (*chip, c), sibling) for j, chip in enumerate(chips)]
        for j, chip in enumerate(chips):
            copy(1 + j, (*chip, c), me).wait_recv()
            passed[j].start()
        copy(0, sibling, me).wait_recv()
        for j, chip in enumerate(chips):
            copy(4 + j, (*chip, 1 - c), me).wait_recv()
        for cp in first + passed:
            cp.wait_send()
        mine.wait()

    return pl.pallas_call(
        body, name=name, out_shape=jax.ShapeDtypeStruct((N_DEV * R, C), block.dtype),
        in_specs=[pl.BlockSpec(memory_space=pltpu.VMEM)], out_specs=ANY,
        scratch_shapes=[pltpu.SemaphoreType.DMA((7,)), pltpu.SemaphoreType.DMA((7,)), pltpu.SemaphoreType.DMA(())],
    )(block)


def _gather_flat(vec, name):
    n = vec.shape[0]
    padded = -(-n // 1024) * 1024
    block = jnp.pad(vec, (0, padded - n)).reshape(8, padded // 8)
    out = _all_gather8(block, name)
    return out.reshape(N_DEV, padded)[:, :n]


_SHARD_KINDS = ("cols", "rows", "cols", "rows")


def _shard_of(ref, shape, kind, chip):
    m, n = shape
    return ref.at[:, pl.ds(chip * (n // 4), n // 4)] if kind == "cols" else ref.at[pl.ds(chip * (m // 4), m // 4), :]


def _half_of(ref, shape, kind, h):
    m, n = shape
    return ref.at[pl.ds(h * (m // 2), m // 2), :] if kind == "cols" else ref.at[:, pl.ds(h * (n // 2), n // 2)]


def _half_shape(shape, kind):
    m, n = shape
    return (m // 2, n) if kind == "cols" else (m, n // 2)


def _shard_shape(shape, kind):
    m, n = shape
    return (m, n // 4) if kind == "cols" else (m // 4, n)


def _remote(src, dst, k, to, send_sems, recv_sems):
    return pltpu.make_async_remote_copy(src_ref=src, dst_ref=dst, send_sem=send_sems.at[k], recv_sem=recv_sems.at[k],
                                        device_id=to, device_id_type=MESH)


def _gather_exchange(shards, full, kind, layer):
    shape = full.shape[1:]

    def copies(xin, xout, send_sems, recv_sems):
        x, y, c, chips = _position()
        my_chip = 2 * x + y
        own = xin[0].at[layer]
        place = lambda chip: _shard_of(xout[0].at[layer], shape, kind, chip)
        peers = [(x, y, 1 - c)] + [(*chip, c) for chip in chips]
        sources = [my_chip] + [2 * cx + cy for cx, cy in chips]
        sends = [_remote(own, place(my_chip), k, peer, send_sems, recv_sems) for k, peer in enumerate(peers)]
        arrivals = [_remote(place(s), place(s), k, (x, y, c), send_sems, recv_sems) for k, s in enumerate(sources)]
        return sends, arrivals

    return _Exchange([shards, full], [1], [], 4, copies)


def _swap_exchange(grads):
    shapes = [g.shape for g in grads]
    fresh = [jax.ShapeDtypeStruct(_half_shape(s, k), F32) for s, k in zip(shapes, _SHARD_KINDS)]

    def copies(xin, xout, send_sems, recv_sems):
        x, y, c, _ = _position()
        sends = [_remote(_half_of(xin[a], shapes[a], _SHARD_KINDS[a], 1 - c), xout[a], a, (x, y, 1 - c),
                         send_sems, recv_sems) for a in range(len(grads))]
        arrivals = [_remote(xout[a], xout[a], a, (x, y, c), send_sems, recv_sems) for a in range(len(grads))]
        return sends, arrivals

    return _Exchange(list(grads), [], fresh, len(grads), copies)


def _scatter_exchange(partials):
    n_w = len(partials)
    shapes = [p.shape for p in partials]
    fresh = [jax.ShapeDtypeStruct((3,) + _shard_shape(s, k), p.dtype) for s, k, p in zip(shapes, _SHARD_KINDS, partials)]

    def copies(xin, xout, send_sems, recv_sems):
        x, y, c, chips = _position()
        sends, arrivals = [], []
        for j, (cx, cy) in enumerate(chips):
            for a in range(n_w):
                src = _shard_of(xin[a], shapes[a], _SHARD_KINDS[a], 2 * cx + cy)
                sends.append(_remote(src, xout[a].at[j], j * n_w + a, (cx, cy, c), send_sems, recv_sems))
                arrivals.append(_remote(xout[a].at[j], xout[a].at[j], j * n_w + a, (x, y, c), send_sems, recv_sems))
        return sends, arrivals

    return _Exchange(list(partials), [], fresh, 3 * n_w, copies)


def _share_exchange(reduced, layer):
    n_w = len(reduced)
    shapes = [r.shape[1:] for r in reduced]

    def copies(xin, xout, send_sems, recv_sems):
        x, y, c, _ = _position()
        half = lambda a, h: _half_of(xout[a].at[layer], shapes[a], _SHARD_KINDS[a], h)
        sends = [_remote(half(a, c), half(a, c), a, (x, y, 1 - c), send_sems, recv_sems) for a in range(n_w)]
        arrivals = [_remote(half(a, 1 - c), half(a, 1 - c), a, (x, y, c), send_sems, recv_sems) for a in range(n_w)]
        return sends, arrivals

    return _Exchange(list(reduced), list(range(n_w)), [], n_w, copies)


def _chip_partial(pos, grad, theirs, kind, name):
    M, N = grad.shape
    if kind == "cols":
        tm = 256
        steps = M // 2 // tm
        block, g_map = (tm, N), (lambda i, pos: (pos[0] * steps + i, 0))
    else:
        tm = M // 4
        steps = 4
        block, g_map = (tm, N // 2), (lambda i, pos: (i, pos[0]))

    def body(pos_ref, g_ref, t_ref, o_ref):
        o_ref[...] = (g_ref[...] + t_ref[...]).astype(WIRE_DTYPE)

    grid_spec = pltpu.PrefetchScalarGridSpec(
        num_scalar_prefetch=1, grid=(steps,),
        in_specs=[pl.BlockSpec(block, g_map), pl.BlockSpec(block, lambda i, pos: (i, 0))],
        out_specs=pl.BlockSpec(block, lambda i, pos: (i, 0)))
    return pl.pallas_call(
        body, name=name, grid_spec=grid_spec, out_shape=jax.ShapeDtypeStruct(_half_shape((M, N), kind), WIRE_DTYPE),
        compiler_params=_params(1),
    )(pos, grad, theirs)


def _reduce_shard(pos, grad, theirs, received, reduced, kind, layer, name):
    M, N = grad.shape
    if kind == "cols":
        tm = min(M // 2, 512)
        steps = M // 2 // tm
        block = (tm, N // 4)
        g_map = lambda i, pos: (pos[0] * steps + i, pos[1])
        t_map = lambda i, pos: (i, pos[1])
        o_map = lambda i, pos: (layer, pos[0] * steps + i, 0)
    else:
        steps = 1
        block = (M // 4, N // 2)
        g_map = lambda i, pos: (pos[1], pos[0])
        t_map = lambda i, pos: (pos[1], 0)
        o_map = lambda i, pos: (layer, 0, pos[0])
    out_shape = (DEPTH,) + _shard_shape((M, N), kind)

    def body(pos_ref, g_ref, t_ref, r0_ref, r1_ref, r2_ref, *rest):
        o_ref = rest[-1]
        chip = pos_ref[1]
        own = g_ref[...] + t_ref[...]
        r = [r0_ref[...].astype(F32), r1_ref[...].astype(F32), r2_ref[...].astype(F32)]
        total = None
        for s in range(N_CHIPS):
            rel = jnp.bitwise_xor(chip, s)
            term = jnp.where(rel == 0, own, jnp.where(rel == 2, r[0], jnp.where(rel == 1, r[1], r[2])))
            total = term if total is None else total + term
        o_ref[...] = total

    r_spec = lambda j: pl.BlockSpec((None,) + block, lambda i, pos: (j, i, 0))
    in_specs = [pl.BlockSpec(block, g_map), pl.BlockSpec(block, t_map), r_spec(0), r_spec(1), r_spec(2)]
    args = [pos, grad, theirs, received, received, received]
    aliases = {}
    if reduced is not None:
        in_specs.append(ANY)
        args.append(reduced)
        aliases = {6: 0}
    grid_spec = pltpu.PrefetchScalarGridSpec(
        num_scalar_prefetch=1, grid=(steps,), in_specs=in_specs, out_specs=pl.BlockSpec((None,) + block, o_map))
    return pl.pallas_call(
        body, name=name, grid_spec=grid_spec, out_shape=jax.ShapeDtypeStruct(out_shape, F32),
        input_output_aliases=aliases, compiler_params=_params(1),
    )(*args)


def _sum_devices(gathered):
    R8, C = gathered.shape
    R = R8 // N_DEV
    lanes = C // 128
    tc = 128 * max(k for k in range(1, lanes + 1) if lanes % k == 0 and k * 128 * R8 * 4 <= 2 ** 22)

    def body(g_ref, o_ref):
        total = g_ref[0:R, :]
        for d in range(1, N_DEV):
            total = total + g_ref[d * R:(d + 1) * R, :]
        o_ref[...] = total

    return pl.pallas_call(
        body, name="sum_devices", grid=(C // tc,),
        in_specs=[pl.BlockSpec((R8, tc), lambda j: (0, j))], out_specs=pl.BlockSpec((R, tc), lambda j: (0, j)),
        out_shape=jax.ShapeDtypeStruct((R, C), F32), compiler_params=_params(1),
    )(gathered)


_SMALL = ("pool_w", "pool_scale", "sgu_ln_g", "sgu_ln_b", "sgu_w", "sgu_b", "ln1_g", "ln1_b", "conv_w",
          "conv_b", "ln2_g", "ln2_b", "ada_b")
_WEIGHTS = ("ada_w", "ada_b", "w_in", "pool_w", "pool_scale", "sgu_ln_g", "sgu_ln_b", "sgu_w", "sgu_b", "w_out",
            "ln1_g", "ln1_b", "w_up", "conv_w", "conv_b", "w_down", "ln2_g", "ln2_b")


def kernel(x, c, ada_w, ada_b, w_in, pool_w, pool_scale, sgu_ln_g, sgu_ln_b, sgu_w, sgu_b, w_out, ln1_g, ln1_b, w_up, conv_w, conv_b, w_down, ln2_g, ln2_b, loss_target, m_ada_w, m_ada_b, m_w_in, m_pool_w, m_pool_scale, m_sgu_ln_g, m_sgu_ln_b, m_sgu_w, m_sgu_b, m_w_out, m_ln1_g, m_ln1_b, m_w_up, m_conv_w, m_conv_b, m_w_down, m_ln2_g, m_ln2_b, v_ada_w, v_ada_b, v_w_in, v_pool_w, v_pool_scale, v_sgu_ln_g, v_sgu_ln_b, v_sgu_w, v_sgu_b, v_w_out, v_ln1_g, v_ln1_b, v_w_up, v_conv_w, v_conv_b, v_w_down, v_ln2_g, v_ln2_b):
    weights = dict(ada_w=ada_w, ada_b=ada_b, w_in=w_in, pool_w=pool_w, pool_scale=pool_scale, sgu_ln_g=sgu_ln_g,
                   sgu_ln_b=sgu_ln_b, sgu_w=sgu_w, sgu_b=sgu_b, w_out=w_out, ln1_g=ln1_g, ln1_b=ln1_b, w_up=w_up,
                   conv_w=conv_w, conv_b=conv_b, w_down=w_down, ln2_g=ln2_g, ln2_b=ln2_b)
    mom_m = dict(ada_w=m_ada_w, ada_b=m_ada_b, w_in=m_w_in, pool_w=m_pool_w, pool_scale=m_pool_scale,
                 sgu_ln_g=m_sgu_ln_g, sgu_ln_b=m_sgu_ln_b, sgu_w=m_sgu_w, sgu_b=m_sgu_b, w_out=m_w_out,
                 ln1_g=m_ln1_g, ln1_b=m_ln1_b, w_up=m_w_up, conv_w=m_conv_w, conv_b=m_conv_b, w_down=m_w_down,
                 ln2_g=m_ln2_g, ln2_b=m_ln2_b)
    mom_v = dict(ada_w=v_ada_w, ada_b=v_ada_b, w_in=v_w_in, pool_w=v_pool_w, pool_scale=v_pool_scale,
                 sgu_ln_g=v_sgu_ln_g, sgu_ln_b=v_sgu_ln_b, sgu_w=v_sgu_w, sgu_b=v_sgu_b, w_out=v_w_out,
                 ln1_g=v_ln1_g, ln1_b=v_ln1_b, w_up=v_w_up, conv_w=v_conv_w, conv_b=v_conv_b, w_down=v_w_down,
                 ln2_g=v_ln2_g, ln2_b=v_ln2_b)

    ix, iy, ic = lax.axis_index("x"), lax.axis_index("y"), lax.axis_index("c")
    chip = 2 * ix + iy
    dev = 4 * ix + 2 * iy + ic
    pos = jnp.stack([ic, chip]).astype(jnp.int32)
    xs = x[0]
    target = loss_target[0]
    ff_shard = conv_w.shape[2]
    mod_shard = ada_w.shape[2]

    first = _gather_flat(jnp.concatenate([c.reshape(-1), conv_w.reshape(-1)]), "gather_c_conv")
    c_all = jnp.pad(first[:, :D], ((0, 8), (0, 0)))
    conv_parts = first[0::2, D:].reshape(N_CHIPS, DEPTH, 3, ff_shard)
    conv_full = jnp.transpose(conv_parts, (1, 2, 0, 3)).reshape(DEPTH, 3, D_FF)
    ada_b_cols = lax.dynamic_slice_in_dim(ada_b, chip * mod_shard, mod_shard, axis=1).reshape(DEPTH, 1, mod_shard)
    mod_part = _ada_forward(c_all, ada_w, ada_b_cols)[:, :8, :]
    mod_all = _gather_flat(mod_part.reshape(-1), "gather_mod").reshape(N_DEV, DEPTH, 8, mod_shard)
    mod_mine = lax.dynamic_index_in_dim(mod_all[0::2], dev, axis=2, keepdims=False)
    mod = jnp.transpose(mod_mine, (1, 0, 2)).reshape(DEPTH, 6, 1, D)

    sh_in, sh_out, sh_up, sh_down = (w.astype(WIRE_DTYPE) for w in (w_in, w_out, w_up, w_down))
    wf_in = lax.empty((DEPTH, D, D_IN), WIRE_DTYPE)
    wf_out = lax.empty((DEPTH, D, D), WIRE_DTYPE)
    wf_up = lax.empty((DEPTH, D, 2 * D_FF), WIRE_DTYPE)
    wf_down = lax.empty((DEPTH, D_FF, D), WIRE_DTYPE)
    (wf_in,) = _run_exchange(_gather_exchange(sh_in, wf_in, "cols", 0), "gather_w_in_l0")

    tri = jnp.tril(jnp.ones((CHUNK, CHUNK), dtype=bool))
    ones_row = jnp.ones((1, D), F32)
    zeros_row = jnp.zeros((1, D), F32)
    row = lambda a, l: a[l].reshape(1, -1)

    saved = []
    xn, lg, lb = xs, ones_row, zeros_row
    h = _modulate_input(xs, mod[0, 0], mod[0, 1])
    for l in range(DEPTH):
        shift1, scale1, gate1, shift2, scale2, gate2 = (mod[l, k] for k in range(6))
        nxt = min(l + 1, DEPTH - 1)
        pw = pool_w[l].astype(MXU_DTYPE)
        wm = jnp.where(tri[None], sgu_w[l], 0.0).astype(MXU_DTYPE)
        bias_full = jnp.repeat(jnp.transpose(sgu_b[l]), GROUP, axis=1)
        lg1, lb1 = row(ln1_g, l), row(ln1_b, l)
        proj, (wf_out,) = _matmul(h, wf_in, l, D_IN, STORE_DTYPE, f"f1_in_proj_l{l}",
                                  exchange=_gather_exchange(sh_out, wf_out, "rows", l))
        (mix, f1, xhat1, rstd1, h2, gel), (wf_up,) = _f2_mixers(
            proj, xn, lg, lb, gate1, pw, row(pool_scale, l), row(sgu_ln_g, l), row(sgu_ln_b, l), wm, bias_full,
            wf_out, lg1, lb1, shift2, scale2, l, exchange=_gather_exchange(sh_up, wf_up, "cols", l))
        up, (wf_down,) = _matmul(h2, wf_up, l, D_FF, STORE_DTYPE, f"f3_up_proj_l{l}",
                                 exchange=_gather_exchange(sh_down, wf_down, "rows", l))
        (act, f2, xhat2, rstd2, h_next, ge, dge), filled = _f4_ffn(
            up, xhat1, lg1, lb1, gate2, conv_full[l], row(conv_b, l), wf_down, row(ln2_g, l), row(ln2_b, l),
            mod[nxt, 0], mod[nxt, 1], l,
            exchange=_gather_exchange(sh_in, wf_in, "cols", l + 1) if l + 1 < DEPTH else None)
        if filled:
            (wf_in,) = filled
        saved.append(dict(xn=xn, lg=lg, lb=lb, h=h, proj=proj, mix=mix, f1=f1, xhat1=xhat1, rstd1=rstd1, h2=h2,
                          up=up, act=act, f2=f2, xhat2=xhat2, rstd2=rstd2, pw=pw, wm=wm, bias_full=bias_full,
                          ge=ge, dge=dge, gel=gel))
        xn, lg, lb, h = xhat2, row(ln2_g, l), row(ln2_b, l), h_next

    d_out = target

    tags = ("in", "out", "up", "down")
    small = {n: [None] * DEPTH for n in _SMALL}
    dmod = [[None] * 6 for _ in range(DEPTH)]
    below = None
    ready = None
    reduced = [None] * 4

    def partial_sums(layer, full, theirs):
        return [_chip_partial(pos, g, t, kind, f"chip_partial_{tag}_l{layer}")
                for g, t, kind, tag in zip(full, theirs, _SHARD_KINDS, tags)]

    def chip_sums(layer, full, theirs, received):
        return [_reduce_shard(pos, g, t, r, out, kind, layer, f"reduce_shard_{tag}_l{layer}")
                for g, t, r, out, kind, tag in zip(full, theirs, received, reduced, _SHARD_KINDS, tags)]

    for l in reversed(range(DEPTH)):
        sv = saved[l]
        scale1, gate1, scale2, gate2 = mod[l, 1], mod[l, 2], mod[l, 4], mod[l, 5]
        lg1, lb1 = row(ln1_g, l), row(ln1_b, l)
        outs, theirs = _b_ln(d_out, sv["xhat2"], sv["rstd2"], sv["f2"], row(ln2_g, l), row(ln2_b, l), gate2,
                             f"b1_ln2_l{l}", modulated=below, loss_head=below is None,
                             exchange=_swap_exchange(ready[1]) if ready else None)
        df2, dres2, dlg2, dlb2, dmod[l][5] = outs[:5]
        if below is not None:
            dmod[l + 1][1], dmod[l + 1][0] = outs[5], outs[6]
        else:
            loss = lax.psum(jnp.sum(outs[5]) * (0.5 / D), ("x", "y", "c"))
        partials = partial_sums(ready[0], ready[1], theirs) if ready else None
        (dup, dcw, dcb), received = _b2_ffn(df2, wf_down, sv["up"], sv["ge"], sv["dge"], conv_full[l], l,
                                            exchange=_scatter_exchange(partials) if ready else None)
        if ready:
            reduced = chip_sums(ready[0], ready[1], theirs, received)
        g_down = _weight_grad(sv["act"], [df2], 512, f"wg_down_l{l}", rows=2 * TS_TN)
        dh2, shared = _matmul_nt(dup, wf_up, l, D_FF, f"b3_up_l{l}",
                                 exchange=_share_exchange(reduced, ready[0]) if ready else None)
        if ready:
            reduced = shared
        g_up = _weight_grad(sv["h2"], [dup], 1408, f"wg_up_l{l}", rows=2 * TS_TN)
        (df1, dres1, dlg1, dlb1, dmod[l][2], dmod[l][4], dmod[l][3]), _ = _b_ln(
            dh2, sv["xhat1"], sv["rstd1"], sv["f1"], lg1, lb1, gate1, f"b4_ln1_l{l}", modulated=(dres2, scale2))
        dproj, dpw, dps, dsg, dsb, dwm, dbias = _b5_mixers(
            df1, wf_out, sv["proj"], sv["gel"], sv["pw"], row(pool_scale, l), row(sgu_ln_g, l), row(sgu_ln_b, l), sv["wm"],
            sv["bias_full"], l)
        g_out = _weight_grad(sv["mix"], [df1], D, f"wg_out_l{l}", rows=2 * TS_TN)
        d_out, _ = _matmul_nt(dproj, wf_in, l, D_IN, f"b6_in_l{l}")
        g_in = _weight_grad(sv["h"], [dproj], 768, f"wg_in_l{l}", rows=2 * TS_TN)
        ready = (l, [g_in, g_out, g_up, g_down])
        below = (dres1, scale1)
        small["pool_w"][l], small["pool_scale"][l] = dpw, dps[0]
        small["sgu_ln_g"][l], small["sgu_ln_b"][l], small["sgu_w"][l] = dsg[0], dsb[0], dwm
        small["sgu_b"][l] = jnp.transpose(jnp.sum(dbias.reshape(CHUNK, HEADS, GROUP), axis=2))
        small["ln1_g"][l], small["ln1_b"][l] = dlg1[0], dlb1[0]
        small["conv_w"][l], small["conv_b"][l] = dcw, dcb[0]
        small["ln2_g"][l], small["ln2_b"][l] = dlg2[0], dlb2[0]
    (grad_x2d, dmod[0][1], dmod[0][0]), theirs = _input_grad(d_out, below[0], below[1], xs,
                                                              exchange=_swap_exchange(ready[1]))
    grad_x = grad_x2d[None]
    small["ada_b"] = [jnp.concatenate([part[0] for part in dmod[l]]) for l in range(DEPTH)]

    names = _SMALL
    flat = jnp.concatenate([jnp.stack(small[n]).reshape(-1) for n in names])
    n_small = flat.shape[0]
    padded = -(-n_small // 1024) * 1024
    gathered = _all_gather8(jnp.pad(flat, (0, padded - n_small)).reshape(8, padded // 8), "gather_small_grads")
    summed = _sum_devices(gathered).reshape(-1)[:n_small]
    grads = {}
    offset = 0
    for n in names:
        size = math.prod(weights[n].shape[1:]) * DEPTH if n != "conv_w" else DEPTH * 3 * D_FF
        grads[n] = summed[offset:offset + size]
        offset += size
    grads["conv_w"] = lax.dynamic_slice_in_dim(grads["conv_w"].reshape(DEPTH, 3, D_FF), chip * ff_shard, ff_shard,
                                               axis=2)
    for n in names:
        grads[n] = grads[n].reshape(weights[n].shape)
    dmod_all = gathered.reshape(N_DEV, padded)[:, n_small - DEPTH * 6 * D:n_small].reshape(N_DEV, DEPTH, 6 * D)
    dmod_cols = lax.dynamic_slice_in_dim(jnp.transpose(dmod_all, (1, 0, 2)), chip * mod_shard, mod_shard, axis=2)
    grads["ada_w"] = _ada_backward(c_all, jnp.pad(dmod_cols, ((0, 0), (0, 8), (0, 0))))

    partials = partial_sums(0, ready[1], theirs)
    received = _run_exchange(_scatter_exchange(partials), "scatter_partials_l0")
    reduced = chip_sums(0, ready[1], theirs, received)
    grads["w_in"], grads["w_out"], grads["w_up"], grads["w_down"] = _run_exchange(
        _share_exchange(reduced, 0), "share_reduced_l0")

    delta, new_m, new_v = {}, {}, {}
    for n in ("ada_w", "w_in", "w_out", "w_up", "w_down"):
        shape = weights[n].shape
        two_d = (shape[0] * shape[1], shape[2])
        d_, m_, v_ = _adamw(weights[n].reshape(two_d), grads[n].reshape(two_d), mom_m[n].reshape(two_d),
                            mom_v[n].reshape(two_d), f"adamw_{n}")
        delta[n], new_m[n], new_v[n] = d_.reshape(shape), m_.reshape(shape), v_.reshape(shape)
    sizes = [math.prod(weights[n].shape) for n in names]
    total = sum(sizes)
    padded = -(-total // 1024) * 1024
    pack = lambda d: jnp.pad(jnp.concatenate([d[n].reshape(-1) for n in names]), (0, padded - total)).reshape(8, -1)
    v_pack = jnp.pad(jnp.concatenate([mom_v[n].reshape(-1) for n in names]), (0, padded - total),
                     constant_values=1.0).reshape(8, -1)
    d_, m_, v_ = _adamw(pack(weights), pack(grads), pack(mom_m), v_pack, "adamw_small")
    offset = 0
    for n, size in zip(names, sizes):
        for src, dst in ((d_, delta), (m_, new_m), (v_, new_v)):
            dst[n] = src.reshape(-1)[offset:offset + size].reshape(weights[n].shape)
        offset += size

    return (loss, grad_x, *[grads[n] for n in _WEIGHTS], *[delta[n] for n in _WEIGHTS],
            *[new_m[n] for n in _WEIGHTS], *[new_v[n] for n in _WEIGHTS])
```

```python
import functools
import math

import jax
import jax.numpy as jnp
from jax import lax
from jax.experimental import pallas as pl
from jax.experimental.pallas import tpu as pltpu

F32 = jnp.float32
MXU_DTYPE = jnp.bfloat16
WIRE_DTYPE = jnp.bfloat16

DEPTH = 4
D = 1024
D_POOL = 512
D_SGU = 512
N_GROUPS = 4
GROUP = 128
POOL_WINDOWS = (2, 4, 8, 16)
POOL_HALO = 16
CHUNK = 128
HEADS = 4
D_IN = D_POOL + 2 * D_SGU
D_FF = 2816
CONV_HALO = 16
STORE_DTYPE = jnp.bfloat16
GELU_DTYPE = jnp.bfloat16
N_CHIPS = 4
N_DEV = 8
ALPHA = (2.0 * DEPTH) ** 0.25
LN_EPS = 1e-5
ADAM_LR, ADAM_B1, ADAM_B2, ADAM_EPS, ADAM_WD, ADAM_STEP = 0.001, 0.9, 0.999, 1e-08, 0.01, 10

TS_PROJ = 512
TS_MM = 1024
TS_MIX = 512
TS_FF = 256
TS_TN = 1024
VMEM_LIMIT = 52 * 2 ** 20

MESH = pl.DeviceIdType.MESH
ANY = pl.BlockSpec(memory_space=pl.ANY)

_GELU_K0 = math.sqrt(2.0 / math.pi)
_GELU_K1 = 0.044715


def _params(n_axes):
    return pltpu.CompilerParams(dimension_semantics=("arbitrary",) * n_axes, vmem_limit_bytes=VMEM_LIMIT)


class _Exchange:
    def __init__(self, inputs, inplace, fresh, n_copies, copies):
        self.inputs, self.inplace, self.fresh, self.n_copies, self.copies = inputs, inplace, fresh, n_copies, copies

    def out_shapes(self):
        return [jax.ShapeDtypeStruct(self.inputs[i].shape, self.inputs[i].dtype) for i in self.inplace] + list(self.fresh)

    def semaphores(self):
        return [pltpu.SemaphoreType.DMA((self.n_copies,)), pltpu.SemaphoreType.DMA((self.n_copies,))]

    def aliases(self, first_input, first_output):
        return {first_input + i: first_output + k for k, i in enumerate(self.inplace)}


def _call(body, *, name, grid, in_specs, out_specs, out_shape, args, scratch_shapes=(), exchange=None):
    n_axes = len(grid)
    in_specs, out_specs, out_shape, scratch_shapes = list(in_specs), list(out_specs), list(out_shape), list(scratch_shapes)
    if exchange is None:
        outs = pl.pallas_call(body, name=name, grid=grid, in_specs=in_specs, out_specs=out_specs, out_shape=out_shape,
                              scratch_shapes=scratch_shapes, compiler_params=_params(n_axes))(*args)
        return list(outs), []
    n_in, n_out, n_scr = len(in_specs), len(out_shape), len(scratch_shapes)
    x_out = exchange.out_shapes()
    n_xin, n_xout = len(exchange.inputs), len(x_out)

    def hosted(*refs):
        ins, xin = refs[:n_in], refs[n_in:n_in + n_xin]
        o = n_in + n_xin
        outs, xout = refs[o:o + n_out], refs[o + n_out:o + n_out + n_xout]
        s = o + n_out + n_xout
        scr, (send_sems, recv_sems) = refs[s:s + n_scr], refs[s + n_scr:]
        first = functools.reduce(jnp.logical_and, [pl.program_id(d) == 0 for d in range(n_axes)])
        last = functools.reduce(jnp.logical_and, [pl.program_id(d) == pl.num_programs(d) - 1 for d in range(n_axes)])

        @pl.when(first)
        def _():
            for cp in exchange.copies(xin, xout, send_sems, recv_sems)[0]:
                cp.start()

        body(*ins, *outs, *scr)

        @pl.when(last)
        def _():
            sends, arrivals = exchange.copies(xin, xout, send_sems, recv_sems)
            for cp in arrivals:
                cp.wait_recv()
            for cp in sends:
                cp.wait_send()

    results = pl.pallas_call(
        hosted, name=name, grid=grid, in_specs=in_specs + [ANY] * n_xin, out_specs=out_specs + [ANY] * n_xout,
        out_shape=out_shape + x_out, scratch_shapes=scratch_shapes + exchange.semaphores(),
        input_output_aliases=exchange.aliases(n_in, n_out), compiler_params=_params(n_axes),
    )(*args, *exchange.inputs)
    return list(results[:n_out]), list(results[n_out:])


def _run_exchange(exchange, name):
    x_out = exchange.out_shapes()
    n_xin, n_xout = len(exchange.inputs), len(x_out)

    def body(*refs):
        xin, xout = refs[:n_xin], refs[n_xin:n_xin + n_xout]
        send_sems, recv_sems = refs[n_xin + n_xout:]
        sends, arrivals = exchange.copies(xin, xout, send_sems, recv_sems)
        for cp in sends:
            cp.start()
        for cp in arrivals:
            cp.wait_recv()
        for cp in sends:
            cp.wait_send()

    return list(pl.pallas_call(
        body, name=name, in_specs=[ANY] * n_xin, out_specs=[ANY] * n_xout, out_shape=x_out,
        scratch_shapes=exchange.semaphores(), input_output_aliases=exchange.aliases(0, 0),
    )(*exchange.inputs))


def _dot(a, b):
    return jnp.dot(a, b, preferred_element_type=F32)


def _dot_nt(a, b):
    return lax.dot_general(a, b, (((1,), (1,)), ((), ())), preferred_element_type=F32)


def _dot_tn(a, b):
    return lax.dot_general(a, b, (((0,), (0,)), ((), ())), preferred_element_type=F32)


def _gelu(x):
    x2 = x * x
    t = jnp.tanh(x * (x2 * (_GELU_K0 * _GELU_K1) + _GELU_K0))
    cdf = 0.5 * t + 0.5
    dg = cdf + (x * (1.0 - t * t)) * (x2 * (1.5 * _GELU_K0 * _GELU_K1) + 0.5 * _GELU_K0)
    return x * cdf, dg


def _colsum(x):
    return jnp.sum(x, axis=0, keepdims=True)


def _row(d):
    return pl.BlockSpec((1, d), lambda *_: (0, 0))


def _full(shape):
    n = len(shape)
    return pl.BlockSpec(shape, lambda *_: (0,) * n)


def _layer_block(shape, layer):
    n = len(shape)
    return pl.BlockSpec((None,) + tuple(shape), lambda *_: (layer,) + (0,) * n)


def _pool_counts(first_row, rows, window):
    t = first_row + lax.broadcasted_iota(jnp.int32, (rows, 1), 0)
    return 1.0 / jnp.minimum(t + 1, window).astype(F32)


def _pool_forward(a_ext, a, first_row, rows):
    out = []
    for g, window in enumerate(POOL_WINDOWS):
        s = a_ext[:, g * GROUP:(g + 1) * GROUP]
        k = 1
        while k < window:
            s = s + pltpu.roll(s, k, 0)
            k *= 2
        inv = _pool_counts(first_row, rows, window)
        out.append(s[POOL_HALO:] * inv - a[:, g * GROUP:(g + 1) * GROUP])
    return out


def _sgu_norm(gv):
    mu = jnp.mean(gv, axis=-1, keepdims=True)
    xc = gv - mu
    var = jnp.mean(xc * xc, axis=-1, keepdims=True)
    rs = lax.rsqrt(var + LN_EPS)
    return xc * rs, rs


def _modulate_input(x, shift, scale):
    S = x.shape[0]
    ts = min(TS_PROJ, S)

    def body(x_ref, sh_ref, sc_ref, h_ref):
        h_ref[...] = (x_ref[...] * (1.0 + sc_ref[...]) + sh_ref[...]).astype(MXU_DTYPE)

    tile = pl.BlockSpec((ts, D), lambda i: (i, 0))
    return pl.pallas_call(
        body, name="modulate_input", grid=(S // ts,), in_specs=[tile, _row(D), _row(D)], out_specs=tile,
        out_shape=jax.ShapeDtypeStruct((S, D), MXU_DTYPE), compiler_params=_params(1),
    )(x, shift, scale)


def _matmul(a, w_all, layer, tn, out_dtype, name, exchange=None):
    S, K = a.shape
    N = w_all.shape[2]
    ts = min(TS_MM, S)

    def body(a_ref, w_ref, o_ref):
        o_ref[...] = _dot(a_ref[...], w_ref[...]).astype(out_dtype)

    (out,), extra = _call(
        body, name=name, grid=(N // tn, S // ts),
        in_specs=[pl.BlockSpec((ts, K), lambda j, i: (i, 0)), pl.BlockSpec((None, K, tn), lambda j, i: (layer, 0, j))],
        out_specs=[pl.BlockSpec((ts, tn), lambda j, i: (i, j))],
        out_shape=[jax.ShapeDtypeStruct((S, N), out_dtype)], args=(a, w_all), exchange=exchange)
    return out, extra


def _matmul_nt(a, w_all, layer, tk, name, exchange=None):
    S, K = a.shape
    N = w_all.shape[1]
    ts = min(TS_MM, S)
    n_k = K // tk

    def body(a_ref, w_ref, o_ref, acc):
        k = pl.program_id(1)
        part = _dot_nt(a_ref[...], w_ref[...])
        if n_k == 1:
            o_ref[...] = part
        else:
            @pl.when(k == 0)
            def _():
                acc[...] = part

            @pl.when(jnp.logical_and(k > 0, k < n_k - 1))
            def _():
                acc[...] += part

            @pl.when(k == n_k - 1)
            def _():
                o_ref[...] = acc[...] + part

    (out,), extra = _call(
        body, name=name, grid=(S // ts, n_k),
        in_specs=[pl.BlockSpec((ts, tk), lambda i, k: (i, k)), pl.BlockSpec((None, N, tk), lambda i, k: (layer, 0, k))],
        out_specs=[pl.BlockSpec((ts, N), lambda i, k: (i, 0))],
        out_shape=[jax.ShapeDtypeStruct((S, N), F32)],
        scratch_shapes=[pltpu.VMEM((ts, N), F32)], args=(a, w_all), exchange=exchange)
    return out, extra


def _f2_mixers(proj, xn, lg, lb, gate, pool_w, pool_scale, sg, sb, wm, bias_full, w_out_all, lg1, lb1, shift2, scale2,
               layer, exchange=None):
    S = xn.shape[0]
    ts = min(TS_MIX, S)
    n_chunks = ts // CHUNK
    halo_blocks = ts // POOL_HALO

    def body(p_ref, halo_ref, xn_ref, lg_ref, lb_ref, gate_ref, pw_ref, ps_ref, sg_ref, sb_ref, wm_ref,
             bias_ref, wo_ref, lg1_ref, lb1_ref, sh2_ref, sc2_ref, mix_ref, f_ref, xh_ref, rs_ref, h2_ref, gel_ref,
             z_scr):
        i = pl.program_id(0)
        a = p_ref[:, 0:D_POOL].astype(F32)
        u = p_ref[:, D_POOL:D_POOL + D_SGU].astype(GELU_DTYPE)
        v = p_ref[:, D_POOL + D_SGU:D_IN].astype(GELU_DTYPE)
        halo = halo_ref[...].astype(F32) * (i > 0).astype(F32)
        a_ext = jnp.concatenate([halo, a], axis=0)
        pooled = _pool_forward(a_ext, a, i * ts, ts)
        for g in range(N_GROUPS):
            mixed = _dot(pooled[g].astype(MXU_DTYPE), pw_ref[g])
            mix_ref[:, g * GROUP:(g + 1) * GROUP] = (mixed * ps_ref[:, g * GROUP:(g + 1) * GROUP]).astype(MXU_DTYPE)
        gu, dgu = _gelu(u)
        gv, dgv = _gelu(v)
        for k, part in enumerate((gu, dgu, gv, dgv)):
            gel_ref[:, k * D_SGU:(k + 1) * D_SGU] = part.astype(STORE_DTYPE)
        gu = gu.astype(F32)
        vhat, _ = _sgu_norm(gv.astype(F32))
        vn = (vhat * sg_ref[...] + sb_ref[...]).astype(MXU_DTYPE)
        for h in range(HEADS):
            cols = slice(h * GROUP, (h + 1) * GROUP)
            vn_h = jnp.concatenate([vn[c * CHUNK:(c + 1) * CHUNK, cols] for c in range(n_chunks)], axis=1)
            z_h = _dot(wm_ref[h], vn_h)
            for c in range(n_chunks):
                z_scr[c * CHUNK:(c + 1) * CHUNK, cols] = z_h[:, c * GROUP:(c + 1) * GROUP] + bias_ref[:, cols]
        mix_ref[:, D_POOL:D] = (gu * z_scr[...]).astype(MXU_DTYPE)
        f = _dot(mix_ref[...], wo_ref[...])
        f_ref[...] = f.astype(STORE_DTYPE)
        x = xn_ref[...] * lg_ref[...] + lb_ref[...]
        z1 = ALPHA * x + gate_ref[...] * f
        mu = jnp.mean(z1, axis=-1, keepdims=True)
        zc = z1 - mu
        var = jnp.mean(zc * zc, axis=-1, keepdims=True)
        rs = lax.rsqrt(var + LN_EPS)
        xhat = zc * rs
        xh_ref[...] = xhat
        rs_ref[...] = rs
        x1 = xhat * lg1_ref[...] + lb1_ref[...]
        h2_ref[...] = (x1 * (1.0 + sc2_ref[...]) + sh2_ref[...]).astype(MXU_DTYPE)

    tile = lambda w: pl.BlockSpec((ts, w), lambda i: (i, 0))
    return _call(
        body, name=f"f2_mixers_l{layer}", grid=(S // ts,),
        in_specs=[tile(D_IN),
                  pl.BlockSpec((POOL_HALO, D_POOL), lambda i: (jnp.maximum(i * halo_blocks - 1, 0), 0)),
                  tile(D), _row(D), _row(D), _row(D),
                  _full((N_GROUPS, GROUP, GROUP)), _row(D_POOL), _row(D_SGU), _row(D_SGU),
                  _full((HEADS, CHUNK, CHUNK)), _full((CHUNK, D_SGU)), _layer_block((D, D), layer),
                  _row(D), _row(D), _row(D), _row(D)],
        out_specs=[tile(D), tile(D), tile(D), tile(1), tile(D), tile(4 * D_SGU)],
        out_shape=[jax.ShapeDtypeStruct((S, D), MXU_DTYPE), jax.ShapeDtypeStruct((S, D), STORE_DTYPE),
                   jax.ShapeDtypeStruct((S, D), F32), jax.ShapeDtypeStruct((S, 1), F32),
                   jax.ShapeDtypeStruct((S, D), MXU_DTYPE), jax.ShapeDtypeStruct((S, 4 * D_SGU), STORE_DTYPE)],
        scratch_shapes=[pltpu.VMEM((ts, D_SGU), F32)],
        args=(proj, proj, xn, lg, lb, gate, pool_w, pool_scale, sg, sb, wm, bias_full, w_out_all, lg1, lb1, shift2,
              scale2), exchange=exchange)


def _conv_forward(g_ext, cw_ref, cb_ref):
    gm2 = pltpu.roll(g_ext, 2, 0)[CONV_HALO:]
    gm1 = pltpu.roll(g_ext, 1, 0)[CONV_HALO:]
    g0 = g_ext[CONV_HALO:]
    gc = ((cb_ref[...] + gm2 * cw_ref[0:1, :]) + gm1 * cw_ref[1:2, :]) + g0 * cw_ref[2:3, :]
    return gc, gm2, gm1


def _f4_ffn(up, xhat1, lg, lb, gate, conv_w, conv_b, w_down_all, lg2, lb2, shift_next, scale_next, layer,
            exchange=None):
    S = xhat1.shape[0]
    ts = min(TS_FF, S)
    halo_blocks = ts // CONV_HALO

    def body(g_ref, halo_ref, val_ref, xh_ref, lg_ref, lb_ref, gate_ref, cw_ref, cb_ref, wd_ref, lg2_ref, lb2_ref,
             shn_ref, scn_ref, act_ref, f_ref, xo_ref, rs_ref, hn_ref, ge_ref, dge_ref):
        i = pl.program_id(0)
        halo = halo_ref[...].astype(GELU_DTYPE) * (i > 0).astype(GELU_DTYPE)
        g_ext = jnp.concatenate([halo, g_ref[...].astype(GELU_DTYPE)], axis=0)
        cw = cw_ref[...].astype(GELU_DTYPE)
        gm2 = pltpu.roll(g_ext, 2, 0)[CONV_HALO:]
        gm1 = pltpu.roll(g_ext, 1, 0)[CONV_HALO:]
        gc = ((cb_ref[...].astype(GELU_DTYPE) + gm2 * cw[0:1, :]) + gm1 * cw[1:2, :]) + g_ext[CONV_HALO:] * cw[2:3, :]
        ge, dge = _gelu(gc)
        ge_ref[...] = ge.astype(STORE_DTYPE)
        dge_ref[...] = dge.astype(STORE_DTYPE)
        act = (ge * val_ref[...].astype(GELU_DTYPE)).astype(MXU_DTYPE)
        act_ref[...] = act
        f = _dot(act, wd_ref[...])
        f_ref[...] = f.astype(STORE_DTYPE)
        x = xh_ref[...] * lg_ref[...] + lb_ref[...]
        z = ALPHA * x + gate_ref[...] * f
        mu = jnp.mean(z, axis=-1, keepdims=True)
        zc = z - mu
        var = jnp.mean(zc * zc, axis=-1, keepdims=True)
        rs = lax.rsqrt(var + LN_EPS)
        xhat = zc * rs
        xo_ref[...] = xhat
        rs_ref[...] = rs
        x2 = xhat * lg2_ref[...] + lb2_ref[...]
        hn_ref[...] = (x2 * (1.0 + scn_ref[...]) + shn_ref[...]).astype(MXU_DTYPE)

    tile = lambda w: pl.BlockSpec((ts, w), lambda i: (i, 0))
    return _call(
        body, name=f"f4_ffn_l{layer}", grid=(S // ts,),
        in_specs=[pl.BlockSpec((ts, D_FF), lambda i: (i, 0)),
                  pl.BlockSpec((CONV_HALO, D_FF), lambda i: (jnp.maximum(i * halo_blocks - 1, 0), 0)),
                  pl.BlockSpec((ts, D_FF), lambda i: (i, 1)),
                  tile(D), _row(D), _row(D), _row(D), _full((3, D_FF)), _row(D_FF),
                  _layer_block((D_FF, D), layer), _row(D), _row(D), _row(D), _row(D)],
        out_specs=[tile(D_FF), tile(D), tile(D), tile(1), tile(D), tile(D_FF), tile(D_FF)],
        out_shape=[jax.ShapeDtypeStruct((S, D_FF), MXU_DTYPE), jax.ShapeDtypeStruct((S, D), STORE_DTYPE),
                   jax.ShapeDtypeStruct((S, D), F32), jax.ShapeDtypeStruct((S, 1), F32),
                   jax.ShapeDtypeStruct((S, D), MXU_DTYPE), jax.ShapeDtypeStruct((S, D_FF), STORE_DTYPE),
                   jax.ShapeDtypeStruct((S, D_FF), STORE_DTYPE)],
        args=(up, up, up, xhat1, lg, lb, gate, conv_w, conv_b, w_down_all, lg2, lb2, shift_next, scale_next),
        exchange=exchange)


def _accumulate(i, ref, value):
    @pl.when(i == 0)
    def _():
        ref[...] = value

    @pl.when(i > 0)
    def _():
        ref[...] += value


def _b_ln(d, xhat, rstd, f, lg, lb, gate, name, modulated=None, loss_head=False, exchange=None):
    S = d.shape[0]
    ts = min(TS_PROJ, S)
    has_mod = modulated is not None
    assert not (has_mod and loss_head)

    def body(*refs):
        if has_mod:
            (d_ref, xh_ref, rs_ref, f_ref, lg_ref, lb_ref, gate_ref, dri_ref, sc_ref,
             df_ref, dres_ref, dlg_ref, dlb_ref, dgate_ref, dsc_ref, dsh_ref) = refs
        elif loss_head:
            (d_ref, xh_ref, rs_ref, f_ref, lg_ref, lb_ref, gate_ref,
             df_ref, dres_ref, dlg_ref, dlb_ref, dgate_ref, sq_ref) = refs
        else:
            (d_ref, xh_ref, rs_ref, f_ref, lg_ref, lb_ref, gate_ref,
             df_ref, dres_ref, dlg_ref, dlb_ref, dgate_ref) = refs
        i = pl.program_id(0)
        xh = xh_ref[...]
        if has_mod:
            dh = d_ref[...]
            dxo_t = dh * (1.0 + sc_ref[...]) + dri_ref[...]
            _accumulate(i, dsc_ref, _colsum(dh * (xh * lg_ref[...] + lb_ref[...])))
            _accumulate(i, dsh_ref, _colsum(dh))
        elif loss_head:
            err = (xh * lg_ref[...] + lb_ref[...]) - d_ref[...]
            dxo_t = err * (1.0 / D)
            _accumulate(i, sq_ref, _colsum(err * err))
        else:
            dxo_t = d_ref[...]
        dxh = dxo_t * lg_ref[...]
        m1 = jnp.mean(dxh, axis=-1, keepdims=True)
        m2 = jnp.mean(dxh * xh, axis=-1, keepdims=True)
        dz = rs_ref[...] * (dxh - m1 - xh * m2)
        df_ref[...] = (dz * gate_ref[...]).astype(MXU_DTYPE)
        dres_ref[...] = ALPHA * dz
        _accumulate(i, dlg_ref, _colsum(dxo_t * xh))
        _accumulate(i, dlb_ref, _colsum(dxo_t))
        _accumulate(i, dgate_ref, _colsum(dz * f_ref[...].astype(F32)))

    tile = lambda w: pl.BlockSpec((ts, w), lambda i: (i, 0))
    n_sums = 5 if has_mod else 4 if loss_head else 3
    in_specs = [tile(D), tile(D), tile(1), tile(D), _row(D), _row(D), _row(D)]
    args = [d, xhat, rstd, f, lg, lb, gate]
    if has_mod:
        in_specs += [tile(D), _row(D)]
        args += list(modulated)
    return _call(
        body, name=name, grid=(S // ts,), in_specs=in_specs,
        out_specs=[tile(D), tile(D)] + [_row(D)] * n_sums,
        out_shape=[jax.ShapeDtypeStruct((S, D), MXU_DTYPE), jax.ShapeDtypeStruct((S, D), F32)]
        + [jax.ShapeDtypeStruct((1, D), F32)] * n_sums,
        args=args, exchange=exchange)


def _input_grad(dh, dres, scale, x, exchange=None):
    S = dh.shape[0]
    ts = min(TS_PROJ, S)

    def body(dh_ref, dres_ref, sc_ref, x_ref, dx_ref, dsc_ref, dsh_ref):
        i = pl.program_id(0)
        dh_t = dh_ref[...]
        dx_ref[...] = dh_t * (1.0 + sc_ref[...]) + dres_ref[...]
        _accumulate(i, dsc_ref, _colsum(dh_t * x_ref[...]))
        _accumulate(i, dsh_ref, _colsum(dh_t))

    tile = pl.BlockSpec((ts, D), lambda i: (i, 0))
    return _call(
        body, name="input_grad", grid=(S // ts,), in_specs=[tile, tile, _row(D), tile],
        out_specs=[tile, _row(D), _row(D)],
        out_shape=[jax.ShapeDtypeStruct((S, D), F32)] + [jax.ShapeDtypeStruct((1, D), F32)] * 2,
        args=(dh, dres, scale, x), exchange=exchange)


def _b2_ffn(df2, w_down_all, up, ge, dge, conv_w, layer, exchange=None):
    S = df2.shape[0]
    ts = min(TS_FF, S)
    n_tiles = S // ts
    n_ext = ts + CONV_HALO

    def body(df_ref, wd_ref, g_ref, val_ref, ge_ref, dge_ref, cw_ref, dup_ref, dcw_ref, dcb_ref, next_dgc):
        i = pl.program_id(0)

        @pl.when(i == 0)
        def _():
            next_dgc[...] = jnp.zeros_like(next_dgc)

        dact = _dot_nt(df_ref[...], wd_ref[...]).astype(GELU_DTYPE)
        dup_ref[:, D_FF:2 * D_FF] = (dact * ge_ref[...].astype(GELU_DTYPE)).astype(MXU_DTYPE)
        dgc = (dact * val_ref[...].astype(GELU_DTYPE) * dge_ref[...].astype(GELU_DTYPE)).astype(F32)
        ext = jnp.concatenate([dgc, next_dgc[...]], axis=0)
        dp1 = pltpu.roll(ext, n_ext - 1, 0)[:ts]
        dp2 = pltpu.roll(ext, n_ext - 2, 0)[:ts]
        dup_ref[:, 0:D_FF] = (dgc * cw_ref[2:3, :] + dp1 * cw_ref[1:2, :] + dp2 * cw_ref[0:1, :]).astype(MXU_DTYPE)
        next_dgc[...] = dgc[0:CONV_HALO]
        g = g_ref[...].astype(F32)
        dcw = jnp.concatenate([_colsum(dp2 * g), _colsum(dp1 * g), _colsum(dgc * g)], axis=0)
        _accumulate(i, dcw_ref, dcw)
        _accumulate(i, dcb_ref, _colsum(dgc))

    tile = lambda w, col=0: pl.BlockSpec((ts, w), lambda i: (n_tiles - 1 - i, col))
    return _call(
        body, name=f"b2_ffn_l{layer}", grid=(n_tiles,),
        in_specs=[tile(D), _layer_block((D_FF, D), layer), tile(D_FF), tile(D_FF, 1), tile(D_FF), tile(D_FF),
                  _full((3, D_FF))],
        out_specs=[tile(2 * D_FF), _full((3, D_FF)), _row(D_FF)],
        out_shape=[jax.ShapeDtypeStruct((S, 2 * D_FF), MXU_DTYPE),
                   jax.ShapeDtypeStruct((3, D_FF), F32), jax.ShapeDtypeStruct((1, D_FF), F32)],
        scratch_shapes=[pltpu.VMEM((CONV_HALO, D_FF), F32)],
        args=(df2, w_down_all, up, up, ge, dge, conv_w), exchange=exchange)


def _b5_mixers(df1, w_out_all, proj, gel, pool_w, pool_scale, sg, sb, wm, bias_full, layer):
    S = df1.shape[0]
    ts = min(TS_MIX, S)
    n_chunks = ts // CHUNK
    halo_blocks = ts // POOL_HALO
    last_halo = S // POOL_HALO - 1

    def body(df_ref, dfh_ref, wo_ref, p_ref, ah_ref, gel_ref, pw_ref, ps_ref, sg_ref, sb_ref, wm_ref, bias_ref,
             dp_ref, dpw_ref, dps_ref, dsg_ref, dsb_ref, dwm_ref, dbias_ref, z_scr, dvn_scr):
        i = pl.program_id(0)
        last = pl.num_programs(0) - 1
        dmix = _dot_nt(df_ref[...], wo_ref[...])
        dmix_halo = _dot_nt(dfh_ref[...], wo_ref[0:D_POOL, :]) * (i < last).astype(F32)

        a = p_ref[:, 0:D_POOL].astype(F32)
        halo = ah_ref[...].astype(F32) * (i > 0).astype(F32)
        pooled = _pool_forward(jnp.concatenate([halo, a], axis=0), a, i * ts, ts)
        n = ts + POOL_HALO
        dps_parts = []
        for g, window in enumerate(POOL_WINDOWS):
            cols = slice(g * GROUP, (g + 1) * GROUP)
            pooled_b = pooled[g].astype(MXU_DTYPE)
            mixed = _dot(pooled_b, pw_ref[g])
            dya = dmix[:, cols]
            dps_parts.append(_colsum(dya * mixed))
            dmixed = (dya * ps_ref[:, cols]).astype(MXU_DTYPE)
            dpw_g = _dot_tn(pooled_b, dmixed)

            @pl.when(i == 0)
            def _():
                dpw_ref[g] = dpw_g

            @pl.when(i > 0)
            def _():
                dpw_ref[g] += dpw_g

            dpooled = _dot_nt(dmixed, pw_ref[g])
            dmixed_h = (dmix_halo[:, cols] * ps_ref[:, cols]).astype(MXU_DTYPE)
            dpooled_h = _dot_nt(dmixed_h, pw_ref[g])
            q = dpooled * _pool_counts(i * ts, ts, window)
            s = jnp.concatenate([q, dpooled_h * (1.0 / window)], axis=0)
            k = 1
            while k < window:
                s = s + pltpu.roll(s, n - k, 0)
                k *= 2
            dp_ref[:, cols] = (s[:ts] - dpooled).astype(MXU_DTYPE)
        _accumulate(i, dps_ref, jnp.concatenate(dps_parts, axis=1))

        gu, dgu, gv, dgv = (gel_ref[:, k * D_SGU:(k + 1) * D_SGU].astype(F32) for k in range(4))
        vhat, rs = _sgu_norm(gv)
        vn = (vhat * sg_ref[...] + sb_ref[...]).astype(MXU_DTYPE)
        dyb = dmix[:, D_POOL:D]
        dz = dyb * gu
        dzb = dz.astype(MXU_DTYPE)
        dbias = dz[0:CHUNK]
        for c in range(1, n_chunks):
            dbias = dbias + dz[c * CHUNK:(c + 1) * CHUNK]
        _accumulate(i, dbias_ref, dbias)
        for h in range(HEADS):
            cols = slice(h * GROUP, (h + 1) * GROUP)
            vn_h = jnp.concatenate([vn[c * CHUNK:(c + 1) * CHUNK, cols] for c in range(n_chunks)], axis=1)
            dz_h = jnp.concatenate([dzb[c * CHUNK:(c + 1) * CHUNK, cols] for c in range(n_chunks)], axis=1)
            z_h = _dot(wm_ref[h], vn_h)
            dvn_h = _dot_tn(wm_ref[h], dz_h)
            dwm_h = _dot_nt(dz_h, vn_h)
            for c in range(n_chunks):
                rows = slice(c * CHUNK, (c + 1) * CHUNK)
                z_scr[rows, cols] = z_h[:, c * GROUP:(c + 1) * GROUP] + bias_ref[:, cols]
                dvn_scr[rows, cols] = dvn_h[:, c * GROUP:(c + 1) * GROUP]

            @pl.when(i == 0)
            def _():
                dwm_ref[h] = dwm_h

            @pl.when(i > 0)
            def _():
                dwm_ref[h] += dwm_h

        dp_ref[:, D_POOL:D_POOL + D_SGU] = (dyb * z_scr[...] * dgu).astype(MXU_DTYPE)
        dvn = dvn_scr[...]
        _accumulate(i, dsg_ref, _colsum(dvn * vhat))
        _accumulate(i, dsb_ref, _colsum(dvn))
        dvh = dvn * sg_ref[...]
        m1 = jnp.mean(dvh, axis=-1, keepdims=True)
        m2 = jnp.mean(dvh * vhat, axis=-1, keepdims=True)
        dp_ref[:, D_POOL + D_SGU:D_IN] = (rs * (dvh - m1 - vhat * m2) * dgv).astype(MXU_DTYPE)

        @pl.when(i == last)
        def _():
            tri = (lax.broadcasted_iota(jnp.int32, (CHUNK, CHUNK), 0)
                   >= lax.broadcasted_iota(jnp.int32, (CHUNK, CHUNK), 1))
            for h in range(HEADS):
                dwm_ref[h] = jnp.where(tri, dwm_ref[h], 0.0)

    tile = lambda w: pl.BlockSpec((ts, w), lambda i: (i, 0))
    return pl.pallas_call(
        body, name=f"b5_mixers_l{layer}", grid=(S // ts,),
        in_specs=[tile(D),
                  pl.BlockSpec((POOL_HALO, D), lambda i: (jnp.minimum((i + 1) * halo_blocks, last_halo), 0)),
                  _layer_block((D, D), layer), tile(D_POOL),
                  pl.BlockSpec((POOL_HALO, D_POOL), lambda i: (jnp.maximum(i * halo_blocks - 1, 0), 0)),
                  tile(4 * D_SGU),
                  _full((N_GROUPS, GROUP, GROUP)), _row(D_POOL), _row(D_SGU), _row(D_SGU),
                  _full((HEADS, CHUNK, CHUNK)), _full((CHUNK, D_SGU))],
        out_specs=[tile(D_IN), _full((N_GROUPS, GROUP, GROUP)), _row(D_POOL), _row(D_SGU), _row(D_SGU),
                   _full((HEADS, CHUNK, CHUNK)), _full((CHUNK, D_SGU))],
        out_shape=[jax.ShapeDtypeStruct((S, D_IN), MXU_DTYPE), jax.ShapeDtypeStruct((N_GROUPS, GROUP, GROUP), F32),
                   jax.ShapeDtypeStruct((1, D_POOL), F32), jax.ShapeDtypeStruct((1, D_SGU), F32),
                   jax.ShapeDtypeStruct((1, D_SGU), F32), jax.ShapeDtypeStruct((HEADS, CHUNK, CHUNK), F32),
                   jax.ShapeDtypeStruct((CHUNK, D_SGU), F32)],
        scratch_shapes=[pltpu.VMEM((ts, D_SGU), F32), pltpu.VMEM((ts, D_SGU), F32)],
        compiler_params=_params(1),
    )(df1, df1, w_out_all, proj, proj, gel, pool_w, pool_scale, sg, sb, wm, bias_full)


def _weight_grad(a, bs, tn, name, rows=TS_TN):
    S, M = a.shape
    counts = [b.shape[1] // tn for b in bs]
    starts = [sum(counts[:k]) for k in range(len(bs))]
    ts = min(rows, S)

    def body(a_ref, *refs):
        b_refs, o_ref = refs[:-1], refs[-1]
        j = pl.program_id(0)
        for b_ref, lo, n in zip(b_refs, starts, counts):
            @pl.when(jnp.logical_and(j >= lo, j < lo + n))
            def _():
                _accumulate(pl.program_id(1), o_ref, _dot_tn(a_ref[...], b_ref[...]))

    b_spec = lambda lo, n: pl.BlockSpec((ts, tn), lambda j, i: (i, jnp.clip(j - lo, 0, n - 1)))
    return pl.pallas_call(
        body, name=name, grid=(sum(counts), S // ts),
        in_specs=[pl.BlockSpec((ts, M), lambda j, i: (i, 0))] + [b_spec(lo, n) for lo, n in zip(starts, counts)],
        out_specs=pl.BlockSpec((M, tn), lambda j, i: (0, j)),
        out_shape=jax.ShapeDtypeStruct((M, sum(counts) * tn), F32), compiler_params=_params(2),
    )(a, *bs)


def _silu(x):
    return x * (1.0 / (1.0 + jnp.exp(-x)))


def _ada_forward(c_all, ada_w, ada_b_cols):
    n_cols = ada_w.shape[2]
    tc = 512

    def body(c_ref, w_ref, b_ref, o_ref):
        ca = _silu(c_ref[...]).astype(MXU_DTYPE)
        o_ref[...] = _dot(ca, w_ref[...].astype(MXU_DTYPE)) + b_ref[...]

    return pl.pallas_call(
        body, name="ada_forward", grid=(DEPTH, n_cols // tc),
        in_specs=[pl.BlockSpec((16, D), lambda l, j: (0, 0)), pl.BlockSpec((None, D, tc), lambda l, j: (l, 0, j)),
                  pl.BlockSpec((None, 1, tc), lambda l, j: (l, 0, j))],
        out_specs=pl.BlockSpec((None, 16, tc), lambda l, j: (l, 0, j)),
        out_shape=jax.ShapeDtypeStruct((DEPTH, 16, n_cols), F32),
        compiler_params=_params(2),
    )(c_all, ada_w, ada_b_cols)


def _ada_backward(c_all, dmod_cols):
    n_cols = dmod_cols.shape[2]
    tc = 512

    def body(c_ref, d_ref, o_ref):
        ca = _silu(c_ref[...]).astype(MXU_DTYPE)
        o_ref[...] = _dot_tn(ca, d_ref[...].astype(MXU_DTYPE))

    return pl.pallas_call(
        body, name="ada_backward", grid=(DEPTH, n_cols // tc),
        in_specs=[pl.BlockSpec((16, D), lambda l, j: (0, 0)), pl.BlockSpec((None, 16, tc), lambda l, j: (l, 0, j))],
        out_specs=pl.BlockSpec((None, D, tc), lambda l, j: (l, 0, j)),
        out_shape=jax.ShapeDtypeStruct((DEPTH, D, n_cols), F32),
        compiler_params=_params(2),
    )(c_all, dmod_cols)


def _adamw(w, g, m, v, name):
    R, C = w.shape
    tr = R
    for cand in (512, 256, 128, 64, 32, 16, 8):
        if R % cand == 0 and cand * C * 4 <= 2 ** 21:
            tr = cand
            break
    c1 = 1.0 - ADAM_B1 ** ADAM_STEP
    c2 = 1.0 - ADAM_B2 ** ADAM_STEP

    def body(w_ref, g_ref, m_ref, v_ref, d_ref, mo_ref, vo_ref):
        gg = g_ref[...]
        mn = ADAM_B1 * m_ref[...] + (1.0 - ADAM_B1) * gg
        vn = ADAM_B2 * v_ref[...] + (1.0 - ADAM_B2) * (gg * gg)
        mo_ref[...] = mn
        vo_ref[...] = vn
        d_ref[...] = -ADAM_LR * ((mn / c1) / (jnp.sqrt(vn / c2) + ADAM_EPS) + ADAM_WD * w_ref[...])

    tile = pl.BlockSpec((tr, C), lambda i: (i, 0))
    return pl.pallas_call(
        body, name=name, grid=(R // tr,), in_specs=[tile] * 4, out_specs=[tile] * 3,
        out_shape=[jax.ShapeDtypeStruct((R, C), F32)] * 3, compiler_params=_params(1),
    )(w, g, m, v)


def _position():
    x, y, c = lax.axis_index("x"), lax.axis_index("y"), lax.axis_index("c")
    other_chips = [(1 - x, y), (x, 1 - y), (1 - x, 1 - y)]
    return x, y, c, other_chips


def _all_gather8(block, name):
    R, C = block.shape

    def body(x_ref, out_ref, send_sems, recv_sems, local_sem):
        x, y, c, chips = _position()
        me, sibling = (x, y, c), (x, y, 1 - c)

        def rows(px, py, pc):
            return out_ref.at[pl.ds((4 * px + 2 * py + pc) * R, R), :]

        def copy(k, blk, to, src=None):
            return pltpu.make_async_remote_copy(
                src_ref=rows(*blk) if src is None else src, dst_ref=rows(*blk),
                send_sem=send_sems.at[k], recv_sem=recv_sems.at[k], device_id=to, device_id_type=MESH)

        mine = pltpu.make_async_copy(x_ref, rows(*me), local_sem)
        mine.start()
        first = [copy(0, me, sibling, src=x_ref)]
        first += [copy(1 + j, me, (*chip, c), src=x_ref) for j, chip in enumerate(chips)]
        for cp in first:
            cp.start()
        passed = [copy(4 + j, (*chip, c), sibling) for j, chip in enumerate(chips)]
        for j, chip in enumerate(chips):
            copy(1 + j, (*chip, c), me).wait_recv()
            passed[j].start()
        copy(0, sibling, me).wait_recv()
        for j, chip in enumerate(chips):
            copy(4 + j, (*chip, 1 - c), me).wait_recv()
        for cp in first + passed:
            cp.wait_send()
        mine.wait()

    return pl.pallas_call(
        body, name=name, out_shape=jax.ShapeDtypeStruct((N_DEV * R, C), block.dtype),
        in_specs=[pl.BlockSpec(memory_space=pltpu.VMEM)], out_specs=ANY,
        scratch_shapes=[pltpu.SemaphoreType.DMA((7,)), pltpu.SemaphoreType.DMA((7,)), pltpu.SemaphoreType.DMA(())],
    )(block)


def _gather_flat(vec, name):
    n = vec.shape[0]
    padded = -(-n // 1024) * 1024
    block = jnp.pad(vec, (0, padded - n)).reshape(8, padded // 8)
    out = _all_gather8(block, name)
    return out.reshape(N_DEV, padded)[:, :n]


_SHARD_KINDS = ("cols", "rows", "cols", "rows")


def _shard_of(ref, shape, kind, chip):
    m, n = shape
    return ref.at[:, pl.ds(chip * (n // 4), n // 4)] if kind == "cols" else ref.at[pl.ds(chip * (m // 4), m // 4), :]


def _half_of(ref, shape, kind, h):
    m, n = shape
    return ref.at[pl.ds(h * (m // 2), m // 2), :] if kind == "cols" else ref.at[:, pl.ds(h * (n // 2), n // 2)]


def _half_shape(shape, kind):
    m, n = shape
    return (m // 2, n) if kind == "cols" else (m, n // 2)


def _shard_shape(shape, kind):
    m, n = shape
    return (m, n // 4) if kind == "cols" else (m // 4, n)


def _remote(src, dst, k, to, send_sems, recv_sems):
    return pltpu.make_async_remote_copy(src_ref=src, dst_ref=dst, send_sem=send_sems.at[k], recv_sem=recv_sems.at[k],
                                        device_id=to, device_id_type=MESH)


def _gather_exchange(shards, full, kind, layer):
    shape = full.shape[1:]

    def copies(xin, xout, send_sems, recv_sems):
        x, y, c, chips = _position()
        my_chip = 2 * x + y
        own = xin[0].at[layer]
        place = lambda chip: _shard_of(xout[0].at[layer], shape, kind, chip)
        peers = [(x, y, 1 - c)] + [(*chip, c) for chip in chips]
        sources = [my_chip] + [2 * cx + cy for cx, cy in chips]
        sends = [_remote(own, place(my_chip), k, peer, send_sems, recv_sems) for k, peer in enumerate(peers)]
        arrivals = [_remote(place(s), place(s), k, (x, y, c), send_sems, recv_sems) for k, s in enumerate(sources)]
        return sends, arrivals

    return _Exchange([shards, full], [1], [], 4, copies)


def _swap_exchange(grads):
    shapes = [g.shape for g in grads]
    fresh = [jax.ShapeDtypeStruct(_half_shape(s, k), F32) for s, k in zip(shapes, _SHARD_KINDS)]

    def copies(xin, xout, send_sems, recv_sems):
        x, y, c, _ = _position()
        sends = [_remote(_half_of(xin[a], shapes[a], _SHARD_KINDS[a], 1 - c), xout[a], a, (x, y, 1 - c),
                         send_sems, recv_sems) for a in range(len(grads))]
        arrivals = [_remote(xout[a], xout[a], a, (x, y, c), send_sems, recv_sems) for a in range(len(grads))]
        return sends, arrivals

    return _Exchange(list(grads), [], fresh, len(grads), copies)


def _scatter_exchange(partials):
    n_w = len(partials)
    shapes = [p.shape for p in partials]
    fresh = [jax.ShapeDtypeStruct((3,) + _shard_shape(s, k), p.dtype) for s, k, p in zip(shapes, _SHARD_KINDS, partials)]

    def copies(xin, xout, send_sems, recv_sems):
        x, y, c, chips = _position()
        sends, arrivals = [], []
        for j, (cx, cy) in enumerate(chips):
            for a in range(n_w):
                src = _shard_of(xin[a], shapes[a], _SHARD_KINDS[a], 2 * cx + cy)
                sends.append(_remote(src, xout[a].at[j], j * n_w + a, (cx, cy, c), send_sems, recv_sems))
                arrivals.append(_remote(xout[a].at[j], xout[a].at[j], j * n_w + a, (x, y, c), send_sems, recv_sems))
        return sends, arrivals

    return _Exchange(list(partials), [], fresh, 3 * n_w, copies)


def _share_exchange(reduced, layer):
    n_w = len(reduced)
    shapes = [r.shape[1:] for r in reduced]

    def copies(xin, xout, send_sems, recv_sems):
        x, y, c, _ = _position()
        half = lambda a, h: _half_of(xout[a].at[layer], shapes[a], _SHARD_KINDS[a], h)
        sends = [_remote(half(a, c), half(a, c), a, (x, y, 1 - c), send_sems, recv_sems) for a in range(n_w)]
        arrivals = [_remote(half(a, 1 - c), half(a, 1 - c), a, (x, y, c), send_sems, recv_sems) for a in range(n_w)]
        return sends, arrivals

    return _Exchange(list(reduced), list(range(n_w)), [], n_w, copies)


def _chip_partial(pos, grad, theirs, kind, name):
    M, N = grad.shape
    if kind == "cols":
        tm = 256
        steps = M // 2 // tm
        block, g_map = (tm, N), (lambda i, pos: (pos[0] * steps + i, 0))
    else:
        tm = M // 4
        steps = 4
        block, g_map = (tm, N // 2), (lambda i, pos: (i, pos[0]))

    def body(pos_ref, g_ref, t_ref, o_ref):
        o_ref[...] = (g_ref[...] + t_ref[...]).astype(WIRE_DTYPE)

    grid_spec = pltpu.PrefetchScalarGridSpec(
        num_scalar_prefetch=1, grid=(steps,),
        in_specs=[pl.BlockSpec(block, g_map), pl.BlockSpec(block, lambda i, pos: (i, 0))],
        out_specs=pl.BlockSpec(block, lambda i, pos: (i, 0)))
    return pl.pallas_call(
        body, name=name, grid_spec=grid_spec, out_shape=jax.ShapeDtypeStruct(_half_shape((M, N), kind), WIRE_DTYPE),
        compiler_params=_params(1),
    )(pos, grad, theirs)


def _reduce_shard(pos, grad, theirs, received, reduced, kind, layer, name):
    M, N = grad.shape
    if kind == "cols":
        tm = min(M // 2, 512)
        steps = M // 2 // tm
        block = (tm, N // 4)
        g_map = lambda i, pos: (pos[0] * steps + i, pos[1])
        t_map = lambda i, pos: (i, pos[1])
        o_map = lambda i, pos: (layer, pos[0] * steps + i, 0)
    else:
        steps = 1
        block = (M // 4, N // 2)
        g_map = lambda i, pos: (pos[1], pos[0])
        t_map = lambda i, pos: (pos[1], 0)
        o_map = lambda i, pos: (layer, 0, pos[0])
    out_shape = (DEPTH,) + _shard_shape((M, N), kind)

    def body(pos_ref, g_ref, t_ref, r0_ref, r1_ref, r2_ref, *rest):
        o_ref = rest[-1]
        chip = pos_ref[1]
        own = g_ref[...] + t_ref[...]
        r = [r0_ref[...].astype(F32), r1_ref[...].astype(F32), r2_ref[...].astype(F32)]
        total = None
        for s in range(N_CHIPS):
            rel = jnp.bitwise_xor(chip, s)
            term = jnp.where(rel == 0, own, jnp.where(rel == 2, r[0], jnp.where(rel == 1, r[1], r[2])))
            total = term if total is None else total + term
        o_ref[...] = total

    r_spec = lambda j: pl.BlockSpec((None,) + block, lambda i, pos: (j, i, 0))
    in_specs = [pl.BlockSpec(block, g_map), pl.BlockSpec(block, t_map), r_spec(0), r_spec(1), r_spec(2)]
    args = [pos, grad, theirs, received, received, received]
    aliases = {}
    if reduced is not None:
        in_specs.append(ANY)
        args.append(reduced)
        aliases = {6: 0}
    grid_spec = pltpu.PrefetchScalarGridSpec(
        num_scalar_prefetch=1, grid=(steps,), in_specs=in_specs, out_specs=pl.BlockSpec((None,) + block, o_map))
    return pl.pallas_call(
        body, name=name, grid_spec=grid_spec, out_shape=jax.ShapeDtypeStruct(out_shape, F32),
        input_output_aliases=aliases, compiler_params=_params(1),
    )(*args)


def _sum_devices(gathered):
    R8, C = gathered.shape
    R = R8 // N_DEV
    lanes = C // 128
    tc = 128 * max(k for k in range(1, lanes + 1) if lanes % k == 0 and k * 128 * R8 * 4 <= 2 ** 22)

    def body(g_ref, o_ref):
        total = g_ref[0:R, :]
        for d in range(1, N_DEV):
            total = total + g_ref[d * R:(d + 1) * R, :]
        o_ref[...] = total

    return pl.pallas_call(
        body, name="sum_devices", grid=(C // tc,),
        in_specs=[pl.BlockSpec((R8, tc), lambda j: (0, j))], out_specs=pl.BlockSpec((R, tc), lambda j: (0, j)),
        out_shape=jax.ShapeDtypeStruct((R, C), F32), compiler_params=_params(1),
    )(gathered)


_SMALL = ("pool_w", "pool_scale", "sgu_ln_g", "sgu_ln_b", "sgu_w", "sgu_b", "ln1_g", "ln1_b", "conv_w",
          "conv_b", "ln2_g", "ln2_b", "ada_b")
_WEIGHTS = ("ada_w", "ada_b", "w_in", "pool_w", "pool_scale", "sgu_ln_g", "sgu_ln_b", "sgu_w", "sgu_b", "w_out",
            "ln1_g", "ln1_b", "w_up", "conv_w", "conv_b", "w_down", "ln2_g", "ln2_b")


def kernel(x, c, ada_w, ada_b, w_in, pool_w, pool_scale, sgu_ln_g, sgu_ln_b, sgu_w, sgu_b, w_out, ln1_g, ln1_b, w_up, conv_w, conv_b, w_down, ln2_g, ln2_b, loss_target, m_ada_w, m_ada_b, m_w_in, m_pool_w, m_pool_scale, m_sgu_ln_g, m_sgu_ln_b, m_sgu_w, m_sgu_b, m_w_out, m_ln1_g, m_ln1_b, m_w_up, m_conv_w, m_conv_b, m_w_down, m_ln2_g, m_ln2_b, v_ada_w, v_ada_b, v_w_in, v_pool_w, v_pool_scale, v_sgu_ln_g, v_sgu_ln_b, v_sgu_w, v_sgu_b, v_w_out, v_ln1_g, v_ln1_b, v_w_up, v_conv_w, v_conv_b, v_w_down, v_ln2_g, v_ln2_b):
    weights = dict(ada_w=ada_w, ada_b=ada_b, w_in=w_in, pool_w=pool_w, pool_scale=pool_scale, sgu_ln_g=sgu_ln_g,
                   sgu_ln_b=sgu_ln_b, sgu_w=sgu_w, sgu_b=sgu_b, w_out=w_out, ln1_g=ln1_g, ln1_b=ln1_b, w_up=w_up,
                   conv_w=conv_w, conv_b=conv_b, w_down=w_down, ln2_g=ln2_g, ln2_b=ln2_b)
    mom_m = dict(ada_w=m_ada_w, ada_b=m_ada_b, w_in=m_w_in, pool_w=m_pool_w, pool_scale=m_pool_scale,
                 sgu_ln_g=m_sgu_ln_g, sgu_ln_b=m_sgu_ln_b, sgu_w=m_sgu_w, sgu_b=m_sgu_b, w_out=m_w_out,
                 ln1_g=m_ln1_g, ln1_b=m_ln1_b, w_up=m_w_up, conv_w=m_conv_w, conv_b=m_conv_b, w_down=m_w_down,
                 ln2_g=m_ln2_g, ln2_b=m_ln2_b)
    mom_v = dict(ada_w=v_ada_w, ada_b=v_ada_b, w_in=v_w_in, pool_w=v_pool_w, pool_scale=v_pool_scale,
                 sgu_ln_g=v_sgu_ln_g, sgu_ln_b=v_sgu_ln_b, sgu_w=v_sgu_w, sgu_b=v_sgu_b, w_out=v_w_out,
                 ln1_g=v_ln1_g, ln1_b=v_ln1_b, w_up=v_w_up, conv_w=v_conv_w, conv_b=v_conv_b, w_down=v_w_down,
                 ln2_g=v_ln2_g, ln2_b=v_ln2_b)

    ix, iy, ic = lax.axis_index("x"), lax.axis_index("y"), lax.axis_index("c")
    chip = 2 * ix + iy
    dev = 4 * ix + 2 * iy + ic
    pos = jnp.stack([ic, chip]).astype(jnp.int32)
    xs = x.reshape(x.shape[1:])
    target = loss_target.reshape(loss_target.shape[1:])
    ff_shard = conv_w.shape[2]
    mod_shard = ada_w.shape[2]

    first = _gather_flat(jnp.concatenate([c.reshape(-1), conv_w.reshape(-1)]), "gather_c_conv")
    c_all = jnp.pad(first[:, :D], ((0, 8), (0, 0)))
    conv_parts = first[0::2, D:].reshape(N_CHIPS, DEPTH, 3, ff_shard)
    conv_full = jnp.transpose(conv_parts, (1, 2, 0, 3)).reshape(DEPTH, 3, D_FF)
    ada_b_cols = lax.dynamic_slice_in_dim(ada_b, chip * mod_shard, mod_shard, axis=1).reshape(DEPTH, 1, mod_shard)
    mod_part = _ada_forward(c_all, ada_w, ada_b_cols)[:, :8, :]
    mod_all = _gather_flat(mod_part.reshape(-1), "gather_mod").reshape(N_DEV, DEPTH, 8, mod_shard)
    mod_mine = lax.dynamic_index_in_dim(mod_all[0::2], dev, axis=2, keepdims=False)
    mod = jnp.transpose(mod_mine, (1, 0, 2)).reshape(DEPTH, 6, 1, D)

    sh_in, sh_out, sh_up, sh_down = (w.astype(WIRE_DTYPE) for w in (w_in, w_out, w_up, w_down))
    wf_in = lax.empty((DEPTH, D, D_IN), WIRE_DTYPE)
    wf_out = lax.empty((DEPTH, D, D), WIRE_DTYPE)
    wf_up = lax.empty((DEPTH, D, 2 * D_FF), WIRE_DTYPE)
    wf_down = lax.empty((DEPTH, D_FF, D), WIRE_DTYPE)
    (wf_in,) = _run_exchange(_gather_exchange(sh_in, wf_in, "cols", 0), "gather_w_in_l0")

    tri = jnp.tril(jnp.ones((CHUNK, CHUNK), dtype=bool))
    ones_row = jnp.ones((1, D), F32)
    zeros_row = jnp.zeros((1, D), F32)
    row = lambda a, l: a[l].reshape(1, -1)

    saved = []
    xn, lg, lb = xs, ones_row, zeros_row
    h = _modulate_input(xs, mod[0, 0], mod[0, 1])
    for l in range(DEPTH):
        shift1, scale1, gate1, shift2, scale2, gate2 = (mod[l, k] for k in range(6))
        nxt = min(l + 1, DEPTH - 1)
        pw = pool_w[l].astype(MXU_DTYPE)
        wm = jnp.where(tri[None], sgu_w[l], 0.0).astype(MXU_DTYPE)
        bias_full = jnp.repeat(jnp.transpose(sgu_b[l]), GROUP, axis=1)
        lg1, lb1 = row(ln1_g, l), row(ln1_b, l)
        proj, (wf_out,) = _matmul(h, wf_in, l, D_IN, STORE_DTYPE, f"f1_in_proj_l{l}",
                                  exchange=_gather_exchange(sh_out, wf_out, "rows", l))
        (mix, f1, xhat1, rstd1, h2, gel), (wf_up,) = _f2_mixers(
            proj, xn, lg, lb, gate1, pw, row(pool_scale, l), row(sgu_ln_g, l), row(sgu_ln_b, l), wm, bias_full,
            wf_out, lg1, lb1, shift2, scale2, l, exchange=_gather_exchange(sh_up, wf_up, "cols", l))
        up, (wf_down,) = _matmul(h2, wf_up, l, D_FF, STORE_DTYPE, f"f3_up_proj_l{l}",
                                 exchange=_gather_exchange(sh_down, wf_down, "rows", l))
        (act, f2, xhat2, rstd2, h_next, ge, dge), filled = _f4_ffn(
            up, xhat1, lg1, lb1, gate2, conv_full[l], row(conv_b, l), wf_down, row(ln2_g, l), row(ln2_b, l),
            mod[nxt, 0], mod[nxt, 1], l,
            exchange=_gather_exchange(sh_in, wf_in, "cols", l + 1) if l + 1 < DEPTH else None)
        if filled:
            (wf_in,) = filled
        saved.append(dict(xn=xn, lg=lg, lb=lb, h=h, proj=proj, mix=mix, f1=f1, xhat1=xhat1, rstd1=rstd1, h2=h2,
                          up=up, act=act, f2=f2, xhat2=xhat2, rstd2=rstd2, pw=pw, wm=wm, bias_full=bias_full,
                          ge=ge, dge=dge, gel=gel))
        xn, lg, lb, h = xhat2, row(ln2_g, l), row(ln2_b, l), h_next

    d_out = target

    tags = ("in", "out", "up", "down")
    small = {n: [None] * DEPTH for n in _SMALL}
    dmod = [[None] * 6 for _ in range(DEPTH)]
    below = None
    ready = None
    reduced = [None] * 4

    def partial_sums(layer, full, theirs):
        return [_chip_partial(pos, g, t, kind, f"chip_partial_{tag}_l{layer}")
                for g, t, kind, tag in zip(full, theirs, _SHARD_KINDS, tags)]

    def chip_sums(layer, full, theirs, received):
        return [_reduce_shard(pos, g, t, r, out, kind, layer, f"reduce_shard_{tag}_l{layer}")
                for g, t, r, out, kind, tag in zip(full, theirs, received, reduced, _SHARD_KINDS, tags)]

    for l in reversed(range(DEPTH)):
        sv = saved[l]
        scale1, gate1, scale2, gate2 = mod[l, 1], mod[l, 2], mod[l, 4], mod[l, 5]
        lg1, lb1 = row(ln1_g, l), row(ln1_b, l)
        outs, theirs = _b_ln(d_out, sv["xhat2"], sv["rstd2"], sv["f2"], row(ln2_g, l), row(ln2_b, l), gate2,
                             f"b1_ln2_l{l}", modulated=below, loss_head=below is None,
                             exchange=_swap_exchange(ready[1]) if ready else None)
        df2, dres2, dlg2, dlb2, dmod[l][5] = outs[:5]
        if below is not None:
            dmod[l + 1][1], dmod[l + 1][0] = outs[5], outs[6]
        else:
            loss = lax.psum(jnp.sum(outs[5]) * (0.5 / D), ("x", "y", "c"))
        partials = partial_sums(ready[0], ready[1], theirs) if ready else None
        (dup, dcw, dcb), received = _b2_ffn(df2, wf_down, sv["up"], sv["ge"], sv["dge"], conv_full[l], l,
                                            exchange=_scatter_exchange(partials) if ready else None)
        if ready:
            reduced = chip_sums(ready[0], ready[1], theirs, received)
        g_down = _weight_grad(sv["act"], [df2], 512, f"wg_down_l{l}", rows=2 * TS_TN)
        dh2, shared = _matmul_nt(dup, wf_up, l, D_FF, f"b3_up_l{l}",
                                 exchange=_share_exchange(reduced, ready[0]) if ready else None)
        if ready:
            reduced = shared
        g_up = _weight_grad(sv["h2"], [dup], 1408, f"wg_up_l{l}", rows=2 * TS_TN)
        (df1, dres1, dlg1, dlb1, dmod[l][2], dmod[l][4], dmod[l][3]), _ = _b_ln(
            dh2, sv["xhat1"], sv["rstd1"], sv["f1"], lg1, lb1, gate1, f"b4_ln1_l{l}", modulated=(dres2, scale2))
        dproj, dpw, dps, dsg, dsb, dwm, dbias = _b5_mixers(
            df1, wf_out, sv["proj"], sv["gel"], sv["pw"], row(pool_scale, l), row(sgu_ln_g, l), row(sgu_ln_b, l), sv["wm"],
            sv["bias_full"], l)
        g_out = _weight_grad(sv["mix"], [df1], D, f"wg_out_l{l}", rows=2 * TS_TN)
        d_out, _ = _matmul_nt(dproj, wf_in, l, D_IN, f"b6_in_l{l}")
        g_in = _weight_grad(sv["h"], [dproj], 768, f"wg_in_l{l}", rows=2 * TS_TN)
        ready = (l, [g_in, g_out, g_up, g_down])
        below = (dres1, scale1)
        small["pool_w"][l], small["pool_scale"][l] = dpw, dps[0]
        small["sgu_ln_g"][l], small["sgu_ln_b"][l], small["sgu_w"][l] = dsg[0], dsb[0], dwm
        small["sgu_b"][l] = jnp.transpose(jnp.sum(dbias.reshape(CHUNK, HEADS, GROUP), axis=2))
        small["ln1_g"][l], small["ln1_b"][l] = dlg1[0], dlb1[0]
        small["conv_w"][l], small["conv_b"][l] = dcw, dcb[0]
        small["ln2_g"][l], small["ln2_b"][l] = dlg2[0], dlb2[0]
    (grad_x2d, dmod[0][1], dmod[0][0]), _ = _input_grad(d_out, below[0], below[1], xs)
    grad_x = grad_x2d.reshape(x.shape)
    small["ada_b"] = [jnp.concatenate([part[0] for part in dmod[l]]) for l in range(DEPTH)]

    names = _SMALL
    flat = jnp.concatenate([jnp.stack(small[n]).reshape(-1) for n in names])
    n_small = flat.shape[0]
    padded = -(-n_small // 1024) * 1024
    gathered = _all_gather8(jnp.pad(flat, (0, padded - n_small)).reshape(8, padded // 8), "gather_small_grads")
    summed = _sum_devices(gathered).reshape(-1)[:n_small]
    grads = {}
    offset = 0
    for n in names:
        size = math.prod(weights[n].shape[1:]) * DEPTH if n != "conv_w" else DEPTH * 3 * D_FF
        grads[n] = summed[offset:offset + size]
        offset += size
    grads["conv_w"] = lax.dynamic_slice_in_dim(grads["conv_w"].reshape(DEPTH, 3, D_FF), chip * ff_shard, ff_shard,
                                               axis=2)
    for n in names:
        grads[n] = grads[n].reshape(weights[n].shape)
    dmod_all = gathered.reshape(N_DEV, padded)[:, n_small - DEPTH * 6 * D:n_small].reshape(N_DEV, DEPTH, 6 * D)
    dmod_cols = lax.dynamic_slice_in_dim(jnp.transpose(dmod_all, (1, 0, 2)), chip * mod_shard, mod_shard, axis=2)
    grads["ada_w"] = _ada_backward(c_all, jnp.pad(dmod_cols, ((0, 0), (0, 8), (0, 0))))

    theirs = _run_exchange(_swap_exchange(ready[1]), "swap_halves_l0")
    partials = partial_sums(0, ready[1], theirs)
    received = _run_exchange(_scatter_exchange(partials), "scatter_partials_l0")
    reduced = chip_sums(0, ready[1], theirs, received)
    grads["w_in"], grads["w_out"], grads["w_up"], grads["w_down"] = _run_exchange(
        _share_exchange(reduced, 0), "share_reduced_l0")

    delta, new_m, new_v = {}, {}, {}
    for n in ("ada_w", "w_in", "w_out", "w_up", "w_down"):
        shape = weights[n].shape
        two_d = (shape[0] * shape[1], shape[2])
        d_, m_, v_ = _adamw(weights[n].reshape(two_d), grads[n].reshape(two_d), mom_m[n].reshape(two_d),
                            mom_v[n].reshape(two_d), f"adamw_{n}")
        delta[n], new_m[n], new_v[n] = d_.reshape(shape), m_.reshape(shape), v_.reshape(shape)
    sizes = [math.prod(weights[n].shape) for n in names]
    total = sum(sizes)
    padded = -(-total // 1024) * 1024
    pack = lambda d: jnp.pad(jnp.concatenate([d[n].reshape(-1) for n in names]), (0, padded - total)).reshape(8, -1)
    v_pack = jnp.pad(jnp.concatenate([mom_v[n].reshape(-1) for n in names]), (0, padded - total),
                     constant_values=1.0).reshape(8, -1)
    d_, m_, v_ = _adamw(pack(weights), pack(grads), pack(mom_m), v_pack, "adamw_small")
    offset = 0
    for n, size in zip(names, sizes):
        for src, dst in ((d_, delta), (m_, new_m), (v_, new_v)):
            dst[n] = src.reshape(-1)[offset:offset + size].reshape(weights[n].shape)
        offset += size

    return (loss, grad_x, *[grads[n] for n in _WEIGHTS], *[delta[n] for n in _WEIGHTS],
            *[new_m[n] for n in _WEIGHTS], *[new_v[n] for n in _WEIGHTS])
```

```python
import functools
import math

import jax
import jax.numpy as jnp
from jax import lax
from jax.experimental import pallas as pl
from jax.experimental.pallas import tpu as pltpu

F32 = jnp.float32
MXU_DTYPE = jnp.bfloat16
WIRE_DTYPE = jnp.bfloat16

DEPTH = 4
D = 1024
D_POOL = 512
D_SGU = 512
N_GROUPS = 4
GROUP = 128
POOL_WINDOWS = (2, 4, 8, 16)
POOL_HALO = 16
CHUNK = 128
HEADS = 4
D_IN = D_POOL + 2 * D_SGU
D_FF = 2816
CONV_HALO = 16
STORE_DTYPE = jnp.bfloat16
GELU_DTYPE = jnp.bfloat16
N_CHIPS = 4
N_DEV = 8
ALPHA = (2.0 * DEPTH) ** 0.25
LN_EPS = 1e-5
ADAM_LR, ADAM_B1, ADAM_B2, ADAM_EPS, ADAM_WD, ADAM_STEP = 0.001, 0.9, 0.999, 1e-08, 0.01, 10

TS_PROJ = 512
TS_MM = 1024
TS_MIX = 512
TS_FF = 256
TS_TN = 1024
VMEM_LIMIT = 52 * 2 ** 20

MESH = pl.DeviceIdType.MESH
ANY = pl.BlockSpec(memory_space=pl.ANY)

_GELU_K0 = math.sqrt(2.0 / math.pi)
_GELU_K1 = 0.044715


def _params(n_axes):
    return pltpu.CompilerParams(dimension_semantics=("arbitrary",) * n_axes, vmem_limit_bytes=VMEM_LIMIT)


class _Exchange:
    def __init__(self, inputs, inplace, fresh, n_copies, copies):
        self.inputs, self.inplace, self.fresh, self.n_copies, self.copies = inputs, inplace, fresh, n_copies, copies

    def out_shapes(self):
        return [jax.ShapeDtypeStruct(self.inputs[i].shape, self.inputs[i].dtype) for i in self.inplace] + list(self.fresh)

    def semaphores(self):
        return [pltpu.SemaphoreType.DMA((self.n_copies,)), pltpu.SemaphoreType.DMA((self.n_copies,))]

    def aliases(self, first_input, first_output):
        return {first_input + i: first_output + k for k, i in enumerate(self.inplace)}


def _call(body, *, name, grid, in_specs, out_specs, out_shape, args, scratch_shapes=(), exchange=None):
    n_axes = len(grid)
    in_specs, out_specs, out_shape, scratch_shapes = list(in_specs), list(out_specs), list(out_shape), list(scratch_shapes)
    if exchange is None:
        outs = pl.pallas_call(body, name=name, grid=grid, in_specs=in_specs, out_specs=out_specs, out_shape=out_shape,
                              scratch_shapes=scratch_shapes, compiler_params=_params(n_axes))(*args)
        return list(outs), []
    n_in, n_out, n_scr = len(in_specs), len(out_shape), len(scratch_shapes)
    x_out = exchange.out_shapes()
    n_xin, n_xout = len(exchange.inputs), len(x_out)

    def hosted(*refs):
        ins, xin = refs[:n_in], refs[n_in:n_in + n_xin]
        o = n_in + n_xin
        outs, xout = refs[o:o + n_out], refs[o + n_out:o + n_out + n_xout]
        s = o + n_out + n_xout
        scr, (send_sems, recv_sems) = refs[s:s + n_scr], refs[s + n_scr:]
        first = functools.reduce(jnp.logical_and, [pl.program_id(d) == 0 for d in range(n_axes)])
        last = functools.reduce(jnp.logical_and, [pl.program_id(d) == pl.num_programs(d) - 1 for d in range(n_axes)])

        @pl.when(first)
        def _():
            for cp in exchange.copies(xin, xout, send_sems, recv_sems)[0]:
                cp.start()

        body(*ins, *outs, *scr)

        @pl.when(last)
        def _():
            sends, arrivals = exchange.copies(xin, xout, send_sems, recv_sems)
            for cp in arrivals:
                cp.wait_recv()
            for cp in sends:
                cp.wait_send()

    results = pl.pallas_call(
        hosted, name=name, grid=grid, in_specs=in_specs + [ANY] * n_xin, out_specs=out_specs + [ANY] * n_xout,
        out_shape=out_shape + x_out, scratch_shapes=scratch_shapes + exchange.semaphores(),
        input_output_aliases=exchange.aliases(n_in, n_out), compiler_params=_params(n_axes),
    )(*args, *exchange.inputs)
    return list(results[:n_out]), list(results[n_out:])


def _run_exchange(exchange, name):
    x_out = exchange.out_shapes()
    n_xin, n_xout = len(exchange.inputs), len(x_out)

    def body(*refs):
        xin, xout = refs[:n_xin], refs[n_xin:n_xin + n_xout]
        send_sems, recv_sems = refs[n_xin + n_xout:]
        sends, arrivals = exchange.copies(xin, xout, send_sems, recv_sems)
        for cp in sends:
            cp.start()
        for cp in arrivals:
            cp.wait_recv()
        for cp in sends:
            cp.wait_send()

    return list(pl.pallas_call(
        body, name=name, in_specs=[ANY] * n_xin, out_specs=[ANY] * n_xout, out_shape=x_out,
        scratch_shapes=exchange.semaphores(), input_output_aliases=exchange.aliases(0, 0),
    )(*exchange.inputs))


def _dot(a, b):
    return jnp.dot(a, b, preferred_element_type=F32)


def _dot_nt(a, b):
    return lax.dot_general(a, b, (((1,), (1,)), ((), ())), preferred_element_type=F32)


def _dot_tn(a, b):
    return lax.dot_general(a, b, (((0,), (0,)), ((), ())), preferred_element_type=F32)


def _gelu(x):
    x2 = x * x
    t = jnp.tanh(x * (x2 * (_GELU_K0 * _GELU_K1) + _GELU_K0))
    cdf = 0.5 * t + 0.5
    dg = cdf + (x * (1.0 - t * t)) * (x2 * (1.5 * _GELU_K0 * _GELU_K1) + 0.5 * _GELU_K0)
    return x * cdf, dg


def _colsum(x):
    return jnp.sum(x, axis=0, keepdims=True)


def _row(d):
    return pl.BlockSpec((1, d), lambda *_: (0, 0))


def _full(shape):
    n = len(shape)
    return pl.BlockSpec(shape, lambda *_: (0,) * n)


def _layer_block(shape, layer):
    n = len(shape)
    return pl.BlockSpec((None,) + tuple(shape), lambda *_: (layer,) + (0,) * n)


def _pool_counts(first_row, rows, window):
    t = first_row + lax.broadcasted_iota(jnp.int32, (rows, 1), 0)
    return 1.0 / jnp.minimum(t + 1, window).astype(F32)


def _pool_forward(a_ext, a, first_row, rows):
    out = []
    for g, window in enumerate(POOL_WINDOWS):
        s = a_ext[:, g * GROUP:(g + 1) * GROUP]
        k = 1
        while k < window:
            s = s + pltpu.roll(s, k, 0)
            k *= 2
        inv = _pool_counts(first_row, rows, window)
        out.append(s[POOL_HALO:] * inv - a[:, g * GROUP:(g + 1) * GROUP])
    return out


def _sgu_norm(gv):
    mu = jnp.mean(gv, axis=-1, keepdims=True)
    xc = gv - mu
    var = jnp.mean(xc * xc, axis=-1, keepdims=True)
    rs = lax.rsqrt(var + LN_EPS)
    return xc * rs, rs


def _modulate_input(x, shift, scale):
    S = x.shape[0]
    ts = min(TS_PROJ, S)

    def body(x_ref, sh_ref, sc_ref, h_ref):
        h_ref[...] = (x_ref[...] * (1.0 + sc_ref[...]) + sh_ref[...]).astype(MXU_DTYPE)

    tile = pl.BlockSpec((ts, D), lambda i: (i, 0))
    return pl.pallas_call(
        body, name="modulate_input", grid=(S // ts,), in_specs=[tile, _row(D), _row(D)], out_specs=tile,
        out_shape=jax.ShapeDtypeStruct((S, D), MXU_DTYPE), compiler_params=_params(1),
    )(x, shift, scale)


def _matmul(a, w_all, layer, tn, out_dtype, name, exchange=None):
    S, K = a.shape
    N = w_all.shape[2]
    ts = min(TS_MM, S)

    def body(a_ref, w_ref, o_ref):
        o_ref[...] = _dot(a_ref[...], w_ref[...]).astype(out_dtype)

    (out,), extra = _call(
        body, name=name, grid=(N // tn, S // ts),
        in_specs=[pl.BlockSpec((ts, K), lambda j, i: (i, 0)), pl.BlockSpec((None, K, tn), lambda j, i: (layer, 0, j))],
        out_specs=[pl.BlockSpec((ts, tn), lambda j, i: (i, j))],
        out_shape=[jax.ShapeDtypeStruct((S, N), out_dtype)], args=(a, w_all), exchange=exchange)
    return out, extra


def _matmul_nt(a, w_all, layer, tk, name, exchange=None):
    S, K = a.shape
    N = w_all.shape[1]
    ts = min(TS_MM, S)
    n_k = K // tk

    def body(a_ref, w_ref, o_ref, acc):
        k = pl.program_id(1)
        part = _dot_nt(a_ref[...], w_ref[...])
        if n_k == 1:
            o_ref[...] = part
        else:
            @pl.when(k == 0)
            def _():
                acc[...] = part

            @pl.when(jnp.logical_and(k > 0, k < n_k - 1))
            def _():
                acc[...] += part

            @pl.when(k == n_k - 1)
            def _():
                o_ref[...] = acc[...] + part

    (out,), extra = _call(
        body, name=name, grid=(S // ts, n_k),
        in_specs=[pl.BlockSpec((ts, tk), lambda i, k: (i, k)), pl.BlockSpec((None, N, tk), lambda i, k: (layer, 0, k))],
        out_specs=[pl.BlockSpec((ts, N), lambda i, k: (i, 0))],
        out_shape=[jax.ShapeDtypeStruct((S, N), F32)],
        scratch_shapes=[pltpu.VMEM((ts, N), F32)], args=(a, w_all), exchange=exchange)
    return out, extra


def _f2_mixers(proj, xn, lg, lb, gate, pool_w, pool_scale, sg, sb, wm, bias_full, w_out_all, lg1, lb1, shift2, scale2,
               layer, exchange=None):
    S = xn.shape[0]
    ts = min(TS_MIX, S)
    n_chunks = ts // CHUNK
    halo_blocks = ts // POOL_HALO

    def body(p_ref, halo_ref, xn_ref, lg_ref, lb_ref, gate_ref, pw_ref, ps_ref, sg_ref, sb_ref, wm_ref,
             bias_ref, wo_ref, lg1_ref, lb1_ref, sh2_ref, sc2_ref, mix_ref, f_ref, xh_ref, rs_ref, h2_ref, gel_ref,
             z_scr):
        i = pl.program_id(0)
        a = p_ref[:, 0:D_POOL].astype(F32)
        u = p_ref[:, D_POOL:D_POOL + D_SGU].astype(GELU_DTYPE)
        v = p_ref[:, D_POOL + D_SGU:D_IN].astype(GELU_DTYPE)
        halo = halo_ref[...].astype(F32) * (i > 0).astype(F32)
        a_ext = jnp.concatenate([halo, a], axis=0)
        pooled = _pool_forward(a_ext, a, i * ts, ts)
        for g in range(N_GROUPS):
            mixed = _dot(pooled[g].astype(MXU_DTYPE), pw_ref[g])
            mix_ref[:, g * GROUP:(g + 1) * GROUP] = (mixed * ps_ref[:, g * GROUP:(g + 1) * GROUP]).astype(MXU_DTYPE)
        gu, dgu = _gelu(u)
        gv, dgv = _gelu(v)
        for k, part in enumerate((gu, dgu, gv, dgv)):
            gel_ref[:, k * D_SGU:(k + 1) * D_SGU] = part.astype(STORE_DTYPE)
        gu = gu.astype(F32)
        vhat, _ = _sgu_norm(gv.astype(F32))
        vn = (vhat * sg_ref[...] + sb_ref[...]).astype(MXU_DTYPE)
        for h in range(HEADS):
            cols = slice(h * GROUP, (h + 1) * GROUP)
            vn_h = jnp.concatenate([vn[c * CHUNK:(c + 1) * CHUNK, cols] for c in range(n_chunks)], axis=1)
            z_h = _dot(wm_ref[h], vn_h)
            for c in range(n_chunks):
                z_scr[c * CHUNK:(c + 1) * CHUNK, cols] = z_h[:, c * GROUP:(c + 1) * GROUP] + bias_ref[:, cols]
        mix_ref[:, D_POOL:D] = (gu * z_scr[...]).astype(MXU_DTYPE)
        f = _dot(mix_ref[...], wo_ref[...])
        f_ref[...] = f.astype(STORE_DTYPE)
        x = xn_ref[...] * lg_ref[...] + lb_ref[...]
        z1 = ALPHA * x + gate_ref[...] * f
        mu = jnp.mean(z1, axis=-1, keepdims=True)
        zc = z1 - mu
        var = jnp.mean(zc * zc, axis=-1, keepdims=True)
        rs = lax.rsqrt(var + LN_EPS)
        xhat = zc * rs
        xh_ref[...] = xhat
        rs_ref[...] = rs
        x1 = xhat * lg1_ref[...] + lb1_ref[...]
        h2_ref[...] = (x1 * (1.0 + sc2_ref[...]) + sh2_ref[...]).astype(MXU_DTYPE)

    tile = lambda w: pl.BlockSpec((ts, w), lambda i: (i, 0))
    return _call(
        body, name=f"f2_mixers_l{layer}", grid=(S // ts,),
        in_specs=[tile(D_IN),
                  pl.BlockSpec((POOL_HALO, D_POOL), lambda i: (jnp.maximum(i * halo_blocks - 1, 0), 0)),
                  tile(D), _row(D), _row(D), _row(D),
                  _full((N_GROUPS, GROUP, GROUP)), _row(D_POOL), _row(D_SGU), _row(D_SGU),
                  _full((HEADS, CHUNK, CHUNK)), _full((CHUNK, D_SGU)), _layer_block((D, D), layer),
                  _row(D), _row(D), _row(D), _row(D)],
        out_specs=[tile(D), tile(D), tile(D), tile(1), tile(D), tile(4 * D_SGU)],
        out_shape=[jax.ShapeDtypeStruct((S, D), MXU_DTYPE), jax.ShapeDtypeStruct((S, D), STORE_DTYPE),
                   jax.ShapeDtypeStruct((S, D), F32), jax.ShapeDtypeStruct((S, 1), F32),
                   jax.ShapeDtypeStruct((S, D), MXU_DTYPE), jax.ShapeDtypeStruct((S, 4 * D_SGU), STORE_DTYPE)],
        scratch_shapes=[pltpu.VMEM((ts, D_SGU), F32)],
        args=(proj, proj, xn, lg, lb, gate, pool_w, pool_scale, sg, sb, wm, bias_full, w_out_all, lg1, lb1, shift2,
              scale2), exchange=exchange)


def _f4_ffn(up, xhat1, lg, lb, gate, conv_w, conv_b, w_down_all, lg2, lb2, shift_next, scale_next, layer,
            exchange=None):
    S = xhat1.shape[0]
    ts = min(TS_FF, S)
    halo_blocks = ts // CONV_HALO

    def body(g_ref, halo_ref, val_ref, xh_ref, lg_ref, lb_ref, gate_ref, cw_ref, cb_ref, wd_ref, lg2_ref, lb2_ref,
             shn_ref, scn_ref, act_ref, f_ref, xo_ref, rs_ref, hn_ref, ge_ref, dge_ref):
        i = pl.program_id(0)
        halo = halo_ref[...].astype(GELU_DTYPE) * (i > 0).astype(GELU_DTYPE)
        g_ext = jnp.concatenate([halo, g_ref[...].astype(GELU_DTYPE)], axis=0)
        cw = cw_ref[...].astype(GELU_DTYPE)
        gm2 = pltpu.roll(g_ext, 2, 0)[CONV_HALO:]
        gm1 = pltpu.roll(g_ext, 1, 0)[CONV_HALO:]
        gc = ((cb_ref[...].astype(GELU_DTYPE) + gm2 * cw[0:1, :]) + gm1 * cw[1:2, :]) + g_ext[CONV_HALO:] * cw[2:3, :]
        ge, dge = _gelu(gc)
        ge_ref[...] = ge.astype(STORE_DTYPE)
        dge_ref[...] = dge.astype(STORE_DTYPE)
        act = (ge * val_ref[...].astype(GELU_DTYPE)).astype(MXU_DTYPE)
        act_ref[...] = act
        f = _dot(act, wd_ref[...])
        f_ref[...] = f.astype(STORE_DTYPE)
        x = xh_ref[...] * lg_ref[...] + lb_ref[...]
        z = ALPHA * x + gate_ref[...] * f
        mu = jnp.mean(z, axis=-1, keepdims=True)
        zc = z - mu
        var = jnp.mean(zc * zc, axis=-1, keepdims=True)
        rs = lax.rsqrt(var + LN_EPS)
        xhat = zc * rs
        xo_ref[...] = xhat
        rs_ref[...] = rs
        x2 = xhat * lg2_ref[...] + lb2_ref[...]
        hn_ref[...] = (x2 * (1.0 + scn_ref[...]) + shn_ref[...]).astype(MXU_DTYPE)

    tile = lambda w: pl.BlockSpec((ts, w), lambda i: (i, 0))
    return _call(
        body, name=f"f4_ffn_l{layer}", grid=(S // ts,),
        in_specs=[pl.BlockSpec((ts, D_FF), lambda i: (i, 0)),
                  pl.BlockSpec((CONV_HALO, D_FF), lambda i: (jnp.maximum(i * halo_blocks - 1, 0), 0)),
                  pl.BlockSpec((ts, D_FF), lambda i: (i, 1)),
                  tile(D), _row(D), _row(D), _row(D), _full((3, D_FF)), _row(D_FF),
                  _layer_block((D_FF, D), layer), _row(D), _row(D), _row(D), _row(D)],
        out_specs=[tile(D_FF), tile(D), tile(D), tile(1), tile(D), tile(D_FF), tile(D_FF)],
        out_shape=[jax.ShapeDtypeStruct((S, D_FF), MXU_DTYPE), jax.ShapeDtypeStruct((S, D), STORE_DTYPE),
                   jax.ShapeDtypeStruct((S, D), F32), jax.ShapeDtypeStruct((S, 1), F32),
                   jax.ShapeDtypeStruct((S, D), MXU_DTYPE), jax.ShapeDtypeStruct((S, D_FF), STORE_DTYPE),
                   jax.ShapeDtypeStruct((S, D_FF), STORE_DTYPE)],
        args=(up, up, up, xhat1, lg, lb, gate, conv_w, conv_b, w_down_all, lg2, lb2, shift_next, scale_next),
        exchange=exchange)


def _accumulate(i, ref, value):
    @pl.when(i == 0)
    def _():
        ref[...] = value

    @pl.when(i > 0)
    def _():
        ref[...] += value


def _b_ln(d, xhat, rstd, f, lg, lb, gate, name, modulated=None, loss_head=False, exchange=None):
    S = d.shape[0]
    ts = min(TS_PROJ, S)
    has_mod = modulated is not None
    assert not (has_mod and loss_head)

    def body(*refs):
        if has_mod:
            (d_ref, xh_ref, rs_ref, f_ref, lg_ref, lb_ref, gate_ref, dri_ref, sc_ref,
             df_ref, dres_ref, dlg_ref, dlb_ref, dgate_ref, dsc_ref, dsh_ref) = refs
        elif loss_head:
            (d_ref, xh_ref, rs_ref, f_ref, lg_ref, lb_ref, gate_ref,
             df_ref, dres_ref, dlg_ref, dlb_ref, dgate_ref, sq_ref) = refs
        else:
            (d_ref, xh_ref, rs_ref, f_ref, lg_ref, lb_ref, gate_ref,
             df_ref, dres_ref, dlg_ref, dlb_ref, dgate_ref) = refs
        i = pl.program_id(0)
        xh = xh_ref[...]
        if has_mod:
            dh = d_ref[...]
            dxo_t = dh * (1.0 + sc_ref[...]) + dri_ref[...]
            _accumulate(i, dsc_ref, _colsum(dh * (xh * lg_ref[...] + lb_ref[...])))
            _accumulate(i, dsh_ref, _colsum(dh))
        elif loss_head:
            err = (xh * lg_ref[...] + lb_ref[...]) - d_ref[...]
            dxo_t = err * (1.0 / D)
            _accumulate(i, sq_ref, _colsum(err * err))
        else:
            dxo_t = d_ref[...]
        dxh = dxo_t * lg_ref[...]
        m1 = jnp.mean(dxh, axis=-1, keepdims=True)
        m2 = jnp.mean(dxh * xh, axis=-1, keepdims=True)
        dz = rs_ref[...] * (dxh - m1 - xh * m2)
        df_ref[...] = (dz * gate_ref[...]).astype(MXU_DTYPE)
        dres_ref[...] = ALPHA * dz
        _accumulate(i, dlg_ref, _colsum(dxo_t * xh))
        _accumulate(i, dlb_ref, _colsum(dxo_t))
        _accumulate(i, dgate_ref, _colsum(dz * f_ref[...].astype(F32)))

    tile = lambda w: pl.BlockSpec((ts, w), lambda i: (i, 0))
    n_sums = 5 if has_mod else 4 if loss_head else 3
    in_specs = [tile(D), tile(D), tile(1), tile(D), _row(D), _row(D), _row(D)]
    args = [d, xhat, rstd, f, lg, lb, gate]
    if has_mod:
        in_specs += [tile(D), _row(D)]
        args += list(modulated)
    return _call(
        body, name=name, grid=(S // ts,), in_specs=in_specs,
        out_specs=[tile(D), tile(D)] + [_row(D)] * n_sums,
        out_shape=[jax.ShapeDtypeStruct((S, D), MXU_DTYPE), jax.ShapeDtypeStruct((S, D), F32)]
        + [jax.ShapeDtypeStruct((1, D), F32)] * n_sums,
        args=args, exchange=exchange)


def _input_grad(dh, dres, scale, x, exchange=None):
    S = dh.shape[0]
    ts = min(TS_PROJ, S)

    def body(dh_ref, dres_ref, sc_ref, x_ref, dx_ref, dsc_ref, dsh_ref):
        i = pl.program_id(0)
        dh_t = dh_ref[...]
        dx_ref[...] = dh_t * (1.0 + sc_ref[...]) + dres_ref[...]
        _accumulate(i, dsc_ref, _colsum(dh_t * x_ref[...]))
        _accumulate(i, dsh_ref, _colsum(dh_t))

    tile = pl.BlockSpec((ts, D), lambda i: (i, 0))
    return _call(
        body, name="input_grad", grid=(S // ts,), in_specs=[tile, tile, _row(D), tile],
        out_specs=[tile, _row(D), _row(D)],
        out_shape=[jax.ShapeDtypeStruct((S, D), F32)] + [jax.ShapeDtypeStruct((1, D), F32)] * 2,
        args=(dh, dres, scale, x), exchange=exchange)


def _b2_ffn(df2, w_down_all, up, ge, dge, conv_w, layer, exchange=None):
    S = df2.shape[0]
    ts = min(TS_FF, S)
    n_tiles = S // ts
    n_ext = ts + CONV_HALO

    def body(df_ref, wd_ref, g_ref, val_ref, ge_ref, dge_ref, cw_ref, dup_ref, dcw_ref, dcb_ref, next_dgc):
        i = pl.program_id(0)

        @pl.when(i == 0)
        def _():
            next_dgc[...] = jnp.zeros_like(next_dgc)

        dact = _dot_nt(df_ref[...], wd_ref[...]).astype(GELU_DTYPE)
        dup_ref[:, D_FF:2 * D_FF] = (dact * ge_ref[...].astype(GELU_DTYPE)).astype(MXU_DTYPE)
        dgc = (dact * val_ref[...].astype(GELU_DTYPE) * dge_ref[...].astype(GELU_DTYPE)).astype(F32)
        ext = jnp.concatenate([dgc, next_dgc[...]], axis=0)
        dp1 = pltpu.roll(ext, n_ext - 1, 0)[:ts]
        dp2 = pltpu.roll(ext, n_ext - 2, 0)[:ts]
        dup_ref[:, 0:D_FF] = (dgc * cw_ref[2:3, :] + dp1 * cw_ref[1:2, :] + dp2 * cw_ref[0:1, :]).astype(MXU_DTYPE)
        next_dgc[...] = dgc[0:CONV_HALO]
        g = g_ref[...].astype(F32)
        dcw = jnp.concatenate([_colsum(dp2 * g), _colsum(dp1 * g), _colsum(dgc * g)], axis=0)
        _accumulate(i, dcw_ref, dcw)
        _accumulate(i, dcb_ref, _colsum(dgc))

    tile = lambda w, col=0: pl.BlockSpec((ts, w), lambda i: (n_tiles - 1 - i, col))
    return _call(
        body, name=f"b2_ffn_l{layer}", grid=(n_tiles,),
        in_specs=[tile(D), _layer_block((D_FF, D), layer), tile(D_FF), tile(D_FF, 1), tile(D_FF), tile(D_FF),
                  _full((3, D_FF))],
        out_specs=[tile(2 * D_FF), _full((3, D_FF)), _row(D_FF)],
        out_shape=[jax.ShapeDtypeStruct((S, 2 * D_FF), MXU_DTYPE),
                   jax.ShapeDtypeStruct((3, D_FF), F32), jax.ShapeDtypeStruct((1, D_FF), F32)],
        scratch_shapes=[pltpu.VMEM((CONV_HALO, D_FF), F32)],
        args=(df2, w_down_all, up, up, ge, dge, conv_w), exchange=exchange)


def _b5_mixers(df1, w_out_all, proj, gel, pool_w, pool_scale, sg, sb, wm, bias_full, layer):
    S = df1.shape[0]
    ts = min(TS_MIX, S)
    n_chunks = ts // CHUNK
    halo_blocks = ts // POOL_HALO
    last_halo = S // POOL_HALO - 1

    def body(df_ref, dfh_ref, wo_ref, p_ref, ah_ref, gel_ref, pw_ref, ps_ref, sg_ref, sb_ref, wm_ref, bias_ref,
             dp_ref, dpw_ref, dps_ref, dsg_ref, dsb_ref, dwm_ref, dbias_ref, z_scr, dvn_scr):
        i = pl.program_id(0)
        last = pl.num_programs(0) - 1
        dmix = _dot_nt(df_ref[...], wo_ref[...])
        dmix_halo = _dot_nt(dfh_ref[...], wo_ref[0:D_POOL, :]) * (i < last).astype(F32)

        a = p_ref[:, 0:D_POOL].astype(F32)
        halo = ah_ref[...].astype(F32) * (i > 0).astype(F32)
        pooled = _pool_forward(jnp.concatenate([halo, a], axis=0), a, i * ts, ts)
        n = ts + POOL_HALO
        dps_parts = []
        for g, window in enumerate(POOL_WINDOWS):
            cols = slice(g * GROUP, (g + 1) * GROUP)
            pooled_b = pooled[g].astype(MXU_DTYPE)
            mixed = _dot(pooled_b, pw_ref[g])
            dya = dmix[:, cols]
            dps_parts.append(_colsum(dya * mixed))
            dmixed = (dya * ps_ref[:, cols]).astype(MXU_DTYPE)
            dpw_g = _dot_tn(pooled_b, dmixed)

            @pl.when(i == 0)
            def _():
                dpw_ref[g] = dpw_g

            @pl.when(i > 0)
            def _():
                dpw_ref[g] += dpw_g

            dpooled = _dot_nt(dmixed, pw_ref[g])
            dmixed_h = (dmix_halo[:, cols] * ps_ref[:, cols]).astype(MXU_DTYPE)
            dpooled_h = _dot_nt(dmixed_h, pw_ref[g])
            q = dpooled * _pool_counts(i * ts, ts, window)
            s = jnp.concatenate([q, dpooled_h * (1.0 / window)], axis=0)
            k = 1
            while k < window:
                s = s + pltpu.roll(s, n - k, 0)
                k *= 2
            dp_ref[:, cols] = (s[:ts] - dpooled).astype(MXU_DTYPE)
        _accumulate(i, dps_ref, jnp.concatenate(dps_parts, axis=1))

        gu, dgu, gv, dgv = (gel_ref[:, k * D_SGU:(k + 1) * D_SGU].astype(F32) for k in range(4))
        vhat, rs = _sgu_norm(gv)
        vn = (vhat * sg_ref[...] + sb_ref[...]).astype(MXU_DTYPE)
        dyb = dmix[:, D_POOL:D]
        dz = dyb * gu
        dzb = dz.astype(MXU_DTYPE)
        dbias = dz[0:CHUNK]
        for c in range(1, n_chunks):
            dbias = dbias + dz[c * CHUNK:(c + 1) * CHUNK]
        _accumulate(i, dbias_ref, dbias)
        for h in range(HEADS):
            cols = slice(h * GROUP, (h + 1) * GROUP)
            vn_h = jnp.concatenate([vn[c * CHUNK:(c + 1) * CHUNK, cols] for c in range(n_chunks)], axis=1)
            dz_h = jnp.concatenate([dzb[c * CHUNK:(c + 1) * CHUNK, cols] for c in range(n_chunks)], axis=1)
            z_h = _dot(wm_ref[h], vn_h)
            dvn_h = _dot_tn(wm_ref[h], dz_h)
            dwm_h = _dot_nt(dz_h, vn_h)
            for c in range(n_chunks):
                rows = slice(c * CHUNK, (c + 1) * CHUNK)
                z_scr[rows, cols] = z_h[:, c * GROUP:(c + 1) * GROUP] + bias_ref[:, cols]
                dvn_scr[rows, cols] = dvn_h[:, c * GROUP:(c + 1) * GROUP]

            @pl.when(i == 0)
            def _():
                dwm_ref[h] = dwm_h

            @pl.when(i > 0)
            def _():
                dwm_ref[h] += dwm_h

        dp_ref[:, D_POOL:D_POOL + D_SGU] = (dyb * z_scr[...] * dgu).astype(MXU_DTYPE)
        dvn = dvn_scr[...]
        _accumulate(i, dsg_ref, _colsum(dvn * vhat))
        _accumulate(i, dsb_ref, _colsum(dvn))
        dvh = dvn * sg_ref[...]
        m1 = jnp.mean(dvh, axis=-1, keepdims=True)
        m2 = jnp.mean(dvh * vhat, axis=-1, keepdims=True)
        dp_ref[:, D_POOL + D_SGU:D_IN] = (rs * (dvh - m1 - vhat * m2) * dgv).astype(MXU_DTYPE)

        @pl.when(i == last)
        def _():
            tri = (lax.broadcasted_iota(jnp.int32, (CHUNK, CHUNK), 0)
                   >= lax.broadcasted_iota(jnp.int32, (CHUNK, CHUNK), 1))
            for h in range(HEADS):
                dwm_ref[h] = jnp.where(tri, dwm_ref[h], 0.0)

    tile = lambda w: pl.BlockSpec((ts, w), lambda i: (i, 0))
    return pl.pallas_call(
        body, name=f"b5_mixers_l{layer}", grid=(S // ts,),
        in_specs=[tile(D),
                  pl.BlockSpec((POOL_HALO, D), lambda i: (jnp.minimum((i + 1) * halo_blocks, last_halo), 0)),
                  _layer_block((D, D), layer), tile(D_POOL),
                  pl.BlockSpec((POOL_HALO, D_POOL), lambda i: (jnp.maximum(i * halo_blocks - 1, 0), 0)),
                  tile(4 * D_SGU),
                  _full((N_GROUPS, GROUP, GROUP)), _row(D_POOL), _row(D_SGU), _row(D_SGU),
                  _full((HEADS, CHUNK, CHUNK)), _full((CHUNK, D_SGU))],
        out_specs=[tile(D_IN), _full((N_GROUPS, GROUP, GROUP)), _row(D_POOL), _row(D_SGU), _row(D_SGU),
                   _full((HEADS, CHUNK, CHUNK)), _full((CHUNK, D_SGU))],
        out_shape=[jax.ShapeDtypeStruct((S, D_IN), MXU_DTYPE), jax.ShapeDtypeStruct((N_GROUPS, GROUP, GROUP), F32),
                   jax.ShapeDtypeStruct((1, D_POOL), F32), jax.ShapeDtypeStruct((1, D_SGU), F32),
                   jax.ShapeDtypeStruct((1, D_SGU), F32), jax.ShapeDtypeStruct((HEADS, CHUNK, CHUNK), F32),
                   jax.ShapeDtypeStruct((CHUNK, D_SGU), F32)],
        scratch_shapes=[pltpu.VMEM((ts, D_SGU), F32), pltpu.VMEM((ts, D_SGU), F32)],
        compiler_params=_params(1),
    )(df1, df1, w_out_all, proj, proj, gel, pool_w, pool_scale, sg, sb, wm, bias_full)


def _weight_grad(a, bs, tn, name, rows=TS_TN):
    S, M = a.shape
    counts = [b.shape[1] // tn for b in bs]
    starts = [sum(counts[:k]) for k in range(len(bs))]
    ts = min(rows, S)

    def body(a_ref, *refs):
        b_refs, o_ref = refs[:-1], refs[-1]
        j = pl.program_id(0)
        for b_ref, lo, n in zip(b_refs, starts, counts):
            @pl.when(jnp.logical_and(j >= lo, j < lo + n))
            def _():
                _accumulate(pl.program_id(1), o_ref, _dot_tn(a_ref[...], b_ref[...]))

    b_spec = lambda lo, n: pl.BlockSpec((ts, tn), lambda j, i: (i, jnp.clip(j - lo, 0, n - 1)))
    return pl.pallas_call(
        body, name=name, grid=(sum(counts), S // ts),
        in_specs=[pl.BlockSpec((ts, M), lambda j, i: (i, 0))] + [b_spec(lo, n) for lo, n in zip(starts, counts)],
        out_specs=pl.BlockSpec((M, tn), lambda j, i: (0, j)),
        out_shape=jax.ShapeDtypeStruct((M, sum(counts) * tn), F32), compiler_params=_params(2),
    )(a, *bs)


def _silu(x):
    return x * (1.0 / (1.0 + jnp.exp(-x)))


def _ada_forward(c_all, ada_w, ada_b_cols):
    n_cols = ada_w.shape[2]
    tc = 512

    def body(c_ref, w_ref, b_ref, o_ref):
        ca = _silu(c_ref[...]).astype(MXU_DTYPE)
        o_ref[...] = _dot(ca, w_ref[...].astype(MXU_DTYPE)) + b_ref[...]

    return pl.pallas_call(
        body, name="ada_forward", grid=(DEPTH, n_cols // tc),
        in_specs=[pl.BlockSpec((16, D), lambda l, j: (0, 0)), pl.BlockSpec((None, D, tc), lambda l, j: (l, 0, j)),
                  pl.BlockSpec((None, 1, tc), lambda l, j: (l, 0, j))],
        out_specs=pl.BlockSpec((None, 16, tc), lambda l, j: (l, 0, j)),
        out_shape=jax.ShapeDtypeStruct((DEPTH, 16, n_cols), F32),
        compiler_params=_params(2),
    )(c_all, ada_w, ada_b_cols)


def _ada_backward(c_all, dmod_cols):
    n_cols = dmod_cols.shape[2]
    tc = 512

    def body(c_ref, d_ref, o_ref):
        ca = _silu(c_ref[...]).astype(MXU_DTYPE)
        o_ref[...] = _dot_tn(ca, d_ref[...].astype(MXU_DTYPE))

    return pl.pallas_call(
        body, name="ada_backward", grid=(DEPTH, n_cols // tc),
        in_specs=[pl.BlockSpec((16, D), lambda l, j: (0, 0)), pl.BlockSpec((None, 16, tc), lambda l, j: (l, 0, j))],
        out_specs=pl.BlockSpec((None, D, tc), lambda l, j: (l, 0, j)),
        out_shape=jax.ShapeDtypeStruct((DEPTH, D, n_cols), F32),
        compiler_params=_params(2),
    )(c_all, dmod_cols)


def _adamw(w, g, m, v, name):
    R, C = w.shape
    tr = R
    for cand in (512, 256, 128, 64, 32, 16, 8):
        if R % cand == 0 and cand * C * 4 <= 2 ** 21:
            tr = cand
            break
    c1 = 1.0 - ADAM_B1 ** ADAM_STEP
    c2 = 1.0 - ADAM_B2 ** ADAM_STEP

    def body(w_ref, g_ref, m_ref, v_ref, d_ref, mo_ref, vo_ref):
        gg = g_ref[...]
        mn = ADAM_B1 * m_ref[...] + (1.0 - ADAM_B1) * gg
        vn = ADAM_B2 * v_ref[...] + (1.0 - ADAM_B2) * (gg * gg)
        mo_ref[...] = mn
        vo_ref[...] = vn
        d_ref[...] = -ADAM_LR * ((mn / c1) / (jnp.sqrt(vn / c2) + ADAM_EPS) + ADAM_WD * w_ref[...])

    tile = pl.BlockSpec((tr, C), lambda i: (i, 0))
    return pl.pallas_call(
        body, name=name, grid=(R // tr,), in_specs=[tile] * 4, out_specs=[tile] * 3,
        out_shape=[jax.ShapeDtypeStruct((R, C), F32)] * 3, compiler_params=_params(1),
    )(w, g, m, v)


def _position():
    x, y, c = lax.axis_index("x"), lax.axis_index("y"), lax.axis_index("c")
    other_chips = [(1 - x, y), (x, 1 - y), (1 - x, 1 - y)]
    return x, y, c, other_chips


def _all_gather8(block, name):
    R, C = block.shape

    def body(x_ref, out_ref, send_sems, recv_sems, local_sem):
        x, y, c, chips = _position()
        me, sibling = (x, y, c), (x, y, 1 - c)

        def rows(px, py, pc):
            return out_ref.at[pl.ds((4 * px + 2 * py + pc) * R, R), :]

        def copy(k, blk, to, src=None):
            return pltpu.make_async_remote_copy(
                src_ref=rows(*blk) if src is None else src, dst_ref=rows(*blk),
                send_sem=send_sems.at[k], recv_sem=recv_sems.at[k], device_id=to, device_id_type=MESH)

        mine = pltpu.make_async_copy(x_ref, rows(*me), local_sem)
        mine.start()
        first = [copy(0, me, sibling, src=x_ref)]
        first += [copy(1 + j, me, (*chip, c), src=x_ref) for j, chip in enumerate(chips)]
        for cp in first:
            cp.start()
        passed = [copy(4 + j, (*chip, c), sibling) for j, chip in enumerate(chips)]
        for j, chip in enumerate(chips):
            copy(1 + j, (*chip, c), me).wait_recv()
            passed[j].start()
        copy(0, sibling, me).wait_recv()
        for j, chip in enumerate(chips):
            copy(4 + j, (*chip, 1 - c), me).wait_recv()
        for cp in first + passed:
            cp.wait_send()
        mine.wait()

    return pl.pallas_call(
        body, name=name, out_shape=jax.ShapeDtypeStruct((N_DEV * R, C), block.dtype),
        in_specs=[pl.BlockSpec(memory_space=pltpu.VMEM)], out_specs=ANY,
        scratch_shapes=[pltpu.SemaphoreType.DMA((7,)), pltpu.SemaphoreType.DMA((7,)), pltpu.SemaphoreType.DMA(())],
    )(block)


def _gather_flat(vec, name):
    n = vec.shape[0]
    padded = -(-n // 1024) * 1024
    block = jnp.pad(vec, (0, padded - n)).reshape(8, padded // 8)
    out = _all_gather8(block, name)
    return out.reshape(N_DEV, padded)[:, :n]


_SHARD_KINDS = ("cols", "rows", "cols", "rows")


def _shard_of(ref, shape, kind, chip):
    m, n = shape
    return ref.at[:, pl.ds(chip * (n // 4), n // 4)] if kind == "cols" else ref.at[pl.ds(chip * (m // 4), m // 4), :]


def _half_of(ref, shape, kind, h):
    m, n = shape
    return ref.at[pl.ds(h * (m // 2), m // 2), :] if kind == "cols" else ref.at[:, pl.ds(h * (n // 2), n // 2)]


def _half_shape(shape, kind):
    m, n = shape
    return (m // 2, n) if kind == "cols" else (m, n // 2)


def _shard_shape(shape, kind):
    m, n = shape
    return (m, n // 4) if kind == "cols" else (m // 4, n)


def _remote(src, dst, k, to, send_sems, recv_sems):
    return pltpu.make_async_remote_copy(src_ref=src, dst_ref=dst, send_sem=send_sems.at[k], recv_sem=recv_sems.at[k],
                                        device_id=to, device_id_type=MESH)


def _gather_exchange(shards, full, kind, layer):
    shape = full.shape[1:]

    def copies(xin, xout, send_sems, recv_sems):
        x, y, c, chips = _position()
        my_chip = 2 * x + y
        own = xin[0].at[layer]
        place = lambda chip: _shard_of(xout[0].at[layer], shape, kind, chip)
        peers = [(x, y, 1 - c)] + [(*chip, c) for chip in chips]
        sources = [my_chip] + [2 * cx + cy for cx, cy in chips]
        sends = [_remote(own, place(my_chip), k, peer, send_sems, recv_sems) for k, peer in enumerate(peers)]
        arrivals = [_remote(place(s), place(s), k, (x, y, c), send_sems, recv_sems) for k, s in enumerate(sources)]
        return sends, arrivals

    return _Exchange([shards, full], [1], [], 4, copies)


def _swap_exchange(grads):
    shapes = [g.shape for g in grads]
    fresh = [jax.ShapeDtypeStruct(_half_shape(s, k), F32) for s, k in zip(shapes, _SHARD_KINDS)]

    def copies(xin, xout, send_sems, recv_sems):
        x, y, c, _ = _position()
        sends = [_remote(_half_of(xin[a], shapes[a], _SHARD_KINDS[a], 1 - c), xout[a], a, (x, y, 1 - c),
                         send_sems, recv_sems) for a in range(len(grads))]
        arrivals = [_remote(xout[a], xout[a], a, (x, y, c), send_sems, recv_sems) for a in range(len(grads))]
        return sends, arrivals

    return _Exchange(list(grads), [], fresh, len(grads), copies)


def _scatter_exchange(partials):
    n_w = len(partials)
    shapes = [p.shape for p in partials]
    fresh = [jax.ShapeDtypeStruct((3,) + _shard_shape(s, k), p.dtype) for s, k, p in zip(shapes, _SHARD_KINDS, partials)]

    def copies(xin, xout, send_sems, recv_sems):
        x, y, c, chips = _position()
        sends, arrivals = [], []
        for j, (cx, cy) in enumerate(chips):
            for a in range(n_w):
                src = _shard_of(xin[a], shapes[a], _SHARD_KINDS[a], 2 * cx + cy)
                sends.append(_remote(src, xout[a].at[j], j * n_w + a, (cx, cy, c), send_sems, recv_sems))
                arrivals.append(_remote(xout[a].at[j], xout[a].at[j], j * n_w + a, (x, y, c), send_sems, recv_sems))
        return sends, arrivals

    return _Exchange(list(partials), [], fresh, 3 * n_w, copies)


def _share_exchange(reduced, layer):
    n_w = len(reduced)
    shapes = [r.shape[1:] for r in reduced]

    def copies(xin, xout, send_sems, recv_sems):
        x, y, c, _ = _position()
        half = lambda a, h: _half_of(xout[a].at[layer], shapes[a], _SHARD_KINDS[a], h)
        sends = [_remote(half(a, c), half(a, c), a, (x, y, 1 - c), send_sems, recv_sems) for a in range(n_w)]
        arrivals = [_remote(half(a, 1 - c), half(a, 1 - c), a, (x, y, c), send_sems, recv_sems) for a in range(n_w)]
        return sends, arrivals

    return _Exchange(list(reduced), list(range(n_w)), [], n_w, copies)


def _chip_partial(pos, grad, theirs, kind, name):
    M, N = grad.shape
    if kind == "cols":
        tm = 256
        steps = M // 2 // tm
        block, g_map = (tm, N), (lambda i, pos: (pos[0] * steps + i, 0))
    else:
        tm = M // 4
        steps = 4
        block, g_map = (tm, N // 2), (lambda i, pos: (i, pos[0]))

    def body(pos_ref, g_ref, t_ref, o_ref):
        o_ref[...] = (g_ref[...] + t_ref[...]).astype(WIRE_DTYPE)

    grid_spec = pltpu.PrefetchScalarGridSpec(
        num_scalar_prefetch=1, grid=(steps,),
        in_specs=[pl.BlockSpec(block, g_map), pl.BlockSpec(block, lambda i, pos: (i, 0))],
        out_specs=pl.BlockSpec(block, lambda i, pos: (i, 0)))
    return pl.pallas_call(
        body, name=name, grid_spec=grid_spec, out_shape=jax.ShapeDtypeStruct(_half_shape((M, N), kind), WIRE_DTYPE),
        compiler_params=_params(1),
    )(pos, grad, theirs)


def _reduce_shard(pos, grad, theirs, received, reduced, kind, layer, name):
    M, N = grad.shape
    if kind == "cols":
        tm = min(M // 2, 512)
        steps = M // 2 // tm
        block = (tm, N // 4)
        g_map = lambda i, pos: (pos[0] * steps + i, pos[1])
        t_map = lambda i, pos: (i, pos[1])
        o_map = lambda i, pos: (layer, pos[0] * steps + i, 0)
    else:
        steps = 1
        block = (M // 4, N // 2)
        g_map = lambda i, pos: (pos[1], pos[0])
        t_map = lambda i, pos: (pos[1], 0)
        o_map = lambda i, pos: (layer, 0, pos[0])
    out_shape = (DEPTH,) + _shard_shape((M, N), kind)

    def body(pos_ref, g_ref, t_ref, r0_ref, r1_ref, r2_ref, *rest):
        o_ref = rest[-1]
        chip = pos_ref[1]
        own = g_ref[...] + t_ref[...]
        r = [r0_ref[...].astype(F32), r1_ref[...].astype(F32), r2_ref[...].astype(F32)]
        total = None
        for s in range(N_CHIPS):
            rel = jnp.bitwise_xor(chip, s)
            term = jnp.where(rel == 0, own, jnp.where(rel == 2, r[0], jnp.where(rel == 1, r[1], r[2])))
            total = term if total is None else total + term
        o_ref[...] = total

    r_spec = lambda j: pl.BlockSpec((None,) + block, lambda i, pos: (j, i, 0))
    in_specs = [pl.BlockSpec(block, g_map), pl.BlockSpec(block, t_map), r_spec(0), r_spec(1), r_spec(2)]
    args = [pos, grad, theirs, received, received, received]
    aliases = {}
    if reduced is not None:
        in_specs.append(ANY)
        args.append(reduced)
        aliases = {6: 0}
    grid_spec = pltpu.PrefetchScalarGridSpec(
        num_scalar_prefetch=1, grid=(steps,), in_specs=in_specs, out_specs=pl.BlockSpec((None,) + block, o_map))
    return pl.pallas_call(
        body, name=name, grid_spec=grid_spec, out_shape=jax.ShapeDtypeStruct(out_shape, F32),
        input_output_aliases=aliases, compiler_params=_params(1),
    )(*args)


def _sum_devices(gathered):
    R8, C = gathered.shape
    R = R8 // N_DEV
    lanes = C // 128
    tc = 128 * max(k for k in range(1, lanes + 1) if lanes % k == 0 and k * 128 * R8 * 4 <= 2 ** 22)

    def body(g_ref, o_ref):
        total = g_ref[0:R, :]
        for d in range(1, N_DEV):
            total = total + g_ref[d * R:(d + 1) * R, :]
        o_ref[...] = total

    return pl.pallas_call(
        body, name="sum_devices", grid=(C // tc,),
        in_specs=[pl.BlockSpec((R8, tc), lambda j: (0, j))], out_specs=pl.BlockSpec((R, tc), lambda j: (0, j)),
        out_shape=jax.ShapeDtypeStruct((R, C), F32), compiler_params=_params(1),
    )(gathered)


_SMALL = ("ada_b", "pool_w", "pool_scale", "sgu_ln_g", "sgu_ln_b", "sgu_w", "sgu_b", "ln1_g", "ln1_b", "conv_w",
          "conv_b", "ln2_g", "ln2_b")
_WEIGHTS = ("ada_w", "ada_b", "w_in", "pool_w", "pool_scale", "sgu_ln_g", "sgu_ln_b", "sgu_w", "sgu_b", "w_out",
            "ln1_g", "ln1_b", "w_up", "conv_w", "conv_b", "w_down", "ln2_g", "ln2_b")


def kernel(x, c, ada_w, ada_b, w_in, pool_w, pool_scale, sgu_ln_g, sgu_ln_b, sgu_w, sgu_b, w_out, ln1_g, ln1_b, w_up, conv_w, conv_b, w_down, ln2_g, ln2_b, loss_target, m_ada_w, m_ada_b, m_w_in, m_pool_w, m_pool_scale, m_sgu_ln_g, m_sgu_ln_b, m_sgu_w, m_sgu_b, m_w_out, m_ln1_g, m_ln1_b, m_w_up, m_conv_w, m_conv_b, m_w_down, m_ln2_g, m_ln2_b, v_ada_w, v_ada_b, v_w_in, v_pool_w, v_pool_scale, v_sgu_ln_g, v_sgu_ln_b, v_sgu_w, v_sgu_b, v_w_out, v_ln1_g, v_ln1_b, v_w_up, v_conv_w, v_conv_b, v_w_down, v_ln2_g, v_ln2_b):
    weights = dict(ada_w=ada_w, ada_b=ada_b, w_in=w_in, pool_w=pool_w, pool_scale=pool_scale, sgu_ln_g=sgu_ln_g,
                   sgu_ln_b=sgu_ln_b, sgu_w=sgu_w, sgu_b=sgu_b, w_out=w_out, ln1_g=ln1_g, ln1_b=ln1_b, w_up=w_up,
                   conv_w=conv_w, conv_b=conv_b, w_down=w_down, ln2_g=ln2_g, ln2_b=ln2_b)
    mom_m = dict(ada_w=m_ada_w, ada_b=m_ada_b, w_in=m_w_in, pool_w=m_pool_w, pool_scale=m_pool_scale,
                 sgu_ln_g=m_sgu_ln_g, sgu_ln_b=m_sgu_ln_b, sgu_w=m_sgu_w, sgu_b=m_sgu_b, w_out=m_w_out,
                 ln1_g=m_ln1_g, ln1_b=m_ln1_b, w_up=m_w_up, conv_w=m_conv_w, conv_b=m_conv_b, w_down=m_w_down,
                 ln2_g=m_ln2_g, ln2_b=m_ln2_b)
    mom_v = dict(ada_w=v_ada_w, ada_b=v_ada_b, w_in=v_w_in, pool_w=v_pool_w, pool_scale=v_pool_scale,
                 sgu_ln_g=v_sgu_ln_g, sgu_ln_b=v_sgu_ln_b, sgu_w=v_sgu_w, sgu_b=v_sgu_b, w_out=v_w_out,
                 ln1_g=v_ln1_g, ln1_b=v_ln1_b, w_up=v_w_up, conv_w=v_conv_w, conv_b=v_conv_b, w_down=v_w_down,
                 ln2_g=v_ln2_g, ln2_b=v_ln2_b)

    ix, iy, ic = lax.axis_index("x"), lax.axis_index("y"), lax.axis_index("c")
    chip = 2 * ix + iy
    dev = 4 * ix + 2 * iy + ic
    pos = jnp.stack([ic, chip]).astype(jnp.int32)
    xs = x.reshape(x.shape[1:])
    target = loss_target.reshape(loss_target.shape[1:])
    ff_shard = conv_w.shape[2]
    mod_shard = ada_w.shape[2]

    first = _gather_flat(jnp.concatenate([c.reshape(-1), conv_w.reshape(-1)]), "gather_c_conv")
    c_all = jnp.pad(first[:, :D], ((0, 8), (0, 0)))
    conv_parts = first[0::2, D:].reshape(N_CHIPS, DEPTH, 3, ff_shard)
    conv_full = jnp.transpose(conv_parts, (1, 2, 0, 3)).reshape(DEPTH, 3, D_FF)
    ada_b_cols = lax.dynamic_slice_in_dim(ada_b, chip * mod_shard, mod_shard, axis=1).reshape(DEPTH, 1, mod_shard)
    mod_part = _ada_forward(c_all, ada_w, ada_b_cols)[:, :8, :]
    mod_all = _gather_flat(mod_part.reshape(-1), "gather_mod").reshape(N_DEV, DEPTH, 8, mod_shard)
    mod_mine = lax.dynamic_index_in_dim(mod_all[0::2], dev, axis=2, keepdims=False)
    mod = jnp.transpose(mod_mine, (1, 0, 2)).reshape(DEPTH, 6, 1, D)

    sh_in, sh_out, sh_up, sh_down = (w.astype(WIRE_DTYPE) for w in (w_in, w_out, w_up, w_down))
    wf_in = lax.empty((DEPTH, D, D_IN), WIRE_DTYPE)
    wf_out = lax.empty((DEPTH, D, D), WIRE_DTYPE)
    wf_up = lax.empty((DEPTH, D, 2 * D_FF), WIRE_DTYPE)
    wf_down = lax.empty((DEPTH, D_FF, D), WIRE_DTYPE)
    (wf_in,) = _run_exchange(_gather_exchange(sh_in, wf_in, "cols", 0), "gather_w_in_l0")

    tri = jnp.tril(jnp.ones((CHUNK, CHUNK), dtype=bool))
    ones_row = jnp.ones((1, D), F32)
    zeros_row = jnp.zeros((1, D), F32)
    row = lambda a, l: a[l].reshape(1, -1)

    saved = []
    xn, lg, lb = xs, ones_row, zeros_row
    h = _modulate_input(xs, mod[0, 0], mod[0, 1])
    for l in range(DEPTH):
        shift1, scale1, gate1, shift2, scale2, gate2 = (mod[l, k] for k in range(6))
        nxt = min(l + 1, DEPTH - 1)
        pw = pool_w[l].astype(MXU_DTYPE)
        wm = jnp.where(tri[None], sgu_w[l], 0.0).astype(MXU_DTYPE)
        bias_full = jnp.repeat(jnp.transpose(sgu_b[l]), GROUP, axis=1)
        lg1, lb1 = row(ln1_g, l), row(ln1_b, l)
        proj, (wf_out,) = _matmul(h, wf_in, l, D_IN, STORE_DTYPE, f"f1_in_proj_l{l}",
                                  exchange=_gather_exchange(sh_out, wf_out, "rows", l))
        (mix, f1, xhat1, rstd1, h2, gel), (wf_up,) = _f2_mixers(
            proj, xn, lg, lb, gate1, pw, row(pool_scale, l), row(sgu_ln_g, l), row(sgu_ln_b, l), wm, bias_full,
            wf_out, lg1, lb1, shift2, scale2, l, exchange=_gather_exchange(sh_up, wf_up, "cols", l))
        up, (wf_down,) = _matmul(h2, wf_up, l, D_FF, STORE_DTYPE, f"f3_up_proj_l{l}",
                                 exchange=_gather_exchange(sh_down, wf_down, "rows", l))
        (act, f2, xhat2, rstd2, h_next, ge, dge), filled = _f4_ffn(
            up, xhat1, lg1, lb1, gate2, conv_full[l], row(conv_b, l), wf_down, row(ln2_g, l), row(ln2_b, l),
            mod[nxt, 0], mod[nxt, 1], l,
            exchange=_gather_exchange(sh_in, wf_in, "cols", l + 1) if l + 1 < DEPTH else None)
        if filled:
            (wf_in,) = filled
        saved.append(dict(xn=xn, lg=lg, lb=lb, h=h, proj=proj, mix=mix, f1=f1, xhat1=xhat1, rstd1=rstd1, h2=h2,
                          up=up, act=act, f2=f2, xhat2=xhat2, rstd2=rstd2, pw=pw, wm=wm, bias_full=bias_full,
                          ge=ge, dge=dge, gel=gel))
        xn, lg, lb, h = xhat2, row(ln2_g, l), row(ln2_b, l), h_next

    d_out = target

    tags = ("in", "out", "up", "down")
    small = {n: [None] * DEPTH for n in _SMALL}
    dmod = [[None] * 6 for _ in range(DEPTH)]
    below = None
    ready = None
    reduced = [None] * 4

    def partial_sums(layer, full, theirs):
        return [_chip_partial(pos, g, t, kind, f"chip_partial_{tag}_l{layer}")
                for g, t, kind, tag in zip(full, theirs, _SHARD_KINDS, tags)]

    def chip_sums(layer, full, theirs, received):
        return [_reduce_shard(pos, g, t, r, out, kind, layer, f"reduce_shard_{tag}_l{layer}")
                for g, t, r, out, kind, tag in zip(full, theirs, received, reduced, _SHARD_KINDS, tags)]

    for l in reversed(range(DEPTH)):
        sv = saved[l]
        scale1, gate1, scale2, gate2 = mod[l, 1], mod[l, 2], mod[l, 4], mod[l, 5]
        lg1, lb1 = row(ln1_g, l), row(ln1_b, l)
        outs, theirs = _b_ln(d_out, sv["xhat2"], sv["rstd2"], sv["f2"], row(ln2_g, l), row(ln2_b, l), gate2,
                             f"b1_ln2_l{l}", modulated=below, loss_head=below is None,
                             exchange=_swap_exchange(ready[1]) if ready else None)
        df2, dres2, dlg2, dlb2, dmod[l][5] = outs[:5]
        if below is not None:
            dmod[l + 1][1], dmod[l + 1][0] = outs[5], outs[6]
        else:
            loss = lax.psum(jnp.sum(outs[5]) * (0.5 / D), ("x", "y", "c"))
        partials = partial_sums(ready[0], ready[1], theirs) if ready else None
        (dup, dcw, dcb), received = _b2_ffn(df2, wf_down, sv["up"], sv["ge"], sv["dge"], conv_full[l], l,
                                            exchange=_scatter_exchange(partials) if ready else None)
        if ready:
            reduced = chip_sums(ready[0], ready[1], theirs, received)
        g_down = _weight_grad(sv["act"], [df2], 512, f"wg_down_l{l}", rows=2 * TS_TN)
        dh2, shared = _matmul_nt(dup, wf_up, l, D_FF, f"b3_up_l{l}",
                                 exchange=_share_exchange(reduced, ready[0]) if ready else None)
        if ready:
            reduced = shared
        g_up = _weight_grad(sv["h2"], [dup], 1408, f"wg_up_l{l}", rows=2 * TS_TN)
        (df1, dres1, dlg1, dlb1, dmod[l][2], dmod[l][4], dmod[l][3]), _ = _b_ln(
            dh2, sv["xhat1"], sv["rstd1"], sv["f1"], lg1, lb1, gate1, f"b4_ln1_l{l}", modulated=(dres2, scale2))
        dproj, dpw, dps, dsg, dsb, dwm, dbias = _b5_mixers(
            df1, wf_out, sv["proj"], sv["gel"], sv["pw"], row(pool_scale, l), row(sgu_ln_g, l), row(sgu_ln_b, l), sv["wm"],
            sv["bias_full"], l)
        g_out = _weight_grad(sv["mix"], [df1], D, f"wg_out_l{l}", rows=2 * TS_TN)
        d_out, _ = _matmul_nt(dproj, wf_in, l, D_IN, f"b6_in_l{l}")
        g_in = _weight_grad(sv["h"], [dproj], 768, f"wg_in_l{l}", rows=2 * TS_TN)
        ready = (l, [g_in, g_out, g_up, g_down])
        below = (dres1, scale1)
        small["pool_w"][l], small["pool_scale"][l] = dpw, dps[0]
        small["sgu_ln_g"][l], small["sgu_ln_b"][l], small["sgu_w"][l] = dsg[0], dsb[0], dwm
        small["sgu_b"][l] = jnp.transpose(jnp.sum(dbias.reshape(CHUNK, HEADS, GROUP), axis=2))
        small["ln1_g"][l], small["ln1_b"][l] = dlg1[0], dlb1[0]
        small["conv_w"][l], small["conv_b"][l] = dcw, dcb[0]
        small["ln2_g"][l], small["ln2_b"][l] = dlg2[0], dlb2[0]
    (grad_x2d, dmod[0][1], dmod[0][0]), _ = _input_grad(d_out, below[0], below[1], xs)
    grad_x = grad_x2d.reshape(x.shape)
    small["ada_b"] = [jnp.concatenate([part[0] for part in dmod[l]]) for l in range(DEPTH)]

    names = _SMALL
    flat = jnp.concatenate([jnp.stack(small[n]).reshape(-1) for n in names])
    n_small = flat.shape[0]
    padded = -(-n_small // 1024) * 1024
    gathered = _all_gather8(jnp.pad(flat, (0, padded - n_small)).reshape(8, padded // 8), "gather_small_grads")
    summed = _sum_devices(gathered).reshape(-1)[:n_small]
    grads = {}
    offset = 0
    for n in names:
        size = math.prod(weights[n].shape[1:]) * DEPTH if n != "conv_w" else DEPTH * 3 * D_FF
        grads[n] = summed[offset:offset + size]
        offset += size
    grads["conv_w"] = lax.dynamic_slice_in_dim(grads["conv_w"].reshape(DEPTH, 3, D_FF), chip * ff_shard, ff_shard,
                                               axis=2)
    for n in names:
        grads[n] = grads[n].reshape(weights[n].shape)
    dmod_all = gathered[0::8, :DEPTH * 6 * D].reshape(N_DEV, DEPTH, 6 * D)
    dmod_cols = lax.dynamic_slice_in_dim(jnp.transpose(dmod_all, (1, 0, 2)), chip * mod_shard, mod_shard, axis=2)
    grads["ada_w"] = _ada_backward(c_all, jnp.pad(dmod_cols, ((0, 0), (0, 8), (0, 0))))

    theirs = _run_exchange(_swap_exchange(ready[1]), "swap_halves_l0")
    partials = partial_sums(0, ready[1], theirs)
    received = _run_exchange(_scatter_exchange(partials), "scatter_partials_l0")
    reduced = chip_sums(0, ready[1], theirs, received)
    grads["w_in"], grads["w_out"], grads["w_up"], grads["w_down"] = _run_exchange(
        _share_exchange(reduced, 0), "share_reduced_l0")

    delta, new_m, new_v = {}, {}, {}
    for n in ("ada_w", "w_in", "w_out", "w_up", "w_down"):
        shape = weights[n].shape
        two_d = (shape[0] * shape[1], shape[2])
        d_, m_, v_ = _adamw(weights[n].reshape(two_d), grads[n].reshape(two_d), mom_m[n].reshape(two_d),
                            mom_v[n].reshape(two_d), f"adamw_{n}")
        delta[n], new_m[n], new_v[n] = d_.reshape(shape), m_.reshape(shape), v_.reshape(shape)
    sizes = [math.prod(weights[n].shape) for n in names]
    total = sum(sizes)
    padded = -(-total // 1024) * 1024
    pack = lambda d: jnp.pad(jnp.concatenate([d[n].reshape(-1) for n in names]), (0, padded - total)).reshape(8, -1)
    v_pack = jnp.pad(jnp.concatenate([mom_v[n].reshape(-1) for n in names]), (0, padded - total),
                     constant_values=1.0).reshape(8, -1)
    d_, m_, v_ = _adamw(pack(weights), pack(grads), pack(mom_m), v_pack, "adamw_small")
    offset = 0
    for n, size in zip(names, sizes):
        for src, dst in ((d_, delta), (m_, new_m), (v_, new_v)):
            dst[n] = src.reshape(-1)[offset:offset + size].reshape(weights[n].shape)
        offset += size

    return (loss, grad_x, *[grads[n] for n in _WEIGHTS], *[delta[n] for n in _WEIGHTS],
            *[new_m[n] for n in _WEIGHTS], *[new_v[n] for n in _WEIGHTS])
```

```python
import functools
import math

import jax
import jax.numpy as jnp
from jax import lax
from jax.experimental import pallas as pl
from jax.experimental.pallas import tpu as pltpu

F32 = jnp.float32
MXU_DTYPE = jnp.bfloat16
WIRE_DTYPE = jnp.bfloat16

DEPTH = 4
D = 1024
D_POOL = 512
D_SGU = 512
N_GROUPS = 4
GROUP = 128
POOL_WINDOWS = (2, 4, 8, 16)
POOL_HALO = 16
CHUNK = 128
HEADS = 4
D_IN = D_POOL + 2 * D_SGU
D_FF = 2816
CONV_HALO = 16
STORE_DTYPE = jnp.bfloat16
GELU_DTYPE = jnp.bfloat16
N_CHIPS = 4
N_DEV = 8
ALPHA = (2.0 * DEPTH) ** 0.25
LN_EPS = 1e-5
ADAM_LR, ADAM_B1, ADAM_B2, ADAM_EPS, ADAM_WD, ADAM_STEP = 0.001, 0.9, 0.999, 1e-08, 0.01, 10

TS_PROJ = 512
TS_MM = 1024
TS_MIX = 512
TS_FF = 256
TS_TN = 1024
VMEM_LIMIT = 52 * 2 ** 20

MESH = pl.DeviceIdType.MESH
ANY = pl.BlockSpec(memory_space=pl.ANY)

_GELU_K0 = math.sqrt(2.0 / math.pi)
_GELU_K1 = 0.044715


def _params(n_axes):
    return pltpu.CompilerParams(dimension_semantics=("arbitrary",) * n_axes, vmem_limit_bytes=VMEM_LIMIT)


class _Exchange:
    def __init__(self, inputs, inplace, fresh, n_copies, copies):
        self.inputs, self.inplace, self.fresh, self.n_copies, self.copies = inputs, inplace, fresh, n_copies, copies

    def out_shapes(self):
        return [jax.ShapeDtypeStruct(self.inputs[i].shape, self.inputs[i].dtype) for i in self.inplace] + list(self.fresh)

    def semaphores(self):
        return [pltpu.SemaphoreType.DMA((self.n_copies,)), pltpu.SemaphoreType.DMA((self.n_copies,))]

    def aliases(self, first_input, first_output):
        return {first_input + i: first_output + k for k, i in enumerate(self.inplace)}


def _call(body, *, name, grid, in_specs, out_specs, out_shape, args, scratch_shapes=(), exchange=None):
    n_axes = len(grid)
    in_specs, out_specs, out_shape, scratch_shapes = list(in_specs), list(out_specs), list(out_shape), list(scratch_shapes)
    if exchange is None:
        outs = pl.pallas_call(body, name=name, grid=grid, in_specs=in_specs, out_specs=out_specs, out_shape=out_shape,
                              scratch_shapes=scratch_shapes, compiler_params=_params(n_axes))(*args)
        return list(outs), []
    n_in, n_out, n_scr = len(in_specs), len(out_shape), len(scratch_shapes)
    x_out = exchange.out_shapes()
    n_xin, n_xout = len(exchange.inputs), len(x_out)

    def hosted(*refs):
        ins, xin = refs[:n_in], refs[n_in:n_in + n_xin]
        o = n_in + n_xin
        outs, xout = refs[o:o + n_out], refs[o + n_out:o + n_out + n_xout]
        s = o + n_out + n_xout
        scr, (send_sems, recv_sems) = refs[s:s + n_scr], refs[s + n_scr:]
        first = functools.reduce(jnp.logical_and, [pl.program_id(d) == 0 for d in range(n_axes)])
        last = functools.reduce(jnp.logical_and, [pl.program_id(d) == pl.num_programs(d) - 1 for d in range(n_axes)])

        @pl.when(first)
        def _():
            for cp in exchange.copies(xin, xout, send_sems, recv_sems)[0]:
                cp.start()

        body(*ins, *outs, *scr)

        @pl.when(last)
        def _():
            sends, arrivals = exchange.copies(xin, xout, send_sems, recv_sems)
            for cp in arrivals:
                cp.wait_recv()
            for cp in sends:
                cp.wait_send()

    results = pl.pallas_call(
        hosted, name=name, grid=grid, in_specs=in_specs + [ANY] * n_xin, out_specs=out_specs + [ANY] * n_xout,
        out_shape=out_shape + x_out, scratch_shapes=scratch_shapes + exchange.semaphores(),
        input_output_aliases=exchange.aliases(n_in, n_out), compiler_params=_params(n_axes),
    )(*args, *exchange.inputs)
    return list(results[:n_out]), list(results[n_out:])


def _run_exchange(exchange, name):
    x_out = exchange.out_shapes()
    n_xin, n_xout = len(exchange.inputs), len(x_out)

    def body(*refs):
        xin, xout = refs[:n_xin], refs[n_xin:n_xin + n_xout]
        send_sems, recv_sems = refs[n_xin + n_xout:]
        sends, arrivals = exchange.copies(xin, xout, send_sems, recv_sems)
        for cp in sends:
            cp.start()
        for cp in arrivals:
            cp.wait_recv()
        for cp in sends:
            cp.wait_send()

    return list(pl.pallas_call(
        body, name=name, in_specs=[ANY] * n_xin, out_specs=[ANY] * n_xout, out_shape=x_out,
        scratch_shapes=exchange.semaphores(), input_output_aliases=exchange.aliases(0, 0),
    )(*exchange.inputs))


def _dot(a, b):
    return jnp.dot(a, b, preferred_element_type=F32)


def _dot_nt(a, b):
    return lax.dot_general(a, b, (((1,), (1,)), ((), ())), preferred_element_type=F32)


def _dot_tn(a, b):
    return lax.dot_general(a, b, (((0,), (0,)), ((), ())), preferred_element_type=F32)


def _gelu(x):
    x2 = x * x
    t = jnp.tanh(x * (x2 * (_GELU_K0 * _GELU_K1) + _GELU_K0))
    cdf = 0.5 * t + 0.5
    dg = cdf + (x * (1.0 - t * t)) * (x2 * (1.5 * _GELU_K0 * _GELU_K1) + 0.5 * _GELU_K0)
    return x * cdf, dg


def _colsum(x):
    return jnp.sum(x, axis=0, keepdims=True)


def _row(d):
    return pl.BlockSpec((1, d), lambda *_: (0, 0))


def _full(shape):
    n = len(shape)
    return pl.BlockSpec(shape, lambda *_: (0,) * n)


def _layer_block(shape, layer):
    n = len(shape)
    return pl.BlockSpec((None,) + tuple(shape), lambda *_: (layer,) + (0,) * n)


def _pool_counts(first_row, rows, window):
    t = first_row + lax.broadcasted_iota(jnp.int32, (rows, 1), 0)
    return 1.0 / jnp.minimum(t + 1, window).astype(F32)


def _pool_forward(a_ext, a, first_row, rows):
    out = []
    for g, window in enumerate(POOL_WINDOWS):
        s = a_ext[:, g * GROUP:(g + 1) * GROUP]
        k = 1
        while k < window:
            s = s + pltpu.roll(s, k, 0)
            k *= 2
        inv = _pool_counts(first_row, rows, window)
        out.append(s[POOL_HALO:] * inv - a[:, g * GROUP:(g + 1) * GROUP])
    return out


def _sgu_norm(gv):
    mu = jnp.mean(gv, axis=-1, keepdims=True)
    xc = gv - mu
    var = jnp.mean(xc * xc, axis=-1, keepdims=True)
    rs = lax.rsqrt(var + LN_EPS)
    return xc * rs, rs


def _matmul(a, w_all, layer, tn, out_dtype, name, exchange=None):
    S, K = a.shape
    N = w_all.shape[2]
    ts = min(TS_MM, S)

    def body(a_ref, w_ref, o_ref):
        o_ref[...] = _dot(a_ref[...], w_ref[...]).astype(out_dtype)

    (out,), extra = _call(
        body, name=name, grid=(N // tn, S // ts),
        in_specs=[pl.BlockSpec((ts, K), lambda j, i: (i, 0)), pl.BlockSpec((None, K, tn), lambda j, i: (layer, 0, j))],
        out_specs=[pl.BlockSpec((ts, tn), lambda j, i: (i, j))],
        out_shape=[jax.ShapeDtypeStruct((S, N), out_dtype)], args=(a, w_all), exchange=exchange)
    return out, extra


def _modulated_matmul(x, shift, scale, w_all, layer, out_dtype, name, exchange=None):
    S, K = x.shape
    N = w_all.shape[2]
    ts = min(TS_MM, S)

    def body(x_ref, sh_ref, sc_ref, w_ref, h_ref, o_ref):
        h = (x_ref[...] * (1.0 + sc_ref[...]) + sh_ref[...]).astype(MXU_DTYPE)
        h_ref[...] = h
        o_ref[...] = _dot(h, w_ref[...]).astype(out_dtype)

    tile = lambda w: pl.BlockSpec((ts, w), lambda i: (i, 0))
    return _call(
        body, name=name, grid=(S // ts,), in_specs=[tile(K), _row(K), _row(K), _layer_block((K, N), layer)],
        out_specs=[tile(K), tile(N)],
        out_shape=[jax.ShapeDtypeStruct((S, K), MXU_DTYPE), jax.ShapeDtypeStruct((S, N), out_dtype)],
        args=(x, shift, scale, w_all), exchange=exchange)


def _matmul_nt(a, w_all, layer, tk, name, exchange=None):
    S, K = a.shape
    N = w_all.shape[1]
    ts = min(TS_MM, S)
    n_k = K // tk

    def body(a_ref, w_ref, o_ref, acc):
        k = pl.program_id(1)
        part = _dot_nt(a_ref[...], w_ref[...])
        if n_k == 1:
            o_ref[...] = part
        else:
            @pl.when(k == 0)
            def _():
                acc[...] = part

            @pl.when(jnp.logical_and(k > 0, k < n_k - 1))
            def _():
                acc[...] += part

            @pl.when(k == n_k - 1)
            def _():
                o_ref[...] = acc[...] + part

    (out,), extra = _call(
        body, name=name, grid=(S // ts, n_k),
        in_specs=[pl.BlockSpec((ts, tk), lambda i, k: (i, k)), pl.BlockSpec((None, N, tk), lambda i, k: (layer, 0, k))],
        out_specs=[pl.BlockSpec((ts, N), lambda i, k: (i, 0))],
        out_shape=[jax.ShapeDtypeStruct((S, N), F32)],
        scratch_shapes=[pltpu.VMEM((ts, N), F32)], args=(a, w_all), exchange=exchange)
    return out, extra


def _f2_mixers(proj, xn, lg, lb, gate, pool_w, pool_scale, sg, sb, wm, bias_full, w_out_all, lg1, lb1, shift2, scale2,
               layer, exchange=None):
    S = xn.shape[0]
    ts = min(TS_MIX, S)
    n_chunks = ts // CHUNK
    halo_blocks = ts // POOL_HALO

    def body(p_ref, halo_ref, xn_ref, lg_ref, lb_ref, gate_ref, pw_ref, ps_ref, sg_ref, sb_ref, wm_ref,
             bias_ref, wo_ref, lg1_ref, lb1_ref, sh2_ref, sc2_ref, mix_ref, f_ref, xh_ref, rs_ref, h2_ref, gel_ref,
             z_scr):
        i = pl.program_id(0)
        a = p_ref[:, 0:D_POOL].astype(F32)
        u = p_ref[:, D_POOL:D_POOL + D_SGU].astype(GELU_DTYPE)
        v = p_ref[:, D_POOL + D_SGU:D_IN].astype(GELU_DTYPE)
        halo = halo_ref[...].astype(F32) * (i > 0).astype(F32)
        a_ext = jnp.concatenate([halo, a], axis=0)
        pooled = _pool_forward(a_ext, a, i * ts, ts)
        for g in range(N_GROUPS):
            mixed = _dot(pooled[g].astype(MXU_DTYPE), pw_ref[g])
            mix_ref[:, g * GROUP:(g + 1) * GROUP] = (mixed * ps_ref[:, g * GROUP:(g + 1) * GROUP]).astype(MXU_DTYPE)
        gu, dgu = _gelu(u)
        gv, dgv = _gelu(v)
        for k, part in enumerate((gu, dgu, gv, dgv)):
            gel_ref[:, k * D_SGU:(k + 1) * D_SGU] = part.astype(STORE_DTYPE)
        gu = gu.astype(F32)
        vhat, _ = _sgu_norm(gv.astype(F32))
        vn = (vhat * sg_ref[...] + sb_ref[...]).astype(MXU_DTYPE)
        for h in range(HEADS):
            cols = slice(h * GROUP, (h + 1) * GROUP)
            vn_h = jnp.concatenate([vn[c * CHUNK:(c + 1) * CHUNK, cols] for c in range(n_chunks)], axis=1)
            z_h = _dot(wm_ref[h], vn_h)
            for c in range(n_chunks):
                z_scr[c * CHUNK:(c + 1) * CHUNK, cols] = z_h[:, c * GROUP:(c + 1) * GROUP] + bias_ref[:, cols]
        mix_ref[:, D_POOL:D] = (gu * z_scr[...]).astype(MXU_DTYPE)
        f = _dot(mix_ref[...], wo_ref[...])
        f_ref[...] = f.astype(STORE_DTYPE)
        x = xn_ref[...] * lg_ref[...] + lb_ref[...]
        z1 = ALPHA * x + gate_ref[...] * f
        mu = jnp.mean(z1, axis=-1, keepdims=True)
        zc = z1 - mu
        var = jnp.mean(zc * zc, axis=-1, keepdims=True)
        rs = lax.rsqrt(var + LN_EPS)
        xhat = zc * rs
        xh_ref[...] = xhat
        rs_ref[...] = rs
        x1 = xhat * lg1_ref[...] + lb1_ref[...]
        h2_ref[...] = (x1 * (1.0 + sc2_ref[...]) + sh2_ref[...]).astype(MXU_DTYPE)

    tile = lambda w: pl.BlockSpec((ts, w), lambda i: (i, 0))
    return _call(
        body, name=f"f2_mixers_l{layer}", grid=(S // ts,),
        in_specs=[tile(D_IN),
                  pl.BlockSpec((POOL_HALO, D_POOL), lambda i: (jnp.maximum(i * halo_blocks - 1, 0), 0)),
                  tile(D), _row(D), _row(D), _row(D),
                  _full((N_GROUPS, GROUP, GROUP)), _row(D_POOL), _row(D_SGU), _row(D_SGU),
                  _full((HEADS, CHUNK, CHUNK)), _full((CHUNK, D_SGU)), _layer_block((D, D), layer),
                  _row(D), _row(D), _row(D), _row(D)],
        out_specs=[tile(D), tile(D), tile(D), tile(1), tile(D), tile(4 * D_SGU)],
        out_shape=[jax.ShapeDtypeStruct((S, D), MXU_DTYPE), jax.ShapeDtypeStruct((S, D), STORE_DTYPE),
                   jax.ShapeDtypeStruct((S, D), F32), jax.ShapeDtypeStruct((S, 1), F32),
                   jax.ShapeDtypeStruct((S, D), MXU_DTYPE), jax.ShapeDtypeStruct((S, 4 * D_SGU), STORE_DTYPE)],
        scratch_shapes=[pltpu.VMEM((ts, D_SGU), F32)],
        args=(proj, proj, xn, lg, lb, gate, pool_w, pool_scale, sg, sb, wm, bias_full, w_out_all, lg1, lb1, shift2,
              scale2), exchange=exchange)


def _f4_ffn(up, xhat1, lg, lb, gate, conv_w, conv_b, w_down_all, lg2, lb2, shift_next, scale_next, layer,
            exchange=None):
    S = xhat1.shape[0]
    ts = min(TS_FF, S)
    halo_blocks = ts // CONV_HALO

    def body(g_ref, halo_ref, val_ref, xh_ref, lg_ref, lb_ref, gate_ref, cw_ref, cb_ref, wd_ref, lg2_ref, lb2_ref,
             shn_ref, scn_ref, act_ref, f_ref, xo_ref, rs_ref, hn_ref, ge_ref, dge_ref):
        i = pl.program_id(0)
        halo = halo_ref[...].astype(GELU_DTYPE) * (i > 0).astype(GELU_DTYPE)
        g_ext = jnp.concatenate([halo, g_ref[...].astype(GELU_DTYPE)], axis=0)
        cw = cw_ref[...].astype(GELU_DTYPE)
        gm2 = pltpu.roll(g_ext, 2, 0)[CONV_HALO:]
        gm1 = pltpu.roll(g_ext, 1, 0)[CONV_HALO:]
        gc = ((cb_ref[...].astype(GELU_DTYPE) + gm2 * cw[0:1, :]) + gm1 * cw[1:2, :]) + g_ext[CONV_HALO:] * cw[2:3, :]
        ge, dge = _gelu(gc)
        ge_ref[...] = ge.astype(STORE_DTYPE)
        dge_ref[...] = dge.astype(STORE_DTYPE)
        act = (ge * val_ref[...].astype(GELU_DTYPE)).astype(MXU_DTYPE)
        act_ref[...] = act
        f = _dot(act, wd_ref[...])
        f_ref[...] = f.astype(STORE_DTYPE)
        x = xh_ref[...] * lg_ref[...] + lb_ref[...]
        z = ALPHA * x + gate_ref[...] * f
        mu = jnp.mean(z, axis=-1, keepdims=True)
        zc = z - mu
        var = jnp.mean(zc * zc, axis=-1, keepdims=True)
        rs = lax.rsqrt(var + LN_EPS)
        xhat = zc * rs
        xo_ref[...] = xhat
        rs_ref[...] = rs
        x2 = xhat * lg2_ref[...] + lb2_ref[...]
        hn_ref[...] = (x2 * (1.0 + scn_ref[...]) + shn_ref[...]).astype(MXU_DTYPE)

    tile = lambda w: pl.BlockSpec((ts, w), lambda i: (i, 0))
    return _call(
        body, name=f"f4_ffn_l{layer}", grid=(S // ts,),
        in_specs=[pl.BlockSpec((ts, D_FF), lambda i: (i, 0)),
                  pl.BlockSpec((CONV_HALO, D_FF), lambda i: (jnp.maximum(i * halo_blocks - 1, 0), 0)),
                  pl.BlockSpec((ts, D_FF), lambda i: (i, 1)),
                  tile(D), _row(D), _row(D), _row(D), _full((3, D_FF)), _row(D_FF),
                  _layer_block((D_FF, D), layer), _row(D), _row(D), _row(D), _row(D)],
        out_specs=[tile(D_FF), tile(D), tile(D), tile(1), tile(D), tile(D_FF), tile(D_FF)],
        out_shape=[jax.ShapeDtypeStruct((S, D_FF), MXU_DTYPE), jax.ShapeDtypeStruct((S, D), STORE_DTYPE),
                   jax.ShapeDtypeStruct((S, D), F32), jax.ShapeDtypeStruct((S, 1), F32),
                   jax.ShapeDtypeStruct((S, D), MXU_DTYPE), jax.ShapeDtypeStruct((S, D_FF), STORE_DTYPE),
                   jax.ShapeDtypeStruct((S, D_FF), STORE_DTYPE)],
        args=(up, up, up, xhat1, lg, lb, gate, conv_w, conv_b, w_down_all, lg2, lb2, shift_next, scale_next),
        exchange=exchange)


def _accumulate(i, ref, value):
    @pl.when(i == 0)
    def _():
        ref[...] = value

    @pl.when(i > 0)
    def _():
        ref[...] += value


def _b_ln(d, xhat, rstd, f, lg, lb, gate, name, modulated=None, loss_head=False, exchange=None):
    S = d.shape[0]
    ts = min(TS_PROJ, S)
    has_mod = modulated is not None
    assert not (has_mod and loss_head)

    def body(*refs):
        if has_mod:
            (d_ref, xh_ref, rs_ref, f_ref, lg_ref, lb_ref, gate_ref, dri_ref, sc_ref,
             df_ref, dres_ref, dlg_ref, dlb_ref, dgate_ref, dsc_ref, dsh_ref) = refs
        elif loss_head:
            (d_ref, xh_ref, rs_ref, f_ref, lg_ref, lb_ref, gate_ref,
             df_ref, dres_ref, dlg_ref, dlb_ref, dgate_ref, sq_ref) = refs
        else:
            (d_ref, xh_ref, rs_ref, f_ref, lg_ref, lb_ref, gate_ref,
             df_ref, dres_ref, dlg_ref, dlb_ref, dgate_ref) = refs
        i = pl.program_id(0)
        xh = xh_ref[...]
        if has_mod:
            dh = d_ref[...]
            dxo_t = dh * (1.0 + sc_ref[...]) + dri_ref[...]
            _accumulate(i, dsc_ref, _colsum(dh * (xh * lg_ref[...] + lb_ref[...])))
            _accumulate(i, dsh_ref, _colsum(dh))
        elif loss_head:
            err = (xh * lg_ref[...] + lb_ref[...]) - d_ref[...]
            dxo_t = err * (1.0 / D)
            _accumulate(i, sq_ref, _colsum(err * err))
        else:
            dxo_t = d_ref[...]
        dxh = dxo_t * lg_ref[...]
        m1 = jnp.mean(dxh, axis=-1, keepdims=True)
        m2 = jnp.mean(dxh * xh, axis=-1, keepdims=True)
        dz = rs_ref[...] * (dxh - m1 - xh * m2)
        df_ref[...] = (dz * gate_ref[...]).astype(MXU_DTYPE)
        dres_ref[...] = ALPHA * dz
        _accumulate(i, dlg_ref, _colsum(dxo_t * xh))
        _accumulate(i, dlb_ref, _colsum(dxo_t))
        _accumulate(i, dgate_ref, _colsum(dz * f_ref[...].astype(F32)))

    tile = lambda w: pl.BlockSpec((ts, w), lambda i: (i, 0))
    n_sums = 5 if has_mod else 4 if loss_head else 3
    in_specs = [tile(D), tile(D), tile(1), tile(D), _row(D), _row(D), _row(D)]
    args = [d, xhat, rstd, f, lg, lb, gate]
    if has_mod:
        in_specs += [tile(D), _row(D)]
        args += list(modulated)
    return _call(
        body, name=name, grid=(S // ts,), in_specs=in_specs,
        out_specs=[tile(D), tile(D)] + [_row(D)] * n_sums,
        out_shape=[jax.ShapeDtypeStruct((S, D), MXU_DTYPE), jax.ShapeDtypeStruct((S, D), F32)]
        + [jax.ShapeDtypeStruct((1, D), F32)] * n_sums,
        args=args, exchange=exchange)


def _input_grad(dh, dres, scale, x, exchange=None):
    S = dh.shape[0]
    ts = min(TS_PROJ, S)

    def body(dh_ref, dres_ref, sc_ref, x_ref, dx_ref, dsc_ref, dsh_ref):
        i = pl.program_id(0)
        dh_t = dh_ref[...]
        dx_ref[...] = dh_t * (1.0 + sc_ref[...]) + dres_ref[...]
        _accumulate(i, dsc_ref, _colsum(dh_t * x_ref[...]))
        _accumulate(i, dsh_ref, _colsum(dh_t))

    tile = pl.BlockSpec((ts, D), lambda i: (i, 0))
    return _call(
        body, name="input_grad", grid=(S // ts,), in_specs=[tile, tile, _row(D), tile],
        out_specs=[tile, _row(D), _row(D)],
        out_shape=[jax.ShapeDtypeStruct((S, D), F32)] + [jax.ShapeDtypeStruct((1, D), F32)] * 2,
        args=(dh, dres, scale, x), exchange=exchange)


def _b2_ffn(df2, w_down_all, up, ge, dge, conv_w, layer, exchange=None):
    S = df2.shape[0]
    ts = min(TS_FF, S)
    n_tiles = S // ts
    n_ext = ts + CONV_HALO

    def body(df_ref, wd_ref, g_ref, val_ref, ge_ref, dge_ref, cw_ref, dup_ref, dcw_ref, dcb_ref, next_dgc):
        i = pl.program_id(0)

        @pl.when(i == 0)
        def _():
            next_dgc[...] = jnp.zeros_like(next_dgc)

        dact = _dot_nt(df_ref[...], wd_ref[...]).astype(GELU_DTYPE)
        dup_ref[:, D_FF:2 * D_FF] = (dact * ge_ref[...].astype(GELU_DTYPE)).astype(MXU_DTYPE)
        dgc = (dact * val_ref[...].astype(GELU_DTYPE) * dge_ref[...].astype(GELU_DTYPE)).astype(F32)
        ext = jnp.concatenate([dgc, next_dgc[...]], axis=0)
        dp1 = pltpu.roll(ext, n_ext - 1, 0)[:ts]
        dp2 = pltpu.roll(ext, n_ext - 2, 0)[:ts]
        dup_ref[:, 0:D_FF] = (dgc * cw_ref[2:3, :] + dp1 * cw_ref[1:2, :] + dp2 * cw_ref[0:1, :]).astype(MXU_DTYPE)
        next_dgc[...] = dgc[0:CONV_HALO]
        g = g_ref[...].astype(F32)
        dcw = jnp.concatenate([_colsum(dp2 * g), _colsum(dp1 * g), _colsum(dgc * g)], axis=0)
        _accumulate(i, dcw_ref, dcw)
        _accumulate(i, dcb_ref, _colsum(dgc))

    tile = lambda w, col=0: pl.BlockSpec((ts, w), lambda i: (n_tiles - 1 - i, col))
    return _call(
        body, name=f"b2_ffn_l{layer}", grid=(n_tiles,),
        in_specs=[tile(D), _layer_block((D_FF, D), layer), tile(D_FF), tile(D_FF, 1), tile(D_FF), tile(D_FF),
                  _full((3, D_FF))],
        out_specs=[tile(2 * D_FF), _full((3, D_FF)), _row(D_FF)],
        out_shape=[jax.ShapeDtypeStruct((S, 2 * D_FF), MXU_DTYPE),
                   jax.ShapeDtypeStruct((3, D_FF), F32), jax.ShapeDtypeStruct((1, D_FF), F32)],
        scratch_shapes=[pltpu.VMEM((CONV_HALO, D_FF), F32)],
        args=(df2, w_down_all, up, up, ge, dge, conv_w), exchange=exchange)


def _b5_mixers(df1, w_out_all, proj, gel, pool_w, pool_scale, sg, sb, wm, bias_full, layer):
    S = df1.shape[0]
    ts = min(TS_MIX, S)
    n_chunks = ts // CHUNK
    halo_blocks = ts // POOL_HALO
    last_halo = S // POOL_HALO - 1

    def body(df_ref, dfh_ref, wo_ref, p_ref, ah_ref, gel_ref, pw_ref, ps_ref, sg_ref, sb_ref, wm_ref, bias_ref,
             dp_ref, dpw_ref, dps_ref, dsg_ref, dsb_ref, dwm_ref, dbias_ref, z_scr, dvn_scr):
        i = pl.program_id(0)
        last = pl.num_programs(0) - 1
        dmix = _dot_nt(df_ref[...], wo_ref[...])
        dmix_halo = _dot_nt(dfh_ref[...], wo_ref[0:D_POOL, :]) * (i < last).astype(F32)

        a = p_ref[:, 0:D_POOL].astype(F32)
        halo = ah_ref[...].astype(F32) * (i > 0).astype(F32)
        pooled = _pool_forward(jnp.concatenate([halo, a], axis=0), a, i * ts, ts)
        n = ts + POOL_HALO
        dps_parts = []
        for g, window in enumerate(POOL_WINDOWS):
            cols = slice(g * GROUP, (g + 1) * GROUP)
            pooled_b = pooled[g].astype(MXU_DTYPE)
            mixed = _dot(pooled_b, pw_ref[g])
            dya = dmix[:, cols]
            dps_parts.append(_colsum(dya * mixed))
            dmixed = (dya * ps_ref[:, cols]).astype(MXU_DTYPE)
            dpw_g = _dot_tn(pooled_b, dmixed)

            @pl.when(i == 0)
            def _():
                dpw_ref[g] = dpw_g

            @pl.when(i > 0)
            def _():
                dpw_ref[g] += dpw_g

            dpooled = _dot_nt(dmixed, pw_ref[g])
            dmixed_h = (dmix_halo[:, cols] * ps_ref[:, cols]).astype(MXU_DTYPE)
            dpooled_h = _dot_nt(dmixed_h, pw_ref[g])
            q = dpooled * _pool_counts(i * ts, ts, window)
            s = jnp.concatenate([q, dpooled_h * (1.0 / window)], axis=0)
            k = 1
            while k < window:
                s = s + pltpu.roll(s, n - k, 0)
                k *= 2
            dp_ref[:, cols] = (s[:ts] - dpooled).astype(MXU_DTYPE)
        _accumulate(i, dps_ref, jnp.concatenate(dps_parts, axis=1))

        gu, dgu, gv, dgv = (gel_ref[:, k * D_SGU:(k + 1) * D_SGU].astype(F32) for k in range(4))
        vhat, rs = _sgu_norm(gv)
        vn = (vhat * sg_ref[...] + sb_ref[...]).astype(MXU_DTYPE)
        dyb = dmix[:, D_POOL:D]
        dz = dyb * gu
        dzb = dz.astype(MXU_DTYPE)
        dbias = dz[0:CHUNK]
        for c in range(1, n_chunks):
            dbias = dbias + dz[c * CHUNK:(c + 1) * CHUNK]
        _accumulate(i, dbias_ref, dbias)
        for h in range(HEADS):
            cols = slice(h * GROUP, (h + 1) * GROUP)
            vn_h = jnp.concatenate([vn[c * CHUNK:(c + 1) * CHUNK, cols] for c in range(n_chunks)], axis=1)
            dz_h = jnp.concatenate([dzb[c * CHUNK:(c + 1) * CHUNK, cols] for c in range(n_chunks)], axis=1)
            z_h = _dot(wm_ref[h], vn_h)
            dvn_h = _dot_tn(wm_ref[h], dz_h)
            dwm_h = _dot_nt(dz_h, vn_h)
            for c in range(n_chunks):
                rows = slice(c * CHUNK, (c + 1) * CHUNK)
                z_scr[rows, cols] = z_h[:, c * GROUP:(c + 1) * GROUP] + bias_ref[:, cols]
                dvn_scr[rows, cols] = dvn_h[:, c * GROUP:(c + 1) * GROUP]

            @pl.when(i == 0)
            def _():
                dwm_ref[h] = dwm_h

            @pl.when(i > 0)
            def _():
                dwm_ref[h] += dwm_h

        dp_ref[:, D_POOL:D_POOL + D_SGU] = (dyb * z_scr[...] * dgu).astype(MXU_DTYPE)
        dvn = dvn_scr[...]
        _accumulate(i, dsg_ref, _colsum(dvn * vhat))
        _accumulate(i, dsb_ref, _colsum(dvn))
        dvh = dvn * sg_ref[...]
        m1 = jnp.mean(dvh, axis=-1, keepdims=True)
        m2 = jnp.mean(dvh * vhat, axis=-1, keepdims=True)
        dp_ref[:, D_POOL + D_SGU:D_IN] = (rs * (dvh - m1 - vhat * m2) * dgv).astype(MXU_DTYPE)

        @pl.when(i == last)
        def _():
            tri = (lax.broadcasted_iota(jnp.int32, (CHUNK, CHUNK), 0)
                   >= lax.broadcasted_iota(jnp.int32, (CHUNK, CHUNK), 1))
            for h in range(HEADS):
                dwm_ref[h] = jnp.where(tri, dwm_ref[h], 0.0)

    tile = lambda w: pl.BlockSpec((ts, w), lambda i: (i, 0))
    return pl.pallas_call(
        body, name=f"b5_mixers_l{layer}", grid=(S // ts,),
        in_specs=[tile(D),
                  pl.BlockSpec((POOL_HALO, D), lambda i: (jnp.minimum((i + 1) * halo_blocks, last_halo), 0)),
                  _layer_block((D, D), layer), tile(D_POOL),
                  pl.BlockSpec((POOL_HALO, D_POOL), lambda i: (jnp.maximum(i * halo_blocks - 1, 0), 0)),
                  tile(4 * D_SGU),
                  _full((N_GROUPS, GROUP, GROUP)), _row(D_POOL), _row(D_SGU), _row(D_SGU),
                  _full((HEADS, CHUNK, CHUNK)), _full((CHUNK, D_SGU))],
        out_specs=[tile(D_IN), _full((N_GROUPS, GROUP, GROUP)), _row(D_POOL), _row(D_SGU), _row(D_SGU),
                   _full((HEADS, CHUNK, CHUNK)), _full((CHUNK, D_SGU))],
        out_shape=[jax.ShapeDtypeStruct((S, D_IN), MXU_DTYPE), jax.ShapeDtypeStruct((N_GROUPS, GROUP, GROUP), F32),
                   jax.ShapeDtypeStruct((1, D_POOL), F32), jax.ShapeDtypeStruct((1, D_SGU), F32),
                   jax.ShapeDtypeStruct((1, D_SGU), F32), jax.ShapeDtypeStruct((HEADS, CHUNK, CHUNK), F32),
                   jax.ShapeDtypeStruct((CHUNK, D_SGU), F32)],
        scratch_shapes=[pltpu.VMEM((ts, D_SGU), F32), pltpu.VMEM((ts, D_SGU), F32)],
        compiler_params=_params(1),
    )(df1, df1, w_out_all, proj, proj, gel, pool_w, pool_scale, sg, sb, wm, bias_full)


def _weight_grad(a, bs, tn, name, rows=TS_TN):
    S, M = a.shape
    counts = [b.shape[1] // tn for b in bs]
    starts = [sum(counts[:k]) for k in range(len(bs))]
    ts = min(rows, S)

    def body(a_ref, *refs):
        b_refs, o_ref = refs[:-1], refs[-1]
        j = pl.program_id(0)
        for b_ref, lo, n in zip(b_refs, starts, counts):
            @pl.when(jnp.logical_and(j >= lo, j < lo + n))
            def _():
                _accumulate(pl.program_id(1), o_ref, _dot_tn(a_ref[...], b_ref[...]))

    b_spec = lambda lo, n: pl.BlockSpec((ts, tn), lambda j, i: (i, jnp.clip(j - lo, 0, n - 1)))
    return pl.pallas_call(
        body, name=name, grid=(sum(counts), S // ts),
        in_specs=[pl.BlockSpec((ts, M), lambda j, i: (i, 0))] + [b_spec(lo, n) for lo, n in zip(starts, counts)],
        out_specs=pl.BlockSpec((M, tn), lambda j, i: (0, j)),
        out_shape=jax.ShapeDtypeStruct((M, sum(counts) * tn), F32), compiler_params=_params(2),
    )(a, *bs)


def _silu(x):
    return x * (1.0 / (1.0 + jnp.exp(-x)))


def _ada_forward(c_all, ada_w, ada_b_cols):
    n_cols = ada_w.shape[2]
    tc = 512

    def body(c_ref, w_ref, b_ref, o_ref):
        ca = _silu(c_ref[...]).astype(MXU_DTYPE)
        o_ref[...] = _dot(ca, w_ref[...].astype(MXU_DTYPE)) + b_ref[...]

    return pl.pallas_call(
        body, name="ada_forward", grid=(DEPTH, n_cols // tc),
        in_specs=[pl.BlockSpec((16, D), lambda l, j: (0, 0)), pl.BlockSpec((None, D, tc), lambda l, j: (l, 0, j)),
                  pl.BlockSpec((None, 1, tc), lambda l, j: (l, 0, j))],
        out_specs=pl.BlockSpec((None, 16, tc), lambda l, j: (l, 0, j)),
        out_shape=jax.ShapeDtypeStruct((DEPTH, 16, n_cols), F32),
        compiler_params=_params(2),
    )(c_all, ada_w, ada_b_cols)


def _ada_backward(c_all, dmod_cols):
    n_cols = dmod_cols.shape[2]
    tc = 512

    def body(c_ref, d_ref, o_ref):
        ca = _silu(c_ref[...]).astype(MXU_DTYPE)
        o_ref[...] = _dot_tn(ca, d_ref[...].astype(MXU_DTYPE))

    return pl.pallas_call(
        body, name="ada_backward", grid=(DEPTH, n_cols // tc),
        in_specs=[pl.BlockSpec((16, D), lambda l, j: (0, 0)), pl.BlockSpec((None, 16, tc), lambda l, j: (l, 0, j))],
        out_specs=pl.BlockSpec((None, D, tc), lambda l, j: (l, 0, j)),
        out_shape=jax.ShapeDtypeStruct((DEPTH, D, n_cols), F32),
        compiler_params=_params(2),
    )(c_all, dmod_cols)


def _adamw(w, g, m, v, name):
    R, C = w.shape
    tr = R
    for cand in (512, 256, 128, 64, 32, 16, 8):
        if R % cand == 0 and cand * C * 4 <= 2 ** 21:
            tr = cand
            break
    c1 = 1.0 - ADAM_B1 ** ADAM_STEP
    c2 = 1.0 - ADAM_B2 ** ADAM_STEP

    def body(w_ref, g_ref, m_ref, v_ref, d_ref, mo_ref, vo_ref):
        gg = g_ref[...]
        mn = ADAM_B1 * m_ref[...] + (1.0 - ADAM_B1) * gg
        vn = ADAM_B2 * v_ref[...] + (1.0 - ADAM_B2) * (gg * gg)
        mo_ref[...] = mn
        vo_ref[...] = vn
        d_ref[...] = -ADAM_LR * ((mn / c1) / (jnp.sqrt(vn / c2) + ADAM_EPS) + ADAM_WD * w_ref[...])

    tile = pl.BlockSpec((tr, C), lambda i: (i, 0))
    return pl.pallas_call(
        body, name=name, grid=(R // tr,), in_specs=[tile] * 4, out_specs=[tile] * 3,
        out_shape=[jax.ShapeDtypeStruct((R, C), F32)] * 3, compiler_params=_params(1),
    )(w, g, m, v)


def _position():
    x, y, c = lax.axis_index("x"), lax.axis_index("y"), lax.axis_index("c")
    other_chips = [(1 - x, y), (x, 1 - y), (1 - x, 1 - y)]
    return x, y, c, other_chips


def _all_gather8(block, name):
    R, C = block.shape

    def body(x_ref, out_ref, send_sems, recv_sems, local_sem):
        x, y, c, chips = _position()
        me, sibling = (x, y, c), (x, y, 1 - c)

        def rows(px, py, pc):
            return out_ref.at[pl.ds((4 * px + 2 * py + pc) * R, R), :]

        def copy(k, blk, to, src=None):
            return pltpu.make_async_remote_copy(
                src_ref=rows(*blk) if src is None else src, dst_ref=rows(*blk),
                send_sem=send_sems.at[k], recv_sem=recv_sems.at[k], device_id=to, device_id_type=MESH)

        mine = pltpu.make_async_copy(x_ref, rows(*me), local_sem)
        mine.start()
        first = [copy(0, me, sibling, src=x_ref)]
        first += [copy(1 + j, me, (*chip, c), src=x_ref) for j, chip in enumerate(chips)]
        for cp in first:
            cp.start()
        passed = [copy(4 + j, (*chip, c), sibling) for j, chip in enumerate(chips)]
        for j, chip in enumerate(chips):
            copy(1 + j, (*chip, c), me).wait_recv()
            passed[j].start()
        copy(0, sibling, me).wait_recv()
        for j, chip in enumerate(chips):
            copy(4 + j, (*chip, 1 - c), me).wait_recv()
        for cp in first + passed:
            cp.wait_send()
        mine.wait()

    return pl.pallas_call(
        body, name=name, out_shape=jax.ShapeDtypeStruct((N_DEV * R, C), block.dtype),
        in_specs=[pl.BlockSpec(memory_space=pltpu.VMEM)], out_specs=ANY,
        scratch_shapes=[pltpu.SemaphoreType.DMA((7,)), pltpu.SemaphoreType.DMA((7,)), pltpu.SemaphoreType.DMA(())],
    )(block)


def _gather_flat(vec, name):
    n = vec.shape[0]
    padded = -(-n // 1024) * 1024
    block = jnp.pad(vec, (0, padded - n)).reshape(8, padded // 8)
    out = _all_gather8(block, name)
    return out.reshape(N_DEV, padded)[:, :n]


_SHARD_KINDS = ("cols", "rows", "cols", "rows")


def _shard_of(ref, shape, kind, chip):
    m, n = shape
    return ref.at[:, pl.ds(chip * (n // 4), n // 4)] if kind == "cols" else ref.at[pl.ds(chip * (m // 4), m // 4), :]


def _half_of(ref, shape, kind, h):
    m, n = shape
    return ref.at[pl.ds(h * (m // 2), m // 2), :] if kind == "cols" else ref.at[:, pl.ds(h * (n // 2), n // 2)]


def _half_shape(shape, kind):
    m, n = shape
    return (m // 2, n) if kind == "cols" else (m, n // 2)


def _shard_shape(shape, kind):
    m, n = shape
    return (m, n // 4) if kind == "cols" else (m // 4, n)


def _remote(src, dst, k, to, send_sems, recv_sems):
    return pltpu.make_async_remote_copy(src_ref=src, dst_ref=dst, send_sem=send_sems.at[k], recv_sem=recv_sems.at[k],
                                        device_id=to, device_id_type=MESH)


def _gather_exchange(shards, full, kind, layer):
    shape = full.shape[1:]

    def copies(xin, xout, send_sems, recv_sems):
        x, y, c, chips = _position()
        my_chip = 2 * x + y
        own = xin[0].at[layer]
        place = lambda chip: _shard_of(xout[0].at[layer], shape, kind, chip)
        peers = [(x, y, 1 - c)] + [(*chip, c) for chip in chips]
        sources = [my_chip] + [2 * cx + cy for cx, cy in chips]
        sends = [_remote(own, place(my_chip), k, peer, send_sems, recv_sems) for k, peer in enumerate(peers)]
        arrivals = [_remote(place(s), place(s), k, (x, y, c), send_sems, recv_sems) for k, s in enumerate(sources)]
        return sends, arrivals

    return _Exchange([shards, full], [1], [], 4, copies)


def _swap_exchange(grads):
    shapes = [g.shape for g in grads]
    fresh = [jax.ShapeDtypeStruct(_half_shape(s, k), F32) for s, k in zip(shapes, _SHARD_KINDS)]

    def copies(xin, xout, send_sems, recv_sems):
        x, y, c, _ = _position()
        sends = [_remote(_half_of(xin[a], shapes[a], _SHARD_KINDS[a], 1 - c), xout[a], a, (x, y, 1 - c),
                         send_sems, recv_sems) for a in range(len(grads))]
        arrivals = [_remote(xout[a], xout[a], a, (x, y, c), send_sems, recv_sems) for a in range(len(grads))]
        return sends, arrivals

    return _Exchange(list(grads), [], fresh, len(grads), copies)


def _scatter_exchange(partials):
    n_w = len(partials)
    shapes = [p.shape for p in partials]
    fresh = [jax.ShapeDtypeStruct((3,) + _shard_shape(s, k), p.dtype) for s, k, p in zip(shapes, _SHARD_KINDS, partials)]

    def copies(xin, xout, send_sems, recv_sems):
        x, y, c, chips = _position()
        sends, arrivals = [], []
        for j, (cx, cy) in enumerate(chips):
            for a in range(n_w):
                src = _shard_of(xin[a], shapes[a], _SHARD_KINDS[a], 2 * cx + cy)
                sends.append(_remote(src, xout[a].at[j], j * n_w + a, (cx, cy, c), send_sems, recv_sems))
                arrivals.append(_remote(xout[a].at[j], xout[a].at[j], j * n_w + a, (x, y, c), send_sems, recv_sems))
        return sends, arrivals

    return _Exchange(list(partials), [], fresh, 3 * n_w, copies)


def _share_exchange(reduced, layer):
    n_w = len(reduced)
    shapes = [r.shape[1:] for r in reduced]

    def copies(xin, xout, send_sems, recv_sems):
        x, y, c, _ = _position()
        half = lambda a, h: _half_of(xout[a].at[layer], shapes[a], _SHARD_KINDS[a], h)
        sends = [_remote(half(a, c), half(a, c), a, (x, y, 1 - c), send_sems, recv_sems) for a in range(n_w)]
        arrivals = [_remote(half(a, 1 - c), half(a, 1 - c), a, (x, y, c), send_sems, recv_sems) for a in range(n_w)]
        return sends, arrivals

    return _Exchange(list(reduced), list(range(n_w)), [], n_w, copies)


def _chip_partial(pos, grad, theirs, kind, name):
    M, N = grad.shape
    if kind == "cols":
        tm = 256
        steps = M // 2 // tm
        block, g_map = (tm, N), (lambda i, pos: (pos[0] * steps + i, 0))
    else:
        tm = M // 4
        steps = 4
        block, g_map = (tm, N // 2), (lambda i, pos: (i, pos[0]))

    def body(pos_ref, g_ref, t_ref, o_ref):
        o_ref[...] = (g_ref[...] + t_ref[...]).astype(WIRE_DTYPE)

    grid_spec = pltpu.PrefetchScalarGridSpec(
        num_scalar_prefetch=1, grid=(steps,),
        in_specs=[pl.BlockSpec(block, g_map), pl.BlockSpec(block, lambda i, pos: (i, 0))],
        out_specs=pl.BlockSpec(block, lambda i, pos: (i, 0)))
    return pl.pallas_call(
        body, name=name, grid_spec=grid_spec, out_shape=jax.ShapeDtypeStruct(_half_shape((M, N), kind), WIRE_DTYPE),
        compiler_params=_params(1),
    )(pos, grad, theirs)


def _reduce_shard(pos, grad, theirs, received, reduced, kind, layer, name):
    M, N = grad.shape
    if kind == "cols":
        tm = min(M // 2, 512)
        steps = M // 2 // tm
        block = (tm, N // 4)
        g_map = lambda i, pos: (pos[0] * steps + i, pos[1])
        t_map = lambda i, pos: (i, pos[1])
        o_map = lambda i, pos: (layer, pos[0] * steps + i, 0)
    else:
        steps = 1
        block = (M // 4, N // 2)
        g_map = lambda i, pos: (pos[1], pos[0])
        t_map = lambda i, pos: (pos[1], 0)
        o_map = lambda i, pos: (layer, 0, pos[0])
    out_shape = (DEPTH,) + _shard_shape((M, N), kind)

    def body(pos_ref, g_ref, t_ref, r0_ref, r1_ref, r2_ref, *rest):
        o_ref = rest[-1]
        chip = pos_ref[1]
        own = g_ref[...] + t_ref[...]
        r = [r0_ref[...].astype(F32), r1_ref[...].astype(F32), r2_ref[...].astype(F32)]
        total = None
        for s in range(N_CHIPS):
            rel = jnp.bitwise_xor(chip, s)
            term = jnp.where(rel == 0, own, jnp.where(rel == 2, r[0], jnp.where(rel == 1, r[1], r[2])))
            total = term if total is None else total + term
        o_ref[...] = total

    r_spec = lambda j: pl.BlockSpec((None,) + block, lambda i, pos: (j, i, 0))
    in_specs = [pl.BlockSpec(block, g_map), pl.BlockSpec(block, t_map), r_spec(0), r_spec(1), r_spec(2)]
    args = [pos, grad, theirs, received, received, received]
    aliases = {}
    if reduced is not None:
        in_specs.append(ANY)
        args.append(reduced)
        aliases = {6: 0}
    grid_spec = pltpu.PrefetchScalarGridSpec(
        num_scalar_prefetch=1, grid=(steps,), in_specs=in_specs, out_specs=pl.BlockSpec((None,) + block, o_map))
    return pl.pallas_call(
        body, name=name, grid_spec=grid_spec, out_shape=jax.ShapeDtypeStruct(out_shape, F32),
        input_output_aliases=aliases, compiler_params=_params(1),
    )(*args)


def _sum_devices(gathered):
    R8, C = gathered.shape
    R = R8 // N_DEV
    lanes = C // 128
    tc = 128 * max(k for k in range(1, lanes + 1) if lanes % k == 0 and k * 128 * R8 * 4 <= 2 ** 22)

    def body(g_ref, o_ref):
        total = g_ref[0:R, :]
        for d in range(1, N_DEV):
            total = total + g_ref[d * R:(d + 1) * R, :]
        o_ref[...] = total

    return pl.pallas_call(
        body, name="sum_devices", grid=(C // tc,),
        in_specs=[pl.BlockSpec((R8, tc), lambda j: (0, j))], out_specs=pl.BlockSpec((R, tc), lambda j: (0, j)),
        out_shape=jax.ShapeDtypeStruct((R, C), F32), compiler_params=_params(1),
    )(gathered)


_SMALL = ("ada_b", "pool_w", "pool_scale", "sgu_ln_g", "sgu_ln_b", "sgu_w", "sgu_b", "ln1_g", "ln1_b", "conv_w",
          "conv_b", "ln2_g", "ln2_b")
_WEIGHTS = ("ada_w", "ada_b", "w_in", "pool_w", "pool_scale", "sgu_ln_g", "sgu_ln_b", "sgu_w", "sgu_b", "w_out",
            "ln1_g", "ln1_b", "w_up", "conv_w", "conv_b", "w_down", "ln2_g", "ln2_b")


def kernel(x, c, ada_w, ada_b, w_in, pool_w, pool_scale, sgu_ln_g, sgu_ln_b, sgu_w, sgu_b, w_out, ln1_g, ln1_b, w_up, conv_w, conv_b, w_down, ln2_g, ln2_b, loss_target, m_ada_w, m_ada_b, m_w_in, m_pool_w, m_pool_scale, m_sgu_ln_g, m_sgu_ln_b, m_sgu_w, m_sgu_b, m_w_out, m_ln1_g, m_ln1_b, m_w_up, m_conv_w, m_conv_b, m_w_down, m_ln2_g, m_ln2_b, v_ada_w, v_ada_b, v_w_in, v_pool_w, v_pool_scale, v_sgu_ln_g, v_sgu_ln_b, v_sgu_w, v_sgu_b, v_w_out, v_ln1_g, v_ln1_b, v_w_up, v_conv_w, v_conv_b, v_w_down, v_ln2_g, v_ln2_b):
    weights = dict(ada_w=ada_w, ada_b=ada_b, w_in=w_in, pool_w=pool_w, pool_scale=pool_scale, sgu_ln_g=sgu_ln_g,
                   sgu_ln_b=sgu_ln_b, sgu_w=sgu_w, sgu_b=sgu_b, w_out=w_out, ln1_g=ln1_g, ln1_b=ln1_b, w_up=w_up,
                   conv_w=conv_w, conv_b=conv_b, w_down=w_down, ln2_g=ln2_g, ln2_b=ln2_b)
    mom_m = dict(ada_w=m_ada_w, ada_b=m_ada_b, w_in=m_w_in, pool_w=m_pool_w, pool_scale=m_pool_scale,
                 sgu_ln_g=m_sgu_ln_g, sgu_ln_b=m_sgu_ln_b, sgu_w=m_sgu_w, sgu_b=m_sgu_b, w_out=m_w_out,
                 ln1_g=m_ln1_g, ln1_b=m_ln1_b, w_up=m_w_up, conv_w=m_conv_w, conv_b=m_conv_b, w_down=m_w_down,
                 ln2_g=m_ln2_g, ln2_b=m_ln2_b)
    mom_v = dict(ada_w=v_ada_w, ada_b=v_ada_b, w_in=v_w_in, pool_w=v_pool_w, pool_scale=v_pool_scale,
                 sgu_ln_g=v_sgu_ln_g, sgu_ln_b=v_sgu_ln_b, sgu_w=v_sgu_w, sgu_b=v_sgu_b, w_out=v_w_out,
                 ln1_g=v_ln1_g, ln1_b=v_ln1_b, w_up=v_w_up, conv_w=v_conv_w, conv_b=v_conv_b, w_down=v_w_down,
                 ln2_g=v_ln2_g, ln2_b=v_ln2_b)

    ix, iy, ic = lax.axis_index("x"), lax.axis_index("y"), lax.axis_index("c")
    chip = 2 * ix + iy
    dev = 4 * ix + 2 * iy + ic
    pos = jnp.stack([ic, chip]).astype(jnp.int32)
    xs = x.reshape(x.shape[1:])
    target = loss_target.reshape(loss_target.shape[1:])
    ff_shard = conv_w.shape[2]
    mod_shard = ada_w.shape[2]

    first = _gather_flat(jnp.concatenate([c.reshape(-1), conv_w.reshape(-1)]), "gather_c_conv")
    c_all = jnp.pad(first[:, :D], ((0, 8), (0, 0)))
    conv_parts = first[0::2, D:].reshape(N_CHIPS, DEPTH, 3, ff_shard)
    conv_full = jnp.transpose(conv_parts, (1, 2, 0, 3)).reshape(DEPTH, 3, D_FF)
    ada_b_cols = lax.dynamic_slice_in_dim(ada_b, chip * mod_shard, mod_shard, axis=1).reshape(DEPTH, 1, mod_shard)
    mod_part = _ada_forward(c_all, ada_w, ada_b_cols)[:, :8, :]
    mod_all = _gather_flat(mod_part.reshape(-1), "gather_mod").reshape(N_DEV, DEPTH, 8, mod_shard)
    mod_mine = lax.dynamic_index_in_dim(mod_all[0::2], dev, axis=2, keepdims=False)
    mod = jnp.transpose(mod_mine, (1, 0, 2)).reshape(DEPTH, 6, 1, D)

    sh_in, sh_out, sh_up, sh_down = (w.astype(WIRE_DTYPE) for w in (w_in, w_out, w_up, w_down))
    wf_in = lax.empty((DEPTH, D, D_IN), WIRE_DTYPE)
    wf_out = lax.empty((DEPTH, D, D), WIRE_DTYPE)
    wf_up = lax.empty((DEPTH, D, 2 * D_FF), WIRE_DTYPE)
    wf_down = lax.empty((DEPTH, D_FF, D), WIRE_DTYPE)
    (wf_in,) = _run_exchange(_gather_exchange(sh_in, wf_in, "cols", 0), "gather_w_in_l0")

    tri = jnp.tril(jnp.ones((CHUNK, CHUNK), dtype=bool))
    ones_row = jnp.ones((1, D), F32)
    zeros_row = jnp.zeros((1, D), F32)
    row = lambda a, l: a[l].reshape(1, -1)

    saved = []
    xn, lg, lb = xs, ones_row, zeros_row
    h = None
    for l in range(DEPTH):
        shift1, scale1, gate1, shift2, scale2, gate2 = (mod[l, k] for k in range(6))
        nxt = min(l + 1, DEPTH - 1)
        pw = pool_w[l].astype(MXU_DTYPE)
        wm = jnp.where(tri[None], sgu_w[l], 0.0).astype(MXU_DTYPE)
        bias_full = jnp.repeat(jnp.transpose(sgu_b[l]), GROUP, axis=1)
        lg1, lb1 = row(ln1_g, l), row(ln1_b, l)
        if l == 0:
            (h, proj), (wf_out,) = _modulated_matmul(xs, shift1, scale1, wf_in, l, STORE_DTYPE, f"f1_in_proj_l{l}",
                                                     exchange=_gather_exchange(sh_out, wf_out, "rows", l))
        else:
            proj, (wf_out,) = _matmul(h, wf_in, l, D_IN, STORE_DTYPE, f"f1_in_proj_l{l}",
                                      exchange=_gather_exchange(sh_out, wf_out, "rows", l))
        (mix, f1, xhat1, rstd1, h2, gel), (wf_up,) = _f2_mixers(
            proj, xn, lg, lb, gate1, pw, row(pool_scale, l), row(sgu_ln_g, l), row(sgu_ln_b, l), wm, bias_full,
            wf_out, lg1, lb1, shift2, scale2, l, exchange=_gather_exchange(sh_up, wf_up, "cols", l))
        up, (wf_down,) = _matmul(h2, wf_up, l, D_FF, STORE_DTYPE, f"f3_up_proj_l{l}",
                                 exchange=_gather_exchange(sh_down, wf_down, "rows", l))
        (act, f2, xhat2, rstd2, h_next, ge, dge), filled = _f4_ffn(
            up, xhat1, lg1, lb1, gate2, conv_full[l], row(conv_b, l), wf_down, row(ln2_g, l), row(ln2_b, l),
            mod[nxt, 0], mod[nxt, 1], l,
            exchange=_gather_exchange(sh_in, wf_in, "cols", l + 1) if l + 1 < DEPTH else None)
        if filled:
            (wf_in,) = filled
        saved.append(dict(xn=xn, lg=lg, lb=lb, h=h, proj=proj, mix=mix, f1=f1, xhat1=xhat1, rstd1=rstd1, h2=h2,
                          up=up, act=act, f2=f2, xhat2=xhat2, rstd2=rstd2, pw=pw, wm=wm, bias_full=bias_full,
                          ge=ge, dge=dge, gel=gel))
        xn, lg, lb, h = xhat2, row(ln2_g, l), row(ln2_b, l), h_next

    d_out = target

    tags = ("in", "out", "up", "down")
    small = {n: [None] * DEPTH for n in _SMALL}
    dmod = [[None] * 6 for _ in range(DEPTH)]
    below = None
    ready = None
    reduced = [None] * 4

    def partial_sums(layer, full, theirs):
        return [_chip_partial(pos, g, t, kind, f"chip_partial_{tag}_l{layer}")
                for g, t, kind, tag in zip(full, theirs, _SHARD_KINDS, tags)]

    def chip_sums(layer, full, theirs, received):
        return [_reduce_shard(pos, g, t, r, out, kind, layer, f"reduce_shard_{tag}_l{layer}")
                for g, t, r, out, kind, tag in zip(full, theirs, received, reduced, _SHARD_KINDS, tags)]

    for l in reversed(range(DEPTH)):
        sv = saved[l]
        scale1, gate1, scale2, gate2 = mod[l, 1], mod[l, 2], mod[l, 4], mod[l, 5]
        lg1, lb1 = row(ln1_g, l), row(ln1_b, l)
        outs, theirs = _b_ln(d_out, sv["xhat2"], sv["rstd2"], sv["f2"], row(ln2_g, l), row(ln2_b, l), gate2,
                             f"b1_ln2_l{l}", modulated=below, loss_head=below is None,
                             exchange=_swap_exchange(ready[1]) if ready else None)
        df2, dres2, dlg2, dlb2, dmod[l][5] = outs[:5]
        if below is not None:
            dmod[l + 1][1], dmod[l + 1][0] = outs[5], outs[6]
        else:
            loss = lax.psum(jnp.sum(outs[5]) * (0.5 / D), ("x", "y", "c"))
        partials = partial_sums(ready[0], ready[1], theirs) if ready else None
        (dup, dcw, dcb), received = _b2_ffn(df2, wf_down, sv["up"], sv["ge"], sv["dge"], conv_full[l], l,
                                            exchange=_scatter_exchange(partials) if ready else None)
        if ready:
            reduced = chip_sums(ready[0], ready[1], theirs, received)
        g_down = _weight_grad(sv["act"], [df2], 512, f"wg_down_l{l}", rows=2 * TS_TN)
        dh2, shared = _matmul_nt(dup, wf_up, l, D_FF, f"b3_up_l{l}",
                                 exchange=_share_exchange(reduced, ready[0]) if ready else None)
        if ready:
            reduced = shared
        g_up = _weight_grad(sv["h2"], [dup], 1408, f"wg_up_l{l}", rows=2 * TS_TN)
        (df1, dres1, dlg1, dlb1, dmod[l][2], dmod[l][4], dmod[l][3]), _ = _b_ln(
            dh2, sv["xhat1"], sv["rstd1"], sv["f1"], lg1, lb1, gate1, f"b4_ln1_l{l}", modulated=(dres2, scale2))
        dproj, dpw, dps, dsg, dsb, dwm, dbias = _b5_mixers(
            df1, wf_out, sv["proj"], sv["gel"], sv["pw"], row(pool_scale, l), row(sgu_ln_g, l), row(sgu_ln_b, l), sv["wm"],
            sv["bias_full"], l)
        g_out = _weight_grad(sv["mix"], [df1], D, f"wg_out_l{l}", rows=2 * TS_TN)
        d_out, _ = _matmul_nt(dproj, wf_in, l, D_IN, f"b6_in_l{l}")
        g_in = _weight_grad(sv["h"], [dproj], 768, f"wg_in_l{l}", rows=2 * TS_TN)
        ready = (l, [g_in, g_out, g_up, g_down])
        below = (dres1, scale1)
        small["pool_w"][l], small["pool_scale"][l] = dpw, dps[0]
        small["sgu_ln_g"][l], small["sgu_ln_b"][l], small["sgu_w"][l] = dsg[0], dsb[0], dwm
        small["sgu_b"][l] = jnp.transpose(jnp.sum(dbias.reshape(CHUNK, HEADS, GROUP), axis=2))
        small["ln1_g"][l], small["ln1_b"][l] = dlg1[0], dlb1[0]
        small["conv_w"][l], small["conv_b"][l] = dcw, dcb[0]
        small["ln2_g"][l], small["ln2_b"][l] = dlg2[0], dlb2[0]
    (grad_x2d, dmod[0][1], dmod[0][0]), _ = _input_grad(d_out, below[0], below[1], xs)
    grad_x = grad_x2d.reshape(x.shape)
    small["ada_b"] = [jnp.concatenate([part[0] for part in dmod[l]]) for l in range(DEPTH)]

    names = _SMALL
    flat = jnp.concatenate([jnp.stack(small[n]).reshape(-1) for n in names])
    n_small = flat.shape[0]
    padded = -(-n_small // 1024) * 1024
    gathered = _all_gather8(jnp.pad(flat, (0, padded - n_small)).reshape(8, padded // 8), "gather_small_grads")
    summed = _sum_devices(gathered).reshape(-1)[:n_small]
    grads = {}
    offset = 0
    for n in names:
        size = math.prod(weights[n].shape[1:]) * DEPTH if n != "conv_w" else DEPTH * 3 * D_FF
        grads[n] = summed[offset:offset + size]
        offset += size
    grads["conv_w"] = lax.dynamic_slice_in_dim(grads["conv_w"].reshape(DEPTH, 3, D_FF), chip * ff_shard, ff_shard,
                                               axis=2)
    for n in names:
        grads[n] = grads[n].reshape(weights[n].shape)
    dmod_all = gathered[0::8, :DEPTH * 6 * D].reshape(N_DEV, DEPTH, 6 * D)
    dmod_cols = lax.dynamic_slice_in_dim(jnp.transpose(dmod_all, (1, 0, 2)), chip * mod_shard, mod_shard, axis=2)
    grads["ada_w"] = _ada_backward(c_all, jnp.pad(dmod_cols, ((0, 0), (0, 8), (0, 0))))

    theirs = _run_exchange(_swap_exchange(ready[1]), "swap_halves_l0")
    partials = partial_sums(0, ready[1], theirs)
    received = _run_exchange(_scatter_exchange(partials), "scatter_partials_l0")
    reduced = chip_sums(0, ready[1], theirs, received)
    grads["w_in"], grads["w_out"], grads["w_up"], grads["w_down"] = _run_exchange(
        _share_exchange(reduced, 0), "share_reduced_l0")

    delta, new_m, new_v = {}, {}, {}
    for n in ("ada_w", "w_in", "w_out", "w_up", "w_down"):
        shape = weights[n].shape
        two_d = (shape[0] * shape[1], shape[2])
        d_, m_, v_ = _adamw(weights[n].reshape(two_d), grads[n].reshape(two_d), mom_m[n].reshape(two_d),
                            mom_v[n].reshape(two_d), f"adamw_{n}")
        delta[n], new_m[n], new_v[n] = d_.reshape(shape), m_.reshape(shape), v_.reshape(shape)
    sizes = [math.prod(weights[n].shape) for n in names]
    total = sum(sizes)
    padded = -(-total // 1024) * 1024
    pack = lambda d: jnp.pad(jnp.concatenate([d[n].reshape(-1) for n in names]), (0, padded - total)).reshape(8, -1)
    v_pack = jnp.pad(jnp.concatenate([mom_v[n].reshape(-1) for n in names]), (0, padded - total),
                     constant_values=1.0).reshape(8, -1)
    d_, m_, v_ = _adamw(pack(weights), pack(grads), pack(mom_m), v_pack, "adamw_small")
    offset = 0
    for n, size in zip(names, sizes):
        for src, dst in ((d_, delta), (m_, new_m), (v_, new_v)):
            dst[n] = src.reshape(-1)[offset:offset + size].reshape(weights[n].shape)
        offset += size

    return (loss, grad_x, *[grads[n] for n in _WEIGHTS], *[delta[n] for n in _WEIGHTS],
            *[new_m[n] for n in _WEIGHTS], *[new_v[n] for n in _WEIGHTS])
```

```python
import functools
import math

import jax
import jax.numpy as jnp
from jax import lax
from jax.experimental import pallas as pl
from jax.experimental.pallas import tpu as pltpu

F32 = jnp.float32
MXU_DTYPE = jnp.bfloat16
WIRE_DTYPE = jnp.bfloat16

DEPTH = 4
D = 1024
D_POOL = 512
D_SGU = 512
N_GROUPS = 4
GROUP = 128
POOL_WINDOWS = (2, 4, 8, 16)
POOL_HALO = 16
CHUNK = 128
HEADS = 4
D_IN = D_POOL + 2 * D_SGU
D_FF = 2816
CONV_HALO = 16
STORE_DTYPE = jnp.bfloat16
GELU_DTYPE = jnp.bfloat16
N_CHIPS = 4
N_DEV = 8
ALPHA = (2.0 * DEPTH) ** 0.25
LN_EPS = 1e-5
ADAM_LR, ADAM_B1, ADAM_B2, ADAM_EPS, ADAM_WD, ADAM_STEP = 0.001, 0.9, 0.999, 1e-08, 0.01, 10

TS_PROJ = 512
TS_MM = 1024
TS_MIX = 512
TS_FF = 256
TS_TN = 1024
VMEM_LIMIT = 52 * 2 ** 20

MESH = pl.DeviceIdType.MESH
ANY = pl.BlockSpec(memory_space=pl.ANY)

_GELU_K0 = math.sqrt(2.0 / math.pi)
_GELU_K1 = 0.044715


def _params(n_axes):
    return pltpu.CompilerParams(dimension_semantics=("arbitrary",) * n_axes, vmem_limit_bytes=VMEM_LIMIT)


class _Exchange:
    def __init__(self, inputs, inplace, fresh, n_copies, copies):
        self.inputs, self.inplace, self.fresh, self.n_copies, self.copies = inputs, inplace, fresh, n_copies, copies

    def out_shapes(self):
        return [jax.ShapeDtypeStruct(self.inputs[i].shape, self.inputs[i].dtype) for i in self.inplace] + list(self.fresh)

    def semaphores(self):
        return [pltpu.SemaphoreType.DMA((self.n_copies,)), pltpu.SemaphoreType.DMA((self.n_copies,))]

    def aliases(self, first_input, first_output):
        return {first_input + i: first_output + k for k, i in enumerate(self.inplace)}


def _call(body, *, name, grid, in_specs, out_specs, out_shape, args, scratch_shapes=(), exchange=None):
    n_axes = len(grid)
    in_specs, out_specs, out_shape, scratch_shapes = list(in_specs), list(out_specs), list(out_shape), list(scratch_shapes)
    if exchange is None:
        outs = pl.pallas_call(body, name=name, grid=grid, in_specs=in_specs, out_specs=out_specs, out_shape=out_shape,
                              scratch_shapes=scratch_shapes, compiler_params=_params(n_axes))(*args)
        return list(outs), []
    n_in, n_out, n_scr = len(in_specs), len(out_shape), len(scratch_shapes)
    x_out = exchange.out_shapes()
    n_xin, n_xout = len(exchange.inputs), len(x_out)

    def hosted(*refs):
        ins, xin = refs[:n_in], refs[n_in:n_in + n_xin]
        o = n_in + n_xin
        outs, xout = refs[o:o + n_out], refs[o + n_out:o + n_out + n_xout]
        s = o + n_out + n_xout
        scr, (send_sems, recv_sems) = refs[s:s + n_scr], refs[s + n_scr:]
        first = functools.reduce(jnp.logical_and, [pl.program_id(d) == 0 for d in range(n_axes)])
        last = functools.reduce(jnp.logical_and, [pl.program_id(d) == pl.num_programs(d) - 1 for d in range(n_axes)])

        @pl.when(first)
        def _():
            for cp in exchange.copies(xin, xout, send_sems, recv_sems)[0]:
                cp.start()

        body(*ins, *outs, *scr)

        @pl.when(last)
        def _():
            sends, arrivals = exchange.copies(xin, xout, send_sems, recv_sems)
            for cp in arrivals:
                cp.wait_recv()
            for cp in sends:
                cp.wait_send()

    results = pl.pallas_call(
        hosted, name=name, grid=grid, in_specs=in_specs + [ANY] * n_xin, out_specs=out_specs + [ANY] * n_xout,
        out_shape=out_shape + x_out, scratch_shapes=scratch_shapes + exchange.semaphores(),
        input_output_aliases=exchange.aliases(n_in, n_out), compiler_params=_params(n_axes),
    )(*args, *exchange.inputs)
    return list(results[:n_out]), list(results[n_out:])


def _run_exchange(exchange, name):
    x_out = exchange.out_shapes()
    n_xin, n_xout = len(exchange.inputs), len(x_out)

    def body(*refs):
        xin, xout = refs[:n_xin], refs[n_xin:n_xin + n_xout]
        send_sems, recv_sems = refs[n_xin + n_xout:]
        sends, arrivals = exchange.copies(xin, xout, send_sems, recv_sems)
        for cp in sends:
            cp.start()
        for cp in arrivals:
            cp.wait_recv()
        for cp in sends:
            cp.wait_send()

    return list(pl.pallas_call(
        body, name=name, in_specs=[ANY] * n_xin, out_specs=[ANY] * n_xout, out_shape=x_out,
        scratch_shapes=exchange.semaphores(), input_output_aliases=exchange.aliases(0, 0),
    )(*exchange.inputs))


def _dot(a, b):
    return jnp.dot(a, b, preferred_element_type=F32)


def _dot_nt(a, b):
    return lax.dot_general(a, b, (((1,), (1,)), ((), ())), preferred_element_type=F32)


def _dot_tn(a, b):
    return lax.dot_general(a, b, (((0,), (0,)), ((), ())), preferred_element_type=F32)


def _gelu(x):
    x2 = x * x
    t = jnp.tanh(x * (x2 * (_GELU_K0 * _GELU_K1) + _GELU_K0))
    cdf = 0.5 * t + 0.5
    dg = cdf + (x * (1.0 - t * t)) * (x2 * (1.5 * _GELU_K0 * _GELU_K1) + 0.5 * _GELU_K0)
    return x * cdf, dg


def _colsum(x):
    return jnp.sum(x, axis=0, keepdims=True)


def _row(d):
    return pl.BlockSpec((1, d), lambda *_: (0, 0))


def _full(shape):
    n = len(shape)
    return pl.BlockSpec(shape, lambda *_: (0,) * n)


def _layer_block(shape, layer):
    n = len(shape)
    return pl.BlockSpec((None,) + tuple(shape), lambda *_: (layer,) + (0,) * n)


def _pool_counts(first_row, rows, window):
    t = first_row + lax.broadcasted_iota(jnp.int32, (rows, 1), 0)
    return 1.0 / jnp.minimum(t + 1, window).astype(F32)


def _pool_forward(a_ext, a, first_row, rows):
    out = []
    for g, window in enumerate(POOL_WINDOWS):
        s = a_ext[:, g * GROUP:(g + 1) * GROUP]
        k = 1
        while k < window:
            s = s + pltpu.roll(s, k, 0)
            k *= 2
        inv = _pool_counts(first_row, rows, window)
        out.append(s[POOL_HALO:] * inv - a[:, g * GROUP:(g + 1) * GROUP])
    return out


def _sgu_norm(gv):
    mu = jnp.mean(gv, axis=-1, keepdims=True)
    xc = gv - mu
    var = jnp.mean(xc * xc, axis=-1, keepdims=True)
    rs = lax.rsqrt(var + LN_EPS)
    return xc * rs, rs


def _matmul(a, w_all, layer, tn, out_dtype, name, exchange=None):
    S, K = a.shape
    N = w_all.shape[2]
    ts = min(TS_MM, S)

    def body(a_ref, w_ref, o_ref):
        o_ref[...] = _dot(a_ref[...], w_ref[...]).astype(out_dtype)

    (out,), extra = _call(
        body, name=name, grid=(N // tn, S // ts),
        in_specs=[pl.BlockSpec((ts, K), lambda j, i: (i, 0)), pl.BlockSpec((None, K, tn), lambda j, i: (layer, 0, j))],
        out_specs=[pl.BlockSpec((ts, tn), lambda j, i: (i, j))],
        out_shape=[jax.ShapeDtypeStruct((S, N), out_dtype)], args=(a, w_all), exchange=exchange)
    return out, extra


def _modulated_matmul(x, shift, scale, w_all, layer, out_dtype, name, exchange=None):
    S, K = x.shape
    N = w_all.shape[2]
    ts = min(TS_MM, S)

    def body(x_ref, sh_ref, sc_ref, w_ref, h_ref, o_ref):
        h = (x_ref[...] * (1.0 + sc_ref[...]) + sh_ref[...]).astype(MXU_DTYPE)
        h_ref[...] = h
        o_ref[...] = _dot(h, w_ref[...]).astype(out_dtype)

    tile = lambda w: pl.BlockSpec((ts, w), lambda i: (i, 0))
    return _call(
        body, name=name, grid=(S // ts,), in_specs=[tile(K), _row(K), _row(K), _layer_block((K, N), layer)],
        out_specs=[tile(K), tile(N)],
        out_shape=[jax.ShapeDtypeStruct((S, K), MXU_DTYPE), jax.ShapeDtypeStruct((S, N), out_dtype)],
        args=(x, shift, scale, w_all), exchange=exchange)


def _matmul_nt(a, w_all, layer, tk, name, exchange=None):
    S, K = a.shape
    N = w_all.shape[1]
    ts = min(TS_MM, S)
    n_k = K // tk

    def body(a_ref, w_ref, o_ref, acc):
        k = pl.program_id(1)
        part = _dot_nt(a_ref[...], w_ref[...])
        if n_k == 1:
            o_ref[...] = part
        else:
            @pl.when(k == 0)
            def _():
                acc[...] = part

            @pl.when(jnp.logical_and(k > 0, k < n_k - 1))
            def _():
                acc[...] += part

            @pl.when(k == n_k - 1)
            def _():
                o_ref[...] = acc[...] + part

    (out,), extra = _call(
        body, name=name, grid=(S // ts, n_k),
        in_specs=[pl.BlockSpec((ts, tk), lambda i, k: (i, k)), pl.BlockSpec((None, N, tk), lambda i, k: (layer, 0, k))],
        out_specs=[pl.BlockSpec((ts, N), lambda i, k: (i, 0))],
        out_shape=[jax.ShapeDtypeStruct((S, N), F32)],
        scratch_shapes=[pltpu.VMEM((ts, N), F32)], args=(a, w_all), exchange=exchange)
    return out, extra


def _f2_mixers(proj, xn, lg, lb, gate, pool_w, pool_scale, sg, sb, wm, bias_full, w_out_all, lg1, lb1, shift2, scale2,
               layer, exchange=None):
    S = xn.shape[0]
    ts = min(TS_MIX, S)
    n_chunks = ts // CHUNK
    halo_blocks = ts // POOL_HALO

    def body(p_ref, halo_ref, xn_ref, lg_ref, lb_ref, gate_ref, pw_ref, ps_ref, sg_ref, sb_ref, wm_ref,
             bias_ref, wo_ref, lg1_ref, lb1_ref, sh2_ref, sc2_ref, mix_ref, f_ref, xh_ref, rs_ref, h2_ref, gel_ref,
             z_scr):
        i = pl.program_id(0)
        a = p_ref[:, 0:D_POOL].astype(F32)
        u = p_ref[:, D_POOL:D_POOL + D_SGU].astype(GELU_DTYPE)
        v = p_ref[:, D_POOL + D_SGU:D_IN].astype(GELU_DTYPE)
        halo = halo_ref[...].astype(F32) * (i > 0).astype(F32)
        a_ext = jnp.concatenate([halo, a], axis=0)
        pooled = _pool_forward(a_ext, a, i * ts, ts)
        for g in range(N_GROUPS):
            mixed = _dot(pooled[g].astype(MXU_DTYPE), pw_ref[g])
            mix_ref[:, g * GROUP:(g + 1) * GROUP] = (mixed * ps_ref[:, g * GROUP:(g + 1) * GROUP]).astype(MXU_DTYPE)
        gu, dgu = _gelu(u)
        gv, dgv = _gelu(v)
        for k, part in enumerate((gu, dgu, gv, dgv)):
            gel_ref[:, k * D_SGU:(k + 1) * D_SGU] = part.astype(STORE_DTYPE)
        gu = gu.astype(F32)
        vhat, _ = _sgu_norm(gv.astype(F32))
        vn = (vhat * sg_ref[...] + sb_ref[...]).astype(MXU_DTYPE)
        for h in range(HEADS):
            cols = slice(h * GROUP, (h + 1) * GROUP)
            vn_h = jnp.concatenate([vn[c * CHUNK:(c + 1) * CHUNK, cols] for c in range(n_chunks)], axis=1)
            z_h = _dot(wm_ref[h], vn_h)
            for c in range(n_chunks):
                z_scr[c * CHUNK:(c + 1) * CHUNK, cols] = z_h[:, c * GROUP:(c + 1) * GROUP] + bias_ref[:, cols]
        mix_ref[:, D_POOL:D] = (gu * z_scr[...]).astype(MXU_DTYPE)
        f = _dot(mix_ref[...], wo_ref[...])
        f_ref[...] = f.astype(STORE_DTYPE)
        x = xn_ref[...] * lg_ref[...] + lb_ref[...]
        z1 = ALPHA * x + gate_ref[...] * f
        mu = jnp.mean(z1, axis=-1, keepdims=True)
        zc = z1 - mu
        var = jnp.mean(zc * zc, axis=-1, keepdims=True)
        rs = lax.rsqrt(var + LN_EPS)
        xhat = zc * rs
        xh_ref[...] = xhat
        rs_ref[...] = rs
        x1 = xhat * lg1_ref[...] + lb1_ref[...]
        h2_ref[...] = (x1 * (1.0 + sc2_ref[...]) + sh2_ref[...]).astype(MXU_DTYPE)

    tile = lambda w: pl.BlockSpec((ts, w), lambda i: (i, 0))
    return _call(
        body, name=f"f2_mixers_l{layer}", grid=(S // ts,),
        in_specs=[tile(D_IN),
                  pl.BlockSpec((POOL_HALO, D_POOL), lambda i: (jnp.maximum(i * halo_blocks - 1, 0), 0)),
                  tile(D), _row(D), _row(D), _row(D),
                  _full((N_GROUPS, GROUP, GROUP)), _row(D_POOL), _row(D_SGU), _row(D_SGU),
                  _full((HEADS, CHUNK, CHUNK)), _full((CHUNK, D_SGU)), _layer_block((D, D), layer),
                  _row(D), _row(D), _row(D), _row(D)],
        out_specs=[tile(D), tile(D), tile(D), tile(1), tile(D), tile(4 * D_SGU)],
        out_shape=[jax.ShapeDtypeStruct((S, D), MXU_DTYPE), jax.ShapeDtypeStruct((S, D), STORE_DTYPE),
                   jax.ShapeDtypeStruct((S, D), F32), jax.ShapeDtypeStruct((S, 1), F32),
                   jax.ShapeDtypeStruct((S, D), MXU_DTYPE), jax.ShapeDtypeStruct((S, 4 * D_SGU), STORE_DTYPE)],
        scratch_shapes=[pltpu.VMEM((ts, D_SGU), F32)],
        args=(proj, proj, xn, lg, lb, gate, pool_w, pool_scale, sg, sb, wm, bias_full, w_out_all, lg1, lb1, shift2,
              scale2), exchange=exchange)


def _f4_ffn(up, xhat1, lg, lb, gate, conv_w, conv_b, w_down_all, lg2, lb2, shift_next, scale_next, layer,
            exchange=None):
    S = xhat1.shape[0]
    ts = min(TS_FF, S)
    halo_blocks = ts // CONV_HALO

    def body(g_ref, halo_ref, val_ref, xh_ref, lg_ref, lb_ref, gate_ref, cw_ref, cb_ref, wd_ref, lg2_ref, lb2_ref,
             shn_ref, scn_ref, act_ref, f_ref, xo_ref, rs_ref, hn_ref, ge_ref, dge_ref):
        i = pl.program_id(0)
        halo = halo_ref[...].astype(GELU_DTYPE) * (i > 0).astype(GELU_DTYPE)
        g_ext = jnp.concatenate([halo, g_ref[...].astype(GELU_DTYPE)], axis=0)
        cw = cw_ref[...].astype(GELU_DTYPE)
        gm2 = pltpu.roll(g_ext, 2, 0)[CONV_HALO:]
        gm1 = pltpu.roll(g_ext, 1, 0)[CONV_HALO:]
        gc = ((cb_ref[...].astype(GELU_DTYPE) + gm2 * cw[0:1, :]) + gm1 * cw[1:2, :]) + g_ext[CONV_HALO:] * cw[2:3, :]
        ge, dge = _gelu(gc)
        ge_ref[...] = ge.astype(STORE_DTYPE)
        dge_ref[...] = dge.astype(STORE_DTYPE)
        act = (ge * val_ref[...].astype(GELU_DTYPE)).astype(MXU_DTYPE)
        act_ref[...] = act
        f = _dot(act, wd_ref[...])
        f_ref[...] = f.astype(STORE_DTYPE)
        x = xh_ref[...] * lg_ref[...] + lb_ref[...]
        z = ALPHA * x + gate_ref[...] * f
        mu = jnp.mean(z, axis=-1, keepdims=True)
        zc = z - mu
        var = jnp.mean(zc * zc, axis=-1, keepdims=True)
        rs = lax.rsqrt(var + LN_EPS)
        xhat = zc * rs
        xo_ref[...] = xhat
        rs_ref[...] = rs
        x2 = xhat * lg2_ref[...] + lb2_ref[...]
        hn_ref[...] = (x2 * (1.0 + scn_ref[...]) + shn_ref[...]).astype(MXU_DTYPE)

    tile = lambda w: pl.BlockSpec((ts, w), lambda i: (i, 0))
    return _call(
        body, name=f"f4_ffn_l{layer}", grid=(S // ts,),
        in_specs=[pl.BlockSpec((ts, D_FF), lambda i: (i, 0)),
                  pl.BlockSpec((CONV_HALO, D_FF), lambda i: (jnp.maximum(i * halo_blocks - 1, 0), 0)),
                  pl.BlockSpec((ts, D_FF), lambda i: (i, 1)),
                  tile(D), _row(D), _row(D), _row(D), _full((3, D_FF)), _row(D_FF),
                  _layer_block((D_FF, D), layer), _row(D), _row(D), _row(D), _row(D)],
        out_specs=[tile(D_FF), tile(D), tile(D), tile(1), tile(D), tile(D_FF), tile(D_FF)],
        out_shape=[jax.ShapeDtypeStruct((S, D_FF), MXU_DTYPE), jax.ShapeDtypeStruct((S, D), STORE_DTYPE),
                   jax.ShapeDtypeStruct((S, D), F32), jax.ShapeDtypeStruct((S, 1), F32),
                   jax.ShapeDtypeStruct((S, D), MXU_DTYPE), jax.ShapeDtypeStruct((S, D_FF), STORE_DTYPE),
                   jax.ShapeDtypeStruct((S, D_FF), STORE_DTYPE)],
        args=(up, up, up, xhat1, lg, lb, gate, conv_w, conv_b, w_down_all, lg2, lb2, shift_next, scale_next),
        exchange=exchange)


def _accumulate(i, ref, value):
    @pl.when(i == 0)
    def _():
        ref[...] = value

    @pl.when(i > 0)
    def _():
        ref[...] += value


def _b_ln(d, xhat, rstd, f, lg, lb, gate, name, modulated=None, loss_head=False, exchange=None):
    S = d.shape[0]
    ts = min(TS_PROJ, S)
    has_mod = modulated is not None
    assert not (has_mod and loss_head)

    def body(*refs):
        if has_mod:
            (d_ref, xh_ref, rs_ref, f_ref, lg_ref, lb_ref, gate_ref, dri_ref, sc_ref,
             df_ref, dres_ref, dlg_ref, dlb_ref, dgate_ref, dsc_ref, dsh_ref) = refs
        elif loss_head:
            (d_ref, xh_ref, rs_ref, f_ref, lg_ref, lb_ref, gate_ref,
             df_ref, dres_ref, dlg_ref, dlb_ref, dgate_ref, sq_ref) = refs
        else:
            (d_ref, xh_ref, rs_ref, f_ref, lg_ref, lb_ref, gate_ref,
             df_ref, dres_ref, dlg_ref, dlb_ref, dgate_ref) = refs
        i = pl.program_id(0)
        xh = xh_ref[...]
        if has_mod:
            dh = d_ref[...]
            dxo_t = dh * (1.0 + sc_ref[...]) + dri_ref[...]
            _accumulate(i, dsc_ref, _colsum(dh * (xh * lg_ref[...] + lb_ref[...])))
            _accumulate(i, dsh_ref, _colsum(dh))
        elif loss_head:
            err = (xh * lg_ref[...] + lb_ref[...]) - d_ref[...]
            dxo_t = err * (1.0 / D)
            _accumulate(i, sq_ref, _colsum(err * err))
        else:
            dxo_t = d_ref[...]
        dxh = dxo_t * lg_ref[...]
        m1 = jnp.mean(dxh, axis=-1, keepdims=True)
        m2 = jnp.mean(dxh * xh, axis=-1, keepdims=True)
        dz = rs_ref[...] * (dxh - m1 - xh * m2)
        df_ref[...] = (dz * gate_ref[...]).astype(MXU_DTYPE)
        dres_ref[...] = ALPHA * dz
        _accumulate(i, dlg_ref, _colsum(dxo_t * xh))
        _accumulate(i, dlb_ref, _colsum(dxo_t))
        _accumulate(i, dgate_ref, _colsum(dz * f_ref[...].astype(F32)))

    tile = lambda w: pl.BlockSpec((ts, w), lambda i: (i, 0))
    n_sums = 5 if has_mod else 4 if loss_head else 3
    in_specs = [tile(D), tile(D), tile(1), tile(D), _row(D), _row(D), _row(D)]
    args = [d, xhat, rstd, f, lg, lb, gate]
    if has_mod:
        in_specs += [tile(D), _row(D)]
        args += list(modulated)
    return _call(
        body, name=name, grid=(S // ts,), in_specs=in_specs,
        out_specs=[tile(D), tile(D)] + [_row(D)] * n_sums,
        out_shape=[jax.ShapeDtypeStruct((S, D), MXU_DTYPE), jax.ShapeDtypeStruct((S, D), F32)]
        + [jax.ShapeDtypeStruct((1, D), F32)] * n_sums,
        args=args, exchange=exchange)


def _input_grad(dh, dres, scale, x, exchange=None):
    S = dh.shape[0]
    ts = min(TS_PROJ, S)

    def body(dh_ref, dres_ref, sc_ref, x_ref, dx_ref, dsc_ref, dsh_ref):
        i = pl.program_id(0)
        dh_t = dh_ref[...]
        dx_ref[...] = dh_t * (1.0 + sc_ref[...]) + dres_ref[...]
        _accumulate(i, dsc_ref, _colsum(dh_t * x_ref[...]))
        _accumulate(i, dsh_ref, _colsum(dh_t))

    tile = pl.BlockSpec((ts, D), lambda i: (i, 0))
    return _call(
        body, name="input_grad", grid=(S // ts,), in_specs=[tile, tile, _row(D), tile],
        out_specs=[tile, _row(D), _row(D)],
        out_shape=[jax.ShapeDtypeStruct((S, D), F32)] + [jax.ShapeDtypeStruct((1, D), F32)] * 2,
        args=(dh, dres, scale, x), exchange=exchange)


def _b2_ffn(df2, w_down_all, up, ge, dge, conv_w, layer, exchange=None):
    S = df2.shape[0]
    ts = min(TS_FF, S)
    n_tiles = S // ts
    n_ext = ts + CONV_HALO

    def body(df_ref, wd_ref, g_ref, val_ref, ge_ref, dge_ref, cw_ref, dup_ref, dcw_ref, dcb_ref, next_dgc):
        i = pl.program_id(0)

        @pl.when(i == 0)
        def _():
            next_dgc[...] = jnp.zeros_like(next_dgc)

        dact = _dot_nt(df_ref[...], wd_ref[...]).astype(GELU_DTYPE)
        dup_ref[:, D_FF:2 * D_FF] = (dact * ge_ref[...].astype(GELU_DTYPE)).astype(MXU_DTYPE)
        dgc = (dact * val_ref[...].astype(GELU_DTYPE) * dge_ref[...].astype(GELU_DTYPE)).astype(F32)
        ext = jnp.concatenate([dgc, next_dgc[...]], axis=0)
        dp1 = pltpu.roll(ext, n_ext - 1, 0)[:ts]
        dp2 = pltpu.roll(ext, n_ext - 2, 0)[:ts]
        dup_ref[:, 0:D_FF] = (dgc * cw_ref[2:3, :] + dp1 * cw_ref[1:2, :] + dp2 * cw_ref[0:1, :]).astype(MXU_DTYPE)
        next_dgc[...] = dgc[0:CONV_HALO]
        g = g_ref[...].astype(F32)
        dcw = jnp.concatenate([_colsum(dp2 * g), _colsum(dp1 * g), _colsum(dgc * g)], axis=0)
        _accumulate(i, dcw_ref, dcw)
        _accumulate(i, dcb_ref, _colsum(dgc))

    tile = lambda w, col=0: pl.BlockSpec((ts, w), lambda i: (n_tiles - 1 - i, col))
    return _call(
        body, name=f"b2_ffn_l{layer}", grid=(n_tiles,),
        in_specs=[tile(D), _layer_block((D_FF, D), layer), tile(D_FF), tile(D_FF, 1), tile(D_FF), tile(D_FF),
                  _full((3, D_FF))],
        out_specs=[tile(2 * D_FF), _full((3, D_FF)), _row(D_FF)],
        out_shape=[jax.ShapeDtypeStruct((S, 2 * D_FF), MXU_DTYPE),
                   jax.ShapeDtypeStruct((3, D_FF), F32), jax.ShapeDtypeStruct((1, D_FF), F32)],
        scratch_shapes=[pltpu.VMEM((CONV_HALO, D_FF), F32)],
        args=(df2, w_down_all, up, up, ge, dge, conv_w), exchange=exchange)


def _b5_mixers(df1, w_out_all, proj, gel, pool_w, pool_scale, sg, sb, wm, bias_full, layer):
    S = df1.shape[0]
    ts = min(TS_MIX, S)
    n_chunks = ts // CHUNK
    halo_blocks = ts // POOL_HALO
    last_halo = S // POOL_HALO - 1

    def body(df_ref, dfh_ref, wo_ref, p_ref, ah_ref, gel_ref, pw_ref, ps_ref, sg_ref, sb_ref, wm_ref, bias_ref,
             dp_ref, dpw_ref, dps_ref, dsg_ref, dsb_ref, dwm_ref, dbias_ref, z_scr, dvn_scr):
        i = pl.program_id(0)
        last = pl.num_programs(0) - 1
        dmix = _dot_nt(df_ref[...], wo_ref[...])
        dmix_halo = _dot_nt(dfh_ref[...], wo_ref[0:D_POOL, :]) * (i < last).astype(F32)

        a = p_ref[:, 0:D_POOL].astype(F32)
        halo = ah_ref[...].astype(F32) * (i > 0).astype(F32)
        pooled = _pool_forward(jnp.concatenate([halo, a], axis=0), a, i * ts, ts)
        n = ts + POOL_HALO
        dps_parts = []
        for g, window in enumerate(POOL_WINDOWS):
            cols = slice(g * GROUP, (g + 1) * GROUP)
            pooled_b = pooled[g].astype(MXU_DTYPE)
            mixed = _dot(pooled_b, pw_ref[g])
            dya = dmix[:, cols]
            dps_parts.append(_colsum(dya * mixed))
            dmixed = (dya * ps_ref[:, cols]).astype(MXU_DTYPE)
            dpw_g = _dot_tn(pooled_b, dmixed)

            @pl.when(i == 0)
            def _():
                dpw_ref[g] = dpw_g

            @pl.when(i > 0)
            def _():
                dpw_ref[g] += dpw_g

            dpooled = _dot_nt(dmixed, pw_ref[g])
            dmixed_h = (dmix_halo[:, cols] * ps_ref[:, cols]).astype(MXU_DTYPE)
            dpooled_h = _dot_nt(dmixed_h, pw_ref[g])
            q = dpooled * _pool_counts(i * ts, ts, window)
            s = jnp.concatenate([q, dpooled_h * (1.0 / window)], axis=0)
            k = 1
            while k < window:
                s = s + pltpu.roll(s, n - k, 0)
                k *= 2
            dp_ref[:, cols] = (s[:ts] - dpooled).astype(MXU_DTYPE)
        _accumulate(i, dps_ref, jnp.concatenate(dps_parts, axis=1))

        saved = lambda k: gel_ref[:, k * D_SGU:(k + 1) * D_SGU].astype(F32)
        vhat, rs = _sgu_norm(saved(2))
        vn = (vhat * sg_ref[...] + sb_ref[...]).astype(MXU_DTYPE)
        dyb = dmix[:, D_POOL:D]
        dz = dyb * saved(0)
        dzb = dz.astype(MXU_DTYPE)
        dbias = dz[0:CHUNK]
        for c in range(1, n_chunks):
            dbias = dbias + dz[c * CHUNK:(c + 1) * CHUNK]
        _accumulate(i, dbias_ref, dbias)
        for h in range(HEADS):
            cols = slice(h * GROUP, (h + 1) * GROUP)
            vn_h = jnp.concatenate([vn[c * CHUNK:(c + 1) * CHUNK, cols] for c in range(n_chunks)], axis=1)
            dz_h = jnp.concatenate([dzb[c * CHUNK:(c + 1) * CHUNK, cols] for c in range(n_chunks)], axis=1)
            z_h = _dot(wm_ref[h], vn_h)
            dvn_h = _dot_tn(wm_ref[h], dz_h)
            dwm_h = _dot_nt(dz_h, vn_h)
            for c in range(n_chunks):
                rows = slice(c * CHUNK, (c + 1) * CHUNK)
                z_scr[rows, cols] = z_h[:, c * GROUP:(c + 1) * GROUP] + bias_ref[:, cols]
                dvn_scr[rows, cols] = dvn_h[:, c * GROUP:(c + 1) * GROUP]

            @pl.when(i == 0)
            def _():
                dwm_ref[h] = dwm_h

            @pl.when(i > 0)
            def _():
                dwm_ref[h] += dwm_h

        dp_ref[:, D_POOL:D_POOL + D_SGU] = (dyb * z_scr[...] * saved(1)).astype(MXU_DTYPE)
        dvn = dvn_scr[...]
        _accumulate(i, dsg_ref, _colsum(dvn * vhat))
        _accumulate(i, dsb_ref, _colsum(dvn))
        dvh = dvn * sg_ref[...]
        m1 = jnp.mean(dvh, axis=-1, keepdims=True)
        m2 = jnp.mean(dvh * vhat, axis=-1, keepdims=True)
        dp_ref[:, D_POOL + D_SGU:D_IN] = (rs * (dvh - m1 - vhat * m2) * saved(3)).astype(MXU_DTYPE)

        @pl.when(i == last)
        def _():
            tri = (lax.broadcasted_iota(jnp.int32, (CHUNK, CHUNK), 0)
                   >= lax.broadcasted_iota(jnp.int32, (CHUNK, CHUNK), 1))
            for h in range(HEADS):
                dwm_ref[h] = jnp.where(tri, dwm_ref[h], 0.0)

    tile = lambda w: pl.BlockSpec((ts, w), lambda i: (i, 0))
    return pl.pallas_call(
        body, name=f"b5_mixers_l{layer}", grid=(S // ts,),
        in_specs=[tile(D),
                  pl.BlockSpec((POOL_HALO, D), lambda i: (jnp.minimum((i + 1) * halo_blocks, last_halo), 0)),
                  _layer_block((D, D), layer), tile(D_POOL),
                  pl.BlockSpec((POOL_HALO, D_POOL), lambda i: (jnp.maximum(i * halo_blocks - 1, 0), 0)),
                  tile(4 * D_SGU),
                  _full((N_GROUPS, GROUP, GROUP)), _row(D_POOL), _row(D_SGU), _row(D_SGU),
                  _full((HEADS, CHUNK, CHUNK)), _full((CHUNK, D_SGU))],
        out_specs=[tile(D_IN), _full((N_GROUPS, GROUP, GROUP)), _row(D_POOL), _row(D_SGU), _row(D_SGU),
                   _full((HEADS, CHUNK, CHUNK)), _full((CHUNK, D_SGU))],
        out_shape=[jax.ShapeDtypeStruct((S, D_IN), MXU_DTYPE), jax.ShapeDtypeStruct((N_GROUPS, GROUP, GROUP), F32),
                   jax.ShapeDtypeStruct((1, D_POOL), F32), jax.ShapeDtypeStruct((1, D_SGU), F32),
                   jax.ShapeDtypeStruct((1, D_SGU), F32), jax.ShapeDtypeStruct((HEADS, CHUNK, CHUNK), F32),
                   jax.ShapeDtypeStruct((CHUNK, D_SGU), F32)],
        scratch_shapes=[pltpu.VMEM((ts, D_SGU), F32), pltpu.VMEM((ts, D_SGU), F32)],
        compiler_params=_params(1),
    )(df1, df1, w_out_all, proj, proj, gel, pool_w, pool_scale, sg, sb, wm, bias_full)


def _weight_grad(a, bs, tn, name, rows=TS_TN):
    S, M = a.shape
    counts = [b.shape[1] // tn for b in bs]
    starts = [sum(counts[:k]) for k in range(len(bs))]
    ts = min(rows, S)

    def body(a_ref, *refs):
        b_refs, o_ref = refs[:-1], refs[-1]
        j = pl.program_id(0)
        for b_ref, lo, n in zip(b_refs, starts, counts):
            @pl.when(jnp.logical_and(j >= lo, j < lo + n))
            def _():
                _accumulate(pl.program_id(1), o_ref, _dot_tn(a_ref[...], b_ref[...]))

    b_spec = lambda lo, n: pl.BlockSpec((ts, tn), lambda j, i: (i, jnp.clip(j - lo, 0, n - 1)))
    return pl.pallas_call(
        body, name=name, grid=(sum(counts), S // ts),
        in_specs=[pl.BlockSpec((ts, M), lambda j, i: (i, 0))] + [b_spec(lo, n) for lo, n in zip(starts, counts)],
        out_specs=pl.BlockSpec((M, tn), lambda j, i: (0, j)),
        out_shape=jax.ShapeDtypeStruct((M, sum(counts) * tn), F32), compiler_params=_params(2),
    )(a, *bs)


def _silu(x):
    return x * (1.0 / (1.0 + jnp.exp(-x)))


def _ada_forward(c_all, ada_w, ada_b_cols):
    n_cols = ada_w.shape[2]
    tc = 512

    def body(c_ref, w_ref, b_ref, o_ref):
        ca = _silu(c_ref[...]).astype(MXU_DTYPE)
        o_ref[...] = _dot(ca, w_ref[...].astype(MXU_DTYPE)) + b_ref[...]

    return pl.pallas_call(
        body, name="ada_forward", grid=(DEPTH, n_cols // tc),
        in_specs=[pl.BlockSpec((16, D), lambda l, j: (0, 0)), pl.BlockSpec((None, D, tc), lambda l, j: (l, 0, j)),
                  pl.BlockSpec((None, 1, tc), lambda l, j: (l, 0, j))],
        out_specs=pl.BlockSpec((None, 16, tc), lambda l, j: (l, 0, j)),
        out_shape=jax.ShapeDtypeStruct((DEPTH, 16, n_cols), F32),
        compiler_params=_params(2),
    )(c_all, ada_w, ada_b_cols)


def _ada_backward(c_all, dmod_cols):
    n_cols = dmod_cols.shape[2]
    tc = 512

    def body(c_ref, d_ref, o_ref):
        ca = _silu(c_ref[...]).astype(MXU_DTYPE)
        o_ref[...] = _dot_tn(ca, d_ref[...].astype(MXU_DTYPE))

    return pl.pallas_call(
        body, name="ada_backward", grid=(DEPTH, n_cols // tc),
        in_specs=[pl.BlockSpec((16, D), lambda l, j: (0, 0)), pl.BlockSpec((None, 16, tc), lambda l, j: (l, 0, j))],
        out_specs=pl.BlockSpec((None, D, tc), lambda l, j: (l, 0, j)),
        out_shape=jax.ShapeDtypeStruct((DEPTH, D, n_cols), F32),
        compiler_params=_params(2),
    )(c_all, dmod_cols)


def _adamw(w, g, m, v, name):
    R, C = w.shape
    tr = R
    for cand in (512, 256, 128, 64, 32, 16, 8):
        if R % cand == 0 and cand * C * 4 <= 2 ** 21:
            tr = cand
            break
    c1 = 1.0 - ADAM_B1 ** ADAM_STEP
    c2 = 1.0 - ADAM_B2 ** ADAM_STEP

    def body(w_ref, g_ref, m_ref, v_ref, d_ref, mo_ref, vo_ref):
        gg = g_ref[...]
        mn = ADAM_B1 * m_ref[...] + (1.0 - ADAM_B1) * gg
        vn = ADAM_B2 * v_ref[...] + (1.0 - ADAM_B2) * (gg * gg)
        mo_ref[...] = mn
        vo_ref[...] = vn
        d_ref[...] = -ADAM_LR * ((mn / c1) / (jnp.sqrt(vn / c2) + ADAM_EPS) + ADAM_WD * w_ref[...])

    tile = pl.BlockSpec((tr, C), lambda i: (i, 0))
    return pl.pallas_call(
        body, name=name, grid=(R // tr,), in_specs=[tile] * 4, out_specs=[tile] * 3,
        out_shape=[jax.ShapeDtypeStruct((R, C), F32)] * 3, compiler_params=_params(1),
    )(w, g, m, v)


def _position():
    x, y, c = lax.axis_index("x"), lax.axis_index("y"), lax.axis_index("c")
    other_chips = [(1 - x, y), (x, 1 - y), (1 - x, 1 - y)]
    return x, y, c, other_chips


def _all_gather8(block, name):
    R, C = block.shape

    def body(x_ref, out_ref, send_sems, recv_sems, local_sem):
        x, y, c, chips = _position()
        me, sibling = (x, y, c), (x, y, 1 - c)

        def rows(px, py, pc):
            return out_ref.at[pl.ds((4 * px + 2 * py + pc) * R, R), :]

        def copy(k, blk, to, src=None):
            return pltpu.make_async_remote_copy(
                src_ref=rows(*blk) if src is None else src, dst_ref=rows(*blk),
                send_sem=send_sems.at[k], recv_sem=recv_sems.at[k], device_id=to, device_id_type=MESH)

        mine = pltpu.make_async_copy(x_ref, rows(*me), local_sem)
        mine.start()
        first = [copy(0, me, sibling, src=x_ref)]
        first += [copy(1 + j, me, (*chip, c), src=x_ref) for j, chip in enumerate(chips)]
        for cp in first:
            cp.start()
        passed = [copy(4 + j, (*chip, c), sibling) for j, chip in enumerate(chips)]
        for j, chip in enumerate(chips):
            copy(1 + j, (*chip, c), me).wait_recv()
            passed[j].start()
        copy(0, sibling, me).wait_recv()
        for j, chip in enumerate(chips):
            copy(4 + j, (*chip, 1 - c), me).wait_recv()
        for cp in first + passed:
            cp.wait_send()
        mine.wait()

    return pl.pallas_call(
        body, name=name, out_shape=jax.ShapeDtypeStruct((N_DEV * R, C), block.dtype),
        in_specs=[pl.BlockSpec(memory_space=pltpu.VMEM)], out_specs=ANY,
        scratch_shapes=[pltpu.SemaphoreType.DMA((7,)), pltpu.SemaphoreType.DMA((7,)), pltpu.SemaphoreType.DMA(())],
    )(block)


def _gather_flat(vec, name):
    n = vec.shape[0]
    padded = -(-n // 1024) * 1024
    block = jnp.pad(vec, (0, padded - n)).reshape(8, padded // 8)
    out = _all_gather8(block, name)
    return out.reshape(N_DEV, padded)[:, :n]


_SHARD_KINDS = ("cols", "rows", "cols", "rows")


def _shard_of(ref, shape, kind, chip):
    m, n = shape
    return ref.at[:, pl.ds(chip * (n // 4), n // 4)] if kind == "cols" else ref.at[pl.ds(chip * (m // 4), m // 4), :]


def _half_of(ref, shape, kind, h):
    m, n = shape
    return ref.at[pl.ds(h * (m // 2), m // 2), :] if kind == "cols" else ref.at[:, pl.ds(h * (n // 2), n // 2)]


def _half_shape(shape, kind):
    m, n = shape
    return (m // 2, n) if kind == "cols" else (m, n // 2)


def _shard_shape(shape, kind):
    m, n = shape
    return (m, n // 4) if kind == "cols" else (m // 4, n)


def _remote(src, dst, k, to, send_sems, recv_sems):
    return pltpu.make_async_remote_copy(src_ref=src, dst_ref=dst, send_sem=send_sems.at[k], recv_sem=recv_sems.at[k],
                                        device_id=to, device_id_type=MESH)


def _gather_exchange(shards, full, kind, layer):
    shape = full.shape[1:]

    def copies(xin, xout, send_sems, recv_sems):
        x, y, c, chips = _position()
        my_chip = 2 * x + y
        own = xin[0].at[layer]
        place = lambda chip: _shard_of(xout[0].at[layer], shape, kind, chip)
        peers = [(x, y, 1 - c)] + [(*chip, c) for chip in chips]
        sources = [my_chip] + [2 * cx + cy for cx, cy in chips]
        sends = [_remote(own, place(my_chip), k, peer, send_sems, recv_sems) for k, peer in enumerate(peers)]
        arrivals = [_remote(place(s), place(s), k, (x, y, c), send_sems, recv_sems) for k, s in enumerate(sources)]
        return sends, arrivals

    return _Exchange([shards, full], [1], [], 4, copies)


def _swap_exchange(grads):
    shapes = [g.shape for g in grads]
    fresh = [jax.ShapeDtypeStruct(_half_shape(s, k), F32) for s, k in zip(shapes, _SHARD_KINDS)]

    def copies(xin, xout, send_sems, recv_sems):
        x, y, c, _ = _position()
        sends = [_remote(_half_of(xin[a], shapes[a], _SHARD_KINDS[a], 1 - c), xout[a], a, (x, y, 1 - c),
                         send_sems, recv_sems) for a in range(len(grads))]
        arrivals = [_remote(xout[a], xout[a], a, (x, y, c), send_sems, recv_sems) for a in range(len(grads))]
        return sends, arrivals

    return _Exchange(list(grads), [], fresh, len(grads), copies)


def _scatter_exchange(partials):
    n_w = len(partials)
    shapes = [p.shape for p in partials]
    fresh = [jax.ShapeDtypeStruct((3,) + _shard_shape(s, k), p.dtype) for s, k, p in zip(shapes, _SHARD_KINDS, partials)]

    def copies(xin, xout, send_sems, recv_sems):
        x, y, c, chips = _position()
        sends, arrivals = [], []
        for j, (cx, cy) in enumerate(chips):
            for a in range(n_w):
                src = _shard_of(xin[a], shapes[a], _SHARD_KINDS[a], 2 * cx + cy)
                sends.append(_remote(src, xout[a].at[j], j * n_w + a, (cx, cy, c), send_sems, recv_sems))
                arrivals.append(_remote(xout[a].at[j], xout[a].at[j], j * n_w + a, (x, y, c), send_sems, recv_sems))
        return sends, arrivals

    return _Exchange(list(partials), [], fresh, 3 * n_w, copies)


def _share_exchange(reduced, layer):
    n_w = len(reduced)
    shapes = [r.shape[1:] for r in reduced]

    def copies(xin, xout, send_sems, recv_sems):
        x, y, c, _ = _position()
        half = lambda a, h: _half_of(xout[a].at[layer], shapes[a], _SHARD_KINDS[a], h)
        sends = [_remote(half(a, c), half(a, c), a, (x, y, 1 - c), send_sems, recv_sems) for a in range(n_w)]
        arrivals = [_remote(half(a, 1 - c), half(a, 1 - c), a, (x, y, c), send_sems, recv_sems) for a in range(n_w)]
        return sends, arrivals

    return _Exchange(list(reduced), list(range(n_w)), [], n_w, copies)


def _chip_partial(pos, grad, theirs, kind, name):
    M, N = grad.shape
    if kind == "cols":
        tm = 256
        steps = M // 2 // tm
        block, g_map = (tm, N), (lambda i, pos: (pos[0] * steps + i, 0))
    else:
        tm = M // 4
        steps = 4
        block, g_map = (tm, N // 2), (lambda i, pos: (i, pos[0]))

    def body(pos_ref, g_ref, t_ref, o_ref):
        o_ref[...] = (g_ref[...] + t_ref[...]).astype(WIRE_DTYPE)

    grid_spec = pltpu.PrefetchScalarGridSpec(
        num_scalar_prefetch=1, grid=(steps,),
        in_specs=[pl.BlockSpec(block, g_map), pl.BlockSpec(block, lambda i, pos: (i, 0))],
        out_specs=pl.BlockSpec(block, lambda i, pos: (i, 0)))
    return pl.pallas_call(
        body, name=name, grid_spec=grid_spec, out_shape=jax.ShapeDtypeStruct(_half_shape((M, N), kind), WIRE_DTYPE),
        compiler_params=_params(1),
    )(pos, grad, theirs)


def _reduce_shard(pos, grad, theirs, received, reduced, kind, layer, name):
    M, N = grad.shape
    if kind == "cols":
        tm = min(M // 2, 512)
        steps = M // 2 // tm
        block = (tm, N // 4)
        g_map = lambda i, pos: (pos[0] * steps + i, pos[1])
        t_map = lambda i, pos: (i, pos[1])
        o_map = lambda i, pos: (layer, pos[0] * steps + i, 0)
    else:
        steps = 1
        block = (M // 4, N // 2)
        g_map = lambda i, pos: (pos[1], pos[0])
        t_map = lambda i, pos: (pos[1], 0)
        o_map = lambda i, pos: (layer, 0, pos[0])
    out_shape = (DEPTH,) + _shard_shape((M, N), kind)

    def body(pos_ref, g_ref, t_ref, r0_ref, r1_ref, r2_ref, *rest):
        o_ref = rest[-1]
        chip = pos_ref[1]
        own = g_ref[...] + t_ref[...]
        r = [r0_ref[...].astype(F32), r1_ref[...].astype(F32), r2_ref[...].astype(F32)]
        total = None
        for s in range(N_CHIPS):
            rel = jnp.bitwise_xor(chip, s)
            term = jnp.where(rel == 0, own, jnp.where(rel == 2, r[0], jnp.where(rel == 1, r[1], r[2])))
            total = term if total is None else total + term
        o_ref[...] = total

    r_spec = lambda j: pl.BlockSpec((None,) + block, lambda i, pos: (j, i, 0))
    in_specs = [pl.BlockSpec(block, g_map), pl.BlockSpec(block, t_map), r_spec(0), r_spec(1), r_spec(2)]
    args = [pos, grad, theirs, received, received, received]
    aliases = {}
    if reduced is not None:
        in_specs.append(ANY)
        args.append(reduced)
        aliases = {6: 0}
    grid_spec = pltpu.PrefetchScalarGridSpec(
        num_scalar_prefetch=1, grid=(steps,), in_specs=in_specs, out_specs=pl.BlockSpec((None,) + block, o_map))
    return pl.pallas_call(
        body, name=name, grid_spec=grid_spec, out_shape=jax.ShapeDtypeStruct(out_shape, F32),
        input_output_aliases=aliases, compiler_params=_params(1),
    )(*args)


def _sum_devices(gathered):
    R8, C = gathered.shape
    R = R8 // N_DEV
    lanes = C // 128
    tc = 128 * max(k for k in range(1, lanes + 1) if lanes % k == 0 and k * 128 * R8 * 4 <= 2 ** 22)

    def body(g_ref, o_ref):
        total = g_ref[0:R, :]
        for d in range(1, N_DEV):
            total = total + g_ref[d * R:(d + 1) * R, :]
        o_ref[...] = total

    return pl.pallas_call(
        body, name="sum_devices", grid=(C // tc,),
        in_specs=[pl.BlockSpec((R8, tc), lambda j: (0, j))], out_specs=pl.BlockSpec((R, tc), lambda j: (0, j)),
        out_shape=jax.ShapeDtypeStruct((R, C), F32), compiler_params=_params(1),
    )(gathered)


_SMALL = ("ada_b", "pool_w", "pool_scale", "sgu_ln_g", "sgu_ln_b", "sgu_w", "sgu_b", "ln1_g", "ln1_b", "conv_w",
          "conv_b", "ln2_g", "ln2_b")
_WEIGHTS = ("ada_w", "ada_b", "w_in", "pool_w", "pool_scale", "sgu_ln_g", "sgu_ln_b", "sgu_w", "sgu_b", "w_out",
            "ln1_g", "ln1_b", "w_up", "conv_w", "conv_b", "w_down", "ln2_g", "ln2_b")


def kernel(x, c, ada_w, ada_b, w_in, pool_w, pool_scale, sgu_ln_g, sgu_ln_b, sgu_w, sgu_b, w_out, ln1_g, ln1_b, w_up, conv_w, conv_b, w_down, ln2_g, ln2_b, loss_target, m_ada_w, m_ada_b, m_w_in, m_pool_w, m_pool_scale, m_sgu_ln_g, m_sgu_ln_b, m_sgu_w, m_sgu_b, m_w_out, m_ln1_g, m_ln1_b, m_w_up, m_conv_w, m_conv_b, m_w_down, m_ln2_g, m_ln2_b, v_ada_w, v_ada_b, v_w_in, v_pool_w, v_pool_scale, v_sgu_ln_g, v_sgu_ln_b, v_sgu_w, v_sgu_b, v_w_out, v_ln1_g, v_ln1_b, v_w_up, v_conv_w, v_conv_b, v_w_down, v_ln2_g, v_ln2_b):
    weights = dict(ada_w=ada_w, ada_b=ada_b, w_in=w_in, pool_w=pool_w, pool_scale=pool_scale, sgu_ln_g=sgu_ln_g,
                   sgu_ln_b=sgu_ln_b, sgu_w=sgu_w, sgu_b=sgu_b, w_out=w_out, ln1_g=ln1_g, ln1_b=ln1_b, w_up=w_up,
                   conv_w=conv_w, conv_b=conv_b, w_down=w_down, ln2_g=ln2_g, ln2_b=ln2_b)
    mom_m = dict(ada_w=m_ada_w, ada_b=m_ada_b, w_in=m_w_in, pool_w=m_pool_w, pool_scale=m_pool_scale,
                 sgu_ln_g=m_sgu_ln_g, sgu_ln_b=m_sgu_ln_b, sgu_w=m_sgu_w, sgu_b=m_sgu_b, w_out=m_w_out,
                 ln1_g=m_ln1_g, ln1_b=m_ln1_b, w_up=m_w_up, conv_w=m_conv_w, conv_b=m_conv_b, w_down=m_w_down,
                 ln2_g=m_ln2_g, ln2_b=m_ln2_b)
    mom_v = dict(ada_w=v_ada_w, ada_b=v_ada_b, w_in=v_w_in, pool_w=v_pool_w, pool_scale=v_pool_scale,
                 sgu_ln_g=v_sgu_ln_g, sgu_ln_b=v_sgu_ln_b, sgu_w=v_sgu_w, sgu_b=v_sgu_b, w_out=v_w_out,
                 ln1_g=v_ln1_g, ln1_b=v_ln1_b, w_up=v_w_up, conv_w=v_conv_w, conv_b=v_conv_b, w_down=v_w_down,
                 ln2_g=v_ln2_g, ln2_b=v_ln2_b)

    ix, iy, ic = lax.axis_index("x"), lax.axis_index("y"), lax.axis_index("c")
    chip = 2 * ix + iy
    dev = 4 * ix + 2 * iy + ic
    pos = jnp.stack([ic, chip]).astype(jnp.int32)
    xs = x.reshape(x.shape[1:])
    target = loss_target.reshape(loss_target.shape[1:])
    ff_shard = conv_w.shape[2]
    mod_shard = ada_w.shape[2]

    first = _gather_flat(jnp.concatenate([c.reshape(-1), conv_w.reshape(-1)]), "gather_c_conv")
    c_all = jnp.pad(first[:, :D], ((0, 8), (0, 0)))
    conv_parts = first[0::2, D:].reshape(N_CHIPS, DEPTH, 3, ff_shard)
    conv_full = jnp.transpose(conv_parts, (1, 2, 0, 3)).reshape(DEPTH, 3, D_FF)
    ada_b_cols = lax.dynamic_slice_in_dim(ada_b, chip * mod_shard, mod_shard, axis=1).reshape(DEPTH, 1, mod_shard)
    mod_part = _ada_forward(c_all, ada_w, ada_b_cols)[:, :8, :]
    mod_all = _gather_flat(mod_part.reshape(-1), "gather_mod").reshape(N_DEV, DEPTH, 8, mod_shard)
    mod_mine = lax.dynamic_index_in_dim(mod_all[0::2], dev, axis=2, keepdims=False)
    mod = jnp.transpose(mod_mine, (1, 0, 2)).reshape(DEPTH, 6, 1, D)

    sh_in, sh_out, sh_up, sh_down = (w.astype(WIRE_DTYPE) for w in (w_in, w_out, w_up, w_down))
    wf_in = lax.empty((DEPTH, D, D_IN), WIRE_DTYPE)
    wf_out = lax.empty((DEPTH, D, D), WIRE_DTYPE)
    wf_up = lax.empty((DEPTH, D, 2 * D_FF), WIRE_DTYPE)
    wf_down = lax.empty((DEPTH, D_FF, D), WIRE_DTYPE)
    (wf_in,) = _run_exchange(_gather_exchange(sh_in, wf_in, "cols", 0), "gather_w_in_l0")

    tri = jnp.tril(jnp.ones((CHUNK, CHUNK), dtype=bool))
    ones_row = jnp.ones((1, D), F32)
    zeros_row = jnp.zeros((1, D), F32)
    row = lambda a, l: a[l].reshape(1, -1)

    saved = []
    xn, lg, lb = xs, ones_row, zeros_row
    h = None
    for l in range(DEPTH):
        shift1, scale1, gate1, shift2, scale2, gate2 = (mod[l, k] for k in range(6))
        nxt = min(l + 1, DEPTH - 1)
        pw = pool_w[l].astype(MXU_DTYPE)
        wm = jnp.where(tri[None], sgu_w[l], 0.0).astype(MXU_DTYPE)
        bias_full = jnp.repeat(jnp.transpose(sgu_b[l]), GROUP, axis=1)
        lg1, lb1 = row(ln1_g, l), row(ln1_b, l)
        if l == 0:
            (h, proj), (wf_out,) = _modulated_matmul(xs, shift1, scale1, wf_in, l, STORE_DTYPE, f"f1_in_proj_l{l}",
                                                     exchange=_gather_exchange(sh_out, wf_out, "rows", l))
        else:
            proj, (wf_out,) = _matmul(h, wf_in, l, D_IN, STORE_DTYPE, f"f1_in_proj_l{l}",
                                      exchange=_gather_exchange(sh_out, wf_out, "rows", l))
        (mix, f1, xhat1, rstd1, h2, gel), (wf_up,) = _f2_mixers(
            proj, xn, lg, lb, gate1, pw, row(pool_scale, l), row(sgu_ln_g, l), row(sgu_ln_b, l), wm, bias_full,
            wf_out, lg1, lb1, shift2, scale2, l, exchange=_gather_exchange(sh_up, wf_up, "cols", l))
        up, (wf_down,) = _matmul(h2, wf_up, l, D_FF, STORE_DTYPE, f"f3_up_proj_l{l}",
                                 exchange=_gather_exchange(sh_down, wf_down, "rows", l))
        (act, f2, xhat2, rstd2, h_next, ge, dge), filled = _f4_ffn(
            up, xhat1, lg1, lb1, gate2, conv_full[l], row(conv_b, l), wf_down, row(ln2_g, l), row(ln2_b, l),
            mod[nxt, 0], mod[nxt, 1], l,
            exchange=_gather_exchange(sh_in, wf_in, "cols", l + 1) if l + 1 < DEPTH else None)
        if filled:
            (wf_in,) = filled
        saved.append(dict(xn=xn, lg=lg, lb=lb, h=h, proj=proj, mix=mix, f1=f1, xhat1=xhat1, rstd1=rstd1, h2=h2,
                          up=up, act=act, f2=f2, xhat2=xhat2, rstd2=rstd2, pw=pw, wm=wm, bias_full=bias_full,
                          ge=ge, dge=dge, gel=gel))
        xn, lg, lb, h = xhat2, row(ln2_g, l), row(ln2_b, l), h_next

    d_out = target

    tags = ("in", "out", "up", "down")
    small = {n: [None] * DEPTH for n in _SMALL}
    dmod = [[None] * 6 for _ in range(DEPTH)]
    below = None
    ready = None
    reduced = [None] * 4

    def partial_sums(layer, full, theirs):
        return [_chip_partial(pos, g, t, kind, f"chip_partial_{tag}_l{layer}")
                for g, t, kind, tag in zip(full, theirs, _SHARD_KINDS, tags)]

    def chip_sums(layer, full, theirs, received):
        return [_reduce_shard(pos, g, t, r, out, kind, layer, f"reduce_shard_{tag}_l{layer}")
                for g, t, r, out, kind, tag in zip(full, theirs, received, reduced, _SHARD_KINDS, tags)]

    for l in reversed(range(DEPTH)):
        sv = saved[l]
        scale1, gate1, scale2, gate2 = mod[l, 1], mod[l, 2], mod[l, 4], mod[l, 5]
        lg1, lb1 = row(ln1_g, l), row(ln1_b, l)
        outs, theirs = _b_ln(d_out, sv["xhat2"], sv["rstd2"], sv["f2"], row(ln2_g, l), row(ln2_b, l), gate2,
                             f"b1_ln2_l{l}", modulated=below, loss_head=below is None,
                             exchange=_swap_exchange(ready[1]) if ready else None)
        df2, dres2, dlg2, dlb2, dmod[l][5] = outs[:5]
        if below is not None:
            dmod[l + 1][1], dmod[l + 1][0] = outs[5], outs[6]
        else:
            loss = lax.psum(jnp.sum(outs[5]) * (0.5 / D), ("x", "y", "c"))
        partials = partial_sums(ready[0], ready[1], theirs) if ready else None
        (dup, dcw, dcb), received = _b2_ffn(df2, wf_down, sv["up"], sv["ge"], sv["dge"], conv_full[l], l,
                                            exchange=_scatter_exchange(partials) if ready else None)
        if ready:
            reduced = chip_sums(ready[0], ready[1], theirs, received)
        g_down = _weight_grad(sv["act"], [df2], 512, f"wg_down_l{l}", rows=2 * TS_TN)
        dh2, shared = _matmul_nt(dup, wf_up, l, D_FF, f"b3_up_l{l}",
                                 exchange=_share_exchange(reduced, ready[0]) if ready else None)
        if ready:
            reduced = shared
        g_up = _weight_grad(sv["h2"], [dup], 1408, f"wg_up_l{l}", rows=2 * TS_TN)
        (df1, dres1, dlg1, dlb1, dmod[l][2], dmod[l][4], dmod[l][3]), _ = _b_ln(
            dh2, sv["xhat1"], sv["rstd1"], sv["f1"], lg1, lb1, gate1, f"b4_ln1_l{l}", modulated=(dres2, scale2))
        dproj, dpw, dps, dsg, dsb, dwm, dbias = _b5_mixers(
            df1, wf_out, sv["proj"], sv["gel"], sv["pw"], row(pool_scale, l), row(sgu_ln_g, l), row(sgu_ln_b, l), sv["wm"],
            sv["bias_full"], l)
        g_out = _weight_grad(sv["mix"], [df1], D, f"wg_out_l{l}", rows=2 * TS_TN)
        d_out, _ = _matmul_nt(dproj, wf_in, l, D_IN, f"b6_in_l{l}")
        g_in = _weight_grad(sv["h"], [dproj], 768, f"wg_in_l{l}", rows=2 * TS_TN)
        ready = (l, [g_in, g_out, g_up, g_down])
        below = (dres1, scale1)
        small["pool_w"][l], small["pool_scale"][l] = dpw, dps[0]
        small["sgu_ln_g"][l], small["sgu_ln_b"][l], small["sgu_w"][l] = dsg[0], dsb[0], dwm
        small["sgu_b"][l] = jnp.transpose(jnp.sum(dbias.reshape(CHUNK, HEADS, GROUP), axis=2))
        small["ln1_g"][l], small["ln1_b"][l] = dlg1[0], dlb1[0]
        small["conv_w"][l], small["conv_b"][l] = dcw, dcb[0]
        small["ln2_g"][l], small["ln2_b"][l] = dlg2[0], dlb2[0]
    (grad_x2d, dmod[0][1], dmod[0][0]), _ = _input_grad(d_out, below[0], below[1], xs)
    grad_x = grad_x2d.reshape(x.shape)
    small["ada_b"] = [jnp.concatenate([part[0] for part in dmod[l]]) for l in range(DEPTH)]

    names = _SMALL
    flat = jnp.concatenate([jnp.stack(small[n]).reshape(-1) for n in names])
    n_small = flat.shape[0]
    padded = -(-n_small // 1024) * 1024
    gathered = _all_gather8(jnp.pad(flat, (0, padded - n_small)).reshape(8, padded // 8), "gather_small_grads")
    summed = _sum_devices(gathered).reshape(-1)[:n_small]
    grads = {}
    offset = 0
    for n in names:
        size = math.prod(weights[n].shape[1:]) * DEPTH if n != "conv_w" else DEPTH * 3 * D_FF
        grads[n] = summed[offset:offset + size]
        offset += size
    grads["conv_w"] = lax.dynamic_slice_in_dim(grads["conv_w"].reshape(DEPTH, 3, D_FF), chip * ff_shard, ff_shard,
                                               axis=2)
    for n in names:
        grads[n] = grads[n].reshape(weights[n].shape)
    dmod_all = gathered[0::8, :DEPTH * 6 * D].reshape(N_DEV, DEPTH, 6 * D)
    dmod_cols = lax.dynamic_slice_in_dim(jnp.transpose(dmod_all, (1, 0, 2)), chip * mod_shard, mod_shard, axis=2)
    grads["ada_w"] = _ada_backward(c_all, jnp.pad(dmod_cols, ((0, 0), (0, 8), (0, 0))))

    theirs = _run_exchange(_swap_exchange(ready[1]), "swap_halves_l0")
    partials = partial_sums(0, ready[1], theirs)
    received = _run_exchange(_scatter_exchange(partials), "scatter_partials_l0")
    reduced = chip_sums(0, ready[1], theirs, received)
    grads["w_in"], grads["w_out"], grads["w_up"], grads["w_down"] = _run_exchange(
        _share_exchange(reduced, 0), "share_reduced_l0")

    delta, new_m, new_v = {}, {}, {}
    for n in ("ada_w", "w_in", "w_out", "w_up", "w_down"):
        shape = weights[n].shape
        two_d = (shape[0] * shape[1], shape[2])
        d_, m_, v_ = _adamw(weights[n].reshape(two_d), grads[n].reshape(two_d), mom_m[n].reshape(two_d),
                            mom_v[n].reshape(two_d), f"adamw_{n}")
        delta[n], new_m[n], new_v[n] = d_.reshape(shape), m_.reshape(shape), v_.reshape(shape)
    sizes = [math.prod(weights[n].shape) for n in names]
    total = sum(sizes)
    padded = -(-total // 1024) * 1024
    pack = lambda d: jnp.pad(jnp.concatenate([d[n].reshape(-1) for n in names]), (0, padded - total)).reshape(8, -1)
    v_pack = jnp.pad(jnp.concatenate([mom_v[n].reshape(-1) for n in names]), (0, padded - total),
                     constant_values=1.0).reshape(8, -1)
    d_, m_, v_ = _adamw(pack(weights), pack(grads), pack(mom_m), v_pack, "adamw_small")
    offset = 0
    for n, size in zip(names, sizes):
        for src, dst in ((d_, delta), (m_, new_m), (v_, new_v)):
            dst[n] = src.reshape(-1)[offset:offset + size].reshape(weights[n].shape)
        offset += size

    return (loss, grad_x, *[grads[n] for n in _WEIGHTS], *[delta[n] for n in _WEIGHTS],
            *[new_m[n] for n in _WEIGHTS], *[new_v[n] for n in _WEIGHTS])
```
